```python
import functools
import jax, jax.numpy as jnp
from jax import lax
import numpy as np

D_MODEL = 2048
BATCH = 8
SEQ = 2048
DEPTH = 4

D_FF = 5632
NORM_EPS = 1e-5
BLOCK = 128
NEG_INF = -1e30
MIXER_KINDS = ("gmlp", "swa", "fox")
N_MIXERS = 3
GMLP_WIDTH = D_MODEL
GMLP_CHUNK = 128
GMLP_GROUPS = 16
GMLP_GROUP_WIDTH = GMLP_WIDTH // GMLP_GROUPS
SWA_HEAD_DIM = 64
SWA_Q_HEADS = D_MODEL // SWA_HEAD_DIM
SWA_KV_HEADS = SWA_Q_HEADS // 8
SWA_WIDTH = SWA_Q_HEADS * SWA_HEAD_DIM
SWA_KV_WIDTH = SWA_KV_HEADS * SWA_HEAD_DIM
SWA_WINDOW = 128
ROPE_THETA = 500000.0
ROPE_DIM = SWA_HEAD_DIM // 4
FOX_HEAD_DIM = 128
FOX_HEADS = D_MODEL // FOX_HEAD_DIM
FOX_WIDTH = FOX_HEADS * FOX_HEAD_DIM

kernel_name = "hybrid_gmlp_swa_fox_macaron"


def rms_norm(x, g):
    x32 = x.astype(jnp.float32)
    y = x32 * lax.rsqrt(jnp.mean(x32 * x32, axis=-1, keepdims=True) + NORM_EPS)
    return (y * g.astype(jnp.float32)).astype(x.dtype)


def swiglu_ffn(h, wi, wo):
    gate, up = jnp.split(h @ wi, 2, axis=-1)
    return (jax.nn.silu(gate) * up) @ wo


def partial_rotary(x, pos):
    half = ROPE_DIM // 2
    inv_freq = ROPE_THETA ** (-(jnp.arange(half, dtype=jnp.float32) * 2.0 / ROPE_DIM))
    ang = pos.astype(jnp.float32)[:, None] * inv_freq[None, :]
    cos = jnp.cos(ang)[None, :, None, :]
    sin = jnp.sin(ang)[None, :, None, :]
    xr = x[..., :ROPE_DIM].astype(jnp.float32)
    x1, x2 = xr[..., :half], xr[..., half:]
    rot = jnp.concatenate([x1 * cos - x2 * sin, x2 * cos + x1 * sin], axis=-1).astype(x.dtype)
    return jnp.concatenate([rot, x[..., ROPE_DIM:]], axis=-1)


def gmlp_mixer(h, w_in, v_gain, w_s, b_s, w_out):
    B, S, _ = h.shape
    z = jax.nn.gelu(h @ w_in, approximate=False)
    u, v = jnp.split(z, 2, axis=-1)
    v = rms_norm(v, v_gain)
    nc = S // GMLP_CHUNK
    v = v.reshape(B, nc, GMLP_CHUNK, GMLP_GROUPS, GMLP_GROUP_WIDTH)
    causal = jnp.tril(jnp.ones((GMLP_CHUNK, GMLP_CHUNK), dtype=bool))
    ws = jnp.where(causal[None], w_s, jnp.zeros_like(w_s))
    mixed = jnp.einsum('gts,bcsgw->bctgw', ws, v) + b_s.T[:, :, None]
    y = u * mixed.reshape(B, S, GMLP_WIDTH)
    return y @ w_out


def swa_mixer(h, w_in, sinks, w_out):
    B, S, _ = h.shape
    nb = S // BLOCK
    G = SWA_Q_HEADS // SWA_KV_HEADS
    q, k, v = jnp.split(h @ w_in, [SWA_WIDTH, SWA_WIDTH + SWA_KV_WIDTH], axis=-1)
    q = q.reshape(B, S, SWA_Q_HEADS, SWA_HEAD_DIM)
    k = k.reshape(B, S, SWA_KV_HEADS, SWA_HEAD_DIM)
    v = v.reshape(B, S, SWA_KV_HEADS, SWA_HEAD_DIM)
    pos = jnp.arange(S)
    q = partial_rotary(q, pos)
    k = partial_rotary(k, pos)
    qb = q.reshape(B, nb, BLOCK, SWA_KV_HEADS, G, SWA_HEAD_DIM).astype(jnp.float32)

    def band(t):
        tb = t.reshape(B, nb, BLOCK, SWA_KV_HEADS, SWA_HEAD_DIM)
        prev = jnp.concatenate([jnp.zeros_like(tb[:, :1]), tb[:, :-1]], axis=1)
        return jnp.concatenate([prev, tb], axis=2).astype(jnp.float32)

    kb, vb = band(k), band(v)
    s = jnp.einsum('bnqkgd,bnskd->bnkgqs', qb, kb) * (SWA_HEAD_DIM ** -0.5)
    start = jnp.arange(nb)[:, None, None] * BLOCK
    qpos = start + jnp.arange(BLOCK)[None, :, None]
    kpos = start - BLOCK + jnp.arange(2 * BLOCK)[None, None, :]
    valid = (kpos <= qpos) & (qpos - kpos < SWA_WINDOW) & (kpos >= 0)
    s = jnp.where(valid[None, :, None, None], s, NEG_INF)
    sink = sinks.astype(jnp.float32).reshape(1, 1, SWA_KV_HEADS, G, 1, 1)
    m = jnp.maximum(jnp.max(s, axis=-1, keepdims=True), sink)
    p = jnp.exp(s - m)
    denom = jnp.sum(p, axis=-1, keepdims=True) + jnp.exp(sink - m)
    o = jnp.einsum('bnkgqs,bnskd->bnqkgd', p / denom, vb)
    o = o.reshape(B, S, SWA_WIDTH).astype(h.dtype)
    return o @ w_out


def fox_mixer(h, w_in, b_f, w_out):
    B, S, _ = h.shape
    nb = S // BLOCK
    q, k, v, fl = jnp.split(h @ w_in, [FOX_WIDTH, 2 * FOX_WIDTH, 3 * FOX_WIDTH], axis=-1)
    q = q.reshape(B, S, FOX_HEADS, FOX_HEAD_DIM)
    kf = k.reshape(B, S, FOX_HEADS, FOX_HEAD_DIM).astype(jnp.float32)
    vf = v.reshape(B, S, FOX_HEADS, FOX_HEAD_DIM).astype(jnp.float32)
    log_f = jax.nn.log_sigmoid(fl.astype(jnp.float32) + b_f.astype(jnp.float32))
    dec = jnp.cumsum(log_f, axis=1).transpose(0, 2, 1)
    kpos = jnp.arange(S)
    scale = FOX_HEAD_DIM ** -0.5

    def attend_block(i):
        start = i * BLOCK
        qb = lax.dynamic_slice_in_dim(q, start, BLOCK, axis=1).astype(jnp.float32)
        db = lax.dynamic_slice_in_dim(dec, start, BLOCK, axis=2)
        s = jnp.einsum('bqhd,bshd->bhqs', qb, kf) * scale + db[..., None] - dec[:, :, None, :]
        qpos = start + jnp.arange(BLOCK)
        s = jnp.where(kpos[None, :] <= qpos[:, None], s, NEG_INF)
        p = jax.nn.softmax(s, axis=-1)
        return jnp.einsum('bhqs,bshd->bqhd', p, vf)

    o = lax.map(attend_block, jnp.arange(nb))
    o = o.transpose(1, 0, 2, 3, 4).reshape(B, S, FOX_WIDTH).astype(h.dtype)
    return o @ w_out


def _fwd_setup_inputs(seed: int = 0) -> dict:
    key = jax.random.key(seed)
    keys = iter(jax.random.split(key, 16 * DEPTH + 4))

    def nrm(shape, scale):
        return jax.random.normal(next(keys), shape, jnp.float32) * scale

    def gain(n):
        return 1.0 + 0.1 * nrm((n,), 1.0)

    inp = {"x": nrm((BATCH, SEQ, D_MODEL), 1.0)}
    for i in range(DEPTH):
        kind = MIXER_KINDS[i % N_MIXERS]
        p = f"l{i}_"
        inp[p + "ffn1_norm"] = gain(D_MODEL)
        inp[p + "ffn1_wi"] = nrm((D_MODEL, 2 * D_FF), D_MODEL ** -0.5)
        inp[p + "ffn1_wo"] = nrm((D_FF, D_MODEL), D_FF ** -0.5)
        inp[p + "mix_norm"] = gain(D_MODEL)
        if kind == "gmlp":
            inp[p + "mix_win"] = nrm((D_MODEL, 2 * GMLP_WIDTH), D_MODEL ** -0.5)
            inp[p + "gmlp_vnorm"] = gain(GMLP_WIDTH)
            inp[p + "gmlp_ws"] = nrm((GMLP_GROUPS, GMLP_CHUNK, GMLP_CHUNK), GMLP_CHUNK ** -0.5)
            inp[p + "gmlp_bs"] = 1.0 + 0.1 * nrm((GMLP_GROUPS, GMLP_CHUNK), 1.0)
            mix_width = GMLP_WIDTH
        elif kind == "swa":
            inp[p + "mix_win"] = nrm((D_MODEL, SWA_WIDTH + 2 * SWA_KV_WIDTH), D_MODEL ** -0.5)
            inp[p + "swa_sinks"] = nrm((SWA_Q_HEADS,), 0.5)
            mix_width = SWA_WIDTH
        else:
            inp[p + "mix_win"] = nrm((D_MODEL, 3 * FOX_WIDTH + FOX_HEADS), D_MODEL ** -0.5)
            inp[p + "fox_bf"] = 1.0 + 0.5 * nrm((FOX_HEADS,), 1.0)
            mix_width = FOX_WIDTH
        inp[p + "mix_wout"] = nrm((mix_width, D_MODEL), mix_width ** -0.5)
        inp[p + "ffn2_norm"] = gain(D_MODEL)
        inp[p + "ffn2_wi"] = nrm((D_MODEL, 2 * D_FF), D_MODEL ** -0.5)
        inp[p + "ffn2_wo"] = nrm((D_FF, D_MODEL), D_FF ** -0.5)
    inp["final_norm"] = gain(D_MODEL)
    return inp


def _fwd_reference(x,
              l0_ffn1_norm, l0_ffn1_wi, l0_ffn1_wo, l0_mix_norm, l0_mix_win,
              l0_gmlp_vnorm, l0_gmlp_ws, l0_gmlp_bs, l0_mix_wout,
              l0_ffn2_norm, l0_ffn2_wi, l0_ffn2_wo,
              l1_ffn1_norm, l1_ffn1_wi, l1_ffn1_wo, l1_mix_norm, l1_mix_win,
              l1_swa_sinks, l1_mix_wout,
              l1_ffn2_norm, l1_ffn2_wi, l1_ffn2_wo,
              l2_ffn1_norm, l2_ffn1_wi, l2_ffn1_wo, l2_mix_norm, l2_mix_win,
              l2_fox_bf, l2_mix_wout,
              l2_ffn2_norm, l2_ffn2_wi, l2_ffn2_wo,
              l3_ffn1_norm, l3_ffn1_wi, l3_ffn1_wo, l3_mix_norm, l3_mix_win,
              l3_gmlp_vnorm, l3_gmlp_ws, l3_gmlp_bs, l3_mix_wout,
              l3_ffn2_norm, l3_ffn2_wi, l3_ffn2_wo,
              final_norm):
    ffn_params = [
        (l0_ffn1_norm, l0_ffn1_wi, l0_ffn1_wo, l0_mix_norm, l0_ffn2_norm, l0_ffn2_wi, l0_ffn2_wo),
        (l1_ffn1_norm, l1_ffn1_wi, l1_ffn1_wo, l1_mix_norm, l1_ffn2_norm, l1_ffn2_wi, l1_ffn2_wo),
        (l2_ffn1_norm, l2_ffn1_wi, l2_ffn1_wo, l2_mix_norm, l2_ffn2_norm, l2_ffn2_wi, l2_ffn2_wo),
        (l3_ffn1_norm, l3_ffn1_wi, l3_ffn1_wo, l3_mix_norm, l3_ffn2_norm, l3_ffn2_wi, l3_ffn2_wo),
    ]
    mixers = [
        functools.partial(gmlp_mixer, w_in=l0_mix_win, v_gain=l0_gmlp_vnorm, w_s=l0_gmlp_ws,
                          b_s=l0_gmlp_bs, w_out=l0_mix_wout),
        functools.partial(swa_mixer, w_in=l1_mix_win, sinks=l1_swa_sinks, w_out=l1_mix_wout),
        functools.partial(fox_mixer, w_in=l2_mix_win, b_f=l2_fox_bf, w_out=l2_mix_wout),
        functools.partial(gmlp_mixer, w_in=l3_mix_win, v_gain=l3_gmlp_vnorm, w_s=l3_gmlp_ws,
                          b_s=l3_gmlp_bs, w_out=l3_mix_wout),
    ]
    h = x
    for i in range(DEPTH):
        n1, wi1, wo1, nm, n2, wi2, wo2 = ffn_params[i]
        h = h + 0.5 * swiglu_ffn(rms_norm(h, n1), wi1, wo1)
        h = h + mixers[i](rms_norm(h, nm))
        h = h + 0.5 * swiglu_ffn(rms_norm(h, n2), wi2, wo2)
    return rms_norm(h, final_norm)


import jax as _jax
import jax.numpy as _jnp

TWIN_FORMAT = 'train_step'
FWD_PARAMS = ['x', 'l0_ffn1_norm', 'l0_ffn1_wi', 'l0_ffn1_wo', 'l0_mix_norm', 'l0_mix_win', 'l0_gmlp_vnorm', 'l0_gmlp_ws', 'l0_gmlp_bs', 'l0_mix_wout', 'l0_ffn2_norm', 'l0_ffn2_wi', 'l0_ffn2_wo', 'l1_ffn1_norm', 'l1_ffn1_wi', 'l1_ffn1_wo', 'l1_mix_norm', 'l1_mix_win', 'l1_swa_sinks', 'l1_mix_wout', 'l1_ffn2_norm', 'l1_ffn2_wi', 'l1_ffn2_wo', 'l2_ffn1_norm', 'l2_ffn1_wi', 'l2_ffn1_wo', 'l2_mix_norm', 'l2_mix_win', 'l2_fox_bf', 'l2_mix_wout', 'l2_ffn2_norm', 'l2_ffn2_wi', 'l2_ffn2_wo', 'l3_ffn1_norm', 'l3_ffn1_wi', 'l3_ffn1_wo', 'l3_mix_norm', 'l3_mix_win', 'l3_gmlp_vnorm', 'l3_gmlp_ws', 'l3_gmlp_bs', 'l3_mix_wout', 'l3_ffn2_norm', 'l3_ffn2_wi', 'l3_ffn2_wo', 'final_norm']
TWIN_WEIGHTS = ['l0_ffn1_norm', 'l0_ffn1_wi', 'l0_ffn1_wo', 'l0_mix_norm', 'l0_mix_win', 'l0_gmlp_vnorm', 'l0_gmlp_ws', 'l0_gmlp_bs', 'l0_mix_wout', 'l0_ffn2_norm', 'l0_ffn2_wi', 'l0_ffn2_wo', 'l1_ffn1_norm', 'l1_ffn1_wi', 'l1_ffn1_wo', 'l1_mix_norm', 'l1_mix_win', 'l1_swa_sinks', 'l1_mix_wout', 'l1_ffn2_norm', 'l1_ffn2_wi', 'l1_ffn2_wo', 'l2_ffn1_norm', 'l2_ffn1_wi', 'l2_ffn1_wo', 'l2_mix_norm', 'l2_mix_win', 'l2_fox_bf', 'l2_mix_wout', 'l2_ffn2_norm', 'l2_ffn2_wi', 'l2_ffn2_wo', 'l3_ffn1_norm', 'l3_ffn1_wi', 'l3_ffn1_wo', 'l3_mix_norm', 'l3_mix_win', 'l3_gmlp_vnorm', 'l3_gmlp_ws', 'l3_gmlp_bs', 'l3_mix_wout', 'l3_ffn2_norm', 'l3_ffn2_wi', 'l3_ffn2_wo', 'final_norm']
TWIN_DIFF_INPUT = 'x'
TWIN_INPUTS = ['x', 'l0_ffn1_norm', 'l0_ffn1_wi', 'l0_ffn1_wo', 'l0_mix_norm', 'l0_mix_win', 'l0_gmlp_vnorm', 'l0_gmlp_ws', 'l0_gmlp_bs', 'l0_mix_wout', 'l0_ffn2_norm', 'l0_ffn2_wi', 'l0_ffn2_wo', 'l1_ffn1_norm', 'l1_ffn1_wi', 'l1_ffn1_wo', 'l1_mix_norm', 'l1_mix_win', 'l1_swa_sinks', 'l1_mix_wout', 'l1_ffn2_norm', 'l1_ffn2_wi', 'l1_ffn2_wo', 'l2_ffn1_norm', 'l2_ffn1_wi', 'l2_ffn1_wo', 'l2_mix_norm', 'l2_mix_win', 'l2_fox_bf', 'l2_mix_wout', 'l2_ffn2_norm', 'l2_ffn2_wi', 'l2_ffn2_wo', 'l3_ffn1_norm', 'l3_ffn1_wi', 'l3_ffn1_wo', 'l3_mix_norm', 'l3_mix_win', 'l3_gmlp_vnorm', 'l3_gmlp_ws', 'l3_gmlp_bs', 'l3_mix_wout', 'l3_ffn2_norm', 'l3_ffn2_wi', 'l3_ffn2_wo', 'final_norm', 'loss_target', 'm_l0_ffn1_norm', 'm_l0_ffn1_wi', 'm_l0_ffn1_wo', 'm_l0_mix_norm', 'm_l0_mix_win', 'm_l0_gmlp_vnorm', 'm_l0_gmlp_ws', 'm_l0_gmlp_bs', 'm_l0_mix_wout', 'm_l0_ffn2_norm', 'm_l0_ffn2_wi', 'm_l0_ffn2_wo', 'm_l1_ffn1_norm', 'm_l1_ffn1_wi', 'm_l1_ffn1_wo', 'm_l1_mix_norm', 'm_l1_mix_win', 'm_l1_swa_sinks', 'm_l1_mix_wout', 'm_l1_ffn2_norm', 'm_l1_ffn2_wi', 'm_l1_ffn2_wo', 'm_l2_ffn1_norm', 'm_l2_ffn1_wi', 'm_l2_ffn1_wo', 'm_l2_mix_norm', 'm_l2_mix_win', 'm_l2_fox_bf', 'm_l2_mix_wout', 'm_l2_ffn2_norm', 'm_l2_ffn2_wi', 'm_l2_ffn2_wo', 'm_l3_ffn1_norm', 'm_l3_ffn1_wi', 'm_l3_ffn1_wo', 'm_l3_mix_norm', 'm_l3_mix_win', 'm_l3_gmlp_vnorm', 'm_l3_gmlp_ws', 'm_l3_gmlp_bs', 'm_l3_mix_wout', 'm_l3_ffn2_norm', 'm_l3_ffn2_wi', 'm_l3_ffn2_wo', 'm_final_norm', 'v_l0_ffn1_norm', 'v_l0_ffn1_wi', 'v_l0_ffn1_wo', 'v_l0_mix_norm', 'v_l0_mix_win', 'v_l0_gmlp_vnorm', 'v_l0_gmlp_ws', 'v_l0_gmlp_bs', 'v_l0_mix_wout', 'v_l0_ffn2_norm', 'v_l0_ffn2_wi', 'v_l0_ffn2_wo', 'v_l1_ffn1_norm', 'v_l1_ffn1_wi', 'v_l1_ffn1_wo', 'v_l1_mix_norm', 'v_l1_mix_win', 'v_l1_swa_sinks', 'v_l1_mix_wout', 'v_l1_ffn2_norm', 'v_l1_ffn2_wi', 'v_l1_ffn2_wo', 'v_l2_ffn1_norm', 'v_l2_ffn1_wi', 'v_l2_ffn1_wo', 'v_l2_mix_norm', 'v_l2_mix_win', 'v_l2_fox_bf', 'v_l2_mix_wout', 'v_l2_ffn2_norm', 'v_l2_ffn2_wi', 'v_l2_ffn2_wo', 'v_l3_ffn1_norm', 'v_l3_ffn1_wi', 'v_l3_ffn1_wo', 'v_l3_mix_norm', 'v_l3_mix_win', 'v_l3_gmlp_vnorm', 'v_l3_gmlp_ws', 'v_l3_gmlp_bs', 'v_l3_mix_wout', 'v_l3_ffn2_norm', 'v_l3_ffn2_wi', 'v_l3_ffn2_wo', 'v_final_norm']
TWIN_OUTPUTS = ['loss', 'grad_x', 'grad_l0_ffn1_norm', 'grad_l0_ffn1_wi', 'grad_l0_ffn1_wo', 'grad_l0_mix_norm', 'grad_l0_mix_win', 'grad_l0_gmlp_vnorm', 'grad_l0_gmlp_ws', 'grad_l0_gmlp_bs', 'grad_l0_mix_wout', 'grad_l0_ffn2_norm', 'grad_l0_ffn2_wi', 'grad_l0_ffn2_wo', 'grad_l1_ffn1_norm', 'grad_l1_ffn1_wi', 'grad_l1_ffn1_wo', 'grad_l1_mix_norm', 'grad_l1_mix_win', 'grad_l1_swa_sinks', 'grad_l1_mix_wout', 'grad_l1_ffn2_norm', 'grad_l1_ffn2_wi', 'grad_l1_ffn2_wo', 'grad_l2_ffn1_norm', 'grad_l2_ffn1_wi', 'grad_l2_ffn1_wo', 'grad_l2_mix_norm', 'grad_l2_mix_win', 'grad_l2_fox_bf', 'grad_l2_mix_wout', 'grad_l2_ffn2_norm', 'grad_l2_ffn2_wi', 'grad_l2_ffn2_wo', 'grad_l3_ffn1_norm', 'grad_l3_ffn1_wi', 'grad_l3_ffn1_wo', 'grad_l3_mix_norm', 'grad_l3_mix_win', 'grad_l3_gmlp_vnorm', 'grad_l3_gmlp_ws', 'grad_l3_gmlp_bs', 'grad_l3_mix_wout', 'grad_l3_ffn2_norm', 'grad_l3_ffn2_wi', 'grad_l3_ffn2_wo', 'grad_final_norm', 'delta_l0_ffn1_norm', 'delta_l0_ffn1_wi', 'delta_l0_ffn1_wo', 'delta_l0_mix_norm', 'delta_l0_mix_win', 'delta_l0_gmlp_vnorm', 'delta_l0_gmlp_ws', 'delta_l0_gmlp_bs', 'delta_l0_mix_wout', 'delta_l0_ffn2_norm', 'delta_l0_ffn2_wi', 'delta_l0_ffn2_wo', 'delta_l1_ffn1_norm', 'delta_l1_ffn1_wi', 'delta_l1_ffn1_wo', 'delta_l1_mix_norm', 'delta_l1_mix_win', 'delta_l1_swa_sinks', 'delta_l1_mix_wout', 'delta_l1_ffn2_norm', 'delta_l1_ffn2_wi', 'delta_l1_ffn2_wo', 'delta_l2_ffn1_norm', 'delta_l2_ffn1_wi', 'delta_l2_ffn1_wo', 'delta_l2_mix_norm', 'delta_l2_mix_win', 'delta_l2_fox_bf', 'delta_l2_mix_wout', 'delta_l2_ffn2_norm', 'delta_l2_ffn2_wi', 'delta_l2_ffn2_wo', 'delta_l3_ffn1_norm', 'delta_l3_ffn1_wi', 'delta_l3_ffn1_wo', 'delta_l3_mix_norm', 'delta_l3_mix_win', 'delta_l3_gmlp_vnorm', 'delta_l3_gmlp_ws', 'delta_l3_gmlp_bs', 'delta_l3_mix_wout', 'delta_l3_ffn2_norm', 'delta_l3_ffn2_wi', 'delta_l3_ffn2_wo', 'delta_final_norm', 'new_m_l0_ffn1_norm', 'new_m_l0_ffn1_wi', 'new_m_l0_ffn1_wo', 'new_m_l0_mix_norm', 'new_m_l0_mix_win', 'new_m_l0_gmlp_vnorm', 'new_m_l0_gmlp_ws', 'new_m_l0_gmlp_bs', 'new_m_l0_mix_wout', 'new_m_l0_ffn2_norm', 'new_m_l0_ffn2_wi', 'new_m_l0_ffn2_wo', 'new_m_l1_ffn1_norm', 'new_m_l1_ffn1_wi', 'new_m_l1_ffn1_wo', 'new_m_l1_mix_norm', 'new_m_l1_mix_win', 'new_m_l1_swa_sinks', 'new_m_l1_mix_wout', 'new_m_l1_ffn2_norm', 'new_m_l1_ffn2_wi', 'new_m_l1_ffn2_wo', 'new_m_l2_ffn1_norm', 'new_m_l2_ffn1_wi', 'new_m_l2_ffn1_wo', 'new_m_l2_mix_norm', 'new_m_l2_mix_win', 'new_m_l2_fox_bf', 'new_m_l2_mix_wout', 'new_m_l2_ffn2_norm', 'new_m_l2_ffn2_wi', 'new_m_l2_ffn2_wo', 'new_m_l3_ffn1_norm', 'new_m_l3_ffn1_wi', 'new_m_l3_ffn1_wo', 'new_m_l3_mix_norm', 'new_m_l3_mix_win', 'new_m_l3_gmlp_vnorm', 'new_m_l3_gmlp_ws', 'new_m_l3_gmlp_bs', 'new_m_l3_mix_wout', 'new_m_l3_ffn2_norm', 'new_m_l3_ffn2_wi', 'new_m_l3_ffn2_wo', 'new_m_final_norm', 'new_v_l0_ffn1_norm', 'new_v_l0_ffn1_wi', 'new_v_l0_ffn1_wo', 'new_v_l0_mix_norm', 'new_v_l0_mix_win', 'new_v_l0_gmlp_vnorm', 'new_v_l0_gmlp_ws', 'new_v_l0_gmlp_bs', 'new_v_l0_mix_wout', 'new_v_l0_ffn2_norm', 'new_v_l0_ffn2_wi', 'new_v_l0_ffn2_wo', 'new_v_l1_ffn1_norm', 'new_v_l1_ffn1_wi', 'new_v_l1_ffn1_wo', 'new_v_l1_mix_norm', 'new_v_l1_mix_win', 'new_v_l1_swa_sinks', 'new_v_l1_mix_wout', 'new_v_l1_ffn2_norm', 'new_v_l1_ffn2_wi', 'new_v_l1_ffn2_wo', 'new_v_l2_ffn1_norm', 'new_v_l2_ffn1_wi', 'new_v_l2_ffn1_wo', 'new_v_l2_mix_norm', 'new_v_l2_mix_win', 'new_v_l2_fox_bf', 'new_v_l2_mix_wout', 'new_v_l2_ffn2_norm', 'new_v_l2_ffn2_wi', 'new_v_l2_ffn2_wo', 'new_v_l3_ffn1_norm', 'new_v_l3_ffn1_wi', 'new_v_l3_ffn1_wo', 'new_v_l3_mix_norm', 'new_v_l3_mix_win', 'new_v_l3_gmlp_vnorm', 'new_v_l3_gmlp_ws', 'new_v_l3_gmlp_bs', 'new_v_l3_mix_wout', 'new_v_l3_ffn2_norm', 'new_v_l3_ffn2_wi', 'new_v_l3_ffn2_wo', 'new_v_final_norm']
TWIN_LEAF_KINDS = {'loss': 'loss', 'grad_x': 'grad_x', 'grad_l0_ffn1_norm': 'grad_w', 'grad_l0_ffn1_wi': 'grad_w', 'grad_l0_ffn1_wo': 'grad_w', 'grad_l0_mix_norm': 'grad_w', 'grad_l0_mix_win': 'grad_w', 'grad_l0_gmlp_vnorm': 'grad_w', 'grad_l0_gmlp_ws': 'grad_w', 'grad_l0_gmlp_bs': 'grad_w', 'grad_l0_mix_wout': 'grad_w', 'grad_l0_ffn2_norm': 'grad_w', 'grad_l0_ffn2_wi': 'grad_w', 'grad_l0_ffn2_wo': 'grad_w', 'grad_l1_ffn1_norm': 'grad_w', 'grad_l1_ffn1_wi': 'grad_w', 'grad_l1_ffn1_wo': 'grad_w', 'grad_l1_mix_norm': 'grad_w', 'grad_l1_mix_win': 'grad_w', 'grad_l1_swa_sinks': 'grad_w', 'grad_l1_mix_wout': 'grad_w', 'grad_l1_ffn2_norm': 'grad_w', 'grad_l1_ffn2_wi': 'grad_w', 'grad_l1_ffn2_wo': 'grad_w', 'grad_l2_ffn1_norm': 'grad_w', 'grad_l2_ffn1_wi': 'grad_w', 'grad_l2_ffn1_wo': 'grad_w', 'grad_l2_mix_norm': 'grad_w', 'grad_l2_mix_win': 'grad_w', 'grad_l2_fox_bf': 'grad_w', 'grad_l2_mix_wout': 'grad_w', 'grad_l2_ffn2_norm': 'grad_w', 'grad_l2_ffn2_wi': 'grad_w', 'grad_l2_ffn2_wo': 'grad_w', 'grad_l3_ffn1_norm': 'grad_w', 'grad_l3_ffn1_wi': 'grad_w', 'grad_l3_ffn1_wo': 'grad_w', 'grad_l3_mix_norm': 'grad_w', 'grad_l3_mix_win': 'grad_w', 'grad_l3_gmlp_vnorm': 'grad_w', 'grad_l3_gmlp_ws': 'grad_w', 'grad_l3_gmlp_bs': 'grad_w', 'grad_l3_mix_wout': 'grad_w', 'grad_l3_ffn2_norm': 'grad_w', 'grad_l3_ffn2_wi': 'grad_w', 'grad_l3_ffn2_wo': 'grad_w', 'grad_final_norm': 'grad_w', 'delta_l0_ffn1_norm': 'delta_w', 'delta_l0_ffn1_wi': 'delta_w', 'delta_l0_ffn1_wo': 'delta_w', 'delta_l0_mix_norm': 'delta_w', 'delta_l0_mix_win': 'delta_w', 'delta_l0_gmlp_vnorm': 'delta_w', 'delta_l0_gmlp_ws': 'delta_w', 'delta_l0_gmlp_bs': 'delta_w', 'delta_l0_mix_wout': 'delta_w', 'delta_l0_ffn2_norm': 'delta_w', 'delta_l0_ffn2_wi': 'delta_w', 'delta_l0_ffn2_wo': 'delta_w', 'delta_l1_ffn1_norm': 'delta_w', 'delta_l1_ffn1_wi': 'delta_w', 'delta_l1_ffn1_wo': 'delta_w', 'delta_l1_mix_norm': 'delta_w', 'delta_l1_mix_win': 'delta_w', 'delta_l1_swa_sinks': 'delta_w', 'delta_l1_mix_wout': 'delta_w', 'delta_l1_ffn2_norm': 'delta_w', 'delta_l1_ffn2_wi': 'delta_w', 'delta_l1_ffn2_wo': 'delta_w', 'delta_l2_ffn1_norm': 'delta_w', 'delta_l2_ffn1_wi': 'delta_w', 'delta_l2_ffn1_wo': 'delta_w', 'delta_l2_mix_norm': 'delta_w', 'delta_l2_mix_win': 'delta_w', 'delta_l2_fox_bf': 'delta_w', 'delta_l2_mix_wout': 'delta_w', 'delta_l2_ffn2_norm': 'delta_w', 'delta_l2_ffn2_wi': 'delta_w', 'delta_l2_ffn2_wo': 'delta_w', 'delta_l3_ffn1_norm': 'delta_w', 'delta_l3_ffn1_wi': 'delta_w', 'delta_l3_ffn1_wo': 'delta_w', 'delta_l3_mix_norm': 'delta_w', 'delta_l3_mix_win': 'delta_w', 'delta_l3_gmlp_vnorm': 'delta_w', 'delta_l3_gmlp_ws': 'delta_w', 'delta_l3_gmlp_bs': 'delta_w', 'delta_l3_mix_wout': 'delta_w', 'delta_l3_ffn2_norm': 'delta_w', 'delta_l3_ffn2_wi': 'delta_w', 'delta_l3_ffn2_wo': 'delta_w', 'delta_final_norm': 'delta_w', 'new_m_l0_ffn1_norm': 'new_m', 'new_m_l0_ffn1_wi': 'new_m', 'new_m_l0_ffn1_wo': 'new_m', 'new_m_l0_mix_norm': 'new_m', 'new_m_l0_mix_win': 'new_m', 'new_m_l0_gmlp_vnorm': 'new_m', 'new_m_l0_gmlp_ws': 'new_m', 'new_m_l0_gmlp_bs': 'new_m', 'new_m_l0_mix_wout': 'new_m', 'new_m_l0_ffn2_norm': 'new_m', 'new_m_l0_ffn2_wi': 'new_m', 'new_m_l0_ffn2_wo': 'new_m', 'new_m_l1_ffn1_norm': 'new_m', 'new_m_l1_ffn1_wi': 'new_m', 'new_m_l1_ffn1_wo': 'new_m', 'new_m_l1_mix_norm': 'new_m', 'new_m_l1_mix_win': 'new_m', 'new_m_l1_swa_sinks': 'new_m', 'new_m_l1_mix_wout': 'new_m', 'new_m_l1_ffn2_norm': 'new_m', 'new_m_l1_ffn2_wi': 'new_m', 'new_m_l1_ffn2_wo': 'new_m', 'new_m_l2_ffn1_norm': 'new_m', 'new_m_l2_ffn1_wi': 'new_m', 'new_m_l2_ffn1_wo': 'new_m', 'new_m_l2_mix_norm': 'new_m', 'new_m_l2_mix_win': 'new_m', 'new_m_l2_fox_bf': 'new_m', 'new_m_l2_mix_wout': 'new_m', 'new_m_l2_ffn2_norm': 'new_m', 'new_m_l2_ffn2_wi': 'new_m', 'new_m_l2_ffn2_wo': 'new_m', 'new_m_l3_ffn1_norm': 'new_m', 'new_m_l3_ffn1_wi': 'new_m', 'new_m_l3_ffn1_wo': 'new_m', 'new_m_l3_mix_norm': 'new_m', 'new_m_l3_mix_win': 'new_m', 'new_m_l3_gmlp_vnorm': 'new_m', 'new_m_l3_gmlp_ws': 'new_m', 'new_m_l3_gmlp_bs': 'new_m', 'new_m_l3_mix_wout': 'new_m', 'new_m_l3_ffn2_norm': 'new_m', 'new_m_l3_ffn2_wi': 'new_m', 'new_m_l3_ffn2_wo': 'new_m', 'new_m_final_norm': 'new_m', 'new_v_l0_ffn1_norm': 'new_v', 'new_v_l0_ffn1_wi': 'new_v', 'new_v_l0_ffn1_wo': 'new_v', 'new_v_l0_mix_norm': 'new_v', 'new_v_l0_mix_win': 'new_v', 'new_v_l0_gmlp_vnorm': 'new_v', 'new_v_l0_gmlp_ws': 'new_v', 'new_v_l0_gmlp_bs': 'new_v', 'new_v_l0_mix_wout': 'new_v', 'new_v_l0_ffn2_norm': 'new_v', 'new_v_l0_ffn2_wi': 'new_v', 'new_v_l0_ffn2_wo': 'new_v', 'new_v_l1_ffn1_norm': 'new_v', 'new_v_l1_ffn1_wi': 'new_v', 'new_v_l1_ffn1_wo': 'new_v', 'new_v_l1_mix_norm': 'new_v', 'new_v_l1_mix_win': 'new_v', 'new_v_l1_swa_sinks': 'new_v', 'new_v_l1_mix_wout': 'new_v', 'new_v_l1_ffn2_norm': 'new_v', 'new_v_l1_ffn2_wi': 'new_v', 'new_v_l1_ffn2_wo': 'new_v', 'new_v_l2_ffn1_norm': 'new_v', 'new_v_l2_ffn1_wi': 'new_v', 'new_v_l2_ffn1_wo': 'new_v', 'new_v_l2_mix_norm': 'new_v', 'new_v_l2_mix_win': 'new_v', 'new_v_l2_fox_bf': 'new_v', 'new_v_l2_mix_wout': 'new_v', 'new_v_l2_ffn2_norm': 'new_v', 'new_v_l2_ffn2_wi': 'new_v', 'new_v_l2_ffn2_wo': 'new_v', 'new_v_l3_ffn1_norm': 'new_v', 'new_v_l3_ffn1_wi': 'new_v', 'new_v_l3_ffn1_wo': 'new_v', 'new_v_l3_mix_norm': 'new_v', 'new_v_l3_mix_win': 'new_v', 'new_v_l3_gmlp_vnorm': 'new_v', 'new_v_l3_gmlp_ws': 'new_v', 'new_v_l3_gmlp_bs': 'new_v', 'new_v_l3_mix_wout': 'new_v', 'new_v_l3_ffn2_norm': 'new_v', 'new_v_l3_ffn2_wi': 'new_v', 'new_v_l3_ffn2_wo': 'new_v', 'new_v_final_norm': 'new_v'}


def _forward(args):
    return _fwd_reference(*[args[k] for k in FWD_PARAMS])


def _output_shape():
    out = _jax.eval_shape(lambda: _forward(_fwd_setup_inputs(0)))
    return out.shape, out.dtype

N_MICROBATCH = 1
ADAM_LR = 0.001
ADAM_B1 = 0.9
ADAM_B2 = 0.999
ADAM_EPS = 1e-08
ADAM_WD = 0.01
ADAM_STEP = 10
PER_EXAMPLE_BATCH_AXIS = {'x': 0, 'loss_target': 0}
SHARED_INPUTS = []
_WEIGHT_DTYPES = {'l0_ffn1_norm': _jnp.float32, 'l0_ffn1_wi': _jnp.float32, 'l0_ffn1_wo': _jnp.float32, 'l0_mix_norm': _jnp.float32, 'l0_mix_win': _jnp.float32, 'l0_gmlp_vnorm': _jnp.float32, 'l0_gmlp_ws': _jnp.float32, 'l0_gmlp_bs': _jnp.float32, 'l0_mix_wout': _jnp.float32, 'l0_ffn2_norm': _jnp.float32, 'l0_ffn2_wi': _jnp.float32, 'l0_ffn2_wo': _jnp.float32, 'l1_ffn1_norm': _jnp.float32, 'l1_ffn1_wi': _jnp.float32, 'l1_ffn1_wo': _jnp.float32, 'l1_mix_norm': _jnp.float32, 'l1_mix_win': _jnp.float32, 'l1_swa_sinks': _jnp.float32, 'l1_mix_wout': _jnp.float32, 'l1_ffn2_norm': _jnp.float32, 'l1_ffn2_wi': _jnp.float32, 'l1_ffn2_wo': _jnp.float32, 'l2_ffn1_norm': _jnp.float32, 'l2_ffn1_wi': _jnp.float32, 'l2_ffn1_wo': _jnp.float32, 'l2_mix_norm': _jnp.float32, 'l2_mix_win': _jnp.float32, 'l2_fox_bf': _jnp.float32, 'l2_mix_wout': _jnp.float32, 'l2_ffn2_norm': _jnp.float32, 'l2_ffn2_wi': _jnp.float32, 'l2_ffn2_wo': _jnp.float32, 'l3_ffn1_norm': _jnp.float32, 'l3_ffn1_wi': _jnp.float32, 'l3_ffn1_wo': _jnp.float32, 'l3_mix_norm': _jnp.float32, 'l3_mix_win': _jnp.float32, 'l3_gmlp_vnorm': _jnp.float32, 'l3_gmlp_ws': _jnp.float32, 'l3_gmlp_bs': _jnp.float32, 'l3_mix_wout': _jnp.float32, 'l3_ffn2_norm': _jnp.float32, 'l3_ffn2_wi': _jnp.float32, 'l3_ffn2_wo': _jnp.float32, 'final_norm': _jnp.float32}
MOMENT_SCALE = {'l0_ffn1_norm': 3.574640e-02, 'l0_ffn1_wi': 1.537774e-02, 'l0_ffn1_wo': 2.513634e-02, 'l0_mix_norm': 5.883230e-02, 'l0_mix_win': 4.142816e-02, 'l0_gmlp_vnorm': 2.835184e-02, 'l0_gmlp_ws': 2.767253e-02, 'l0_gmlp_bs': 4.063959e-02, 'l0_mix_wout': 9.406837e-02, 'l0_ffn2_norm': 2.608960e-02, 'l0_ffn2_wi': 1.127585e-02, 'l0_ffn2_wo': 1.883134e-02, 'l1_ffn1_norm': 2.478315e-02, 'l1_ffn1_wi': 1.073561e-02, 'l1_ffn1_wo': 1.801492e-02, 'l1_mix_norm': 4.377484e-02, 'l1_mix_win': 4.207203e-02, 'l1_swa_sinks': 1.104390e-02, 'l1_mix_wout': 5.541483e-02, 'l1_ffn2_norm': 2.450450e-02, 'l1_ffn2_wi': 1.033956e-02, 'l1_ffn2_wo': 1.738004e-02, 'l2_ffn1_norm': 2.407474e-02, 'l2_ffn1_wi': 9.764273e-03, 'l2_ffn1_wo': 1.640339e-02, 'l2_mix_norm': 5.572370e-02, 'l2_mix_win': 3.181581e-02, 'l2_fox_bf': 1.057491e-01, 'l2_mix_wout': 5.401722e-02, 'l2_ffn2_norm': 2.038733e-02, 'l2_ffn2_wi': 8.532187e-03, 'l2_ffn2_wo': 1.436975e-02, 'l3_ffn1_norm': 1.915991e-02, 'l3_ffn1_wi': 8.192482e-03, 'l3_ffn1_wo': 1.378675e-02, 'l3_mix_norm': 4.288963e-02, 'l3_mix_win': 2.923829e-02, 'l3_gmlp_vnorm': 1.792568e-02, 'l3_gmlp_ws': 1.664926e-02, 'l3_gmlp_bs': 2.350173e-02, 'l3_mix_wout': 6.640594e-02, 'l3_ffn2_norm': 1.658233e-02, 'l3_ffn2_wi': 7.237743e-03, 'l3_ffn2_wo': 1.240470e-02, 'final_norm': 8.097234e+00}


def _to_microbatches(a, axis):
    t = _jnp.moveaxis(a, axis, 0)
    t = t.reshape((N_MICROBATCH, t.shape[0] // N_MICROBATCH) + t.shape[1:])
    return _jnp.moveaxis(t, 1, axis + 1)


def setup_inputs(seed: int = 0) -> dict:
    inp = _fwd_setup_inputs(seed)
    key = _jax.random.fold_in(_jax.random.key(seed), 7919)
    shape, _ = _output_shape()
    out = dict(inp)
    out["loss_target"] = _jax.random.normal(_jax.random.fold_in(key, 0), shape, _jnp.float32)
    for i, name in enumerate(TWIN_WEIGHTS):
        w = inp[name].astype(_jnp.float32)
        if MOMENT_SCALE is None:
            s = _jnp.sqrt(_jnp.mean(_jnp.square(w)) + 1e-30)
        else:
            s = MOMENT_SCALE[name]
        km, kv = _jax.random.split(_jax.random.fold_in(key, i + 1))
        out[name] = w
        out["m_" + name] = s * _jax.random.normal(km, w.shape, _jnp.float32)
        out["v_" + name] = (s * s) * _jax.random.uniform(kv, w.shape, _jnp.float32, 0.5, 1.5)
    if N_MICROBATCH > 1:
        for name, axis in PER_EXAMPLE_BATCH_AXIS.items():
            out[name] = _to_microbatches(out[name], axis)
    return {'x': out['x'], 'l0_ffn1_norm': out['l0_ffn1_norm'], 'l0_ffn1_wi': out['l0_ffn1_wi'], 'l0_ffn1_wo': out['l0_ffn1_wo'], 'l0_mix_norm': out['l0_mix_norm'], 'l0_mix_win': out['l0_mix_win'], 'l0_gmlp_vnorm': out['l0_gmlp_vnorm'], 'l0_gmlp_ws': out['l0_gmlp_ws'], 'l0_gmlp_bs': out['l0_gmlp_bs'], 'l0_mix_wout': out['l0_mix_wout'], 'l0_ffn2_norm': out['l0_ffn2_norm'], 'l0_ffn2_wi': out['l0_ffn2_wi'], 'l0_ffn2_wo': out['l0_ffn2_wo'], 'l1_ffn1_norm': out['l1_ffn1_norm'], 'l1_ffn1_wi': out['l1_ffn1_wi'], 'l1_ffn1_wo': out['l1_ffn1_wo'], 'l1_mix_norm': out['l1_mix_norm'], 'l1_mix_win': out['l1_mix_win'], 'l1_swa_sinks': out['l1_swa_sinks'], 'l1_mix_wout': out['l1_mix_wout'], 'l1_ffn2_norm': out['l1_ffn2_norm'], 'l1_ffn2_wi': out['l1_ffn2_wi'], 'l1_ffn2_wo': out['l1_ffn2_wo'], 'l2_ffn1_norm': out['l2_ffn1_norm'], 'l2_ffn1_wi': out['l2_ffn1_wi'], 'l2_ffn1_wo': out['l2_ffn1_wo'], 'l2_mix_norm': out['l2_mix_norm'], 'l2_mix_win': out['l2_mix_win'], 'l2_fox_bf': out['l2_fox_bf'], 'l2_mix_wout': out['l2_mix_wout'], 'l2_ffn2_norm': out['l2_ffn2_norm'], 'l2_ffn2_wi': out['l2_ffn2_wi'], 'l2_ffn2_wo': out['l2_ffn2_wo'], 'l3_ffn1_norm': out['l3_ffn1_norm'], 'l3_ffn1_wi': out['l3_ffn1_wi'], 'l3_ffn1_wo': out['l3_ffn1_wo'], 'l3_mix_norm': out['l3_mix_norm'], 'l3_mix_win': out['l3_mix_win'], 'l3_gmlp_vnorm': out['l3_gmlp_vnorm'], 'l3_gmlp_ws': out['l3_gmlp_ws'], 'l3_gmlp_bs': out['l3_gmlp_bs'], 'l3_mix_wout': out['l3_mix_wout'], 'l3_ffn2_norm': out['l3_ffn2_norm'], 'l3_ffn2_wi': out['l3_ffn2_wi'], 'l3_ffn2_wo': out['l3_ffn2_wo'], 'final_norm': out['final_norm'], 'loss_target': out['loss_target'], 'm_l0_ffn1_norm': out['m_l0_ffn1_norm'], 'm_l0_ffn1_wi': out['m_l0_ffn1_wi'], 'm_l0_ffn1_wo': out['m_l0_ffn1_wo'], 'm_l0_mix_norm': out['m_l0_mix_norm'], 'm_l0_mix_win': out['m_l0_mix_win'], 'm_l0_gmlp_vnorm': out['m_l0_gmlp_vnorm'], 'm_l0_gmlp_ws': out['m_l0_gmlp_ws'], 'm_l0_gmlp_bs': out['m_l0_gmlp_bs'], 'm_l0_mix_wout': out['m_l0_mix_wout'], 'm_l0_ffn2_norm': out['m_l0_ffn2_norm'], 'm_l0_ffn2_wi': out['m_l0_ffn2_wi'], 'm_l0_ffn2_wo': out['m_l0_ffn2_wo'], 'm_l1_ffn1_norm': out['m_l1_ffn1_norm'], 'm_l1_ffn1_wi': out['m_l1_ffn1_wi'], 'm_l1_ffn1_wo': out['m_l1_ffn1_wo'], 'm_l1_mix_norm': out['m_l1_mix_norm'], 'm_l1_mix_win': out['m_l1_mix_win'], 'm_l1_swa_sinks': out['m_l1_swa_sinks'], 'm_l1_mix_wout': out['m_l1_mix_wout'], 'm_l1_ffn2_norm': out['m_l1_ffn2_norm'], 'm_l1_ffn2_wi': out['m_l1_ffn2_wi'], 'm_l1_ffn2_wo': out['m_l1_ffn2_wo'], 'm_l2_ffn1_norm': out['m_l2_ffn1_norm'], 'm_l2_ffn1_wi': out['m_l2_ffn1_wi'], 'm_l2_ffn1_wo': out['m_l2_ffn1_wo'], 'm_l2_mix_norm': out['m_l2_mix_norm'], 'm_l2_mix_win': out['m_l2_mix_win'], 'm_l2_fox_bf': out['m_l2_fox_bf'], 'm_l2_mix_wout': out['m_l2_mix_wout'], 'm_l2_ffn2_norm': out['m_l2_ffn2_norm'], 'm_l2_ffn2_wi': out['m_l2_ffn2_wi'], 'm_l2_ffn2_wo': out['m_l2_ffn2_wo'], 'm_l3_ffn1_norm': out['m_l3_ffn1_norm'], 'm_l3_ffn1_wi': out['m_l3_ffn1_wi'], 'm_l3_ffn1_wo': out['m_l3_ffn1_wo'], 'm_l3_mix_norm': out['m_l3_mix_norm'], 'm_l3_mix_win': out['m_l3_mix_win'], 'm_l3_gmlp_vnorm': out['m_l3_gmlp_vnorm'], 'm_l3_gmlp_ws': out['m_l3_gmlp_ws'], 'm_l3_gmlp_bs': out['m_l3_gmlp_bs'], 'm_l3_mix_wout': out['m_l3_mix_wout'], 'm_l3_ffn2_norm': out['m_l3_ffn2_norm'], 'm_l3_ffn2_wi': out['m_l3_ffn2_wi'], 'm_l3_ffn2_wo': out['m_l3_ffn2_wo'], 'm_final_norm': out['m_final_norm'], 'v_l0_ffn1_norm': out['v_l0_ffn1_norm'], 'v_l0_ffn1_wi': out['v_l0_ffn1_wi'], 'v_l0_ffn1_wo': out['v_l0_ffn1_wo'], 'v_l0_mix_norm': out['v_l0_mix_norm'], 'v_l0_mix_win': out['v_l0_mix_win'], 'v_l0_gmlp_vnorm': out['v_l0_gmlp_vnorm'], 'v_l0_gmlp_ws': out['v_l0_gmlp_ws'], 'v_l0_gmlp_bs': out['v_l0_gmlp_bs'], 'v_l0_mix_wout': out['v_l0_mix_wout'], 'v_l0_ffn2_norm': out['v_l0_ffn2_norm'], 'v_l0_ffn2_wi': out['v_l0_ffn2_wi'], 'v_l0_ffn2_wo': out['v_l0_ffn2_wo'], 'v_l1_ffn1_norm': out['v_l1_ffn1_norm'], 'v_l1_ffn1_wi': out['v_l1_ffn1_wi'], 'v_l1_ffn1_wo': out['v_l1_ffn1_wo'], 'v_l1_mix_norm': out['v_l1_mix_norm'], 'v_l1_mix_win': out['v_l1_mix_win'], 'v_l1_swa_sinks': out['v_l1_swa_sinks'], 'v_l1_mix_wout': out['v_l1_mix_wout'], 'v_l1_ffn2_norm': out['v_l1_ffn2_norm'], 'v_l1_ffn2_wi': out['v_l1_ffn2_wi'], 'v_l1_ffn2_wo': out['v_l1_ffn2_wo'], 'v_l2_ffn1_norm': out['v_l2_ffn1_norm'], 'v_l2_ffn1_wi': out['v_l2_ffn1_wi'], 'v_l2_ffn1_wo': out['v_l2_ffn1_wo'], 'v_l2_mix_norm': out['v_l2_mix_norm'], 'v_l2_mix_win': out['v_l2_mix_win'], 'v_l2_fox_bf': out['v_l2_fox_bf'], 'v_l2_mix_wout': out['v_l2_mix_wout'], 'v_l2_ffn2_norm': out['v_l2_ffn2_norm'], 'v_l2_ffn2_wi': out['v_l2_ffn2_wi'], 'v_l2_ffn2_wo': out['v_l2_ffn2_wo'], 'v_l3_ffn1_norm': out['v_l3_ffn1_norm'], 'v_l3_ffn1_wi': out['v_l3_ffn1_wi'], 'v_l3_ffn1_wo': out['v_l3_ffn1_wo'], 'v_l3_mix_norm': out['v_l3_mix_norm'], 'v_l3_mix_win': out['v_l3_mix_win'], 'v_l3_gmlp_vnorm': out['v_l3_gmlp_vnorm'], 'v_l3_gmlp_ws': out['v_l3_gmlp_ws'], 'v_l3_gmlp_bs': out['v_l3_gmlp_bs'], 'v_l3_mix_wout': out['v_l3_mix_wout'], 'v_l3_ffn2_norm': out['v_l3_ffn2_norm'], 'v_l3_ffn2_wi': out['v_l3_ffn2_wi'], 'v_l3_ffn2_wo': out['v_l3_ffn2_wo'], 'v_final_norm': out['v_final_norm']}


def _loss(weights, diff, rest, loss_target):
    with _jax.named_scope("forward"):
        args = {**rest, TWIN_DIFF_INPUT: diff, **{k: w.astype(_WEIGHT_DTYPES[k]) for k, w in weights.items()}}
        y = _forward(args)
    with _jax.named_scope("loss_head"):
        err = _jnp.square(y.astype(_jnp.float32) - loss_target)
        return 0.5 * _jnp.sum(_jnp.mean(err, axis=-1)) if err.ndim else 0.5 * err


def _adamw(w, g, m, v):
    m = ADAM_B1 * m + (1.0 - ADAM_B1) * g
    v = ADAM_B2 * v + (1.0 - ADAM_B2) * _jnp.square(g)
    m_hat = m / (1.0 - ADAM_B1 ** ADAM_STEP)
    v_hat = v / (1.0 - ADAM_B2 ** ADAM_STEP)
    delta = -ADAM_LR * (m_hat / (_jnp.sqrt(v_hat) + ADAM_EPS) + ADAM_WD * w)
    return delta, m, v


def reference(x, l0_ffn1_norm, l0_ffn1_wi, l0_ffn1_wo, l0_mix_norm, l0_mix_win, l0_gmlp_vnorm, l0_gmlp_ws, l0_gmlp_bs, l0_mix_wout, l0_ffn2_norm, l0_ffn2_wi, l0_ffn2_wo, l1_ffn1_norm, l1_ffn1_wi, l1_ffn1_wo, l1_mix_norm, l1_mix_win, l1_swa_sinks, l1_mix_wout, l1_ffn2_norm, l1_ffn2_wi, l1_ffn2_wo, l2_ffn1_norm, l2_ffn1_wi, l2_ffn1_wo, l2_mix_norm, l2_mix_win, l2_fox_bf, l2_mix_wout, l2_ffn2_norm, l2_ffn2_wi, l2_ffn2_wo, l3_ffn1_norm, l3_ffn1_wi, l3_ffn1_wo, l3_mix_norm, l3_mix_win, l3_gmlp_vnorm, l3_gmlp_ws, l3_gmlp_bs, l3_mix_wout, l3_ffn2_norm, l3_ffn2_wi, l3_ffn2_wo, final_norm, loss_target, m_l0_ffn1_norm, m_l0_ffn1_wi, m_l0_ffn1_wo, m_l0_mix_norm, m_l0_mix_win, m_l0_gmlp_vnorm, m_l0_gmlp_ws, m_l0_gmlp_bs, m_l0_mix_wout, m_l0_ffn2_norm, m_l0_ffn2_wi, m_l0_ffn2_wo, m_l1_ffn1_norm, m_l1_ffn1_wi, m_l1_ffn1_wo, m_l1_mix_norm, m_l1_mix_win, m_l1_swa_sinks, m_l1_mix_wout, m_l1_ffn2_norm, m_l1_ffn2_wi, m_l1_ffn2_wo, m_l2_ffn1_norm, m_l2_ffn1_wi, m_l2_ffn1_wo, m_l2_mix_norm, m_l2_mix_win, m_l2_fox_bf, m_l2_mix_wout, m_l2_ffn2_norm, m_l2_ffn2_wi, m_l2_ffn2_wo, m_l3_ffn1_norm, m_l3_ffn1_wi, m_l3_ffn1_wo, m_l3_mix_norm, m_l3_mix_win, m_l3_gmlp_vnorm, m_l3_gmlp_ws, m_l3_gmlp_bs, m_l3_mix_wout, m_l3_ffn2_norm, m_l3_ffn2_wi, m_l3_ffn2_wo, m_final_norm, v_l0_ffn1_norm, v_l0_ffn1_wi, v_l0_ffn1_wo, v_l0_mix_norm, v_l0_mix_win, v_l0_gmlp_vnorm, v_l0_gmlp_ws, v_l0_gmlp_bs, v_l0_mix_wout, v_l0_ffn2_norm, v_l0_ffn2_wi, v_l0_ffn2_wo, v_l1_ffn1_norm, v_l1_ffn1_wi, v_l1_ffn1_wo, v_l1_mix_norm, v_l1_mix_win, v_l1_swa_sinks, v_l1_mix_wout, v_l1_ffn2_norm, v_l1_ffn2_wi, v_l1_ffn2_wo, v_l2_ffn1_norm, v_l2_ffn1_wi, v_l2_ffn1_wo, v_l2_mix_norm, v_l2_mix_win, v_l2_fox_bf, v_l2_mix_wout, v_l2_ffn2_norm, v_l2_ffn2_wi, v_l2_ffn2_wo, v_l3_ffn1_norm, v_l3_ffn1_wi, v_l3_ffn1_wo, v_l3_mix_norm, v_l3_mix_win, v_l3_gmlp_vnorm, v_l3_gmlp_ws, v_l3_gmlp_bs, v_l3_mix_wout, v_l3_ffn2_norm, v_l3_ffn2_wi, v_l3_ffn2_wo, v_final_norm):
    given = dict(x=x, l0_ffn1_norm=l0_ffn1_norm, l0_ffn1_wi=l0_ffn1_wi, l0_ffn1_wo=l0_ffn1_wo, l0_mix_norm=l0_mix_norm, l0_mix_win=l0_mix_win, l0_gmlp_vnorm=l0_gmlp_vnorm, l0_gmlp_ws=l0_gmlp_ws, l0_gmlp_bs=l0_gmlp_bs, l0_mix_wout=l0_mix_wout, l0_ffn2_norm=l0_ffn2_norm, l0_ffn2_wi=l0_ffn2_wi, l0_ffn2_wo=l0_ffn2_wo, l1_ffn1_norm=l1_ffn1_norm, l1_ffn1_wi=l1_ffn1_wi, l1_ffn1_wo=l1_ffn1_wo, l1_mix_norm=l1_mix_norm, l1_mix_win=l1_mix_win, l1_swa_sinks=l1_swa_sinks, l1_mix_wout=l1_mix_wout, l1_ffn2_norm=l1_ffn2_norm, l1_ffn2_wi=l1_ffn2_wi, l1_ffn2_wo=l1_ffn2_wo, l2_ffn1_norm=l2_ffn1_norm, l2_ffn1_wi=l2_ffn1_wi, l2_ffn1_wo=l2_ffn1_wo, l2_mix_norm=l2_mix_norm, l2_mix_win=l2_mix_win, l2_fox_bf=l2_fox_bf, l2_mix_wout=l2_mix_wout, l2_ffn2_norm=l2_ffn2_norm, l2_ffn2_wi=l2_ffn2_wi, l2_ffn2_wo=l2_ffn2_wo, l3_ffn1_norm=l3_ffn1_norm, l3_ffn1_wi=l3_ffn1_wi, l3_ffn1_wo=l3_ffn1_wo, l3_mix_norm=l3_mix_norm, l3_mix_win=l3_mix_win, l3_gmlp_vnorm=l3_gmlp_vnorm, l3_gmlp_ws=l3_gmlp_ws, l3_gmlp_bs=l3_gmlp_bs, l3_mix_wout=l3_mix_wout, l3_ffn2_norm=l3_ffn2_norm, l3_ffn2_wi=l3_ffn2_wi, l3_ffn2_wo=l3_ffn2_wo, final_norm=final_norm, loss_target=loss_target, m_l0_ffn1_norm=m_l0_ffn1_norm, m_l0_ffn1_wi=m_l0_ffn1_wi, m_l0_ffn1_wo=m_l0_ffn1_wo, m_l0_mix_norm=m_l0_mix_norm, m_l0_mix_win=m_l0_mix_win, m_l0_gmlp_vnorm=m_l0_gmlp_vnorm, m_l0_gmlp_ws=m_l0_gmlp_ws, m_l0_gmlp_bs=m_l0_gmlp_bs, m_l0_mix_wout=m_l0_mix_wout, m_l0_ffn2_norm=m_l0_ffn2_norm, m_l0_ffn2_wi=m_l0_ffn2_wi, m_l0_ffn2_wo=m_l0_ffn2_wo, m_l1_ffn1_norm=m_l1_ffn1_norm, m_l1_ffn1_wi=m_l1_ffn1_wi, m_l1_ffn1_wo=m_l1_ffn1_wo, m_l1_mix_norm=m_l1_mix_norm, m_l1_mix_win=m_l1_mix_win, m_l1_swa_sinks=m_l1_swa_sinks, m_l1_mix_wout=m_l1_mix_wout, m_l1_ffn2_norm=m_l1_ffn2_norm, m_l1_ffn2_wi=m_l1_ffn2_wi, m_l1_ffn2_wo=m_l1_ffn2_wo, m_l2_ffn1_norm=m_l2_ffn1_norm, m_l2_ffn1_wi=m_l2_ffn1_wi, m_l2_ffn1_wo=m_l2_ffn1_wo, m_l2_mix_norm=m_l2_mix_norm, m_l2_mix_win=m_l2_mix_win, m_l2_fox_bf=m_l2_fox_bf, m_l2_mix_wout=m_l2_mix_wout, m_l2_ffn2_norm=m_l2_ffn2_norm, m_l2_ffn2_wi=m_l2_ffn2_wi, m_l2_ffn2_wo=m_l2_ffn2_wo, m_l3_ffn1_norm=m_l3_ffn1_norm, m_l3_ffn1_wi=m_l3_ffn1_wi, m_l3_ffn1_wo=m_l3_ffn1_wo, m_l3_mix_norm=m_l3_mix_norm, m_l3_mix_win=m_l3_mix_win, m_l3_gmlp_vnorm=m_l3_gmlp_vnorm, m_l3_gmlp_ws=m_l3_gmlp_ws, m_l3_gmlp_bs=m_l3_gmlp_bs, m_l3_mix_wout=m_l3_mix_wout, m_l3_ffn2_norm=m_l3_ffn2_norm, m_l3_ffn2_wi=m_l3_ffn2_wi, m_l3_ffn2_wo=m_l3_ffn2_wo, m_final_norm=m_final_norm, v_l0_ffn1_norm=v_l0_ffn1_norm, v_l0_ffn1_wi=v_l0_ffn1_wi, v_l0_ffn1_wo=v_l0_ffn1_wo, v_l0_mix_norm=v_l0_mix_norm, v_l0_mix_win=v_l0_mix_win, v_l0_gmlp_vnorm=v_l0_gmlp_vnorm, v_l0_gmlp_ws=v_l0_gmlp_ws, v_l0_gmlp_bs=v_l0_gmlp_bs, v_l0_mix_wout=v_l0_mix_wout, v_l0_ffn2_norm=v_l0_ffn2_norm, v_l0_ffn2_wi=v_l0_ffn2_wi, v_l0_ffn2_wo=v_l0_ffn2_wo, v_l1_ffn1_norm=v_l1_ffn1_norm, v_l1_ffn1_wi=v_l1_ffn1_wi, v_l1_ffn1_wo=v_l1_ffn1_wo, v_l1_mix_norm=v_l1_mix_norm, v_l1_mix_win=v_l1_mix_win, v_l1_swa_sinks=v_l1_swa_sinks, v_l1_mix_wout=v_l1_mix_wout, v_l1_ffn2_norm=v_l1_ffn2_norm, v_l1_ffn2_wi=v_l1_ffn2_wi, v_l1_ffn2_wo=v_l1_ffn2_wo, v_l2_ffn1_norm=v_l2_ffn1_norm, v_l2_ffn1_wi=v_l2_ffn1_wi, v_l2_ffn1_wo=v_l2_ffn1_wo, v_l2_mix_norm=v_l2_mix_norm, v_l2_mix_win=v_l2_mix_win, v_l2_fox_bf=v_l2_fox_bf, v_l2_mix_wout=v_l2_mix_wout, v_l2_ffn2_norm=v_l2_ffn2_norm, v_l2_ffn2_wi=v_l2_ffn2_wi, v_l2_ffn2_wo=v_l2_ffn2_wo, v_l3_ffn1_norm=v_l3_ffn1_norm, v_l3_ffn1_wi=v_l3_ffn1_wi, v_l3_ffn1_wo=v_l3_ffn1_wo, v_l3_mix_norm=v_l3_mix_norm, v_l3_mix_win=v_l3_mix_win, v_l3_gmlp_vnorm=v_l3_gmlp_vnorm, v_l3_gmlp_ws=v_l3_gmlp_ws, v_l3_gmlp_bs=v_l3_gmlp_bs, v_l3_mix_wout=v_l3_mix_wout, v_l3_ffn2_norm=v_l3_ffn2_norm, v_l3_ffn2_wi=v_l3_ffn2_wi, v_l3_ffn2_wo=v_l3_ffn2_wo, v_final_norm=v_final_norm)
    weights = {n: given[n] for n in TWIN_WEIGHTS}
    shared = {n: given[n] for n in SHARED_INPUTS}
    per_example = {n: given[n] for n in ['x']}
    grad_fn = _jax.value_and_grad(_loss, argnums=(0, 1))

    def one_microbatch(ex, loss_target):
        ex = dict(ex)
        diff = ex.pop(TWIN_DIFF_INPUT)
        return grad_fn(weights, diff, {**shared, **ex}, loss_target)

    if N_MICROBATCH == 1:
        loss, (grad_w, grad_x) = one_microbatch(per_example, given["loss_target"])
    else:
        def body(carry, xs):
            loss_sum, grad_sum = carry
            l_k, (gw_k, gx_k) = one_microbatch(xs[0], xs[1])
            with _jax.named_scope("update"):
                return (loss_sum + l_k, _jax.tree.map(_jnp.add, grad_sum, gw_k)), gx_k

        init = (_jnp.zeros((), _jnp.float32), _jax.tree.map(_jnp.zeros_like, weights))
        (loss, grad_w), grad_x = _jax.lax.scan(body, init, (per_example, given["loss_target"]))
    with _jax.named_scope("update"):
        delta_w, new_m, new_v = {}, {}, {}
        for n in TWIN_WEIGHTS:
            delta_w[n], new_m[n], new_v[n] = _adamw(weights[n], grad_w[n], given["m_" + n], given["v_" + n])
    return (loss, grad_x, *[grad_w[n] for n in TWIN_WEIGHTS], *[delta_w[n] for n in TWIN_WEIGHTS],
            *[new_m[n] for n in TWIN_WEIGHTS], *[new_v[n] for n in TWIN_WEIGHTS])
```

```python
import functools
import math

import jax
import jax.numpy as jnp
from jax import lax
from jax.experimental import pallas as pl
from jax.experimental.pallas import tpu as pltpu

F32 = jnp.float32
BF16 = jnp.bfloat16

NORM_EPS = 1e-5
NEG_INF = -1e30
BLOCK = 128
GMLP_GROUPS = 16
SWA_HEAD_DIM = 64
SWA_GROUP = 8
ROPE_DIM = SWA_HEAD_DIM // 4
ROPE_THETA = 500000.0
FOX_HEAD_DIM = 128
ADAM_LR = 0.001
ADAM_B1 = 0.9
ADAM_B2 = 0.999
ADAM_EPS = 1e-08
ADAM_WD = 0.01
ADAM_STEP = 10
N_CHIPS = 4
N_DEV = 8
LANES = 128
VMEM_LIMIT = 56 * 1024 * 1024
MESH = pl.DeviceIdType.MESH
HBM = pl.BlockSpec(memory_space=pltpu.HBM)

MM_TILES = (1024, 1408, 896, 640, 512, 384, 256, 128)


def _pick(n, prefs):
    for p in prefs:
        if p <= n and n % p == 0:
            return p
    return n


def _params(*sem):
    return pltpu.CompilerParams(dimension_semantics=sem or None, vmem_limit_bytes=VMEM_LIMIT)


def _cols(arr):
    return arr.shape[-1] * (arr.shape[0] if arr.ndim == 3 else 1)


def _mat_spec(arr, rb, cb, ridx, cidx):
    if arr.ndim == 2:
        return pl.BlockSpec((rb, cb), lambda j, i, k: (ridx(j, i, k), cidx(j, i, k)))
    per = arr.shape[2] // cb
    return pl.BlockSpec((None, rb, cb),
                        lambda j, i, k: (cidx(j, i, k) // per, ridx(j, i, k), cidx(j, i, k) % per))


def _matmul(a, b, *, name, out_dtype, ta=False, tb=False, out_shards=1, scale=1.0, resid=None):
    m_dim, k_dim = (a.shape[1], a.shape[0]) if ta else a.shape
    n_dim = b.shape[-2] if tb else _cols(b)
    assert k_dim == (_cols(b) if tb else b.shape[-2]), (a.shape, b.shape, ta, tb)
    n_unit = n_dim // out_shards
    if b.ndim == 3 and not tb:
        n_unit = math.gcd(n_unit, b.shape[2])
    k_unit = b.shape[2] if (b.ndim == 3 and tb) else k_dim
    bm = _pick(m_dim, MM_TILES)
    bn = _pick(n_unit, MM_TILES)
    bk = k_unit if k_unit <= 2048 else _pick(k_unit, MM_TILES)
    nk = k_dim // bk
    i_of, j_of, k_of = (lambda j, i, k: i), (lambda j, i, k: j), (lambda j, i, k: k)
    a_spec = _mat_spec(a, bk, bm, k_of, i_of) if ta else _mat_spec(a, bm, bk, i_of, k_of)
    b_spec = _mat_spec(b, bn, bk, j_of, k_of) if tb else _mat_spec(b, bk, bn, k_of, j_of)
    out_shape = (m_dim, n_dim) if out_shards == 1 else (out_shards, m_dim, n_dim // out_shards)
    out = jax.ShapeDtypeStruct(out_shape, out_dtype)
    o_spec = _mat_spec(out, bm, bn, i_of, j_of)
    dims = (((0 if ta else 1,), (1 if tb else 0,)), ((), ()))
    operands, in_specs = [a, b], [a_spec, b_spec]
    if resid is not None:
        operands.append(resid)
        in_specs.append(_mat_spec(resid, bm, bn, i_of, j_of))

    def body(*refs):
        a_ref, b_ref = refs[0], refs[1]
        r_ref = refs[2] if resid is not None else None
        o_ref = refs[3] if resid is not None else refs[2]
        part = lax.dot_general(a_ref[...].astype(BF16), b_ref[...].astype(BF16), dims,
                               preferred_element_type=F32)

        def finish(acc):
            val = acc * scale if scale != 1.0 else acc
            if r_ref is not None:
                val = r_ref[...] + val
            o_ref[...] = val.astype(o_ref.dtype)

        if nk == 1:
            finish(part)
        else:
            acc_ref = refs[-1]
            k = pl.program_id(2)

            @pl.when(k == 0)
            def _():
                acc_ref[...] = part

            @pl.when(k > 0)
            def _():
                acc_ref[...] += part

            @pl.when(k == nk - 1)
            def _():
                finish(acc_ref[...])

    return pl.pallas_call(
        body, name=name, grid=(n_dim // bn, m_dim // bm, nk),
        in_specs=in_specs, out_specs=o_spec, out_shape=out,
        scratch_shapes=[pltpu.VMEM((bm, bn), F32)] if nk > 1 else [],
        compiler_params=_params("parallel", "parallel", "arbitrary"),
    )(*operands)


def _row_block(rows, width, itemsize=4, budget=2 << 20):
    for br in (512, 256, 128, 64, 32, 16, 8):
        if rows % br == 0 and br * width * itemsize <= budget:
            return br
    return rows


def _cast_bf16(w, name):
    rows, width = w.shape
    br = _row_block(rows, width)

    def body(w_ref, o_ref):
        o_ref[...] = w_ref[...].astype(BF16)

    spec = pl.BlockSpec((br, width), lambda i: (i, 0))
    return pl.pallas_call(body, name=name, grid=(rows // br,), in_specs=[spec], out_specs=spec,
                          out_shape=jax.ShapeDtypeStruct(w.shape, BF16),
                          compiler_params=_params("parallel"))(w)


def _rms_fwd(h, g, name):
    s_len, d = h.shape
    br = _row_block(s_len, d)

    def body(h_ref, g_ref, o_ref):
        x = h_ref[...]
        r = lax.rsqrt(jnp.mean(x * x, axis=-1, keepdims=True) + NORM_EPS)
        o_ref[...] = (x * r * g_ref[...]).astype(BF16)

    spec = pl.BlockSpec((br, d), lambda i: (i, 0))
    return pl.pallas_call(body, name=name, grid=(s_len // br,),
                          in_specs=[spec, pl.BlockSpec((1, d), lambda i: (0, 0))], out_specs=spec,
                          out_shape=jax.ShapeDtypeStruct((s_len, d), BF16),
                          compiler_params=_params("parallel"))(h, g.reshape(1, d))


def _rms_bwd_rows(x, g, dn):
    r = lax.rsqrt(jnp.mean(x * x, axis=-1, keepdims=True) + NORM_EPS)
    xhat = x * r
    gdn = dn * g
    dx = r * (gdn - xhat * jnp.mean(gdn * xhat, axis=-1, keepdims=True))
    return dx, dn * xhat


def _norm_bwd(h, g, dn, dres, name):
    s_len, d = h.shape
    br = _row_block(s_len, d, budget=1 << 20)

    def body(h_ref, g_ref, dn_ref, dres_ref, dh_ref, dg_ref):
        dx, dg_rows = _rms_bwd_rows(h_ref[...], g_ref[...], dn_ref[...].astype(F32))
        dh_ref[...] = dres_ref[...] + dx

        @pl.when(pl.program_id(0) == 0)
        def _():
            dg_ref[...] = jnp.zeros_like(dg_ref)

        dg_ref[...] += jnp.sum(dg_rows, axis=0, keepdims=True)

    spec = pl.BlockSpec((br, d), lambda i: (i, 0))
    vec = pl.BlockSpec((1, d), lambda i: (0, 0))
    return pl.pallas_call(body, name=name, grid=(s_len // br,),
                          in_specs=[spec, vec, spec, spec], out_specs=[spec, vec],
                          out_shape=[jax.ShapeDtypeStruct((s_len, d), F32),
                                     jax.ShapeDtypeStruct((1, d), F32)],
                          compiler_params=_params("arbitrary"))(h, g.reshape(1, d), dn, dres)


def _sigmoid(x):
    return 1.0 / (1.0 + jnp.exp(-x))


def _swiglu_fwd(z, name):
    s_len, f2 = z.shape
    f = f2 // 2
    br = _row_block(s_len, f2, budget=3 << 20)

    def body(z_ref, a_ref):
        gate = z_ref[:, :f].astype(F32)
        up = z_ref[:, f:].astype(F32)
        a_ref[...] = (gate * _sigmoid(gate) * up).astype(BF16)

    return pl.pallas_call(body, name=name, grid=(s_len // br,),
                          in_specs=[pl.BlockSpec((br, f2), lambda i: (i, 0))],
                          out_specs=pl.BlockSpec((br, f), lambda i: (i, 0)),
                          out_shape=jax.ShapeDtypeStruct((s_len, f), BF16),
                          compiler_params=_params("parallel"))(z)


def _swiglu_bwd(z, da, name):
    s_len, f2 = z.shape
    f = f2 // 2
    br = _row_block(s_len, f2, budget=3 << 20)

    def body(z_ref, da_ref, dz_ref):
        gate = z_ref[:, :f].astype(F32)
        up = z_ref[:, f:].astype(F32)
        d = da_ref[...].astype(F32)
        sig = _sigmoid(gate)
        dz_ref[:, :f] = (d * up * (sig * (1.0 + gate * (1.0 - sig)))).astype(BF16)
        dz_ref[:, f:] = (d * gate * sig).astype(BF16)

    return pl.pallas_call(body, name=name, grid=(s_len // br,),
                          in_specs=[pl.BlockSpec((br, f2), lambda i: (i, 0)),
                                    pl.BlockSpec((br, f), lambda i: (i, 0))],
                          out_specs=pl.BlockSpec((br, f2), lambda i: (i, 0)),
                          out_shape=jax.ShapeDtypeStruct((s_len, f2), BF16),
                          compiler_params=_params("parallel"))(z, da)


def _loss_head(h, g, target, name):
    s_len, d = h.shape
    br = _row_block(s_len, d, budget=1 << 20)

    def body(h_ref, g_ref, t_ref, loss_ref, dh_ref, dg_ref):
        x = h_ref[...]
        gain = g_ref[...]
        r = lax.rsqrt(jnp.mean(x * x, axis=-1, keepdims=True) + NORM_EPS)
        err = x * r * gain - t_ref[...]
        part = 0.5 * jnp.sum(jnp.mean(err * err, axis=-1, keepdims=True), axis=0, keepdims=True)
        dx, dg_rows = _rms_bwd_rows(x, gain, err * (1.0 / d))
        dh_ref[...] = dx

        @pl.when(pl.program_id(0) == 0)
        def _():
            dg_ref[...] = jnp.zeros_like(dg_ref)
            loss_ref[...] = jnp.zeros_like(loss_ref)

        dg_ref[...] += jnp.sum(dg_rows, axis=0, keepdims=True)
        loss_ref[...] += jnp.broadcast_to(part, loss_ref.shape)

    spec = pl.BlockSpec((br, d), lambda i: (i, 0))
    vec = pl.BlockSpec((1, d), lambda i: (0, 0))
    one = pl.BlockSpec((1, LANES), lambda i: (0, 0))
    loss, dh, dg = pl.pallas_call(
        body, name=name, grid=(s_len // br,), in_specs=[spec, vec, spec],
        out_specs=[one, spec, vec],
        out_shape=[jax.ShapeDtypeStruct((1, LANES), F32), jax.ShapeDtypeStruct((s_len, d), F32),
                   jax.ShapeDtypeStruct((1, d), F32)],
        compiler_params=_params("arbitrary"))(h, g.reshape(1, d), target)
    return loss[0, 0], dh, dg


def _adamw(w, g, m, v, name):
    rows, width = w.shape
    br = _row_block(rows, width, budget=1 << 20)
    c1 = 1.0 - ADAM_B1 ** ADAM_STEP
    c2 = 1.0 - ADAM_B2 ** ADAM_STEP

    def body(w_ref, g_ref, m_ref, v_ref, d_ref, nm_ref, nv_ref):
        grad = g_ref[...]
        new_m = ADAM_B1 * m_ref[...] + (1.0 - ADAM_B1) * grad
        new_v = ADAM_B2 * v_ref[...] + (1.0 - ADAM_B2) * (grad * grad)
        d_ref[...] = -ADAM_LR * ((new_m / c1) / (jnp.sqrt(new_v / c2) + ADAM_EPS) + ADAM_WD * w_ref[...])
        nm_ref[...] = new_m
        nv_ref[...] = new_v

    spec = pl.BlockSpec((br, width), lambda i: (i, 0))
    shp = jax.ShapeDtypeStruct(w.shape, F32)
    return pl.pallas_call(body, name=name, grid=(rows // br,), in_specs=[spec] * 4,
                          out_specs=[spec] * 3, out_shape=[shp] * 3,
                          compiler_params=_params("parallel"))(w, g, m, v)


def _gelu(x):
    return 0.5 * x * (1.0 + lax.erf(x * (2.0 ** -0.5)))


def _gelu_grad(x):
    return 0.5 * (1.0 + lax.erf(x * (2.0 ** -0.5))) + x * jnp.exp(-0.5 * x * x) * ((2.0 * math.pi) ** -0.5)


def _tril_mask():
    row = lax.broadcasted_iota(jnp.int32, (BLOCK, BLOCK), 0)
    col = lax.broadcasted_iota(jnp.int32, (BLOCK, BLOCK), 1)
    return col <= row


def _gmlp_specs(s_len, d):
    gw = d // GMLP_GROUPS
    zp = pl.BlockSpec((BLOCK, 2 * d), lambda i: (i, 0))
    row = pl.BlockSpec((BLOCK, d), lambda i: (i, 0))
    vec = pl.BlockSpec((1, d), lambda i: (0, 0))
    ws = pl.BlockSpec((GMLP_GROUPS, BLOCK, BLOCK), lambda i: (0, 0, 0))
    bst = pl.BlockSpec((BLOCK, GMLP_GROUPS), lambda i: (0, 0))
    return gw, zp, row, vec, ws, bst


def _gmlp_fwd(zp, vgain, ws, bs, name):
    s_len, d2 = zp.shape
    d = d2 // 2
    gw, zp_spec, row_spec, vec_spec, ws_spec, bst_spec = _gmlp_specs(s_len, d)

    def body(zp_ref, vg_ref, ws_ref, bst_ref, y_ref):
        u = _gelu(zp_ref[:, :d].astype(F32))
        vv = _gelu(zp_ref[:, d:].astype(F32))
        r = lax.rsqrt(jnp.mean(vv * vv, axis=-1, keepdims=True) + NORM_EPS)
        vn = (vv * r * vg_ref[...]).astype(BF16)
        mask = _tril_mask()
        for g in range(GMLP_GROUPS):
            cols = slice(g * gw, (g + 1) * gw)
            wg = jnp.where(mask, ws_ref[g], 0.0).astype(BF16)
            mixed = jnp.dot(wg, vn[:, cols], preferred_element_type=F32) + bst_ref[:, g:g + 1]
            y_ref[:, cols] = (u[:, cols] * mixed).astype(BF16)

    return pl.pallas_call(body, name=name, grid=(s_len // BLOCK,),
                          in_specs=[zp_spec, vec_spec, ws_spec, bst_spec], out_specs=row_spec,
                          out_shape=jax.ShapeDtypeStruct((s_len, d), BF16),
                          compiler_params=_params("parallel"))(zp, vgain.reshape(1, d), ws, bs.T)


def _gmlp_bwd(zp, dy, vgain, ws, bs, name):
    s_len, d2 = zp.shape
    d = d2 // 2
    gw, zp_spec, row_spec, vec_spec, ws_spec, bst_spec = _gmlp_specs(s_len, d)

    def body(zp_ref, dy_ref, vg_ref, ws_ref, bst_ref, dzp_ref, dws_ref, dbst_ref, dvg_ref, dvn_ref):
        @pl.when(pl.program_id(0) == 0)
        def _():
            dws_ref[...] = jnp.zeros_like(dws_ref)
            dbst_ref[...] = jnp.zeros_like(dbst_ref)
            dvg_ref[...] = jnp.zeros_like(dvg_ref)

        zu = zp_ref[:, :d].astype(F32)
        zv = zp_ref[:, d:].astype(F32)
        u = _gelu(zu)
        vv = _gelu(zv)
        r = lax.rsqrt(jnp.mean(vv * vv, axis=-1, keepdims=True) + NORM_EPS)
        vhat = vv * r
        gain = vg_ref[...]
        vn = (vhat * gain).astype(BF16)
        dyf = dy_ref[...].astype(F32)
        dmixed = dyf * u
        dmixed_b = dmixed.astype(BF16)
        mask = _tril_mask()
        lane = lax.broadcasted_iota(jnp.int32, (BLOCK, GMLP_GROUPS), 1)
        dbs_step = jnp.zeros((BLOCK, GMLP_GROUPS), F32)
        for g in range(GMLP_GROUPS):
            cols = slice(g * gw, (g + 1) * gw)
            wg = jnp.where(mask, ws_ref[g], 0.0).astype(BF16)
            mixed = jnp.dot(wg, vn[:, cols], preferred_element_type=F32) + bst_ref[:, g:g + 1]
            dzp_ref[:, cols] = (dyf[:, cols] * mixed * _gelu_grad(zu[:, cols])).astype(BF16)
            dm = dmixed_b[:, cols]
            dw = lax.dot_general(dm, vn[:, cols], (((1,), (1,)), ((), ())), preferred_element_type=F32)
            dws_ref[g] += jnp.where(mask, dw, 0.0)
            dbs_step = dbs_step + jnp.where(lane == g, jnp.sum(dmixed[:, cols], axis=-1, keepdims=True), 0.0)
            dvn_ref[:, cols] = lax.dot_general(wg, dm, (((0,), (0,)), ((), ())), preferred_element_type=F32)
        dbst_ref[...] += dbs_step
        dvn = dvn_ref[...]
        dvg_ref[...] += jnp.sum(dvn * vhat, axis=0, keepdims=True)
        dvhat = dvn * gain
        dvv = r * (dvhat - vhat * jnp.mean(dvhat * vhat, axis=-1, keepdims=True))
        dzp_ref[:, d:] = (dvv * _gelu_grad(zv)).astype(BF16)

    return pl.pallas_call(
        body, name=name, grid=(s_len // BLOCK,),
        in_specs=[zp_spec, row_spec, vec_spec, ws_spec, bst_spec],
        out_specs=[zp_spec, ws_spec, bst_spec, vec_spec],
        out_shape=[jax.ShapeDtypeStruct((s_len, d2), BF16), jax.ShapeDtypeStruct(ws.shape, F32),
                   jax.ShapeDtypeStruct((BLOCK, GMLP_GROUPS), F32), jax.ShapeDtypeStruct((1, d), F32)],
        scratch_shapes=[pltpu.VMEM((BLOCK, d), F32)],
        compiler_params=_params("arbitrary"))(zp, dy, vgain.reshape(1, d), ws, bs.T)


def _rope_tables(s_len, sign):
    half = ROPE_DIM // 2
    inv_freq = ROPE_THETA ** (-(jnp.arange(half, dtype=F32) * 2.0 / ROPE_DIM))
    ang = jnp.arange(s_len, dtype=F32)[:, None] * inv_freq[None, :]
    cos, sin = jnp.cos(ang), jnp.sin(ang) * sign
    pad = jnp.zeros((s_len, SWA_HEAD_DIM - ROPE_DIM), F32)
    zero = jnp.zeros_like(sin)
    cos_t = jnp.concatenate([cos, cos, pad + 1.0], axis=1)
    sin_up = jnp.concatenate([-sin, zero, pad], axis=1)
    sin_dn = jnp.concatenate([zero, sin, pad], axis=1)
    return [jnp.tile(t, (1, LANES // SWA_HEAD_DIM)) for t in (cos_t, sin_up, sin_dn)]


def _rotate(x, cos_t, sin_up, sin_dn):
    width = x.shape[-1]
    half = ROPE_DIM // 2
    reps = width // cos_t.shape[-1]
    if reps > 1:
        cos_t, sin_up, sin_dn = (jnp.tile(t, (1, reps)) for t in (cos_t, sin_up, sin_dn))
    elif reps == 0:
        cos_t, sin_up, sin_dn = (t[:, :width] for t in (cos_t, sin_up, sin_dn))
    return x * cos_t + pltpu.roll(x, width - half, 1) * sin_up + pltpu.roll(x, half, 1) * sin_dn


def _rope_fwd(qkv, name):
    s_len, total = qkv.shape
    wkv = total // (SWA_GROUP + 2)
    wq = SWA_GROUP * wkv
    br = _row_block(s_len, total, budget=2 << 20)
    tables = _rope_tables(s_len, 1.0)

    def body(q_ref, k_ref, v_ref, c_ref, su_ref, sd_ref, qo_ref, ko_ref, vo_ref):
        t = (c_ref[...], su_ref[...], sd_ref[...])
        qo_ref[...] = _rotate(q_ref[...], *t).astype(BF16)
        ko_ref[...] = _rotate(k_ref[...], *t).astype(BF16)
        vo_ref[...] = v_ref[...].astype(BF16)

    qs = pl.BlockSpec((br, wq), lambda i: (i, 0))
    ks = pl.BlockSpec((br, wkv), lambda i: (i, SWA_GROUP))
    vs = pl.BlockSpec((br, wkv), lambda i: (i, SWA_GROUP + 1))
    ts = pl.BlockSpec((br, LANES), lambda i: (i, 0))
    kv_out = pl.BlockSpec((br, wkv), lambda i: (i, 0))
    return pl.pallas_call(
        body, name=name, grid=(s_len // br,), in_specs=[qs, ks, vs, ts, ts, ts],
        out_specs=[qs, kv_out, kv_out],
        out_shape=[jax.ShapeDtypeStruct((s_len, wq), BF16), jax.ShapeDtypeStruct((s_len, wkv), BF16),
                   jax.ShapeDtypeStruct((s_len, wkv), BF16)],
        compiler_params=_params("parallel"))(qkv, qkv, qkv, *tables)


def _rope_bwd(dq, dk, dv, name):
    s_len, wq = dq.shape
    wkv = dk.shape[1]
    br = _row_block(s_len, wq + 2 * wkv, budget=2 << 20)
    tables = _rope_tables(s_len, -1.0)

    def body(q_ref, k_ref, v_ref, c_ref, su_ref, sd_ref, o_ref):
        t = (c_ref[...], su_ref[...], sd_ref[...])
        o_ref[:, :wq] = _rotate(q_ref[...], *t).astype(BF16)
        o_ref[:, wq:wq + wkv] = _rotate(k_ref[...], *t).astype(BF16)
        o_ref[:, wq + wkv:] = v_ref[...].astype(BF16)

    qs = pl.BlockSpec((br, wq), lambda i: (i, 0))
    kvs = pl.BlockSpec((br, wkv), lambda i: (i, 0))
    ts = pl.BlockSpec((br, LANES), lambda i: (i, 0))
    return pl.pallas_call(
        body, name=name, grid=(s_len // br,), in_specs=[qs, kvs, kvs, ts, ts, ts],
        out_specs=pl.BlockSpec((br, wq + 2 * wkv), lambda i: (i, 0)),
        out_shape=jax.ShapeDtypeStruct((s_len, wq + 2 * wkv), BF16),
        compiler_params=_params("parallel"))(dq, dk, dv, *tables)


def _swa_valid(i):
    row = lax.broadcasted_iota(jnp.int32, (BLOCK, 2 * BLOCK), 0)
    col = lax.broadcasted_iota(jnp.int32, (BLOCK, 2 * BLOCK), 1)
    return (col - BLOCK <= row) & (row < col) & ((col >= BLOCK) | (i > 0))


def _swa_specs(wq, wkv):
    q_spec = pl.BlockSpec((BLOCK, wq), lambda i: (i, 0))
    cur = pl.BlockSpec((BLOCK, wkv), lambda i: (i, 0))
    prev = pl.BlockSpec((BLOCK, wkv), lambda i: (jnp.maximum(i - 1, 0), 0))
    sink = pl.BlockSpec(memory_space=pltpu.SMEM)
    return q_spec, cur, prev, sink


def _swa_probs(q_h, k_cat, valid, sink):
    s = lax.dot_general(q_h, k_cat, (((1,), (1,)), ((), ())), preferred_element_type=F32)
    s = jnp.where(valid, s * (SWA_HEAD_DIM ** -0.5), NEG_INF)
    m = jnp.maximum(jnp.max(s, axis=-1, keepdims=True), sink)
    p = jnp.exp(s - m)
    e_sink = jnp.exp(sink - m)
    denom = jnp.sum(p, axis=-1, keepdims=True) + e_sink
    return p / denom, e_sink / denom


def _swa_fwd(q, k, v, sinks, name):
    s_len, wq = q.shape
    wkv = k.shape[1]
    hd = SWA_HEAD_DIM
    q_spec, cur, prev, sink_spec = _swa_specs(wq, wkv)

    def body(q_ref, kc_ref, kp_ref, vc_ref, vp_ref, sink_ref, o_ref):
        valid = _swa_valid(pl.program_id(0))
        for j in range(wkv // hd):
            lanes = slice(j * hd, (j + 1) * hd)
            k_cat = jnp.concatenate([kp_ref[:, lanes], kc_ref[:, lanes]], axis=0)
            v_cat = jnp.concatenate([vp_ref[:, lanes], vc_ref[:, lanes]], axis=0)
            for hh in range(SWA_GROUP):
                h = j * SWA_GROUP + hh
                pn, _ = _swa_probs(q_ref[:, h * hd:(h + 1) * hd], k_cat, valid, sink_ref[h])
                o_ref[:, h * hd:(h + 1) * hd] = jnp.dot(
                    pn.astype(BF16), v_cat, preferred_element_type=F32).astype(BF16)

    return pl.pallas_call(body, name=name, grid=(s_len // BLOCK,),
                          in_specs=[q_spec, cur, prev, cur, prev, sink_spec], out_specs=q_spec,
                          out_shape=jax.ShapeDtypeStruct((s_len, wq), BF16),
                          compiler_params=_params("parallel"))(q, k, k, v, v, sinks)


def _swa_bwd(q, k, v, sinks, do, name):
    s_len, wq = q.shape
    wkv = k.shape[1]
    hd = SWA_HEAD_DIM
    q_spec, cur, prev, sink_spec = _swa_specs(wq, wkv)
    full = pl.BlockSpec((s_len, wkv), lambda i: (0, 0))
    one = pl.BlockSpec((1, LANES), lambda i: (0, 0))
    scale = hd ** -0.5

    def body(q_ref, kc_ref, kp_ref, vc_ref, vp_ref, sink_ref, do_ref, dq_ref, dk_ref, dv_ref, ds_ref):
        i = pl.program_id(0)

        @pl.when(i == 0)
        def _():
            dk_ref[...] = jnp.zeros_like(dk_ref)
            dv_ref[...] = jnp.zeros_like(dv_ref)
            ds_ref[...] = jnp.zeros_like(ds_ref)

        valid = _swa_valid(i)
        lane = lax.broadcasted_iota(jnp.int32, (1, LANES), 1)
        dsink_step = jnp.zeros((1, LANES), F32)
        rows_prev = pl.ds(pl.multiple_of(jnp.maximum(i - 1, 0) * BLOCK, BLOCK), BLOCK)
        rows_cur = pl.ds(pl.multiple_of(i * BLOCK, BLOCK), BLOCK)
        for j in range(wkv // hd):
            lanes = slice(j * hd, (j + 1) * hd)
            k_cat = jnp.concatenate([kp_ref[:, lanes], kc_ref[:, lanes]], axis=0)
            v_cat = jnp.concatenate([vp_ref[:, lanes], vc_ref[:, lanes]], axis=0)
            dk_cat = jnp.zeros((2 * BLOCK, hd), F32)
            dv_cat = jnp.zeros((2 * BLOCK, hd), F32)
            for hh in range(SWA_GROUP):
                h = j * SWA_GROUP + hh
                q_h = q_ref[:, h * hd:(h + 1) * hd]
                do_h = do_ref[:, h * hd:(h + 1) * hd]
                pn, p_sink = _swa_probs(q_h, k_cat, valid, sink_ref[h])
                dpn = lax.dot_general(do_h, v_cat, (((1,), (1,)), ((), ())), preferred_element_type=F32)
                delta = jnp.sum(dpn * pn, axis=-1, keepdims=True)
                ds = (pn * (dpn - delta) * scale).astype(BF16)
                dsink_h = -jnp.sum(p_sink * delta, axis=0, keepdims=True)
                dsink_step = dsink_step + jnp.where(lane == h, dsink_h, 0.0)
                dq_ref[:, h * hd:(h + 1) * hd] = jnp.dot(ds, k_cat, preferred_element_type=F32)
                dk_cat = dk_cat + lax.dot_general(ds, q_h, (((0,), (0,)), ((), ())),
                                                  preferred_element_type=F32)
                dv_cat = dv_cat + lax.dot_general(pn.astype(BF16), do_h, (((0,), (0,)), ((), ())),
                                                  preferred_element_type=F32)
            dk_ref[rows_prev, lanes] += dk_cat[:BLOCK]
            dk_ref[rows_cur, lanes] += dk_cat[BLOCK:]
            dv_ref[rows_prev, lanes] += dv_cat[:BLOCK]
            dv_ref[rows_cur, lanes] += dv_cat[BLOCK:]
        ds_ref[...] += dsink_step

    return pl.pallas_call(
        body, name=name, grid=(s_len // BLOCK,),
        in_specs=[q_spec, cur, prev, cur, prev, sink_spec, q_spec],
        out_specs=[q_spec, full, full, one],
        out_shape=[jax.ShapeDtypeStruct((s_len, wq), F32), jax.ShapeDtypeStruct((s_len, wkv), F32),
                   jax.ShapeDtypeStruct((s_len, wkv), F32), jax.ShapeDtypeStruct((1, LANES), F32)],
        compiler_params=_params("arbitrary"))(q, k, k, v, v, sinks, do)


def _log_sigmoid(x):
    return jnp.minimum(x, 0.0) - jnp.log(1.0 + jnp.exp(-jnp.abs(x)))


def _tri_ones(lower):
    row = lax.broadcasted_iota(jnp.int32, (BLOCK, BLOCK), 0)
    col = lax.broadcasted_iota(jnp.int32, (BLOCK, BLOCK), 1)
    return jnp.where((col <= row) if lower else (col >= row), 1.0, 0.0).astype(F32)


def _fox_decay(proj, bf_row, fl_block, name):
    s_len = proj.shape[0]
    nchunk = s_len // BLOCK

    def body(fl_ref, bf_ref, dec_ref):
        tri = _tri_ones(True)
        carry = jnp.zeros((1, LANES), F32)
        for c in range(nchunk):
            rows = slice(c * BLOCK, (c + 1) * BLOCK)
            log_f = _log_sigmoid(fl_ref[rows, :] + bf_ref[...])
            loc = jnp.dot(tri, log_f, preferred_element_type=F32, precision=lax.Precision.HIGHEST) + carry
            dec_ref[rows, :] = loc
            carry = loc[BLOCK - 1:BLOCK, :]

    return pl.pallas_call(
        body, name=name, grid=(1,),
        in_specs=[pl.BlockSpec((s_len, LANES), lambda i: (0, fl_block)),
                  pl.BlockSpec((1, LANES), lambda i: (0, 0))],
        out_specs=pl.BlockSpec((s_len, LANES), lambda i: (0, 0)),
        out_shape=jax.ShapeDtypeStruct((s_len, LANES), F32),
        compiler_params=_params("arbitrary"))(proj, bf_row)


def _fox_decay_bwd(ddq, ddk, proj, bf_row, fl_block, heads, name):
    s_len = proj.shape[0]
    nchunk = s_len // BLOCK

    def body(ddq_ref, ddk_ref, fl_ref, bf_ref, dfl_ref, dbf_ref):
        tri = _tri_ones(False)
        lane_ok = lax.broadcasted_iota(jnp.int32, (BLOCK, LANES), 1) < heads
        carry = jnp.zeros((1, LANES), F32)
        dbf = jnp.zeros((1, LANES), F32)
        for c in reversed(range(nchunk)):
            rows = slice(c * BLOCK, (c + 1) * BLOCK)
            ddec = ddq_ref[rows, :] + ddk_ref[rows, :]
            dlog = jnp.dot(tri, ddec, preferred_element_type=F32, precision=lax.Precision.HIGHEST) + carry
            carry = dlog[0:1, :]
            dfl = jnp.where(lane_ok, dlog * _sigmoid(-(fl_ref[rows, :] + bf_ref[...])), 0.0)
            dfl_ref[rows, :] = dfl.astype(BF16)
            dbf = dbf + jnp.sum(dfl, axis=0, keepdims=True)
        dbf_ref[...] = dbf

    blk = pl.BlockSpec((s_len, LANES), lambda i: (0, 0))
    one = pl.BlockSpec((1, LANES), lambda i: (0, 0))
    return pl.pallas_call(
        body, name=name, grid=(1,),
        in_specs=[blk, blk, pl.BlockSpec((s_len, LANES), lambda i: (0, fl_block)), one],
        out_specs=[blk, one],
        out_shape=[jax.ShapeDtypeStruct((s_len, LANES), BF16), jax.ShapeDtypeStruct((1, LANES), F32)],
        compiler_params=_params("arbitrary"))(ddq, ddk, proj, bf_row)


def _fox_scores(q, k, decq, deck, i, bq):
    s_len = k.shape[0]
    s = lax.dot_general(q, k, (((1,), (1,)), ((), ())), preferred_element_type=F32)
    s = s * (FOX_HEAD_DIM ** -0.5) + decq - deck
    row = lax.broadcasted_iota(jnp.int32, (bq, s_len), 0) + i * bq
    col = lax.broadcasted_iota(jnp.int32, (bq, s_len), 1)
    s = jnp.where(col <= row, s, NEG_INF)
    p = jnp.exp(s - jnp.max(s, axis=-1, keepdims=True))
    return p / jnp.sum(p, axis=-1, keepdims=True)


def _fox_specs(s_len, heads, bq):
    hd = FOX_HEAD_DIM
    q_spec = pl.BlockSpec((bq, hd), lambda h, i: (i, h))
    k_spec = pl.BlockSpec((s_len, hd), lambda h, i: (0, heads + h))
    v_spec = pl.BlockSpec((s_len, hd), lambda h, i: (0, 2 * heads + h))
    dq_spec = pl.BlockSpec((None, bq, 1), lambda h, i: (h, i, 0))
    dk_spec = pl.BlockSpec((None, 1, s_len), lambda h, i: (h, 0, 0))
    return q_spec, k_spec, v_spec, dq_spec, dk_spec


def _fox_fwd(proj, decq, deck, heads, name):
    s_len = proj.shape[0]
    bq = _pick(s_len, (256, 128))
    q_spec, k_spec, v_spec, dq_spec, dk_spec = _fox_specs(s_len, heads, bq)

    def body(q_ref, k_ref, v_ref, decq_ref, deck_ref, o_ref):
        pn = _fox_scores(q_ref[...].astype(BF16), k_ref[...].astype(BF16), decq_ref[...], deck_ref[...],
                         pl.program_id(1), bq)
        o_ref[...] = jnp.dot(pn.astype(BF16), v_ref[...].astype(BF16),
                             preferred_element_type=F32).astype(BF16)

    return pl.pallas_call(body, name=name, grid=(heads, s_len // bq),
                          in_specs=[q_spec, k_spec, v_spec, dq_spec, dk_spec], out_specs=q_spec,
                          out_shape=jax.ShapeDtypeStruct((s_len, heads * FOX_HEAD_DIM), BF16),
                          compiler_params=_params("parallel", "parallel"))(proj, proj, proj, decq, deck)


def _fox_bwd(proj, decq, deck, do, heads, name):
    s_len = proj.shape[0]
    d = heads * FOX_HEAD_DIM
    bq = _pick(s_len, (256, 128))
    q_spec, k_spec, v_spec, dq_spec, dk_spec = _fox_specs(s_len, heads, bq)
    acc_spec = pl.BlockSpec((s_len, FOX_HEAD_DIM), lambda h, i: (0, h))
    scale = FOX_HEAD_DIM ** -0.5

    def body(q_ref, k_ref, v_ref, decq_ref, deck_ref, do_ref, dq_ref, dk_ref, dv_ref, ddq_ref, ddk_ref):
        i = pl.program_id(1)

        @pl.when(i == 0)
        def _():
            dk_ref[...] = jnp.zeros_like(dk_ref)
            dv_ref[...] = jnp.zeros_like(dv_ref)
            ddk_ref[...] = jnp.zeros_like(ddk_ref)

        q = q_ref[...].astype(BF16)
        k = k_ref[...].astype(BF16)
        do_b = do_ref[...]
        pn = _fox_scores(q, k, decq_ref[...], deck_ref[...], i, bq)
        dpn = lax.dot_general(do_b, v_ref[...].astype(BF16), (((1,), (1,)), ((), ())),
                              preferred_element_type=F32)
        ds = pn * (dpn - jnp.sum(dpn * pn, axis=-1, keepdims=True))
        ddq_ref[...] = jnp.sum(ds, axis=-1, keepdims=True)
        ddk_ref[...] -= jnp.sum(ds, axis=0, keepdims=True)
        ds_b = (ds * scale).astype(BF16)
        dq_ref[...] = jnp.dot(ds_b, k, preferred_element_type=F32).astype(BF16)
        dk_ref[...] += lax.dot_general(ds_b, q, (((0,), (0,)), ((), ())), preferred_element_type=F32)
        dv_ref[...] += lax.dot_general(pn.astype(BF16), do_b, (((0,), (0,)), ((), ())),
                                       preferred_element_type=F32)

    return pl.pallas_call(
        body, name=name, grid=(heads, s_len // bq),
        in_specs=[q_spec, k_spec, v_spec, dq_spec, dk_spec, q_spec],
        out_specs=[q_spec, acc_spec, acc_spec, dq_spec, dk_spec],
        out_shape=[jax.ShapeDtypeStruct((s_len, d), BF16), jax.ShapeDtypeStruct((s_len, d), F32),
                   jax.ShapeDtypeStruct((s_len, d), F32), jax.ShapeDtypeStruct((heads, s_len, 1), F32),
                   jax.ShapeDtypeStruct((heads, 1, s_len), F32)],
        compiler_params=_params("parallel", "arbitrary"))(proj, proj, proj, decq, deck, do)


def _place():
    x, y, c = lax.axis_index("x"), lax.axis_index("y"), lax.axis_index("c")
    chips = [(1 - x, y), (x, 1 - y), (1 - x, 1 - y)]
    return x, y, c, chips


def _remote(src, dst, send_sems, recv_sems, idx, to):
    return pltpu.make_async_remote_copy(src_ref=src, dst_ref=dst, send_sem=send_sems.at[idx],
                                        recv_sem=recv_sems.at[idx], device_id=to, device_id_type=MESH)


def _gather_shards(shards, name):
    n = len(shards)

    def body(*refs):
        ins, outs = refs[:n], refs[n:2 * n]
        send1, recv1, send2, recv2, local_sems = refs[2 * n:]
        x, y, c, chips = _place()
        me = 2 * x + y
        sibling = (x, y, 1 - c)
        local = [pltpu.make_async_copy(ins[t], outs[t].at[me], local_sems.at[t]) for t in range(n)]
        for cp in local:
            cp.start()
        started = []
        for t in range(n):
            half = ins[t].shape[0] // 2
            mine = pl.ds(c * half, half)
            for r, chip in enumerate(chips):
                cp = _remote(ins[t].at[mine], outs[t].at[me, mine], send1, recv1, 3 * t + r, (*chip, c))
                cp.start()
                started.append(cp)
        for t in range(n):
            half = ins[t].shape[0] // 2
            mine = pl.ds(c * half, half)
            for r, chip in enumerate(chips):
                block = outs[t].at[2 * chip[0] + chip[1], mine]
                _remote(block, block, send1, recv1, 3 * t + r, (*chip, c)).wait_recv()
                cp = _remote(block, block, send2, recv2, 3 * t + r, sibling)
                cp.start()
                started.append(cp)
        for t in range(n):
            half = ins[t].shape[0] // 2
            other = pl.ds((1 - c) * half, half)
            for r, chip in enumerate(chips):
                block = outs[t].at[2 * chip[0] + chip[1], other]
                _remote(block, block, send2, recv2, 3 * t + r, sibling).wait_recv()
        for cp in started:
            cp.wait_send()
        for cp in local:
            cp.wait()

    return pl.pallas_call(
        body, name=name, in_specs=[HBM] * n, out_specs=[HBM] * n,
        out_shape=[jax.ShapeDtypeStruct((N_CHIPS, *s.shape), s.dtype) for s in shards],
        scratch_shapes=[pltpu.SemaphoreType.DMA((3 * n,))] * 4 + [pltpu.SemaphoreType.DMA((n,))],
    )(*shards)


def _pair_split(grads, name):
    n = len(grads)

    def body(*refs):
        ins, own, got = refs[:n], refs[n:2 * n], refs[2 * n:3 * n]
        send, recv, local_sems = refs[3 * n:]
        x, y, c, _ = _place()
        copies = []
        for t in range(n):
            half = ins[t].shape[1] // 2
            cp = pltpu.make_async_copy(ins[t].at[:, pl.ds(c * half, half)], own[t], local_sems.at[t])
            cp.start()
            copies.append(cp)
            rc = _remote(ins[t].at[:, pl.ds((1 - c) * half, half)], got[t], send, recv, t, (x, y, 1 - c))
            rc.start()
            copies.append(rc)
        for cp in copies:
            cp.wait()

    halves = [jax.ShapeDtypeStruct((g.shape[0], g.shape[1] // 2, g.shape[2]), g.dtype) for g in grads]
    res = pl.pallas_call(
        body, name=name, in_specs=[HBM] * n, out_specs=[HBM] * (2 * n), out_shape=halves + halves,
        scratch_shapes=[pltpu.SemaphoreType.DMA((n,))] * 3,
    )(*grads)
    return res[:n], res[n:]


def _chip_scatter(parts, name):
    n = len(parts)

    def body(*refs):
        ins, outs = refs[:n], refs[n:2 * n]
        send, recv, local_sems = refs[2 * n:]
        x, y, c, chips = _place()
        me = 2 * x + y
        local, sent = [], []
        for t in range(n):
            cp = pltpu.make_async_copy(ins[t].at[me], outs[t].at[me], local_sems.at[t])
            cp.start()
            local.append(cp)
            for r, chip in enumerate(chips):
                rc = _remote(ins[t].at[2 * chip[0] + chip[1]], outs[t].at[me], send, recv, 3 * t + r, (*chip, c))
                rc.start()
                sent.append(rc)
        for t in range(n):
            for r, chip in enumerate(chips):
                block = outs[t].at[2 * chip[0] + chip[1]]
                _remote(block, block, send, recv, 3 * t + r, (*chip, c)).wait_recv()
        for cp in sent:
            cp.wait_send()
        for cp in local:
            cp.wait()

    return pl.pallas_call(
        body, name=name, in_specs=[HBM] * n, out_specs=[HBM] * n,
        out_shape=[jax.ShapeDtypeStruct(p.shape, p.dtype) for p in parts],
        scratch_shapes=[pltpu.SemaphoreType.DMA((3 * n,))] * 2 + [pltpu.SemaphoreType.DMA((n,))],
    )(*parts)


def _pair_join(halves, name):
    n = len(halves)

    def body(*refs):
        ins, outs = refs[:n], refs[n:2 * n]
        send, recv, local_sems = refs[2 * n:]
        x, y, c, _ = _place()
        copies = []
        for t in range(n):
            half = ins[t].shape[0]
            mine = outs[t].at[pl.ds(c * half, half)]
            cp = pltpu.make_async_copy(ins[t], mine, local_sems.at[t])
            cp.start()
            copies.append(cp)
            rc = _remote(ins[t], mine, send, recv, t, (x, y, 1 - c))
            rc.start()
            copies.append(rc)
        for cp in copies:
            cp.wait()

    return pl.pallas_call(
        body, name=name, in_specs=[HBM] * n, out_specs=[HBM] * n,
        out_shape=[jax.ShapeDtypeStruct((2 * h.shape[0], h.shape[1]), h.dtype) for h in halves],
        scratch_shapes=[pltpu.SemaphoreType.DMA((n,))] * 3,
    )(*halves)


def _add_pair(own, got, name):
    _, rows, width = own.shape
    br = _row_block(rows, width, itemsize=2, budget=1 << 20)

    def body(a_ref, b_ref, o_ref):
        o_ref[...] = (a_ref[...].astype(F32) + b_ref[...].astype(F32)).astype(BF16)

    spec = pl.BlockSpec((None, br, width), lambda j, i: (j, i, 0))
    return pl.pallas_call(body, name=name, grid=(own.shape[0], rows // br), in_specs=[spec, spec],
                          out_specs=spec, out_shape=jax.ShapeDtypeStruct(own.shape, BF16),
                          compiler_params=_params("parallel", "parallel"))(own, got)


def _sum_chips(parts, name):
    _, rows, width = parts.shape
    br = _row_block(rows, width, itemsize=4, budget=1 << 20)

    def body(p_ref, o_ref):
        acc = p_ref[0].astype(F32)
        for j in range(1, N_CHIPS):
            acc = acc + p_ref[j].astype(F32)
        o_ref[...] = acc

    return pl.pallas_call(body, name=name, grid=(rows // br,),
                          in_specs=[pl.BlockSpec((N_CHIPS, br, width), lambda i: (0, i, 0))],
                          out_specs=pl.BlockSpec((br, width), lambda i: (i, 0)),
                          out_shape=jax.ShapeDtypeStruct((rows, width), F32),
                          compiler_params=_params("parallel"))(parts)


def _allreduce_small(part, name):
    rows = part.shape[0]

    def body(p_ref, o_ref, all_ref, send, recv):
        x, y, c, _ = _place()
        me = 4 * x + 2 * y + c
        all_ref[me] = p_ref[...]
        copies = []
        for r in range(1, N_DEV):
            to = (x ^ (r >> 2), y ^ ((r >> 1) & 1), c ^ (r & 1))
            cp = _remote(p_ref, all_ref.at[me], send, recv, r - 1, to)
            cp.start()
            copies.append(cp)
        for r in range(1, N_DEV):
            frm = (x ^ (r >> 2), y ^ ((r >> 1) & 1), c ^ (r & 1))
            slot = all_ref.at[4 * frm[0] + 2 * frm[1] + frm[2]]
            _remote(slot, slot, send, recv, r - 1, frm).wait_recv()
        for cp in copies:
            cp.wait_send()
        acc = all_ref[0]
        for dev in range(1, N_DEV):
            acc = acc + all_ref[dev]
        o_ref[...] = acc

    vmem = pl.BlockSpec(memory_space=pltpu.VMEM)
    return pl.pallas_call(
        body, name=name, in_specs=[vmem], out_specs=vmem,
        out_shape=jax.ShapeDtypeStruct(part.shape, F32),
        scratch_shapes=[pltpu.VMEM((N_DEV, rows, LANES), F32), pltpu.SemaphoreType.DMA((N_DEV - 1,)),
                        pltpu.SemaphoreType.DMA((N_DEV - 1,))],
        compiler_params=pltpu.CompilerParams(vmem_limit_bytes=VMEM_LIMIT),
    )(part)


INPUT_NAMES = None


def _weight_names():
    names = []
    for i, kind in enumerate(("gmlp", "swa", "fox", "gmlp")):
        p = f"l{i}_"
        names += [p + "ffn1_norm", p + "ffn1_wi", p + "ffn1_wo", p + "mix_norm", p + "mix_win"]
        if kind == "gmlp":
            names += [p + "gmlp_vnorm", p + "gmlp_ws", p + "gmlp_bs"]
        elif kind == "swa":
            names += [p + "swa_sinks"]
        else:
            names += [p + "fox_bf"]
        names += [p + "mix_wout", p + "ffn2_norm", p + "ffn2_wi", p + "ffn2_wo"]
    return names + ["final_norm"]


WEIGHTS = _weight_names()
MIXERS = ("gmlp", "swa", "fox", "gmlp")
BIG = ("ffn1_wi", "ffn1_wo", "mix_win", "mix_wout", "ffn2_wi", "ffn2_wo")


def _ffn_fwd(h, gain, wi, wo, tag):
    n = _rms_fwd(h, gain, tag + "_norm")
    z = _matmul(n, wi, name=tag + "_up", out_dtype=BF16)
    a = _swiglu_fwd(z, tag + "_act")
    f, d = wo.shape[0] * wo.shape[1], wo.shape[2]
    out = _matmul(a, wo.reshape(f, d), name=tag + "_down", out_dtype=F32, scale=0.5, resid=h)
    return out, (h, n, z, a)


def _ffn_bwd(dout, saved, gain, wi, wo, tag):
    h, n, z, a = saved
    f, d = wo.shape[0] * wo.shape[1], wo.shape[2]
    da = _matmul(dout, wo.reshape(f, d), tb=True, name=tag + "_bdown", out_dtype=BF16, scale=0.5)
    dwo = _matmul(a, dout, ta=True, name=tag + "_gdown", out_dtype=BF16, scale=0.5)
    dz = _swiglu_bwd(z, da, tag + "_bact")
    dn = _matmul(dz, wi, tb=True, name=tag + "_bup", out_dtype=F32)
    dwi = _matmul(n, dz, ta=True, name=tag + "_gup", out_dtype=BF16, out_shards=N_CHIPS)
    dh, dgain = _norm_bwd(h, gain, dn, dout, tag + "_bnorm")
    return dh, dgain, dwi, dwo.reshape(wo.shape)


def _natural(w_sharded, pad_to):
    ns, rows, csh = w_sharded.shape
    nat = jnp.transpose(w_sharded, (1, 0, 2)).reshape(rows, ns * csh)
    extra = (-nat.shape[1]) % pad_to
    return jnp.pad(nat, ((0, 0), (0, extra))) if extra else nat


def _mixer_fwd(kind, h, p, tag):
    s_len, d = h.shape
    n = _rms_fwd(h, p["mix_norm"], tag + "_norm")
    wout = p["mix_wout"].reshape(d, d)
    if kind == "gmlp":
        zp = _matmul(n, p["mix_win"], name=tag + "_in", out_dtype=BF16)
        y = _gmlp_fwd(zp, p["gmlp_vnorm"], p["gmlp_ws"], p["gmlp_bs"], tag + "_gate")
        saved = (h, n, zp, y)
    elif kind == "swa":
        qkv = _matmul(n, p["mix_win"], name=tag + "_in", out_dtype=F32)
        q, k, v = _rope_fwd(qkv, tag + "_rope")
        y = _swa_fwd(q, k, v, p["swa_sinks"], tag + "_attn")
        saved = (h, n, q, k, v, y)
    else:
        heads = d // FOX_HEAD_DIM
        win = _natural(p["mix_win"], LANES)
        proj = _matmul(n, win, name=tag + "_in", out_dtype=F32)
        bf_row = jnp.pad(p["fox_bf"], (0, LANES - heads)).reshape(1, LANES)
        dec = _fox_decay(proj, bf_row, 3 * heads, tag + "_decay")
        dec_t = dec[:, :heads].T
        decq, deck = dec_t.reshape(heads, s_len, 1), dec_t.reshape(heads, 1, s_len)
        y = _fox_fwd(proj, decq, deck, heads, tag + "_attn")
        saved = (h, n, win, proj, bf_row, decq, deck, y)
    out = _matmul(y, wout, name=tag + "_out", out_dtype=F32, resid=h)
    return out, saved


def _mixer_bwd(kind, dout, saved, p, tag):
    h, n = saved[0], saved[1]
    y = saved[-1]
    s_len, d = h.shape
    wout = p["mix_wout"].reshape(d, d)
    grads = {}
    dy = _matmul(dout, wout, tb=True, name=tag + "_bout", out_dtype=BF16)
    grads["mix_wout"] = _matmul(y, dout, ta=True, name=tag + "_gout", out_dtype=BF16).reshape(p["mix_wout"].shape)
    if kind == "gmlp":
        zp = saved[2]
        dzp, dws, dbst, dvg = _gmlp_bwd(zp, dy, p["gmlp_vnorm"], p["gmlp_ws"], p["gmlp_bs"], tag + "_bgate")
        grads.update(gmlp_ws=dws, gmlp_bs=dbst.T, gmlp_vnorm=dvg.reshape(d))
        dn = _matmul(dzp, p["mix_win"], tb=True, name=tag + "_bin", out_dtype=F32)
        grads["mix_win"] = _matmul(n, dzp, ta=True, name=tag + "_gin", out_dtype=BF16, out_shards=N_CHIPS)
    elif kind == "swa":
        q, k, v = saved[2:5]
        dq, dk, dv, dsinks = _swa_bwd(q, k, v, p["swa_sinks"], dy, tag + "_battn")
        grads["swa_sinks"] = dsinks[0, :p["swa_sinks"].shape[0]]
        dqkv = _rope_bwd(dq, dk, dv, tag + "_brope")
        dn = _matmul(dqkv, p["mix_win"], tb=True, name=tag + "_bin", out_dtype=F32)
        grads["mix_win"] = _matmul(n, dqkv, ta=True, name=tag + "_gin", out_dtype=BF16, out_shards=N_CHIPS)
    else:
        win, proj, bf_row, decq, deck = saved[2:7]
        heads = d // FOX_HEAD_DIM
        dq, dk, dv, ddq, ddk = _fox_bwd(proj, decq, deck, dy, heads, tag + "_battn")
        widen = lambda t: jnp.pad(t.reshape(heads, s_len).T, ((0, 0), (0, LANES - heads)))
        dfl, dbf = _fox_decay_bwd(widen(ddq), widen(ddk), proj, bf_row, 3 * heads, heads, tag + "_bdecay")
        grads["fox_bf"] = dbf[0, :heads]
        dproj = jnp.concatenate([dq, dk.astype(BF16), dv.astype(BF16), dfl], axis=1)
        dn = _matmul(dproj, win, tb=True, name=tag + "_bin", out_dtype=F32)
        dwin = _matmul(n, dproj, ta=True, name=tag + "_gin", out_dtype=BF16)
        ns, rows, csh = p["mix_win"].shape
        grads["mix_win"] = jnp.transpose(dwin[:, :ns * csh].reshape(rows, ns, csh), (1, 0, 2))
    dh, dgain = _norm_bwd(h, p["mix_norm"], dn, dout, tag + "_bnorm")
    grads["mix_norm"] = dgain.reshape(d)
    return dh, grads


def _pack_small(arrays):
    flat = jnp.concatenate([a.reshape(-1).astype(F32) for a in arrays])
    pad = (-flat.shape[0]) % (512 * LANES)
    return jnp.pad(flat, (0, pad)).reshape(-1, LANES)


def _unpack_small(packed, like):
    flat, out, pos = packed.reshape(-1), [], 0
    for a in like:
        out.append(flat[pos:pos + a.size].reshape(a.shape))
        pos += a.size
    return out


def _step(inp):
    x, target = inp["x"][0], inp["loss_target"][0]
    d = x.shape[1]

    full = {}
    for i in range(len(MIXERS)):
        names = [f"l{i}_{b}" for b in BIG]
        shards = [_cast_bf16(inp[nm], nm + "_cast") for nm in names]
        for nm, g in zip(names, _gather_shards(shards, f"l{i}_gather")):
            full[nm] = g

    def layer_params(i):
        p = {}
        for nm in WEIGHTS:
            if nm.startswith(f"l{i}_"):
                key = nm[len(f"l{i}_"):]
                p[key] = full[nm] if key in BIG else inp[nm]
        return p

    h, saved = x, []
    for i, kind in enumerate(MIXERS):
        p = layer_params(i)
        h, s1 = _ffn_fwd(h, p["ffn1_norm"], p["ffn1_wi"], p["ffn1_wo"], f"l{i}_ffn1")
        h, s2 = _mixer_fwd(kind, h, p, f"l{i}_mix")
        h, s3 = _ffn_fwd(h, p["ffn2_norm"], p["ffn2_wi"], p["ffn2_wo"], f"l{i}_ffn2")
        saved.append((s1, s2, s3))
    loss_part, dh, dfinal = _loss_head(h, inp["final_norm"], target, "loss_head")
    loss = lax.psum(loss_part, ("x", "y", "c"))

    small_grads = {"final_norm": dfinal.reshape(d)}
    outs = {}
    for i in reversed(range(len(MIXERS))):
        kind = MIXERS[i]
        p = layer_params(i)
        s1, s2, s3 = saved[i]
        big = {}
        dh, g_norm, big["ffn2_wi"], big["ffn2_wo"] = _ffn_bwd(dh, s3, p["ffn2_norm"], p["ffn2_wi"], p["ffn2_wo"], f"l{i}_ffn2")
        small_grads[f"l{i}_ffn2_norm"] = g_norm.reshape(d)
        dh, mg = _mixer_bwd(kind, dh, s2, p, f"l{i}_mix")
        for key, val in mg.items():
            if key in BIG:
                big[key] = val
            else:
                small_grads[f"l{i}_{key}"] = val
        dh, g_norm, big["ffn1_wi"], big["ffn1_wo"] = _ffn_bwd(dh, s1, p["ffn1_norm"], p["ffn1_wi"], p["ffn1_wo"], f"l{i}_ffn1")
        small_grads[f"l{i}_ffn1_norm"] = g_norm.reshape(d)

        grads = [big[b] for b in BIG]
        own, got = _pair_split(grads, f"l{i}_rs_pair")
        pair = [_add_pair(a, b, f"l{i}_{nm}_rs_add") for a, b, nm in zip(own, got, BIG)]
        slots = _chip_scatter(pair, f"l{i}_rs_chips")
        halves = [_sum_chips(sl, f"l{i}_{nm}_rs_sum") for sl, nm in zip(slots, BIG)]
        reduced = _pair_join(halves, f"l{i}_rs_join")
        for b, g in zip(BIG, reduced):
            nm = f"l{i}_{b}"
            g = g.reshape(inp[nm].shape)
            outs[nm] = (g, *_adamw(inp[nm], g, inp["m_" + nm], inp["v_" + nm], nm + "_adamw"))

    small_names = [nm for nm in WEIGHTS if nm not in outs]
    total = _allreduce_small(_pack_small([small_grads[nm] for nm in small_names]), "small_allreduce")
    like = [inp[nm] for nm in small_names]
    upd = _adamw(_pack_small(like), total, _pack_small([inp["m_" + nm] for nm in small_names]),
                 _pack_small([inp["v_" + nm] for nm in small_names]), "small_adamw")
    unpacked = [_unpack_small(t, like) for t in (total, *upd)]
    for k, nm in enumerate(small_names):
        outs[nm] = tuple(u[k] for u in unpacked)

    result = [loss, dh[None]]
    for part in range(4):
        result += [outs[nm][part] for nm in WEIGHTS]
    return tuple(result)


def kernel(x, l0_ffn1_norm, l0_ffn1_wi, l0_ffn1_wo, l0_mix_norm, l0_mix_win, l0_gmlp_vnorm, l0_gmlp_ws, l0_gmlp_bs, l0_mix_wout, l0_ffn2_norm, l0_ffn2_wi, l0_ffn2_wo, l1_ffn1_norm, l1_ffn1_wi, l1_ffn1_wo, l1_mix_norm, l1_mix_win, l1_swa_sinks, l1_mix_wout, l1_ffn2_norm, l1_ffn2_wi, l1_ffn2_wo, l2_ffn1_norm, l2_ffn1_wi, l2_ffn1_wo, l2_mix_norm, l2_mix_win, l2_fox_bf, l2_mix_wout, l2_ffn2_norm, l2_ffn2_wi, l2_ffn2_wo, l3_ffn1_norm, l3_ffn1_wi, l3_ffn1_wo, l3_mix_norm, l3_mix_win, l3_gmlp_vnorm, l3_gmlp_ws, l3_gmlp_bs, l3_mix_wout, l3_ffn2_norm, l3_ffn2_wi, l3_ffn2_wo, final_norm, loss_target, m_l0_ffn1_norm, m_l0_ffn1_wi, m_l0_ffn1_wo, m_l0_mix_norm, m_l0_mix_win, m_l0_gmlp_vnorm, m_l0_gmlp_ws, m_l0_gmlp_bs, m_l0_mix_wout, m_l0_ffn2_norm, m_l0_ffn2_wi, m_l0_ffn2_wo, m_l1_ffn1_norm, m_l1_ffn1_wi, m_l1_ffn1_wo, m_l1_mix_norm, m_l1_mix_win, m_l1_swa_sinks, m_l1_mix_wout, m_l1_ffn2_norm, m_l1_ffn2_wi, m_l1_ffn2_wo, m_l2_ffn1_norm, m_l2_ffn1_wi, m_l2_ffn1_wo, m_l2_mix_norm, m_l2_mix_win, m_l2_fox_bf, m_l2_mix_wout, m_l2_ffn2_norm, m_l2_ffn2_wi, m_l2_ffn2_wo, m_l3_ffn1_norm, m_l3_ffn1_wi, m_l3_ffn1_wo, m_l3_mix_norm, m_l3_mix_win, m_l3_gmlp_vnorm, m_l3_gmlp_ws, m_l3_gmlp_bs, m_l3_mix_wout, m_l3_ffn2_norm, m_l3_ffn2_wi, m_l3_ffn2_wo, m_final_norm, v_l0_ffn1_norm, v_l0_ffn1_wi, v_l0_ffn1_wo, v_l0_mix_norm, v_l0_mix_win, v_l0_gmlp_vnorm, v_l0_gmlp_ws, v_l0_gmlp_bs, v_l0_mix_wout, v_l0_ffn2_norm, v_l0_ffn2_wi, v_l0_ffn2_wo, v_l1_ffn1_norm, v_l1_ffn1_wi, v_l1_ffn1_wo, v_l1_mix_norm, v_l1_mix_win, v_l1_swa_sinks, v_l1_mix_wout, v_l1_ffn2_norm, v_l1_ffn2_wi, v_l1_ffn2_wo, v_l2_ffn1_norm, v_l2_ffn1_wi, v_l2_ffn1_wo, v_l2_mix_norm, v_l2_mix_win, v_l2_fox_bf, v_l2_mix_wout, v_l2_ffn2_norm, v_l2_ffn2_wi, v_l2_ffn2_wo, v_l3_ffn1_norm, v_l3_ffn1_wi, v_l3_ffn1_wo, v_l3_mix_norm, v_l3_mix_win, v_l3_gmlp_vnorm, v_l3_gmlp_ws, v_l3_gmlp_bs, v_l3_mix_wout, v_l3_ffn2_norm, v_l3_ffn2_wi, v_l3_ffn2_wo, v_final_norm):
    return _step(dict(locals()))
```

```python
import functools
import math

import jax
import jax.numpy as jnp
from jax import lax
from jax.experimental import pallas as pl
from jax.experimental.pallas import tpu as pltpu

F32 = jnp.float32
BF16 = jnp.bfloat16

NORM_EPS = 1e-5
NEG_INF = -1e30
BLOCK = 128
GMLP_GROUPS = 16
SWA_HEAD_DIM = 64
SWA_GROUP = 8
ROPE_DIM = SWA_HEAD_DIM // 4
ROPE_THETA = 500000.0
FOX_HEAD_DIM = 128
ADAM_LR = 0.001
ADAM_B1 = 0.9
ADAM_B2 = 0.999
ADAM_EPS = 1e-08
ADAM_WD = 0.01
ADAM_STEP = 10
N_CHIPS = 4
N_DEV = 8
LANES = 128
VMEM_LIMIT = 56 * 1024 * 1024
MESH = pl.DeviceIdType.MESH
HBM = pl.BlockSpec(memory_space=pltpu.HBM)

MM_TILES = (1024, 1408, 896, 640, 512, 384, 256, 128)


def _pick(n, prefs):
    for p in prefs:
        if p <= n and n % p == 0:
            return p
    return n


def _params(*sem):
    return pltpu.CompilerParams(dimension_semantics=sem or None, vmem_limit_bytes=VMEM_LIMIT)


def _cols(arr):
    return arr.shape[-1] * (arr.shape[0] if arr.ndim == 3 else 1)


def _mat_spec(arr, rb, cb, ridx, cidx):
    if arr.ndim == 2:
        return pl.BlockSpec((rb, cb), lambda j, i, k: (ridx(j, i, k), cidx(j, i, k)))
    per = arr.shape[2] // cb
    return pl.BlockSpec((None, rb, cb),
                        lambda j, i, k: (cidx(j, i, k) // per, ridx(j, i, k), cidx(j, i, k) % per))


def _matmul(a, b, *, name, out_dtype, ta=False, tb=False, out_shards=1, scale=1.0, resid=None):
    m_dim, k_dim = (a.shape[1], a.shape[0]) if ta else a.shape
    n_dim = b.shape[-2] if tb else _cols(b)
    assert k_dim == (_cols(b) if tb else b.shape[-2]), (a.shape, b.shape, ta, tb)
    n_unit = n_dim // out_shards
    if b.ndim == 3 and not tb:
        n_unit = math.gcd(n_unit, b.shape[2])
    k_unit = b.shape[2] if (b.ndim == 3 and tb) else k_dim
    bm = _pick(m_dim, MM_TILES)
    bn = _pick(n_unit, MM_TILES)
    bk = k_unit if k_unit <= 2048 else _pick(k_unit, MM_TILES)
    nk = k_dim // bk
    i_of, j_of, k_of = (lambda j, i, k: i), (lambda j, i, k: j), (lambda j, i, k: k)
    a_spec = _mat_spec(a, bk, bm, k_of, i_of) if ta else _mat_spec(a, bm, bk, i_of, k_of)
    b_spec = _mat_spec(b, bn, bk, j_of, k_of) if tb else _mat_spec(b, bk, bn, k_of, j_of)
    out_shape = (m_dim, n_dim) if out_shards == 1 else (out_shards, m_dim, n_dim // out_shards)
    out = jax.ShapeDtypeStruct(out_shape, out_dtype)
    o_spec = _mat_spec(out, bm, bn, i_of, j_of)
    dims = (((0 if ta else 1,), (1 if tb else 0,)), ((), ()))
    operands, in_specs = [a, b], [a_spec, b_spec]
    if resid is not None:
        operands.append(resid)
        in_specs.append(_mat_spec(resid, bm, bn, i_of, j_of))

    def body(*refs):
        a_ref, b_ref = refs[0], refs[1]
        r_ref = refs[2] if resid is not None else None
        o_ref = refs[3] if resid is not None else refs[2]
        part = lax.dot_general(a_ref[...].astype(BF16), b_ref[...].astype(BF16), dims,
                               preferred_element_type=F32)

        def finish(acc):
            val = acc * scale if scale != 1.0 else acc
            if r_ref is not None:
                val = r_ref[...] + val
            o_ref[...] = val.astype(o_ref.dtype)

        if nk == 1:
            finish(part)
        else:
            acc_ref = refs[-1]
            k = pl.program_id(2)

            @pl.when(k == 0)
            def _():
                acc_ref[...] = part

            @pl.when(k > 0)
            def _():
                acc_ref[...] += part

            @pl.when(k == nk - 1)
            def _():
                finish(acc_ref[...])

    return pl.pallas_call(
        body, name=name, grid=(n_dim // bn, m_dim // bm, nk),
        in_specs=in_specs, out_specs=o_spec, out_shape=out,
        scratch_shapes=[pltpu.VMEM((bm, bn), F32)] if nk > 1 else [],
        compiler_params=_params("parallel", "parallel", "arbitrary"),
    )(*operands)


def _row_block(rows, width, itemsize=4, budget=2 << 20):
    for br in (512, 256, 128, 64, 32, 16, 8):
        if rows % br == 0 and br * width * itemsize <= budget:
            return br
    return rows


def _rms_fwd(h, g, name):
    s_len, d = h.shape
    br = _row_block(s_len, d)

    def body(h_ref, g_ref, o_ref):
        x = h_ref[...]
        r = lax.rsqrt(jnp.mean(x * x, axis=-1, keepdims=True) + NORM_EPS)
        o_ref[...] = (x * r * g_ref[...]).astype(BF16)

    spec = pl.BlockSpec((br, d), lambda i: (i, 0))
    return pl.pallas_call(body, name=name, grid=(s_len // br,),
                          in_specs=[spec, pl.BlockSpec((1, d), lambda i: (0, 0))], out_specs=spec,
                          out_shape=jax.ShapeDtypeStruct((s_len, d), BF16),
                          compiler_params=_params("parallel"))(h, g.reshape(1, d))


def _rms_bwd_rows(x, g, dn):
    r = lax.rsqrt(jnp.mean(x * x, axis=-1, keepdims=True) + NORM_EPS)
    xhat = x * r
    gdn = dn * g
    dx = r * (gdn - xhat * jnp.mean(gdn * xhat, axis=-1, keepdims=True))
    return dx, dn * xhat


def _norm_bwd(h, g, dn, dres, name):
    s_len, d = h.shape
    br = _row_block(s_len, d, budget=1 << 20)

    def body(h_ref, g_ref, dn_ref, dres_ref, dh_ref, dg_ref):
        dx, dg_rows = _rms_bwd_rows(h_ref[...], g_ref[...], dn_ref[...].astype(F32))
        dh_ref[...] = dres_ref[...] + dx

        @pl.when(pl.program_id(0) == 0)
        def _():
            dg_ref[...] = jnp.zeros_like(dg_ref)

        dg_ref[...] += jnp.sum(dg_rows, axis=0, keepdims=True)

    spec = pl.BlockSpec((br, d), lambda i: (i, 0))
    vec = pl.BlockSpec((1, d), lambda i: (0, 0))
    return pl.pallas_call(body, name=name, grid=(s_len // br,),
                          in_specs=[spec, vec, spec, spec], out_specs=[spec, vec],
                          out_shape=[jax.ShapeDtypeStruct((s_len, d), F32),
                                     jax.ShapeDtypeStruct((1, d), F32)],
                          compiler_params=_params("arbitrary"))(h, g.reshape(1, d), dn, dres)


def _sigmoid(x):
    return 1.0 / (1.0 + jnp.exp(-x))


def _swiglu_fwd(z, name):
    s_len, f2 = z.shape
    f = f2 // 2
    br = _row_block(s_len, f2, budget=3 << 20)

    def body(z_ref, a_ref):
        gate = z_ref[:, :f].astype(F32)
        up = z_ref[:, f:].astype(F32)
        a_ref[...] = (gate * _sigmoid(gate) * up).astype(BF16)

    return pl.pallas_call(body, name=name, grid=(s_len // br,),
                          in_specs=[pl.BlockSpec((br, f2), lambda i: (i, 0))],
                          out_specs=pl.BlockSpec((br, f), lambda i: (i, 0)),
                          out_shape=jax.ShapeDtypeStruct((s_len, f), BF16),
                          compiler_params=_params("parallel"))(z)


def _swiglu_bwd(z, da, name):
    s_len, f2 = z.shape
    f = f2 // 2
    br = _row_block(s_len, f2, budget=3 << 20)

    def body(z_ref, da_ref, dz_ref):
        gate = z_ref[:, :f].astype(F32)
        up = z_ref[:, f:].astype(F32)
        d = da_ref[...].astype(F32)
        sig = _sigmoid(gate)
        dz_ref[:, :f] = (d * up * (sig * (1.0 + gate * (1.0 - sig)))).astype(BF16)
        dz_ref[:, f:] = (d * gate * sig).astype(BF16)

    return pl.pallas_call(body, name=name, grid=(s_len // br,),
                          in_specs=[pl.BlockSpec((br, f2), lambda i: (i, 0)),
                                    pl.BlockSpec((br, f), lambda i: (i, 0))],
                          out_specs=pl.BlockSpec((br, f2), lambda i: (i, 0)),
                          out_shape=jax.ShapeDtypeStruct((s_len, f2), BF16),
                          compiler_params=_params("parallel"))(z, da)


def _loss_head(h, g, target, name):
    s_len, d = h.shape
    br = _row_block(s_len, d, budget=1 << 20)

    def body(h_ref, g_ref, t_ref, loss_ref, dh_ref, dg_ref):
        x = h_ref[...]
        gain = g_ref[...]
        r = lax.rsqrt(jnp.mean(x * x, axis=-1, keepdims=True) + NORM_EPS)
        err = x * r * gain - t_ref[...]
        part = 0.5 * jnp.sum(jnp.mean(err * err, axis=-1, keepdims=True), axis=0, keepdims=True)
        dx, dg_rows = _rms_bwd_rows(x, gain, err * (1.0 / d))
        dh_ref[...] = dx

        @pl.when(pl.program_id(0) == 0)
        def _():
            dg_ref[...] = jnp.zeros_like(dg_ref)
            loss_ref[...] = jnp.zeros_like(loss_ref)

        dg_ref[...] += jnp.sum(dg_rows, axis=0, keepdims=True)
        loss_ref[...] += jnp.broadcast_to(part, loss_ref.shape)

    spec = pl.BlockSpec((br, d), lambda i: (i, 0))
    vec = pl.BlockSpec((1, d), lambda i: (0, 0))
    one = pl.BlockSpec((1, LANES), lambda i: (0, 0))
    loss, dh, dg = pl.pallas_call(
        body, name=name, grid=(s_len // br,), in_specs=[spec, vec, spec],
        out_specs=[one, spec, vec],
        out_shape=[jax.ShapeDtypeStruct((1, LANES), F32), jax.ShapeDtypeStruct((s_len, d), F32),
                   jax.ShapeDtypeStruct((1, d), F32)],
        compiler_params=_params("arbitrary"))(h, g.reshape(1, d), target)
    return loss[0, 0], dh, dg


def _adamw(w, g, m, v, name):
    rows, width = w.shape
    br = _row_block(rows, width, budget=1 << 20)
    c1 = 1.0 - ADAM_B1 ** ADAM_STEP
    c2 = 1.0 - ADAM_B2 ** ADAM_STEP

    def body(w_ref, g_ref, m_ref, v_ref, d_ref, nm_ref, nv_ref):
        grad = g_ref[...]
        new_m = ADAM_B1 * m_ref[...] + (1.0 - ADAM_B1) * grad
        new_v = ADAM_B2 * v_ref[...] + (1.0 - ADAM_B2) * (grad * grad)
        d_ref[...] = -ADAM_LR * ((new_m / c1) / (jnp.sqrt(new_v / c2) + ADAM_EPS) + ADAM_WD * w_ref[...])
        nm_ref[...] = new_m
        nv_ref[...] = new_v

    spec = pl.BlockSpec((br, width), lambda i: (i, 0))
    shp = jax.ShapeDtypeStruct(w.shape, F32)
    return pl.pallas_call(body, name=name, grid=(rows // br,), in_specs=[spec] * 4,
                          out_specs=[spec] * 3, out_shape=[shp] * 3,
                          compiler_params=_params("parallel"))(w, g, m, v)


def _gelu(x):
    return 0.5 * x * (1.0 + lax.erf(x * (2.0 ** -0.5)))


def _gelu_grad(x):
    return 0.5 * (1.0 + lax.erf(x * (2.0 ** -0.5))) + x * jnp.exp(-0.5 * x * x) * ((2.0 * math.pi) ** -0.5)


def _tril_mask():
    row = lax.broadcasted_iota(jnp.int32, (BLOCK, BLOCK), 0)
    col = lax.broadcasted_iota(jnp.int32, (BLOCK, BLOCK), 1)
    return col <= row


def _gmlp_specs(s_len, d):
    gw = d // GMLP_GROUPS
    zp = pl.BlockSpec((BLOCK, 2 * d), lambda i: (i, 0))
    row = pl.BlockSpec((BLOCK, d), lambda i: (i, 0))
    vec = pl.BlockSpec((1, d), lambda i: (0, 0))
    ws = pl.BlockSpec((GMLP_GROUPS, BLOCK, BLOCK), lambda i: (0, 0, 0))
    bst = pl.BlockSpec((BLOCK, GMLP_GROUPS), lambda i: (0, 0))
    return gw, zp, row, vec, ws, bst


def _gmlp_fwd(zp, vgain, ws, bs, name):
    s_len, d2 = zp.shape
    d = d2 // 2
    gw, zp_spec, row_spec, vec_spec, ws_spec, bst_spec = _gmlp_specs(s_len, d)

    def body(zp_ref, vg_ref, ws_ref, bst_ref, y_ref):
        u = _gelu(zp_ref[:, :d].astype(F32))
        vv = _gelu(zp_ref[:, d:].astype(F32))
        r = lax.rsqrt(jnp.mean(vv * vv, axis=-1, keepdims=True) + NORM_EPS)
        vn = (vv * r * vg_ref[...]).astype(BF16)
        mask = _tril_mask()
        for g in range(GMLP_GROUPS):
            cols = slice(g * gw, (g + 1) * gw)
            wg = jnp.where(mask, ws_ref[g], 0.0).astype(BF16)
            mixed = jnp.dot(wg, vn[:, cols], preferred_element_type=F32) + bst_ref[:, g:g + 1]
            y_ref[:, cols] = (u[:, cols] * mixed).astype(BF16)

    return pl.pallas_call(body, name=name, grid=(s_len // BLOCK,),
                          in_specs=[zp_spec, vec_spec, ws_spec, bst_spec], out_specs=row_spec,
                          out_shape=jax.ShapeDtypeStruct((s_len, d), BF16),
                          compiler_params=_params("parallel"))(zp, vgain.reshape(1, d), ws, bs.T)


def _gmlp_bwd(zp, dy, vgain, ws, bs, name):
    s_len, d2 = zp.shape
    d = d2 // 2
    gw, zp_spec, row_spec, vec_spec, ws_spec, bst_spec = _gmlp_specs(s_len, d)

    def body(zp_ref, dy_ref, vg_ref, ws_ref, bst_ref, dzp_ref, dws_ref, dbst_ref, dvg_ref, dvn_ref):
        @pl.when(pl.program_id(0) == 0)
        def _():
            dws_ref[...] = jnp.zeros_like(dws_ref)
            dbst_ref[...] = jnp.zeros_like(dbst_ref)
            dvg_ref[...] = jnp.zeros_like(dvg_ref)

        zu = zp_ref[:, :d].astype(F32)
        zv = zp_ref[:, d:].astype(F32)
        u = _gelu(zu)
        vv = _gelu(zv)
        r = lax.rsqrt(jnp.mean(vv * vv, axis=-1, keepdims=True) + NORM_EPS)
        vhat = vv * r
        gain = vg_ref[...]
        vn = (vhat * gain).astype(BF16)
        dyf = dy_ref[...].astype(F32)
        dmixed = dyf * u
        dmixed_b = dmixed.astype(BF16)
        mask = _tril_mask()
        lane = lax.broadcasted_iota(jnp.int32, (BLOCK, GMLP_GROUPS), 1)
        dbs_step = jnp.zeros((BLOCK, GMLP_GROUPS), F32)
        for g in range(GMLP_GROUPS):
            cols = slice(g * gw, (g + 1) * gw)
            wg = jnp.where(mask, ws_ref[g], 0.0).astype(BF16)
            mixed = jnp.dot(wg, vn[:, cols], preferred_element_type=F32) + bst_ref[:, g:g + 1]
            dzp_ref[:, cols] = (dyf[:, cols] * mixed * _gelu_grad(zu[:, cols])).astype(BF16)
            dm = dmixed_b[:, cols]
            dw = lax.dot_general(dm, vn[:, cols], (((1,), (1,)), ((), ())), preferred_element_type=F32)
            dws_ref[g] += jnp.where(mask, dw, 0.0)
            dbs_step = dbs_step + jnp.where(lane == g, jnp.sum(dmixed[:, cols], axis=-1, keepdims=True), 0.0)
            dvn_ref[:, cols] = lax.dot_general(wg, dm, (((0,), (0,)), ((), ())), preferred_element_type=F32)
        dbst_ref[...] += dbs_step
        dvn = dvn_ref[...]
        dvg_ref[...] += jnp.sum(dvn * vhat, axis=0, keepdims=True)
        dvhat = dvn * gain
        dvv = r * (dvhat - vhat * jnp.mean(dvhat * vhat, axis=-1, keepdims=True))
        dzp_ref[:, d:] = (dvv * _gelu_grad(zv)).astype(BF16)

    return pl.pallas_call(
        body, name=name, grid=(s_len // BLOCK,),
        in_specs=[zp_spec, row_spec, vec_spec, ws_spec, bst_spec],
        out_specs=[zp_spec, ws_spec, bst_spec, vec_spec],
        out_shape=[jax.ShapeDtypeStruct((s_len, d2), BF16), jax.ShapeDtypeStruct(ws.shape, F32),
                   jax.ShapeDtypeStruct((BLOCK, GMLP_GROUPS), F32), jax.ShapeDtypeStruct((1, d), F32)],
        scratch_shapes=[pltpu.VMEM((BLOCK, d), F32)],
        compiler_params=_params("arbitrary"))(zp, dy, vgain.reshape(1, d), ws, bs.T)


def _rope_tables(s_len, sign):
    half = ROPE_DIM // 2
    inv_freq = ROPE_THETA ** (-(jnp.arange(half, dtype=F32) * 2.0 / ROPE_DIM))
    ang = jnp.arange(s_len, dtype=F32)[:, None] * inv_freq[None, :]
    cos, sin = jnp.cos(ang), jnp.sin(ang) * sign
    pad = jnp.zeros((s_len, SWA_HEAD_DIM - ROPE_DIM), F32)
    zero = jnp.zeros_like(sin)
    cos_t = jnp.concatenate([cos, cos, pad + 1.0], axis=1)
    sin_up = jnp.concatenate([-sin, zero, pad], axis=1)
    sin_dn = jnp.concatenate([zero, sin, pad], axis=1)
    return [jnp.tile(t, (1, LANES // SWA_HEAD_DIM)) for t in (cos_t, sin_up, sin_dn)]


def _rotate(x, cos_t, sin_up, sin_dn):
    width = x.shape[-1]
    half = ROPE_DIM // 2
    reps = width // cos_t.shape[-1]
    if reps > 1:
        cos_t, sin_up, sin_dn = (jnp.tile(t, (1, reps)) for t in (cos_t, sin_up, sin_dn))
    elif reps == 0:
        cos_t, sin_up, sin_dn = (t[:, :width] for t in (cos_t, sin_up, sin_dn))
    return x * cos_t + pltpu.roll(x, width - half, 1) * sin_up + pltpu.roll(x, half, 1) * sin_dn


def _rope_fwd(qkv, name):
    s_len, total = qkv.shape
    wkv = total // (SWA_GROUP + 2)
    wq = SWA_GROUP * wkv
    br = _row_block(s_len, total, budget=2 << 20)
    tables = _rope_tables(s_len, 1.0)

    def body(q_ref, k_ref, v_ref, c_ref, su_ref, sd_ref, qo_ref, ko_ref, vo_ref):
        t = (c_ref[...], su_ref[...], sd_ref[...])
        qo_ref[...] = _rotate(q_ref[...], *t).astype(BF16)
        ko_ref[...] = _rotate(k_ref[...], *t).astype(BF16)
        vo_ref[...] = v_ref[...].astype(BF16)

    qs = pl.BlockSpec((br, wq), lambda i: (i, 0))
    ks = pl.BlockSpec((br, wkv), lambda i: (i, SWA_GROUP))
    vs = pl.BlockSpec((br, wkv), lambda i: (i, SWA_GROUP + 1))
    ts = pl.BlockSpec((br, LANES), lambda i: (i, 0))
    kv_out = pl.BlockSpec((br, wkv), lambda i: (i, 0))
    return pl.pallas_call(
        body, name=name, grid=(s_len // br,), in_specs=[qs, ks, vs, ts, ts, ts],
        out_specs=[qs, kv_out, kv_out],
        out_shape=[jax.ShapeDtypeStruct((s_len, wq), BF16), jax.ShapeDtypeStruct((s_len, wkv), BF16),
                   jax.ShapeDtypeStruct((s_len, wkv), BF16)],
        compiler_params=_params("parallel"))(qkv, qkv, qkv, *tables)


def _rope_bwd(dq, dk, dv, name):
    s_len, wq = dq.shape
    wkv = dk.shape[1]
    br = _row_block(s_len, wq + 2 * wkv, budget=2 << 20)
    tables = _rope_tables(s_len, -1.0)

    def body(q_ref, k_ref, v_ref, c_ref, su_ref, sd_ref, o_ref):
        t = (c_ref[...], su_ref[...], sd_ref[...])
        o_ref[:, :wq] = _rotate(q_ref[...], *t).astype(BF16)
        o_ref[:, wq:wq + wkv] = _rotate(k_ref[...], *t).astype(BF16)
        o_ref[:, wq + wkv:] = v_ref[...].astype(BF16)

    qs = pl.BlockSpec((br, wq), lambda i: (i, 0))
    kvs = pl.BlockSpec((br, wkv), lambda i: (i, 0))
    ts = pl.BlockSpec((br, LANES), lambda i: (i, 0))
    return pl.pallas_call(
        body, name=name, grid=(s_len // br,), in_specs=[qs, kvs, kvs, ts, ts, ts],
        out_specs=pl.BlockSpec((br, wq + 2 * wkv), lambda i: (i, 0)),
        out_shape=jax.ShapeDtypeStruct((s_len, wq + 2 * wkv), BF16),
        compiler_params=_params("parallel"))(dq, dk, dv, *tables)


def _swa_valid(i):
    row = lax.broadcasted_iota(jnp.int32, (BLOCK, 2 * BLOCK), 0)
    col = lax.broadcasted_iota(jnp.int32, (BLOCK, 2 * BLOCK), 1)
    return (col - BLOCK <= row) & (row < col) & ((col >= BLOCK) | (i > 0))


def _swa_specs(wq, wkv):
    q_spec = pl.BlockSpec((BLOCK, wq), lambda i: (i, 0))
    cur = pl.BlockSpec((BLOCK, wkv), lambda i: (i, 0))
    prev = pl.BlockSpec((BLOCK, wkv), lambda i: (jnp.maximum(i - 1, 0), 0))
    sink = pl.BlockSpec(memory_space=pltpu.SMEM)
    return q_spec, cur, prev, sink


def _swa_probs(q_h, k_cat, valid, sink):
    s = lax.dot_general(q_h, k_cat, (((1,), (1,)), ((), ())), preferred_element_type=F32)
    s = jnp.where(valid, s * (SWA_HEAD_DIM ** -0.5), NEG_INF)
    m = jnp.maximum(jnp.max(s, axis=-1, keepdims=True), sink)
    p = jnp.exp(s - m)
    e_sink = jnp.exp(sink - m)
    denom = jnp.sum(p, axis=-1, keepdims=True) + e_sink
    return p / denom, e_sink / denom


def _swa_fwd(q, k, v, sinks, name):
    s_len, wq = q.shape
    wkv = k.shape[1]
    hd = SWA_HEAD_DIM
    q_spec, cur, prev, sink_spec = _swa_specs(wq, wkv)

    def body(q_ref, kc_ref, kp_ref, vc_ref, vp_ref, sink_ref, o_ref):
        valid = _swa_valid(pl.program_id(0))
        for j in range(wkv // hd):
            lanes = slice(j * hd, (j + 1) * hd)
            k_cat = jnp.concatenate([kp_ref[:, lanes], kc_ref[:, lanes]], axis=0)
            v_cat = jnp.concatenate([vp_ref[:, lanes], vc_ref[:, lanes]], axis=0)
            for hh in range(SWA_GROUP):
                h = j * SWA_GROUP + hh
                pn, _ = _swa_probs(q_ref[:, h * hd:(h + 1) * hd], k_cat, valid, sink_ref[h])
                o_ref[:, h * hd:(h + 1) * hd] = jnp.dot(
                    pn.astype(BF16), v_cat, preferred_element_type=F32).astype(BF16)

    return pl.pallas_call(body, name=name, grid=(s_len // BLOCK,),
                          in_specs=[q_spec, cur, prev, cur, prev, sink_spec], out_specs=q_spec,
                          out_shape=jax.ShapeDtypeStruct((s_len, wq), BF16),
                          compiler_params=_params("parallel"))(q, k, k, v, v, sinks)


def _swa_bwd(q, k, v, sinks, do, name):
    s_len, wq = q.shape
    wkv = k.shape[1]
    hd = SWA_HEAD_DIM
    q_spec, cur, prev, sink_spec = _swa_specs(wq, wkv)
    full = pl.BlockSpec((s_len, wkv), lambda i: (0, 0))
    one = pl.BlockSpec((1, LANES), lambda i: (0, 0))
    scale = hd ** -0.5

    def body(q_ref, kc_ref, kp_ref, vc_ref, vp_ref, sink_ref, do_ref, dq_ref, dk_ref, dv_ref, ds_ref):
        i = pl.program_id(0)

        @pl.when(i == 0)
        def _():
            dk_ref[...] = jnp.zeros_like(dk_ref)
            dv_ref[...] = jnp.zeros_like(dv_ref)
            ds_ref[...] = jnp.zeros_like(ds_ref)

        valid = _swa_valid(i)
        lane = lax.broadcasted_iota(jnp.int32, (1, LANES), 1)
        dsink_step = jnp.zeros((1, LANES), F32)
        rows_prev = pl.ds(pl.multiple_of(jnp.maximum(i - 1, 0) * BLOCK, BLOCK), BLOCK)
        rows_cur = pl.ds(pl.multiple_of(i * BLOCK, BLOCK), BLOCK)
        for j in range(wkv // hd):
            lanes = slice(j * hd, (j + 1) * hd)
            k_cat = jnp.concatenate([kp_ref[:, lanes], kc_ref[:, lanes]], axis=0)
            v_cat = jnp.concatenate([vp_ref[:, lanes], vc_ref[:, lanes]], axis=0)
            dk_cat = jnp.zeros((2 * BLOCK, hd), F32)
            dv_cat = jnp.zeros((2 * BLOCK, hd), F32)
            for hh in range(SWA_GROUP):
                h = j * SWA_GROUP + hh
                q_h = q_ref[:, h * hd:(h + 1) * hd]
                do_h = do_ref[:, h * hd:(h + 1) * hd]
                pn, p_sink = _swa_probs(q_h, k_cat, valid, sink_ref[h])
                dpn = lax.dot_general(do_h, v_cat, (((1,), (1,)), ((), ())), preferred_element_type=F32)
                delta = jnp.sum(dpn * pn, axis=-1, keepdims=True)
                ds = (pn * (dpn - delta) * scale).astype(BF16)
                dsink_h = -jnp.sum(p_sink * delta, axis=0, keepdims=True)
                dsink_step = dsink_step + jnp.where(lane == h, dsink_h, 0.0)
                dq_ref[:, h * hd:(h + 1) * hd] = jnp.dot(ds, k_cat, preferred_element_type=F32)
                dk_cat = dk_cat + lax.dot_general(ds, q_h, (((0,), (0,)), ((), ())),
                                                  preferred_element_type=F32)
                dv_cat = dv_cat + lax.dot_general(pn.astype(BF16), do_h, (((0,), (0,)), ((), ())),
                                                  preferred_element_type=F32)
            dk_ref[rows_prev, lanes] += dk_cat[:BLOCK]
            dk_ref[rows_cur, lanes] += dk_cat[BLOCK:]
            dv_ref[rows_prev, lanes] += dv_cat[:BLOCK]
            dv_ref[rows_cur, lanes] += dv_cat[BLOCK:]
        ds_ref[...] += dsink_step

    return pl.pallas_call(
        body, name=name, grid=(s_len // BLOCK,),
        in_specs=[q_spec, cur, prev, cur, prev, sink_spec, q_spec],
        out_specs=[q_spec, full, full, one],
        out_shape=[jax.ShapeDtypeStruct((s_len, wq), F32), jax.ShapeDtypeStruct((s_len, wkv), F32),
                   jax.ShapeDtypeStruct((s_len, wkv), F32), jax.ShapeDtypeStruct((1, LANES), F32)],
        compiler_params=_params("arbitrary"))(q, k, k, v, v, sinks, do)


def _log_sigmoid(x):
    return jnp.minimum(x, 0.0) - jnp.log(1.0 + jnp.exp(-jnp.abs(x)))


def _tri_ones(lower):
    row = lax.broadcasted_iota(jnp.int32, (BLOCK, BLOCK), 0)
    col = lax.broadcasted_iota(jnp.int32, (BLOCK, BLOCK), 1)
    return jnp.where((col <= row) if lower else (col >= row), 1.0, 0.0).astype(F32)


def _fox_decay(proj, bf_row, fl_block, name):
    s_len = proj.shape[0]
    nchunk = s_len // BLOCK

    def body(fl_ref, bf_ref, dec_ref):
        tri = _tri_ones(True)
        carry = jnp.zeros((1, LANES), F32)
        for c in range(nchunk):
            rows = slice(c * BLOCK, (c + 1) * BLOCK)
            log_f = _log_sigmoid(fl_ref[rows, :] + bf_ref[...])
            loc = jnp.dot(tri, log_f, preferred_element_type=F32, precision=lax.Precision.HIGHEST) + carry
            dec_ref[rows, :] = loc
            carry = loc[BLOCK - 1:BLOCK, :]

    return pl.pallas_call(
        body, name=name, grid=(1,),
        in_specs=[pl.BlockSpec((s_len, LANES), lambda i: (0, fl_block)),
                  pl.BlockSpec((1, LANES), lambda i: (0, 0))],
        out_specs=pl.BlockSpec((s_len, LANES), lambda i: (0, 0)),
        out_shape=jax.ShapeDtypeStruct((s_len, LANES), F32),
        compiler_params=_params("arbitrary"))(proj, bf_row)


def _fox_decay_bwd(ddq, ddk, proj, bf_row, fl_block, heads, name):
    s_len = proj.shape[0]
    nchunk = s_len // BLOCK

    def body(ddq_ref, ddk_ref, fl_ref, bf_ref, dfl_ref, dbf_ref):
        tri = _tri_ones(False)
        lane_ok = lax.broadcasted_iota(jnp.int32, (BLOCK, LANES), 1) < heads
        carry = jnp.zeros((1, LANES), F32)
        dbf = jnp.zeros((1, LANES), F32)
        for c in reversed(range(nchunk)):
            rows = slice(c * BLOCK, (c + 1) * BLOCK)
            ddec = ddq_ref[rows, :] + ddk_ref[rows, :]
            dlog = jnp.dot(tri, ddec, preferred_element_type=F32, precision=lax.Precision.HIGHEST) + carry
            carry = dlog[0:1, :]
            dfl = jnp.where(lane_ok, dlog * _sigmoid(-(fl_ref[rows, :] + bf_ref[...])), 0.0)
            dfl_ref[rows, :] = dfl.astype(BF16)
            dbf = dbf + jnp.sum(dfl, axis=0, keepdims=True)
        dbf_ref[...] = dbf

    blk = pl.BlockSpec((s_len, LANES), lambda i: (0, 0))
    one = pl.BlockSpec((1, LANES), lambda i: (0, 0))
    return pl.pallas_call(
        body, name=name, grid=(1,),
        in_specs=[blk, blk, pl.BlockSpec((s_len, LANES), lambda i: (0, fl_block)), one],
        out_specs=[blk, one],
        out_shape=[jax.ShapeDtypeStruct((s_len, LANES), BF16), jax.ShapeDtypeStruct((1, LANES), F32)],
        compiler_params=_params("arbitrary"))(ddq, ddk, proj, bf_row)


def _fox_scores(q, k, decq, deck, i, bq):
    s_len = k.shape[0]
    s = lax.dot_general(q, k, (((1,), (1,)), ((), ())), preferred_element_type=F32)
    s = s * (FOX_HEAD_DIM ** -0.5) + decq - deck
    row = lax.broadcasted_iota(jnp.int32, (bq, s_len), 0) + i * bq
    col = lax.broadcasted_iota(jnp.int32, (bq, s_len), 1)
    s = jnp.where(col <= row, s, NEG_INF)
    p = jnp.exp(s - jnp.max(s, axis=-1, keepdims=True))
    return p / jnp.sum(p, axis=-1, keepdims=True)


def _fox_specs(s_len, heads, bq):
    hd = FOX_HEAD_DIM
    q_spec = pl.BlockSpec((bq, hd), lambda h, i: (i, h))
    k_spec = pl.BlockSpec((s_len, hd), lambda h, i: (0, heads + h))
    v_spec = pl.BlockSpec((s_len, hd), lambda h, i: (0, 2 * heads + h))
    dq_spec = pl.BlockSpec((None, bq, 1), lambda h, i: (h, i, 0))
    dk_spec = pl.BlockSpec((None, 1, s_len), lambda h, i: (h, 0, 0))
    return q_spec, k_spec, v_spec, dq_spec, dk_spec


def _fox_fwd(proj, decq, deck, heads, name):
    s_len = proj.shape[0]
    bq = _pick(s_len, (256, 128))
    q_spec, k_spec, v_spec, dq_spec, dk_spec = _fox_specs(s_len, heads, bq)

    def body(q_ref, k_ref, v_ref, decq_ref, deck_ref, o_ref):
        pn = _fox_scores(q_ref[...].astype(BF16), k_ref[...].astype(BF16), decq_ref[...], deck_ref[...],
                         pl.program_id(1), bq)
        o_ref[...] = jnp.dot(pn.astype(BF16), v_ref[...].astype(BF16),
                             preferred_element_type=F32).astype(BF16)

    return pl.pallas_call(body, name=name, grid=(heads, s_len // bq),
                          in_specs=[q_spec, k_spec, v_spec, dq_spec, dk_spec], out_specs=q_spec,
                          out_shape=jax.ShapeDtypeStruct((s_len, heads * FOX_HEAD_DIM), BF16),
                          compiler_params=_params("parallel", "parallel"))(proj, proj, proj, decq, deck)


def _fox_bwd(proj, decq, deck, do, heads, name):
    s_len = proj.shape[0]
    d = heads * FOX_HEAD_DIM
    bq = _pick(s_len, (256, 128))
    q_spec, k_spec, v_spec, dq_spec, dk_spec = _fox_specs(s_len, heads, bq)
    acc_spec = pl.BlockSpec((s_len, FOX_HEAD_DIM), lambda h, i: (0, h))
    scale = FOX_HEAD_DIM ** -0.5

    def body(q_ref, k_ref, v_ref, decq_ref, deck_ref, do_ref, dq_ref, dk_ref, dv_ref, ddq_ref, ddk_ref):
        i = pl.program_id(1)

        @pl.when(i == 0)
        def _():
            dk_ref[...] = jnp.zeros_like(dk_ref)
            dv_ref[...] = jnp.zeros_like(dv_ref)
            ddk_ref[...] = jnp.zeros_like(ddk_ref)

        q = q_ref[...].astype(BF16)
        k = k_ref[...].astype(BF16)
        do_b = do_ref[...]
        pn = _fox_scores(q, k, decq_ref[...], deck_ref[...], i, bq)
        dpn = lax.dot_general(do_b, v_ref[...].astype(BF16), (((1,), (1,)), ((), ())),
                              preferred_element_type=F32)
        ds = pn * (dpn - jnp.sum(dpn * pn, axis=-1, keepdims=True))
        ddq_ref[...] = jnp.sum(ds, axis=-1, keepdims=True)
        ddk_ref[...] -= jnp.sum(ds, axis=0, keepdims=True)
        ds_b = (ds * scale).astype(BF16)
        dq_ref[...] = jnp.dot(ds_b, k, preferred_element_type=F32).astype(BF16)
        dk_ref[...] += lax.dot_general(ds_b, q, (((0,), (0,)), ((), ())), preferred_element_type=F32)
        dv_ref[...] += lax.dot_general(pn.astype(BF16), do_b, (((0,), (0,)), ((), ())),
                                       preferred_element_type=F32)

    return pl.pallas_call(
        body, name=name, grid=(heads, s_len // bq),
        in_specs=[q_spec, k_spec, v_spec, dq_spec, dk_spec, q_spec],
        out_specs=[q_spec, acc_spec, acc_spec, dq_spec, dk_spec],
        out_shape=[jax.ShapeDtypeStruct((s_len, d), BF16), jax.ShapeDtypeStruct((s_len, d), F32),
                   jax.ShapeDtypeStruct((s_len, d), F32), jax.ShapeDtypeStruct((heads, s_len, 1), F32),
                   jax.ShapeDtypeStruct((heads, 1, s_len), F32)],
        compiler_params=_params("parallel", "arbitrary"))(proj, proj, proj, decq, deck, do)


def _place():
    x, y, c = lax.axis_index("x"), lax.axis_index("y"), lax.axis_index("c")
    chips = [(1 - x, y), (x, 1 - y), (1 - x, 1 - y)]
    return x, y, c, chips


def _remote(src, dst, send_sems, recv_sems, idx, to):
    return pltpu.make_async_remote_copy(src_ref=src, dst_ref=dst, send_sem=send_sems.at[idx],
                                        recv_sem=recv_sems.at[idx], device_id=to, device_id_type=MESH)


def _row_chunks(rows, want):
    for k in (want, want // 2, want // 4):
        if k >= 1 and rows % (16 * k) == 0:
            return [(j * (rows // k), rows // k) for j in range(k)]
    return [(0, rows)]


def _chunked(src_of, dst_of, rows, want, send_sems, recv_sems, idx, to):
    for start, size in _row_chunks(rows, want):
        _remote(src_of(start, size), dst_of(start, size), send_sems, recv_sems, idx, to).start()
    return _remote(src_of(0, rows), dst_of(0, rows), send_sems, recv_sems, idx, to)


D2D_CHUNKS = 8


def _cast_into_slot(w, me, name):
    rows, width = w.shape
    br = _row_block(rows, width)

    def body(me_ref, w_ref, o_ref):
        o_ref[...] = w_ref[...].astype(BF16)

    return pl.pallas_call(
        body, name=name,
        grid_spec=pltpu.PrefetchScalarGridSpec(
            num_scalar_prefetch=1, grid=(rows // br,),
            in_specs=[pl.BlockSpec((br, width), lambda i, me_ref: (i, 0))],
            out_specs=pl.BlockSpec((None, br, width), lambda i, me_ref: (me_ref[0], i, 0))),
        out_shape=jax.ShapeDtypeStruct((N_CHIPS, rows, width), BF16),
        compiler_params=_params("parallel"))(me, w)


def _gather_shards(bufs, name):
    n = len(bufs)

    def body(*refs):
        outs = refs[n:2 * n]
        send1, recv1, send2, recv2 = refs[2 * n:]
        x, y, c, chips = _place()
        me = 2 * x + y
        sibling = (x, y, 1 - c)
        started = []
        for t in range(n):
            half = outs[t].shape[1] // 2
            mine = outs[t].at[me, pl.ds(c * half, half)]
            for r, chip in enumerate(chips):
                cp = _remote(mine, mine, send1, recv1, 3 * t + r, (*chip, c))
                cp.start()
                started.append(cp)
        for t in range(n):
            half = outs[t].shape[1] // 2
            for r, chip in enumerate(chips):
                slot = 2 * chip[0] + chip[1]
                block = outs[t].at[slot, pl.ds(c * half, half)]
                _remote(block, block, send1, recv1, 3 * t + r, (*chip, c)).wait_recv()
                part = lambda s, z, t=t, slot=slot, half=half: outs[t].at[slot, pl.ds(c * half + s, z)]
                started.append(_chunked(part, part, half, D2D_CHUNKS, send2, recv2, 3 * t + r, sibling))
        for t in range(n):
            half = outs[t].shape[1] // 2
            for r, chip in enumerate(chips):
                block = outs[t].at[2 * chip[0] + chip[1], pl.ds((1 - c) * half, half)]
                _remote(block, block, send2, recv2, 3 * t + r, sibling).wait_recv()
        for cp in started:
            cp.wait_send()

    return pl.pallas_call(
        body, name=name, in_specs=[HBM] * n, out_specs=[HBM] * n,
        out_shape=[jax.ShapeDtypeStruct(b.shape, b.dtype) for b in bufs],
        input_output_aliases={t: t for t in range(n)},
        scratch_shapes=[pltpu.SemaphoreType.DMA((3 * n,))] * 4,
    )(*bufs)


def _pair_swap(grads, name):
    n = len(grads)

    def body(*refs):
        ins, got = refs[:n], refs[n:2 * n]
        send, recv = refs[2 * n:]
        x, y, c, _ = _place()
        waits = []
        for t in range(n):
            half = ins[t].shape[1] // 2
            for j in range(N_CHIPS):
                src = lambda s, z, t=t, j=j, half=half: ins[t].at[j, pl.ds((1 - c) * half + s, z)]
                dst = lambda s, z, t=t, j=j: got[t].at[j, pl.ds(s, z)]
                waits.append(_chunked(src, dst, half, 2, send, recv, N_CHIPS * t + j, (x, y, 1 - c)))
        for cp in waits:
            cp.wait()

    return pl.pallas_call(
        body, name=name, in_specs=[HBM] * n, out_specs=[HBM] * n,
        out_shape=[jax.ShapeDtypeStruct((g.shape[0], g.shape[1] // 2, g.shape[2]), g.dtype) for g in grads],
        scratch_shapes=[pltpu.SemaphoreType.DMA((N_CHIPS * n,))] * 2,
    )(*grads)


def _chip_scatter(parts, name):
    n = len(parts)

    def body(*refs):
        ins, outs = refs[:n], refs[n:2 * n]
        send, recv = refs[2 * n:]
        x, y, c, chips = _place()
        sent = []
        for t in range(n):
            for r, chip in enumerate(chips):
                rc = _remote(ins[t].at[2 * chip[0] + chip[1]], outs[t].at[r], send, recv, 3 * t + r, (*chip, c))
                rc.start()
                sent.append(rc)
        for t in range(n):
            for r, chip in enumerate(chips):
                _remote(outs[t].at[r], outs[t].at[r], send, recv, 3 * t + r, (*chip, c)).wait_recv()
        for cp in sent:
            cp.wait_send()

    return pl.pallas_call(
        body, name=name, in_specs=[HBM] * n, out_specs=[HBM] * n,
        out_shape=[jax.ShapeDtypeStruct((3, *p.shape[1:]), p.dtype) for p in parts],
        scratch_shapes=[pltpu.SemaphoreType.DMA((3 * n,))] * 2,
    )(*parts)


def _pair_exchange(halves, name):
    n = len(halves)

    def body(*refs):
        ins, got = refs[:n], refs[n:2 * n]
        send, recv = refs[2 * n:]
        x, y, c, _ = _place()
        waits = []
        for t in range(n):
            src = lambda s, z, t=t: ins[t].at[pl.ds(s, z)]
            dst = lambda s, z, t=t: got[t].at[pl.ds(s, z)]
            waits.append(_chunked(src, dst, ins[t].shape[0], D2D_CHUNKS, send, recv, t, (x, y, 1 - c)))
        for cp in waits:
            cp.wait()

    return pl.pallas_call(
        body, name=name, in_specs=[HBM] * n, out_specs=[HBM] * n,
        out_shape=[jax.ShapeDtypeStruct(h.shape, h.dtype) for h in halves],
        scratch_shapes=[pltpu.SemaphoreType.DMA((n,))] * 2,
    )(*halves)


def _add_pair(grad, got, c, name):
    _, half, width = got.shape
    br = _row_block(half, width, itemsize=2, budget=1 << 20)
    nb = half // br

    def body(c_ref, a_ref, b_ref, o_ref):
        o_ref[...] = (a_ref[...].astype(F32) + b_ref[...].astype(F32)).astype(BF16)

    spec = pl.BlockSpec((None, br, width), lambda j, i, c_ref: (j, i, 0))
    mine = pl.BlockSpec((None, br, width), lambda j, i, c_ref: (j, c_ref[0] * nb + i, 0))
    return pl.pallas_call(
        body, name=name,
        grid_spec=pltpu.PrefetchScalarGridSpec(num_scalar_prefetch=1, grid=(N_CHIPS, nb),
                                               in_specs=[mine, spec], out_specs=spec),
        out_shape=jax.ShapeDtypeStruct(got.shape, BF16),
        compiler_params=_params("parallel", "parallel"))(c, grad, got)


def _sum_chips(pair, others, me, name):
    _, rows, width = pair.shape
    br = _row_block(rows, width, itemsize=4, budget=1 << 20)

    def body(me_ref, p_ref, o3_ref, o_ref):
        acc = p_ref[...].astype(F32)
        for r in range(N_CHIPS - 1):
            acc = acc + o3_ref[r].astype(F32)
        o_ref[...] = acc

    return pl.pallas_call(
        body, name=name,
        grid_spec=pltpu.PrefetchScalarGridSpec(
            num_scalar_prefetch=1, grid=(rows // br,),
            in_specs=[pl.BlockSpec((None, br, width), lambda i, me_ref: (me_ref[0], i, 0)),
                      pl.BlockSpec((N_CHIPS - 1, br, width), lambda i, me_ref: (0, i, 0))],
            out_specs=pl.BlockSpec((br, width), lambda i, me_ref: (i, 0))),
        out_shape=jax.ShapeDtypeStruct((rows, width), F32),
        compiler_params=_params("parallel"))(me, pair, others)


def _adamw_halves(w, mine, theirs, m, v, c, name):
    rows, width = w.shape
    half = rows // 2
    br = _row_block(half, width, budget=1 << 20)
    nb = half // br
    c1 = 1.0 - ADAM_B1 ** ADAM_STEP
    c2 = 1.0 - ADAM_B2 ** ADAM_STEP

    def body(c_ref, w_ref, a_ref, b_ref, m_ref, v_ref, g_ref, d_ref, nm_ref, nv_ref):
        grad = jnp.where(pl.program_id(0) == c_ref[0], a_ref[...], b_ref[...])
        new_m = ADAM_B1 * m_ref[...] + (1.0 - ADAM_B1) * grad
        new_v = ADAM_B2 * v_ref[...] + (1.0 - ADAM_B2) * (grad * grad)
        g_ref[...] = grad
        d_ref[...] = -ADAM_LR * ((new_m / c1) / (jnp.sqrt(new_v / c2) + ADAM_EPS) + ADAM_WD * w_ref[...])
        nm_ref[...] = new_m
        nv_ref[...] = new_v

    full = pl.BlockSpec((br, width), lambda h, i, c_ref: (h * nb + i, 0))
    part = pl.BlockSpec((br, width), lambda h, i, c_ref: (i, 0))
    shp = jax.ShapeDtypeStruct(w.shape, F32)
    return pl.pallas_call(
        body, name=name,
        grid_spec=pltpu.PrefetchScalarGridSpec(num_scalar_prefetch=1, grid=(2, nb),
                                               in_specs=[full, part, part, full, full], out_specs=[full] * 4),
        out_shape=[shp] * 4,
        compiler_params=_params("parallel", "parallel"))(c, w, mine, theirs, m, v)


def _allreduce_small(part, name):
    rows = part.shape[0]

    def body(p_ref, o_ref, all_ref, send, recv):
        x, y, c, _ = _place()
        me = 4 * x + 2 * y + c
        all_ref[me] = p_ref[...]
        copies = []
        for r in range(1, N_DEV):
            to = (x ^ (r >> 2), y ^ ((r >> 1) & 1), c ^ (r & 1))
            cp = _remote(p_ref, all_ref.at[me], send, recv, r - 1, to)
            cp.start()
            copies.append(cp)
        for r in range(1, N_DEV):
            frm = (x ^ (r >> 2), y ^ ((r >> 1) & 1), c ^ (r & 1))
            slot = all_ref.at[4 * frm[0] + 2 * frm[1] + frm[2]]
            _remote(slot, slot, send, recv, r - 1, frm).wait_recv()
        for cp in copies:
            cp.wait_send()
        acc = all_ref[0]
        for dev in range(1, N_DEV):
            acc = acc + all_ref[dev]
        o_ref[...] = acc

    vmem = pl.BlockSpec(memory_space=pltpu.VMEM)
    return pl.pallas_call(
        body, name=name, in_specs=[vmem], out_specs=vmem,
        out_shape=jax.ShapeDtypeStruct(part.shape, F32),
        scratch_shapes=[pltpu.VMEM((N_DEV, rows, LANES), F32), pltpu.SemaphoreType.DMA((N_DEV - 1,)),
                        pltpu.SemaphoreType.DMA((N_DEV - 1,))],
        compiler_params=pltpu.CompilerParams(vmem_limit_bytes=VMEM_LIMIT),
    )(part)


INPUT_NAMES = None


def _weight_names():
    names = []
    for i, kind in enumerate(("gmlp", "swa", "fox", "gmlp")):
        p = f"l{i}_"
        names += [p + "ffn1_norm", p + "ffn1_wi", p + "ffn1_wo", p + "mix_norm", p + "mix_win"]
        if kind == "gmlp":
            names += [p + "gmlp_vnorm", p + "gmlp_ws", p + "gmlp_bs"]
        elif kind == "swa":
            names += [p + "swa_sinks"]
        else:
            names += [p + "fox_bf"]
        names += [p + "mix_wout", p + "ffn2_norm", p + "ffn2_wi", p + "ffn2_wo"]
    return names + ["final_norm"]


WEIGHTS = _weight_names()
MIXERS = ("gmlp", "swa", "fox", "gmlp")
BIG = ("ffn1_wi", "ffn1_wo", "mix_win", "mix_wout", "ffn2_wi", "ffn2_wo")


def _ffn_fwd(h, gain, wi, wo, tag):
    n = _rms_fwd(h, gain, tag + "_norm")
    z = _matmul(n, wi, name=tag + "_up", out_dtype=BF16)
    a = _swiglu_fwd(z, tag + "_act")
    f, d = wo.shape[0] * wo.shape[1], wo.shape[2]
    out = _matmul(a, wo.reshape(f, d), name=tag + "_down", out_dtype=F32, scale=0.5, resid=h)
    return out, (h, n, z, a)


def _ffn_bwd(dout, saved, gain, wi, wo, tag):
    h, n, z, a = saved
    f, d = wo.shape[0] * wo.shape[1], wo.shape[2]
    da = _matmul(dout, wo.reshape(f, d), tb=True, name=tag + "_bdown", out_dtype=BF16, scale=0.5)
    dwo = _matmul(a, dout, ta=True, name=tag + "_gdown", out_dtype=BF16, scale=0.5)
    dz = _swiglu_bwd(z, da, tag + "_bact")
    dn = _matmul(dz, wi, tb=True, name=tag + "_bup", out_dtype=F32)
    dwi = _matmul(n, dz, ta=True, name=tag + "_gup", out_dtype=BF16, out_shards=N_CHIPS)
    dh, dgain = _norm_bwd(h, gain, dn, dout, tag + "_bnorm")
    return dh, dgain, dwi, dwo.reshape(wo.shape)


def _natural(w_sharded, pad_to):
    ns, rows, csh = w_sharded.shape
    nat = jnp.transpose(w_sharded, (1, 0, 2)).reshape(rows, ns * csh)
    extra = (-nat.shape[1]) % pad_to
    return jnp.pad(nat, ((0, 0), (0, extra))) if extra else nat


def _mixer_fwd(kind, h, p, tag):
    s_len, d = h.shape
    n = _rms_fwd(h, p["mix_norm"], tag + "_norm")
    wout = p["mix_wout"].reshape(d, d)
    if kind == "gmlp":
        zp = _matmul(n, p["mix_win"], name=tag + "_in", out_dtype=BF16)
        y = _gmlp_fwd(zp, p["gmlp_vnorm"], p["gmlp_ws"], p["gmlp_bs"], tag + "_gate")
        saved = (h, n, zp, y)
    elif kind == "swa":
        qkv = _matmul(n, p["mix_win"], name=tag + "_in", out_dtype=F32)
        q, k, v = _rope_fwd(qkv, tag + "_rope")
        y = _swa_fwd(q, k, v, p["swa_sinks"], tag + "_attn")
        saved = (h, n, q, k, v, y)
    else:
        heads = d // FOX_HEAD_DIM
        win = _natural(p["mix_win"], LANES)
        proj = _matmul(n, win, name=tag + "_in", out_dtype=F32)
        bf_row = jnp.pad(p["fox_bf"], (0, LANES - heads)).reshape(1, LANES)
        dec = _fox_decay(proj, bf_row, 3 * heads, tag + "_decay")
        dec_t = dec[:, :heads].T
        decq, deck = dec_t.reshape(heads, s_len, 1), dec_t.reshape(heads, 1, s_len)
        y = _fox_fwd(proj, decq, deck, heads, tag + "_attn")
        saved = (h, n, win, proj, bf_row, decq, deck, y)
    out = _matmul(y, wout, name=tag + "_out", out_dtype=F32, resid=h)
    return out, saved


def _mixer_bwd(kind, dout, saved, p, tag):
    h, n = saved[0], saved[1]
    y = saved[-1]
    s_len, d = h.shape
    wout = p["mix_wout"].reshape(d, d)
    grads = {}
    dy = _matmul(dout, wout, tb=True, name=tag + "_bout", out_dtype=BF16)
    grads["mix_wout"] = _matmul(y, dout, ta=True, name=tag + "_gout", out_dtype=BF16).reshape(p["mix_wout"].shape)
    if kind == "gmlp":
        zp = saved[2]
        dzp, dws, dbst, dvg = _gmlp_bwd(zp, dy, p["gmlp_vnorm"], p["gmlp_ws"], p["gmlp_bs"], tag + "_bgate")
        grads.update(gmlp_ws=dws, gmlp_bs=dbst.T, gmlp_vnorm=dvg.reshape(d))
        dn = _matmul(dzp, p["mix_win"], tb=True, name=tag + "_bin", out_dtype=F32)
        grads["mix_win"] = _matmul(n, dzp, ta=True, name=tag + "_gin", out_dtype=BF16, out_shards=N_CHIPS)
    elif kind == "swa":
        q, k, v = saved[2:5]
        dq, dk, dv, dsinks = _swa_bwd(q, k, v, p["swa_sinks"], dy, tag + "_battn")
        grads["swa_sinks"] = dsinks[0, :p["swa_sinks"].shape[0]]
        dqkv = _rope_bwd(dq, dk, dv, tag + "_brope")
        dn = _matmul(dqkv, p["mix_win"], tb=True, name=tag + "_bin", out_dtype=F32)
        grads["mix_win"] = _matmul(n, dqkv, ta=True, name=tag + "_gin", out_dtype=BF16, out_shards=N_CHIPS)
    else:
        win, proj, bf_row, decq, deck = saved[2:7]
        heads = d // FOX_HEAD_DIM
        dq, dk, dv, ddq, ddk = _fox_bwd(proj, decq, deck, dy, heads, tag + "_battn")
        widen = lambda t: jnp.pad(t.reshape(heads, s_len).T, ((0, 0), (0, LANES - heads)))
        dfl, dbf = _fox_decay_bwd(widen(ddq), widen(ddk), proj, bf_row, 3 * heads, heads, tag + "_bdecay")
        grads["fox_bf"] = dbf[0, :heads]
        dproj = jnp.concatenate([dq, dk.astype(BF16), dv.astype(BF16), dfl], axis=1)
        dn = _matmul(dproj, win, tb=True, name=tag + "_bin", out_dtype=F32)
        dwin = _matmul(n, dproj, ta=True, name=tag + "_gin", out_dtype=BF16)
        ns, rows, csh = p["mix_win"].shape
        grads["mix_win"] = jnp.transpose(dwin[:, :ns * csh].reshape(rows, ns, csh), (1, 0, 2))
    dh, dgain = _norm_bwd(h, p["mix_norm"], dn, dout, tag + "_bnorm")
    grads["mix_norm"] = dgain.reshape(d)
    return dh, grads


def _pack_small(arrays):
    flat = jnp.concatenate([a.reshape(-1).astype(F32) for a in arrays])
    pad = (-flat.shape[0]) % (512 * LANES)
    return jnp.pad(flat, (0, pad)).reshape(-1, LANES)


def _unpack_small(packed, like):
    flat, out, pos = packed.reshape(-1), [], 0
    for a in like:
        out.append(flat[pos:pos + a.size].reshape(a.shape))
        pos += a.size
    return out


def _step(inp):
    x, target = inp["x"][0], inp["loss_target"][0]
    d = x.shape[1]

    core = lax.axis_index("c").astype(jnp.int32).reshape(1)
    chip = (2 * lax.axis_index("x") + lax.axis_index("y")).astype(jnp.int32).reshape(1)

    full = {}
    for i in range(len(MIXERS)):
        names = [f"l{i}_{b}" for b in BIG]
        bufs = [_cast_into_slot(inp[nm], chip, nm + "_cast") for nm in names]
        for nm, g in zip(names, _gather_shards(bufs, f"l{i}_gather")):
            full[nm] = g

    def layer_params(i):
        p = {}
        for nm in WEIGHTS:
            if nm.startswith(f"l{i}_"):
                key = nm[len(f"l{i}_"):]
                p[key] = full[nm] if key in BIG else inp[nm]
        return p

    h, saved = x, []
    for i, kind in enumerate(MIXERS):
        p = layer_params(i)
        h, s1 = _ffn_fwd(h, p["ffn1_norm"], p["ffn1_wi"], p["ffn1_wo"], f"l{i}_ffn1")
        h, s2 = _mixer_fwd(kind, h, p, f"l{i}_mix")
        h, s3 = _ffn_fwd(h, p["ffn2_norm"], p["ffn2_wi"], p["ffn2_wo"], f"l{i}_ffn2")
        saved.append((s1, s2, s3))
    loss_part, dh, dfinal = _loss_head(h, inp["final_norm"], target, "loss_head")
    loss = lax.psum(loss_part, ("x", "y", "c"))

    small_grads = {"final_norm": dfinal.reshape(d)}
    outs = {}
    for i in reversed(range(len(MIXERS))):
        kind = MIXERS[i]
        p = layer_params(i)
        s1, s2, s3 = saved[i]
        big = {}
        dh, g_norm, big["ffn2_wi"], big["ffn2_wo"] = _ffn_bwd(dh, s3, p["ffn2_norm"], p["ffn2_wi"], p["ffn2_wo"], f"l{i}_ffn2")
        small_grads[f"l{i}_ffn2_norm"] = g_norm.reshape(d)
        dh, mg = _mixer_bwd(kind, dh, s2, p, f"l{i}_mix")
        for key, val in mg.items():
            if key in BIG:
                big[key] = val
            else:
                small_grads[f"l{i}_{key}"] = val
        dh, g_norm, big["ffn1_wi"], big["ffn1_wo"] = _ffn_bwd(dh, s1, p["ffn1_norm"], p["ffn1_wi"], p["ffn1_wo"], f"l{i}_ffn1")
        small_grads[f"l{i}_ffn1_norm"] = g_norm.reshape(d)

        grads = [big[b] for b in BIG]
        got = _pair_swap(grads, f"l{i}_rs_pair")
        pair = [_add_pair(g, b, core, f"l{i}_{nm}_rs_add") for g, b, nm in zip(grads, got, BIG)]
        others = _chip_scatter(pair, f"l{i}_rs_chips")
        halves = [_sum_chips(p, o, chip, f"l{i}_{nm}_rs_sum") for p, o, nm in zip(pair, others, BIG)]
        theirs = _pair_exchange(halves, f"l{i}_rs_join")
        for b, mine, other in zip(BIG, halves, theirs):
            nm = f"l{i}_{b}"
            outs[nm] = tuple(_adamw_halves(inp[nm], mine, other, inp["m_" + nm], inp["v_" + nm], core,
                                           nm + "_adamw"))

    small_names = [nm for nm in WEIGHTS if nm not in outs]
    total = _allreduce_small(_pack_small([small_grads[nm] for nm in small_names]), "small_allreduce")
    like = [inp[nm] for nm in small_names]
    upd = _adamw(_pack_small(like), total, _pack_small([inp["m_" + nm] for nm in small_names]),
                 _pack_small([inp["v_" + nm] for nm in small_names]), "small_adamw")
    unpacked = [_unpack_small(t, like) for t in (total, *upd)]
    for k, nm in enumerate(small_names):
        outs[nm] = tuple(u[k] for u in unpacked)

    result = [loss, dh[None]]
    for part in range(4):
        result += [outs[nm][part] for nm in WEIGHTS]
    return tuple(result)


def kernel(x, l0_ffn1_norm, l0_ffn1_wi, l0_ffn1_wo, l0_mix_norm, l0_mix_win, l0_gmlp_vnorm, l0_gmlp_ws, l0_gmlp_bs, l0_mix_wout, l0_ffn2_norm, l0_ffn2_wi, l0_ffn2_wo, l1_ffn1_norm, l1_ffn1_wi, l1_ffn1_wo, l1_mix_norm, l1_mix_win, l1_swa_sinks, l1_mix_wout, l1_ffn2_norm, l1_ffn2_wi, l1_ffn2_wo, l2_ffn1_norm, l2_ffn1_wi, l2_ffn1_wo, l2_mix_norm, l2_mix_win, l2_fox_bf, l2_mix_wout, l2_ffn2_norm, l2_ffn2_wi, l2_ffn2_wo, l3_ffn1_norm, l3_ffn1_wi, l3_ffn1_wo, l3_mix_norm, l3_mix_win, l3_gmlp_vnorm, l3_gmlp_ws, l3_gmlp_bs, l3_mix_wout, l3_ffn2_norm, l3_ffn2_wi, l3_ffn2_wo, final_norm, loss_target, m_l0_ffn1_norm, m_l0_ffn1_wi, m_l0_ffn1_wo, m_l0_mix_norm, m_l0_mix_win, m_l0_gmlp_vnorm, m_l0_gmlp_ws, m_l0_gmlp_bs, m_l0_mix_wout, m_l0_ffn2_norm, m_l0_ffn2_wi, m_l0_ffn2_wo, m_l1_ffn1_norm, m_l1_ffn1_wi, m_l1_ffn1_wo, m_l1_mix_norm, m_l1_mix_win, m_l1_swa_sinks, m_l1_mix_wout, m_l1_ffn2_norm, m_l1_ffn2_wi, m_l1_ffn2_wo, m_l2_ffn1_norm, m_l2_ffn1_wi, m_l2_ffn1_wo, m_l2_mix_norm, m_l2_mix_win, m_l2_fox_bf, m_l2_mix_wout, m_l2_ffn2_norm, m_l2_ffn2_wi, m_l2_ffn2_wo, m_l3_ffn1_norm, m_l3_ffn1_wi, m_l3_ffn1_wo, m_l3_mix_norm, m_l3_mix_win, m_l3_gmlp_vnorm, m_l3_gmlp_ws, m_l3_gmlp_bs, m_l3_mix_wout, m_l3_ffn2_norm, m_l3_ffn2_wi, m_l3_ffn2_wo, m_final_norm, v_l0_ffn1_norm, v_l0_ffn1_wi, v_l0_ffn1_wo, v_l0_mix_norm, v_l0_mix_win, v_l0_gmlp_vnorm, v_l0_gmlp_ws, v_l0_gmlp_bs, v_l0_mix_wout, v_l0_ffn2_norm, v_l0_ffn2_wi, v_l0_ffn2_wo, v_l1_ffn1_norm, v_l1_ffn1_wi, v_l1_ffn1_wo, v_l1_mix_norm, v_l1_mix_win, v_l1_swa_sinks, v_l1_mix_wout, v_l1_ffn2_norm, v_l1_ffn2_wi, v_l1_ffn2_wo, v_l2_ffn1_norm, v_l2_ffn1_wi, v_l2_ffn1_wo, v_l2_mix_norm, v_l2_mix_win, v_l2_fox_bf, v_l2_mix_wout, v_l2_ffn2_norm, v_l2_ffn2_wi, v_l2_ffn2_wo, v_l3_ffn1_norm, v_l3_ffn1_wi, v_l3_ffn1_wo, v_l3_mix_norm, v_l3_mix_win, v_l3_gmlp_vnorm, v_l3_gmlp_ws, v_l3_gmlp_bs, v_l3_mix_wout, v_l3_ffn2_norm, v_l3_ffn2_wi, v_l3_ffn2_wo, v_final_norm):
    return _step(dict(locals()))
```

```python
import functools
import math

import jax
import jax.numpy as jnp
from jax import lax
from jax.experimental import pallas as pl
from jax.experimental.pallas import tpu as pltpu

F32 = jnp.float32
BF16 = jnp.bfloat16

NORM_EPS = 1e-5
NEG_INF = -1e30
BLOCK = 128
GMLP_GROUPS = 16
SWA_HEAD_DIM = 64
SWA_GROUP = 8
ROPE_DIM = SWA_HEAD_DIM // 4
ROPE_THETA = 500000.0
FOX_HEAD_DIM = 128
ADAM_LR = 0.001
ADAM_B1 = 0.9
ADAM_B2 = 0.999
ADAM_EPS = 1e-08
ADAM_WD = 0.01
ADAM_STEP = 10
N_CHIPS = 4
N_DEV = 8
LANES = 128
VMEM_LIMIT = 56 * 1024 * 1024
MESH = pl.DeviceIdType.MESH
HBM = pl.BlockSpec(memory_space=pltpu.HBM)
SEM = pl.BlockSpec(memory_space=pltpu.SEMAPHORE)
EFFECT = pltpu.SideEffectType.DATAFLOW_SIDE_EFFECTING

MM_TILES = (1024, 1408, 896, 640, 512, 384, 256, 128)


def _pick(n, prefs):
    for p in prefs:
        if p <= n and n % p == 0:
            return p
    return n


def _params(*sem):
    return pltpu.CompilerParams(dimension_semantics=sem or None, vmem_limit_bytes=VMEM_LIMIT)


def _cols(arr):
    return arr.shape[-1] * (arr.shape[0] if arr.ndim == 3 else 1)


def _mat_spec(arr, rb, cb, ridx, cidx):
    if arr.ndim == 2:
        return pl.BlockSpec((rb, cb), lambda j, i, k: (ridx(j, i, k), cidx(j, i, k)))
    per = arr.shape[2] // cb
    return pl.BlockSpec((None, rb, cb),
                        lambda j, i, k: (cidx(j, i, k) // per, ridx(j, i, k), cidx(j, i, k) % per))


def _matmul(a, b, *, name, out_dtype, ta=False, tb=False, out_shards=1, scale=1.0, resid=None, dep=None):
    m_dim, k_dim = (a.shape[1], a.shape[0]) if ta else a.shape
    n_dim = b.shape[-2] if tb else _cols(b)
    assert k_dim == (_cols(b) if tb else b.shape[-2]), (a.shape, b.shape, ta, tb)
    n_unit = n_dim // out_shards
    if b.ndim == 3 and not tb:
        n_unit = math.gcd(n_unit, b.shape[2])
    k_unit = b.shape[2] if (b.ndim == 3 and tb) else k_dim
    bm = _pick(m_dim, MM_TILES)
    bn = _pick(n_unit, MM_TILES)
    bk = k_unit if k_unit <= 2048 else _pick(k_unit, MM_TILES)
    nk = k_dim // bk
    i_of, j_of, k_of = (lambda j, i, k: i), (lambda j, i, k: j), (lambda j, i, k: k)
    a_spec = _mat_spec(a, bk, bm, k_of, i_of) if ta else _mat_spec(a, bm, bk, i_of, k_of)
    b_spec = _mat_spec(b, bn, bk, j_of, k_of) if tb else _mat_spec(b, bk, bn, k_of, j_of)
    out_shape = (m_dim, n_dim) if out_shards == 1 else (out_shards, m_dim, n_dim // out_shards)
    out = jax.ShapeDtypeStruct(out_shape, out_dtype)
    o_spec = _mat_spec(out, bm, bn, i_of, j_of)
    dims = (((0 if ta else 1,), (1 if tb else 0,)), ((), ()))
    operands, in_specs = [a, b], [a_spec, b_spec]
    if resid is not None:
        operands.append(resid)
        in_specs.append(_mat_spec(resid, bm, bn, i_of, j_of))
    if dep is not None:
        operands.append(dep)
        in_specs.append(pl.BlockSpec(dep.shape, lambda j, i, k: (0, 0)))
    n_in = len(operands)

    def body(*refs):
        a_ref, b_ref = refs[0], refs[1]
        r_ref = refs[2] if resid is not None else None
        o_ref = refs[n_in]
        part = lax.dot_general(a_ref[...].astype(BF16), b_ref[...].astype(BF16), dims,
                               preferred_element_type=F32)

        def finish(acc):
            val = acc * scale if scale != 1.0 else acc
            if r_ref is not None:
                val = r_ref[...] + val
            o_ref[...] = val.astype(o_ref.dtype)

        if nk == 1:
            finish(part)
        else:
            acc_ref = refs[-1]
            k = pl.program_id(2)

            @pl.when(k == 0)
            def _():
                acc_ref[...] = part

            @pl.when(k > 0)
            def _():
                acc_ref[...] += part

            @pl.when(k == nk - 1)
            def _():
                finish(acc_ref[...])

    return pl.pallas_call(
        body, name=name, grid=(n_dim // bn, m_dim // bm, nk),
        in_specs=in_specs, out_specs=o_spec, out_shape=out,
        scratch_shapes=[pltpu.VMEM((bm, bn), F32)] if nk > 1 else [],
        compiler_params=_params("parallel", "parallel", "arbitrary"),
    )(*operands)


def _row_block(rows, width, itemsize=4, budget=2 << 20):
    for br in (512, 256, 128, 64, 32, 16, 8):
        if rows % br == 0 and br * width * itemsize <= budget:
            return br
    return rows


def _rms_fwd(h, g, name, dep=None):
    s_len, d = h.shape
    br = _row_block(s_len, d)

    def body(h_ref, g_ref, *rest):
        o_ref = rest[-1]
        x = h_ref[...]
        r = lax.rsqrt(jnp.mean(x * x, axis=-1, keepdims=True) + NORM_EPS)
        o_ref[...] = (x * r * g_ref[...]).astype(BF16)

    spec = pl.BlockSpec((br, d), lambda i: (i, 0))
    operands = [h, g.reshape(1, d)] + ([dep] if dep is not None else [])
    in_specs = [spec, pl.BlockSpec((1, d), lambda i: (0, 0))]
    if dep is not None:
        in_specs.append(pl.BlockSpec(dep.shape, lambda i: (0, 0)))
    return pl.pallas_call(body, name=name, grid=(s_len // br,), in_specs=in_specs, out_specs=spec,
                          out_shape=jax.ShapeDtypeStruct((s_len, d), BF16),
                          compiler_params=_params("parallel"))(*operands)


def _rms_bwd_rows(x, g, dn):
    r = lax.rsqrt(jnp.mean(x * x, axis=-1, keepdims=True) + NORM_EPS)
    xhat = x * r
    gdn = dn * g
    dx = r * (gdn - xhat * jnp.mean(gdn * xhat, axis=-1, keepdims=True))
    return dx, dn * xhat


def _norm_bwd(h, g, dn, dres, name):
    s_len, d = h.shape
    br = _row_block(s_len, d, budget=1 << 20)

    def body(h_ref, g_ref, dn_ref, dres_ref, dh_ref, dg_ref):
        dx, dg_rows = _rms_bwd_rows(h_ref[...], g_ref[...], dn_ref[...].astype(F32))
        dh_ref[...] = dres_ref[...] + dx

        @pl.when(pl.program_id(0) == 0)
        def _():
            dg_ref[...] = jnp.zeros_like(dg_ref)

        dg_ref[...] += jnp.sum(dg_rows, axis=0, keepdims=True)

    spec = pl.BlockSpec((br, d), lambda i: (i, 0))
    vec = pl.BlockSpec((1, d), lambda i: (0, 0))
    return pl.pallas_call(body, name=name, grid=(s_len // br,),
                          in_specs=[spec, vec, spec, spec], out_specs=[spec, vec],
                          out_shape=[jax.ShapeDtypeStruct((s_len, d), F32),
                                     jax.ShapeDtypeStruct((1, d), F32)],
                          compiler_params=_params("arbitrary"))(h, g.reshape(1, d), dn, dres)


def _sigmoid(x):
    return 1.0 / (1.0 + jnp.exp(-x))


def _swiglu_fwd(z, name):
    s_len, f2 = z.shape
    f = f2 // 2
    br = _row_block(s_len, f2, budget=3 << 20)

    def body(z_ref, a_ref):
        gate = z_ref[:, :f].astype(F32)
        up = z_ref[:, f:].astype(F32)
        a_ref[...] = (gate * _sigmoid(gate) * up).astype(BF16)

    return pl.pallas_call(body, name=name, grid=(s_len // br,),
                          in_specs=[pl.BlockSpec((br, f2), lambda i: (i, 0))],
                          out_specs=pl.BlockSpec((br, f), lambda i: (i, 0)),
                          out_shape=jax.ShapeDtypeStruct((s_len, f), BF16),
                          compiler_params=_params("parallel"))(z)


def _swiglu_bwd(z, da, name):
    s_len, f2 = z.shape
    f = f2 // 2
    br = _row_block(s_len, f2, budget=3 << 20)

    def body(z_ref, da_ref, dz_ref):
        gate = z_ref[:, :f].astype(F32)
        up = z_ref[:, f:].astype(F32)
        d = da_ref[...].astype(F32)
        sig = _sigmoid(gate)
        dz_ref[:, :f] = (d * up * (sig * (1.0 + gate * (1.0 - sig)))).astype(BF16)
        dz_ref[:, f:] = (d * gate * sig).astype(BF16)

    return pl.pallas_call(body, name=name, grid=(s_len // br,),
                          in_specs=[pl.BlockSpec((br, f2), lambda i: (i, 0)),
                                    pl.BlockSpec((br, f), lambda i: (i, 0))],
                          out_specs=pl.BlockSpec((br, f2), lambda i: (i, 0)),
                          out_shape=jax.ShapeDtypeStruct((s_len, f2), BF16),
                          compiler_params=_params("parallel"))(z, da)


def _loss_head(h, g, target, name):
    s_len, d = h.shape
    br = _row_block(s_len, d, budget=1 << 20)

    def body(h_ref, g_ref, t_ref, loss_ref, dh_ref, dg_ref):
        x = h_ref[...]
        gain = g_ref[...]
        r = lax.rsqrt(jnp.mean(x * x, axis=-1, keepdims=True) + NORM_EPS)
        err = x * r * gain - t_ref[...]
        part = 0.5 * jnp.sum(jnp.mean(err * err, axis=-1, keepdims=True), axis=0, keepdims=True)
        dx, dg_rows = _rms_bwd_rows(x, gain, err * (1.0 / d))
        dh_ref[...] = dx

        @pl.when(pl.program_id(0) == 0)
        def _():
            dg_ref[...] = jnp.zeros_like(dg_ref)
            loss_ref[...] = jnp.zeros_like(loss_ref)

        dg_ref[...] += jnp.sum(dg_rows, axis=0, keepdims=True)
        loss_ref[...] += jnp.broadcast_to(part, loss_ref.shape)

    spec = pl.BlockSpec((br, d), lambda i: (i, 0))
    vec = pl.BlockSpec((1, d), lambda i: (0, 0))
    one = pl.BlockSpec((1, LANES), lambda i: (0, 0))
    loss, dh, dg = pl.pallas_call(
        body, name=name, grid=(s_len // br,), in_specs=[spec, vec, spec],
        out_specs=[one, spec, vec],
        out_shape=[jax.ShapeDtypeStruct((1, LANES), F32), jax.ShapeDtypeStruct((s_len, d), F32),
                   jax.ShapeDtypeStruct((1, d), F32)],
        compiler_params=_params("arbitrary"))(h, g.reshape(1, d), target)
    return loss[0, 0], dh, dg


def _adamw(w, g, m, v, name):
    rows, width = w.shape
    br = _row_block(rows, width, budget=1 << 20)
    c1 = 1.0 - ADAM_B1 ** ADAM_STEP
    c2 = 1.0 - ADAM_B2 ** ADAM_STEP

    def body(w_ref, g_ref, m_ref, v_ref, d_ref, nm_ref, nv_ref):
        grad = g_ref[...]
        new_m = ADAM_B1 * m_ref[...] + (1.0 - ADAM_B1) * grad
        new_v = ADAM_B2 * v_ref[...] + (1.0 - ADAM_B2) * (grad * grad)
        d_ref[...] = -ADAM_LR * ((new_m / c1) / (jnp.sqrt(new_v / c2) + ADAM_EPS) + ADAM_WD * w_ref[...])
        nm_ref[...] = new_m
        nv_ref[...] = new_v

    spec = pl.BlockSpec((br, width), lambda i: (i, 0))
    shp = jax.ShapeDtypeStruct(w.shape, F32)
    return pl.pallas_call(body, name=name, grid=(rows // br,), in_specs=[spec] * 4,
                          out_specs=[spec] * 3, out_shape=[shp] * 3,
                          compiler_params=_params("parallel"))(w, g, m, v)


def _gelu(x):
    return 0.5 * x * (1.0 + lax.erf(x * (2.0 ** -0.5)))


def _gelu_grad(x):
    return 0.5 * (1.0 + lax.erf(x * (2.0 ** -0.5))) + x * jnp.exp(-0.5 * x * x) * ((2.0 * math.pi) ** -0.5)


def _tril_mask():
    row = lax.broadcasted_iota(jnp.int32, (BLOCK, BLOCK), 0)
    col = lax.broadcasted_iota(jnp.int32, (BLOCK, BLOCK), 1)
    return col <= row


def _gmlp_specs(s_len, d):
    gw = d // GMLP_GROUPS
    zp = pl.BlockSpec((BLOCK, 2 * d), lambda i: (i, 0))
    row = pl.BlockSpec((BLOCK, d), lambda i: (i, 0))
    vec = pl.BlockSpec((1, d), lambda i: (0, 0))
    ws = pl.BlockSpec((GMLP_GROUPS, BLOCK, BLOCK), lambda i: (0, 0, 0))
    bst = pl.BlockSpec((BLOCK, GMLP_GROUPS), lambda i: (0, 0))
    return gw, zp, row, vec, ws, bst


def _gmlp_fwd(zp, vgain, ws, bs, name):
    s_len, d2 = zp.shape
    d = d2 // 2
    gw, zp_spec, row_spec, vec_spec, ws_spec, bst_spec = _gmlp_specs(s_len, d)

    def body(zp_ref, vg_ref, ws_ref, bst_ref, y_ref):
        u = _gelu(zp_ref[:, :d].astype(F32))
        vv = _gelu(zp_ref[:, d:].astype(F32))
        r = lax.rsqrt(jnp.mean(vv * vv, axis=-1, keepdims=True) + NORM_EPS)
        vn = (vv * r * vg_ref[...]).astype(BF16)
        mask = _tril_mask()
        for g in range(GMLP_GROUPS):
            cols = slice(g * gw, (g + 1) * gw)
            wg = jnp.where(mask, ws_ref[g], 0.0).astype(BF16)
            mixed = jnp.dot(wg, vn[:, cols], preferred_element_type=F32) + bst_ref[:, g:g + 1]
            y_ref[:, cols] = (u[:, cols] * mixed).astype(BF16)

    return pl.pallas_call(body, name=name, grid=(s_len // BLOCK,),
                          in_specs=[zp_spec, vec_spec, ws_spec, bst_spec], out_specs=row_spec,
                          out_shape=jax.ShapeDtypeStruct((s_len, d), BF16),
                          compiler_params=_params("parallel"))(zp, vgain.reshape(1, d), ws, bs.T)


def _gmlp_bwd(zp, dy, vgain, ws, bs, name):
    s_len, d2 = zp.shape
    d = d2 // 2
    gw, zp_spec, row_spec, vec_spec, ws_spec, bst_spec = _gmlp_specs(s_len, d)

    def body(zp_ref, dy_ref, vg_ref, ws_ref, bst_ref, dzp_ref, dws_ref, dbst_ref, dvg_ref, dvn_ref):
        @pl.when(pl.program_id(0) == 0)
        def _():
            dws_ref[...] = jnp.zeros_like(dws_ref)
            dbst_ref[...] = jnp.zeros_like(dbst_ref)
            dvg_ref[...] = jnp.zeros_like(dvg_ref)

        zu = zp_ref[:, :d].astype(F32)
        zv = zp_ref[:, d:].astype(F32)
        u = _gelu(zu)
        vv = _gelu(zv)
        r = lax.rsqrt(jnp.mean(vv * vv, axis=-1, keepdims=True) + NORM_EPS)
        vhat = vv * r
        gain = vg_ref[...]
        vn = (vhat * gain).astype(BF16)
        dyf = dy_ref[...].astype(F32)
        dmixed = dyf * u
        dmixed_b = dmixed.astype(BF16)
        mask = _tril_mask()
        lane = lax.broadcasted_iota(jnp.int32, (BLOCK, GMLP_GROUPS), 1)
        dbs_step = jnp.zeros((BLOCK, GMLP_GROUPS), F32)
        for g in range(GMLP_GROUPS):
            cols = slice(g * gw, (g + 1) * gw)
            wg = jnp.where(mask, ws_ref[g], 0.0).astype(BF16)
            mixed = jnp.dot(wg, vn[:, cols], preferred_element_type=F32) + bst_ref[:, g:g + 1]
            dzp_ref[:, cols] = (dyf[:, cols] * mixed * _gelu_grad(zu[:, cols])).astype(BF16)
            dm = dmixed_b[:, cols]
            dw = lax.dot_general(dm, vn[:, cols], (((1,), (1,)), ((), ())), preferred_element_type=F32)
            dws_ref[g] += jnp.where(mask, dw, 0.0)
            dbs_step = dbs_step + jnp.where(lane == g, jnp.sum(dmixed[:, cols], axis=-1, keepdims=True), 0.0)
            dvn_ref[:, cols] = lax.dot_general(wg, dm, (((0,), (0,)), ((), ())), preferred_element_type=F32)
        dbst_ref[...] += dbs_step
        dvn = dvn_ref[...]
        dvg_ref[...] += jnp.sum(dvn * vhat, axis=0, keepdims=True)
        dvhat = dvn * gain
        dvv = r * (dvhat - vhat * jnp.mean(dvhat * vhat, axis=-1, keepdims=True))
        dzp_ref[:, d:] = (dvv * _gelu_grad(zv)).astype(BF16)

    return pl.pallas_call(
        body, name=name, grid=(s_len // BLOCK,),
        in_specs=[zp_spec, row_spec, vec_spec, ws_spec, bst_spec],
        out_specs=[zp_spec, ws_spec, bst_spec, vec_spec],
        out_shape=[jax.ShapeDtypeStruct((s_len, d2), BF16), jax.ShapeDtypeStruct(ws.shape, F32),
                   jax.ShapeDtypeStruct((BLOCK, GMLP_GROUPS), F32), jax.ShapeDtypeStruct((1, d), F32)],
        scratch_shapes=[pltpu.VMEM((BLOCK, d), F32)],
        compiler_params=_params("arbitrary"))(zp, dy, vgain.reshape(1, d), ws, bs.T)


def _rope_tables(s_len, sign):
    half = ROPE_DIM // 2
    inv_freq = ROPE_THETA ** (-(jnp.arange(half, dtype=F32) * 2.0 / ROPE_DIM))
    ang = jnp.arange(s_len, dtype=F32)[:, None] * inv_freq[None, :]
    cos, sin = jnp.cos(ang), jnp.sin(ang) * sign
    pad = jnp.zeros((s_len, SWA_HEAD_DIM - ROPE_DIM), F32)
    zero = jnp.zeros_like(sin)
    cos_t = jnp.concatenate([cos, cos, pad + 1.0], axis=1)
    sin_up = jnp.concatenate([-sin, zero, pad], axis=1)
    sin_dn = jnp.concatenate([zero, sin, pad], axis=1)
    return [jnp.tile(t, (1, LANES // SWA_HEAD_DIM)) for t in (cos_t, sin_up, sin_dn)]


def _rotate(x, cos_t, sin_up, sin_dn):
    width = x.shape[-1]
    half = ROPE_DIM // 2
    reps = width // cos_t.shape[-1]
    if reps > 1:
        cos_t, sin_up, sin_dn = (jnp.tile(t, (1, reps)) for t in (cos_t, sin_up, sin_dn))
    elif reps == 0:
        cos_t, sin_up, sin_dn = (t[:, :width] for t in (cos_t, sin_up, sin_dn))
    return x * cos_t + pltpu.roll(x, width - half, 1) * sin_up + pltpu.roll(x, half, 1) * sin_dn


def _rope_fwd(qkv, name):
    s_len, total = qkv.shape
    wkv = total // (SWA_GROUP + 2)
    wq = SWA_GROUP * wkv
    br = _row_block(s_len, total, budget=2 << 20)
    tables = _rope_tables(s_len, 1.0)

    def body(q_ref, k_ref, v_ref, c_ref, su_ref, sd_ref, qo_ref, ko_ref, vo_ref):
        t = (c_ref[...], su_ref[...], sd_ref[...])
        qo_ref[...] = _rotate(q_ref[...], *t).astype(BF16)
        ko_ref[...] = _rotate(k_ref[...], *t).astype(BF16)
        vo_ref[...] = v_ref[...].astype(BF16)

    qs = pl.BlockSpec((br, wq), lambda i: (i, 0))
    ks = pl.BlockSpec((br, wkv), lambda i: (i, SWA_GROUP))
    vs = pl.BlockSpec((br, wkv), lambda i: (i, SWA_GROUP + 1))
    ts = pl.BlockSpec((br, LANES), lambda i: (i, 0))
    kv_out = pl.BlockSpec((br, wkv), lambda i: (i, 0))
    return pl.pallas_call(
        body, name=name, grid=(s_len // br,), in_specs=[qs, ks, vs, ts, ts, ts],
        out_specs=[qs, kv_out, kv_out],
        out_shape=[jax.ShapeDtypeStruct((s_len, wq), BF16), jax.ShapeDtypeStruct((s_len, wkv), BF16),
                   jax.ShapeDtypeStruct((s_len, wkv), BF16)],
        compiler_params=_params("parallel"))(qkv, qkv, qkv, *tables)


def _rope_bwd(dq, dk, dv, name):
    s_len, wq = dq.shape
    wkv = dk.shape[1]
    br = _row_block(s_len, wq + 2 * wkv, budget=2 << 20)
    tables = _rope_tables(s_len, -1.0)

    def body(q_ref, k_ref, v_ref, c_ref, su_ref, sd_ref, o_ref):
        t = (c_ref[...], su_ref[...], sd_ref[...])
        o_ref[:, :wq] = _rotate(q_ref[...], *t).astype(BF16)
        o_ref[:, wq:wq + wkv] = _rotate(k_ref[...], *t).astype(BF16)
        o_ref[:, wq + wkv:] = v_ref[...].astype(BF16)

    qs = pl.BlockSpec((br, wq), lambda i: (i, 0))
    kvs = pl.BlockSpec((br, wkv), lambda i: (i, 0))
    ts = pl.BlockSpec((br, LANES), lambda i: (i, 0))
    return pl.pallas_call(
        body, name=name, grid=(s_len // br,), in_specs=[qs, kvs, kvs, ts, ts, ts],
        out_specs=pl.BlockSpec((br, wq + 2 * wkv), lambda i: (i, 0)),
        out_shape=jax.ShapeDtypeStruct((s_len, wq + 2 * wkv), BF16),
        compiler_params=_params("parallel"))(dq, dk, dv, *tables)


def _swa_valid(i):
    row = lax.broadcasted_iota(jnp.int32, (BLOCK, 2 * BLOCK), 0)
    col = lax.broadcasted_iota(jnp.int32, (BLOCK, 2 * BLOCK), 1)
    return (col - BLOCK <= row) & (row < col) & ((col >= BLOCK) | (i > 0))


def _swa_specs(wq, wkv):
    q_spec = pl.BlockSpec((BLOCK, wq), lambda i: (i, 0))
    cur = pl.BlockSpec((BLOCK, wkv), lambda i: (i, 0))
    prev = pl.BlockSpec((BLOCK, wkv), lambda i: (jnp.maximum(i - 1, 0), 0))
    sink = pl.BlockSpec(memory_space=pltpu.SMEM)
    return q_spec, cur, prev, sink


def _swa_probs(q_h, k_cat, valid, sink):
    s = lax.dot_general(q_h, k_cat, (((1,), (1,)), ((), ())), preferred_element_type=F32)
    s = jnp.where(valid, s * (SWA_HEAD_DIM ** -0.5), NEG_INF)
    m = jnp.maximum(jnp.max(s, axis=-1, keepdims=True), sink)
    p = jnp.exp(s - m)
    e_sink = jnp.exp(sink - m)
    denom = jnp.sum(p, axis=-1, keepdims=True) + e_sink
    return p / denom, e_sink / denom


def _swa_fwd(q, k, v, sinks, name):
    s_len, wq = q.shape
    wkv = k.shape[1]
    hd = SWA_HEAD_DIM
    q_spec, cur, prev, sink_spec = _swa_specs(wq, wkv)

    def body(q_ref, kc_ref, kp_ref, vc_ref, vp_ref, sink_ref, o_ref):
        valid = _swa_valid(pl.program_id(0))
        for j in range(wkv // hd):
            lanes = slice(j * hd, (j + 1) * hd)
            k_cat = jnp.concatenate([kp_ref[:, lanes], kc_ref[:, lanes]], axis=0)
            v_cat = jnp.concatenate([vp_ref[:, lanes], vc_ref[:, lanes]], axis=0)
            for hh in range(SWA_GROUP):
                h = j * SWA_GROUP + hh
                pn, _ = _swa_probs(q_ref[:, h * hd:(h + 1) * hd], k_cat, valid, sink_ref[h])
                o_ref[:, h * hd:(h + 1) * hd] = jnp.dot(
                    pn.astype(BF16), v_cat, preferred_element_type=F32).astype(BF16)

    return pl.pallas_call(body, name=name, grid=(s_len // BLOCK,),
                          in_specs=[q_spec, cur, prev, cur, prev, sink_spec], out_specs=q_spec,
                          out_shape=jax.ShapeDtypeStruct((s_len, wq), BF16),
                          compiler_params=_params("parallel"))(q, k, k, v, v, sinks)


def _swa_bwd(q, k, v, sinks, do, name):
    s_len, wq = q.shape
    wkv = k.shape[1]
    hd = SWA_HEAD_DIM
    q_spec, cur, prev, sink_spec = _swa_specs(wq, wkv)
    full = pl.BlockSpec((s_len, wkv), lambda i: (0, 0))
    one = pl.BlockSpec((1, LANES), lambda i: (0, 0))
    scale = hd ** -0.5

    def body(q_ref, kc_ref, kp_ref, vc_ref, vp_ref, sink_ref, do_ref, dq_ref, dk_ref, dv_ref, ds_ref):
        i = pl.program_id(0)

        @pl.when(i == 0)
        def _():
            dk_ref[...] = jnp.zeros_like(dk_ref)
            dv_ref[...] = jnp.zeros_like(dv_ref)
            ds_ref[...] = jnp.zeros_like(ds_ref)

        valid = _swa_valid(i)
        lane = lax.broadcasted_iota(jnp.int32, (1, LANES), 1)
        dsink_step = jnp.zeros((1, LANES), F32)
        rows_prev = pl.ds(pl.multiple_of(jnp.maximum(i - 1, 0) * BLOCK, BLOCK), BLOCK)
        rows_cur = pl.ds(pl.multiple_of(i * BLOCK, BLOCK), BLOCK)
        for j in range(wkv // hd):
            lanes = slice(j * hd, (j + 1) * hd)
            k_cat = jnp.concatenate([kp_ref[:, lanes], kc_ref[:, lanes]], axis=0)
            v_cat = jnp.concatenate([vp_ref[:, lanes], vc_ref[:, lanes]], axis=0)
            dk_cat = jnp.zeros((2 * BLOCK, hd), F32)
            dv_cat = jnp.zeros((2 * BLOCK, hd), F32)
            for hh in range(SWA_GROUP):
                h = j * SWA_GROUP + hh
                q_h = q_ref[:, h * hd:(h + 1) * hd]
                do_h = do_ref[:, h * hd:(h + 1) * hd]
                pn, p_sink = _swa_probs(q_h, k_cat, valid, sink_ref[h])
                dpn = lax.dot_general(do_h, v_cat, (((1,), (1,)), ((), ())), preferred_element_type=F32)
                delta = jnp.sum(dpn * pn, axis=-1, keepdims=True)
                ds = (pn * (dpn - delta) * scale).astype(BF16)
                dsink_h = -jnp.sum(p_sink * delta, axis=0, keepdims=True)
                dsink_step = dsink_step + jnp.where(lane == h, dsink_h, 0.0)
                dq_ref[:, h * hd:(h + 1) * hd] = jnp.dot(ds, k_cat, preferred_element_type=F32)
                dk_cat = dk_cat + lax.dot_general(ds, q_h, (((0,), (0,)), ((), ())),
                                                  preferred_element_type=F32)
                dv_cat = dv_cat + lax.dot_general(pn.astype(BF16), do_h, (((0,), (0,)), ((), ())),
                                                  preferred_element_type=F32)
            dk_ref[rows_prev, lanes] += dk_cat[:BLOCK]
            dk_ref[rows_cur, lanes] += dk_cat[BLOCK:]
            dv_ref[rows_prev, lanes] += dv_cat[:BLOCK]
            dv_ref[rows_cur, lanes] += dv_cat[BLOCK:]
        ds_ref[...] += dsink_step

    return pl.pallas_call(
        body, name=name, grid=(s_len // BLOCK,),
        in_specs=[q_spec, cur, prev, cur, prev, sink_spec, q_spec],
        out_specs=[q_spec, full, full, one],
        out_shape=[jax.ShapeDtypeStruct((s_len, wq), F32), jax.ShapeDtypeStruct((s_len, wkv), F32),
                   jax.ShapeDtypeStruct((s_len, wkv), F32), jax.ShapeDtypeStruct((1, LANES), F32)],
        compiler_params=_params("arbitrary"))(q, k, k, v, v, sinks, do)


def _log_sigmoid(x):
    return jnp.minimum(x, 0.0) - jnp.log(1.0 + jnp.exp(-jnp.abs(x)))


def _tri_ones(lower):
    row = lax.broadcasted_iota(jnp.int32, (BLOCK, BLOCK), 0)
    col = lax.broadcasted_iota(jnp.int32, (BLOCK, BLOCK), 1)
    return jnp.where((col <= row) if lower else (col >= row), 1.0, 0.0).astype(F32)


def _fox_decay(proj, bf_row, fl_block, name):
    s_len = proj.shape[0]
    nchunk = s_len // BLOCK

    def body(fl_ref, bf_ref, dec_ref):
        tri = _tri_ones(True)
        carry = jnp.zeros((1, LANES), F32)
        for c in range(nchunk):
            rows = slice(c * BLOCK, (c + 1) * BLOCK)
            log_f = _log_sigmoid(fl_ref[rows, :] + bf_ref[...])
            loc = jnp.dot(tri, log_f, preferred_element_type=F32, precision=lax.Precision.HIGHEST) + carry
            dec_ref[rows, :] = loc
            carry = loc[BLOCK - 1:BLOCK, :]

    return pl.pallas_call(
        body, name=name, grid=(1,),
        in_specs=[pl.BlockSpec((s_len, LANES), lambda i: (0, fl_block)),
                  pl.BlockSpec((1, LANES), lambda i: (0, 0))],
        out_specs=pl.BlockSpec((s_len, LANES), lambda i: (0, 0)),
        out_shape=jax.ShapeDtypeStruct((s_len, LANES), F32),
        compiler_params=_params("arbitrary"))(proj, bf_row)


def _fox_decay_bwd(ddq, ddk, proj, bf_row, fl_block, heads, name):
    s_len = proj.shape[0]
    nchunk = s_len // BLOCK

    def body(ddq_ref, ddk_ref, fl_ref, bf_ref, dfl_ref, dbf_ref):
        tri = _tri_ones(False)
        lane_ok = lax.broadcasted_iota(jnp.int32, (BLOCK, LANES), 1) < heads
        carry = jnp.zeros((1, LANES), F32)
        dbf = jnp.zeros((1, LANES), F32)
        for c in reversed(range(nchunk)):
            rows = slice(c * BLOCK, (c + 1) * BLOCK)
            ddec = ddq_ref[rows, :] + ddk_ref[rows, :]
            dlog = jnp.dot(tri, ddec, preferred_element_type=F32, precision=lax.Precision.HIGHEST) + carry
            carry = dlog[0:1, :]
            dfl = jnp.where(lane_ok, dlog * _sigmoid(-(fl_ref[rows, :] + bf_ref[...])), 0.0)
            dfl_ref[rows, :] = dfl.astype(BF16)
            dbf = dbf + jnp.sum(dfl, axis=0, keepdims=True)
        dbf_ref[...] = dbf

    blk = pl.BlockSpec((s_len, LANES), lambda i: (0, 0))
    one = pl.BlockSpec((1, LANES), lambda i: (0, 0))
    return pl.pallas_call(
        body, name=name, grid=(1,),
        in_specs=[blk, blk, pl.BlockSpec((s_len, LANES), lambda i: (0, fl_block)), one],
        out_specs=[blk, one],
        out_shape=[jax.ShapeDtypeStruct((s_len, LANES), BF16), jax.ShapeDtypeStruct((1, LANES), F32)],
        compiler_params=_params("arbitrary"))(ddq, ddk, proj, bf_row)


def _fox_scores(q, k, decq, deck, i, bq):
    s_len = k.shape[0]
    s = lax.dot_general(q, k, (((1,), (1,)), ((), ())), preferred_element_type=F32)
    s = s * (FOX_HEAD_DIM ** -0.5) + decq - deck
    row = lax.broadcasted_iota(jnp.int32, (bq, s_len), 0) + i * bq
    col = lax.broadcasted_iota(jnp.int32, (bq, s_len), 1)
    s = jnp.where(col <= row, s, NEG_INF)
    p = jnp.exp(s - jnp.max(s, axis=-1, keepdims=True))
    return p / jnp.sum(p, axis=-1, keepdims=True)


def _fox_specs(s_len, heads, bq):
    hd = FOX_HEAD_DIM
    q_spec = pl.BlockSpec((bq, hd), lambda h, i: (i, h))
    k_spec = pl.BlockSpec((s_len, hd), lambda h, i: (0, heads + h))
    v_spec = pl.BlockSpec((s_len, hd), lambda h, i: (0, 2 * heads + h))
    dq_spec = pl.BlockSpec((None, bq, 1), lambda h, i: (h, i, 0))
    dk_spec = pl.BlockSpec((None, 1, s_len), lambda h, i: (h, 0, 0))
    return q_spec, k_spec, v_spec, dq_spec, dk_spec


def _fox_fwd(proj, decq, deck, heads, name):
    s_len = proj.shape[0]
    bq = _pick(s_len, (256, 128))
    q_spec, k_spec, v_spec, dq_spec, dk_spec = _fox_specs(s_len, heads, bq)

    def body(q_ref, k_ref, v_ref, decq_ref, deck_ref, o_ref):
        pn = _fox_scores(q_ref[...].astype(BF16), k_ref[...].astype(BF16), decq_ref[...], deck_ref[...],
                         pl.program_id(1), bq)
        o_ref[...] = jnp.dot(pn.astype(BF16), v_ref[...].astype(BF16),
                             preferred_element_type=F32).astype(BF16)

    return pl.pallas_call(body, name=name, grid=(heads, s_len // bq),
                          in_specs=[q_spec, k_spec, v_spec, dq_spec, dk_spec], out_specs=q_spec,
                          out_shape=jax.ShapeDtypeStruct((s_len, heads * FOX_HEAD_DIM), BF16),
                          compiler_params=_params("parallel", "parallel"))(proj, proj, proj, decq, deck)


def _fox_bwd(proj, decq, deck, do, heads, name):
    s_len = proj.shape[0]
    d = heads * FOX_HEAD_DIM
    bq = _pick(s_len, (256, 128))
    q_spec, k_spec, v_spec, dq_spec, dk_spec = _fox_specs(s_len, heads, bq)
    acc_spec = pl.BlockSpec((s_len, FOX_HEAD_DIM), lambda h, i: (0, h))
    scale = FOX_HEAD_DIM ** -0.5

    def body(q_ref, k_ref, v_ref, decq_ref, deck_ref, do_ref, dq_ref, dk_ref, dv_ref, ddq_ref, ddk_ref):
        i = pl.program_id(1)

        @pl.when(i == 0)
        def _():
            dk_ref[...] = jnp.zeros_like(dk_ref)
            dv_ref[...] = jnp.zeros_like(dv_ref)
            ddk_ref[...] = jnp.zeros_like(ddk_ref)

        q = q_ref[...].astype(BF16)
        k = k_ref[...].astype(BF16)
        do_b = do_ref[...]
        pn = _fox_scores(q, k, decq_ref[...], deck_ref[...], i, bq)
        dpn = lax.dot_general(do_b, v_ref[...].astype(BF16), (((1,), (1,)), ((), ())),
                              preferred_element_type=F32)
        ds = pn * (dpn - jnp.sum(dpn * pn, axis=-1, keepdims=True))
        ddq_ref[...] = jnp.sum(ds, axis=-1, keepdims=True)
        ddk_ref[...] -= jnp.sum(ds, axis=0, keepdims=True)
        ds_b = (ds * scale).astype(BF16)
        dq_ref[...] = jnp.dot(ds_b, k, preferred_element_type=F32).astype(BF16)
        dk_ref[...] += lax.dot_general(ds_b, q, (((0,), (0,)), ((), ())), preferred_element_type=F32)
        dv_ref[...] += lax.dot_general(pn.astype(BF16), do_b, (((0,), (0,)), ((), ())),
                                       preferred_element_type=F32)

    return pl.pallas_call(
        body, name=name, grid=(heads, s_len // bq),
        in_specs=[q_spec, k_spec, v_spec, dq_spec, dk_spec, q_spec],
        out_specs=[q_spec, acc_spec, acc_spec, dq_spec, dk_spec],
        out_shape=[jax.ShapeDtypeStruct((s_len, d), BF16), jax.ShapeDtypeStruct((s_len, d), F32),
                   jax.ShapeDtypeStruct((s_len, d), F32), jax.ShapeDtypeStruct((heads, s_len, 1), F32),
                   jax.ShapeDtypeStruct((heads, 1, s_len), F32)],
        compiler_params=_params("parallel", "arbitrary"))(proj, proj, proj, decq, deck, do)


def _place():
    x, y, c = lax.axis_index("x"), lax.axis_index("y"), lax.axis_index("c")
    chips = [(1 - x, y), (x, 1 - y), (1 - x, 1 - y)]
    return x, y, c, chips


def _remote(src, dst, send_sems, recv_sems, idx, to):
    return pltpu.make_async_remote_copy(src_ref=src, dst_ref=dst, send_sem=send_sems.at[idx],
                                        recv_sem=recv_sems.at[idx], device_id=to, device_id_type=MESH)


def _row_chunks(rows, want):
    for k in (want, want // 2, want // 4):
        if k >= 1 and rows % (16 * k) == 0:
            return [(j * (rows // k), rows // k) for j in range(k)]
    return [(0, rows)]


def _chunked(src_of, dst_of, rows, want, send_sems, recv_sems, idx, to):
    for start, size in _row_chunks(rows, want):
        _remote(src_of(start, size), dst_of(start, size), send_sems, recv_sems, idx, to).start()
    return _remote(src_of(0, rows), dst_of(0, rows), send_sems, recv_sems, idx, to)


D2D_CHUNKS = 8


def _cast_into_slot(w, me, name):
    rows, width = w.shape
    br = _row_block(rows, width)

    def body(me_ref, w_ref, o_ref):
        o_ref[...] = w_ref[...].astype(BF16)

    return pl.pallas_call(
        body, name=name,
        grid_spec=pltpu.PrefetchScalarGridSpec(
            num_scalar_prefetch=1, grid=(rows // br,),
            in_specs=[pl.BlockSpec((br, width), lambda i, me_ref: (i, 0))],
            out_specs=pl.BlockSpec((None, br, width), lambda i, me_ref: (me_ref[0], i, 0))),
        out_shape=jax.ShapeDtypeStruct((N_CHIPS, rows, width), BF16),
        compiler_params=_params("parallel"))(me, w)


def _hbm(arr):
    return pltpu.with_memory_space_constraint(arr, pltpu.HBM)


def _token_shape():
    return jax.ShapeDtypeStruct((8, LANES), F32)


def _gather_start(bufs, after, name):
    n = len(bufs)

    def body(*refs):
        send, recv = refs[n + 1], refs[n + 2]
        outs = refs[n + 3:2 * n + 3]
        token = refs[2 * n + 3]
        x, y, c, chips = _place()
        me = 2 * x + y
        for t in range(n):
            half = outs[t].shape[1] // 2
            mine = outs[t].at[me, pl.ds(c * half, half)]
            for r, chip in enumerate(chips):
                _remote(mine, mine, send, recv, 3 * t + r, (*chip, c)).start()
        token[...] = jnp.zeros_like(token)

    res = pl.pallas_call(
        body, name=name, in_specs=[HBM] * n + [pl.BlockSpec(memory_space=pl.ANY)],
        out_specs=[SEM, SEM] + [HBM] * n + [pl.BlockSpec(memory_space=pltpu.VMEM)],
        out_shape=[pltpu.SemaphoreType.DMA((3 * n,)), pltpu.SemaphoreType.DMA((3 * n,))]
        + [pltpu.HBM(b.shape, b.dtype) for b in bufs] + [_token_shape()],
        input_output_aliases={t: 2 + t for t in range(n)},
        compiler_params=pltpu.CompilerParams(has_side_effects=EFFECT),
    )(*[_hbm(b) for b in bufs], after)
    return res[0], res[1], res[2:2 + n], res[2 + n]


def _gather_wait(send, recv, bufs, after, name):
    n = len(bufs)

    def body(*refs):
        send_ref, recv_ref = refs[n], refs[n + 1]
        outs = refs[n + 3:2 * n + 3]
        x, y, c, chips = _place()
        me = 2 * x + y
        for t in range(n):
            half = outs[t].shape[1] // 2
            mine = outs[t].at[me, pl.ds(c * half, half)]
            for r, chip in enumerate(chips):
                block = outs[t].at[2 * chip[0] + chip[1], pl.ds(c * half, half)]
                cp = _remote(mine, block, send_ref, recv_ref, 3 * t + r, (*chip, c))
                cp.wait_send()
                cp.wait_recv()

    return pl.pallas_call(
        body, name=name, in_specs=[HBM] * n + [SEM, SEM, pl.BlockSpec(memory_space=pl.ANY)],
        out_specs=[HBM] * n, out_shape=[pltpu.HBM(b.shape, b.dtype) for b in bufs],
        input_output_aliases={t: t for t in range(n)},
        compiler_params=pltpu.CompilerParams(has_side_effects=EFFECT),
    )(*bufs, send, recv, after)


def _gather_forward(bufs, name):
    n = len(bufs)

    def body(*refs):
        outs = refs[n:2 * n]
        send, recv = refs[2 * n:]
        x, y, c, chips = _place()
        sibling = (x, y, 1 - c)
        started = []
        for t in range(n):
            half = outs[t].shape[1] // 2
            for r, chip in enumerate(chips):
                slot = 2 * chip[0] + chip[1]
                part = lambda s, z, t=t, slot=slot, half=half: outs[t].at[slot, pl.ds(c * half + s, z)]
                started.append(_chunked(part, part, half, D2D_CHUNKS, send, recv, 3 * t + r, sibling))
        for t in range(n):
            half = outs[t].shape[1] // 2
            for r, chip in enumerate(chips):
                block = outs[t].at[2 * chip[0] + chip[1], pl.ds((1 - c) * half, half)]
                _remote(block, block, send, recv, 3 * t + r, sibling).wait_recv()
        for cp in started:
            cp.wait_send()

    return pl.pallas_call(
        body, name=name, in_specs=[HBM] * n, out_specs=[HBM] * n,
        out_shape=[jax.ShapeDtypeStruct(b.shape, b.dtype) for b in bufs],
        input_output_aliases={t: t for t in range(n)},
        scratch_shapes=[pltpu.SemaphoreType.DMA((3 * n,))] * 2,
    )(*bufs)


def _pair_swap(grads, name):
    n = len(grads)

    def body(*refs):
        ins, got = refs[:n], refs[n:2 * n]
        send, recv = refs[2 * n:]
        x, y, c, _ = _place()
        waits = []
        for t in range(n):
            half = ins[t].shape[1] // 2
            for j in range(N_CHIPS):
                src = lambda s, z, t=t, j=j, half=half: ins[t].at[j, pl.ds((1 - c) * half + s, z)]
                dst = lambda s, z, t=t, j=j: got[t].at[j, pl.ds(s, z)]
                waits.append(_chunked(src, dst, half, 2, send, recv, N_CHIPS * t + j, (x, y, 1 - c)))
        for cp in waits:
            cp.wait()

    return pl.pallas_call(
        body, name=name, in_specs=[HBM] * n, out_specs=[HBM] * n,
        out_shape=[jax.ShapeDtypeStruct((g.shape[0], g.shape[1] // 2, g.shape[2]), g.dtype) for g in grads],
        scratch_shapes=[pltpu.SemaphoreType.DMA((N_CHIPS * n,))] * 2,
    )(*grads)


def _scatter_start(parts, name):
    n = len(parts)
    lands = [lax.empty((N_CHIPS - 1, *p.shape[1:]), p.dtype) for p in parts]

    def body(*refs):
        send, recv = refs[2 * n], refs[2 * n + 1]
        src = refs[2 * n + 2:3 * n + 2]
        dst = refs[3 * n + 2:4 * n + 2]
        token = refs[4 * n + 2]
        x, y, c, chips = _place()
        for t in range(n):
            for r, chip in enumerate(chips):
                _remote(src[t].at[2 * chip[0] + chip[1]], dst[t].at[r], send, recv, 3 * t + r, (*chip, c)).start()
        token[...] = jnp.zeros_like(token)

    res = pl.pallas_call(
        body, name=name, in_specs=[HBM] * (2 * n),
        out_specs=[SEM, SEM] + [HBM] * (2 * n) + [pl.BlockSpec(memory_space=pltpu.VMEM)],
        out_shape=[pltpu.SemaphoreType.DMA((3 * n,)), pltpu.SemaphoreType.DMA((3 * n,))]
        + [pltpu.HBM(a.shape, a.dtype) for a in parts + lands] + [_token_shape()],
        input_output_aliases={t: 2 + t for t in range(2 * n)},
        compiler_params=pltpu.CompilerParams(has_side_effects=EFFECT),
    )(*[_hbm(a) for a in parts + lands])
    return res[0], res[1], res[2:2 + n], res[2 + n:2 + 2 * n], res[2 + 2 * n]


def _scatter_wait(send, recv, parts, lands, after, name):
    n = len(parts)

    def body(*refs):
        send_ref, recv_ref = refs[2 * n], refs[2 * n + 1]
        src = refs[2 * n + 3:3 * n + 3]
        dst = refs[3 * n + 3:4 * n + 3]
        x, y, c, chips = _place()
        for t in range(n):
            for r, chip in enumerate(chips):
                cp = _remote(src[t].at[2 * chip[0] + chip[1]], dst[t].at[r], send_ref, recv_ref, 3 * t + r,
                             (*chip, c))
                cp.wait_send()
                cp.wait_recv()

    res = pl.pallas_call(
        body, name=name, in_specs=[HBM] * (2 * n) + [SEM, SEM, pl.BlockSpec(memory_space=pl.ANY)],
        out_specs=[HBM] * (2 * n), out_shape=[pltpu.HBM(a.shape, a.dtype) for a in list(parts) + list(lands)],
        input_output_aliases={t: t for t in range(2 * n)},
        compiler_params=pltpu.CompilerParams(has_side_effects=EFFECT),
    )(*parts, *lands, send, recv, after)
    return res[:n], res[n:]


def _pair_exchange(halves, name):
    n = len(halves)

    def body(*refs):
        ins, got = refs[:n], refs[n:2 * n]
        send, recv = refs[2 * n:]
        x, y, c, _ = _place()
        waits = []
        for t in range(n):
            src = lambda s, z, t=t: ins[t].at[pl.ds(s, z)]
            dst = lambda s, z, t=t: got[t].at[pl.ds(s, z)]
            waits.append(_chunked(src, dst, ins[t].shape[0], D2D_CHUNKS, send, recv, t, (x, y, 1 - c)))
        for cp in waits:
            cp.wait()

    return pl.pallas_call(
        body, name=name, in_specs=[HBM] * n, out_specs=[HBM] * n,
        out_shape=[jax.ShapeDtypeStruct(h.shape, h.dtype) for h in halves],
        scratch_shapes=[pltpu.SemaphoreType.DMA((n,))] * 2,
    )(*halves)


def _add_pair(grad, got, c, name):
    _, half, width = got.shape
    br = _row_block(half, width, itemsize=2, budget=1 << 20)
    nb = half // br

    def body(c_ref, a_ref, b_ref, o_ref):
        o_ref[...] = (a_ref[...].astype(F32) + b_ref[...].astype(F32)).astype(BF16)

    spec = pl.BlockSpec((None, br, width), lambda j, i, c_ref: (j, i, 0))
    mine = pl.BlockSpec((None, br, width), lambda j, i, c_ref: (j, c_ref[0] * nb + i, 0))
    return pl.pallas_call(
        body, name=name,
        grid_spec=pltpu.PrefetchScalarGridSpec(num_scalar_prefetch=1, grid=(N_CHIPS, nb),
                                               in_specs=[mine, spec], out_specs=spec),
        out_shape=jax.ShapeDtypeStruct(got.shape, BF16),
        compiler_params=_params("parallel", "parallel"))(c, grad, got)


def _sum_chips(pair, others, me, name):
    _, rows, width = pair.shape
    br = _row_block(rows, width, itemsize=4, budget=1 << 20)

    def body(me_ref, p_ref, o3_ref, o_ref):
        acc = p_ref[...].astype(F32)
        for r in range(N_CHIPS - 1):
            acc = acc + o3_ref[r].astype(F32)
        o_ref[...] = acc

    return pl.pallas_call(
        body, name=name,
        grid_spec=pltpu.PrefetchScalarGridSpec(
            num_scalar_prefetch=1, grid=(rows // br,),
            in_specs=[pl.BlockSpec((None, br, width), lambda i, me_ref: (me_ref[0], i, 0)),
                      pl.BlockSpec((N_CHIPS - 1, br, width), lambda i, me_ref: (0, i, 0))],
            out_specs=pl.BlockSpec((br, width), lambda i, me_ref: (i, 0))),
        out_shape=jax.ShapeDtypeStruct((rows, width), F32),
        compiler_params=_params("parallel"))(me, pair, others)


def _adamw_halves(w, mine, theirs, m, v, c, name):
    rows, width = w.shape
    half = rows // 2
    br = _row_block(half, width, budget=1 << 20)
    nb = half // br
    c1 = 1.0 - ADAM_B1 ** ADAM_STEP
    c2 = 1.0 - ADAM_B2 ** ADAM_STEP

    def body(c_ref, w_ref, a_ref, b_ref, m_ref, v_ref, g_ref, d_ref, nm_ref, nv_ref):
        grad = jnp.where(pl.program_id(0) == c_ref[0], a_ref[...], b_ref[...])
        new_m = ADAM_B1 * m_ref[...] + (1.0 - ADAM_B1) * grad
        new_v = ADAM_B2 * v_ref[...] + (1.0 - ADAM_B2) * (grad * grad)
        g_ref[...] = grad
        d_ref[...] = -ADAM_LR * ((new_m / c1) / (jnp.sqrt(new_v / c2) + ADAM_EPS) + ADAM_WD * w_ref[...])
        nm_ref[...] = new_m
        nv_ref[...] = new_v

    full = pl.BlockSpec((br, width), lambda h, i, c_ref: (h * nb + i, 0))
    part = pl.BlockSpec((br, width), lambda h, i, c_ref: (i, 0))
    shp = jax.ShapeDtypeStruct(w.shape, F32)
    return pl.pallas_call(
        body, name=name,
        grid_spec=pltpu.PrefetchScalarGridSpec(num_scalar_prefetch=1, grid=(2, nb),
                                               in_specs=[full, part, part, full, full], out_specs=[full] * 4),
        out_shape=[shp] * 4,
        compiler_params=_params("parallel", "parallel"))(c, w, mine, theirs, m, v)


def _allreduce_small(part, name):
    rows = part.shape[0]

    def body(p_ref, o_ref, all_ref, send, recv):
        x, y, c, _ = _place()
        me = 4 * x + 2 * y + c
        all_ref[me] = p_ref[...]
        copies = []
        for r in range(1, N_DEV):
            to = (x ^ (r >> 2), y ^ ((r >> 1) & 1), c ^ (r & 1))
            cp = _remote(p_ref, all_ref.at[me], send, recv, r - 1, to)
            cp.start()
            copies.append(cp)
        for r in range(1, N_DEV):
            frm = (x ^ (r >> 2), y ^ ((r >> 1) & 1), c ^ (r & 1))
            slot = all_ref.at[4 * frm[0] + 2 * frm[1] + frm[2]]
            _remote(slot, slot, send, recv, r - 1, frm).wait_recv()
        for cp in copies:
            cp.wait_send()
        acc = all_ref[0]
        for dev in range(1, N_DEV):
            acc = acc + all_ref[dev]
        o_ref[...] = acc

    vmem = pl.BlockSpec(memory_space=pltpu.VMEM)
    return pl.pallas_call(
        body, name=name, in_specs=[vmem], out_specs=vmem,
        out_shape=jax.ShapeDtypeStruct(part.shape, F32),
        scratch_shapes=[pltpu.VMEM((N_DEV, rows, LANES), F32), pltpu.SemaphoreType.DMA((N_DEV - 1,)),
                        pltpu.SemaphoreType.DMA((N_DEV - 1,))],
        compiler_params=pltpu.CompilerParams(vmem_limit_bytes=VMEM_LIMIT),
    )(part)


INPUT_NAMES = None


def _weight_names():
    names = []
    for i, kind in enumerate(("gmlp", "swa", "fox", "gmlp")):
        p = f"l{i}_"
        names += [p + "ffn1_norm", p + "ffn1_wi", p + "ffn1_wo", p + "mix_norm", p + "mix_win"]
        if kind == "gmlp":
            names += [p + "gmlp_vnorm", p + "gmlp_ws", p + "gmlp_bs"]
        elif kind == "swa":
            names += [p + "swa_sinks"]
        else:
            names += [p + "fox_bf"]
        names += [p + "mix_wout", p + "ffn2_norm", p + "ffn2_wi", p + "ffn2_wo"]
    return names + ["final_norm"]


WEIGHTS = _weight_names()
MIXERS = ("gmlp", "swa", "fox", "gmlp")
BIG = ("ffn1_wi", "ffn1_wo", "mix_win", "mix_wout", "ffn2_wi", "ffn2_wo")


def _ffn_fwd(h, gain, wi, wo, tag, dep=None):
    n = _rms_fwd(h, gain, tag + "_norm", dep=dep)
    z = _matmul(n, wi, name=tag + "_up", out_dtype=BF16)
    a = _swiglu_fwd(z, tag + "_act")
    f, d = wo.shape[0] * wo.shape[1], wo.shape[2]
    out = _matmul(a, wo.reshape(f, d), name=tag + "_down", out_dtype=F32, scale=0.5, resid=h)
    return out, (h, n, z, a)


def _ffn_bwd(dout, saved, gain, wi, wo, tag, dep=None):
    h, n, z, a = saved
    f, d = wo.shape[0] * wo.shape[1], wo.shape[2]
    da = _matmul(dout, wo.reshape(f, d), tb=True, name=tag + "_bdown", out_dtype=BF16, scale=0.5, dep=dep)
    dwo = _matmul(a, dout, ta=True, name=tag + "_gdown", out_dtype=BF16, scale=0.5, dep=dep)
    dz = _swiglu_bwd(z, da, tag + "_bact")
    dn = _matmul(dz, wi, tb=True, name=tag + "_bup", out_dtype=F32)
    dwi = _matmul(n, dz, ta=True, name=tag + "_gup", out_dtype=BF16, out_shards=N_CHIPS)
    dh, dgain = _norm_bwd(h, gain, dn, dout, tag + "_bnorm")
    return dh, dgain, dwi, dwo.reshape(wo.shape)


def _natural(w_sharded, pad_to):
    ns, rows, csh = w_sharded.shape
    nat = jnp.transpose(w_sharded, (1, 0, 2)).reshape(rows, ns * csh)
    extra = (-nat.shape[1]) % pad_to
    return jnp.pad(nat, ((0, 0), (0, extra))) if extra else nat


def _mixer_fwd(kind, h, p, tag, dep=None):
    s_len, d = h.shape
    n = _rms_fwd(h, p["mix_norm"], tag + "_norm", dep=dep)
    wout = p["mix_wout"].reshape(d, d)
    if kind == "gmlp":
        zp = _matmul(n, p["mix_win"], name=tag + "_in", out_dtype=BF16)
        y = _gmlp_fwd(zp, p["gmlp_vnorm"], p["gmlp_ws"], p["gmlp_bs"], tag + "_gate")
        saved = (h, n, zp, y)
    elif kind == "swa":
        qkv = _matmul(n, p["mix_win"], name=tag + "_in", out_dtype=F32)
        q, k, v = _rope_fwd(qkv, tag + "_rope")
        y = _swa_fwd(q, k, v, p["swa_sinks"], tag + "_attn")
        saved = (h, n, q, k, v, y)
    else:
        heads = d // FOX_HEAD_DIM
        win = _natural(p["mix_win"], LANES)
        proj = _matmul(n, win, name=tag + "_in", out_dtype=F32)
        bf_row = jnp.pad(p["fox_bf"], (0, LANES - heads)).reshape(1, LANES)
        dec = _fox_decay(proj, bf_row, 3 * heads, tag + "_decay")
        dec_t = dec[:, :heads].T
        decq, deck = dec_t.reshape(heads, s_len, 1), dec_t.reshape(heads, 1, s_len)
        y = _fox_fwd(proj, decq, deck, heads, tag + "_attn")
        saved = (h, n, win, proj, bf_row, decq, deck, y)
    out = _matmul(y, wout, name=tag + "_out", out_dtype=F32, resid=h)
    return out, saved


def _mixer_bwd(kind, dout, saved, p, tag, dep=None):
    h, n = saved[0], saved[1]
    y = saved[-1]
    s_len, d = h.shape
    wout = p["mix_wout"].reshape(d, d)
    grads = {}
    dy = _matmul(dout, wout, tb=True, name=tag + "_bout", out_dtype=BF16, dep=dep)
    grads["mix_wout"] = _matmul(y, dout, ta=True, name=tag + "_gout", out_dtype=BF16,
                                dep=dep).reshape(p["mix_wout"].shape)
    if kind == "gmlp":
        zp = saved[2]
        dzp, dws, dbst, dvg = _gmlp_bwd(zp, dy, p["gmlp_vnorm"], p["gmlp_ws"], p["gmlp_bs"], tag + "_bgate")
        grads.update(gmlp_ws=dws, gmlp_bs=dbst.T, gmlp_vnorm=dvg.reshape(d))
        dn = _matmul(dzp, p["mix_win"], tb=True, name=tag + "_bin", out_dtype=F32)
        grads["mix_win"] = _matmul(n, dzp, ta=True, name=tag + "_gin", out_dtype=BF16, out_shards=N_CHIPS)
    elif kind == "swa":
        q, k, v = saved[2:5]
        dq, dk, dv, dsinks = _swa_bwd(q, k, v, p["swa_sinks"], dy, tag + "_battn")
        grads["swa_sinks"] = dsinks[0, :p["swa_sinks"].shape[0]]
        dqkv = _rope_bwd(dq, dk, dv, tag + "_brope")
        dn = _matmul(dqkv, p["mix_win"], tb=True, name=tag + "_bin", out_dtype=F32)
        grads["mix_win"] = _matmul(n, dqkv, ta=True, name=tag + "_gin", out_dtype=BF16, out_shards=N_CHIPS)
    else:
        win, proj, bf_row, decq, deck = saved[2:7]
        heads = d // FOX_HEAD_DIM
        dq, dk, dv, ddq, ddk = _fox_bwd(proj, decq, deck, dy, heads, tag + "_battn")
        widen = lambda t: jnp.pad(t.reshape(heads, s_len).T, ((0, 0), (0, LANES - heads)))
        dfl, dbf = _fox_decay_bwd(widen(ddq), widen(ddk), proj, bf_row, 3 * heads, heads, tag + "_bdecay")
        grads["fox_bf"] = dbf[0, :heads]
        dproj = jnp.concatenate([dq, dk.astype(BF16), dv.astype(BF16), dfl], axis=1)
        dn = _matmul(dproj, win, tb=True, name=tag + "_bin", out_dtype=F32)
        dwin = _matmul(n, dproj, ta=True, name=tag + "_gin", out_dtype=BF16)
        ns, rows, csh = p["mix_win"].shape
        grads["mix_win"] = jnp.transpose(dwin[:, :ns * csh].reshape(rows, ns, csh), (1, 0, 2))
    dh, dgain = _norm_bwd(h, p["mix_norm"], dn, dout, tag + "_bnorm")
    grads["mix_norm"] = dgain.reshape(d)
    return dh, grads


def _pack_small(arrays):
    flat = jnp.concatenate([a.reshape(-1).astype(F32) for a in arrays])
    pad = (-flat.shape[0]) % (512 * LANES)
    return jnp.pad(flat, (0, pad)).reshape(-1, LANES)


def _unpack_small(packed, like):
    flat, out, pos = packed.reshape(-1), [], 0
    for a in like:
        out.append(flat[pos:pos + a.size].reshape(a.shape))
        pos += a.size
    return out


def _step(inp):
    x, target = inp["x"][0], inp["loss_target"][0]
    d = x.shape[1]

    core = lax.axis_index("c").astype(jnp.int32).reshape(1)
    chip = (2 * lax.axis_index("x") + lax.axis_index("y")).astype(jnp.int32).reshape(1)

    groups = []
    for i in range(len(MIXERS)):
        groups += [(i, "ffn1", [f"l{i}_ffn1_wi", f"l{i}_ffn1_wo"]), (i, "mix", [f"l{i}_mix_win", f"l{i}_mix_wout"]),
                   (i, "ffn2", [f"l{i}_ffn2_wi", f"l{i}_ffn2_wo"])]

    def layer_params(i, full):
        p = {nm[len(f"l{i}_"):]: inp[nm] for nm in WEIGHTS if nm.startswith(f"l{i}_")}
        p.update({nm[len(f"l{i}_"):]: w for nm, w in full.items()})
        return p

    bufs = [[_cast_into_slot(inp[nm], chip, nm + "_cast") for nm in names] for _, _, names in groups]
    started = _gather_start(bufs[0], x, "g0_gather_start")
    h, saved, fulls = x, [], []
    for g, (i, part, names) in enumerate(groups):
        send, recv, thru, _ = started
        landed = _gather_wait(send, recv, thru, h, f"g{g}_gather_wait")
        full = dict(zip(names, _gather_forward(landed, f"g{g}_gather_forward")))
        dep = None
        if g + 1 < len(groups):
            started = _gather_start(bufs[g + 1], full[names[0]], f"g{g + 1}_gather_start")
            dep = started[3]
        p = layer_params(i, full)
        if part == "mix":
            h, s = _mixer_fwd(MIXERS[i], h, p, f"l{i}_mix", dep=dep)
        else:
            h, s = _ffn_fwd(h, p[part + "_norm"], p[part + "_wi"], p[part + "_wo"], f"l{i}_{part}", dep=dep)
        saved.append(s)
        fulls.append(full)
    loss_part, dh, dfinal = _loss_head(h, inp["final_norm"], target, "loss_head")
    loss = lax.psum(loss_part, ("x", "y", "c"))

    small_grads = {"final_norm": dfinal.reshape(d)}
    outs = {}
    def finish(pending, after):
        g, names, send, recv, pair, lands = pending
        pair, others = _scatter_wait(send, recv, pair, lands, after, f"g{g}_rs_wait")
        halves = [_sum_chips(p, o, chip, nm + "_rs_sum") for p, o, nm in zip(pair, others, names)]
        theirs = _pair_exchange(halves, f"g{g}_rs_join")
        for nm, mine, other in zip(names, halves, theirs):
            outs[nm] = tuple(_adamw_halves(inp[nm], mine, other, inp["m_" + nm], inp["v_" + nm], core,
                                           nm + "_adamw"))

    pending, dep = None, None
    for g in reversed(range(len(groups))):
        i, part, names = groups[g]
        p = layer_params(i, fulls[g])
        if part == "mix":
            dh, mg = _mixer_bwd(MIXERS[i], dh, saved[g], p, f"l{i}_mix", dep=dep)
            grads = [mg.pop("mix_win"), mg.pop("mix_wout")]
            small_grads.update({f"l{i}_{key}": val for key, val in mg.items()})
        else:
            dh, g_norm, dwi, dwo = _ffn_bwd(dh, saved[g], p[part + "_norm"], p[part + "_wi"], p[part + "_wo"],
                                            f"l{i}_{part}", dep=dep)
            small_grads[f"l{i}_{part}_norm"] = g_norm.reshape(d)
            grads = [dwi, dwo]
        if pending is not None:
            finish(pending, dh)
        got = _pair_swap(grads, f"g{g}_rs_pair")
        pair = [_add_pair(gr, b, core, nm + "_rs_add") for gr, b, nm in zip(grads, got, names)]
        send, recv, pair, lands, dep = _scatter_start(pair, f"g{g}_rs_start")
        pending = (g, names, send, recv, pair, lands)
    finish(pending, dh)

    small_names = [nm for nm in WEIGHTS if nm not in outs]
    total = _allreduce_small(_pack_small([small_grads[nm] for nm in small_names]), "small_allreduce")
    like = [inp[nm] for nm in small_names]
    upd = _adamw(_pack_small(like), total, _pack_small([inp["m_" + nm] for nm in small_names]),
                 _pack_small([inp["v_" + nm] for nm in small_names]), "small_adamw")
    unpacked = [_unpack_small(t, like) for t in (total, *upd)]
    for k, nm in enumerate(small_names):
        outs[nm] = tuple(u[k] for u in unpacked)

    result = [loss, dh[None]]
    for part in range(4):
        result += [outs[nm][part] for nm in WEIGHTS]
    return tuple(result)


def kernel(x, l0_ffn1_norm, l0_ffn1_wi, l0_ffn1_wo, l0_mix_norm, l0_mix_win, l0_gmlp_vnorm, l0_gmlp_ws, l0_gmlp_bs, l0_mix_wout, l0_ffn2_norm, l0_ffn2_wi, l0_ffn2_wo, l1_ffn1_norm, l1_ffn1_wi, l1_ffn1_wo, l1_mix_norm, l1_mix_win, l1_swa_sinks, l1_mix_wout, l1_ffn2_norm, l1_ffn2_wi, l1_ffn2_wo, l2_ffn1_norm, l2_ffn1_wi, l2_ffn1_wo, l2_mix_norm, l2_mix_win, l2_fox_bf, l2_mix_wout, l2_ffn2_norm, l2_ffn2_wi, l2_ffn2_wo, l3_ffn1_norm, l3_ffn1_wi, l3_ffn1_wo, l3_mix_norm, l3_mix_win, l3_gmlp_vnorm, l3_gmlp_ws, l3_gmlp_bs, l3_mix_wout, l3_ffn2_norm, l3_ffn2_wi, l3_ffn2_wo, final_norm, loss_target, m_l0_ffn1_norm, m_l0_ffn1_wi, m_l0_ffn1_wo, m_l0_mix_norm, m_l0_mix_win, m_l0_gmlp_vnorm, m_l0_gmlp_ws, m_l0_gmlp_bs, m_l0_mix_wout, m_l0_ffn2_norm, m_l0_ffn2_wi, m_l0_ffn2_wo, m_l1_ffn1_norm, m_l1_ffn1_wi, m_l1_ffn1_wo, m_l1_mix_norm, m_l1_mix_win, m_l1_swa_sinks, m_l1_mix_wout, m_l1_ffn2_norm, m_l1_ffn2_wi, m_l1_ffn2_wo, m_l2_ffn1_norm, m_l2_ffn1_wi, m_l2_ffn1_wo, m_l2_mix_norm, m_l2_mix_win, m_l2_fox_bf, m_l2_mix_wout, m_l2_ffn2_norm, m_l2_ffn2_wi, m_l2_ffn2_wo, m_l3_ffn1_norm, m_l3_ffn1_wi, m_l3_ffn1_wo, m_l3_mix_norm, m_l3_mix_win, m_l3_gmlp_vnorm, m_l3_gmlp_ws, m_l3_gmlp_bs, m_l3_mix_wout, m_l3_ffn2_norm, m_l3_ffn2_wi, m_l3_ffn2_wo, m_final_norm, v_l0_ffn1_norm, v_l0_ffn1_wi, v_l0_ffn1_wo, v_l0_mix_norm, v_l0_mix_win, v_l0_gmlp_vnorm, v_l0_gmlp_ws, v_l0_gmlp_bs, v_l0_mix_wout, v_l0_ffn2_norm, v_l0_ffn2_wi, v_l0_ffn2_wo, v_l1_ffn1_norm, v_l1_ffn1_wi, v_l1_ffn1_wo, v_l1_mix_norm, v_l1_mix_win, v_l1_swa_sinks, v_l1_mix_wout, v_l1_ffn2_norm, v_l1_ffn2_wi, v_l1_ffn2_wo, v_l2_ffn1_norm, v_l2_ffn1_wi, v_l2_ffn1_wo, v_l2_mix_norm, v_l2_mix_win, v_l2_fox_bf, v_l2_mix_wout, v_l2_ffn2_norm, v_l2_ffn2_wi, v_l2_ffn2_wo, v_l3_ffn1_norm, v_l3_ffn1_wi, v_l3_ffn1_wo, v_l3_mix_norm, v_l3_mix_win, v_l3_gmlp_vnorm, v_l3_gmlp_ws, v_l3_gmlp_bs, v_l3_mix_wout, v_l3_ffn2_norm, v_l3_ffn2_wi, v_l3_ffn2_wo, v_final_norm):
    return _step(dict(locals()))
```

```python
import functools
import math

import jax
import jax.numpy as jnp
from jax import lax
from jax.experimental import pallas as pl
from jax.experimental.pallas import tpu as pltpu

F32 = jnp.float32
BF16 = jnp.bfloat16

NORM_EPS = 1e-5
NEG_INF = -1e30
BLOCK = 128
GMLP_GROUPS = 16
SWA_HEAD_DIM = 64
SWA_GROUP = 8
ROPE_DIM = SWA_HEAD_DIM // 4
ROPE_THETA = 500000.0
FOX_HEAD_DIM = 128
ADAM_LR = 0.001
ADAM_B1 = 0.9
ADAM_B2 = 0.999
ADAM_EPS = 1e-08
ADAM_WD = 0.01
ADAM_STEP = 10
N_CHIPS = 4
N_DEV = 8
LANES = 128
VMEM_LIMIT = 56 * 1024 * 1024
MESH = pl.DeviceIdType.MESH
HBM = pl.BlockSpec(memory_space=pltpu.HBM)
SEM = pl.BlockSpec(memory_space=pltpu.SEMAPHORE)
EFFECT = pltpu.SideEffectType.DATAFLOW_SIDE_EFFECTING

MM_TILES = (1024, 1408, 896, 640, 512, 384, 256, 128)


def _pick(n, prefs):
    for p in prefs:
        if p <= n and n % p == 0:
            return p
    return n


def _params(*sem):
    return pltpu.CompilerParams(dimension_semantics=sem or None, vmem_limit_bytes=VMEM_LIMIT)


def _cols(arr):
    return arr.shape[-1] * (arr.shape[0] if arr.ndim == 3 else 1)


def _mat_spec(arr, rb, cb, ridx, cidx):
    if arr.ndim == 2:
        return pl.BlockSpec((rb, cb), lambda j, i, k: (ridx(j, i, k), cidx(j, i, k)))
    per = arr.shape[2] // cb
    return pl.BlockSpec((None, rb, cb),
                        lambda j, i, k: (cidx(j, i, k) // per, ridx(j, i, k), cidx(j, i, k) % per))


def _matmul(a, b, *, name, out_dtype, ta=False, tb=False, out_shards=1, scale=1.0, resid=None, dep=None):
    m_dim, k_dim = (a.shape[1], a.shape[0]) if ta else a.shape
    n_dim = b.shape[-2] if tb else _cols(b)
    assert k_dim == (_cols(b) if tb else b.shape[-2]), (a.shape, b.shape, ta, tb)
    n_unit = n_dim // out_shards
    if b.ndim == 3 and not tb:
        n_unit = math.gcd(n_unit, b.shape[2])
    k_unit = b.shape[2] if (b.ndim == 3 and tb) else k_dim
    bm = _pick(m_dim, MM_TILES)
    bn = _pick(n_unit, MM_TILES)
    bk = k_unit if k_unit <= 2048 else _pick(k_unit, MM_TILES)
    nk = k_dim // bk
    i_of, j_of, k_of = (lambda j, i, k: i), (lambda j, i, k: j), (lambda j, i, k: k)
    a_spec = _mat_spec(a, bk, bm, k_of, i_of) if ta else _mat_spec(a, bm, bk, i_of, k_of)
    b_spec = _mat_spec(b, bn, bk, j_of, k_of) if tb else _mat_spec(b, bk, bn, k_of, j_of)
    out_shape = (m_dim, n_dim) if out_shards == 1 else (out_shards, m_dim, n_dim // out_shards)
    out = jax.ShapeDtypeStruct(out_shape, out_dtype)
    o_spec = _mat_spec(out, bm, bn, i_of, j_of)
    dims = (((0 if ta else 1,), (1 if tb else 0,)), ((), ()))
    operands, in_specs = [a, b], [a_spec, b_spec]
    if resid is not None:
        operands.append(resid)
        in_specs.append(_mat_spec(resid, bm, bn, i_of, j_of))
    if dep is not None:
        operands.append(dep)
        in_specs.append(pl.BlockSpec(dep.shape, lambda j, i, k: (0, 0)))
    n_in = len(operands)

    def body(*refs):
        a_ref, b_ref = refs[0], refs[1]
        r_ref = refs[2] if resid is not None else None
        o_ref = refs[n_in]
        part = lax.dot_general(a_ref[...].astype(BF16), b_ref[...].astype(BF16), dims,
                               preferred_element_type=F32)

        def finish(acc):
            val = acc * scale if scale != 1.0 else acc
            if r_ref is not None:
                val = r_ref[...] + val
            o_ref[...] = val.astype(o_ref.dtype)

        if nk == 1:
            finish(part)
        else:
            acc_ref = refs[-1]
            k = pl.program_id(2)

            @pl.when(k == 0)
            def _():
                acc_ref[...] = part

            @pl.when(k > 0)
            def _():
                acc_ref[...] += part

            @pl.when(k == nk - 1)
            def _():
                finish(acc_ref[...])

    return pl.pallas_call(
        body, name=name, grid=(n_dim // bn, m_dim // bm, nk),
        in_specs=in_specs, out_specs=o_spec, out_shape=out,
        scratch_shapes=[pltpu.VMEM((bm, bn), F32)] if nk > 1 else [],
        compiler_params=_params("parallel", "parallel", "arbitrary"),
    )(*operands)


def _row_block(rows, width, itemsize=4, budget=2 << 20):
    for br in (512, 256, 128, 64, 32, 16, 8):
        if rows % br == 0 and br * width * itemsize <= budget:
            return br
    return rows


def _rms_fwd(h, g, name, dep=None):
    s_len, d = h.shape
    br = _row_block(s_len, d)

    def body(h_ref, g_ref, *rest):
        o_ref = rest[-1]
        x = h_ref[...]
        r = lax.rsqrt(jnp.mean(x * x, axis=-1, keepdims=True) + NORM_EPS)
        o_ref[...] = (x * r * g_ref[...]).astype(BF16)

    spec = pl.BlockSpec((br, d), lambda i: (i, 0))
    operands = [h, g.reshape(1, d)] + ([dep] if dep is not None else [])
    in_specs = [spec, pl.BlockSpec((1, d), lambda i: (0, 0))]
    if dep is not None:
        in_specs.append(pl.BlockSpec(dep.shape, lambda i: (0, 0)))
    return pl.pallas_call(body, name=name, grid=(s_len // br,), in_specs=in_specs, out_specs=spec,
                          out_shape=jax.ShapeDtypeStruct((s_len, d), BF16),
                          compiler_params=_params("parallel"))(*operands)


def _rms_bwd_rows(x, g, dn):
    r = lax.rsqrt(jnp.mean(x * x, axis=-1, keepdims=True) + NORM_EPS)
    xhat = x * r
    gdn = dn * g
    dx = r * (gdn - xhat * jnp.mean(gdn * xhat, axis=-1, keepdims=True))
    return dx, dn * xhat


def _norm_bwd(h, g, dn, dres, name):
    s_len, d = h.shape
    br = _row_block(s_len, d, budget=1 << 20)

    def body(h_ref, g_ref, dn_ref, dres_ref, dh_ref, dg_ref):
        dx, dg_rows = _rms_bwd_rows(h_ref[...], g_ref[...], dn_ref[...].astype(F32))
        dh_ref[...] = dres_ref[...] + dx

        @pl.when(pl.program_id(0) == 0)
        def _():
            dg_ref[...] = jnp.zeros_like(dg_ref)

        dg_ref[...] += jnp.sum(dg_rows, axis=0, keepdims=True)

    spec = pl.BlockSpec((br, d), lambda i: (i, 0))
    vec = pl.BlockSpec((1, d), lambda i: (0, 0))
    return pl.pallas_call(body, name=name, grid=(s_len // br,),
                          in_specs=[spec, vec, spec, spec], out_specs=[spec, vec],
                          out_shape=[jax.ShapeDtypeStruct((s_len, d), F32),
                                     jax.ShapeDtypeStruct((1, d), F32)],
                          compiler_params=_params("arbitrary"))(h, g.reshape(1, d), dn, dres)


def _sigmoid(x):
    return 1.0 / (1.0 + jnp.exp(-x))


def _swiglu_fwd(z, name):
    s_len, f2 = z.shape
    f = f2 // 2
    br = _row_block(s_len, f2, budget=3 << 20)

    def body(z_ref, a_ref):
        gate = z_ref[:, :f].astype(F32)
        up = z_ref[:, f:].astype(F32)
        a_ref[...] = (gate * _sigmoid(gate) * up).astype(BF16)

    return pl.pallas_call(body, name=name, grid=(s_len // br,),
                          in_specs=[pl.BlockSpec((br, f2), lambda i: (i, 0))],
                          out_specs=pl.BlockSpec((br, f), lambda i: (i, 0)),
                          out_shape=jax.ShapeDtypeStruct((s_len, f), BF16),
                          compiler_params=_params("parallel"))(z)


def _swiglu_bwd(z, da, name):
    s_len, f2 = z.shape
    f = f2 // 2
    br = _row_block(s_len, f2, budget=3 << 20)

    def body(z_ref, da_ref, dz_ref):
        gate = z_ref[:, :f].astype(F32)
        up = z_ref[:, f:].astype(F32)
        d = da_ref[...].astype(F32)
        sig = _sigmoid(gate)
        dz_ref[:, :f] = (d * up * (sig * (1.0 + gate * (1.0 - sig)))).astype(BF16)
        dz_ref[:, f:] = (d * gate * sig).astype(BF16)

    return pl.pallas_call(body, name=name, grid=(s_len // br,),
                          in_specs=[pl.BlockSpec((br, f2), lambda i: (i, 0)),
                                    pl.BlockSpec((br, f), lambda i: (i, 0))],
                          out_specs=pl.BlockSpec((br, f2), lambda i: (i, 0)),
                          out_shape=jax.ShapeDtypeStruct((s_len, f2), BF16),
                          compiler_params=_params("parallel"))(z, da)


def _loss_head(h, g, target, name):
    s_len, d = h.shape
    br = _row_block(s_len, d, budget=1 << 20)

    def body(h_ref, g_ref, t_ref, loss_ref, dh_ref, dg_ref):
        x = h_ref[...]
        gain = g_ref[...]
        r = lax.rsqrt(jnp.mean(x * x, axis=-1, keepdims=True) + NORM_EPS)
        err = x * r * gain - t_ref[...]
        part = 0.5 * jnp.sum(jnp.mean(err * err, axis=-1, keepdims=True), axis=0, keepdims=True)
        dx, dg_rows = _rms_bwd_rows(x, gain, err * (1.0 / d))
        dh_ref[...] = dx

        @pl.when(pl.program_id(0) == 0)
        def _():
            dg_ref[...] = jnp.zeros_like(dg_ref)
            loss_ref[...] = jnp.zeros_like(loss_ref)

        dg_ref[...] += jnp.sum(dg_rows, axis=0, keepdims=True)
        loss_ref[...] += jnp.broadcast_to(part, loss_ref.shape)

    spec = pl.BlockSpec((br, d), lambda i: (i, 0))
    vec = pl.BlockSpec((1, d), lambda i: (0, 0))
    one = pl.BlockSpec((1, LANES), lambda i: (0, 0))
    loss, dh, dg = pl.pallas_call(
        body, name=name, grid=(s_len // br,), in_specs=[spec, vec, spec],
        out_specs=[one, spec, vec],
        out_shape=[jax.ShapeDtypeStruct((1, LANES), F32), jax.ShapeDtypeStruct((s_len, d), F32),
                   jax.ShapeDtypeStruct((1, d), F32)],
        compiler_params=_params("arbitrary"))(h, g.reshape(1, d), target)
    return loss[0, 0], dh, dg


def _adamw(w, g, m, v, name):
    rows, width = w.shape
    br = _row_block(rows, width, budget=1 << 20)
    c1 = 1.0 - ADAM_B1 ** ADAM_STEP
    c2 = 1.0 - ADAM_B2 ** ADAM_STEP

    def body(w_ref, g_ref, m_ref, v_ref, d_ref, nm_ref, nv_ref):
        grad = g_ref[...]
        new_m = ADAM_B1 * m_ref[...] + (1.0 - ADAM_B1) * grad
        new_v = ADAM_B2 * v_ref[...] + (1.0 - ADAM_B2) * (grad * grad)
        d_ref[...] = -ADAM_LR * ((new_m / c1) / (jnp.sqrt(new_v / c2) + ADAM_EPS) + ADAM_WD * w_ref[...])
        nm_ref[...] = new_m
        nv_ref[...] = new_v

    spec = pl.BlockSpec((br, width), lambda i: (i, 0))
    shp = jax.ShapeDtypeStruct(w.shape, F32)
    return pl.pallas_call(body, name=name, grid=(rows // br,), in_specs=[spec] * 4,
                          out_specs=[spec] * 3, out_shape=[shp] * 3,
                          compiler_params=_params("parallel"))(w, g, m, v)


def _gelu(x):
    return 0.5 * x * (1.0 + lax.erf(x * (2.0 ** -0.5)))


def _gelu_grad(x):
    return 0.5 * (1.0 + lax.erf(x * (2.0 ** -0.5))) + x * jnp.exp(-0.5 * x * x) * ((2.0 * math.pi) ** -0.5)


def _tril_mask():
    row = lax.broadcasted_iota(jnp.int32, (BLOCK, BLOCK), 0)
    col = lax.broadcasted_iota(jnp.int32, (BLOCK, BLOCK), 1)
    return col <= row


def _gmlp_specs(s_len, d):
    gw = d // GMLP_GROUPS
    zp = pl.BlockSpec((BLOCK, 2 * d), lambda i: (i, 0))
    row = pl.BlockSpec((BLOCK, d), lambda i: (i, 0))
    vec = pl.BlockSpec((1, d), lambda i: (0, 0))
    ws = pl.BlockSpec((GMLP_GROUPS, BLOCK, BLOCK), lambda i: (0, 0, 0))
    bst = pl.BlockSpec((BLOCK, GMLP_GROUPS), lambda i: (0, 0))
    return gw, zp, row, vec, ws, bst


def _gmlp_fwd(zp, vgain, ws, bs, name):
    s_len, d2 = zp.shape
    d = d2 // 2
    gw, zp_spec, row_spec, vec_spec, ws_spec, bst_spec = _gmlp_specs(s_len, d)

    def body(zp_ref, vg_ref, ws_ref, bst_ref, y_ref):
        u = _gelu(zp_ref[:, :d].astype(F32))
        vv = _gelu(zp_ref[:, d:].astype(F32))
        r = lax.rsqrt(jnp.mean(vv * vv, axis=-1, keepdims=True) + NORM_EPS)
        vn = (vv * r * vg_ref[...]).astype(BF16)
        mask = _tril_mask()
        for g in range(GMLP_GROUPS):
            cols = slice(g * gw, (g + 1) * gw)
            wg = jnp.where(mask, ws_ref[g], 0.0).astype(BF16)
            mixed = jnp.dot(wg, vn[:, cols], preferred_element_type=F32) + bst_ref[:, g:g + 1]
            y_ref[:, cols] = (u[:, cols] * mixed).astype(BF16)

    return pl.pallas_call(body, name=name, grid=(s_len // BLOCK,),
                          in_specs=[zp_spec, vec_spec, ws_spec, bst_spec], out_specs=row_spec,
                          out_shape=jax.ShapeDtypeStruct((s_len, d), BF16),
                          compiler_params=_params("parallel"))(zp, vgain.reshape(1, d), ws, bs.T)


def _gmlp_bwd(zp, dy, vgain, ws, bs, name):
    s_len, d2 = zp.shape
    d = d2 // 2
    gw, zp_spec, row_spec, vec_spec, ws_spec, bst_spec = _gmlp_specs(s_len, d)

    def body(zp_ref, dy_ref, vg_ref, ws_ref, bst_ref, dzp_ref, dws_ref, dbst_ref, dvg_ref, dvn_ref):
        @pl.when(pl.program_id(0) == 0)
        def _():
            dws_ref[...] = jnp.zeros_like(dws_ref)
            dbst_ref[...] = jnp.zeros_like(dbst_ref)
            dvg_ref[...] = jnp.zeros_like(dvg_ref)

        zu = zp_ref[:, :d].astype(F32)
        zv = zp_ref[:, d:].astype(F32)
        u = _gelu(zu)
        vv = _gelu(zv)
        r = lax.rsqrt(jnp.mean(vv * vv, axis=-1, keepdims=True) + NORM_EPS)
        vhat = vv * r
        gain = vg_ref[...]
        vn = (vhat * gain).astype(BF16)
        dyf = dy_ref[...].astype(F32)
        dmixed = dyf * u
        dmixed_b = dmixed.astype(BF16)
        mask = _tril_mask()
        lane = lax.broadcasted_iota(jnp.int32, (BLOCK, GMLP_GROUPS), 1)
        dbs_step = jnp.zeros((BLOCK, GMLP_GROUPS), F32)
        for g in range(GMLP_GROUPS):
            cols = slice(g * gw, (g + 1) * gw)
            wg = jnp.where(mask, ws_ref[g], 0.0).astype(BF16)
            mixed = jnp.dot(wg, vn[:, cols], preferred_element_type=F32) + bst_ref[:, g:g + 1]
            dzp_ref[:, cols] = (dyf[:, cols] * mixed * _gelu_grad(zu[:, cols])).astype(BF16)
            dm = dmixed_b[:, cols]
            dw = lax.dot_general(dm, vn[:, cols], (((1,), (1,)), ((), ())), preferred_element_type=F32)
            dws_ref[g] += jnp.where(mask, dw, 0.0)
            dbs_step = dbs_step + jnp.where(lane == g, jnp.sum(dmixed[:, cols], axis=-1, keepdims=True), 0.0)
            dvn_ref[:, cols] = lax.dot_general(wg, dm, (((0,), (0,)), ((), ())), preferred_element_type=F32)
        dbst_ref[...] += dbs_step
        dvn = dvn_ref[...]
        dvg_ref[...] += jnp.sum(dvn * vhat, axis=0, keepdims=True)
        dvhat = dvn * gain
        dvv = r * (dvhat - vhat * jnp.mean(dvhat * vhat, axis=-1, keepdims=True))
        dzp_ref[:, d:] = (dvv * _gelu_grad(zv)).astype(BF16)

    return pl.pallas_call(
        body, name=name, grid=(s_len // BLOCK,),
        in_specs=[zp_spec, row_spec, vec_spec, ws_spec, bst_spec],
        out_specs=[zp_spec, ws_spec, bst_spec, vec_spec],
        out_shape=[jax.ShapeDtypeStruct((s_len, d2), BF16), jax.ShapeDtypeStruct(ws.shape, F32),
                   jax.ShapeDtypeStruct((BLOCK, GMLP_GROUPS), F32), jax.ShapeDtypeStruct((1, d), F32)],
        scratch_shapes=[pltpu.VMEM((BLOCK, d), F32)],
        compiler_params=_params("arbitrary"))(zp, dy, vgain.reshape(1, d), ws, bs.T)


def _rope_tables(s_len, sign):
    half = ROPE_DIM // 2
    inv_freq = ROPE_THETA ** (-(jnp.arange(half, dtype=F32) * 2.0 / ROPE_DIM))
    ang = jnp.arange(s_len, dtype=F32)[:, None] * inv_freq[None, :]
    cos, sin = jnp.cos(ang), jnp.sin(ang) * sign
    pad = jnp.zeros((s_len, SWA_HEAD_DIM - ROPE_DIM), F32)
    zero = jnp.zeros_like(sin)
    cos_t = jnp.concatenate([cos, cos, pad + 1.0], axis=1)
    sin_up = jnp.concatenate([-sin, zero, pad], axis=1)
    sin_dn = jnp.concatenate([zero, sin, pad], axis=1)
    return [jnp.tile(t, (1, LANES // SWA_HEAD_DIM)) for t in (cos_t, sin_up, sin_dn)]


def _rotate(x, cos_t, sin_up, sin_dn):
    width = x.shape[-1]
    half = ROPE_DIM // 2
    reps = width // cos_t.shape[-1]
    if reps > 1:
        cos_t, sin_up, sin_dn = (jnp.tile(t, (1, reps)) for t in (cos_t, sin_up, sin_dn))
    elif reps == 0:
        cos_t, sin_up, sin_dn = (t[:, :width] for t in (cos_t, sin_up, sin_dn))
    return x * cos_t + pltpu.roll(x, width - half, 1) * sin_up + pltpu.roll(x, half, 1) * sin_dn


def _rope_fwd(qkv, name):
    s_len, total = qkv.shape
    wkv = total // (SWA_GROUP + 2)
    wq = SWA_GROUP * wkv
    br = _row_block(s_len, total, budget=2 << 20)
    tables = _rope_tables(s_len, 1.0)

    def body(q_ref, k_ref, v_ref, c_ref, su_ref, sd_ref, qo_ref, ko_ref, vo_ref):
        t = (c_ref[...], su_ref[...], sd_ref[...])
        qo_ref[...] = _rotate(q_ref[...], *t).astype(BF16)
        ko_ref[...] = _rotate(k_ref[...], *t).astype(BF16)
        vo_ref[...] = v_ref[...].astype(BF16)

    qs = pl.BlockSpec((br, wq), lambda i: (i, 0))
    ks = pl.BlockSpec((br, wkv), lambda i: (i, SWA_GROUP))
    vs = pl.BlockSpec((br, wkv), lambda i: (i, SWA_GROUP + 1))
    ts = pl.BlockSpec((br, LANES), lambda i: (i, 0))
    kv_out = pl.BlockSpec((br, wkv), lambda i: (i, 0))
    return pl.pallas_call(
        body, name=name, grid=(s_len // br,), in_specs=[qs, ks, vs, ts, ts, ts],
        out_specs=[qs, kv_out, kv_out],
        out_shape=[jax.ShapeDtypeStruct((s_len, wq), BF16), jax.ShapeDtypeStruct((s_len, wkv), BF16),
                   jax.ShapeDtypeStruct((s_len, wkv), BF16)],
        compiler_params=_params("parallel"))(qkv, qkv, qkv, *tables)


def _rope_bwd(dq, dk, dv, name):
    s_len, wq = dq.shape
    wkv = dk.shape[1]
    br = _row_block(s_len, wq + 2 * wkv, budget=2 << 20)
    tables = _rope_tables(s_len, -1.0)

    def body(q_ref, k_ref, v_ref, c_ref, su_ref, sd_ref, o_ref):
        t = (c_ref[...], su_ref[...], sd_ref[...])
        o_ref[:, :wq] = _rotate(q_ref[...], *t).astype(BF16)
        o_ref[:, wq:wq + wkv] = _rotate(k_ref[...], *t).astype(BF16)
        o_ref[:, wq + wkv:] = v_ref[...].astype(BF16)

    qs = pl.BlockSpec((br, wq), lambda i: (i, 0))
    kvs = pl.BlockSpec((br, wkv), lambda i: (i, 0))
    ts = pl.BlockSpec((br, LANES), lambda i: (i, 0))
    return pl.pallas_call(
        body, name=name, grid=(s_len // br,), in_specs=[qs, kvs, kvs, ts, ts, ts],
        out_specs=pl.BlockSpec((br, wq + 2 * wkv), lambda i: (i, 0)),
        out_shape=jax.ShapeDtypeStruct((s_len, wq + 2 * wkv), BF16),
        compiler_params=_params("parallel"))(dq, dk, dv, *tables)


def _swa_valid(i):
    row = lax.broadcasted_iota(jnp.int32, (BLOCK, 2 * BLOCK), 0)
    col = lax.broadcasted_iota(jnp.int32, (BLOCK, 2 * BLOCK), 1)
    return (col - BLOCK <= row) & (row < col) & ((col >= BLOCK) | (i > 0))


def _swa_specs(wq, wkv):
    q_spec = pl.BlockSpec((BLOCK, wq), lambda i: (i, 0))
    cur = pl.BlockSpec((BLOCK, wkv), lambda i: (i, 0))
    prev = pl.BlockSpec((BLOCK, wkv), lambda i: (jnp.maximum(i - 1, 0), 0))
    sink = pl.BlockSpec(memory_space=pltpu.SMEM)
    return q_spec, cur, prev, sink


def _swa_probs(q_h, k_cat, valid, sink):
    s = lax.dot_general(q_h, k_cat, (((1,), (1,)), ((), ())), preferred_element_type=F32)
    s = jnp.where(valid, s * (SWA_HEAD_DIM ** -0.5), NEG_INF)
    m = jnp.maximum(jnp.max(s, axis=-1, keepdims=True), sink)
    p = jnp.exp(s - m)
    e_sink = jnp.exp(sink - m)
    denom = jnp.sum(p, axis=-1, keepdims=True) + e_sink
    return p / denom, e_sink / denom


def _swa_fwd(q, k, v, sinks, name):
    s_len, wq = q.shape
    wkv = k.shape[1]
    hd = SWA_HEAD_DIM
    q_spec, cur, prev, sink_spec = _swa_specs(wq, wkv)

    def body(q_ref, kc_ref, kp_ref, vc_ref, vp_ref, sink_ref, o_ref):
        valid = _swa_valid(pl.program_id(0))
        for j in range(wkv // hd):
            lanes = slice(j * hd, (j + 1) * hd)
            k_cat = jnp.concatenate([kp_ref[:, lanes], kc_ref[:, lanes]], axis=0)
            v_cat = jnp.concatenate([vp_ref[:, lanes], vc_ref[:, lanes]], axis=0)
            for hh in range(SWA_GROUP):
                h = j * SWA_GROUP + hh
                pn, _ = _swa_probs(q_ref[:, h * hd:(h + 1) * hd], k_cat, valid, sink_ref[h])
                o_ref[:, h * hd:(h + 1) * hd] = jnp.dot(
                    pn.astype(BF16), v_cat, preferred_element_type=F32).astype(BF16)

    return pl.pallas_call(body, name=name, grid=(s_len // BLOCK,),
                          in_specs=[q_spec, cur, prev, cur, prev, sink_spec], out_specs=q_spec,
                          out_shape=jax.ShapeDtypeStruct((s_len, wq), BF16),
                          compiler_params=_params("parallel"))(q, k, k, v, v, sinks)


def _swa_bwd(q, k, v, sinks, do, name):
    s_len, wq = q.shape
    wkv = k.shape[1]
    hd = SWA_HEAD_DIM
    q_spec, cur, prev, sink_spec = _swa_specs(wq, wkv)
    full = pl.BlockSpec((s_len, wkv), lambda i: (0, 0))
    one = pl.BlockSpec((1, LANES), lambda i: (0, 0))
    scale = hd ** -0.5

    def body(q_ref, kc_ref, kp_ref, vc_ref, vp_ref, sink_ref, do_ref, dq_ref, dk_ref, dv_ref, ds_ref):
        i = pl.program_id(0)

        @pl.when(i == 0)
        def _():
            dk_ref[...] = jnp.zeros_like(dk_ref)
            dv_ref[...] = jnp.zeros_like(dv_ref)
            ds_ref[...] = jnp.zeros_like(ds_ref)

        valid = _swa_valid(i)
        lane = lax.broadcasted_iota(jnp.int32, (1, LANES), 1)
        dsink_step = jnp.zeros((1, LANES), F32)
        rows_prev = pl.ds(pl.multiple_of(jnp.maximum(i - 1, 0) * BLOCK, BLOCK), BLOCK)
        rows_cur = pl.ds(pl.multiple_of(i * BLOCK, BLOCK), BLOCK)
        for j in range(wkv // hd):
            lanes = slice(j * hd, (j + 1) * hd)
            k_cat = jnp.concatenate([kp_ref[:, lanes], kc_ref[:, lanes]], axis=0)
            v_cat = jnp.concatenate([vp_ref[:, lanes], vc_ref[:, lanes]], axis=0)
            dk_cat = jnp.zeros((2 * BLOCK, hd), F32)
            dv_cat = jnp.zeros((2 * BLOCK, hd), F32)
            for hh in range(SWA_GROUP):
                h = j * SWA_GROUP + hh
                q_h = q_ref[:, h * hd:(h + 1) * hd]
                do_h = do_ref[:, h * hd:(h + 1) * hd]
                pn, p_sink = _swa_probs(q_h, k_cat, valid, sink_ref[h])
                dpn = lax.dot_general(do_h, v_cat, (((1,), (1,)), ((), ())), preferred_element_type=F32)
                delta = jnp.sum(dpn * pn, axis=-1, keepdims=True)
                ds = (pn * (dpn - delta) * scale).astype(BF16)
                dsink_h = -jnp.sum(p_sink * delta, axis=0, keepdims=True)
                dsink_step = dsink_step + jnp.where(lane == h, dsink_h, 0.0)
                dq_ref[:, h * hd:(h + 1) * hd] = jnp.dot(ds, k_cat, preferred_element_type=F32)
                dk_cat = dk_cat + lax.dot_general(ds, q_h, (((0,), (0,)), ((), ())),
                                                  preferred_element_type=F32)
                dv_cat = dv_cat + lax.dot_general(pn.astype(BF16), do_h, (((0,), (0,)), ((), ())),
                                                  preferred_element_type=F32)
            dk_ref[rows_prev, lanes] += dk_cat[:BLOCK]
            dk_ref[rows_cur, lanes] += dk_cat[BLOCK:]
            dv_ref[rows_prev, lanes] += dv_cat[:BLOCK]
            dv_ref[rows_cur, lanes] += dv_cat[BLOCK:]
        ds_ref[...] += dsink_step

    return pl.pallas_call(
        body, name=name, grid=(s_len // BLOCK,),
        in_specs=[q_spec, cur, prev, cur, prev, sink_spec, q_spec],
        out_specs=[q_spec, full, full, one],
        out_shape=[jax.ShapeDtypeStruct((s_len, wq), F32), jax.ShapeDtypeStruct((s_len, wkv), F32),
                   jax.ShapeDtypeStruct((s_len, wkv), F32), jax.ShapeDtypeStruct((1, LANES), F32)],
        compiler_params=_params("arbitrary"))(q, k, k, v, v, sinks, do)


def _log_sigmoid(x):
    return jnp.minimum(x, 0.0) - jnp.log(1.0 + jnp.exp(-jnp.abs(x)))


def _tri_ones(lower):
    row = lax.broadcasted_iota(jnp.int32, (BLOCK, BLOCK), 0)
    col = lax.broadcasted_iota(jnp.int32, (BLOCK, BLOCK), 1)
    return jnp.where((col <= row) if lower else (col >= row), 1.0, 0.0).astype(F32)


def _fox_decay(proj, bf_row, fl_block, name):
    s_len = proj.shape[0]
    nchunk = s_len // BLOCK

    def body(fl_ref, bf_ref, dec_ref):
        tri = _tri_ones(True)
        carry = jnp.zeros((1, LANES), F32)
        for c in range(nchunk):
            rows = slice(c * BLOCK, (c + 1) * BLOCK)
            log_f = _log_sigmoid(fl_ref[rows, :] + bf_ref[...])
            loc = jnp.dot(tri, log_f, preferred_element_type=F32, precision=lax.Precision.HIGHEST) + carry
            dec_ref[rows, :] = loc
            carry = loc[BLOCK - 1:BLOCK, :]

    return pl.pallas_call(
        body, name=name, grid=(1,),
        in_specs=[pl.BlockSpec((s_len, LANES), lambda i: (0, fl_block)),
                  pl.BlockSpec((1, LANES), lambda i: (0, 0))],
        out_specs=pl.BlockSpec((s_len, LANES), lambda i: (0, 0)),
        out_shape=jax.ShapeDtypeStruct((s_len, LANES), F32),
        compiler_params=_params("arbitrary"))(proj, bf_row)


def _fox_decay_bwd(ddq, ddk, proj, bf_row, fl_block, heads, name):
    s_len = proj.shape[0]
    nchunk = s_len // BLOCK

    def body(ddq_ref, ddk_ref, fl_ref, bf_ref, dfl_ref, dbf_ref):
        tri = _tri_ones(False)
        lane_ok = lax.broadcasted_iota(jnp.int32, (BLOCK, LANES), 1) < heads
        carry = jnp.zeros((1, LANES), F32)
        dbf = jnp.zeros((1, LANES), F32)
        for c in reversed(range(nchunk)):
            rows = slice(c * BLOCK, (c + 1) * BLOCK)
            ddec = ddq_ref[rows, :] + ddk_ref[rows, :]
            dlog = jnp.dot(tri, ddec, preferred_element_type=F32, precision=lax.Precision.HIGHEST) + carry
            carry = dlog[0:1, :]
            dfl = jnp.where(lane_ok, dlog * _sigmoid(-(fl_ref[rows, :] + bf_ref[...])), 0.0)
            dfl_ref[rows, :] = dfl.astype(BF16)
            dbf = dbf + jnp.sum(dfl, axis=0, keepdims=True)
        dbf_ref[...] = dbf

    blk = pl.BlockSpec((s_len, LANES), lambda i: (0, 0))
    one = pl.BlockSpec((1, LANES), lambda i: (0, 0))
    return pl.pallas_call(
        body, name=name, grid=(1,),
        in_specs=[blk, blk, pl.BlockSpec((s_len, LANES), lambda i: (0, fl_block)), one],
        out_specs=[blk, one],
        out_shape=[jax.ShapeDtypeStruct((s_len, LANES), BF16), jax.ShapeDtypeStruct((1, LANES), F32)],
        compiler_params=_params("arbitrary"))(ddq, ddk, proj, bf_row)


def _fox_scores(q, k, decq, deck, i, bq):
    s_len = k.shape[0]
    s = lax.dot_general(q, k, (((1,), (1,)), ((), ())), preferred_element_type=F32)
    s = s * (FOX_HEAD_DIM ** -0.5) + decq - deck
    row = lax.broadcasted_iota(jnp.int32, (bq, s_len), 0) + i * bq
    col = lax.broadcasted_iota(jnp.int32, (bq, s_len), 1)
    s = jnp.where(col <= row, s, NEG_INF)
    p = jnp.exp(s - jnp.max(s, axis=-1, keepdims=True))
    return p / jnp.sum(p, axis=-1, keepdims=True)


def _fox_specs(s_len, heads, bq):
    hd = FOX_HEAD_DIM
    q_spec = pl.BlockSpec((bq, hd), lambda h, i: (i, h))
    k_spec = pl.BlockSpec((s_len, hd), lambda h, i: (0, heads + h))
    v_spec = pl.BlockSpec((s_len, hd), lambda h, i: (0, 2 * heads + h))
    dq_spec = pl.BlockSpec((None, bq, 1), lambda h, i: (h, i, 0))
    dk_spec = pl.BlockSpec((None, 1, s_len), lambda h, i: (h, 0, 0))
    return q_spec, k_spec, v_spec, dq_spec, dk_spec


def _fox_fwd(proj, decq, deck, heads, name):
    s_len = proj.shape[0]
    bq = _pick(s_len, (256, 128))
    q_spec, k_spec, v_spec, dq_spec, dk_spec = _fox_specs(s_len, heads, bq)

    def body(q_ref, k_ref, v_ref, decq_ref, deck_ref, o_ref):
        pn = _fox_scores(q_ref[...].astype(BF16), k_ref[...].astype(BF16), decq_ref[...], deck_ref[...],
                         pl.program_id(1), bq)
        o_ref[...] = jnp.dot(pn.astype(BF16), v_ref[...].astype(BF16),
                             preferred_element_type=F32).astype(BF16)

    return pl.pallas_call(body, name=name, grid=(heads, s_len // bq),
                          in_specs=[q_spec, k_spec, v_spec, dq_spec, dk_spec], out_specs=q_spec,
                          out_shape=jax.ShapeDtypeStruct((s_len, heads * FOX_HEAD_DIM), BF16),
                          compiler_params=_params("parallel", "parallel"))(proj, proj, proj, decq, deck)


def _fox_bwd(proj, decq, deck, do, heads, name):
    s_len = proj.shape[0]
    d = heads * FOX_HEAD_DIM
    bq = _pick(s_len, (256, 128))
    q_spec, k_spec, v_spec, dq_spec, dk_spec = _fox_specs(s_len, heads, bq)
    acc_spec = pl.BlockSpec((s_len, FOX_HEAD_DIM), lambda h, i: (0, h))
    scale = FOX_HEAD_DIM ** -0.5

    def body(q_ref, k_ref, v_ref, decq_ref, deck_ref, do_ref, dq_ref, dk_ref, dv_ref, ddq_ref, ddk_ref):
        i = pl.program_id(1)

        @pl.when(i == 0)
        def _():
            dk_ref[...] = jnp.zeros_like(dk_ref)
            dv_ref[...] = jnp.zeros_like(dv_ref)
            ddk_ref[...] = jnp.zeros_like(ddk_ref)

        q = q_ref[...].astype(BF16)
        k = k_ref[...].astype(BF16)
        do_b = do_ref[...]
        pn = _fox_scores(q, k, decq_ref[...], deck_ref[...], i, bq)
        dpn = lax.dot_general(do_b, v_ref[...].astype(BF16), (((1,), (1,)), ((), ())),
                              preferred_element_type=F32)
        ds = pn * (dpn - jnp.sum(dpn * pn, axis=-1, keepdims=True))
        ddq_ref[...] = jnp.sum(ds, axis=-1, keepdims=True)
        ddk_ref[...] -= jnp.sum(ds, axis=0, keepdims=True)
        ds_b = (ds * scale).astype(BF16)
        dq_ref[...] = jnp.dot(ds_b, k, preferred_element_type=F32).astype(BF16)
        dk_ref[...] += lax.dot_general(ds_b, q, (((0,), (0,)), ((), ())), preferred_element_type=F32)
        dv_ref[...] += lax.dot_general(pn.astype(BF16), do_b, (((0,), (0,)), ((), ())),
                                       preferred_element_type=F32)

    return pl.pallas_call(
        body, name=name, grid=(heads, s_len // bq),
        in_specs=[q_spec, k_spec, v_spec, dq_spec, dk_spec, q_spec],
        out_specs=[q_spec, acc_spec, acc_spec, dq_spec, dk_spec],
        out_shape=[jax.ShapeDtypeStruct((s_len, d), BF16), jax.ShapeDtypeStruct((s_len, d), F32),
                   jax.ShapeDtypeStruct((s_len, d), F32), jax.ShapeDtypeStruct((heads, s_len, 1), F32),
                   jax.ShapeDtypeStruct((heads, 1, s_len), F32)],
        compiler_params=_params("parallel", "arbitrary"))(proj, proj, proj, decq, deck, do)


def _place():
    x, y, c = lax.axis_index("x"), lax.axis_index("y"), lax.axis_index("c")
    chips = [(1 - x, y), (x, 1 - y), (1 - x, 1 - y)]
    return x, y, c, chips


def _remote(src, dst, send_sems, recv_sems, idx, to):
    return pltpu.make_async_remote_copy(src_ref=src, dst_ref=dst, send_sem=send_sems.at[idx],
                                        recv_sem=recv_sems.at[idx], device_id=to, device_id_type=MESH)


def _row_chunks(rows, want):
    for k in (want, want // 2, want // 4):
        if k >= 1 and rows % (16 * k) == 0:
            return [(j * (rows // k), rows // k) for j in range(k)]
    return [(0, rows)]


def _chunked(src_of, dst_of, rows, want, send_sems, recv_sems, idx, to):
    for start, size in _row_chunks(rows, want):
        _remote(src_of(start, size), dst_of(start, size), send_sems, recv_sems, idx, to).start()
    return _remote(src_of(0, rows), dst_of(0, rows), send_sems, recv_sems, idx, to)


D2D_CHUNKS = 8


def _cast_into_slot(w, me, name, dep=None):
    rows, width = w.shape
    br = _row_block(rows, width, budget=4 << 20)

    def body(me_ref, w_ref, *rest):
        rest[-1][...] = w_ref[...].astype(BF16)

    in_specs = [pl.BlockSpec((br, width), lambda i, me_ref: (i, 0))]
    if dep is not None:
        in_specs.append(pl.BlockSpec(dep.shape, lambda i, me_ref: (0, 0)))
    return pl.pallas_call(
        body, name=name,
        grid_spec=pltpu.PrefetchScalarGridSpec(
            num_scalar_prefetch=1, grid=(rows // br,), in_specs=in_specs,
            out_specs=pl.BlockSpec((None, br, width), lambda i, me_ref: (me_ref[0], i, 0))),
        out_shape=jax.ShapeDtypeStruct((N_CHIPS, rows, width), BF16),
        compiler_params=_params("parallel"))(me, w, *([dep] if dep is not None else []))


def _hbm(arr):
    return pltpu.with_memory_space_constraint(arr, pltpu.HBM)


def _token_shape():
    return jax.ShapeDtypeStruct((8, LANES), F32)


def _gather_start(bufs, after, name):
    n = len(bufs)

    def body(*refs):
        send, recv = refs[n + 1], refs[n + 2]
        outs = refs[n + 3:2 * n + 3]
        token = refs[2 * n + 3]
        x, y, c, chips = _place()
        me = 2 * x + y
        for t in range(n):
            half = outs[t].shape[1] // 2
            mine = outs[t].at[me, pl.ds(c * half, half)]
            for r, chip in enumerate(chips):
                _remote(mine, mine, send, recv, 3 * t + r, (*chip, c)).start()
        token[...] = jnp.zeros_like(token)

    res = pl.pallas_call(
        body, name=name, in_specs=[HBM] * n + [pl.BlockSpec(memory_space=pl.ANY)],
        out_specs=[SEM, SEM] + [HBM] * n + [pl.BlockSpec(memory_space=pltpu.VMEM)],
        out_shape=[pltpu.SemaphoreType.DMA((3 * n,)), pltpu.SemaphoreType.DMA((3 * n,))]
        + [pltpu.HBM(b.shape, b.dtype) for b in bufs] + [_token_shape()],
        input_output_aliases={t: 2 + t for t in range(n)},
        compiler_params=pltpu.CompilerParams(has_side_effects=EFFECT),
    )(*[_hbm(b) for b in bufs], after)
    return res[0], res[1], res[2:2 + n], res[2 + n]


def _gather_wait(send, recv, bufs, after, name):
    n = len(bufs)

    def body(*refs):
        send_ref, recv_ref = refs[n], refs[n + 1]
        outs = refs[n + 3:2 * n + 3]
        x, y, c, chips = _place()
        me = 2 * x + y
        for t in range(n):
            half = outs[t].shape[1] // 2
            mine = outs[t].at[me, pl.ds(c * half, half)]
            for r, chip in enumerate(chips):
                block = outs[t].at[2 * chip[0] + chip[1], pl.ds(c * half, half)]
                cp = _remote(mine, block, send_ref, recv_ref, 3 * t + r, (*chip, c))
                cp.wait_send()
                cp.wait_recv()

    return pl.pallas_call(
        body, name=name, in_specs=[HBM] * n + [SEM, SEM, pl.BlockSpec(memory_space=pl.ANY)],
        out_specs=[HBM] * n, out_shape=[pltpu.HBM(b.shape, b.dtype) for b in bufs],
        input_output_aliases={t: t for t in range(n)},
        compiler_params=pltpu.CompilerParams(has_side_effects=EFFECT),
    )(*bufs, send, recv, after)


def _gather_forward(bufs, name):
    n = len(bufs)

    def body(*refs):
        outs = refs[n:2 * n]
        send, recv = refs[2 * n:]
        x, y, c, chips = _place()
        sibling = (x, y, 1 - c)
        started = []
        for t in range(n):
            half = outs[t].shape[1] // 2
            for r, chip in enumerate(chips):
                slot = 2 * chip[0] + chip[1]
                part = lambda s, z, t=t, slot=slot, half=half: outs[t].at[slot, pl.ds(c * half + s, z)]
                started.append(_chunked(part, part, half, D2D_CHUNKS, send, recv, 3 * t + r, sibling))
        for t in range(n):
            half = outs[t].shape[1] // 2
            for r, chip in enumerate(chips):
                block = outs[t].at[2 * chip[0] + chip[1], pl.ds((1 - c) * half, half)]
                _remote(block, block, send, recv, 3 * t + r, sibling).wait_recv()
        for cp in started:
            cp.wait_send()

    return pl.pallas_call(
        body, name=name, in_specs=[HBM] * n, out_specs=[HBM] * n,
        out_shape=[jax.ShapeDtypeStruct(b.shape, b.dtype) for b in bufs],
        input_output_aliases={t: t for t in range(n)},
        scratch_shapes=[pltpu.SemaphoreType.DMA((3 * n,))] * 2,
    )(*bufs)


def _pair_swap(grads, name):
    n = len(grads)

    def body(*refs):
        ins, got = refs[:n], refs[n:2 * n]
        send, recv = refs[2 * n:]
        x, y, c, _ = _place()
        waits = []
        for t in range(n):
            half = ins[t].shape[1] // 2
            for j in range(N_CHIPS):
                src = lambda s, z, t=t, j=j, half=half: ins[t].at[j, pl.ds((1 - c) * half + s, z)]
                dst = lambda s, z, t=t, j=j: got[t].at[j, pl.ds(s, z)]
                waits.append(_chunked(src, dst, half, 2, send, recv, N_CHIPS * t + j, (x, y, 1 - c)))
        for cp in waits:
            cp.wait()

    return pl.pallas_call(
        body, name=name, in_specs=[HBM] * n, out_specs=[HBM] * n,
        out_shape=[jax.ShapeDtypeStruct((g.shape[0], g.shape[1] // 2, g.shape[2]), g.dtype) for g in grads],
        scratch_shapes=[pltpu.SemaphoreType.DMA((N_CHIPS * n,))] * 2,
    )(*grads)


def _scatter_start(parts, name):
    n = len(parts)
    lands = [lax.empty((N_CHIPS - 1, *p.shape[1:]), p.dtype) for p in parts]

    def body(*refs):
        send, recv = refs[2 * n], refs[2 * n + 1]
        src = refs[2 * n + 2:3 * n + 2]
        dst = refs[3 * n + 2:4 * n + 2]
        token = refs[4 * n + 2]
        x, y, c, chips = _place()
        for t in range(n):
            for r, chip in enumerate(chips):
                _remote(src[t].at[2 * chip[0] + chip[1]], dst[t].at[r], send, recv, 3 * t + r, (*chip, c)).start()
        token[...] = jnp.zeros_like(token)

    res = pl.pallas_call(
        body, name=name, in_specs=[HBM] * (2 * n),
        out_specs=[SEM, SEM] + [HBM] * (2 * n) + [pl.BlockSpec(memory_space=pltpu.VMEM)],
        out_shape=[pltpu.SemaphoreType.DMA((3 * n,)), pltpu.SemaphoreType.DMA((3 * n,))]
        + [pltpu.HBM(a.shape, a.dtype) for a in parts + lands] + [_token_shape()],
        input_output_aliases={t: 2 + t for t in range(2 * n)},
        compiler_params=pltpu.CompilerParams(has_side_effects=EFFECT),
    )(*[_hbm(a) for a in parts + lands])
    return res[0], res[1], res[2:2 + n], res[2 + n:2 + 2 * n], res[2 + 2 * n]


def _scatter_wait(send, recv, parts, lands, after, name):
    n = len(parts)

    def body(*refs):
        send_ref, recv_ref = refs[2 * n], refs[2 * n + 1]
        src = refs[2 * n + 3:3 * n + 3]
        dst = refs[3 * n + 3:4 * n + 3]
        x, y, c, chips = _place()
        for t in range(n):
            for r, chip in enumerate(chips):
                cp = _remote(src[t].at[2 * chip[0] + chip[1]], dst[t].at[r], send_ref, recv_ref, 3 * t + r,
                             (*chip, c))
                cp.wait_send()
                cp.wait_recv()

    res = pl.pallas_call(
        body, name=name, in_specs=[HBM] * (2 * n) + [SEM, SEM, pl.BlockSpec(memory_space=pl.ANY)],
        out_specs=[HBM] * (2 * n), out_shape=[pltpu.HBM(a.shape, a.dtype) for a in list(parts) + list(lands)],
        input_output_aliases={t: t for t in range(2 * n)},
        compiler_params=pltpu.CompilerParams(has_side_effects=EFFECT),
    )(*parts, *lands, send, recv, after)
    return res[:n], res[n:]


def _pair_exchange(halves, name):
    n = len(halves)

    def body(*refs):
        ins, got = refs[:n], refs[n:2 * n]
        send, recv = refs[2 * n:]
        x, y, c, _ = _place()
        waits = []
        for t in range(n):
            src = lambda s, z, t=t: ins[t].at[pl.ds(s, z)]
            dst = lambda s, z, t=t: got[t].at[pl.ds(s, z)]
            waits.append(_chunked(src, dst, ins[t].shape[0], D2D_CHUNKS, send, recv, t, (x, y, 1 - c)))
        for cp in waits:
            cp.wait()

    return pl.pallas_call(
        body, name=name, in_specs=[HBM] * n, out_specs=[HBM] * n,
        out_shape=[jax.ShapeDtypeStruct(h.shape, h.dtype) for h in halves],
        scratch_shapes=[pltpu.SemaphoreType.DMA((n,))] * 2,
    )(*halves)


def _add_pair(grad, got, c, name):
    _, half, width = got.shape
    br = _row_block(half, width, itemsize=2, budget=3 << 20)
    nb = half // br

    def body(c_ref, a_ref, b_ref, o_ref):
        o_ref[...] = (a_ref[...].astype(F32) + b_ref[...].astype(F32)).astype(BF16)

    spec = pl.BlockSpec((None, br, width), lambda j, i, c_ref: (j, i, 0))
    mine = pl.BlockSpec((None, br, width), lambda j, i, c_ref: (j, c_ref[0] * nb + i, 0))
    return pl.pallas_call(
        body, name=name,
        grid_spec=pltpu.PrefetchScalarGridSpec(num_scalar_prefetch=1, grid=(N_CHIPS, nb),
                                               in_specs=[mine, spec], out_specs=spec),
        out_shape=jax.ShapeDtypeStruct(got.shape, BF16),
        compiler_params=_params("parallel", "parallel"))(c, grad, got)


def _sum_chips(pair, others, me, name):
    _, rows, width = pair.shape
    br = _row_block(rows, width, itemsize=4, budget=3 << 20)

    def body(me_ref, p_ref, o3_ref, o_ref):
        acc = p_ref[...].astype(F32)
        for r in range(N_CHIPS - 1):
            acc = acc + o3_ref[r].astype(F32)
        o_ref[...] = acc

    return pl.pallas_call(
        body, name=name,
        grid_spec=pltpu.PrefetchScalarGridSpec(
            num_scalar_prefetch=1, grid=(rows // br,),
            in_specs=[pl.BlockSpec((None, br, width), lambda i, me_ref: (me_ref[0], i, 0)),
                      pl.BlockSpec((N_CHIPS - 1, br, width), lambda i, me_ref: (0, i, 0))],
            out_specs=pl.BlockSpec((br, width), lambda i, me_ref: (i, 0))),
        out_shape=jax.ShapeDtypeStruct((rows, width), F32),
        compiler_params=_params("parallel"))(me, pair, others)


def _adamw_halves(w, mine, theirs, m, v, c, name):
    rows, width = w.shape
    half = rows // 2
    br = _row_block(half, width, budget=3 << 19)
    nb = half // br
    c1 = 1.0 - ADAM_B1 ** ADAM_STEP
    c2 = 1.0 - ADAM_B2 ** ADAM_STEP

    def body(c_ref, w_ref, a_ref, b_ref, m_ref, v_ref, g_ref, d_ref, nm_ref, nv_ref):
        grad = jnp.where(pl.program_id(0) == c_ref[0], a_ref[...], b_ref[...])
        new_m = ADAM_B1 * m_ref[...] + (1.0 - ADAM_B1) * grad
        new_v = ADAM_B2 * v_ref[...] + (1.0 - ADAM_B2) * (grad * grad)
        g_ref[...] = grad
        d_ref[...] = -ADAM_LR * ((new_m / c1) / (jnp.sqrt(new_v / c2) + ADAM_EPS) + ADAM_WD * w_ref[...])
        nm_ref[...] = new_m
        nv_ref[...] = new_v

    full = pl.BlockSpec((br, width), lambda h, i, c_ref: (h * nb + i, 0))
    mine_spec = pl.BlockSpec((br, width), lambda h, i, c_ref: (jnp.where(h == c_ref[0], i, 0), 0))
    theirs_spec = pl.BlockSpec((br, width), lambda h, i, c_ref: (jnp.where(h == c_ref[0], 0, i), 0))
    shp = jax.ShapeDtypeStruct(w.shape, F32)
    return pl.pallas_call(
        body, name=name,
        grid_spec=pltpu.PrefetchScalarGridSpec(num_scalar_prefetch=1, grid=(2, nb),
                                               in_specs=[full, mine_spec, theirs_spec, full, full],
                                               out_specs=[full] * 4),
        out_shape=[shp] * 4,
        compiler_params=_params("parallel", "parallel"))(c, w, mine, theirs, m, v)


def _allreduce_small(part, name):
    rows = part.shape[0]

    def body(p_ref, o_ref, all_ref, send, recv):
        x, y, c, _ = _place()
        me = 4 * x + 2 * y + c
        all_ref[me] = p_ref[...]
        copies = []
        for r in range(1, N_DEV):
            to = (x ^ (r >> 2), y ^ ((r >> 1) & 1), c ^ (r & 1))
            cp = _remote(p_ref, all_ref.at[me], send, recv, r - 1, to)
            cp.start()
            copies.append(cp)
        for r in range(1, N_DEV):
            frm = (x ^ (r >> 2), y ^ ((r >> 1) & 1), c ^ (r & 1))
            slot = all_ref.at[4 * frm[0] + 2 * frm[1] + frm[2]]
            _remote(slot, slot, send, recv, r - 1, frm).wait_recv()
        for cp in copies:
            cp.wait_send()
        acc = all_ref[0]
        for dev in range(1, N_DEV):
            acc = acc + all_ref[dev]
        o_ref[...] = acc

    vmem = pl.BlockSpec(memory_space=pltpu.VMEM)
    return pl.pallas_call(
        body, name=name, in_specs=[vmem], out_specs=vmem,
        out_shape=jax.ShapeDtypeStruct(part.shape, F32),
        scratch_shapes=[pltpu.VMEM((N_DEV, rows, LANES), F32), pltpu.SemaphoreType.DMA((N_DEV - 1,)),
                        pltpu.SemaphoreType.DMA((N_DEV - 1,))],
        compiler_params=pltpu.CompilerParams(vmem_limit_bytes=VMEM_LIMIT),
    )(part)


INPUT_NAMES = None


def _weight_names():
    names = []
    for i, kind in enumerate(("gmlp", "swa", "fox", "gmlp")):
        p = f"l{i}_"
        names += [p + "ffn1_norm", p + "ffn1_wi", p + "ffn1_wo", p + "mix_norm", p + "mix_win"]
        if kind == "gmlp":
            names += [p + "gmlp_vnorm", p + "gmlp_ws", p + "gmlp_bs"]
        elif kind == "swa":
            names += [p + "swa_sinks"]
        else:
            names += [p + "fox_bf"]
        names += [p + "mix_wout", p + "ffn2_norm", p + "ffn2_wi", p + "ffn2_wo"]
    return names + ["final_norm"]


WEIGHTS = _weight_names()
MIXERS = ("gmlp", "swa", "fox", "gmlp")
BIG = ("ffn1_wi", "ffn1_wo", "mix_win", "mix_wout", "ffn2_wi", "ffn2_wo")


def _ffn_fwd(h, gain, wi, wo, tag, dep=None):
    n = _rms_fwd(h, gain, tag + "_norm", dep=dep)
    z = _matmul(n, wi, name=tag + "_up", out_dtype=BF16)
    a = _swiglu_fwd(z, tag + "_act")
    f, d = wo.shape[0] * wo.shape[1], wo.shape[2]
    out = _matmul(a, wo.reshape(f, d), name=tag + "_down", out_dtype=F32, scale=0.5, resid=h)
    return out, (h, n, z, a)


def _ffn_bwd(dout, saved, gain, wi, wo, tag, dep=None):
    h, n, z, a = saved
    f, d = wo.shape[0] * wo.shape[1], wo.shape[2]
    da = _matmul(dout, wo.reshape(f, d), tb=True, name=tag + "_bdown", out_dtype=BF16, scale=0.5, dep=dep)
    dwo = _matmul(a, dout, ta=True, name=tag + "_gdown", out_dtype=BF16, scale=0.5, dep=dep)
    dz = _swiglu_bwd(z, da, tag + "_bact")
    dn = _matmul(dz, wi, tb=True, name=tag + "_bup", out_dtype=F32)
    dwi = _matmul(n, dz, ta=True, name=tag + "_gup", out_dtype=BF16, out_shards=N_CHIPS)
    dh, dgain = _norm_bwd(h, gain, dn, dout, tag + "_bnorm")
    return dh, dgain, dwi, dwo.reshape(wo.shape)


def _natural(w_sharded, pad_to):
    ns, rows, csh = w_sharded.shape
    nat = jnp.transpose(w_sharded, (1, 0, 2)).reshape(rows, ns * csh)
    extra = (-nat.shape[1]) % pad_to
    return jnp.pad(nat, ((0, 0), (0, extra))) if extra else nat


def _mixer_fwd(kind, h, p, tag, dep=None):
    s_len, d = h.shape
    n = _rms_fwd(h, p["mix_norm"], tag + "_norm", dep=dep)
    wout = p["mix_wout"].reshape(d, d)
    if kind == "gmlp":
        zp = _matmul(n, p["mix_win"], name=tag + "_in", out_dtype=BF16)
        y = _gmlp_fwd(zp, p["gmlp_vnorm"], p["gmlp_ws"], p["gmlp_bs"], tag + "_gate")
        saved = (h, n, zp, y)
    elif kind == "swa":
        qkv = _matmul(n, p["mix_win"], name=tag + "_in", out_dtype=F32)
        q, k, v = _rope_fwd(qkv, tag + "_rope")
        y = _swa_fwd(q, k, v, p["swa_sinks"], tag + "_attn")
        saved = (h, n, q, k, v, y)
    else:
        heads = d // FOX_HEAD_DIM
        win = _natural(p["mix_win"], LANES)
        proj = _matmul(n, win, name=tag + "_in", out_dtype=F32)
        bf_row = jnp.pad(p["fox_bf"], (0, LANES - heads)).reshape(1, LANES)
        dec = _fox_decay(proj, bf_row, 3 * heads, tag + "_decay")
        dec_t = dec[:, :heads].T
        decq, deck = dec_t.reshape(heads, s_len, 1), dec_t.reshape(heads, 1, s_len)
        y = _fox_fwd(proj, decq, deck, heads, tag + "_attn")
        saved = (h, n, win, proj, bf_row, decq, deck, y)
    out = _matmul(y, wout, name=tag + "_out", out_dtype=F32, resid=h)
    return out, saved


def _mixer_bwd(kind, dout, saved, p, tag, dep=None):
    h, n = saved[0], saved[1]
    y = saved[-1]
    s_len, d = h.shape
    wout = p["mix_wout"].reshape(d, d)
    grads = {}
    dy = _matmul(dout, wout, tb=True, name=tag + "_bout", out_dtype=BF16, dep=dep)
    grads["mix_wout"] = _matmul(y, dout, ta=True, name=tag + "_gout", out_dtype=BF16,
                                dep=dep).reshape(p["mix_wout"].shape)
    if kind == "gmlp":
        zp = saved[2]
        dzp, dws, dbst, dvg = _gmlp_bwd(zp, dy, p["gmlp_vnorm"], p["gmlp_ws"], p["gmlp_bs"], tag + "_bgate")
        grads.update(gmlp_ws=dws, gmlp_bs=dbst.T, gmlp_vnorm=dvg.reshape(d))
        dn = _matmul(dzp, p["mix_win"], tb=True, name=tag + "_bin", out_dtype=F32)
        grads["mix_win"] = _matmul(n, dzp, ta=True, name=tag + "_gin", out_dtype=BF16, out_shards=N_CHIPS)
    elif kind == "swa":
        q, k, v = saved[2:5]
        dq, dk, dv, dsinks = _swa_bwd(q, k, v, p["swa_sinks"], dy, tag + "_battn")
        grads["swa_sinks"] = dsinks[0, :p["swa_sinks"].shape[0]]
        dqkv = _rope_bwd(dq, dk, dv, tag + "_brope")
        dn = _matmul(dqkv, p["mix_win"], tb=True, name=tag + "_bin", out_dtype=F32)
        grads["mix_win"] = _matmul(n, dqkv, ta=True, name=tag + "_gin", out_dtype=BF16, out_shards=N_CHIPS)
    else:
        win, proj, bf_row, decq, deck = saved[2:7]
        heads = d // FOX_HEAD_DIM
        dq, dk, dv, ddq, ddk = _fox_bwd(proj, decq, deck, dy, heads, tag + "_battn")
        widen = lambda t: jnp.pad(t.reshape(heads, s_len).T, ((0, 0), (0, LANES - heads)))
        dfl, dbf = _fox_decay_bwd(widen(ddq), widen(ddk), proj, bf_row, 3 * heads, heads, tag + "_bdecay")
        grads["fox_bf"] = dbf[0, :heads]
        dproj = jnp.concatenate([dq, dk.astype(BF16), dv.astype(BF16), dfl], axis=1)
        dn = _matmul(dproj, win, tb=True, name=tag + "_bin", out_dtype=F32)
        dwin = _matmul(n, dproj, ta=True, name=tag + "_gin", out_dtype=BF16)
        ns, rows, csh = p["mix_win"].shape
        grads["mix_win"] = jnp.transpose(dwin[:, :ns * csh].reshape(rows, ns, csh), (1, 0, 2))
    dh, dgain = _norm_bwd(h, p["mix_norm"], dn, dout, tag + "_bnorm")
    grads["mix_norm"] = dgain.reshape(d)
    return dh, grads


def _pack_small(arrays):
    flat = jnp.concatenate([a.reshape(-1).astype(F32) for a in arrays])
    pad = (-flat.shape[0]) % (512 * LANES)
    return jnp.pad(flat, (0, pad)).reshape(-1, LANES)


def _unpack_small(packed, like):
    flat, out, pos = packed.reshape(-1), [], 0
    for a in like:
        out.append(flat[pos:pos + a.size].reshape(a.shape))
        pos += a.size
    return out


def _step(inp):
    x, target = inp["x"][0], inp["loss_target"][0]
    d = x.shape[1]

    core = lax.axis_index("c").astype(jnp.int32).reshape(1)
    chip = (2 * lax.axis_index("x") + lax.axis_index("y")).astype(jnp.int32).reshape(1)

    groups = []
    for i in range(len(MIXERS)):
        groups += [(i, "ffn1", [f"l{i}_ffn1_wi", f"l{i}_ffn1_wo"]), (i, "mix", [f"l{i}_mix_win", f"l{i}_mix_wout"]),
                   (i, "ffn2", [f"l{i}_ffn2_wi", f"l{i}_ffn2_wo"])]

    def layer_params(i, full):
        p = {nm[len(f"l{i}_"):]: inp[nm] for nm in WEIGHTS if nm.startswith(f"l{i}_")}
        p.update({nm[len(f"l{i}_"):]: w for nm, w in full.items()})
        return p

    bufs = [[_cast_into_slot(inp[nm], chip, nm + "_cast") for nm in groups[0][2]]]
    started = _gather_start(bufs[0], x, "g0_gather_start")
    bufs += [[_cast_into_slot(inp[nm], chip, nm + "_cast", dep=started[3]) for nm in names]
             for _, _, names in groups[1:]]
    h, saved, fulls = x, [], []
    for g, (i, part, names) in enumerate(groups):
        send, recv, thru, _ = started
        landed = _gather_wait(send, recv, thru, h, f"g{g}_gather_wait")
        full = dict(zip(names, _gather_forward(landed, f"g{g}_gather_forward")))
        dep = None
        if g + 1 < len(groups):
            started = _gather_start(bufs[g + 1], full[names[0]], f"g{g + 1}_gather_start")
            dep = started[3]
        p = layer_params(i, full)
        if part == "mix":
            h, s = _mixer_fwd(MIXERS[i], h, p, f"l{i}_mix", dep=dep)
        else:
            h, s = _ffn_fwd(h, p[part + "_norm"], p[part + "_wi"], p[part + "_wo"], f"l{i}_{part}", dep=dep)
        saved.append(s)
        fulls.append(full)
    loss_part, dh, dfinal = _loss_head(h, inp["final_norm"], target, "loss_head")
    loss = lax.psum(loss_part, ("x", "y", "c"))

    small_grads = {"final_norm": dfinal.reshape(d)}
    outs = {}
    def finish(pending, after):
        g, names, send, recv, pair, lands = pending
        pair, others = _scatter_wait(send, recv, pair, lands, after, f"g{g}_rs_wait")
        halves = [_sum_chips(p, o, chip, nm + "_rs_sum") for p, o, nm in zip(pair, others, names)]
        theirs = _pair_exchange(halves, f"g{g}_rs_join")
        for nm, mine, other in zip(names, halves, theirs):
            outs[nm] = tuple(_adamw_halves(inp[nm], mine, other, inp["m_" + nm], inp["v_" + nm], core,
                                           nm + "_adamw"))

    pending, dep = None, None
    for g in reversed(range(len(groups))):
        i, part, names = groups[g]
        p = layer_params(i, fulls[g])
        if part == "mix":
            dh, mg = _mixer_bwd(MIXERS[i], dh, saved[g], p, f"l{i}_mix", dep=dep)
            grads = [mg.pop("mix_win"), mg.pop("mix_wout")]
            small_grads.update({f"l{i}_{key}": val for key, val in mg.items()})
        else:
            dh, g_norm, dwi, dwo = _ffn_bwd(dh, saved[g], p[part + "_norm"], p[part + "_wi"], p[part + "_wo"],
                                            f"l{i}_{part}", dep=dep)
            small_grads[f"l{i}_{part}_norm"] = g_norm.reshape(d)
            grads = [dwi, dwo]
        got = _pair_swap(grads, f"g{g}_rs_pair")
        pair = [_add_pair(gr, b, core, nm + "_rs_add") for gr, b, nm in zip(grads, got, names)]
        send, recv, pair, lands, dep = _scatter_start(pair, f"g{g}_rs_start")
        if pending is not None:
            finish(pending, dep)
        pending = (g, names, send, recv, pair, lands)

    small_names = [nm for nm in WEIGHTS if nm.split("_", 1)[1] not in BIG]
    total = _allreduce_small(_pack_small([small_grads[nm] for nm in small_names]), "small_allreduce")
    like = [inp[nm] for nm in small_names]
    upd = _adamw(_pack_small(like), total, _pack_small([inp["m_" + nm] for nm in small_names]),
                 _pack_small([inp["v_" + nm] for nm in small_names]), "small_adamw")
    finish(pending, upd[0])
    unpacked = [_unpack_small(t, like) for t in (total, *upd)]
    for k, nm in enumerate(small_names):
        outs[nm] = tuple(u[k] for u in unpacked)

    result = [loss, dh[None]]
    for part in range(4):
        result += [outs[nm][part] for nm in WEIGHTS]
    return tuple(result)


def kernel(x, l0_ffn1_norm, l0_ffn1_wi, l0_ffn1_wo, l0_mix_norm, l0_mix_win, l0_gmlp_vnorm, l0_gmlp_ws, l0_gmlp_bs, l0_mix_wout, l0_ffn2_norm, l0_ffn2_wi, l0_ffn2_wo, l1_ffn1_norm, l1_ffn1_wi, l1_ffn1_wo, l1_mix_norm, l1_mix_win, l1_swa_sinks, l1_mix_wout, l1_ffn2_norm, l1_ffn2_wi, l1_ffn2_wo, l2_ffn1_norm, l2_ffn1_wi, l2_ffn1_wo, l2_mix_norm, l2_mix_win, l2_fox_bf, l2_mix_wout, l2_ffn2_norm, l2_ffn2_wi, l2_ffn2_wo, l3_ffn1_norm, l3_ffn1_wi, l3_ffn1_wo, l3_mix_norm, l3_mix_win, l3_gmlp_vnorm, l3_gmlp_ws, l3_gmlp_bs, l3_mix_wout, l3_ffn2_norm, l3_ffn2_wi, l3_ffn2_wo, final_norm, loss_target, m_l0_ffn1_norm, m_l0_ffn1_wi, m_l0_ffn1_wo, m_l0_mix_norm, m_l0_mix_win, m_l0_gmlp_vnorm, m_l0_gmlp_ws, m_l0_gmlp_bs, m_l0_mix_wout, m_l0_ffn2_norm, m_l0_ffn2_wi, m_l0_ffn2_wo, m_l1_ffn1_norm, m_l1_ffn1_wi, m_l1_ffn1_wo, m_l1_mix_norm, m_l1_mix_win, m_l1_swa_sinks, m_l1_mix_wout, m_l1_ffn2_norm, m_l1_ffn2_wi, m_l1_ffn2_wo, m_l2_ffn1_norm, m_l2_ffn1_wi, m_l2_ffn1_wo, m_l2_mix_norm, m_l2_mix_win, m_l2_fox_bf, m_l2_mix_wout, m_l2_ffn2_norm, m_l2_ffn2_wi, m_l2_ffn2_wo, m_l3_ffn1_norm, m_l3_ffn1_wi, m_l3_ffn1_wo, m_l3_mix_norm, m_l3_mix_win, m_l3_gmlp_vnorm, m_l3_gmlp_ws, m_l3_gmlp_bs, m_l3_mix_wout, m_l3_ffn2_norm, m_l3_ffn2_wi, m_l3_ffn2_wo, m_final_norm, v_l0_ffn1_norm, v_l0_ffn1_wi, v_l0_ffn1_wo, v_l0_mix_norm, v_l0_mix_win, v_l0_gmlp_vnorm, v_l0_gmlp_ws, v_l0_gmlp_bs, v_l0_mix_wout, v_l0_ffn2_norm, v_l0_ffn2_wi, v_l0_ffn2_wo, v_l1_ffn1_norm, v_l1_ffn1_wi, v_l1_ffn1_wo, v_l1_mix_norm, v_l1_mix_win, v_l1_swa_sinks, v_l1_mix_wout, v_l1_ffn2_norm, v_l1_ffn2_wi, v_l1_ffn2_wo, v_l2_ffn1_norm, v_l2_ffn1_wi, v_l2_ffn1_wo, v_l2_mix_norm, v_l2_mix_win, v_l2_fox_bf, v_l2_mix_wout, v_l2_ffn2_norm, v_l2_ffn2_wi, v_l2_ffn2_wo, v_l3_ffn1_norm, v_l3_ffn1_wi, v_l3_ffn1_wo, v_l3_mix_norm, v_l3_mix_win, v_l3_gmlp_vnorm, v_l3_gmlp_ws, v_l3_gmlp_bs, v_l3_mix_wout, v_l3_ffn2_norm, v_l3_ffn2_wi, v_l3_ffn2_wo, v_final_norm):
    return _step(dict(locals()))
```

```python
import functools
import math

import jax
import jax.numpy as jnp
from jax import lax
from jax.experimental import pallas as pl
from jax.experimental.pallas import tpu as pltpu

F32 = jnp.float32
BF16 = jnp.bfloat16

NORM_EPS = 1e-5
NEG_INF = -1e30
BLOCK = 128
GMLP_GROUPS = 16
SWA_HEAD_DIM = 64
SWA_GROUP = 8
ROPE_DIM = SWA_HEAD_DIM // 4
ROPE_THETA = 500000.0
FOX_HEAD_DIM = 128
ADAM_LR = 0.001
ADAM_B1 = 0.9
ADAM_B2 = 0.999
ADAM_EPS = 1e-08
ADAM_WD = 0.01
ADAM_STEP = 10
N_CHIPS = 4
N_DEV = 8
LANES = 128
VMEM_LIMIT = 56 * 1024 * 1024
MESH = pl.DeviceIdType.MESH
HBM = pl.BlockSpec(memory_space=pltpu.HBM)
SEM = pl.BlockSpec(memory_space=pltpu.SEMAPHORE)
EFFECT = pltpu.SideEffectType.DATAFLOW_SIDE_EFFECTING

MM_TILES = (1024, 1408, 896, 640, 512, 384, 256, 128)


def _pick(n, prefs):
    for p in prefs:
        if p <= n and n % p == 0:
            return p
    return n


def _params(*sem):
    return pltpu.CompilerParams(dimension_semantics=sem or None, vmem_limit_bytes=VMEM_LIMIT)


def _cols(arr):
    return arr.shape[-1] * (arr.shape[0] if arr.ndim == 3 else 1)


def _mat_spec(arr, rb, cb, ridx, cidx):
    if arr.ndim == 2:
        return pl.BlockSpec((rb, cb), lambda j, i, k: (ridx(j, i, k), cidx(j, i, k)))
    per = arr.shape[2] // cb
    return pl.BlockSpec((None, rb, cb),
                        lambda j, i, k: (cidx(j, i, k) // per, ridx(j, i, k), cidx(j, i, k) % per))


def _matmul(a, b, *, name, out_dtype, ta=False, tb=False, out_shards=1, scale=1.0, resid=None, dep=None):
    m_dim, k_dim = (a.shape[1], a.shape[0]) if ta else a.shape
    n_dim = b.shape[-2] if tb else _cols(b)
    assert k_dim == (_cols(b) if tb else b.shape[-2]), (a.shape, b.shape, ta, tb)
    n_unit = n_dim // out_shards
    if b.ndim == 3 and not tb:
        n_unit = math.gcd(n_unit, b.shape[2])
    k_unit = b.shape[2] if (b.ndim == 3 and tb) else k_dim
    bm = _pick(m_dim, MM_TILES)
    bn = _pick(n_unit, MM_TILES)
    bk = k_unit if k_unit <= 2048 else _pick(k_unit, MM_TILES)
    nk = k_dim // bk
    i_of, j_of, k_of = (lambda j, i, k: i), (lambda j, i, k: j), (lambda j, i, k: k)
    a_spec = _mat_spec(a, bk, bm, k_of, i_of) if ta else _mat_spec(a, bm, bk, i_of, k_of)
    b_spec = _mat_spec(b, bn, bk, j_of, k_of) if tb else _mat_spec(b, bk, bn, k_of, j_of)
    out_shape = (m_dim, n_dim) if out_shards == 1 else (out_shards, m_dim, n_dim // out_shards)
    out = jax.ShapeDtypeStruct(out_shape, out_dtype)
    o_spec = _mat_spec(out, bm, bn, i_of, j_of)
    dims = (((0 if ta else 1,), (1 if tb else 0,)), ((), ()))
    operands, in_specs = [a, b], [a_spec, b_spec]
    if resid is not None:
        operands.append(resid)
        in_specs.append(_mat_spec(resid, bm, bn, i_of, j_of))
    if dep is not None:
        operands.append(dep)
        in_specs.append(pl.BlockSpec(dep.shape, lambda j, i, k: (0, 0)))
    n_in = len(operands)

    def body(*refs):
        a_ref, b_ref = refs[0], refs[1]
        r_ref = refs[2] if resid is not None else None
        o_ref = refs[n_in]
        part = lax.dot_general(a_ref[...].astype(BF16), b_ref[...].astype(BF16), dims,
                               preferred_element_type=F32)

        def finish(acc):
            val = acc * scale if scale != 1.0 else acc
            if r_ref is not None:
                val = r_ref[...] + val
            o_ref[...] = val.astype(o_ref.dtype)

        if nk == 1:
            finish(part)
        else:
            acc_ref = refs[-1]
            k = pl.program_id(2)

            @pl.when(k == 0)
            def _():
                acc_ref[...] = part

            @pl.when(k > 0)
            def _():
                acc_ref[...] += part

            @pl.when(k == nk - 1)
            def _():
                finish(acc_ref[...])

    return pl.pallas_call(
        body, name=name, grid=(n_dim // bn, m_dim // bm, nk),
        in_specs=in_specs, out_specs=o_spec, out_shape=out,
        scratch_shapes=[pltpu.VMEM((bm, bn), F32)] if nk > 1 else [],
        compiler_params=_params("parallel", "parallel", "arbitrary"),
    )(*operands)


def _row_block(rows, width, itemsize=4, budget=2 << 20):
    for br in (512, 256, 128, 64, 32, 16, 8):
        if rows % br == 0 and br * width * itemsize <= budget:
            return br
    return rows


def _rms_fwd(h, g, name, dep=None):
    s_len, d = h.shape
    br = _row_block(s_len, d)

    def body(h_ref, g_ref, *rest):
        o_ref = rest[-1]
        x = h_ref[...]
        r = lax.rsqrt(jnp.mean(x * x, axis=-1, keepdims=True) + NORM_EPS)
        o_ref[...] = (x * r * g_ref[...]).astype(BF16)

    spec = pl.BlockSpec((br, d), lambda i: (i, 0))
    operands = [h, g.reshape(1, d)] + ([dep] if dep is not None else [])
    in_specs = [spec, pl.BlockSpec((1, d), lambda i: (0, 0))]
    if dep is not None:
        in_specs.append(pl.BlockSpec(dep.shape, lambda i: (0, 0)))
    return pl.pallas_call(body, name=name, grid=(s_len // br,), in_specs=in_specs, out_specs=spec,
                          out_shape=jax.ShapeDtypeStruct((s_len, d), BF16),
                          compiler_params=_params("parallel"))(*operands)


def _rms_bwd_rows(x, g, dn):
    r = lax.rsqrt(jnp.mean(x * x, axis=-1, keepdims=True) + NORM_EPS)
    xhat = x * r
    gdn = dn * g
    dx = r * (gdn - xhat * jnp.mean(gdn * xhat, axis=-1, keepdims=True))
    return dx, dn * xhat


def _norm_bwd(h, g, dn, dres, name):
    s_len, d = h.shape
    br = _row_block(s_len, d, budget=1 << 20)

    def body(h_ref, g_ref, dn_ref, dres_ref, dh_ref, dg_ref):
        dx, dg_rows = _rms_bwd_rows(h_ref[...], g_ref[...], dn_ref[...].astype(F32))
        dh_ref[...] = dres_ref[...] + dx

        @pl.when(pl.program_id(0) == 0)
        def _():
            dg_ref[...] = jnp.zeros_like(dg_ref)

        dg_ref[...] += jnp.sum(dg_rows, axis=0, keepdims=True)

    spec = pl.BlockSpec((br, d), lambda i: (i, 0))
    vec = pl.BlockSpec((1, d), lambda i: (0, 0))
    return pl.pallas_call(body, name=name, grid=(s_len // br,),
                          in_specs=[spec, vec, spec, spec], out_specs=[spec, vec],
                          out_shape=[jax.ShapeDtypeStruct((s_len, d), F32),
                                     jax.ShapeDtypeStruct((1, d), F32)],
                          compiler_params=_params("arbitrary"))(h, g.reshape(1, d), dn, dres)


def _sigmoid(x):
    return 1.0 / (1.0 + jnp.exp(-x))


def _swiglu_fwd(z, name):
    s_len, f2 = z.shape
    f = f2 // 2
    br = _row_block(s_len, f2, budget=3 << 20)

    def body(z_ref, a_ref):
        gate = z_ref[:, :f].astype(F32)
        up = z_ref[:, f:].astype(F32)
        a_ref[...] = (gate * _sigmoid(gate) * up).astype(BF16)

    return pl.pallas_call(body, name=name, grid=(s_len // br,),
                          in_specs=[pl.BlockSpec((br, f2), lambda i: (i, 0))],
                          out_specs=pl.BlockSpec((br, f), lambda i: (i, 0)),
                          out_shape=jax.ShapeDtypeStruct((s_len, f), BF16),
                          compiler_params=_params("parallel"))(z)


def _swiglu_bwd(z, da, name):
    s_len, f2 = z.shape
    f = f2 // 2
    br = _row_block(s_len, f2, budget=3 << 20)

    def body(z_ref, da_ref, dz_ref):
        gate = z_ref[:, :f].astype(F32)
        up = z_ref[:, f:].astype(F32)
        d = da_ref[...].astype(F32)
        sig = _sigmoid(gate)
        dz_ref[:, :f] = (d * up * (sig * (1.0 + gate * (1.0 - sig)))).astype(BF16)
        dz_ref[:, f:] = (d * gate * sig).astype(BF16)

    return pl.pallas_call(body, name=name, grid=(s_len // br,),
                          in_specs=[pl.BlockSpec((br, f2), lambda i: (i, 0)),
                                    pl.BlockSpec((br, f), lambda i: (i, 0))],
                          out_specs=pl.BlockSpec((br, f2), lambda i: (i, 0)),
                          out_shape=jax.ShapeDtypeStruct((s_len, f2), BF16),
                          compiler_params=_params("parallel"))(z, da)


def _loss_head(h, g, target, name):
    s_len, d = h.shape
    br = _row_block(s_len, d, budget=1 << 20)

    def body(h_ref, g_ref, t_ref, loss_ref, dh_ref, dg_ref):
        x = h_ref[...]
        gain = g_ref[...]
        r = lax.rsqrt(jnp.mean(x * x, axis=-1, keepdims=True) + NORM_EPS)
        err = x * r * gain - t_ref[...]
        part = 0.5 * jnp.sum(jnp.mean(err * err, axis=-1, keepdims=True), axis=0, keepdims=True)
        dx, dg_rows = _rms_bwd_rows(x, gain, err * (1.0 / d))
        dh_ref[...] = dx

        @pl.when(pl.program_id(0) == 0)
        def _():
            dg_ref[...] = jnp.zeros_like(dg_ref)
            loss_ref[...] = jnp.zeros_like(loss_ref)

        dg_ref[...] += jnp.sum(dg_rows, axis=0, keepdims=True)
        loss_ref[...] += jnp.broadcast_to(part, loss_ref.shape)

    spec = pl.BlockSpec((br, d), lambda i: (i, 0))
    vec = pl.BlockSpec((1, d), lambda i: (0, 0))
    one = pl.BlockSpec((1, LANES), lambda i: (0, 0))
    loss, dh, dg = pl.pallas_call(
        body, name=name, grid=(s_len // br,), in_specs=[spec, vec, spec],
        out_specs=[one, spec, vec],
        out_shape=[jax.ShapeDtypeStruct((1, LANES), F32), jax.ShapeDtypeStruct((s_len, d), F32),
                   jax.ShapeDtypeStruct((1, d), F32)],
        compiler_params=_params("arbitrary"))(h, g.reshape(1, d), target)
    return loss[0, 0], dh, dg


def _adamw(w, g, m, v, name):
    rows, width = w.shape
    br = _row_block(rows, width, budget=1 << 20)
    c1 = 1.0 - ADAM_B1 ** ADAM_STEP
    c2 = 1.0 - ADAM_B2 ** ADAM_STEP

    def body(w_ref, g_ref, m_ref, v_ref, d_ref, nm_ref, nv_ref):
        grad = g_ref[...]
        new_m = ADAM_B1 * m_ref[...] + (1.0 - ADAM_B1) * grad
        new_v = ADAM_B2 * v_ref[...] + (1.0 - ADAM_B2) * (grad * grad)
        d_ref[...] = -ADAM_LR * ((new_m / c1) / (jnp.sqrt(new_v / c2) + ADAM_EPS) + ADAM_WD * w_ref[...])
        nm_ref[...] = new_m
        nv_ref[...] = new_v

    spec = pl.BlockSpec((br, width), lambda i: (i, 0))
    shp = jax.ShapeDtypeStruct(w.shape, F32)
    return pl.pallas_call(body, name=name, grid=(rows // br,), in_specs=[spec] * 4,
                          out_specs=[spec] * 3, out_shape=[shp] * 3,
                          compiler_params=_params("parallel"))(w, g, m, v)


def _gelu(x):
    return 0.5 * x * (1.0 + lax.erf(x * (2.0 ** -0.5)))


def _gelu_grad(x):
    return 0.5 * (1.0 + lax.erf(x * (2.0 ** -0.5))) + x * jnp.exp(-0.5 * x * x) * ((2.0 * math.pi) ** -0.5)


def _tril_mask():
    row = lax.broadcasted_iota(jnp.int32, (BLOCK, BLOCK), 0)
    col = lax.broadcasted_iota(jnp.int32, (BLOCK, BLOCK), 1)
    return col <= row


def _gmlp_specs(s_len, d):
    gw = d // GMLP_GROUPS
    zp = pl.BlockSpec((BLOCK, 2 * d), lambda i: (i, 0))
    row = pl.BlockSpec((BLOCK, d), lambda i: (i, 0))
    vec = pl.BlockSpec((1, d), lambda i: (0, 0))
    ws = pl.BlockSpec((GMLP_GROUPS, BLOCK, BLOCK), lambda i: (0, 0, 0))
    bst = pl.BlockSpec((BLOCK, GMLP_GROUPS), lambda i: (0, 0))
    return gw, zp, row, vec, ws, bst


def _gmlp_fwd(zp, vgain, ws, bs, name):
    s_len, d2 = zp.shape
    d = d2 // 2
    gw, zp_spec, row_spec, vec_spec, ws_spec, bst_spec = _gmlp_specs(s_len, d)

    def body(zp_ref, vg_ref, ws_ref, bst_ref, y_ref):
        u = _gelu(zp_ref[:, :d].astype(F32))
        vv = _gelu(zp_ref[:, d:].astype(F32))
        r = lax.rsqrt(jnp.mean(vv * vv, axis=-1, keepdims=True) + NORM_EPS)
        vn = (vv * r * vg_ref[...]).astype(BF16)
        mask = _tril_mask()
        for g in range(GMLP_GROUPS):
            cols = slice(g * gw, (g + 1) * gw)
            wg = jnp.where(mask, ws_ref[g], 0.0).astype(BF16)
            mixed = jnp.dot(wg, vn[:, cols], preferred_element_type=F32) + bst_ref[:, g:g + 1]
            y_ref[:, cols] = (u[:, cols] * mixed).astype(BF16)

    return pl.pallas_call(body, name=name, grid=(s_len // BLOCK,),
                          in_specs=[zp_spec, vec_spec, ws_spec, bst_spec], out_specs=row_spec,
                          out_shape=jax.ShapeDtypeStruct((s_len, d), BF16),
                          compiler_params=_params("parallel"))(zp, vgain.reshape(1, d), ws, bs.T)


def _gmlp_bwd(zp, dy, vgain, ws, bs, name):
    s_len, d2 = zp.shape
    d = d2 // 2
    gw, zp_spec, row_spec, vec_spec, ws_spec, bst_spec = _gmlp_specs(s_len, d)

    def body(zp_ref, dy_ref, vg_ref, ws_ref, bst_ref, dzp_ref, dws_ref, dbst_ref, dvg_ref, dvn_ref):
        @pl.when(pl.program_id(0) == 0)
        def _():
            dws_ref[...] = jnp.zeros_like(dws_ref)
            dbst_ref[...] = jnp.zeros_like(dbst_ref)
            dvg_ref[...] = jnp.zeros_like(dvg_ref)

        zu = zp_ref[:, :d].astype(F32)
        zv = zp_ref[:, d:].astype(F32)
        u = _gelu(zu)
        vv = _gelu(zv)
        r = lax.rsqrt(jnp.mean(vv * vv, axis=-1, keepdims=True) + NORM_EPS)
        vhat = vv * r
        gain = vg_ref[...]
        vn = (vhat * gain).astype(BF16)
        dyf = dy_ref[...].astype(F32)
        dmixed = dyf * u
        dmixed_b = dmixed.astype(BF16)
        mask = _tril_mask()
        lane = lax.broadcasted_iota(jnp.int32, (BLOCK, GMLP_GROUPS), 1)
        dbs_step = jnp.zeros((BLOCK, GMLP_GROUPS), F32)
        for g in range(GMLP_GROUPS):
            cols = slice(g * gw, (g + 1) * gw)
            wg = jnp.where(mask, ws_ref[g], 0.0).astype(BF16)
            mixed = jnp.dot(wg, vn[:, cols], preferred_element_type=F32) + bst_ref[:, g:g + 1]
            dzp_ref[:, cols] = (dyf[:, cols] * mixed * _gelu_grad(zu[:, cols])).astype(BF16)
            dm = dmixed_b[:, cols]
            dw = lax.dot_general(dm, vn[:, cols], (((1,), (1,)), ((), ())), preferred_element_type=F32)
            dws_ref[g] += jnp.where(mask, dw, 0.0)
            dbs_step = dbs_step + jnp.where(lane == g, jnp.sum(dmixed[:, cols], axis=-1, keepdims=True), 0.0)
            dvn_ref[:, cols] = lax.dot_general(wg, dm, (((0,), (0,)), ((), ())), preferred_element_type=F32)
        dbst_ref[...] += dbs_step
        dvn = dvn_ref[...]
        dvg_ref[...] += jnp.sum(dvn * vhat, axis=0, keepdims=True)
        dvhat = dvn * gain
        dvv = r * (dvhat - vhat * jnp.mean(dvhat * vhat, axis=-1, keepdims=True))
        dzp_ref[:, d:] = (dvv * _gelu_grad(zv)).astype(BF16)

    return pl.pallas_call(
        body, name=name, grid=(s_len // BLOCK,),
        in_specs=[zp_spec, row_spec, vec_spec, ws_spec, bst_spec],
        out_specs=[zp_spec, ws_spec, bst_spec, vec_spec],
        out_shape=[jax.ShapeDtypeStruct((s_len, d2), BF16), jax.ShapeDtypeStruct(ws.shape, F32),
                   jax.ShapeDtypeStruct((BLOCK, GMLP_GROUPS), F32), jax.ShapeDtypeStruct((1, d), F32)],
        scratch_shapes=[pltpu.VMEM((BLOCK, d), F32)],
        compiler_params=_params("arbitrary"))(zp, dy, vgain.reshape(1, d), ws, bs.T)


def _rope_tables(s_len, sign):
    half = ROPE_DIM // 2
    inv_freq = ROPE_THETA ** (-(jnp.arange(half, dtype=F32) * 2.0 / ROPE_DIM))
    ang = jnp.arange(s_len, dtype=F32)[:, None] * inv_freq[None, :]
    cos, sin = jnp.cos(ang), jnp.sin(ang) * sign
    pad = jnp.zeros((s_len, SWA_HEAD_DIM - ROPE_DIM), F32)
    zero = jnp.zeros_like(sin)
    cos_t = jnp.concatenate([cos, cos, pad + 1.0], axis=1)
    sin_up = jnp.concatenate([-sin, zero, pad], axis=1)
    sin_dn = jnp.concatenate([zero, sin, pad], axis=1)
    return [jnp.tile(t, (1, LANES // SWA_HEAD_DIM)) for t in (cos_t, sin_up, sin_dn)]


def _rotate(x, cos_t, sin_up, sin_dn):
    width = x.shape[-1]
    half = ROPE_DIM // 2
    reps = width // cos_t.shape[-1]
    if reps > 1:
        cos_t, sin_up, sin_dn = (jnp.tile(t, (1, reps)) for t in (cos_t, sin_up, sin_dn))
    elif reps == 0:
        cos_t, sin_up, sin_dn = (t[:, :width] for t in (cos_t, sin_up, sin_dn))
    return x * cos_t + pltpu.roll(x, width - half, 1) * sin_up + pltpu.roll(x, half, 1) * sin_dn


def _rope_fwd(qkv, name):
    s_len, total = qkv.shape
    wkv = total // (SWA_GROUP + 2)
    wq = SWA_GROUP * wkv
    br = _row_block(s_len, total, budget=2 << 20)
    tables = _rope_tables(s_len, 1.0)

    def body(q_ref, k_ref, v_ref, c_ref, su_ref, sd_ref, qo_ref, ko_ref, vo_ref):
        t = (c_ref[...], su_ref[...], sd_ref[...])
        qo_ref[...] = _rotate(q_ref[...], *t).astype(BF16)
        ko_ref[...] = _rotate(k_ref[...], *t).astype(BF16)
        vo_ref[...] = v_ref[...].astype(BF16)

    qs = pl.BlockSpec((br, wq), lambda i: (i, 0))
    ks = pl.BlockSpec((br, wkv), lambda i: (i, SWA_GROUP))
    vs = pl.BlockSpec((br, wkv), lambda i: (i, SWA_GROUP + 1))
    ts = pl.BlockSpec((br, LANES), lambda i: (i, 0))
    kv_out = pl.BlockSpec((br, wkv), lambda i: (i, 0))
    return pl.pallas_call(
        body, name=name, grid=(s_len // br,), in_specs=[qs, ks, vs, ts, ts, ts],
        out_specs=[qs, kv_out, kv_out],
        out_shape=[jax.ShapeDtypeStruct((s_len, wq), BF16), jax.ShapeDtypeStruct((s_len, wkv), BF16),
                   jax.ShapeDtypeStruct((s_len, wkv), BF16)],
        compiler_params=_params("parallel"))(qkv, qkv, qkv, *tables)


def _rope_bwd(dq, dk, dv, name):
    s_len, wq = dq.shape
    wkv = dk.shape[1]
    br = _row_block(s_len, wq + 2 * wkv, budget=2 << 20)
    tables = _rope_tables(s_len, -1.0)

    def body(q_ref, k_ref, v_ref, c_ref, su_ref, sd_ref, o_ref):
        t = (c_ref[...], su_ref[...], sd_ref[...])
        o_ref[:, :wq] = _rotate(q_ref[...], *t).astype(BF16)
        o_ref[:, wq:wq + wkv] = _rotate(k_ref[...], *t).astype(BF16)
        o_ref[:, wq + wkv:] = v_ref[...].astype(BF16)

    qs = pl.BlockSpec((br, wq), lambda i: (i, 0))
    kvs = pl.BlockSpec((br, wkv), lambda i: (i, 0))
    ts = pl.BlockSpec((br, LANES), lambda i: (i, 0))
    return pl.pallas_call(
        body, name=name, grid=(s_len // br,), in_specs=[qs, kvs, kvs, ts, ts, ts],
        out_specs=pl.BlockSpec((br, wq + 2 * wkv), lambda i: (i, 0)),
        out_shape=jax.ShapeDtypeStruct((s_len, wq + 2 * wkv), BF16),
        compiler_params=_params("parallel"))(dq, dk, dv, *tables)


def _swa_valid(i):
    row = lax.broadcasted_iota(jnp.int32, (BLOCK, 2 * BLOCK), 0)
    col = lax.broadcasted_iota(jnp.int32, (BLOCK, 2 * BLOCK), 1)
    return (col - BLOCK <= row) & (row < col) & ((col >= BLOCK) | (i > 0))


def _swa_specs(wq, wkv):
    q_spec = pl.BlockSpec((BLOCK, wq), lambda i: (i, 0))
    cur = pl.BlockSpec((BLOCK, wkv), lambda i: (i, 0))
    prev = pl.BlockSpec((BLOCK, wkv), lambda i: (jnp.maximum(i - 1, 0), 0))
    sink = pl.BlockSpec(memory_space=pltpu.SMEM)
    return q_spec, cur, prev, sink


def _swa_probs(q_h, k_cat, valid, sink):
    s = lax.dot_general(q_h, k_cat, (((1,), (1,)), ((), ())), preferred_element_type=F32)
    s = jnp.where(valid, s * (SWA_HEAD_DIM ** -0.5), NEG_INF)
    m = jnp.maximum(jnp.max(s, axis=-1, keepdims=True), sink)
    p = jnp.exp(s - m)
    e_sink = jnp.exp(sink - m)
    denom = jnp.sum(p, axis=-1, keepdims=True) + e_sink
    return p / denom, e_sink / denom


def _swa_fwd(q, k, v, sinks, name):
    s_len, wq = q.shape
    wkv = k.shape[1]
    hd = SWA_HEAD_DIM
    q_spec, cur, prev, sink_spec = _swa_specs(wq, wkv)

    def body(q_ref, kc_ref, kp_ref, vc_ref, vp_ref, sink_ref, o_ref):
        valid = _swa_valid(pl.program_id(0))
        for j in range(wkv // hd):
            lanes = slice(j * hd, (j + 1) * hd)
            k_cat = jnp.concatenate([kp_ref[:, lanes], kc_ref[:, lanes]], axis=0)
            v_cat = jnp.concatenate([vp_ref[:, lanes], vc_ref[:, lanes]], axis=0)
            for hh in range(SWA_GROUP):
                h = j * SWA_GROUP + hh
                pn, _ = _swa_probs(q_ref[:, h * hd:(h + 1) * hd], k_cat, valid, sink_ref[h])
                o_ref[:, h * hd:(h + 1) * hd] = jnp.dot(
                    pn.astype(BF16), v_cat, preferred_element_type=F32).astype(BF16)

    return pl.pallas_call(body, name=name, grid=(s_len // BLOCK,),
                          in_specs=[q_spec, cur, prev, cur, prev, sink_spec], out_specs=q_spec,
                          out_shape=jax.ShapeDtypeStruct((s_len, wq), BF16),
                          compiler_params=_params("parallel"))(q, k, k, v, v, sinks)


def _swa_bwd(q, k, v, sinks, do, name):
    s_len, wq = q.shape
    wkv = k.shape[1]
    hd = SWA_HEAD_DIM
    q_spec, cur, prev, sink_spec = _swa_specs(wq, wkv)
    full = pl.BlockSpec((s_len, wkv), lambda i: (0, 0))
    one = pl.BlockSpec((1, LANES), lambda i: (0, 0))
    scale = hd ** -0.5

    def body(q_ref, kc_ref, kp_ref, vc_ref, vp_ref, sink_ref, do_ref, dq_ref, dk_ref, dv_ref, ds_ref):
        i = pl.program_id(0)

        @pl.when(i == 0)
        def _():
            dk_ref[...] = jnp.zeros_like(dk_ref)
            dv_ref[...] = jnp.zeros_like(dv_ref)
            ds_ref[...] = jnp.zeros_like(ds_ref)

        valid = _swa_valid(i)
        lane = lax.broadcasted_iota(jnp.int32, (1, LANES), 1)
        dsink_step = jnp.zeros((1, LANES), F32)
        rows_prev = pl.ds(pl.multiple_of(jnp.maximum(i - 1, 0) * BLOCK, BLOCK), BLOCK)
        rows_cur = pl.ds(pl.multiple_of(i * BLOCK, BLOCK), BLOCK)
        for j in range(wkv // hd):
            lanes = slice(j * hd, (j + 1) * hd)
            k_cat = jnp.concatenate([kp_ref[:, lanes], kc_ref[:, lanes]], axis=0)
            v_cat = jnp.concatenate([vp_ref[:, lanes], vc_ref[:, lanes]], axis=0)
            dk_cat = jnp.zeros((2 * BLOCK, hd), F32)
            dv_cat = jnp.zeros((2 * BLOCK, hd), F32)
            for hh in range(SWA_GROUP):
                h = j * SWA_GROUP + hh
                q_h = q_ref[:, h * hd:(h + 1) * hd]
                do_h = do_ref[:, h * hd:(h + 1) * hd]
                pn, p_sink = _swa_probs(q_h, k_cat, valid, sink_ref[h])
                dpn = lax.dot_general(do_h, v_cat, (((1,), (1,)), ((), ())), preferred_element_type=F32)
                delta = jnp.sum(dpn * pn, axis=-1, keepdims=True)
                ds = (pn * (dpn - delta) * scale).astype(BF16)
                dsink_h = -jnp.sum(p_sink * delta, axis=0, keepdims=True)
                dsink_step = dsink_step + jnp.where(lane == h, dsink_h, 0.0)
                dq_ref[:, h * hd:(h + 1) * hd] = jnp.dot(ds, k_cat, preferred_element_type=F32)
                dk_cat = dk_cat + lax.dot_general(ds, q_h, (((0,), (0,)), ((), ())),
                                                  preferred_element_type=F32)
                dv_cat = dv_cat + lax.dot_general(pn.astype(BF16), do_h, (((0,), (0,)), ((), ())),
                                                  preferred_element_type=F32)
            dk_ref[rows_prev, lanes] += dk_cat[:BLOCK]
            dk_ref[rows_cur, lanes] += dk_cat[BLOCK:]
            dv_ref[rows_prev, lanes] += dv_cat[:BLOCK]
            dv_ref[rows_cur, lanes] += dv_cat[BLOCK:]
        ds_ref[...] += dsink_step

    return pl.pallas_call(
        body, name=name, grid=(s_len // BLOCK,),
        in_specs=[q_spec, cur, prev, cur, prev, sink_spec, q_spec],
        out_specs=[q_spec, full, full, one],
        out_shape=[jax.ShapeDtypeStruct((s_len, wq), F32), jax.ShapeDtypeStruct((s_len, wkv), F32),
                   jax.ShapeDtypeStruct((s_len, wkv), F32), jax.ShapeDtypeStruct((1, LANES), F32)],
        compiler_params=_params("arbitrary"))(q, k, k, v, v, sinks, do)


def _log_sigmoid(x):
    return jnp.minimum(x, 0.0) - jnp.log(1.0 + jnp.exp(-jnp.abs(x)))


def _tri_ones(lower):
    row = lax.broadcasted_iota(jnp.int32, (BLOCK, BLOCK), 0)
    col = lax.broadcasted_iota(jnp.int32, (BLOCK, BLOCK), 1)
    return jnp.where((col <= row) if lower else (col >= row), 1.0, 0.0).astype(F32)


def _fox_decay(proj, bf_row, fl_block, name):
    s_len = proj.shape[0]
    nchunk = s_len // BLOCK

    def body(fl_ref, bf_ref, dec_ref):
        tri = _tri_ones(True)
        carry = jnp.zeros((1, LANES), F32)
        for c in range(nchunk):
            rows = slice(c * BLOCK, (c + 1) * BLOCK)
            log_f = _log_sigmoid(fl_ref[rows, :] + bf_ref[...])
            loc = jnp.dot(tri, log_f, preferred_element_type=F32, precision=lax.Precision.HIGHEST) + carry
            dec_ref[rows, :] = loc
            carry = loc[BLOCK - 1:BLOCK, :]

    return pl.pallas_call(
        body, name=name, grid=(1,),
        in_specs=[pl.BlockSpec((s_len, LANES), lambda i: (0, fl_block)),
                  pl.BlockSpec((1, LANES), lambda i: (0, 0))],
        out_specs=pl.BlockSpec((s_len, LANES), lambda i: (0, 0)),
        out_shape=jax.ShapeDtypeStruct((s_len, LANES), F32),
        compiler_params=_params("arbitrary"))(proj, bf_row)


def _fox_decay_bwd(ddq, ddk, proj, bf_row, fl_block, heads, name):
    s_len = proj.shape[0]
    nchunk = s_len // BLOCK

    def body(ddq_ref, ddk_ref, fl_ref, bf_ref, dfl_ref, dbf_ref):
        tri = _tri_ones(False)
        lane_ok = lax.broadcasted_iota(jnp.int32, (BLOCK, LANES), 1) < heads
        carry = jnp.zeros((1, LANES), F32)
        dbf = jnp.zeros((1, LANES), F32)
        for c in reversed(range(nchunk)):
            rows = slice(c * BLOCK, (c + 1) * BLOCK)
            ddec = ddq_ref[rows, :] + ddk_ref[rows, :]
            dlog = jnp.dot(tri, ddec, preferred_element_type=F32, precision=lax.Precision.HIGHEST) + carry
            carry = dlog[0:1, :]
            dfl = jnp.where(lane_ok, dlog * _sigmoid(-(fl_ref[rows, :] + bf_ref[...])), 0.0)
            dfl_ref[rows, :] = dfl.astype(BF16)
            dbf = dbf + jnp.sum(dfl, axis=0, keepdims=True)
        dbf_ref[...] = dbf

    blk = pl.BlockSpec((s_len, LANES), lambda i: (0, 0))
    one = pl.BlockSpec((1, LANES), lambda i: (0, 0))
    return pl.pallas_call(
        body, name=name, grid=(1,),
        in_specs=[blk, blk, pl.BlockSpec((s_len, LANES), lambda i: (0, fl_block)), one],
        out_specs=[blk, one],
        out_shape=[jax.ShapeDtypeStruct((s_len, LANES), BF16), jax.ShapeDtypeStruct((1, LANES), F32)],
        compiler_params=_params("arbitrary"))(ddq, ddk, proj, bf_row)


def _fox_scores(q, k, decq, deck, i, bq):
    s_len = k.shape[0]
    s = lax.dot_general(q, k, (((1,), (1,)), ((), ())), preferred_element_type=F32)
    s = s * (FOX_HEAD_DIM ** -0.5) + decq - deck
    row = lax.broadcasted_iota(jnp.int32, (bq, s_len), 0) + i * bq
    col = lax.broadcasted_iota(jnp.int32, (bq, s_len), 1)
    s = jnp.where(col <= row, s, NEG_INF)
    p = jnp.exp(s - jnp.max(s, axis=-1, keepdims=True))
    return p / jnp.sum(p, axis=-1, keepdims=True)


def _fox_specs(s_len, heads, bq):
    hd = FOX_HEAD_DIM
    q_spec = pl.BlockSpec((bq, hd), lambda h, i: (i, h))
    k_spec = pl.BlockSpec((s_len, hd), lambda h, i: (0, heads + h))
    v_spec = pl.BlockSpec((s_len, hd), lambda h, i: (0, 2 * heads + h))
    dq_spec = pl.BlockSpec((None, bq, 1), lambda h, i: (h, i, 0))
    dk_spec = pl.BlockSpec((None, 1, s_len), lambda h, i: (h, 0, 0))
    return q_spec, k_spec, v_spec, dq_spec, dk_spec


def _fox_fwd(proj, decq, deck, heads, name):
    s_len = proj.shape[0]
    bq = _pick(s_len, (256, 128))
    q_spec, k_spec, v_spec, dq_spec, dk_spec = _fox_specs(s_len, heads, bq)

    def body(q_ref, k_ref, v_ref, decq_ref, deck_ref, o_ref):
        pn = _fox_scores(q_ref[...].astype(BF16), k_ref[...].astype(BF16), decq_ref[...], deck_ref[...],
                         pl.program_id(1), bq)
        o_ref[...] = jnp.dot(pn.astype(BF16), v_ref[...].astype(BF16),
                             preferred_element_type=F32).astype(BF16)

    return pl.pallas_call(body, name=name, grid=(heads, s_len // bq),
                          in_specs=[q_spec, k_spec, v_spec, dq_spec, dk_spec], out_specs=q_spec,
                          out_shape=jax.ShapeDtypeStruct((s_len, heads * FOX_HEAD_DIM), BF16),
                          compiler_params=_params("parallel", "parallel"))(proj, proj, proj, decq, deck)


def _fox_bwd(proj, decq, deck, do, heads, name):
    s_len = proj.shape[0]
    d = heads * FOX_HEAD_DIM
    bq = _pick(s_len, (256, 128))
    q_spec, k_spec, v_spec, dq_spec, dk_spec = _fox_specs(s_len, heads, bq)
    acc_spec = pl.BlockSpec((s_len, FOX_HEAD_DIM), lambda h, i: (0, h))
    scale = FOX_HEAD_DIM ** -0.5

    def body(q_ref, k_ref, v_ref, decq_ref, deck_ref, do_ref, dq_ref, dk_ref, dv_ref, ddq_ref, ddk_ref):
        i = pl.program_id(1)

        @pl.when(i == 0)
        def _():
            dk_ref[...] = jnp.zeros_like(dk_ref)
            dv_ref[...] = jnp.zeros_like(dv_ref)
            ddk_ref[...] = jnp.zeros_like(ddk_ref)

        q = q_ref[...].astype(BF16)
        k = k_ref[...].astype(BF16)
        do_b = do_ref[...]
        pn = _fox_scores(q, k, decq_ref[...], deck_ref[...], i, bq)
        dpn = lax.dot_general(do_b, v_ref[...].astype(BF16), (((1,), (1,)), ((), ())),
                              preferred_element_type=F32)
        ds = pn * (dpn - jnp.sum(dpn * pn, axis=-1, keepdims=True))
        ddq_ref[...] = jnp.sum(ds, axis=-1, keepdims=True)
        ddk_ref[...] -= jnp.sum(ds, axis=0, keepdims=True)
        ds_b = (ds * scale).astype(BF16)
        dq_ref[...] = jnp.dot(ds_b, k, preferred_element_type=F32).astype(BF16)
        dk_ref[...] += lax.dot_general(ds_b, q, (((0,), (0,)), ((), ())), preferred_element_type=F32)
        dv_ref[...] += lax.dot_general(pn.astype(BF16), do_b, (((0,), (0,)), ((), ())),
                                       preferred_element_type=F32)

    return pl.pallas_call(
        body, name=name, grid=(heads, s_len // bq),
        in_specs=[q_spec, k_spec, v_spec, dq_spec, dk_spec, q_spec],
        out_specs=[q_spec, acc_spec, acc_spec, dq_spec, dk_spec],
        out_shape=[jax.ShapeDtypeStruct((s_len, d), BF16), jax.ShapeDtypeStruct((s_len, d), F32),
                   jax.ShapeDtypeStruct((s_len, d), F32), jax.ShapeDtypeStruct((heads, s_len, 1), F32),
                   jax.ShapeDtypeStruct((heads, 1, s_len), F32)],
        compiler_params=_params("parallel", "arbitrary"))(proj, proj, proj, decq, deck, do)


def _place():
    x, y, c = lax.axis_index("x"), lax.axis_index("y"), lax.axis_index("c")
    chips = [(1 - x, y), (x, 1 - y), (1 - x, 1 - y)]
    return x, y, c, chips


def _remote(src, dst, send_sems, recv_sems, idx, to):
    return pltpu.make_async_remote_copy(src_ref=src, dst_ref=dst, send_sem=send_sems.at[idx],
                                        recv_sem=recv_sems.at[idx], device_id=to, device_id_type=MESH)


def _row_chunks(rows, want):
    for k in (want, want // 2, want // 4):
        if k >= 1 and rows % (16 * k) == 0:
            return [(j * (rows // k), rows // k) for j in range(k)]
    return [(0, rows)]


D2D_CHUNKS = 8


def _cast_into_slot(w, me, name, dep=None):
    rows, width = w.shape
    br = _row_block(rows, width, budget=4 << 20)

    def body(me_ref, w_ref, *rest):
        rest[-1][...] = w_ref[...].astype(BF16)

    in_specs = [pl.BlockSpec((br, width), lambda i, me_ref: (i, 0))]
    if dep is not None:
        in_specs.append(pl.BlockSpec(dep.shape, lambda i, me_ref: (0, 0)))
    return pl.pallas_call(
        body, name=name,
        grid_spec=pltpu.PrefetchScalarGridSpec(
            num_scalar_prefetch=1, grid=(rows // br,), in_specs=in_specs,
            out_specs=pl.BlockSpec((None, br, width), lambda i, me_ref: (me_ref[0], i, 0))),
        out_shape=jax.ShapeDtypeStruct((N_CHIPS, rows, width), BF16),
        compiler_params=_params("parallel"))(me, w, *([dep] if dep is not None else []))


def _hbm(arr):
    return pltpu.with_memory_space_constraint(arr, pltpu.HBM)


def _token_shape():
    return jax.ShapeDtypeStruct((8, LANES), F32)


def _add_pair(grad, got, c, name):
    _, half, width = got.shape
    br = _row_block(half, width, itemsize=2, budget=3 << 20)
    nb = half // br

    def body(c_ref, a_ref, b_ref, o_ref):
        o_ref[...] = (a_ref[...].astype(F32) + b_ref[...].astype(F32)).astype(BF16)

    spec = pl.BlockSpec((None, br, width), lambda j, i, c_ref: (j, i, 0))
    mine = pl.BlockSpec((None, br, width), lambda j, i, c_ref: (j, c_ref[0] * nb + i, 0))
    return pl.pallas_call(
        body, name=name,
        grid_spec=pltpu.PrefetchScalarGridSpec(num_scalar_prefetch=1, grid=(N_CHIPS, nb),
                                               in_specs=[mine, spec], out_specs=spec),
        out_shape=jax.ShapeDtypeStruct(got.shape, BF16),
        compiler_params=_params("parallel", "parallel"))(c, grad, got)


def _sum_chips(pair, others, me, name):
    _, rows, width = pair.shape
    br = _row_block(rows, width, itemsize=4, budget=3 << 20)

    def body(me_ref, p_ref, o3_ref, o_ref):
        acc = p_ref[...].astype(F32)
        for r in range(N_CHIPS - 1):
            acc = acc + o3_ref[r].astype(F32)
        o_ref[...] = acc

    return pl.pallas_call(
        body, name=name,
        grid_spec=pltpu.PrefetchScalarGridSpec(
            num_scalar_prefetch=1, grid=(rows // br,),
            in_specs=[pl.BlockSpec((None, br, width), lambda i, me_ref: (me_ref[0], i, 0)),
                      pl.BlockSpec((N_CHIPS - 1, br, width), lambda i, me_ref: (0, i, 0))],
            out_specs=pl.BlockSpec((br, width), lambda i, me_ref: (i, 0))),
        out_shape=jax.ShapeDtypeStruct((rows, width), F32),
        compiler_params=_params("parallel"))(me, pair, others)


def _adamw_halves(w, mine, theirs, m, v, c, name):
    rows, width = w.shape
    half = rows // 2
    br = _row_block(half, width, budget=3 << 19)
    nb = half // br
    c1 = 1.0 - ADAM_B1 ** ADAM_STEP
    c2 = 1.0 - ADAM_B2 ** ADAM_STEP

    def body(c_ref, w_ref, a_ref, b_ref, m_ref, v_ref, g_ref, d_ref, nm_ref, nv_ref):
        grad = jnp.where(pl.program_id(0) == c_ref[0], a_ref[...], b_ref[...])
        new_m = ADAM_B1 * m_ref[...] + (1.0 - ADAM_B1) * grad
        new_v = ADAM_B2 * v_ref[...] + (1.0 - ADAM_B2) * (grad * grad)
        g_ref[...] = grad
        d_ref[...] = -ADAM_LR * ((new_m / c1) / (jnp.sqrt(new_v / c2) + ADAM_EPS) + ADAM_WD * w_ref[...])
        nm_ref[...] = new_m
        nv_ref[...] = new_v

    full = pl.BlockSpec((br, width), lambda h, i, c_ref: (h * nb + i, 0))
    mine_spec = pl.BlockSpec((br, width), lambda h, i, c_ref: (jnp.where(h == c_ref[0], i, 0), 0))
    theirs_spec = pl.BlockSpec((br, width), lambda h, i, c_ref: (jnp.where(h == c_ref[0], 0, i), 0))
    shp = jax.ShapeDtypeStruct(w.shape, F32)
    return pl.pallas_call(
        body, name=name,
        grid_spec=pltpu.PrefetchScalarGridSpec(num_scalar_prefetch=1, grid=(2, nb),
                                               in_specs=[full, mine_spec, theirs_spec, full, full],
                                               out_specs=[full] * 4),
        out_shape=[shp] * 4,
        compiler_params=_params("parallel", "parallel"))(c, w, mine, theirs, m, v)


class _Transfer:
    def __init__(self, n_sems, build):
        self.n_sems, self.build = n_sems, build


def _copies(src_of, dst_of, land_of, rows, chunks, send, recv, idx, to):
    starts = [_remote(src_of(s, z), dst_of(s, z), send, recv, idx, to) for s, z in _row_chunks(rows, chunks)]
    return starts, _remote(src_of(0, rows), land_of(0, rows), send, recv, idx, to)


def _gather_direct(keys, shapes):
    def build(refs, send, recv):
        x, y, c, chips = _place()
        me = 2 * x + y
        out = []
        for t, key in enumerate(keys):
            half = shapes[t][1] // 2
            for r, chip in enumerate(chips[:2]):
                slot = 2 * chip[0] + chip[1]
                out.append(_copies(lambda s, z, key=key, half=half: refs[key].at[me, pl.ds(c * half + s, z)],
                                   lambda s, z, key=key, half=half: refs[key].at[me, pl.ds(c * half + s, z)],
                                   lambda s, z, key=key, half=half, slot=slot: refs[key].at[slot, pl.ds(c * half + s, z)],
                                   half, 1, send, recv, 2 * t + r, (*chip, c)))
        return out
    return _Transfer(2 * len(keys), build)


def _gather_relay(keys, shapes):
    def build(refs, send, recv):
        x, y, c, chips = _place()
        slot_x, slot_y, slot_d = (2 * ch[0] + ch[1] for ch in chips)
        src_slot = slot_y + c * (slot_x - slot_y)
        to = (x ^ (1 - c), y ^ c, c)
        out = []
        for t, key in enumerate(keys):
            half = shapes[t][1] // 2
            out.append(_copies(lambda s, z, key=key, half=half: refs[key].at[src_slot, pl.ds(c * half + s, z)],
                               lambda s, z, key=key, half=half: refs[key].at[src_slot, pl.ds(c * half + s, z)],
                               lambda s, z, key=key, half=half: refs[key].at[slot_d, pl.ds(c * half + s, z)],
                               half, 1, send, recv, t, to))
        return out
    return _Transfer(len(keys), build)


def _gather_pair(keys, shapes):
    def build(refs, send, recv):
        x, y, c, chips = _place()
        out = []
        for t, key in enumerate(keys):
            half = shapes[t][1] // 2
            for r, chip in enumerate(chips):
                slot = 2 * chip[0] + chip[1]
                mine = lambda s, z, key=key, half=half, slot=slot: refs[key].at[slot, pl.ds(c * half + s, z)]
                land = lambda s, z, key=key, half=half, slot=slot: refs[key].at[slot, pl.ds((1 - c) * half + s, z)]
                out.append(_copies(mine, mine, land, half, D2D_CHUNKS, send, recv, 3 * t + r, (x, y, 1 - c)))
        return out
    return _Transfer(3 * len(keys), build)


def _grad_pair(keys, lands, shapes):
    def build(refs, send, recv):
        x, y, c, _ = _place()
        out = []
        for t, (key, land) in enumerate(zip(keys, lands)):
            half = shapes[t][1] // 2
            for j in range(N_CHIPS):
                out.append(_copies(
                    lambda s, z, key=key, half=half, j=j: refs[key].at[j, pl.ds((1 - c) * half + s, z)],
                    lambda s, z, land=land, j=j: refs[land].at[j, pl.ds(s, z)],
                    lambda s, z, land=land, j=j: refs[land].at[j, pl.ds(s, z)],
                    half, 2, send, recv, N_CHIPS * t + j, (x, y, 1 - c)))
        return out
    return _Transfer(N_CHIPS * len(keys), build)


def _grad_chips(keys, lands, shapes):
    def build(refs, send, recv):
        x, y, c, chips = _place()
        out = []
        for t, (key, land) in enumerate(zip(keys, lands)):
            rows = shapes[t][1]
            for r, chip in enumerate(chips):
                slot = 2 * chip[0] + chip[1]
                out.append(_copies(lambda s, z, key=key, slot=slot: refs[key].at[slot, pl.ds(s, z)],
                                   lambda s, z, land=land, r=r: refs[land].at[r, pl.ds(s, z)],
                                   lambda s, z, land=land, r=r: refs[land].at[r, pl.ds(s, z)],
                                   rows, 1, send, recv, 3 * t + r, (*chip, c)))
        return out
    return _Transfer(3 * len(keys), build)


def _grad_join(keys, lands, shapes):
    def build(refs, send, recv):
        x, y, c, _ = _place()
        out = []
        for t, (key, land) in enumerate(zip(keys, lands)):
            out.append(_copies(lambda s, z, key=key: refs[key].at[pl.ds(s, z)],
                               lambda s, z, land=land: refs[land].at[pl.ds(s, z)],
                               lambda s, z, land=land: refs[land].at[pl.ds(s, z)],
                               shapes[t][0], D2D_CHUNKS, send, recv, t, (x, y, 1 - c)))
        return out
    return _Transfer(len(keys), build)


def _comm_call(name, arrays, waits, starts, after):
    keys = list(arrays)
    n, nw, ns = len(keys), len(waits), len(starts)

    def body(*refs):
        in_sems = refs[n:n + 2 * nw]
        base = n + 2 * nw + 1
        out_sems = refs[base:base + 2 * ns]
        bufs = dict(zip(keys, refs[base + 2 * ns:base + 2 * ns + n]))
        token = refs[base + 2 * ns + n]
        for k, (transfer, _, _) in enumerate(waits):
            for _, whole in transfer.build(bufs, in_sems[2 * k], in_sems[2 * k + 1]):
                whole.wait_send()
                whole.wait_recv()
        for k, transfer in enumerate(starts):
            for chunks, _ in transfer.build(bufs, out_sems[2 * k], out_sems[2 * k + 1]):
                for cp in chunks:
                    cp.start()
        token[...] = jnp.zeros_like(token)

    sem_shapes = []
    for transfer in starts:
        sem_shapes += [pltpu.SemaphoreType.DMA((transfer.n_sems,))] * 2
    operands = [_hbm(arrays[k]) for k in keys]
    for _, send, recv in waits:
        operands += [send, recv]
    res = pl.pallas_call(
        body, name=name, in_specs=[HBM] * n + [SEM] * (2 * nw) + [pl.BlockSpec(memory_space=pl.ANY)],
        out_specs=[SEM] * (2 * ns) + [HBM] * n + [pl.BlockSpec(memory_space=pltpu.VMEM)],
        out_shape=sem_shapes + [pltpu.HBM(arrays[k].shape, arrays[k].dtype) for k in keys] + [_token_shape()],
        input_output_aliases={t: 2 * ns + t for t in range(n)},
        compiler_params=pltpu.CompilerParams(has_side_effects=EFFECT),
    )(*operands, after)
    sems = [(res[2 * k], res[2 * k + 1]) for k in range(ns)]
    return dict(zip(keys, res[2 * ns:2 * ns + n])), sems, res[2 * ns + n]


def _allreduce_small(part, name, dep=None):
    rows = part.shape[0]

    def body(p_ref, *rest):
        o_ref, all_ref, send, recv = rest[-4:]
        x, y, c, _ = _place()
        me = 4 * x + 2 * y + c
        all_ref[me] = p_ref[...]
        copies = []
        for r in range(1, N_DEV):
            to = (x ^ (r >> 2), y ^ ((r >> 1) & 1), c ^ (r & 1))
            cp = _remote(p_ref, all_ref.at[me], send, recv, r - 1, to)
            cp.start()
            copies.append(cp)
        for r in range(1, N_DEV):
            frm = (x ^ (r >> 2), y ^ ((r >> 1) & 1), c ^ (r & 1))
            slot = all_ref.at[4 * frm[0] + 2 * frm[1] + frm[2]]
            _remote(slot, slot, send, recv, r - 1, frm).wait_recv()
        for cp in copies:
            cp.wait_send()
        acc = all_ref[0]
        for dev in range(1, N_DEV):
            acc = acc + all_ref[dev]
        o_ref[...] = acc

    vmem = pl.BlockSpec(memory_space=pltpu.VMEM)
    operands = [part] + ([dep] if dep is not None else [])
    return pl.pallas_call(
        body, name=name, in_specs=[vmem] * len(operands), out_specs=vmem,
        out_shape=jax.ShapeDtypeStruct(part.shape, F32),
        scratch_shapes=[pltpu.VMEM((N_DEV, rows, LANES), F32), pltpu.SemaphoreType.DMA((N_DEV - 1,)),
                        pltpu.SemaphoreType.DMA((N_DEV - 1,))],
        compiler_params=pltpu.CompilerParams(vmem_limit_bytes=VMEM_LIMIT),
    )(*operands)


INPUT_NAMES = None


def _weight_names():
    names = []
    for i, kind in enumerate(("gmlp", "swa", "fox", "gmlp")):
        p = f"l{i}_"
        names += [p + "ffn1_norm", p + "ffn1_wi", p + "ffn1_wo", p + "mix_norm", p + "mix_win"]
        if kind == "gmlp":
            names += [p + "gmlp_vnorm", p + "gmlp_ws", p + "gmlp_bs"]
        elif kind == "swa":
            names += [p + "swa_sinks"]
        else:
            names += [p + "fox_bf"]
        names += [p + "mix_wout", p + "ffn2_norm", p + "ffn2_wi", p + "ffn2_wo"]
    return names + ["final_norm"]


WEIGHTS = _weight_names()
MIXERS = ("gmlp", "swa", "fox", "gmlp")
BIG = ("ffn1_wi", "ffn1_wo", "mix_win", "mix_wout", "ffn2_wi", "ffn2_wo")


def _ffn_fwd(h, gain, wi, wo, tag, dep=None):
    n = _rms_fwd(h, gain, tag + "_norm", dep=dep)
    z = _matmul(n, wi, name=tag + "_up", out_dtype=BF16)
    a = _swiglu_fwd(z, tag + "_act")
    f, d = wo.shape[0] * wo.shape[1], wo.shape[2]
    out = _matmul(a, wo.reshape(f, d), name=tag + "_down", out_dtype=F32, scale=0.5, resid=h)
    return out, (h, n, z, a)


def _ffn_bwd(dout, saved, gain, wi, wo, tag, dep=None):
    h, n, z, a = saved
    f, d = wo.shape[0] * wo.shape[1], wo.shape[2]
    da = _matmul(dout, wo.reshape(f, d), tb=True, name=tag + "_bdown", out_dtype=BF16, scale=0.5, dep=dep)
    dwo = _matmul(a, dout, ta=True, name=tag + "_gdown", out_dtype=BF16, scale=0.5, dep=dep)
    dz = _swiglu_bwd(z, da, tag + "_bact")
    dn = _matmul(dz, wi, tb=True, name=tag + "_bup", out_dtype=F32)
    dwi = _matmul(n, dz, ta=True, name=tag + "_gup", out_dtype=BF16, out_shards=N_CHIPS)
    dh, dgain = _norm_bwd(h, gain, dn, dout, tag + "_bnorm")
    return dh, dgain, dwi, dwo.reshape(wo.shape)


def _natural(w_sharded, pad_to):
    ns, rows, csh = w_sharded.shape
    nat = jnp.transpose(w_sharded, (1, 0, 2)).reshape(rows, ns * csh)
    extra = (-nat.shape[1]) % pad_to
    return jnp.pad(nat, ((0, 0), (0, extra))) if extra else nat


def _mixer_fwd(kind, h, p, tag, dep=None):
    s_len, d = h.shape
    n = _rms_fwd(h, p["mix_norm"], tag + "_norm", dep=dep)
    wout = p["mix_wout"].reshape(d, d)
    if kind == "gmlp":
        zp = _matmul(n, p["mix_win"], name=tag + "_in", out_dtype=BF16)
        y = _gmlp_fwd(zp, p["gmlp_vnorm"], p["gmlp_ws"], p["gmlp_bs"], tag + "_gate")
        saved = (h, n, zp, y)
    elif kind == "swa":
        qkv = _matmul(n, p["mix_win"], name=tag + "_in", out_dtype=F32)
        q, k, v = _rope_fwd(qkv, tag + "_rope")
        y = _swa_fwd(q, k, v, p["swa_sinks"], tag + "_attn")
        saved = (h, n, q, k, v, y)
    else:
        heads = d // FOX_HEAD_DIM
        win = _natural(p["mix_win"], LANES)
        proj = _matmul(n, win, name=tag + "_in", out_dtype=F32)
        bf_row = jnp.pad(p["fox_bf"], (0, LANES - heads)).reshape(1, LANES)
        dec = _fox_decay(proj, bf_row, 3 * heads, tag + "_decay")
        dec_t = dec[:, :heads].T
        decq, deck = dec_t.reshape(heads, s_len, 1), dec_t.reshape(heads, 1, s_len)
        y = _fox_fwd(proj, decq, deck, heads, tag + "_attn")
        saved = (h, n, win, proj, bf_row, decq, deck, y)
    out = _matmul(y, wout, name=tag + "_out", out_dtype=F32, resid=h)
    return out, saved


def _mixer_bwd(kind, dout, saved, p, tag, dep=None):
    h, n = saved[0], saved[1]
    y = saved[-1]
    s_len, d = h.shape
    wout = p["mix_wout"].reshape(d, d)
    grads = {}
    dy = _matmul(dout, wout, tb=True, name=tag + "_bout", out_dtype=BF16, dep=dep)
    grads["mix_wout"] = _matmul(y, dout, ta=True, name=tag + "_gout", out_dtype=BF16,
                                dep=dep).reshape(p["mix_wout"].shape)
    if kind == "gmlp":
        zp = saved[2]
        dzp, dws, dbst, dvg = _gmlp_bwd(zp, dy, p["gmlp_vnorm"], p["gmlp_ws"], p["gmlp_bs"], tag + "_bgate")
        grads.update(gmlp_ws=dws, gmlp_bs=dbst.T, gmlp_vnorm=dvg.reshape(d))
        dn = _matmul(dzp, p["mix_win"], tb=True, name=tag + "_bin", out_dtype=F32)
        grads["mix_win"] = _matmul(n, dzp, ta=True, name=tag + "_gin", out_dtype=BF16, out_shards=N_CHIPS)
    elif kind == "swa":
        q, k, v = saved[2:5]
        dq, dk, dv, dsinks = _swa_bwd(q, k, v, p["swa_sinks"], dy, tag + "_battn")
        grads["swa_sinks"] = dsinks[0, :p["swa_sinks"].shape[0]]
        dqkv = _rope_bwd(dq, dk, dv, tag + "_brope")
        dn = _matmul(dqkv, p["mix_win"], tb=True, name=tag + "_bin", out_dtype=F32)
        grads["mix_win"] = _matmul(n, dqkv, ta=True, name=tag + "_gin", out_dtype=BF16, out_shards=N_CHIPS)
    else:
        win, proj, bf_row, decq, deck = saved[2:7]
        heads = d // FOX_HEAD_DIM
        dq, dk, dv, ddq, ddk = _fox_bwd(proj, decq, deck, dy, heads, tag + "_battn")
        widen = lambda t: jnp.pad(t.reshape(heads, s_len).T, ((0, 0), (0, LANES - heads)))
        dfl, dbf = _fox_decay_bwd(widen(ddq), widen(ddk), proj, bf_row, 3 * heads, heads, tag + "_bdecay")
        grads["fox_bf"] = dbf[0, :heads]
        dproj = jnp.concatenate([dq, dk.astype(BF16), dv.astype(BF16), dfl], axis=1)
        dn = _matmul(dproj, win, tb=True, name=tag + "_bin", out_dtype=F32)
        dwin = _matmul(n, dproj, ta=True, name=tag + "_gin", out_dtype=BF16)
        ns, rows, csh = p["mix_win"].shape
        grads["mix_win"] = jnp.transpose(dwin[:, :ns * csh].reshape(rows, ns, csh), (1, 0, 2))
    dh, dgain = _norm_bwd(h, p["mix_norm"], dn, dout, tag + "_bnorm")
    grads["mix_norm"] = dgain.reshape(d)
    return dh, grads


def _pack_small(arrays):
    flat = jnp.concatenate([a.reshape(-1).astype(F32) for a in arrays])
    pad = (-flat.shape[0]) % (512 * LANES)
    return jnp.pad(flat, (0, pad)).reshape(-1, LANES)


def _unpack_small(packed, like):
    flat, out, pos = packed.reshape(-1), [], 0
    for a in like:
        out.append(flat[pos:pos + a.size].reshape(a.shape))
        pos += a.size
    return out


def _step(inp):
    x, target = inp["x"][0], inp["loss_target"][0]
    d = x.shape[1]

    core = lax.axis_index("c").astype(jnp.int32).reshape(1)
    chip = (2 * lax.axis_index("x") + lax.axis_index("y")).astype(jnp.int32).reshape(1)

    groups = []
    for i in range(len(MIXERS)):
        groups += [(i, "ffn1", [f"l{i}_ffn1_wi", f"l{i}_ffn1_wo"]), (i, "mix", [f"l{i}_mix_win", f"l{i}_mix_wout"]),
                   (i, "ffn2", [f"l{i}_ffn2_wi", f"l{i}_ffn2_wo"])]

    def layer_params(i, full):
        p = {nm[len(f"l{i}_"):]: inp[nm] for nm in WEIGHTS if nm.startswith(f"l{i}_")}
        p.update({nm[len(f"l{i}_"):]: w for nm, w in full.items()})
        return p

    n_groups = len(groups)
    valid = lambda k: 0 <= k < n_groups

    bufs = {}
    full_shapes = lambda names: [(N_CHIPS, *inp[nm].shape) for nm in names]
    direct = [_gather_direct(names, full_shapes(names)) for _, _, names in groups]
    relay = [_gather_relay(names, full_shapes(names)) for _, _, names in groups]
    to_pair = [_gather_pair(names, full_shapes(names)) for _, _, names in groups]
    sems = {}

    def cast_groups(which, dep):
        for g in which:
            for nm in groups[g][2]:
                bufs[nm] = _cast_into_slot(inp[nm], chip, nm + "_cast", dep=dep)

    def gather_step(step, after):
        waits, starts, tags, keys = [], [], [], []
        for kind, transfers, g, begin in (("pair", to_pair, step, False), ("relay", relay, step + 1, False),
                                          ("pair", to_pair, step + 1, True), ("direct", direct, step + 2, False),
                                          ("relay", relay, step + 2, True), ("direct", direct, step + 3, True)):
            if not valid(g):
                continue
            keys += [nm for nm in groups[g][2] if nm not in keys]
            if begin:
                starts.append(transfers[g])
                tags.append((kind, g))
            else:
                waits.append((transfers[g], *sems.pop((kind, g))))
        new, started, token = _comm_call(f"gather_step{step + 3}", {k: bufs[k] for k in keys}, waits, starts, after)
        bufs.update(new)
        sems.update(zip(tags, started))
        return token

    cast_groups([0], None)
    token = gather_step(-3, x)
    cast_groups(range(1, 6), token)
    token = gather_step(-2, bufs[groups[5][2][-1]])
    cast_groups(range(6, n_groups), token)
    token = gather_step(-1, bufs[groups[-1][2][-1]])
    h, saved, fulls = x, [], []
    for g, (i, part, names) in enumerate(groups):
        token = gather_step(g, h)
        full = {nm: bufs[nm] for nm in names}
        p = layer_params(i, full)
        if part == "mix":
            h, s = _mixer_fwd(MIXERS[i], h, p, f"l{i}_mix", dep=token)
        else:
            h, s = _ffn_fwd(h, p[part + "_norm"], p[part + "_wi"], p[part + "_wo"], f"l{i}_{part}", dep=token)
        saved.append(s)
        fulls.append(full)
    loss_part, dh, dfinal = _loss_head(h, inp["final_norm"], target, "loss_head")
    loss = lax.psum(loss_part, ("x", "y", "c"))

    small_grads = {"final_norm": dfinal.reshape(d)}
    outs = {}
    work = {}
    stage = {}
    small_names = [nm for nm in WEIGHTS if nm.split("_", 1)[1] not in BIG]
    like = [inp[nm] for nm in small_names]

    def comm(name, transfers_to_wait, transfers_to_start, after):
        waits = [(stage[k], *sems.pop(k)) for k in transfers_to_wait if valid(k[1])]
        starts = [k for k in transfers_to_start if valid(k[1])]
        if not waits and not starts:
            return after
        keys = []
        for k in [k for k in transfers_to_wait if valid(k[1])] + starts:
            keys += [key for key in stage_keys[k] if key not in keys]
        new, started, token = _comm_call(name, {k: work[k] for k in keys}, waits, [stage[k] for k in starts], after)
        work.update(new)
        sems.update(zip(starts, started))
        return token

    stage_keys = {}

    def reduce_step(g, after):
        token = comm(f"rs_pair_step{n_groups - 1 - g}", [("pair", g + 1)], [("pair", g)], after)
        if valid(g + 1):
            names = groups[g + 1][2]
            for nm in names:
                work[nm + "#sum"] = _add_pair(work[nm + "#grad"], work[nm + "#got"], core, nm + "_rs_add")
                work[nm + "#others"] = lax.empty((N_CHIPS - 1, *work[nm + "#sum"].shape[1:]), BF16)
            shapes = [work[nm + "#sum"].shape for nm in names]
            stage[("chips", g + 1)] = _grad_chips([nm + "#sum" for nm in names], [nm + "#others" for nm in names], shapes)
            stage_keys[("chips", g + 1)] = [nm + sfx for nm in names for sfx in ("#sum", "#others")]
        token = comm(f"rs_chips_step{n_groups - 1 - g}", [("chips", g + 2)], [("chips", g + 1)], token)
        if g == -1:
            total = _allreduce_small(_pack_small([small_grads[nm] for nm in small_names]), "small_allreduce", dep=token)
            upd = _adamw(_pack_small(like), total, _pack_small([inp["m_" + nm] for nm in small_names]),
                         _pack_small([inp["v_" + nm] for nm in small_names]), "small_adamw")
            unpacked = [_unpack_small(t, like) for t in (total, *upd)]
            for k, nm in enumerate(small_names):
                outs[nm] = tuple(u[k] for u in unpacked)
            token = upd[0]
        if valid(g + 2):
            names = groups[g + 2][2]
            for nm in names:
                work[nm + "#half"] = _sum_chips(work[nm + "#sum"], work[nm + "#others"], chip, nm + "_rs_sum")
                work[nm + "#theirs"] = lax.empty(work[nm + "#half"].shape, F32)
            shapes = [work[nm + "#half"].shape for nm in names]
            stage[("join", g + 2)] = _grad_join([nm + "#half" for nm in names], [nm + "#theirs" for nm in names], shapes)
            stage_keys[("join", g + 2)] = [nm + sfx for nm in names for sfx in ("#half", "#theirs")]
        token = comm(f"rs_join_step{n_groups - 1 - g}", [("join", g + 3)], [("join", g + 2)], token)
        if valid(g + 3):
            for nm in groups[g + 3][2]:
                outs[nm] = tuple(_adamw_halves(inp[nm], work[nm + "#half"], work[nm + "#theirs"], inp["m_" + nm],
                                               inp["v_" + nm], core, nm + "_adamw"))
        return token

    dep = None
    for g in reversed(range(n_groups)):
        i, part, names = groups[g]
        p = layer_params(i, fulls[g])
        if part == "mix":
            dh, mg = _mixer_bwd(MIXERS[i], dh, saved[g], p, f"l{i}_mix", dep=dep)
            grads = [mg.pop("mix_win"), mg.pop("mix_wout")]
            small_grads.update({f"l{i}_{key}": val for key, val in mg.items()})
        else:
            dh, g_norm, dwi, dwo = _ffn_bwd(dh, saved[g], p[part + "_norm"], p[part + "_wi"], p[part + "_wo"],
                                            f"l{i}_{part}", dep=dep)
            small_grads[f"l{i}_{part}_norm"] = g_norm.reshape(d)
            grads = [dwi, dwo]
        for nm, gr in zip(names, grads):
            work[nm + "#grad"] = gr
            work[nm + "#got"] = lax.empty((gr.shape[0], gr.shape[1] // 2, gr.shape[2]), BF16)
        stage[("pair", g)] = _grad_pair([nm + "#grad" for nm in names], [nm + "#got" for nm in names],
                                        [gr.shape for gr in grads])
        stage_keys[("pair", g)] = [nm + sfx for nm in names for sfx in ("#grad", "#got")]
        dep = reduce_step(g, dh)
    for g in (-1, -2, -3):
        dep = reduce_step(g, dep)

    result = [loss, dh[None]]
    for part in range(4):
        result += [outs[nm][part] for nm in WEIGHTS]
    return tuple(result)


def kernel(x, l0_ffn1_norm, l0_ffn1_wi, l0_ffn1_wo, l0_mix_norm, l0_mix_win, l0_gmlp_vnorm, l0_gmlp_ws, l0_gmlp_bs, l0_mix_wout, l0_ffn2_norm, l0_ffn2_wi, l0_ffn2_wo, l1_ffn1_norm, l1_ffn1_wi, l1_ffn1_wo, l1_mix_norm, l1_mix_win, l1_swa_sinks, l1_mix_wout, l1_ffn2_norm, l1_ffn2_wi, l1_ffn2_wo, l2_ffn1_norm, l2_ffn1_wi, l2_ffn1_wo, l2_mix_norm, l2_mix_win, l2_fox_bf, l2_mix_wout, l2_ffn2_norm, l2_ffn2_wi, l2_ffn2_wo, l3_ffn1_norm, l3_ffn1_wi, l3_ffn1_wo, l3_mix_norm, l3_mix_win, l3_gmlp_vnorm, l3_gmlp_ws, l3_gmlp_bs, l3_mix_wout, l3_ffn2_norm, l3_ffn2_wi, l3_ffn2_wo, final_norm, loss_target, m_l0_ffn1_norm, m_l0_ffn1_wi, m_l0_ffn1_wo, m_l0_mix_norm, m_l0_mix_win, m_l0_gmlp_vnorm, m_l0_gmlp_ws, m_l0_gmlp_bs, m_l0_mix_wout, m_l0_ffn2_norm, m_l0_ffn2_wi, m_l0_ffn2_wo, m_l1_ffn1_norm, m_l1_ffn1_wi, m_l1_ffn1_wo, m_l1_mix_norm, m_l1_mix_win, m_l1_swa_sinks, m_l1_mix_wout, m_l1_ffn2_norm, m_l1_ffn2_wi, m_l1_ffn2_wo, m_l2_ffn1_norm, m_l2_ffn1_wi, m_l2_ffn1_wo, m_l2_mix_norm, m_l2_mix_win, m_l2_fox_bf, m_l2_mix_wout, m_l2_ffn2_norm, m_l2_ffn2_wi, m_l2_ffn2_wo, m_l3_ffn1_norm, m_l3_ffn1_wi, m_l3_ffn1_wo, m_l3_mix_norm, m_l3_mix_win, m_l3_gmlp_vnorm, m_l3_gmlp_ws, m_l3_gmlp_bs, m_l3_mix_wout, m_l3_ffn2_norm, m_l3_ffn2_wi, m_l3_ffn2_wo, m_final_norm, v_l0_ffn1_norm, v_l0_ffn1_wi, v_l0_ffn1_wo, v_l0_mix_norm, v_l0_mix_win, v_l0_gmlp_vnorm, v_l0_gmlp_ws, v_l0_gmlp_bs, v_l0_mix_wout, v_l0_ffn2_norm, v_l0_ffn2_wi, v_l0_ffn2_wo, v_l1_ffn1_norm, v_l1_ffn1_wi, v_l1_ffn1_wo, v_l1_mix_norm, v_l1_mix_win, v_l1_swa_sinks, v_l1_mix_wout, v_l1_ffn2_norm, v_l1_ffn2_wi, v_l1_ffn2_wo, v_l2_ffn1_norm, v_l2_ffn1_wi, v_l2_ffn1_wo, v_l2_mix_norm, v_l2_mix_win, v_l2_fox_bf, v_l2_mix_wout, v_l2_ffn2_norm, v_l2_ffn2_wi, v_l2_ffn2_wo, v_l3_ffn1_norm, v_l3_ffn1_wi, v_l3_ffn1_wo, v_l3_mix_norm, v_l3_mix_win, v_l3_gmlp_vnorm, v_l3_gmlp_ws, v_l3_gmlp_bs, v_l3_mix_wout, v_l3_ffn2_norm, v_l3_ffn2_wi, v_l3_ffn2_wo, v_final_norm):
    return _step(dict(locals()))
```

```python
import functools
import math

import jax
import jax.numpy as jnp
from jax import lax
from jax.experimental import pallas as pl
from jax.experimental.pallas import tpu as pltpu

F32 = jnp.float32
BF16 = jnp.bfloat16

NORM_EPS = 1e-5
NEG_INF = -1e30
BLOCK = 128
GMLP_GROUPS = 16
SWA_HEAD_DIM = 64
SWA_GROUP = 8
ROPE_DIM = SWA_HEAD_DIM // 4
ROPE_THETA = 500000.0
FOX_HEAD_DIM = 128
ADAM_LR = 0.001
ADAM_B1 = 0.9
ADAM_B2 = 0.999
ADAM_EPS = 1e-08
ADAM_WD = 0.01
ADAM_STEP = 10
N_CHIPS = 4
N_DEV = 8
LANES = 128
VMEM_LIMIT = 56 * 1024 * 1024
MESH = pl.DeviceIdType.MESH
HBM = pl.BlockSpec(memory_space=pltpu.HBM)
SEM = pl.BlockSpec(memory_space=pltpu.SEMAPHORE)
EFFECT = pltpu.SideEffectType.DATAFLOW_SIDE_EFFECTING

MM_TILES = (1024, 1408, 896, 640, 512, 384, 256, 128)
K_TILES = (2816,) + MM_TILES


def _pick(n, prefs):
    for p in prefs:
        if p <= n and n % p == 0:
            return p
    return n


def _params(*sem):
    return pltpu.CompilerParams(dimension_semantics=sem or None, vmem_limit_bytes=VMEM_LIMIT)


def _cols(arr):
    return arr.shape[-1] * (arr.shape[0] if arr.ndim == 3 else 1)


def _mat_spec(arr, rb, cb, ridx, cidx):
    if arr.ndim == 2:
        return pl.BlockSpec((rb, cb), lambda j, i, k: (ridx(j, i, k), cidx(j, i, k)))
    per = arr.shape[2] // cb
    return pl.BlockSpec((None, rb, cb),
                        lambda j, i, k: (cidx(j, i, k) // per, ridx(j, i, k), cidx(j, i, k) % per))


def _matmul(a, b, *, name, out_dtype, ta=False, tb=False, out_shards=1, scale=1.0, resid=None, dep=None):
    m_dim, k_dim = (a.shape[1], a.shape[0]) if ta else a.shape
    n_dim = b.shape[-2] if tb else _cols(b)
    assert k_dim == (_cols(b) if tb else b.shape[-2]), (a.shape, b.shape, ta, tb)
    n_unit = n_dim // out_shards
    if b.ndim == 3 and not tb:
        n_unit = math.gcd(n_unit, b.shape[2])
    k_unit = b.shape[2] if (b.ndim == 3 and tb) else k_dim
    bm = _pick(m_dim, MM_TILES)
    bn = _pick(n_unit, MM_TILES)
    bk = k_unit if k_unit <= 2048 else _pick(k_unit, K_TILES)
    nk = k_dim // bk
    i_of, j_of, k_of = (lambda j, i, k: i), (lambda j, i, k: j), (lambda j, i, k: k)
    a_spec = _mat_spec(a, bk, bm, k_of, i_of) if ta else _mat_spec(a, bm, bk, i_of, k_of)
    b_spec = _mat_spec(b, bn, bk, j_of, k_of) if tb else _mat_spec(b, bk, bn, k_of, j_of)
    out_shape = (m_dim, n_dim) if out_shards == 1 else (out_shards, m_dim, n_dim // out_shards)
    out = jax.ShapeDtypeStruct(out_shape, out_dtype)
    o_spec = _mat_spec(out, bm, bn, i_of, j_of)
    dims = (((0 if ta else 1,), (1 if tb else 0,)), ((), ()))
    operands, in_specs = [a, b], [a_spec, b_spec]
    if resid is not None:
        operands.append(resid)
        in_specs.append(_mat_spec(resid, bm, bn, i_of, j_of))
    if dep is not None:
        operands.append(dep)
        in_specs.append(pl.BlockSpec(dep.shape, lambda j, i, k: (0, 0)))
    n_in = len(operands)

    def body(*refs):
        a_ref, b_ref = refs[0], refs[1]
        r_ref = refs[2] if resid is not None else None
        o_ref = refs[n_in]
        part = lax.dot_general(a_ref[...].astype(BF16), b_ref[...].astype(BF16), dims,
                               preferred_element_type=F32)

        def finish(acc):
            val = acc * scale if scale != 1.0 else acc
            if r_ref is not None:
                val = r_ref[...] + val
            o_ref[...] = val.astype(o_ref.dtype)

        if nk == 1:
            finish(part)
        else:
            acc_ref = refs[-1]
            k = pl.program_id(2)

            @pl.when(k == 0)
            def _():
                acc_ref[...] = part

            @pl.when(k > 0)
            def _():
                acc_ref[...] += part

            @pl.when(k == nk - 1)
            def _():
                finish(acc_ref[...])

    return pl.pallas_call(
        body, name=name, grid=(n_dim // bn, m_dim // bm, nk),
        in_specs=in_specs, out_specs=o_spec, out_shape=out,
        scratch_shapes=[pltpu.VMEM((bm, bn), F32)] if nk > 1 else [],
        compiler_params=_params("parallel", "parallel", "arbitrary"),
    )(*operands)


def _row_block(rows, width, itemsize=4, budget=2 << 20):
    best = None
    for br in range(16, rows + 1, 16):
        if rows % br == 0 and br * width * itemsize <= budget:
            best = br
    return best or rows


def _rms_fwd(h, g, name, dep=None):
    s_len, d = h.shape
    br = _row_block(s_len, d)

    def body(h_ref, g_ref, *rest):
        o_ref = rest[-1]
        x = h_ref[...]
        r = lax.rsqrt(jnp.mean(x * x, axis=-1, keepdims=True) + NORM_EPS)
        o_ref[...] = (x * r * g_ref[...]).astype(BF16)

    spec = pl.BlockSpec((br, d), lambda i: (i, 0))
    operands = [h, g.reshape(1, d)] + ([dep] if dep is not None else [])
    in_specs = [spec, pl.BlockSpec((1, d), lambda i: (0, 0))]
    if dep is not None:
        in_specs.append(pl.BlockSpec(dep.shape, lambda i: (0, 0)))
    return pl.pallas_call(body, name=name, grid=(s_len // br,), in_specs=in_specs, out_specs=spec,
                          out_shape=jax.ShapeDtypeStruct((s_len, d), BF16),
                          compiler_params=_params("parallel"))(*operands)


def _rms_bwd_rows(x, g, dn):
    r = lax.rsqrt(jnp.mean(x * x, axis=-1, keepdims=True) + NORM_EPS)
    xhat = x * r
    gdn = dn * g
    dx = r * (gdn - xhat * jnp.mean(gdn * xhat, axis=-1, keepdims=True))
    return dx, dn * xhat


def _norm_bwd(h, g, dn, dres, name):
    s_len, d = h.shape
    br = _row_block(s_len, d, budget=1 << 20)

    def body(h_ref, g_ref, dn_ref, dres_ref, dh_ref, dg_ref):
        dx, dg_rows = _rms_bwd_rows(h_ref[...], g_ref[...], dn_ref[...].astype(F32))
        dh_ref[...] = dres_ref[...] + dx

        @pl.when(pl.program_id(0) == 0)
        def _():
            dg_ref[...] = jnp.zeros_like(dg_ref)

        dg_ref[...] += jnp.sum(dg_rows, axis=0, keepdims=True)

    spec = pl.BlockSpec((br, d), lambda i: (i, 0))
    vec = pl.BlockSpec((1, d), lambda i: (0, 0))
    return pl.pallas_call(body, name=name, grid=(s_len // br,),
                          in_specs=[spec, vec, spec, spec], out_specs=[spec, vec],
                          out_shape=[jax.ShapeDtypeStruct((s_len, d), F32),
                                     jax.ShapeDtypeStruct((1, d), F32)],
                          compiler_params=_params("arbitrary"))(h, g.reshape(1, d), dn, dres)


def _sigmoid(x):
    return 0.5 * (1.0 + jnp.tanh(0.5 * x))


def _swiglu_fwd(z, name):
    s_len, f2 = z.shape
    f = f2 // 2
    br = _row_block(s_len, f2, budget=3 << 20)

    def body(z_ref, a_ref):
        gate = z_ref[:, :f].astype(F32)
        up = z_ref[:, f:].astype(F32)
        a_ref[...] = (gate * _sigmoid(gate) * up).astype(BF16)

    return pl.pallas_call(body, name=name, grid=(s_len // br,),
                          in_specs=[pl.BlockSpec((br, f2), lambda i: (i, 0))],
                          out_specs=pl.BlockSpec((br, f), lambda i: (i, 0)),
                          out_shape=jax.ShapeDtypeStruct((s_len, f), BF16),
                          compiler_params=_params("parallel"))(z)


def _swiglu_bwd(z, da, name):
    s_len, f2 = z.shape
    f = f2 // 2
    br = _row_block(s_len, f2, budget=3 << 20)

    def body(z_ref, da_ref, dz_ref):
        gate = z_ref[:, :f].astype(F32)
        up = z_ref[:, f:].astype(F32)
        d = da_ref[...].astype(F32)
        sig = _sigmoid(gate)
        dz_ref[:, :f] = (d * up * (sig * (1.0 + gate * (1.0 - sig)))).astype(BF16)
        dz_ref[:, f:] = (d * gate * sig).astype(BF16)

    return pl.pallas_call(body, name=name, grid=(s_len // br,),
                          in_specs=[pl.BlockSpec((br, f2), lambda i: (i, 0)),
                                    pl.BlockSpec((br, f), lambda i: (i, 0))],
                          out_specs=pl.BlockSpec((br, f2), lambda i: (i, 0)),
                          out_shape=jax.ShapeDtypeStruct((s_len, f2), BF16),
                          compiler_params=_params("parallel"))(z, da)


def _loss_head(h, g, target, name):
    s_len, d = h.shape
    br = _row_block(s_len, d, budget=1 << 20)

    def body(h_ref, g_ref, t_ref, loss_ref, dh_ref, dg_ref):
        x = h_ref[...]
        gain = g_ref[...]
        r = lax.rsqrt(jnp.mean(x * x, axis=-1, keepdims=True) + NORM_EPS)
        err = x * r * gain - t_ref[...]
        part = 0.5 * jnp.sum(jnp.mean(err * err, axis=-1, keepdims=True), axis=0, keepdims=True)
        dx, dg_rows = _rms_bwd_rows(x, gain, err * (1.0 / d))
        dh_ref[...] = dx

        @pl.when(pl.program_id(0) == 0)
        def _():
            dg_ref[...] = jnp.zeros_like(dg_ref)
            loss_ref[...] = jnp.zeros_like(loss_ref)

        dg_ref[...] += jnp.sum(dg_rows, axis=0, keepdims=True)
        loss_ref[...] += jnp.broadcast_to(part, loss_ref.shape)

    spec = pl.BlockSpec((br, d), lambda i: (i, 0))
    vec = pl.BlockSpec((1, d), lambda i: (0, 0))
    one = pl.BlockSpec((1, LANES), lambda i: (0, 0))
    loss, dh, dg = pl.pallas_call(
        body, name=name, grid=(s_len // br,), in_specs=[spec, vec, spec],
        out_specs=[one, spec, vec],
        out_shape=[jax.ShapeDtypeStruct((1, LANES), F32), jax.ShapeDtypeStruct((s_len, d), F32),
                   jax.ShapeDtypeStruct((1, d), F32)],
        compiler_params=_params("arbitrary"))(h, g.reshape(1, d), target)
    return loss[0, 0], dh, dg


def _adamw(w, g, m, v, name):
    rows, width = w.shape
    br = _row_block(rows, width, budget=1 << 20)
    c1 = 1.0 - ADAM_B1 ** ADAM_STEP
    c2 = 1.0 - ADAM_B2 ** ADAM_STEP

    def body(w_ref, g_ref, m_ref, v_ref, d_ref, nm_ref, nv_ref):
        grad = g_ref[...]
        new_m = ADAM_B1 * m_ref[...] + (1.0 - ADAM_B1) * grad
        new_v = ADAM_B2 * v_ref[...] + (1.0 - ADAM_B2) * (grad * grad)
        d_ref[...] = -ADAM_LR * ((new_m / c1) / (jnp.sqrt(new_v / c2) + ADAM_EPS) + ADAM_WD * w_ref[...])
        nm_ref[...] = new_m
        nv_ref[...] = new_v

    spec = pl.BlockSpec((br, width), lambda i: (i, 0))
    shp = jax.ShapeDtypeStruct(w.shape, F32)
    return pl.pallas_call(body, name=name, grid=(rows // br,), in_specs=[spec] * 4,
                          out_specs=[spec] * 3, out_shape=[shp] * 3,
                          compiler_params=_params("parallel"))(w, g, m, v)


def _gelu(x):
    return 0.5 * x * (1.0 + lax.erf(x * (2.0 ** -0.5)))


def _gelu_grad(x):
    return 0.5 * (1.0 + lax.erf(x * (2.0 ** -0.5))) + x * jnp.exp(-0.5 * x * x) * ((2.0 * math.pi) ** -0.5)


def _tril_mask():
    row = lax.broadcasted_iota(jnp.int32, (BLOCK, BLOCK), 0)
    col = lax.broadcasted_iota(jnp.int32, (BLOCK, BLOCK), 1)
    return col <= row


def _gmlp_specs(s_len, d):
    gw = d // GMLP_GROUPS
    zp = pl.BlockSpec((BLOCK, 2 * d), lambda i: (i, 0))
    row = pl.BlockSpec((BLOCK, d), lambda i: (i, 0))
    vec = pl.BlockSpec((1, d), lambda i: (0, 0))
    ws = pl.BlockSpec((GMLP_GROUPS, BLOCK, BLOCK), lambda i: (0, 0, 0))
    bst = pl.BlockSpec((BLOCK, GMLP_GROUPS), lambda i: (0, 0))
    return gw, zp, row, vec, ws, bst


def _gmlp_fwd(zp, vgain, ws, bs, name):
    s_len, d2 = zp.shape
    d = d2 // 2
    gw, zp_spec, row_spec, vec_spec, ws_spec, bst_spec = _gmlp_specs(s_len, d)

    def body(zp_ref, vg_ref, ws_ref, bst_ref, y_ref):
        u = _gelu(zp_ref[:, :d].astype(F32))
        vv = _gelu(zp_ref[:, d:].astype(F32))
        r = lax.rsqrt(jnp.mean(vv * vv, axis=-1, keepdims=True) + NORM_EPS)
        vn = (vv * r * vg_ref[...]).astype(BF16)
        mask = _tril_mask()
        for g in range(GMLP_GROUPS):
            cols = slice(g * gw, (g + 1) * gw)
            wg = jnp.where(mask, ws_ref[g], 0.0).astype(BF16)
            mixed = jnp.dot(wg, vn[:, cols], preferred_element_type=F32) + bst_ref[:, g:g + 1]
            y_ref[:, cols] = (u[:, cols] * mixed).astype(BF16)

    return pl.pallas_call(body, name=name, grid=(s_len // BLOCK,),
                          in_specs=[zp_spec, vec_spec, ws_spec, bst_spec], out_specs=row_spec,
                          out_shape=jax.ShapeDtypeStruct((s_len, d), BF16),
                          compiler_params=_params("parallel"))(zp, vgain.reshape(1, d), ws, bs.T)


def _gmlp_bwd(zp, dy, vgain, ws, bs, name):
    s_len, d2 = zp.shape
    d = d2 // 2
    gw, zp_spec, row_spec, vec_spec, ws_spec, bst_spec = _gmlp_specs(s_len, d)

    def body(zp_ref, dy_ref, vg_ref, ws_ref, bst_ref, dzp_ref, dws_ref, dbst_ref, dvg_ref, dvn_ref):
        @pl.when(pl.program_id(0) == 0)
        def _():
            dws_ref[...] = jnp.zeros_like(dws_ref)
            dbst_ref[...] = jnp.zeros_like(dbst_ref)
            dvg_ref[...] = jnp.zeros_like(dvg_ref)

        zu = zp_ref[:, :d].astype(F32)
        zv = zp_ref[:, d:].astype(F32)
        u = _gelu(zu)
        vv = _gelu(zv)
        r = lax.rsqrt(jnp.mean(vv * vv, axis=-1, keepdims=True) + NORM_EPS)
        vhat = vv * r
        gain = vg_ref[...]
        vn = (vhat * gain).astype(BF16)
        dyf = dy_ref[...].astype(F32)
        dmixed = dyf * u
        dmixed_b = dmixed.astype(BF16)
        mask = _tril_mask()
        lane = lax.broadcasted_iota(jnp.int32, (BLOCK, GMLP_GROUPS), 1)
        dbs_step = jnp.zeros((BLOCK, GMLP_GROUPS), F32)
        for g in range(GMLP_GROUPS):
            cols = slice(g * gw, (g + 1) * gw)
            wg = jnp.where(mask, ws_ref[g], 0.0).astype(BF16)
            mixed = jnp.dot(wg, vn[:, cols], preferred_element_type=F32) + bst_ref[:, g:g + 1]
            dzp_ref[:, cols] = (dyf[:, cols] * mixed * _gelu_grad(zu[:, cols])).astype(BF16)
            dm = dmixed_b[:, cols]
            dw = lax.dot_general(dm, vn[:, cols], (((1,), (1,)), ((), ())), preferred_element_type=F32)
            dws_ref[g] += jnp.where(mask, dw, 0.0)
            dbs_step = dbs_step + jnp.where(lane == g, jnp.sum(dmixed[:, cols], axis=-1, keepdims=True), 0.0)
            dvn_ref[:, cols] = lax.dot_general(wg, dm, (((0,), (0,)), ((), ())), preferred_element_type=F32)
        dbst_ref[...] += dbs_step
        dvn = dvn_ref[...]
        dvg_ref[...] += jnp.sum(dvn * vhat, axis=0, keepdims=True)
        dvhat = dvn * gain
        dvv = r * (dvhat - vhat * jnp.mean(dvhat * vhat, axis=-1, keepdims=True))
        dzp_ref[:, d:] = (dvv * _gelu_grad(zv)).astype(BF16)

    return pl.pallas_call(
        body, name=name, grid=(s_len // BLOCK,),
        in_specs=[zp_spec, row_spec, vec_spec, ws_spec, bst_spec],
        out_specs=[zp_spec, ws_spec, bst_spec, vec_spec],
        out_shape=[jax.ShapeDtypeStruct((s_len, d2), BF16), jax.ShapeDtypeStruct(ws.shape, F32),
                   jax.ShapeDtypeStruct((BLOCK, GMLP_GROUPS), F32), jax.ShapeDtypeStruct((1, d), F32)],
        scratch_shapes=[pltpu.VMEM((BLOCK, d), F32)],
        compiler_params=_params("arbitrary"))(zp, dy, vgain.reshape(1, d), ws, bs.T)


def _rope_tables(s_len, sign):
    half = ROPE_DIM // 2
    inv_freq = ROPE_THETA ** (-(jnp.arange(half, dtype=F32) * 2.0 / ROPE_DIM))
    ang = jnp.arange(s_len, dtype=F32)[:, None] * inv_freq[None, :]
    cos, sin = jnp.cos(ang), jnp.sin(ang) * sign
    pad = jnp.zeros((s_len, SWA_HEAD_DIM - ROPE_DIM), F32)
    zero = jnp.zeros_like(sin)
    cos_t = jnp.concatenate([cos, cos, pad + 1.0], axis=1)
    sin_up = jnp.concatenate([-sin, zero, pad], axis=1)
    sin_dn = jnp.concatenate([zero, sin, pad], axis=1)
    return [jnp.tile(t, (1, LANES // SWA_HEAD_DIM)) for t in (cos_t, sin_up, sin_dn)]


def _rotate(x, cos_t, sin_up, sin_dn):
    width = x.shape[-1]
    half = ROPE_DIM // 2
    reps = width // cos_t.shape[-1]
    if reps > 1:
        cos_t, sin_up, sin_dn = (jnp.tile(t, (1, reps)) for t in (cos_t, sin_up, sin_dn))
    elif reps == 0:
        cos_t, sin_up, sin_dn = (t[:, :width] for t in (cos_t, sin_up, sin_dn))
    return x * cos_t + pltpu.roll(x, width - half, 1) * sin_up + pltpu.roll(x, half, 1) * sin_dn


def _rope_fwd(qkv, name):
    s_len, total = qkv.shape
    wkv = total // (SWA_GROUP + 2)
    wq = SWA_GROUP * wkv
    br = _row_block(s_len, total, budget=2 << 20)
    tables = _rope_tables(s_len, 1.0)

    def body(q_ref, k_ref, v_ref, c_ref, su_ref, sd_ref, qo_ref, ko_ref, vo_ref):
        t = (c_ref[...], su_ref[...], sd_ref[...])
        qo_ref[...] = _rotate(q_ref[...], *t).astype(BF16)
        ko_ref[...] = _rotate(k_ref[...], *t).astype(BF16)
        vo_ref[...] = v_ref[...].astype(BF16)

    qs = pl.BlockSpec((br, wq), lambda i: (i, 0))
    ks = pl.BlockSpec((br, wkv), lambda i: (i, SWA_GROUP))
    vs = pl.BlockSpec((br, wkv), lambda i: (i, SWA_GROUP + 1))
    ts = pl.BlockSpec((br, LANES), lambda i: (i, 0))
    kv_out = pl.BlockSpec((br, wkv), lambda i: (i, 0))
    return pl.pallas_call(
        body, name=name, grid=(s_len // br,), in_specs=[qs, ks, vs, ts, ts, ts],
        out_specs=[qs, kv_out, kv_out],
        out_shape=[jax.ShapeDtypeStruct((s_len, wq), BF16), jax.ShapeDtypeStruct((s_len, wkv), BF16),
                   jax.ShapeDtypeStruct((s_len, wkv), BF16)],
        compiler_params=_params("parallel"))(qkv, qkv, qkv, *tables)


def _rope_bwd(dq, dk, dv, name):
    s_len, wq = dq.shape
    wkv = dk.shape[1]
    br = _row_block(s_len, wq + 2 * wkv, budget=2 << 20)
    tables = _rope_tables(s_len, -1.0)

    def body(q_ref, k_ref, v_ref, c_ref, su_ref, sd_ref, o_ref):
        t = (c_ref[...], su_ref[...], sd_ref[...])
        o_ref[:, :wq] = _rotate(q_ref[...], *t).astype(BF16)
        o_ref[:, wq:wq + wkv] = _rotate(k_ref[...], *t).astype(BF16)
        o_ref[:, wq + wkv:] = v_ref[...].astype(BF16)

    qs = pl.BlockSpec((br, wq), lambda i: (i, 0))
    kvs = pl.BlockSpec((br, wkv), lambda i: (i, 0))
    ts = pl.BlockSpec((br, LANES), lambda i: (i, 0))
    return pl.pallas_call(
        body, name=name, grid=(s_len // br,), in_specs=[qs, kvs, kvs, ts, ts, ts],
        out_specs=pl.BlockSpec((br, wq + 2 * wkv), lambda i: (i, 0)),
        out_shape=jax.ShapeDtypeStruct((s_len, wq + 2 * wkv), BF16),
        compiler_params=_params("parallel"))(dq, dk, dv, *tables)


def _swa_valid(i):
    row = lax.broadcasted_iota(jnp.int32, (BLOCK, 2 * BLOCK), 0)
    col = lax.broadcasted_iota(jnp.int32, (BLOCK, 2 * BLOCK), 1)
    return (col - BLOCK <= row) & (row < col) & ((col >= BLOCK) | (i > 0))


def _swa_specs(wq, wkv):
    q_spec = pl.BlockSpec((BLOCK, wq), lambda i: (i, 0))
    cur = pl.BlockSpec((BLOCK, wkv), lambda i: (i, 0))
    prev = pl.BlockSpec((BLOCK, wkv), lambda i: (jnp.maximum(i - 1, 0), 0))
    sink = pl.BlockSpec(memory_space=pltpu.SMEM)
    return q_spec, cur, prev, sink


def _swa_probs(q_h, k_cat, valid, sink):
    s = lax.dot_general(q_h, k_cat, (((1,), (1,)), ((), ())), preferred_element_type=F32)
    s = jnp.where(valid, s * (SWA_HEAD_DIM ** -0.5), NEG_INF)
    m = jnp.maximum(jnp.max(s, axis=-1, keepdims=True), sink)
    p = jnp.exp(s - m)
    e_sink = jnp.exp(sink - m)
    denom = jnp.sum(p, axis=-1, keepdims=True) + e_sink
    return p / denom, e_sink / denom


def _swa_fwd(q, k, v, sinks, name):
    s_len, wq = q.shape
    wkv = k.shape[1]
    hd = SWA_HEAD_DIM
    q_spec, cur, prev, sink_spec = _swa_specs(wq, wkv)

    def body(q_ref, kc_ref, kp_ref, vc_ref, vp_ref, sink_ref, o_ref):
        valid = _swa_valid(pl.program_id(0))
        for j in range(wkv // hd):
            lanes = slice(j * hd, (j + 1) * hd)
            k_cat = jnp.concatenate([kp_ref[:, lanes], kc_ref[:, lanes]], axis=0)
            v_cat = jnp.concatenate([vp_ref[:, lanes], vc_ref[:, lanes]], axis=0)
            for hh in range(SWA_GROUP):
                h = j * SWA_GROUP + hh
                pn, _ = _swa_probs(q_ref[:, h * hd:(h + 1) * hd], k_cat, valid, sink_ref[h])
                o_ref[:, h * hd:(h + 1) * hd] = jnp.dot(
                    pn.astype(BF16), v_cat, preferred_element_type=F32).astype(BF16)

    return pl.pallas_call(body, name=name, grid=(s_len // BLOCK,),
                          in_specs=[q_spec, cur, prev, cur, prev, sink_spec], out_specs=q_spec,
                          out_shape=jax.ShapeDtypeStruct((s_len, wq), BF16),
                          compiler_params=_params("parallel"))(q, k, k, v, v, sinks)


def _swa_bwd(q, k, v, sinks, do, name):
    s_len, wq = q.shape
    wkv = k.shape[1]
    hd = SWA_HEAD_DIM
    q_spec, cur, prev, sink_spec = _swa_specs(wq, wkv)
    full = pl.BlockSpec((s_len, wkv), lambda i: (0, 0))
    one = pl.BlockSpec((1, LANES), lambda i: (0, 0))
    scale = hd ** -0.5

    def body(q_ref, kc_ref, kp_ref, vc_ref, vp_ref, sink_ref, do_ref, dq_ref, dk_ref, dv_ref, ds_ref):
        i = pl.program_id(0)

        @pl.when(i == 0)
        def _():
            dk_ref[...] = jnp.zeros_like(dk_ref)
            dv_ref[...] = jnp.zeros_like(dv_ref)
            ds_ref[...] = jnp.zeros_like(ds_ref)

        valid = _swa_valid(i)
        lane = lax.broadcasted_iota(jnp.int32, (1, LANES), 1)
        dsink_step = jnp.zeros((1, LANES), F32)
        rows_prev = pl.ds(pl.multiple_of(jnp.maximum(i - 1, 0) * BLOCK, BLOCK), BLOCK)
        rows_cur = pl.ds(pl.multiple_of(i * BLOCK, BLOCK), BLOCK)
        for j in range(wkv // hd):
            lanes = slice(j * hd, (j + 1) * hd)
            k_cat = jnp.concatenate([kp_ref[:, lanes], kc_ref[:, lanes]], axis=0)
            v_cat = jnp.concatenate([vp_ref[:, lanes], vc_ref[:, lanes]], axis=0)
            dk_cat = jnp.zeros((2 * BLOCK, hd), F32)
            dv_cat = jnp.zeros((2 * BLOCK, hd), F32)
            for hh in range(SWA_GROUP):
                h = j * SWA_GROUP + hh
                q_h = q_ref[:, h * hd:(h + 1) * hd]
                do_h = do_ref[:, h * hd:(h + 1) * hd]
                pn, p_sink = _swa_probs(q_h, k_cat, valid, sink_ref[h])
                dpn = lax.dot_general(do_h, v_cat, (((1,), (1,)), ((), ())), preferred_element_type=F32)
                delta = jnp.sum(dpn * pn, axis=-1, keepdims=True)
                ds = (pn * (dpn - delta) * scale).astype(BF16)
                dsink_h = -jnp.sum(p_sink * delta, axis=0, keepdims=True)
                dsink_step = dsink_step + jnp.where(lane == h, dsink_h, 0.0)
                dq_ref[:, h * hd:(h + 1) * hd] = jnp.dot(ds, k_cat, preferred_element_type=F32)
                dk_cat = dk_cat + lax.dot_general(ds, q_h, (((0,), (0,)), ((), ())),
                                                  preferred_element_type=F32)
                dv_cat = dv_cat + lax.dot_general(pn.astype(BF16), do_h, (((0,), (0,)), ((), ())),
                                                  preferred_element_type=F32)
            dk_ref[rows_prev, lanes] += dk_cat[:BLOCK]
            dk_ref[rows_cur, lanes] += dk_cat[BLOCK:]
            dv_ref[rows_prev, lanes] += dv_cat[:BLOCK]
            dv_ref[rows_cur, lanes] += dv_cat[BLOCK:]
        ds_ref[...] += dsink_step

    return pl.pallas_call(
        body, name=name, grid=(s_len // BLOCK,),
        in_specs=[q_spec, cur, prev, cur, prev, sink_spec, q_spec],
        out_specs=[q_spec, full, full, one],
        out_shape=[jax.ShapeDtypeStruct((s_len, wq), F32), jax.ShapeDtypeStruct((s_len, wkv), F32),
                   jax.ShapeDtypeStruct((s_len, wkv), F32), jax.ShapeDtypeStruct((1, LANES), F32)],
        compiler_params=_params("arbitrary"))(q, k, k, v, v, sinks, do)


def _log_sigmoid(x):
    return jnp.minimum(x, 0.0) - jnp.log(1.0 + jnp.exp(-jnp.abs(x)))


def _tri_ones(lower):
    row = lax.broadcasted_iota(jnp.int32, (BLOCK, BLOCK), 0)
    col = lax.broadcasted_iota(jnp.int32, (BLOCK, BLOCK), 1)
    return jnp.where((col <= row) if lower else (col >= row), 1.0, 0.0).astype(F32)


def _fox_decay(proj, bf_row, fl_block, name):
    s_len = proj.shape[0]
    nchunk = s_len // BLOCK

    def body(fl_ref, bf_ref, dec_ref):
        tri = _tri_ones(True)
        carry = jnp.zeros((1, LANES), F32)
        for c in range(nchunk):
            rows = slice(c * BLOCK, (c + 1) * BLOCK)
            log_f = _log_sigmoid(fl_ref[rows, :] + bf_ref[...])
            loc = jnp.dot(tri, log_f, preferred_element_type=F32, precision=lax.Precision.HIGHEST) + carry
            dec_ref[rows, :] = loc
            carry = loc[BLOCK - 1:BLOCK, :]

    return pl.pallas_call(
        body, name=name, grid=(1,),
        in_specs=[pl.BlockSpec((s_len, LANES), lambda i: (0, fl_block)),
                  pl.BlockSpec((1, LANES), lambda i: (0, 0))],
        out_specs=pl.BlockSpec((s_len, LANES), lambda i: (0, 0)),
        out_shape=jax.ShapeDtypeStruct((s_len, LANES), F32),
        compiler_params=_params("arbitrary"))(proj, bf_row)


def _fox_decay_bwd(ddq, ddk, proj, bf_row, fl_block, heads, name):
    s_len = proj.shape[0]
    nchunk = s_len // BLOCK

    def body(ddq_ref, ddk_ref, fl_ref, bf_ref, dfl_ref, dbf_ref):
        tri = _tri_ones(False)
        lane_ok = lax.broadcasted_iota(jnp.int32, (BLOCK, LANES), 1) < heads
        carry = jnp.zeros((1, LANES), F32)
        dbf = jnp.zeros((1, LANES), F32)
        for c in reversed(range(nchunk)):
            rows = slice(c * BLOCK, (c + 1) * BLOCK)
            ddec = ddq_ref[rows, :] + ddk_ref[rows, :]
            dlog = jnp.dot(tri, ddec, preferred_element_type=F32, precision=lax.Precision.HIGHEST) + carry
            carry = dlog[0:1, :]
            dfl = jnp.where(lane_ok, dlog * _sigmoid(-(fl_ref[rows, :] + bf_ref[...])), 0.0)
            dfl_ref[rows, :] = dfl.astype(BF16)
            dbf = dbf + jnp.sum(dfl, axis=0, keepdims=True)
        dbf_ref[...] = dbf

    blk = pl.BlockSpec((s_len, LANES), lambda i: (0, 0))
    one = pl.BlockSpec((1, LANES), lambda i: (0, 0))
    return pl.pallas_call(
        body, name=name, grid=(1,),
        in_specs=[blk, blk, pl.BlockSpec((s_len, LANES), lambda i: (0, fl_block)), one],
        out_specs=[blk, one],
        out_shape=[jax.ShapeDtypeStruct((s_len, LANES), BF16), jax.ShapeDtypeStruct((1, LANES), F32)],
        compiler_params=_params("arbitrary"))(ddq, ddk, proj, bf_row)


def _fox_scores(q, k, decq, deck, i, bq):
    s_len = k.shape[0]
    s = lax.dot_general(q, k, (((1,), (1,)), ((), ())), preferred_element_type=F32)
    s = s * (FOX_HEAD_DIM ** -0.5) + decq - deck
    row = lax.broadcasted_iota(jnp.int32, (bq, s_len), 0) + i * bq
    col = lax.broadcasted_iota(jnp.int32, (bq, s_len), 1)
    s = jnp.where(col <= row, s, NEG_INF)
    p = jnp.exp(s - jnp.max(s, axis=-1, keepdims=True))
    return p / jnp.sum(p, axis=-1, keepdims=True)


def _fox_key_spans(s_len, bq):
    n_span = min(4, s_len // bq)
    return [(j + 1) * (s_len // n_span) for j in range(n_span)]


def _fox_span_of(i, s_len, bq):
    span = s_len // min(4, s_len // bq)
    return ((i * bq) // span + 1) * span


def _fox_specs(s_len, heads, bq):
    hd = FOX_HEAD_DIM
    q_spec = pl.BlockSpec((bq, hd), lambda h, i: (i, h))
    k_spec = pl.BlockSpec((s_len, hd), lambda h, i: (0, heads + h))
    v_spec = pl.BlockSpec((s_len, hd), lambda h, i: (0, 2 * heads + h))
    dq_spec = pl.BlockSpec((None, bq, 1), lambda h, i: (h, i, 0))
    dk_spec = pl.BlockSpec((None, 1, s_len), lambda h, i: (h, 0, 0))
    return q_spec, k_spec, v_spec, dq_spec, dk_spec


def _fox_fwd(proj, decq, deck, heads, name):
    s_len = proj.shape[0]
    bq = _pick(s_len, (256, 128))
    q_spec, k_spec, v_spec, dq_spec, dk_spec = _fox_specs(s_len, heads, bq)

    def body(q_ref, k_ref, v_ref, decq_ref, deck_ref, o_ref):
        i = pl.program_id(1)
        for klen in _fox_key_spans(s_len, bq):
            @pl.when(_fox_span_of(i, s_len, bq) == klen)
            def _(klen=klen):
                pn = _fox_scores(q_ref[...].astype(BF16), k_ref[:klen, :].astype(BF16), decq_ref[...],
                                 deck_ref[:, :klen], i, bq)
                o_ref[...] = jnp.dot(pn.astype(BF16), v_ref[:klen, :].astype(BF16),
                                     preferred_element_type=F32).astype(BF16)

    return pl.pallas_call(body, name=name, grid=(heads, s_len // bq),
                          in_specs=[q_spec, k_spec, v_spec, dq_spec, dk_spec], out_specs=q_spec,
                          out_shape=jax.ShapeDtypeStruct((s_len, heads * FOX_HEAD_DIM), BF16),
                          compiler_params=_params("parallel", "parallel"))(proj, proj, proj, decq, deck)


def _fox_bwd(proj, decq, deck, do, heads, name):
    s_len = proj.shape[0]
    d = heads * FOX_HEAD_DIM
    bq = _pick(s_len, (256, 128))
    q_spec, k_spec, v_spec, dq_spec, dk_spec = _fox_specs(s_len, heads, bq)
    acc_spec = pl.BlockSpec((s_len, FOX_HEAD_DIM), lambda h, i: (0, h))
    scale = FOX_HEAD_DIM ** -0.5

    def body(q_ref, k_ref, v_ref, decq_ref, deck_ref, do_ref, dq_ref, dk_ref, dv_ref, ddq_ref, ddk_ref):
        i = pl.program_id(1)

        @pl.when(i == 0)
        def _():
            dk_ref[...] = jnp.zeros_like(dk_ref)
            dv_ref[...] = jnp.zeros_like(dv_ref)
            ddk_ref[...] = jnp.zeros_like(ddk_ref)

        q = q_ref[...].astype(BF16)
        do_b = do_ref[...]
        for klen in _fox_key_spans(s_len, bq):
            @pl.when(_fox_span_of(i, s_len, bq) == klen)
            def _(klen=klen):
                k = k_ref[:klen, :].astype(BF16)
                pn = _fox_scores(q, k, decq_ref[...], deck_ref[:, :klen], i, bq)
                dpn = lax.dot_general(do_b, v_ref[:klen, :].astype(BF16), (((1,), (1,)), ((), ())),
                                      preferred_element_type=F32)
                ds = pn * (dpn - jnp.sum(dpn * pn, axis=-1, keepdims=True))
                ddq_ref[...] = jnp.sum(ds, axis=-1, keepdims=True)
                ddk_ref[:, :klen] -= jnp.sum(ds, axis=0, keepdims=True)
                ds_b = (ds * scale).astype(BF16)
                dq_ref[...] = jnp.dot(ds_b, k, preferred_element_type=F32).astype(BF16)
                dk_ref[:klen, :] += lax.dot_general(ds_b, q, (((0,), (0,)), ((), ())),
                                                    preferred_element_type=F32)
                dv_ref[:klen, :] += lax.dot_general(pn.astype(BF16), do_b, (((0,), (0,)), ((), ())),
                                                    preferred_element_type=F32)

    return pl.pallas_call(
        body, name=name, grid=(heads, s_len // bq),
        in_specs=[q_spec, k_spec, v_spec, dq_spec, dk_spec, q_spec],
        out_specs=[q_spec, acc_spec, acc_spec, dq_spec, dk_spec],
        out_shape=[jax.ShapeDtypeStruct((s_len, d), BF16), jax.ShapeDtypeStruct((s_len, d), F32),
                   jax.ShapeDtypeStruct((s_len, d), F32), jax.ShapeDtypeStruct((heads, s_len, 1), F32),
                   jax.ShapeDtypeStruct((heads, 1, s_len), F32)],
        compiler_params=_params("parallel", "arbitrary"))(proj, proj, proj, decq, deck, do)


def _place():
    x, y, c = lax.axis_index("x"), lax.axis_index("y"), lax.axis_index("c")
    chips = [(1 - x, y), (x, 1 - y), (1 - x, 1 - y)]
    return x, y, c, chips


def _remote(src, dst, send_sems, recv_sems, idx, to):
    return pltpu.make_async_remote_copy(src_ref=src, dst_ref=dst, send_sem=send_sems.at[idx],
                                        recv_sem=recv_sems.at[idx], device_id=to, device_id_type=MESH)


def _row_chunks(rows, want):
    for k in (want, want // 2, want // 4):
        if k >= 1 and rows % (16 * k) == 0:
            return [(j * (rows // k), rows // k) for j in range(k)]
    return [(0, rows)]


D2D_CHUNKS = 8


def _cast_into_slot(w, me, name, dep=None):
    rows, width = w.shape
    br = _row_block(rows, width, budget=4 << 20)

    def body(me_ref, w_ref, *rest):
        rest[-1][...] = w_ref[...].astype(BF16)

    in_specs = [pl.BlockSpec((br, width), lambda i, me_ref: (i, 0))]
    if dep is not None:
        in_specs.append(pl.BlockSpec(dep.shape, lambda i, me_ref: (0, 0)))
    return pl.pallas_call(
        body, name=name,
        grid_spec=pltpu.PrefetchScalarGridSpec(
            num_scalar_prefetch=1, grid=(rows // br,), in_specs=in_specs,
            out_specs=pl.BlockSpec((None, br, width), lambda i, me_ref: (me_ref[0], i, 0))),
        out_shape=jax.ShapeDtypeStruct((N_CHIPS, rows, width), BF16),
        compiler_params=_params("parallel"))(me, w, *([dep] if dep is not None else []))


def _hbm(arr):
    return pltpu.with_memory_space_constraint(arr, pltpu.HBM)


def _token_shape():
    return jax.ShapeDtypeStruct((8, LANES), F32)


def _add_pair(grad, got, c, name):
    _, half, width = got.shape
    br = _row_block(half, width, itemsize=2, budget=3 << 20)
    nb = half // br

    def body(c_ref, a_ref, b_ref, o_ref):
        o_ref[...] = (a_ref[...].astype(F32) + b_ref[...].astype(F32)).astype(BF16)

    spec = pl.BlockSpec((None, br, width), lambda j, i, c_ref: (j, i, 0))
    mine = pl.BlockSpec((None, br, width), lambda j, i, c_ref: (j, c_ref[0] * nb + i, 0))
    return pl.pallas_call(
        body, name=name,
        grid_spec=pltpu.PrefetchScalarGridSpec(num_scalar_prefetch=1, grid=(N_CHIPS, nb),
                                               in_specs=[mine, spec], out_specs=spec),
        out_shape=jax.ShapeDtypeStruct(got.shape, BF16),
        compiler_params=_params("parallel", "parallel"))(c, grad, got)


def _sum_chips(pair, others, me, name):
    _, rows, width = pair.shape
    br = _row_block(rows, width, itemsize=4, budget=3 << 20)

    def body(me_ref, p_ref, o3_ref, o_ref):
        acc = p_ref[...].astype(F32)
        for r in range(N_CHIPS - 1):
            acc = acc + o3_ref[r].astype(F32)
        o_ref[...] = acc

    return pl.pallas_call(
        body, name=name,
        grid_spec=pltpu.PrefetchScalarGridSpec(
            num_scalar_prefetch=1, grid=(rows // br,),
            in_specs=[pl.BlockSpec((None, br, width), lambda i, me_ref: (me_ref[0], i, 0)),
                      pl.BlockSpec((N_CHIPS - 1, br, width), lambda i, me_ref: (0, i, 0))],
            out_specs=pl.BlockSpec((br, width), lambda i, me_ref: (i, 0))),
        out_shape=jax.ShapeDtypeStruct((rows, width), F32),
        compiler_params=_params("parallel"))(me, pair, others)


def _adamw_halves(w, mine, theirs, m, v, c, name):
    rows, width = w.shape
    half = rows // 2
    br = _row_block(half, width, budget=3 << 19)
    nb = half // br
    c1 = 1.0 - ADAM_B1 ** ADAM_STEP
    c2 = 1.0 - ADAM_B2 ** ADAM_STEP

    def body(c_ref, w_ref, a_ref, b_ref, m_ref, v_ref, g_ref, d_ref, nm_ref, nv_ref):
        grad = jnp.where(pl.program_id(0) == c_ref[0], a_ref[...], b_ref[...])
        new_m = ADAM_B1 * m_ref[...] + (1.0 - ADAM_B1) * grad
        new_v = ADAM_B2 * v_ref[...] + (1.0 - ADAM_B2) * (grad * grad)
        g_ref[...] = grad
        d_ref[...] = -ADAM_LR * ((new_m / c1) / (jnp.sqrt(new_v / c2) + ADAM_EPS) + ADAM_WD * w_ref[...])
        nm_ref[...] = new_m
        nv_ref[...] = new_v

    full = pl.BlockSpec((br, width), lambda h, i, c_ref: (h * nb + i, 0))
    mine_spec = pl.BlockSpec((br, width), lambda h, i, c_ref: (jnp.where(h == c_ref[0], i, 0), 0))
    theirs_spec = pl.BlockSpec((br, width), lambda h, i, c_ref: (jnp.where(h == c_ref[0], 0, i), 0))
    shp = jax.ShapeDtypeStruct(w.shape, F32)
    return pl.pallas_call(
        body, name=name,
        grid_spec=pltpu.PrefetchScalarGridSpec(num_scalar_prefetch=1, grid=(2, nb),
                                               in_specs=[full, mine_spec, theirs_spec, full, full],
                                               out_specs=[full] * 4),
        out_shape=[shp] * 4,
        compiler_params=_params("parallel", "parallel"))(c, w, mine, theirs, m, v)


class _Transfer:
    def __init__(self, n_sems, build):
        self.n_sems, self.build = n_sems, build


def _copies(src_of, dst_of, land_of, rows, chunks, send, recv, idx, to):
    starts = [_remote(src_of(s, z), dst_of(s, z), send, recv, idx, to) for s, z in _row_chunks(rows, chunks)]
    return starts, _remote(src_of(0, rows), land_of(0, rows), send, recv, idx, to)


def _gather_direct(keys, shapes):
    def build(refs, send, recv):
        x, y, c, chips = _place()
        me = 2 * x + y
        out = []
        for t, key in enumerate(keys):
            half = shapes[t][1] // 2
            for r, chip in enumerate(chips[:2]):
                slot = 2 * chip[0] + chip[1]
                out.append(_copies(lambda s, z, key=key, half=half: refs[key].at[me, pl.ds(c * half + s, z)],
                                   lambda s, z, key=key, half=half: refs[key].at[me, pl.ds(c * half + s, z)],
                                   lambda s, z, key=key, half=half, slot=slot: refs[key].at[slot, pl.ds(c * half + s, z)],
                                   half, 1, send, recv, 2 * t + r, (*chip, c)))
        return out
    return _Transfer(2 * len(keys), build)


def _gather_relay(keys, shapes):
    def build(refs, send, recv):
        x, y, c, chips = _place()
        slot_x, slot_y, slot_d = (2 * ch[0] + ch[1] for ch in chips)
        src_slot = slot_y + c * (slot_x - slot_y)
        to = (x ^ (1 - c), y ^ c, c)
        out = []
        for t, key in enumerate(keys):
            half = shapes[t][1] // 2
            out.append(_copies(lambda s, z, key=key, half=half: refs[key].at[src_slot, pl.ds(c * half + s, z)],
                               lambda s, z, key=key, half=half: refs[key].at[src_slot, pl.ds(c * half + s, z)],
                               lambda s, z, key=key, half=half: refs[key].at[slot_d, pl.ds(c * half + s, z)],
                               half, 1, send, recv, t, to))
        return out
    return _Transfer(len(keys), build)


def _gather_pair(keys, shapes):
    def build(refs, send, recv):
        x, y, c, chips = _place()
        out = []
        for t, key in enumerate(keys):
            half = shapes[t][1] // 2
            for r, chip in enumerate(chips):
                slot = 2 * chip[0] + chip[1]
                mine = lambda s, z, key=key, half=half, slot=slot: refs[key].at[slot, pl.ds(c * half + s, z)]
                land = lambda s, z, key=key, half=half, slot=slot: refs[key].at[slot, pl.ds((1 - c) * half + s, z)]
                out.append(_copies(mine, mine, land, half, D2D_CHUNKS, send, recv, 3 * t + r, (x, y, 1 - c)))
        return out
    return _Transfer(3 * len(keys), build)


def _grad_pair(keys, lands, shapes):
    def build(refs, send, recv):
        x, y, c, _ = _place()
        out = []
        for t, (key, land) in enumerate(zip(keys, lands)):
            half = shapes[t][1] // 2
            for j in range(N_CHIPS):
                out.append(_copies(
                    lambda s, z, key=key, half=half, j=j: refs[key].at[j, pl.ds((1 - c) * half + s, z)],
                    lambda s, z, land=land, j=j: refs[land].at[j, pl.ds(s, z)],
                    lambda s, z, land=land, j=j: refs[land].at[j, pl.ds(s, z)],
                    half, 2, send, recv, N_CHIPS * t + j, (x, y, 1 - c)))
        return out
    return _Transfer(N_CHIPS * len(keys), build)


def _grad_chips(keys, lands, shapes):
    def build(refs, send, recv):
        x, y, c, chips = _place()
        out = []
        for t, (key, land) in enumerate(zip(keys, lands)):
            rows = shapes[t][1]
            for r, chip in enumerate(chips):
                slot = 2 * chip[0] + chip[1]
                out.append(_copies(lambda s, z, key=key, slot=slot: refs[key].at[slot, pl.ds(s, z)],
                                   lambda s, z, land=land, r=r: refs[land].at[r, pl.ds(s, z)],
                                   lambda s, z, land=land, r=r: refs[land].at[r, pl.ds(s, z)],
                                   rows, 1, send, recv, 3 * t + r, (*chip, c)))
        return out
    return _Transfer(3 * len(keys), build)


def _grad_join(keys, lands, shapes):
    def build(refs, send, recv):
        x, y, c, _ = _place()
        out = []
        for t, (key, land) in enumerate(zip(keys, lands)):
            out.append(_copies(lambda s, z, key=key: refs[key].at[pl.ds(s, z)],
                               lambda s, z, land=land: refs[land].at[pl.ds(s, z)],
                               lambda s, z, land=land: refs[land].at[pl.ds(s, z)],
                               shapes[t][0], D2D_CHUNKS, send, recv, t, (x, y, 1 - c)))
        return out
    return _Transfer(len(keys), build)


def _comm_call(name, arrays, waits, starts, after):
    keys = list(arrays)
    n, nw, ns = len(keys), len(waits), len(starts)

    def body(*refs):
        in_sems = refs[n:n + 2 * nw]
        base = n + 2 * nw + 1
        out_sems = refs[base:base + 2 * ns]
        bufs = dict(zip(keys, refs[base + 2 * ns:base + 2 * ns + n]))
        token = refs[base + 2 * ns + n]
        for k, (transfer, _, _) in enumerate(waits):
            for _, whole in transfer.build(bufs, in_sems[2 * k], in_sems[2 * k + 1]):
                whole.wait_send()
                whole.wait_recv()
        for k, transfer in enumerate(starts):
            for chunks, _ in transfer.build(bufs, out_sems[2 * k], out_sems[2 * k + 1]):
                for cp in chunks:
                    cp.start()
        token[...] = jnp.zeros_like(token)

    sem_shapes = []
    for transfer in starts:
        sem_shapes += [pltpu.SemaphoreType.DMA((transfer.n_sems,))] * 2
    operands = [_hbm(arrays[k]) for k in keys]
    for _, send, recv in waits:
        operands += [send, recv]
    res = pl.pallas_call(
        body, name=name, in_specs=[HBM] * n + [SEM] * (2 * nw) + [pl.BlockSpec(memory_space=pl.ANY)],
        out_specs=[SEM] * (2 * ns) + [HBM] * n + [pl.BlockSpec(memory_space=pltpu.VMEM)],
        out_shape=sem_shapes + [pltpu.HBM(arrays[k].shape, arrays[k].dtype) for k in keys] + [_token_shape()],
        input_output_aliases={t: 2 * ns + t for t in range(n)},
        compiler_params=pltpu.CompilerParams(has_side_effects=EFFECT),
    )(*operands, after)
    sems = [(res[2 * k], res[2 * k + 1]) for k in range(ns)]
    return dict(zip(keys, res[2 * ns:2 * ns + n])), sems, res[2 * ns + n]


def _allreduce_small(part, name, dep=None):
    rows = part.shape[0]

    def body(p_ref, *rest):
        o_ref, all_ref, send, recv = rest[-4:]
        x, y, c, _ = _place()
        me = 4 * x + 2 * y + c
        all_ref[me] = p_ref[...]
        copies = []
        for r in range(1, N_DEV):
            to = (x ^ (r >> 2), y ^ ((r >> 1) & 1), c ^ (r & 1))
            cp = _remote(p_ref, all_ref.at[me], send, recv, r - 1, to)
            cp.start()
            copies.append(cp)
        for r in range(1, N_DEV):
            frm = (x ^ (r >> 2), y ^ ((r >> 1) & 1), c ^ (r & 1))
            slot = all_ref.at[4 * frm[0] + 2 * frm[1] + frm[2]]
            _remote(slot, slot, send, recv, r - 1, frm).wait_recv()
        for cp in copies:
            cp.wait_send()
        acc = all_ref[0]
        for dev in range(1, N_DEV):
            acc = acc + all_ref[dev]
        o_ref[...] = acc

    vmem = pl.BlockSpec(memory_space=pltpu.VMEM)
    operands = [part] + ([dep] if dep is not None else [])
    return pl.pallas_call(
        body, name=name, in_specs=[vmem] * len(operands), out_specs=vmem,
        out_shape=jax.ShapeDtypeStruct(part.shape, F32),
        scratch_shapes=[pltpu.VMEM((N_DEV, rows, LANES), F32), pltpu.SemaphoreType.DMA((N_DEV - 1,)),
                        pltpu.SemaphoreType.DMA((N_DEV - 1,))],
        compiler_params=pltpu.CompilerParams(vmem_limit_bytes=VMEM_LIMIT),
    )(*operands)


INPUT_NAMES = None


def _weight_names():
    names = []
    for i, kind in enumerate(("gmlp", "swa", "fox", "gmlp")):
        p = f"l{i}_"
        names += [p + "ffn1_norm", p + "ffn1_wi", p + "ffn1_wo", p + "mix_norm", p + "mix_win"]
        if kind == "gmlp":
            names += [p + "gmlp_vnorm", p + "gmlp_ws", p + "gmlp_bs"]
        elif kind == "swa":
            names += [p + "swa_sinks"]
        else:
            names += [p + "fox_bf"]
        names += [p + "mix_wout", p + "ffn2_norm", p + "ffn2_wi", p + "ffn2_wo"]
    return names + ["final_norm"]


WEIGHTS = _weight_names()
MIXERS = ("gmlp", "swa", "fox", "gmlp")
BIG = ("ffn1_wi", "ffn1_wo", "mix_win", "mix_wout", "ffn2_wi", "ffn2_wo")


def _ffn_fwd(h, gain, wi, wo, tag, dep=None):
    n = _rms_fwd(h, gain, tag + "_norm", dep=dep)
    z = _matmul(n, wi, name=tag + "_up", out_dtype=BF16)
    a = _swiglu_fwd(z, tag + "_act")
    f, d = wo.shape[0] * wo.shape[1], wo.shape[2]
    out = _matmul(a, wo.reshape(f, d), name=tag + "_down", out_dtype=F32, scale=0.5, resid=h)
    return out, (h, n, z, a)


def _ffn_bwd(dout, saved, gain, wi, wo, tag, dep=None):
    h, n, z, a = saved
    f, d = wo.shape[0] * wo.shape[1], wo.shape[2]
    da = _matmul(dout, wo.reshape(f, d), tb=True, name=tag + "_bdown", out_dtype=BF16, scale=0.5, dep=dep)
    dwo = _matmul(a, dout, ta=True, name=tag + "_gdown", out_dtype=BF16, scale=0.5, dep=dep)
    dz = _swiglu_bwd(z, da, tag + "_bact")
    dn = _matmul(dz, wi, tb=True, name=tag + "_bup", out_dtype=F32)
    dwi = _matmul(n, dz, ta=True, name=tag + "_gup", out_dtype=BF16, out_shards=N_CHIPS)
    dh, dgain = _norm_bwd(h, gain, dn, dout, tag + "_bnorm")
    return dh, dgain, dwi, dwo.reshape(wo.shape)


def _natural(w_sharded, pad_to):
    ns, rows, csh = w_sharded.shape
    nat = jnp.transpose(w_sharded, (1, 0, 2)).reshape(rows, ns * csh)
    extra = (-nat.shape[1]) % pad_to
    return jnp.pad(nat, ((0, 0), (0, extra))) if extra else nat


def _mixer_fwd(kind, h, p, tag, dep=None):
    s_len, d = h.shape
    n = _rms_fwd(h, p["mix_norm"], tag + "_norm", dep=dep)
    wout = p["mix_wout"].reshape(d, d)
    if kind == "gmlp":
        zp = _matmul(n, p["mix_win"], name=tag + "_in", out_dtype=BF16)
        y = _gmlp_fwd(zp, p["gmlp_vnorm"], p["gmlp_ws"], p["gmlp_bs"], tag + "_gate")
        saved = (h, n, zp, y)
    elif kind == "swa":
        qkv = _matmul(n, p["mix_win"], name=tag + "_in", out_dtype=F32)
        q, k, v = _rope_fwd(qkv, tag + "_rope")
        y = _swa_fwd(q, k, v, p["swa_sinks"], tag + "_attn")
        saved = (h, n, q, k, v, y)
    else:
        heads = d // FOX_HEAD_DIM
        win = _natural(p["mix_win"], LANES)
        proj = _matmul(n, win, name=tag + "_in", out_dtype=F32)
        bf_row = jnp.pad(p["fox_bf"], (0, LANES - heads)).reshape(1, LANES)
        dec = _fox_decay(proj, bf_row, 3 * heads, tag + "_decay")
        dec_t = dec[:, :heads].T
        decq, deck = dec_t.reshape(heads, s_len, 1), dec_t.reshape(heads, 1, s_len)
        y = _fox_fwd(proj, decq, deck, heads, tag + "_attn")
        saved = (h, n, win, proj, bf_row, decq, deck, y)
    out = _matmul(y, wout, name=tag + "_out", out_dtype=F32, resid=h)
    return out, saved


def _mixer_bwd(kind, dout, saved, p, tag, dep=None):
    h, n = saved[0], saved[1]
    y = saved[-1]
    s_len, d = h.shape
    wout = p["mix_wout"].reshape(d, d)
    grads = {}
    dy = _matmul(dout, wout, tb=True, name=tag + "_bout", out_dtype=BF16, dep=dep)
    grads["mix_wout"] = _matmul(y, dout, ta=True, name=tag + "_gout", out_dtype=BF16,
                                dep=dep).reshape(p["mix_wout"].shape)
    if kind == "gmlp":
        zp = saved[2]
        dzp, dws, dbst, dvg = _gmlp_bwd(zp, dy, p["gmlp_vnorm"], p["gmlp_ws"], p["gmlp_bs"], tag + "_bgate")
        grads.update(gmlp_ws=dws, gmlp_bs=dbst.T, gmlp_vnorm=dvg.reshape(d))
        dn = _matmul(dzp, p["mix_win"], tb=True, name=tag + "_bin", out_dtype=F32)
        grads["mix_win"] = _matmul(n, dzp, ta=True, name=tag + "_gin", out_dtype=BF16, out_shards=N_CHIPS)
    elif kind == "swa":
        q, k, v = saved[2:5]
        dq, dk, dv, dsinks = _swa_bwd(q, k, v, p["swa_sinks"], dy, tag + "_battn")
        grads["swa_sinks"] = dsinks[0, :p["swa_sinks"].shape[0]]
        dqkv = _rope_bwd(dq, dk, dv, tag + "_brope")
        dn = _matmul(dqkv, p["mix_win"], tb=True, name=tag + "_bin", out_dtype=F32)
        grads["mix_win"] = _matmul(n, dqkv, ta=True, name=tag + "_gin", out_dtype=BF16, out_shards=N_CHIPS)
    else:
        win, proj, bf_row, decq, deck = saved[2:7]
        heads = d // FOX_HEAD_DIM
        dq, dk, dv, ddq, ddk = _fox_bwd(proj, decq, deck, dy, heads, tag + "_battn")
        widen = lambda t: jnp.pad(t.reshape(heads, s_len).T, ((0, 0), (0, LANES - heads)))
        dfl, dbf = _fox_decay_bwd(widen(ddq), widen(ddk), proj, bf_row, 3 * heads, heads, tag + "_bdecay")
        grads["fox_bf"] = dbf[0, :heads]
        dproj = jnp.concatenate([dq, dk.astype(BF16), dv.astype(BF16), dfl], axis=1)
        dn = _matmul(dproj, win, tb=True, name=tag + "_bin", out_dtype=F32)
        dwin = _matmul(n, dproj, ta=True, name=tag + "_gin", out_dtype=BF16)
        ns, rows, csh = p["mix_win"].shape
        grads["mix_win"] = jnp.transpose(dwin[:, :ns * csh].reshape(rows, ns, csh), (1, 0, 2))
    dh, dgain = _norm_bwd(h, p["mix_norm"], dn, dout, tag + "_bnorm")
    grads["mix_norm"] = dgain.reshape(d)
    return dh, grads


def _pack_small(arrays):
    flat = jnp.concatenate([a.reshape(-1).astype(F32) for a in arrays])
    pad = (-flat.shape[0]) % (512 * LANES)
    return jnp.pad(flat, (0, pad)).reshape(-1, LANES)


def _unpack_small(packed, like):
    flat, out, pos = packed.reshape(-1), [], 0
    for a in like:
        out.append(flat[pos:pos + a.size].reshape(a.shape))
        pos += a.size
    return out


def _step(inp):
    x, target = inp["x"][0], inp["loss_target"][0]
    d = x.shape[1]

    core = lax.axis_index("c").astype(jnp.int32).reshape(1)
    chip = (2 * lax.axis_index("x") + lax.axis_index("y")).astype(jnp.int32).reshape(1)

    groups = []
    for i in range(len(MIXERS)):
        groups += [(i, "ffn1", [f"l{i}_ffn1_wi", f"l{i}_ffn1_wo"]), (i, "mix", [f"l{i}_mix_win", f"l{i}_mix_wout"]),
                   (i, "ffn2", [f"l{i}_ffn2_wi", f"l{i}_ffn2_wo"])]

    def layer_params(i, full):
        p = {nm[len(f"l{i}_"):]: inp[nm] for nm in WEIGHTS if nm.startswith(f"l{i}_")}
        p.update({nm[len(f"l{i}_"):]: w for nm, w in full.items()})
        return p

    n_groups = len(groups)
    valid = lambda k: 0 <= k < n_groups

    bufs = {}
    full_shapes = lambda names: [(N_CHIPS, *inp[nm].shape) for nm in names]
    direct = [_gather_direct(names, full_shapes(names)) for _, _, names in groups]
    relay = [_gather_relay(names, full_shapes(names)) for _, _, names in groups]
    to_pair = [_gather_pair(names, full_shapes(names)) for _, _, names in groups]
    sems = {}

    def cast_groups(which, dep):
        for g in which:
            for nm in groups[g][2]:
                bufs[nm] = _cast_into_slot(inp[nm], chip, nm + "_cast", dep=dep)

    def gather_step(step, after):
        waits, starts, tags, keys = [], [], [], []
        for kind, transfers, g, begin in (("pair", to_pair, step, False), ("relay", relay, step + 1, False),
                                          ("pair", to_pair, step + 1, True), ("direct", direct, step + 2, False),
                                          ("relay", relay, step + 2, True), ("direct", direct, step + 3, True)):
            if not valid(g):
                continue
            keys += [nm for nm in groups[g][2] if nm not in keys]
            if begin:
                starts.append(transfers[g])
                tags.append((kind, g))
            else:
                waits.append((transfers[g], *sems.pop((kind, g))))
        new, started, token = _comm_call(f"gather_step{step + 3}", {k: bufs[k] for k in keys}, waits, starts, after)
        bufs.update(new)
        sems.update(zip(tags, started))
        return token

    cast_groups([0], None)
    token = gather_step(-3, x)
    cast_groups(range(1, 6), token)
    token = gather_step(-2, bufs[groups[5][2][-1]])
    cast_groups(range(6, n_groups), token)
    token = gather_step(-1, bufs[groups[-1][2][-1]])
    h, saved, fulls = x, [], []
    for g, (i, part, names) in enumerate(groups):
        token = gather_step(g, h)
        full = {nm: bufs[nm] for nm in names}
        p = layer_params(i, full)
        if part == "mix":
            h, s = _mixer_fwd(MIXERS[i], h, p, f"l{i}_mix", dep=token)
        else:
            h, s = _ffn_fwd(h, p[part + "_norm"], p[part + "_wi"], p[part + "_wo"], f"l{i}_{part}", dep=token)
        saved.append(s)
        fulls.append(full)
    loss_part, dh, dfinal = _loss_head(h, inp["final_norm"], target, "loss_head")
    loss = lax.psum(loss_part, ("x", "y", "c"))

    small_grads = {"final_norm": dfinal.reshape(d)}
    outs = {}
    work = {}
    stage = {}
    small_names = [nm for nm in WEIGHTS if nm.split("_", 1)[1] not in BIG]
    like = [inp[nm] for nm in small_names]

    def comm(name, transfers_to_wait, transfers_to_start, after):
        waits = [(stage[k], *sems.pop(k)) for k in transfers_to_wait if valid(k[1])]
        starts = [k for k in transfers_to_start if valid(k[1])]
        if not waits and not starts:
            return after
        keys = []
        for k in [k for k in transfers_to_wait if valid(k[1])] + starts:
            keys += [key for key in stage_keys[k] if key not in keys]
        new, started, token = _comm_call(name, {k: work[k] for k in keys}, waits, [stage[k] for k in starts], after)
        work.update(new)
        sems.update(zip(starts, started))
        return token

    stage_keys = {}

    def reduce_step(g, after):
        token = comm(f"rs_pair_step{n_groups - 1 - g}", [("pair", g + 1)], [("pair", g)], after)
        if valid(g + 1):
            names = groups[g + 1][2]
            for nm in names:
                work[nm + "#sum"] = _add_pair(work[nm + "#grad"], work[nm + "#got"], core, nm + "_rs_add")
                work[nm + "#others"] = lax.empty((N_CHIPS - 1, *work[nm + "#sum"].shape[1:]), BF16)
            shapes = [work[nm + "#sum"].shape for nm in names]
            stage[("chips", g + 1)] = _grad_chips([nm + "#sum" for nm in names], [nm + "#others" for nm in names], shapes)
            stage_keys[("chips", g + 1)] = [nm + sfx for nm in names for sfx in ("#sum", "#others")]
        token = comm(f"rs_chips_step{n_groups - 1 - g}", [("chips", g + 2)], [("chips", g + 1)], token)
        if g == -1:
            total = _allreduce_small(_pack_small([small_grads[nm] for nm in small_names]), "small_allreduce", dep=token)
            upd = _adamw(_pack_small(like), total, _pack_small([inp["m_" + nm] for nm in small_names]),
                         _pack_small([inp["v_" + nm] for nm in small_names]), "small_adamw")
            unpacked = [_unpack_small(t, like) for t in (total, *upd)]
            for k, nm in enumerate(small_names):
                outs[nm] = tuple(u[k] for u in unpacked)
            token = upd[0]
        if valid(g + 2):
            names = groups[g + 2][2]
            for nm in names:
                work[nm + "#half"] = _sum_chips(work[nm + "#sum"], work[nm + "#others"], chip, nm + "_rs_sum")
                work[nm + "#theirs"] = lax.empty(work[nm + "#half"].shape, F32)
            shapes = [work[nm + "#half"].shape for nm in names]
            stage[("join", g + 2)] = _grad_join([nm + "#half" for nm in names], [nm + "#theirs" for nm in names], shapes)
            stage_keys[("join", g + 2)] = [nm + sfx for nm in names for sfx in ("#half", "#theirs")]
        token = comm(f"rs_join_step{n_groups - 1 - g}", [("join", g + 3)], [("join", g + 2)], token)
        if valid(g + 3):
            for nm in groups[g + 3][2]:
                outs[nm] = tuple(_adamw_halves(inp[nm], work[nm + "#half"], work[nm + "#theirs"], inp["m_" + nm],
                                               inp["v_" + nm], core, nm + "_adamw"))
        return token

    dep = None
    for g in reversed(range(n_groups)):
        i, part, names = groups[g]
        p = layer_params(i, fulls[g])
        if part == "mix":
            dh, mg = _mixer_bwd(MIXERS[i], dh, saved[g], p, f"l{i}_mix", dep=dep)
            grads = [mg.pop("mix_win"), mg.pop("mix_wout")]
            small_grads.update({f"l{i}_{key}": val for key, val in mg.items()})
        else:
            dh, g_norm, dwi, dwo = _ffn_bwd(dh, saved[g], p[part + "_norm"], p[part + "_wi"], p[part + "_wo"],
                                            f"l{i}_{part}", dep=dep)
            small_grads[f"l{i}_{part}_norm"] = g_norm.reshape(d)
            grads = [dwi, dwo]
        for nm, gr in zip(names, grads):
            work[nm + "#grad"] = gr
            work[nm + "#got"] = lax.empty((gr.shape[0], gr.shape[1] // 2, gr.shape[2]), BF16)
        stage[("pair", g)] = _grad_pair([nm + "#grad" for nm in names], [nm + "#got" for nm in names],
                                        [gr.shape for gr in grads])
        stage_keys[("pair", g)] = [nm + sfx for nm in names for sfx in ("#grad", "#got")]
        dep = reduce_step(g, dh)
    for g in (-1, -2, -3):
        dep = reduce_step(g, dep)

    result = [loss, dh[None]]
    for part in range(4):
        result += [outs[nm][part] for nm in WEIGHTS]
    return tuple(result)


def kernel(x, l0_ffn1_norm, l0_ffn1_wi, l0_ffn1_wo, l0_mix_norm, l0_mix_win, l0_gmlp_vnorm, l0_gmlp_ws, l0_gmlp_bs, l0_mix_wout, l0_ffn2_norm, l0_ffn2_wi, l0_ffn2_wo, l1_ffn1_norm, l1_ffn1_wi, l1_ffn1_wo, l1_mix_norm, l1_mix_win, l1_swa_sinks, l1_mix_wout, l1_ffn2_norm, l1_ffn2_wi, l1_ffn2_wo, l2_ffn1_norm, l2_ffn1_wi, l2_ffn1_wo, l2_mix_norm, l2_mix_win, l2_fox_bf, l2_mix_wout, l2_ffn2_norm, l2_ffn2_wi, l2_ffn2_wo, l3_ffn1_norm, l3_ffn1_wi, l3_ffn1_wo, l3_mix_norm, l3_mix_win, l3_gmlp_vnorm, l3_gmlp_ws, l3_gmlp_bs, l3_mix_wout, l3_ffn2_norm, l3_ffn2_wi, l3_ffn2_wo, final_norm, loss_target, m_l0_ffn1_norm, m_l0_ffn1_wi, m_l0_ffn1_wo, m_l0_mix_norm, m_l0_mix_win, m_l0_gmlp_vnorm, m_l0_gmlp_ws, m_l0_gmlp_bs, m_l0_mix_wout, m_l0_ffn2_norm, m_l0_ffn2_wi, m_l0_ffn2_wo, m_l1_ffn1_norm, m_l1_ffn1_wi, m_l1_ffn1_wo, m_l1_mix_norm, m_l1_mix_win, m_l1_swa_sinks, m_l1_mix_wout, m_l1_ffn2_norm, m_l1_ffn2_wi, m_l1_ffn2_wo, m_l2_ffn1_norm, m_l2_ffn1_wi, m_l2_ffn1_wo, m_l2_mix_norm, m_l2_mix_win, m_l2_fox_bf, m_l2_mix_wout, m_l2_ffn2_norm, m_l2_ffn2_wi, m_l2_ffn2_wo, m_l3_ffn1_norm, m_l3_ffn1_wi, m_l3_ffn1_wo, m_l3_mix_norm, m_l3_mix_win, m_l3_gmlp_vnorm, m_l3_gmlp_ws, m_l3_gmlp_bs, m_l3_mix_wout, m_l3_ffn2_norm, m_l3_ffn2_wi, m_l3_ffn2_wo, m_final_norm, v_l0_ffn1_norm, v_l0_ffn1_wi, v_l0_ffn1_wo, v_l0_mix_norm, v_l0_mix_win, v_l0_gmlp_vnorm, v_l0_gmlp_ws, v_l0_gmlp_bs, v_l0_mix_wout, v_l0_ffn2_norm, v_l0_ffn2_wi, v_l0_ffn2_wo, v_l1_ffn1_norm, v_l1_ffn1_wi, v_l1_ffn1_wo, v_l1_mix_norm, v_l1_mix_win, v_l1_swa_sinks, v_l1_mix_wout, v_l1_ffn2_norm, v_l1_ffn2_wi, v_l1_ffn2_wo, v_l2_ffn1_norm, v_l2_ffn1_wi, v_l2_ffn1_wo, v_l2_mix_norm, v_l2_mix_win, v_l2_fox_bf, v_l2_mix_wout, v_l2_ffn2_norm, v_l2_ffn2_wi, v_l2_ffn2_wo, v_l3_ffn1_norm, v_l3_ffn1_wi, v_l3_ffn1_wo, v_l3_mix_norm, v_l3_mix_win, v_l3_gmlp_vnorm, v_l3_gmlp_ws, v_l3_gmlp_bs, v_l3_mix_wout, v_l3_ffn2_norm, v_l3_ffn2_wi, v_l3_ffn2_wo, v_final_norm):
    return _step(dict(locals()))
```

```python
import functools
import math

import jax
import jax.numpy as jnp
from jax import lax
from jax.experimental import pallas as pl
from jax.experimental.pallas import tpu as pltpu

F32 = jnp.float32
BF16 = jnp.bfloat16

NORM_EPS = 1e-5
NEG_INF = -1e30
BLOCK = 128
GMLP_GROUPS = 16
SWA_HEAD_DIM = 64
SWA_GROUP = 8
ROPE_DIM = SWA_HEAD_DIM // 4
ROPE_THETA = 500000.0
FOX_HEAD_DIM = 128
ADAM_LR = 0.001
ADAM_B1 = 0.9
ADAM_B2 = 0.999
ADAM_EPS = 1e-08
ADAM_WD = 0.01
ADAM_STEP = 10
N_CHIPS = 4
N_DEV = 8
LANES = 128
VMEM_LIMIT = 56 * 1024 * 1024
MESH = pl.DeviceIdType.MESH
HBM = pl.BlockSpec(memory_space=pltpu.HBM)
SEM = pl.BlockSpec(memory_space=pltpu.SEMAPHORE)
EFFECT = pltpu.SideEffectType.DATAFLOW_SIDE_EFFECTING

MM_TILES = (1024, 1408, 896, 640, 512, 384, 256, 128)
K_TILES = (2816,) + MM_TILES


def _pick(n, prefs):
    for p in prefs:
        if p <= n and n % p == 0:
            return p
    return n


def _params(*sem):
    return pltpu.CompilerParams(dimension_semantics=sem or None, vmem_limit_bytes=VMEM_LIMIT)


def _cols(arr):
    return arr.shape[-1] * (arr.shape[0] if arr.ndim == 3 else 1)


def _mat_spec(arr, rb, cb, ridx, cidx):
    if arr.ndim == 2:
        return pl.BlockSpec((rb, cb), lambda j, i, k: (ridx(j, i, k), cidx(j, i, k)))
    per = arr.shape[2] // cb
    return pl.BlockSpec((None, rb, cb),
                        lambda j, i, k: (cidx(j, i, k) // per, ridx(j, i, k), cidx(j, i, k) % per))


def _matmul(a, b, *, name, out_dtype, ta=False, tb=False, out_shards=1, scale=1.0, resid=None, dep=None):
    m_dim, k_dim = (a.shape[1], a.shape[0]) if ta else a.shape
    n_dim = b.shape[-2] if tb else _cols(b)
    assert k_dim == (_cols(b) if tb else b.shape[-2]), (a.shape, b.shape, ta, tb)
    n_unit = n_dim // out_shards
    if b.ndim == 3 and not tb:
        n_unit = math.gcd(n_unit, b.shape[2])
    k_unit = b.shape[2] if (b.ndim == 3 and tb) else k_dim
    bm = _pick(m_dim, MM_TILES)
    bn = _pick(n_unit, MM_TILES)
    bk = k_unit if k_unit <= 2048 else _pick(k_unit, K_TILES)
    nk = k_dim // bk
    i_of, j_of, k_of = (lambda j, i, k: i), (lambda j, i, k: j), (lambda j, i, k: k)
    a_spec = _mat_spec(a, bk, bm, k_of, i_of) if ta else _mat_spec(a, bm, bk, i_of, k_of)
    b_spec = _mat_spec(b, bn, bk, j_of, k_of) if tb else _mat_spec(b, bk, bn, k_of, j_of)
    out_shape = (m_dim, n_dim) if out_shards == 1 else (out_shards, m_dim, n_dim // out_shards)
    out = jax.ShapeDtypeStruct(out_shape, out_dtype)
    o_spec = _mat_spec(out, bm, bn, i_of, j_of)
    dims = (((0 if ta else 1,), (1 if tb else 0,)), ((), ()))
    operands, in_specs = [a, b], [a_spec, b_spec]
    if resid is not None:
        operands.append(resid)
        in_specs.append(_mat_spec(resid, bm, bn, i_of, j_of))
    if dep is not None:
        operands.append(dep)
        in_specs.append(pl.BlockSpec(dep.shape, lambda j, i, k: (0, 0)))
    n_in = len(operands)

    def body(*refs):
        a_ref, b_ref = refs[0], refs[1]
        r_ref = refs[2] if resid is not None else None
        o_ref = refs[n_in]
        part = lax.dot_general(a_ref[...].astype(BF16), b_ref[...].astype(BF16), dims,
                               preferred_element_type=F32)

        def finish(acc):
            val = acc * scale if scale != 1.0 else acc
            if r_ref is not None:
                val = r_ref[...] + val
            o_ref[...] = val.astype(o_ref.dtype)

        if nk == 1:
            finish(part)
        else:
            acc_ref = refs[-1]
            k = pl.program_id(2)

            @pl.when(k == 0)
            def _():
                acc_ref[...] = part

            @pl.when(k > 0)
            def _():
                acc_ref[...] += part

            @pl.when(k == nk - 1)
            def _():
                finish(acc_ref[...])

    return pl.pallas_call(
        body, name=name, grid=(n_dim // bn, m_dim // bm, nk),
        in_specs=in_specs, out_specs=o_spec, out_shape=out,
        scratch_shapes=[pltpu.VMEM((bm, bn), F32)] if nk > 1 else [],
        compiler_params=_params("parallel", "parallel", "arbitrary"),
    )(*operands)


def _row_block(rows, width, itemsize=4, budget=2 << 20):
    best = None
    for br in range(16, rows + 1, 16):
        if rows % br == 0 and br * width * itemsize <= budget:
            best = br
    return best or rows


def _rms_fwd(h, g, name, dep=None):
    s_len, d = h.shape
    br = _row_block(s_len, d)

    def body(h_ref, g_ref, *rest):
        o_ref = rest[-1]
        x = h_ref[...]
        r = lax.rsqrt(jnp.mean(x * x, axis=-1, keepdims=True) + NORM_EPS)
        o_ref[...] = (x * r * g_ref[...]).astype(BF16)

    spec = pl.BlockSpec((br, d), lambda i: (i, 0))
    operands = [h, g.reshape(1, d)] + ([dep] if dep is not None else [])
    in_specs = [spec, pl.BlockSpec((1, d), lambda i: (0, 0))]
    if dep is not None:
        in_specs.append(pl.BlockSpec(dep.shape, lambda i: (0, 0)))
    return pl.pallas_call(body, name=name, grid=(s_len // br,), in_specs=in_specs, out_specs=spec,
                          out_shape=jax.ShapeDtypeStruct((s_len, d), BF16),
                          compiler_params=_params("parallel"))(*operands)


def _rms_bwd_rows(x, g, dn):
    r = lax.rsqrt(jnp.mean(x * x, axis=-1, keepdims=True) + NORM_EPS)
    xhat = x * r
    gdn = dn * g
    dx = r * (gdn - xhat * jnp.mean(gdn * xhat, axis=-1, keepdims=True))
    return dx, dn * xhat


def _norm_bwd(h, g, dn, dres, name):
    s_len, d = h.shape
    br = _row_block(s_len, d, budget=1 << 20)

    def body(h_ref, g_ref, dn_ref, dres_ref, dh_ref, dg_ref):
        dx, dg_rows = _rms_bwd_rows(h_ref[...], g_ref[...], dn_ref[...].astype(F32))
        dh_ref[...] = dres_ref[...] + dx

        @pl.when(pl.program_id(0) == 0)
        def _():
            dg_ref[...] = jnp.zeros_like(dg_ref)

        dg_ref[...] += jnp.sum(dg_rows, axis=0, keepdims=True)

    spec = pl.BlockSpec((br, d), lambda i: (i, 0))
    vec = pl.BlockSpec((1, d), lambda i: (0, 0))
    return pl.pallas_call(body, name=name, grid=(s_len // br,),
                          in_specs=[spec, vec, spec, spec], out_specs=[spec, vec],
                          out_shape=[jax.ShapeDtypeStruct((s_len, d), F32),
                                     jax.ShapeDtypeStruct((1, d), F32)],
                          compiler_params=_params("arbitrary"))(h, g.reshape(1, d), dn, dres)


def _sigmoid(x):
    return 0.5 * (1.0 + jnp.tanh(0.5 * x))


def _swiglu_fwd(z, name):
    s_len, f2 = z.shape
    f = f2 // 2
    br = _row_block(s_len, f2, itemsize=2, budget=6 << 20)
    fc = _pick(f, MM_TILES)

    def body(z_ref, a_ref):
        for c0 in range(0, f, fc):
            gate = z_ref[:, c0:c0 + fc].astype(F32)
            up = z_ref[:, f + c0:f + c0 + fc].astype(F32)
            a_ref[:, c0:c0 + fc] = (gate * _sigmoid(gate) * up).astype(BF16)

    return pl.pallas_call(body, name=name, grid=(s_len // br,),
                          in_specs=[pl.BlockSpec((br, f2), lambda i: (i, 0))],
                          out_specs=pl.BlockSpec((br, f), lambda i: (i, 0)),
                          out_shape=jax.ShapeDtypeStruct((s_len, f), BF16),
                          compiler_params=_params("parallel"))(z)


def _swiglu_bwd(z, da, name):
    s_len, f2 = z.shape
    f = f2 // 2
    br = _row_block(s_len, f2, itemsize=2, budget=6 << 20)
    fc = _pick(f, MM_TILES)

    def body(z_ref, da_ref, dz_ref):
        for c0 in range(0, f, fc):
            gate = z_ref[:, c0:c0 + fc].astype(F32)
            up = z_ref[:, f + c0:f + c0 + fc].astype(F32)
            d = da_ref[:, c0:c0 + fc].astype(F32)
            sig = _sigmoid(gate)
            dz_ref[:, c0:c0 + fc] = (d * up * (sig * (1.0 + gate * (1.0 - sig)))).astype(BF16)
            dz_ref[:, f + c0:f + c0 + fc] = (d * gate * sig).astype(BF16)

    return pl.pallas_call(body, name=name, grid=(s_len // br,),
                          in_specs=[pl.BlockSpec((br, f2), lambda i: (i, 0)),
                                    pl.BlockSpec((br, f), lambda i: (i, 0))],
                          out_specs=pl.BlockSpec((br, f2), lambda i: (i, 0)),
                          out_shape=jax.ShapeDtypeStruct((s_len, f2), BF16),
                          compiler_params=_params("parallel"))(z, da)


def _loss_head(h, g, target, name):
    s_len, d = h.shape
    br = _row_block(s_len, d, budget=1 << 20)

    def body(h_ref, g_ref, t_ref, loss_ref, dh_ref, dg_ref):
        x = h_ref[...]
        gain = g_ref[...]
        r = lax.rsqrt(jnp.mean(x * x, axis=-1, keepdims=True) + NORM_EPS)
        err = x * r * gain - t_ref[...]
        part = 0.5 * jnp.sum(jnp.mean(err * err, axis=-1, keepdims=True), axis=0, keepdims=True)
        dx, dg_rows = _rms_bwd_rows(x, gain, err * (1.0 / d))
        dh_ref[...] = dx

        @pl.when(pl.program_id(0) == 0)
        def _():
            dg_ref[...] = jnp.zeros_like(dg_ref)
            loss_ref[...] = jnp.zeros_like(loss_ref)

        dg_ref[...] += jnp.sum(dg_rows, axis=0, keepdims=True)
        loss_ref[...] += jnp.broadcast_to(part, loss_ref.shape)

    spec = pl.BlockSpec((br, d), lambda i: (i, 0))
    vec = pl.BlockSpec((1, d), lambda i: (0, 0))
    one = pl.BlockSpec((1, LANES), lambda i: (0, 0))
    loss, dh, dg = pl.pallas_call(
        body, name=name, grid=(s_len // br,), in_specs=[spec, vec, spec],
        out_specs=[one, spec, vec],
        out_shape=[jax.ShapeDtypeStruct((1, LANES), F32), jax.ShapeDtypeStruct((s_len, d), F32),
                   jax.ShapeDtypeStruct((1, d), F32)],
        compiler_params=_params("arbitrary"))(h, g.reshape(1, d), target)
    return loss[0, 0], dh, dg


def _adamw(w, g, m, v, name):
    rows, width = w.shape
    br = _row_block(rows, width, budget=1 << 20)
    c1 = 1.0 - ADAM_B1 ** ADAM_STEP
    c2 = 1.0 - ADAM_B2 ** ADAM_STEP

    def body(w_ref, g_ref, m_ref, v_ref, d_ref, nm_ref, nv_ref):
        grad = g_ref[...]
        new_m = ADAM_B1 * m_ref[...] + (1.0 - ADAM_B1) * grad
        new_v = ADAM_B2 * v_ref[...] + (1.0 - ADAM_B2) * (grad * grad)
        d_ref[...] = -ADAM_LR * ((new_m / c1) / (jnp.sqrt(new_v / c2) + ADAM_EPS) + ADAM_WD * w_ref[...])
        nm_ref[...] = new_m
        nv_ref[...] = new_v

    spec = pl.BlockSpec((br, width), lambda i: (i, 0))
    shp = jax.ShapeDtypeStruct(w.shape, F32)
    return pl.pallas_call(body, name=name, grid=(rows // br,), in_specs=[spec] * 4,
                          out_specs=[spec] * 3, out_shape=[shp] * 3,
                          compiler_params=_params("parallel"))(w, g, m, v)


def _gelu(x):
    return 0.5 * x * (1.0 + lax.erf(x * (2.0 ** -0.5)))


def _gelu_grad(x):
    return 0.5 * (1.0 + lax.erf(x * (2.0 ** -0.5))) + x * jnp.exp(-0.5 * x * x) * ((2.0 * math.pi) ** -0.5)


def _tril_mask():
    row = lax.broadcasted_iota(jnp.int32, (BLOCK, BLOCK), 0)
    col = lax.broadcasted_iota(jnp.int32, (BLOCK, BLOCK), 1)
    return col <= row


def _gmlp_specs(s_len, d):
    gw = d // GMLP_GROUPS
    zp = pl.BlockSpec((BLOCK, 2 * d), lambda i: (i, 0))
    row = pl.BlockSpec((BLOCK, d), lambda i: (i, 0))
    vec = pl.BlockSpec((1, d), lambda i: (0, 0))
    ws = pl.BlockSpec((GMLP_GROUPS, BLOCK, BLOCK), lambda i: (0, 0, 0))
    bst = pl.BlockSpec((BLOCK, GMLP_GROUPS), lambda i: (0, 0))
    return gw, zp, row, vec, ws, bst


def _gmlp_fwd(zp, vgain, ws, bs, name):
    s_len, d2 = zp.shape
    d = d2 // 2
    gw, zp_spec, row_spec, vec_spec, ws_spec, bst_spec = _gmlp_specs(s_len, d)

    def body(zp_ref, vg_ref, ws_ref, bst_ref, y_ref):
        u = _gelu(zp_ref[:, :d].astype(F32))
        vv = _gelu(zp_ref[:, d:].astype(F32))
        r = lax.rsqrt(jnp.mean(vv * vv, axis=-1, keepdims=True) + NORM_EPS)
        vn = (vv * r * vg_ref[...]).astype(BF16)
        mask = _tril_mask()
        for g in range(GMLP_GROUPS):
            cols = slice(g * gw, (g + 1) * gw)
            wg = jnp.where(mask, ws_ref[g], 0.0).astype(BF16)
            mixed = jnp.dot(wg, vn[:, cols], preferred_element_type=F32) + bst_ref[:, g:g + 1]
            y_ref[:, cols] = (u[:, cols] * mixed).astype(BF16)

    return pl.pallas_call(body, name=name, grid=(s_len // BLOCK,),
                          in_specs=[zp_spec, vec_spec, ws_spec, bst_spec], out_specs=row_spec,
                          out_shape=jax.ShapeDtypeStruct((s_len, d), BF16),
                          compiler_params=_params("parallel"))(zp, vgain.reshape(1, d), ws, bs.T)


def _gmlp_bwd(zp, dy, vgain, ws, bs, name):
    s_len, d2 = zp.shape
    d = d2 // 2
    gw, zp_spec, row_spec, vec_spec, ws_spec, bst_spec = _gmlp_specs(s_len, d)

    def body(zp_ref, dy_ref, vg_ref, ws_ref, bst_ref, dzp_ref, dws_ref, dbst_ref, dvg_ref, dvn_ref):
        @pl.when(pl.program_id(0) == 0)
        def _():
            dws_ref[...] = jnp.zeros_like(dws_ref)
            dbst_ref[...] = jnp.zeros_like(dbst_ref)
            dvg_ref[...] = jnp.zeros_like(dvg_ref)

        zu = zp_ref[:, :d].astype(F32)
        zv = zp_ref[:, d:].astype(F32)
        u = _gelu(zu)
        vv = _gelu(zv)
        r = lax.rsqrt(jnp.mean(vv * vv, axis=-1, keepdims=True) + NORM_EPS)
        vhat = vv * r
        gain = vg_ref[...]
        vn = (vhat * gain).astype(BF16)
        dyf = dy_ref[...].astype(F32)
        dmixed = dyf * u
        dmixed_b = dmixed.astype(BF16)
        mask = _tril_mask()
        lane = lax.broadcasted_iota(jnp.int32, (BLOCK, GMLP_GROUPS), 1)
        dbs_step = jnp.zeros((BLOCK, GMLP_GROUPS), F32)
        for g in range(GMLP_GROUPS):
            cols = slice(g * gw, (g + 1) * gw)
            wg = jnp.where(mask, ws_ref[g], 0.0).astype(BF16)
            mixed = jnp.dot(wg, vn[:, cols], preferred_element_type=F32) + bst_ref[:, g:g + 1]
            dzp_ref[:, cols] = (dyf[:, cols] * mixed * _gelu_grad(zu[:, cols])).astype(BF16)
            dm = dmixed_b[:, cols]
            dw = lax.dot_general(dm, vn[:, cols], (((1,), (1,)), ((), ())), preferred_element_type=F32)
            dws_ref[g] += jnp.where(mask, dw, 0.0)
            dbs_step = dbs_step + jnp.where(lane == g, jnp.sum(dmixed[:, cols], axis=-1, keepdims=True), 0.0)
            dvn_ref[:, cols] = lax.dot_general(wg, dm, (((0,), (0,)), ((), ())), preferred_element_type=F32)
        dbst_ref[...] += dbs_step
        dvn = dvn_ref[...]
        dvg_ref[...] += jnp.sum(dvn * vhat, axis=0, keepdims=True)
        dvhat = dvn * gain
        dvv = r * (dvhat - vhat * jnp.mean(dvhat * vhat, axis=-1, keepdims=True))
        dzp_ref[:, d:] = (dvv * _gelu_grad(zv)).astype(BF16)

    return pl.pallas_call(
        body, name=name, grid=(s_len // BLOCK,),
        in_specs=[zp_spec, row_spec, vec_spec, ws_spec, bst_spec],
        out_specs=[zp_spec, ws_spec, bst_spec, vec_spec],
        out_shape=[jax.ShapeDtypeStruct((s_len, d2), BF16), jax.ShapeDtypeStruct(ws.shape, F32),
                   jax.ShapeDtypeStruct((BLOCK, GMLP_GROUPS), F32), jax.ShapeDtypeStruct((1, d), F32)],
        scratch_shapes=[pltpu.VMEM((BLOCK, d), F32)],
        compiler_params=_params("arbitrary"))(zp, dy, vgain.reshape(1, d), ws, bs.T)


def _rope_tables(s_len, sign):
    half = ROPE_DIM // 2
    inv_freq = ROPE_THETA ** (-(jnp.arange(half, dtype=F32) * 2.0 / ROPE_DIM))
    ang = jnp.arange(s_len, dtype=F32)[:, None] * inv_freq[None, :]
    cos, sin = jnp.cos(ang), jnp.sin(ang) * sign
    pad = jnp.zeros((s_len, SWA_HEAD_DIM - ROPE_DIM), F32)
    zero = jnp.zeros_like(sin)
    cos_t = jnp.concatenate([cos, cos, pad + 1.0], axis=1)
    sin_up = jnp.concatenate([-sin, zero, pad], axis=1)
    sin_dn = jnp.concatenate([zero, sin, pad], axis=1)
    return [jnp.tile(t, (1, LANES // SWA_HEAD_DIM)) for t in (cos_t, sin_up, sin_dn)]


def _rotate(x, cos_t, sin_up, sin_dn):
    width = x.shape[-1]
    half = ROPE_DIM // 2
    reps = width // cos_t.shape[-1]
    if reps > 1:
        cos_t, sin_up, sin_dn = (jnp.tile(t, (1, reps)) for t in (cos_t, sin_up, sin_dn))
    elif reps == 0:
        cos_t, sin_up, sin_dn = (t[:, :width] for t in (cos_t, sin_up, sin_dn))
    return x * cos_t + pltpu.roll(x, width - half, 1) * sin_up + pltpu.roll(x, half, 1) * sin_dn


def _rope_fwd(qkv, name):
    s_len, total = qkv.shape
    wkv = total // (SWA_GROUP + 2)
    wq = SWA_GROUP * wkv
    br = _row_block(s_len, total, budget=2 << 20)
    tables = _rope_tables(s_len, 1.0)

    def body(q_ref, k_ref, v_ref, c_ref, su_ref, sd_ref, qo_ref, ko_ref, vo_ref):
        t = (c_ref[...], su_ref[...], sd_ref[...])
        qo_ref[...] = _rotate(q_ref[...], *t).astype(BF16)
        ko_ref[...] = _rotate(k_ref[...], *t).astype(BF16)
        vo_ref[...] = v_ref[...].astype(BF16)

    qs = pl.BlockSpec((br, wq), lambda i: (i, 0))
    ks = pl.BlockSpec((br, wkv), lambda i: (i, SWA_GROUP))
    vs = pl.BlockSpec((br, wkv), lambda i: (i, SWA_GROUP + 1))
    ts = pl.BlockSpec((br, LANES), lambda i: (i, 0))
    kv_out = pl.BlockSpec((br, wkv), lambda i: (i, 0))
    return pl.pallas_call(
        body, name=name, grid=(s_len // br,), in_specs=[qs, ks, vs, ts, ts, ts],
        out_specs=[qs, kv_out, kv_out],
        out_shape=[jax.ShapeDtypeStruct((s_len, wq), BF16), jax.ShapeDtypeStruct((s_len, wkv), BF16),
                   jax.ShapeDtypeStruct((s_len, wkv), BF16)],
        compiler_params=_params("parallel"))(qkv, qkv, qkv, *tables)


def _rope_bwd(dq, dk, dv, name):
    s_len, wq = dq.shape
    wkv = dk.shape[1]
    br = _row_block(s_len, wq + 2 * wkv, budget=2 << 20)
    tables = _rope_tables(s_len, -1.0)

    def body(q_ref, k_ref, v_ref, c_ref, su_ref, sd_ref, o_ref):
        t = (c_ref[...], su_ref[...], sd_ref[...])
        o_ref[:, :wq] = _rotate(q_ref[...], *t).astype(BF16)
        o_ref[:, wq:wq + wkv] = _rotate(k_ref[...], *t).astype(BF16)
        o_ref[:, wq + wkv:] = v_ref[...].astype(BF16)

    qs = pl.BlockSpec((br, wq), lambda i: (i, 0))
    kvs = pl.BlockSpec((br, wkv), lambda i: (i, 0))
    ts = pl.BlockSpec((br, LANES), lambda i: (i, 0))
    return pl.pallas_call(
        body, name=name, grid=(s_len // br,), in_specs=[qs, kvs, kvs, ts, ts, ts],
        out_specs=pl.BlockSpec((br, wq + 2 * wkv), lambda i: (i, 0)),
        out_shape=jax.ShapeDtypeStruct((s_len, wq + 2 * wkv), BF16),
        compiler_params=_params("parallel"))(dq, dk, dv, *tables)


def _swa_valid(i):
    row = lax.broadcasted_iota(jnp.int32, (BLOCK, 2 * BLOCK), 0)
    col = lax.broadcasted_iota(jnp.int32, (BLOCK, 2 * BLOCK), 1)
    return (col - BLOCK <= row) & (row < col) & ((col >= BLOCK) | (i > 0))


def _swa_specs(wq, wkv):
    q_spec = pl.BlockSpec((BLOCK, wq), lambda i: (i, 0))
    cur = pl.BlockSpec((BLOCK, wkv), lambda i: (i, 0))
    prev = pl.BlockSpec((BLOCK, wkv), lambda i: (jnp.maximum(i - 1, 0), 0))
    sink = pl.BlockSpec(memory_space=pltpu.SMEM)
    return q_spec, cur, prev, sink


def _swa_probs(q_h, k_cat, valid, sink):
    s = lax.dot_general(q_h, k_cat, (((1,), (1,)), ((), ())), preferred_element_type=F32)
    s = jnp.where(valid, s * (SWA_HEAD_DIM ** -0.5), NEG_INF)
    m = jnp.maximum(jnp.max(s, axis=-1, keepdims=True), sink)
    p = jnp.exp(s - m)
    e_sink = jnp.exp(sink - m)
    denom = jnp.sum(p, axis=-1, keepdims=True) + e_sink
    return p / denom, e_sink / denom


def _swa_fwd(q, k, v, sinks, name):
    s_len, wq = q.shape
    wkv = k.shape[1]
    hd = SWA_HEAD_DIM
    q_spec, cur, prev, sink_spec = _swa_specs(wq, wkv)

    def body(q_ref, kc_ref, kp_ref, vc_ref, vp_ref, sink_ref, o_ref):
        valid = _swa_valid(pl.program_id(0))
        for j in range(wkv // hd):
            lanes = slice(j * hd, (j + 1) * hd)
            k_cat = jnp.concatenate([kp_ref[:, lanes], kc_ref[:, lanes]], axis=0)
            v_cat = jnp.concatenate([vp_ref[:, lanes], vc_ref[:, lanes]], axis=0)
            for hh in range(SWA_GROUP):
                h = j * SWA_GROUP + hh
                pn, _ = _swa_probs(q_ref[:, h * hd:(h + 1) * hd], k_cat, valid, sink_ref[h])
                o_ref[:, h * hd:(h + 1) * hd] = jnp.dot(
                    pn.astype(BF16), v_cat, preferred_element_type=F32).astype(BF16)

    return pl.pallas_call(body, name=name, grid=(s_len // BLOCK,),
                          in_specs=[q_spec, cur, prev, cur, prev, sink_spec], out_specs=q_spec,
                          out_shape=jax.ShapeDtypeStruct((s_len, wq), BF16),
                          compiler_params=_params("parallel"))(q, k, k, v, v, sinks)


def _swa_bwd(q, k, v, sinks, do, name):
    s_len, wq = q.shape
    wkv = k.shape[1]
    hd = SWA_HEAD_DIM
    q_spec, cur, prev, sink_spec = _swa_specs(wq, wkv)
    full = pl.BlockSpec((s_len, wkv), lambda i: (0, 0))
    one = pl.BlockSpec((1, LANES), lambda i: (0, 0))
    scale = hd ** -0.5

    def body(q_ref, kc_ref, kp_ref, vc_ref, vp_ref, sink_ref, do_ref, dq_ref, dk_ref, dv_ref, ds_ref):
        i = pl.program_id(0)

        @pl.when(i == 0)
        def _():
            dk_ref[...] = jnp.zeros_like(dk_ref)
            dv_ref[...] = jnp.zeros_like(dv_ref)
            ds_ref[...] = jnp.zeros_like(ds_ref)

        valid = _swa_valid(i)
        lane = lax.broadcasted_iota(jnp.int32, (1, LANES), 1)
        dsink_step = jnp.zeros((1, LANES), F32)
        rows_prev = pl.ds(pl.multiple_of(jnp.maximum(i - 1, 0) * BLOCK, BLOCK), BLOCK)
        rows_cur = pl.ds(pl.multiple_of(i * BLOCK, BLOCK), BLOCK)
        for j in range(wkv // hd):
            lanes = slice(j * hd, (j + 1) * hd)
            k_cat = jnp.concatenate([kp_ref[:, lanes], kc_ref[:, lanes]], axis=0)
            v_cat = jnp.concatenate([vp_ref[:, lanes], vc_ref[:, lanes]], axis=0)
            dk_cat = jnp.zeros((2 * BLOCK, hd), F32)
            dv_cat = jnp.zeros((2 * BLOCK, hd), F32)
            for hh in range(SWA_GROUP):
                h = j * SWA_GROUP + hh
                q_h = q_ref[:, h * hd:(h + 1) * hd]
                do_h = do_ref[:, h * hd:(h + 1) * hd]
                pn, p_sink = _swa_probs(q_h, k_cat, valid, sink_ref[h])
                dpn = lax.dot_general(do_h, v_cat, (((1,), (1,)), ((), ())), preferred_element_type=F32)
                delta = jnp.sum(dpn * pn, axis=-1, keepdims=True)
                ds = (pn * (dpn - delta) * scale).astype(BF16)
                dsink_h = -jnp.sum(p_sink * delta, axis=0, keepdims=True)
                dsink_step = dsink_step + jnp.where(lane == h, dsink_h, 0.0)
                dq_ref[:, h * hd:(h + 1) * hd] = jnp.dot(ds, k_cat, preferred_element_type=F32)
                dk_cat = dk_cat + lax.dot_general(ds, q_h, (((0,), (0,)), ((), ())),
                                                  preferred_element_type=F32)
                dv_cat = dv_cat + lax.dot_general(pn.astype(BF16), do_h, (((0,), (0,)), ((), ())),
                                                  preferred_element_type=F32)
            dk_ref[rows_prev, lanes] += dk_cat[:BLOCK]
            dk_ref[rows_cur, lanes] += dk_cat[BLOCK:]
            dv_ref[rows_prev, lanes] += dv_cat[:BLOCK]
            dv_ref[rows_cur, lanes] += dv_cat[BLOCK:]
        ds_ref[...] += dsink_step

    return pl.pallas_call(
        body, name=name, grid=(s_len // BLOCK,),
        in_specs=[q_spec, cur, prev, cur, prev, sink_spec, q_spec],
        out_specs=[q_spec, full, full, one],
        out_shape=[jax.ShapeDtypeStruct((s_len, wq), F32), jax.ShapeDtypeStruct((s_len, wkv), F32),
                   jax.ShapeDtypeStruct((s_len, wkv), F32), jax.ShapeDtypeStruct((1, LANES), F32)],
        compiler_params=_params("arbitrary"))(q, k, k, v, v, sinks, do)


def _log_sigmoid(x):
    return jnp.minimum(x, 0.0) - jnp.log(1.0 + jnp.exp(-jnp.abs(x)))


def _tri_ones(lower):
    row = lax.broadcasted_iota(jnp.int32, (BLOCK, BLOCK), 0)
    col = lax.broadcasted_iota(jnp.int32, (BLOCK, BLOCK), 1)
    return jnp.where((col <= row) if lower else (col >= row), 1.0, 0.0).astype(F32)


def _fox_decay(proj, bf_row, fl_block, name):
    s_len = proj.shape[0]
    nchunk = s_len // BLOCK

    def body(fl_ref, bf_ref, dec_ref):
        tri = _tri_ones(True)
        carry = jnp.zeros((1, LANES), F32)
        for c in range(nchunk):
            rows = slice(c * BLOCK, (c + 1) * BLOCK)
            log_f = _log_sigmoid(fl_ref[rows, :] + bf_ref[...])
            loc = jnp.dot(tri, log_f, preferred_element_type=F32, precision=lax.Precision.HIGHEST) + carry
            dec_ref[rows, :] = loc
            carry = loc[BLOCK - 1:BLOCK, :]

    return pl.pallas_call(
        body, name=name, grid=(1,),
        in_specs=[pl.BlockSpec((s_len, LANES), lambda i: (0, fl_block)),
                  pl.BlockSpec((1, LANES), lambda i: (0, 0))],
        out_specs=pl.BlockSpec((s_len, LANES), lambda i: (0, 0)),
        out_shape=jax.ShapeDtypeStruct((s_len, LANES), F32),
        compiler_params=_params("arbitrary"))(proj, bf_row)


def _fox_decay_bwd(ddq, ddk, proj, bf_row, fl_block, heads, name):
    s_len = proj.shape[0]
    nchunk = s_len // BLOCK

    def body(ddq_ref, ddk_ref, fl_ref, bf_ref, dfl_ref, dbf_ref):
        tri = _tri_ones(False)
        lane_ok = lax.broadcasted_iota(jnp.int32, (BLOCK, LANES), 1) < heads
        carry = jnp.zeros((1, LANES), F32)
        dbf = jnp.zeros((1, LANES), F32)
        for c in reversed(range(nchunk)):
            rows = slice(c * BLOCK, (c + 1) * BLOCK)
            ddec = ddq_ref[rows, :] + ddk_ref[rows, :]
            dlog = jnp.dot(tri, ddec, preferred_element_type=F32, precision=lax.Precision.HIGHEST) + carry
            carry = dlog[0:1, :]
            dfl = jnp.where(lane_ok, dlog * _sigmoid(-(fl_ref[rows, :] + bf_ref[...])), 0.0)
            dfl_ref[rows, :] = dfl.astype(BF16)
            dbf = dbf + jnp.sum(dfl, axis=0, keepdims=True)
        dbf_ref[...] = dbf

    blk = pl.BlockSpec((s_len, LANES), lambda i: (0, 0))
    one = pl.BlockSpec((1, LANES), lambda i: (0, 0))
    return pl.pallas_call(
        body, name=name, grid=(1,),
        in_specs=[blk, blk, pl.BlockSpec((s_len, LANES), lambda i: (0, fl_block)), one],
        out_specs=[blk, one],
        out_shape=[jax.ShapeDtypeStruct((s_len, LANES), BF16), jax.ShapeDtypeStruct((1, LANES), F32)],
        compiler_params=_params("arbitrary"))(ddq, ddk, proj, bf_row)


def _fox_scores(q, k, decq, deck, i, bq):
    s_len = k.shape[0]
    s = lax.dot_general(q, k, (((1,), (1,)), ((), ())), preferred_element_type=F32)
    s = s * (FOX_HEAD_DIM ** -0.5) + decq - deck
    row = lax.broadcasted_iota(jnp.int32, (bq, s_len), 0) + i * bq
    col = lax.broadcasted_iota(jnp.int32, (bq, s_len), 1)
    s = jnp.where(col <= row, s, NEG_INF)
    p = jnp.exp(s - jnp.max(s, axis=-1, keepdims=True))
    return p / jnp.sum(p, axis=-1, keepdims=True)


def _fox_key_spans(s_len, bq):
    n_span = min(4, s_len // bq)
    return [(j + 1) * (s_len // n_span) for j in range(n_span)]


def _fox_span_of(i, s_len, bq):
    span = s_len // min(4, s_len // bq)
    return ((i * bq) // span + 1) * span


def _fox_specs(s_len, heads, bq):
    hd = FOX_HEAD_DIM
    q_spec = pl.BlockSpec((bq, hd), lambda h, i: (i, h))
    k_spec = pl.BlockSpec((s_len, hd), lambda h, i: (0, heads + h))
    v_spec = pl.BlockSpec((s_len, hd), lambda h, i: (0, 2 * heads + h))
    dq_spec = pl.BlockSpec((None, bq, 1), lambda h, i: (h, i, 0))
    dk_spec = pl.BlockSpec((None, 1, s_len), lambda h, i: (h, 0, 0))
    return q_spec, k_spec, v_spec, dq_spec, dk_spec


def _fox_fwd(proj, decq, deck, heads, name):
    s_len = proj.shape[0]
    bq = _pick(s_len, (256, 128))
    q_spec, k_spec, v_spec, dq_spec, dk_spec = _fox_specs(s_len, heads, bq)

    def body(q_ref, k_ref, v_ref, decq_ref, deck_ref, o_ref):
        i = pl.program_id(1)
        for klen in _fox_key_spans(s_len, bq):
            @pl.when(_fox_span_of(i, s_len, bq) == klen)
            def _(klen=klen):
                pn = _fox_scores(q_ref[...].astype(BF16), k_ref[:klen, :].astype(BF16), decq_ref[...],
                                 deck_ref[:, :klen], i, bq)
                o_ref[...] = jnp.dot(pn.astype(BF16), v_ref[:klen, :].astype(BF16),
                                     preferred_element_type=F32).astype(BF16)

    return pl.pallas_call(body, name=name, grid=(heads, s_len // bq),
                          in_specs=[q_spec, k_spec, v_spec, dq_spec, dk_spec], out_specs=q_spec,
                          out_shape=jax.ShapeDtypeStruct((s_len, heads * FOX_HEAD_DIM), BF16),
                          compiler_params=_params("parallel", "parallel"))(proj, proj, proj, decq, deck)


def _fox_bwd(proj, decq, deck, do, heads, name):
    s_len = proj.shape[0]
    d = heads * FOX_HEAD_DIM
    bq = _pick(s_len, (256, 128))
    q_spec, k_spec, v_spec, dq_spec, dk_spec = _fox_specs(s_len, heads, bq)
    acc_spec = pl.BlockSpec((s_len, FOX_HEAD_DIM), lambda h, i: (0, h))
    scale = FOX_HEAD_DIM ** -0.5

    def body(q_ref, k_ref, v_ref, decq_ref, deck_ref, do_ref, dq_ref, dk_ref, dv_ref, ddq_ref, ddk_ref):
        i = pl.program_id(1)

        @pl.when(i == 0)
        def _():
            dk_ref[...] = jnp.zeros_like(dk_ref)
            dv_ref[...] = jnp.zeros_like(dv_ref)
            ddk_ref[...] = jnp.zeros_like(ddk_ref)

        q = q_ref[...].astype(BF16)
        do_b = do_ref[...]
        for klen in _fox_key_spans(s_len, bq):
            @pl.when(_fox_span_of(i, s_len, bq) == klen)
            def _(klen=klen):
                k = k_ref[:klen, :].astype(BF16)
                pn = _fox_scores(q, k, decq_ref[...], deck_ref[:, :klen], i, bq)
                dpn = lax.dot_general(do_b, v_ref[:klen, :].astype(BF16), (((1,), (1,)), ((), ())),
                                      preferred_element_type=F32)
                ds = pn * (dpn - jnp.sum(dpn * pn, axis=-1, keepdims=True))
                ddq_ref[...] = jnp.sum(ds, axis=-1, keepdims=True)
                ddk_ref[:, :klen] -= jnp.sum(ds, axis=0, keepdims=True)
                ds_b = (ds * scale).astype(BF16)
                dq_ref[...] = jnp.dot(ds_b, k, preferred_element_type=F32).astype(BF16)
                dk_ref[:klen, :] += lax.dot_general(ds_b, q, (((0,), (0,)), ((), ())),
                                                    preferred_element_type=F32)
                dv_ref[:klen, :] += lax.dot_general(pn.astype(BF16), do_b, (((0,), (0,)), ((), ())),
                                                    preferred_element_type=F32)

    return pl.pallas_call(
        body, name=name, grid=(heads, s_len // bq),
        in_specs=[q_spec, k_spec, v_spec, dq_spec, dk_spec, q_spec],
        out_specs=[q_spec, acc_spec, acc_spec, dq_spec, dk_spec],
        out_shape=[jax.ShapeDtypeStruct((s_len, d), BF16), jax.ShapeDtypeStruct((s_len, d), F32),
                   jax.ShapeDtypeStruct((s_len, d), F32), jax.ShapeDtypeStruct((heads, s_len, 1), F32),
                   jax.ShapeDtypeStruct((heads, 1, s_len), F32)],
        compiler_params=_params("parallel", "arbitrary"))(proj, proj, proj, decq, deck, do)


def _place():
    x, y, c = lax.axis_index("x"), lax.axis_index("y"), lax.axis_index("c")
    chips = [(1 - x, y), (x, 1 - y), (1 - x, 1 - y)]
    return x, y, c, chips


def _remote(src, dst, send_sems, recv_sems, idx, to):
    return pltpu.make_async_remote_copy(src_ref=src, dst_ref=dst, send_sem=send_sems.at[idx],
                                        recv_sem=recv_sems.at[idx], device_id=to, device_id_type=MESH)


def _row_chunks(rows, want):
    for k in (want, want // 2, want // 4):
        if k >= 1 and rows % (16 * k) == 0:
            return [(j * (rows // k), rows // k) for j in range(k)]
    return [(0, rows)]


D2D_CHUNKS = 8


def _cast_into_slot(w, me, name, dep=None):
    rows, width = w.shape
    br = _row_block(rows, width, budget=4 << 20)

    def body(me_ref, w_ref, *rest):
        rest[-1][...] = w_ref[...].astype(BF16)

    in_specs = [pl.BlockSpec((br, width), lambda i, me_ref: (i, 0))]
    if dep is not None:
        in_specs.append(pl.BlockSpec(dep.shape, lambda i, me_ref: (0, 0)))
    return pl.pallas_call(
        body, name=name,
        grid_spec=pltpu.PrefetchScalarGridSpec(
            num_scalar_prefetch=1, grid=(rows // br,), in_specs=in_specs,
            out_specs=pl.BlockSpec((None, br, width), lambda i, me_ref: (me_ref[0], i, 0))),
        out_shape=jax.ShapeDtypeStruct((N_CHIPS, rows, width), BF16),
        compiler_params=_params("parallel"))(me, w, *([dep] if dep is not None else []))


def _hbm(arr):
    return pltpu.with_memory_space_constraint(arr, pltpu.HBM)


def _token_shape():
    return jax.ShapeDtypeStruct((8, LANES), F32)


def _add_pair(grad, got, c, name):
    _, half, width = got.shape
    br = _row_block(half, width, itemsize=2, budget=3 << 20)
    nb = half // br

    def body(c_ref, a_ref, b_ref, o_ref):
        o_ref[...] = (a_ref[...].astype(F32) + b_ref[...].astype(F32)).astype(BF16)

    spec = pl.BlockSpec((None, br, width), lambda j, i, c_ref: (j, i, 0))
    mine = pl.BlockSpec((None, br, width), lambda j, i, c_ref: (j, c_ref[0] * nb + i, 0))
    return pl.pallas_call(
        body, name=name,
        grid_spec=pltpu.PrefetchScalarGridSpec(num_scalar_prefetch=1, grid=(N_CHIPS, nb),
                                               in_specs=[mine, spec], out_specs=spec),
        out_shape=jax.ShapeDtypeStruct(got.shape, BF16),
        compiler_params=_params("parallel", "parallel"))(c, grad, got)


def _sum_chips(pair, others, me, name):
    _, rows, width = pair.shape
    br = _row_block(rows, width, itemsize=4, budget=3 << 20)

    def body(me_ref, p_ref, o3_ref, o_ref):
        acc = p_ref[...].astype(F32)
        for r in range(N_CHIPS - 1):
            acc = acc + o3_ref[r].astype(F32)
        o_ref[...] = acc

    return pl.pallas_call(
        body, name=name,
        grid_spec=pltpu.PrefetchScalarGridSpec(
            num_scalar_prefetch=1, grid=(rows // br,),
            in_specs=[pl.BlockSpec((None, br, width), lambda i, me_ref: (me_ref[0], i, 0)),
                      pl.BlockSpec((N_CHIPS - 1, br, width), lambda i, me_ref: (0, i, 0))],
            out_specs=pl.BlockSpec((br, width), lambda i, me_ref: (i, 0))),
        out_shape=jax.ShapeDtypeStruct((rows, width), F32),
        compiler_params=_params("parallel"))(me, pair, others)


def _adamw_halves(w, mine, theirs, m, v, c, name):
    rows, width = w.shape
    half = rows // 2
    br = _row_block(half, width, budget=3 << 19)
    nb = half // br
    c1 = 1.0 - ADAM_B1 ** ADAM_STEP
    c2 = 1.0 - ADAM_B2 ** ADAM_STEP

    def body(c_ref, w_ref, a_ref, b_ref, m_ref, v_ref, g_ref, d_ref, nm_ref, nv_ref):
        grad = jnp.where(pl.program_id(0) == c_ref[0], a_ref[...], b_ref[...])
        new_m = ADAM_B1 * m_ref[...] + (1.0 - ADAM_B1) * grad
        new_v = ADAM_B2 * v_ref[...] + (1.0 - ADAM_B2) * (grad * grad)
        g_ref[...] = grad
        d_ref[...] = -ADAM_LR * ((new_m / c1) / (jnp.sqrt(new_v / c2) + ADAM_EPS) + ADAM_WD * w_ref[...])
        nm_ref[...] = new_m
        nv_ref[...] = new_v

    full = pl.BlockSpec((br, width), lambda h, i, c_ref: (h * nb + i, 0))
    mine_spec = pl.BlockSpec((br, width), lambda h, i, c_ref: (jnp.where(h == c_ref[0], i, 0), 0))
    theirs_spec = pl.BlockSpec((br, width), lambda h, i, c_ref: (jnp.where(h == c_ref[0], 0, i), 0))
    shp = jax.ShapeDtypeStruct(w.shape, F32)
    return pl.pallas_call(
        body, name=name,
        grid_spec=pltpu.PrefetchScalarGridSpec(num_scalar_prefetch=1, grid=(2, nb),
                                               in_specs=[full, mine_spec, theirs_spec, full, full],
                                               out_specs=[full] * 4),
        out_shape=[shp] * 4,
        compiler_params=_params("parallel", "parallel"))(c, w, mine, theirs, m, v)


class _Transfer:
    def __init__(self, n_sems, build):
        self.n_sems, self.build = n_sems, build


def _copies(src_of, dst_of, land_of, rows, chunks, send, recv, idx, to):
    starts = [_remote(src_of(s, z), dst_of(s, z), send, recv, idx, to) for s, z in _row_chunks(rows, chunks)]
    return starts, _remote(src_of(0, rows), land_of(0, rows), send, recv, idx, to)


def _gather_direct(keys, shapes):
    def build(refs, send, recv):
        x, y, c, chips = _place()
        me = 2 * x + y
        out = []
        for t, key in enumerate(keys):
            half = shapes[t][1] // 2
            for r, chip in enumerate(chips[:2]):
                slot = 2 * chip[0] + chip[1]
                out.append(_copies(lambda s, z, key=key, half=half: refs[key].at[me, pl.ds(c * half + s, z)],
                                   lambda s, z, key=key, half=half: refs[key].at[me, pl.ds(c * half + s, z)],
                                   lambda s, z, key=key, half=half, slot=slot: refs[key].at[slot, pl.ds(c * half + s, z)],
                                   half, 1, send, recv, 2 * t + r, (*chip, c)))
        return out
    return _Transfer(2 * len(keys), build)


def _gather_relay(keys, shapes):
    def build(refs, send, recv):
        x, y, c, chips = _place()
        slot_x, slot_y, slot_d = (2 * ch[0] + ch[1] for ch in chips)
        src_slot = slot_y + c * (slot_x - slot_y)
        to = (x ^ (1 - c), y ^ c, c)
        out = []
        for t, key in enumerate(keys):
            half = shapes[t][1] // 2
            out.append(_copies(lambda s, z, key=key, half=half: refs[key].at[src_slot, pl.ds(c * half + s, z)],
                               lambda s, z, key=key, half=half: refs[key].at[src_slot, pl.ds(c * half + s, z)],
                               lambda s, z, key=key, half=half: refs[key].at[slot_d, pl.ds(c * half + s, z)],
                               half, 1, send, recv, t, to))
        return out
    return _Transfer(len(keys), build)


def _gather_pair(keys, shapes):
    def build(refs, send, recv):
        x, y, c, chips = _place()
        out = []
        for t, key in enumerate(keys):
            half = shapes[t][1] // 2
            for r, chip in enumerate(chips):
                slot = 2 * chip[0] + chip[1]
                mine = lambda s, z, key=key, half=half, slot=slot: refs[key].at[slot, pl.ds(c * half + s, z)]
                land = lambda s, z, key=key, half=half, slot=slot: refs[key].at[slot, pl.ds((1 - c) * half + s, z)]
                out.append(_copies(mine, mine, land, half, D2D_CHUNKS, send, recv, 3 * t + r, (x, y, 1 - c)))
        return out
    return _Transfer(3 * len(keys), build)


def _grad_pair(keys, lands, shapes):
    def build(refs, send, recv):
        x, y, c, _ = _place()
        out = []
        for t, (key, land) in enumerate(zip(keys, lands)):
            half = shapes[t][1] // 2
            for j in range(N_CHIPS):
                out.append(_copies(
                    lambda s, z, key=key, half=half, j=j: refs[key].at[j, pl.ds((1 - c) * half + s, z)],
                    lambda s, z, land=land, j=j: refs[land].at[j, pl.ds(s, z)],
                    lambda s, z, land=land, j=j: refs[land].at[j, pl.ds(s, z)],
                    half, 2, send, recv, N_CHIPS * t + j, (x, y, 1 - c)))
        return out
    return _Transfer(N_CHIPS * len(keys), build)


def _grad_chips(keys, lands, shapes):
    def build(refs, send, recv):
        x, y, c, chips = _place()
        out = []
        for t, (key, land) in enumerate(zip(keys, lands)):
            rows = shapes[t][1]
            for r, chip in enumerate(chips):
                slot = 2 * chip[0] + chip[1]
                out.append(_copies(lambda s, z, key=key, slot=slot: refs[key].at[slot, pl.ds(s, z)],
                                   lambda s, z, land=land, r=r: refs[land].at[r, pl.ds(s, z)],
                                   lambda s, z, land=land, r=r: refs[land].at[r, pl.ds(s, z)],
                                   rows, 1, send, recv, 3 * t + r, (*chip, c)))
        return out
    return _Transfer(3 * len(keys), build)


def _grad_join(keys, lands, shapes):
    def build(refs, send, recv):
        x, y, c, _ = _place()
        out = []
        for t, (key, land) in enumerate(zip(keys, lands)):
            out.append(_copies(lambda s, z, key=key: refs[key].at[pl.ds(s, z)],
                               lambda s, z, land=land: refs[land].at[pl.ds(s, z)],
                               lambda s, z, land=land: refs[land].at[pl.ds(s, z)],
                               shapes[t][0], D2D_CHUNKS, send, recv, t, (x, y, 1 - c)))
        return out
    return _Transfer(len(keys), build)


def _comm_call(name, arrays, waits, starts, after):
    keys = list(arrays)
    n, nw, ns = len(keys), len(waits), len(starts)

    def body(*refs):
        in_sems = refs[n:n + 2 * nw]
        base = n + 2 * nw + 1
        out_sems = refs[base:base + 2 * ns]
        bufs = dict(zip(keys, refs[base + 2 * ns:base + 2 * ns + n]))
        token = refs[base + 2 * ns + n]
        for k, (transfer, _, _) in enumerate(waits):
            for _, whole in transfer.build(bufs, in_sems[2 * k], in_sems[2 * k + 1]):
                whole.wait_send()
                whole.wait_recv()
        for k, transfer in enumerate(starts):
            for chunks, _ in transfer.build(bufs, out_sems[2 * k], out_sems[2 * k + 1]):
                for cp in chunks:
                    cp.start()
        token[...] = jnp.zeros_like(token)

    sem_shapes = []
    for transfer in starts:
        sem_shapes += [pltpu.SemaphoreType.DMA((transfer.n_sems,))] * 2
    operands = [_hbm(arrays[k]) for k in keys]
    for _, send, recv in waits:
        operands += [send, recv]
    res = pl.pallas_call(
        body, name=name, in_specs=[HBM] * n + [SEM] * (2 * nw) + [pl.BlockSpec(memory_space=pl.ANY)],
        out_specs=[SEM] * (2 * ns) + [HBM] * n + [pl.BlockSpec(memory_space=pltpu.VMEM)],
        out_shape=sem_shapes + [pltpu.HBM(arrays[k].shape, arrays[k].dtype) for k in keys] + [_token_shape()],
        input_output_aliases={t: 2 * ns + t for t in range(n)},
        compiler_params=pltpu.CompilerParams(has_side_effects=EFFECT),
    )(*operands, after)
    sems = [(res[2 * k], res[2 * k + 1]) for k in range(ns)]
    return dict(zip(keys, res[2 * ns:2 * ns + n])), sems, res[2 * ns + n]


def _device_gather(part_key, all_key, rows):
    def build(refs, send, recv):
        x, y, c, _ = _place()
        me = 4 * x + 2 * y + c
        out = []
        for r in range(1, N_DEV):
            peer = (x ^ (r >> 2), y ^ ((r >> 1) & 1), c ^ (r & 1))
            theirs = 4 * peer[0] + 2 * peer[1] + peer[2]
            out.append(_copies(lambda s, z: refs[part_key].at[pl.ds(s, z)],
                               lambda s, z: refs[all_key].at[me, pl.ds(s, z)],
                               lambda s, z, theirs=theirs: refs[all_key].at[theirs, pl.ds(s, z)],
                               rows, 1, send, recv, r - 1, peer))
        return out
    return _Transfer(N_DEV - 1, build)


def _sum_devices(parts, name):
    _, rows, width = parts.shape
    br = _row_block(rows, width, budget=1 << 19)

    def body(p_ref, o_ref):
        acc = p_ref[0]
        for dev in range(1, N_DEV):
            acc = acc + p_ref[dev]
        o_ref[...] = acc

    return pl.pallas_call(body, name=name, grid=(rows // br,),
                          in_specs=[pl.BlockSpec((N_DEV, br, width), lambda i: (0, i, 0))],
                          out_specs=pl.BlockSpec((br, width), lambda i: (i, 0)),
                          out_shape=jax.ShapeDtypeStruct((rows, width), F32),
                          compiler_params=_params("parallel"))(parts)


INPUT_NAMES = None


def _weight_names():
    names = []
    for i, kind in enumerate(("gmlp", "swa", "fox", "gmlp")):
        p = f"l{i}_"
        names += [p + "ffn1_norm", p + "ffn1_wi", p + "ffn1_wo", p + "mix_norm", p + "mix_win"]
        if kind == "gmlp":
            names += [p + "gmlp_vnorm", p + "gmlp_ws", p + "gmlp_bs"]
        elif kind == "swa":
            names += [p + "swa_sinks"]
        else:
            names += [p + "fox_bf"]
        names += [p + "mix_wout", p + "ffn2_norm", p + "ffn2_wi", p + "ffn2_wo"]
    return names + ["final_norm"]


WEIGHTS = _weight_names()
MIXERS = ("gmlp", "swa", "fox", "gmlp")
BIG = ("ffn1_wi", "ffn1_wo", "mix_win", "mix_wout", "ffn2_wi", "ffn2_wo")


def _ffn_fwd(h, gain, wi, wo, tag, dep=None):
    n = _rms_fwd(h, gain, tag + "_norm", dep=dep)
    z = _matmul(n, wi, name=tag + "_up", out_dtype=BF16)
    a = _swiglu_fwd(z, tag + "_act")
    f, d = wo.shape[0] * wo.shape[1], wo.shape[2]
    out = _matmul(a, wo.reshape(f, d), name=tag + "_down", out_dtype=F32, scale=0.5, resid=h)
    return out, (h, n, z, a)


def _ffn_bwd(dout, saved, gain, wi, wo, tag, dep=None):
    h, n, z, a = saved
    f, d = wo.shape[0] * wo.shape[1], wo.shape[2]
    da = _matmul(dout, wo.reshape(f, d), tb=True, name=tag + "_bdown", out_dtype=BF16, scale=0.5, dep=dep)
    dwo = _matmul(a, dout, ta=True, name=tag + "_gdown", out_dtype=BF16, scale=0.5, dep=dep)
    dz = _swiglu_bwd(z, da, tag + "_bact")
    dn = _matmul(dz, wi, tb=True, name=tag + "_bup", out_dtype=F32)
    dwi = _matmul(n, dz, ta=True, name=tag + "_gup", out_dtype=BF16, out_shards=N_CHIPS)
    dh, dgain = _norm_bwd(h, gain, dn, dout, tag + "_bnorm")
    return dh, dgain, dwi, dwo.reshape(wo.shape)


def _natural(w_sharded, pad_to):
    ns, rows, csh = w_sharded.shape
    nat = jnp.transpose(w_sharded, (1, 0, 2)).reshape(rows, ns * csh)
    extra = (-nat.shape[1]) % pad_to
    return jnp.pad(nat, ((0, 0), (0, extra))) if extra else nat


def _mixer_fwd(kind, h, p, tag, dep=None):
    s_len, d = h.shape
    n = _rms_fwd(h, p["mix_norm"], tag + "_norm", dep=dep)
    wout = p["mix_wout"].reshape(d, d)
    if kind == "gmlp":
        zp = _matmul(n, p["mix_win"], name=tag + "_in", out_dtype=BF16)
        y = _gmlp_fwd(zp, p["gmlp_vnorm"], p["gmlp_ws"], p["gmlp_bs"], tag + "_gate")
        saved = (h, n, zp, y)
    elif kind == "swa":
        qkv = _matmul(n, p["mix_win"], name=tag + "_in", out_dtype=F32)
        q, k, v = _rope_fwd(qkv, tag + "_rope")
        y = _swa_fwd(q, k, v, p["swa_sinks"], tag + "_attn")
        saved = (h, n, q, k, v, y)
    else:
        heads = d // FOX_HEAD_DIM
        win = _natural(p["mix_win"], LANES)
        proj = _matmul(n, win, name=tag + "_in", out_dtype=F32)
        bf_row = jnp.pad(p["fox_bf"], (0, LANES - heads)).reshape(1, LANES)
        dec = _fox_decay(proj, bf_row, 3 * heads, tag + "_decay")
        dec_t = dec[:, :heads].T
        decq, deck = dec_t.reshape(heads, s_len, 1), dec_t.reshape(heads, 1, s_len)
        y = _fox_fwd(proj, decq, deck, heads, tag + "_attn")
        saved = (h, n, win, proj, bf_row, decq, deck, y)
    out = _matmul(y, wout, name=tag + "_out", out_dtype=F32, resid=h)
    return out, saved


def _mixer_bwd(kind, dout, saved, p, tag, dep=None):
    h, n = saved[0], saved[1]
    y = saved[-1]
    s_len, d = h.shape
    wout = p["mix_wout"].reshape(d, d)
    grads = {}
    dy = _matmul(dout, wout, tb=True, name=tag + "_bout", out_dtype=BF16, dep=dep)
    grads["mix_wout"] = _matmul(y, dout, ta=True, name=tag + "_gout", out_dtype=BF16,
                                dep=dep).reshape(p["mix_wout"].shape)
    if kind == "gmlp":
        zp = saved[2]
        dzp, dws, dbst, dvg = _gmlp_bwd(zp, dy, p["gmlp_vnorm"], p["gmlp_ws"], p["gmlp_bs"], tag + "_bgate")
        grads.update(gmlp_ws=dws, gmlp_bs=dbst.T, gmlp_vnorm=dvg.reshape(d))
        dn = _matmul(dzp, p["mix_win"], tb=True, name=tag + "_bin", out_dtype=F32)
        grads["mix_win"] = _matmul(n, dzp, ta=True, name=tag + "_gin", out_dtype=BF16, out_shards=N_CHIPS)
    elif kind == "swa":
        q, k, v = saved[2:5]
        dq, dk, dv, dsinks = _swa_bwd(q, k, v, p["swa_sinks"], dy, tag + "_battn")
        grads["swa_sinks"] = dsinks[0, :p["swa_sinks"].shape[0]]
        dqkv = _rope_bwd(dq, dk, dv, tag + "_brope")
        dn = _matmul(dqkv, p["mix_win"], tb=True, name=tag + "_bin", out_dtype=F32)
        grads["mix_win"] = _matmul(n, dqkv, ta=True, name=tag + "_gin", out_dtype=BF16, out_shards=N_CHIPS)
    else:
        win, proj, bf_row, decq, deck = saved[2:7]
        heads = d // FOX_HEAD_DIM
        dq, dk, dv, ddq, ddk = _fox_bwd(proj, decq, deck, dy, heads, tag + "_battn")
        widen = lambda t: jnp.pad(t.reshape(heads, s_len).T, ((0, 0), (0, LANES - heads)))
        dfl, dbf = _fox_decay_bwd(widen(ddq), widen(ddk), proj, bf_row, 3 * heads, heads, tag + "_bdecay")
        grads["fox_bf"] = dbf[0, :heads]
        dproj = jnp.concatenate([dq, dk.astype(BF16), dv.astype(BF16), dfl], axis=1)
        dn = _matmul(dproj, win, tb=True, name=tag + "_bin", out_dtype=F32)
        dwin = _matmul(n, dproj, ta=True, name=tag + "_gin", out_dtype=BF16)
        ns, rows, csh = p["mix_win"].shape
        grads["mix_win"] = jnp.transpose(dwin[:, :ns * csh].reshape(rows, ns, csh), (1, 0, 2))
    dh, dgain = _norm_bwd(h, p["mix_norm"], dn, dout, tag + "_bnorm")
    grads["mix_norm"] = dgain.reshape(d)
    return dh, grads


def _pack_small(arrays):
    flat = jnp.concatenate([a.reshape(-1).astype(F32) for a in arrays])
    pad = (-flat.shape[0]) % (512 * LANES)
    return jnp.pad(flat, (0, pad)).reshape(-1, LANES)


def _unpack_small(packed, like):
    flat, out, pos = packed.reshape(-1), [], 0
    for a in like:
        out.append(flat[pos:pos + a.size].reshape(a.shape))
        pos += a.size
    return out


def _step(inp):
    x, target = inp["x"][0], inp["loss_target"][0]
    d = x.shape[1]

    core = lax.axis_index("c").astype(jnp.int32).reshape(1)
    chip = (2 * lax.axis_index("x") + lax.axis_index("y")).astype(jnp.int32).reshape(1)

    groups = []
    for i in range(len(MIXERS)):
        groups += [(i, "ffn1", [f"l{i}_ffn1_wi", f"l{i}_ffn1_wo"]), (i, "mix", [f"l{i}_mix_win", f"l{i}_mix_wout"]),
                   (i, "ffn2", [f"l{i}_ffn2_wi", f"l{i}_ffn2_wo"])]

    def layer_params(i, full):
        p = {nm[len(f"l{i}_"):]: inp[nm] for nm in WEIGHTS if nm.startswith(f"l{i}_")}
        p.update({nm[len(f"l{i}_"):]: w for nm, w in full.items()})
        return p

    n_groups = len(groups)
    valid = lambda k: 0 <= k < n_groups

    bufs = {}
    full_shapes = lambda names: [(N_CHIPS, *inp[nm].shape) for nm in names]
    direct = [_gather_direct(names, full_shapes(names)) for _, _, names in groups]
    relay = [_gather_relay(names, full_shapes(names)) for _, _, names in groups]
    to_pair = [_gather_pair(names, full_shapes(names)) for _, _, names in groups]
    sems = {}

    def cast_groups(which, dep):
        for g in which:
            for nm in groups[g][2]:
                bufs[nm] = _cast_into_slot(inp[nm], chip, nm + "_cast", dep=dep)

    def gather_step(step, after):
        waits, starts, tags, keys = [], [], [], []
        for kind, transfers, g, begin in (("pair", to_pair, step, False), ("relay", relay, step + 1, False),
                                          ("pair", to_pair, step + 1, True), ("direct", direct, step + 2, False),
                                          ("relay", relay, step + 2, True), ("direct", direct, step + 3, True)):
            if not valid(g):
                continue
            keys += [nm for nm in groups[g][2] if nm not in keys]
            if begin:
                starts.append(transfers[g])
                tags.append((kind, g))
            else:
                waits.append((transfers[g], *sems.pop((kind, g))))
        new, started, token = _comm_call(f"gather_step{step + 3}", {k: bufs[k] for k in keys}, waits, starts, after)
        bufs.update(new)
        sems.update(zip(tags, started))
        return token

    cast_groups([0], None)
    token = gather_step(-3, x)
    cast_groups(range(1, 6), token)
    token = gather_step(-2, bufs[groups[5][2][-1]])
    cast_groups(range(6, n_groups), token)
    token = gather_step(-1, bufs[groups[-1][2][-1]])
    h, saved, fulls = x, [], []
    for g, (i, part, names) in enumerate(groups):
        token = gather_step(g, h)
        full = {nm: bufs[nm] for nm in names}
        p = layer_params(i, full)
        if part == "mix":
            h, s = _mixer_fwd(MIXERS[i], h, p, f"l{i}_mix", dep=token)
        else:
            h, s = _ffn_fwd(h, p[part + "_norm"], p[part + "_wi"], p[part + "_wo"], f"l{i}_{part}", dep=token)
        saved.append(s)
        fulls.append(full)
    loss_part, dh, dfinal = _loss_head(h, inp["final_norm"], target, "loss_head")
    loss = lax.psum(loss_part, ("x", "y", "c"))

    small_grads = {"final_norm": dfinal.reshape(d)}
    outs = {}
    work = {}
    stage = {}
    small_names = [nm for nm in WEIGHTS if nm.split("_", 1)[1] not in BIG]
    like = [inp[nm] for nm in small_names]

    def comm(name, transfers_to_wait, transfers_to_start, after):
        waits = [(stage[k], *sems.pop(k)) for k in transfers_to_wait if valid(k[1])]
        starts = [k for k in transfers_to_start if valid(k[1])]
        if not waits and not starts:
            return after
        keys = []
        for k in [k for k in transfers_to_wait if valid(k[1])] + starts:
            keys += [key for key in stage_keys[k] if key not in keys]
        new, started, token = _comm_call(name, {k: work[k] for k in keys}, waits, [stage[k] for k in starts], after)
        work.update(new)
        sems.update(zip(starts, started))
        return token

    stage_keys = {}

    def reduce_step(g, after):
        token = comm(f"rs_pair_step{n_groups - 1 - g}", [("pair", g + 1)], [("pair", g)], after)
        if valid(g + 1):
            names = groups[g + 1][2]
            for nm in names:
                work[nm + "#sum"] = _add_pair(work[nm + "#grad"], work[nm + "#got"], core, nm + "_rs_add")
                work[nm + "#others"] = lax.empty((N_CHIPS - 1, *work[nm + "#sum"].shape[1:]), BF16)
            shapes = [work[nm + "#sum"].shape for nm in names]
            stage[("chips", g + 1)] = _grad_chips([nm + "#sum" for nm in names], [nm + "#others" for nm in names], shapes)
            stage_keys[("chips", g + 1)] = [nm + sfx for nm in names for sfx in ("#sum", "#others")]
        token = comm(f"rs_chips_step{n_groups - 1 - g}", [("chips", g + 2)], [("chips", g + 1)], token)
        if valid(g + 2):
            names = groups[g + 2][2]
            for nm in names:
                work[nm + "#half"] = _sum_chips(work[nm + "#sum"], work[nm + "#others"], chip, nm + "_rs_sum")
                work[nm + "#theirs"] = lax.empty(work[nm + "#half"].shape, F32)
            shapes = [work[nm + "#half"].shape for nm in names]
            stage[("join", g + 2)] = _grad_join([nm + "#half" for nm in names], [nm + "#theirs" for nm in names], shapes)
            stage_keys[("join", g + 2)] = [nm + sfx for nm in names for sfx in ("#half", "#theirs")]
        token = comm(f"rs_join_step{n_groups - 1 - g}", [("join", g + 3)], [("join", g + 2)], token)
        if valid(g + 3):
            for nm in groups[g + 3][2]:
                outs[nm] = tuple(_adamw_halves(inp[nm], work[nm + "#half"], work[nm + "#theirs"], inp["m_" + nm],
                                               inp["v_" + nm], core, nm + "_adamw"))
        return token

    dep = None
    for g in reversed(range(n_groups)):
        i, part, names = groups[g]
        p = layer_params(i, fulls[g])
        if part == "mix":
            dh, mg = _mixer_bwd(MIXERS[i], dh, saved[g], p, f"l{i}_mix", dep=dep)
            grads = [mg.pop("mix_win"), mg.pop("mix_wout")]
            small_grads.update({f"l{i}_{key}": val for key, val in mg.items()})
        else:
            dh, g_norm, dwi, dwo = _ffn_bwd(dh, saved[g], p[part + "_norm"], p[part + "_wi"], p[part + "_wo"],
                                            f"l{i}_{part}", dep=dep)
            small_grads[f"l{i}_{part}_norm"] = g_norm.reshape(d)
            grads = [dwi, dwo]
        for nm, gr in zip(names, grads):
            work[nm + "#grad"] = gr
            work[nm + "#got"] = lax.empty((gr.shape[0], gr.shape[1] // 2, gr.shape[2]), BF16)
        stage[("pair", g)] = _grad_pair([nm + "#grad" for nm in names], [nm + "#got" for nm in names],
                                        [gr.shape for gr in grads])
        stage_keys[("pair", g)] = [nm + sfx for nm in names for sfx in ("#grad", "#got")]
        dep = reduce_step(g, dh)

    part = _pack_small([small_grads[nm] for nm in small_names])
    device = 4 * lax.axis_index("x") + 2 * lax.axis_index("y") + lax.axis_index("c")
    work["small#part"] = part
    work["small#all"] = lax.dynamic_update_slice(jnp.zeros((N_DEV, *part.shape), F32), part[None], (device, 0, 0))
    stage[("small", 0)] = _device_gather("small#part", "small#all", part.shape[0])
    stage_keys[("small", 0)] = ["small#part", "small#all"]
    dep = comm("small_gather_start", [], [("small", 0)], dep)
    dep = reduce_step(-1, dep)
    dep = reduce_step(-2, dep)
    dep = comm("small_gather_wait", [("small", 0)], [], dep)
    total = _sum_devices(work["small#all"], "small_sum")
    upd = _adamw(_pack_small(like), total, _pack_small([inp["m_" + nm] for nm in small_names]),
                 _pack_small([inp["v_" + nm] for nm in small_names]), "small_adamw")
    unpacked = [_unpack_small(t, like) for t in (total, *upd)]
    for k, nm in enumerate(small_names):
        outs[nm] = tuple(u[k] for u in unpacked)
    reduce_step(-3, upd[0])

    result = [loss, dh[None]]
    for part in range(4):
        result += [outs[nm][part] for nm in WEIGHTS]
    return tuple(result)


def kernel(x, l0_ffn1_norm, l0_ffn1_wi, l0_ffn1_wo, l0_mix_norm, l0_mix_win, l0_gmlp_vnorm, l0_gmlp_ws, l0_gmlp_bs, l0_mix_wout, l0_ffn2_norm, l0_ffn2_wi, l0_ffn2_wo, l1_ffn1_norm, l1_ffn1_wi, l1_ffn1_wo, l1_mix_norm, l1_mix_win, l1_swa_sinks, l1_mix_wout, l1_ffn2_norm, l1_ffn2_wi, l1_ffn2_wo, l2_ffn1_norm, l2_ffn1_wi, l2_ffn1_wo, l2_mix_norm, l2_mix_win, l2_fox_bf, l2_mix_wout, l2_ffn2_norm, l2_ffn2_wi, l2_ffn2_wo, l3_ffn1_norm, l3_ffn1_wi, l3_ffn1_wo, l3_mix_norm, l3_mix_win, l3_gmlp_vnorm, l3_gmlp_ws, l3_gmlp_bs, l3_mix_wout, l3_ffn2_norm, l3_ffn2_wi, l3_ffn2_wo, final_norm, loss_target, m_l0_ffn1_norm, m_l0_ffn1_wi, m_l0_ffn1_wo, m_l0_mix_norm, m_l0_mix_win, m_l0_gmlp_vnorm, m_l0_gmlp_ws, m_l0_gmlp_bs, m_l0_mix_wout, m_l0_ffn2_norm, m_l0_ffn2_wi, m_l0_ffn2_wo, m_l1_ffn1_norm, m_l1_ffn1_wi, m_l1_ffn1_wo, m_l1_mix_norm, m_l1_mix_win, m_l1_swa_sinks, m_l1_mix_wout, m_l1_ffn2_norm, m_l1_ffn2_wi, m_l1_ffn2_wo, m_l2_ffn1_norm, m_l2_ffn1_wi, m_l2_ffn1_wo, m_l2_mix_norm, m_l2_mix_win, m_l2_fox_bf, m_l2_mix_wout, m_l2_ffn2_norm, m_l2_ffn2_wi, m_l2_ffn2_wo, m_l3_ffn1_norm, m_l3_ffn1_wi, m_l3_ffn1_wo, m_l3_mix_norm, m_l3_mix_win, m_l3_gmlp_vnorm, m_l3_gmlp_ws, m_l3_gmlp_bs, m_l3_mix_wout, m_l3_ffn2_norm, m_l3_ffn2_wi, m_l3_ffn2_wo, m_final_norm, v_l0_ffn1_norm, v_l0_ffn1_wi, v_l0_ffn1_wo, v_l0_mix_norm, v_l0_mix_win, v_l0_gmlp_vnorm, v_l0_gmlp_ws, v_l0_gmlp_bs, v_l0_mix_wout, v_l0_ffn2_norm, v_l0_ffn2_wi, v_l0_ffn2_wo, v_l1_ffn1_norm, v_l1_ffn1_wi, v_l1_ffn1_wo, v_l1_mix_norm, v_l1_mix_win, v_l1_swa_sinks, v_l1_mix_wout, v_l1_ffn2_norm, v_l1_ffn2_wi, v_l1_ffn2_wo, v_l2_ffn1_norm, v_l2_ffn1_wi, v_l2_ffn1_wo, v_l2_mix_norm, v_l2_mix_win, v_l2_fox_bf, v_l2_mix_wout, v_l2_ffn2_norm, v_l2_ffn2_wi, v_l2_ffn2_wo, v_l3_ffn1_norm, v_l3_ffn1_wi, v_l3_ffn1_wo, v_l3_mix_norm, v_l3_mix_win, v_l3_gmlp_vnorm, v_l3_gmlp_ws, v_l3_gmlp_bs, v_l3_mix_wout, v_l3_ffn2_norm, v_l3_ffn2_wi, v_l3_ffn2_wo, v_final_norm):
    return _step(dict(locals()))
```

```python
import functools
import math

import jax
import jax.numpy as jnp
from jax import lax
from jax.experimental import pallas as pl
from jax.experimental.pallas import tpu as pltpu

F32 = jnp.float32
BF16 = jnp.bfloat16

NORM_EPS = 1e-5
NEG_INF = -1e30
BLOCK = 128
GMLP_GROUPS = 16
SWA_HEAD_DIM = 64
SWA_GROUP = 8
ROPE_DIM = SWA_HEAD_DIM // 4
ROPE_THETA = 500000.0
FOX_HEAD_DIM = 128
ADAM_LR = 0.001
ADAM_B1 = 0.9
ADAM_B2 = 0.999
ADAM_EPS = 1e-08
ADAM_WD = 0.01
ADAM_STEP = 10
N_CHIPS = 4
N_DEV = 8
LANES = 128
VMEM_LIMIT = 56 * 1024 * 1024
MESH = pl.DeviceIdType.MESH
HBM = pl.BlockSpec(memory_space=pltpu.HBM)
SEM = pl.BlockSpec(memory_space=pltpu.SEMAPHORE)
EFFECT = pltpu.SideEffectType.DATAFLOW_SIDE_EFFECTING

MM_TILES = (1024, 1408, 896, 640, 512, 384, 256, 128)
K_TILES = (2816,) + MM_TILES


def _pick(n, prefs):
    for p in prefs:
        if p <= n and n % p == 0:
            return p
    return n


def _params(*sem):
    return pltpu.CompilerParams(dimension_semantics=sem or None, vmem_limit_bytes=VMEM_LIMIT)


def _cols(arr):
    return arr.shape[-1] * (arr.shape[0] if arr.ndim == 3 else 1)


def _mat_spec(arr, rb, cb, ridx, cidx):
    if arr.ndim == 2:
        return pl.BlockSpec((rb, cb), lambda j, i, k: (ridx(j, i, k), cidx(j, i, k)))
    per = arr.shape[2] // cb
    return pl.BlockSpec((None, rb, cb),
                        lambda j, i, k: (cidx(j, i, k) // per, ridx(j, i, k), cidx(j, i, k) % per))


def _matmul(a, b, *, name, out_dtype, ta=False, tb=False, out_shards=1, scale=1.0, resid=None, dep=None):
    m_dim, k_dim = (a.shape[1], a.shape[0]) if ta else a.shape
    n_dim = b.shape[-2] if tb else _cols(b)
    assert k_dim == (_cols(b) if tb else b.shape[-2]), (a.shape, b.shape, ta, tb)
    n_unit = n_dim // out_shards
    if b.ndim == 3 and not tb:
        n_unit = math.gcd(n_unit, b.shape[2])
    k_unit = b.shape[2] if (b.ndim == 3 and tb) else k_dim
    bm = _pick(m_dim, MM_TILES)
    bn = _pick(n_unit, MM_TILES)
    bk = k_unit if k_unit <= 2048 else _pick(k_unit, K_TILES)
    nk = k_dim // bk
    i_of, j_of, k_of = (lambda j, i, k: i), (lambda j, i, k: j), (lambda j, i, k: k)
    a_spec = _mat_spec(a, bk, bm, k_of, i_of) if ta else _mat_spec(a, bm, bk, i_of, k_of)
    b_spec = _mat_spec(b, bn, bk, j_of, k_of) if tb else _mat_spec(b, bk, bn, k_of, j_of)
    out_shape = (m_dim, n_dim) if out_shards == 1 else (out_shards, m_dim, n_dim // out_shards)
    out = jax.ShapeDtypeStruct(out_shape, out_dtype)
    o_spec = _mat_spec(out, bm, bn, i_of, j_of)
    dims = (((0 if ta else 1,), (1 if tb else 0,)), ((), ()))
    operands, in_specs = [a, b], [a_spec, b_spec]
    if resid is not None:
        operands.append(resid)
        in_specs.append(_mat_spec(resid, bm, bn, i_of, j_of))
    if dep is not None:
        operands.append(dep)
        in_specs.append(pl.BlockSpec(dep.shape, lambda j, i, k: (0, 0)))
    n_in = len(operands)

    def body(*refs):
        a_ref, b_ref = refs[0], refs[1]
        r_ref = refs[2] if resid is not None else None
        o_ref = refs[n_in]
        part = lax.dot_general(a_ref[...].astype(BF16), b_ref[...].astype(BF16), dims,
                               preferred_element_type=F32)

        def finish(acc):
            val = acc * scale if scale != 1.0 else acc
            if r_ref is not None:
                val = r_ref[...] + val
            o_ref[...] = val.astype(o_ref.dtype)

        if nk == 1:
            finish(part)
        else:
            acc_ref = refs[-1]
            k = pl.program_id(2)

            @pl.when(k == 0)
            def _():
                acc_ref[...] = part

            @pl.when(k > 0)
            def _():
                acc_ref[...] += part

            @pl.when(k == nk - 1)
            def _():
                finish(acc_ref[...])

    return pl.pallas_call(
        body, name=name, grid=(n_dim // bn, m_dim // bm, nk),
        in_specs=in_specs, out_specs=o_spec, out_shape=out,
        scratch_shapes=[pltpu.VMEM((bm, bn), F32)] if nk > 1 else [],
        compiler_params=_params("parallel", "parallel", "arbitrary"),
    )(*operands)


def _row_block(rows, width, itemsize=4, budget=2 << 20):
    best = None
    for br in range(16, rows + 1, 16):
        if rows % br == 0 and br * width * itemsize <= budget:
            best = br
    return best or rows


def _rms_fwd(h, g, name, dep=None):
    s_len, d = h.shape
    br = _row_block(s_len, d)

    def body(h_ref, g_ref, *rest):
        o_ref = rest[-1]
        x = h_ref[...]
        r = lax.rsqrt(jnp.mean(x * x, axis=-1, keepdims=True) + NORM_EPS)
        o_ref[...] = (x * r * g_ref[...]).astype(BF16)

    spec = pl.BlockSpec((br, d), lambda i: (i, 0))
    operands = [h, g.reshape(1, d)] + ([dep] if dep is not None else [])
    in_specs = [spec, pl.BlockSpec((1, d), lambda i: (0, 0))]
    if dep is not None:
        in_specs.append(pl.BlockSpec(dep.shape, lambda i: (0, 0)))
    return pl.pallas_call(body, name=name, grid=(s_len // br,), in_specs=in_specs, out_specs=spec,
                          out_shape=jax.ShapeDtypeStruct((s_len, d), BF16),
                          compiler_params=_params("parallel"))(*operands)


def _rms_bwd_rows(x, g, dn):
    r = lax.rsqrt(jnp.mean(x * x, axis=-1, keepdims=True) + NORM_EPS)
    xhat = x * r
    gdn = dn * g
    dx = r * (gdn - xhat * jnp.mean(gdn * xhat, axis=-1, keepdims=True))
    return dx, dn * xhat


def _norm_bwd(h, g, dn, dres, name):
    s_len, d = h.shape
    br = _row_block(s_len, d, budget=1 << 20)

    def body(h_ref, g_ref, dn_ref, dres_ref, dh_ref, dg_ref):
        dx, dg_rows = _rms_bwd_rows(h_ref[...], g_ref[...], dn_ref[...].astype(F32))
        dh_ref[...] = dres_ref[...] + dx

        @pl.when(pl.program_id(0) == 0)
        def _():
            dg_ref[...] = jnp.zeros_like(dg_ref)

        dg_ref[...] += jnp.sum(dg_rows, axis=0, keepdims=True)

    spec = pl.BlockSpec((br, d), lambda i: (i, 0))
    vec = pl.BlockSpec((1, d), lambda i: (0, 0))
    return pl.pallas_call(body, name=name, grid=(s_len // br,),
                          in_specs=[spec, vec, spec, spec], out_specs=[spec, vec],
                          out_shape=[jax.ShapeDtypeStruct((s_len, d), F32),
                                     jax.ShapeDtypeStruct((1, d), F32)],
                          compiler_params=_params("arbitrary"))(h, g.reshape(1, d), dn, dres)


def _sigmoid(x):
    return 0.5 * (1.0 + jnp.tanh(0.5 * x))


def _swiglu_fwd(z, name):
    s_len, f2 = z.shape
    f = f2 // 2
    br = _row_block(s_len, f2, itemsize=2, budget=6 << 20)
    fc = _pick(f, MM_TILES)

    def body(z_ref, a_ref):
        for c0 in range(0, f, fc):
            gate = z_ref[:, c0:c0 + fc].astype(F32)
            up = z_ref[:, f + c0:f + c0 + fc].astype(F32)
            a_ref[:, c0:c0 + fc] = (gate * _sigmoid(gate) * up).astype(BF16)

    return pl.pallas_call(body, name=name, grid=(s_len // br,),
                          in_specs=[pl.BlockSpec((br, f2), lambda i: (i, 0))],
                          out_specs=pl.BlockSpec((br, f), lambda i: (i, 0)),
                          out_shape=jax.ShapeDtypeStruct((s_len, f), BF16),
                          compiler_params=_params("parallel"))(z)


def _swiglu_bwd(z, da, name):
    s_len, f2 = z.shape
    f = f2 // 2
    br = _row_block(s_len, f2, itemsize=2, budget=6 << 20)
    fc = _pick(f, MM_TILES)

    def body(z_ref, da_ref, dz_ref):
        for c0 in range(0, f, fc):
            gate = z_ref[:, c0:c0 + fc].astype(F32)
            up = z_ref[:, f + c0:f + c0 + fc].astype(F32)
            d = da_ref[:, c0:c0 + fc].astype(F32)
            sig = _sigmoid(gate)
            dz_ref[:, c0:c0 + fc] = (d * up * (sig * (1.0 + gate * (1.0 - sig)))).astype(BF16)
            dz_ref[:, f + c0:f + c0 + fc] = (d * gate * sig).astype(BF16)

    return pl.pallas_call(body, name=name, grid=(s_len // br,),
                          in_specs=[pl.BlockSpec((br, f2), lambda i: (i, 0)),
                                    pl.BlockSpec((br, f), lambda i: (i, 0))],
                          out_specs=pl.BlockSpec((br, f2), lambda i: (i, 0)),
                          out_shape=jax.ShapeDtypeStruct((s_len, f2), BF16),
                          compiler_params=_params("parallel"))(z, da)


def _loss_head(h, g, target, name):
    s_len, d = h.shape
    br = _row_block(s_len, d, budget=1 << 20)

    def body(h_ref, g_ref, t_ref, loss_ref, dh_ref, dg_ref):
        x = h_ref[...]
        gain = g_ref[...]
        r = lax.rsqrt(jnp.mean(x * x, axis=-1, keepdims=True) + NORM_EPS)
        err = x * r * gain - t_ref[...]
        part = 0.5 * jnp.sum(jnp.mean(err * err, axis=-1, keepdims=True), axis=0, keepdims=True)
        dx, dg_rows = _rms_bwd_rows(x, gain, err * (1.0 / d))
        dh_ref[...] = dx

        @pl.when(pl.program_id(0) == 0)
        def _():
            dg_ref[...] = jnp.zeros_like(dg_ref)
            loss_ref[...] = jnp.zeros_like(loss_ref)

        dg_ref[...] += jnp.sum(dg_rows, axis=0, keepdims=True)
        loss_ref[...] += jnp.broadcast_to(part, loss_ref.shape)

    spec = pl.BlockSpec((br, d), lambda i: (i, 0))
    vec = pl.BlockSpec((1, d), lambda i: (0, 0))
    one = pl.BlockSpec((1, LANES), lambda i: (0, 0))
    loss, dh, dg = pl.pallas_call(
        body, name=name, grid=(s_len // br,), in_specs=[spec, vec, spec],
        out_specs=[one, spec, vec],
        out_shape=[jax.ShapeDtypeStruct((1, LANES), F32), jax.ShapeDtypeStruct((s_len, d), F32),
                   jax.ShapeDtypeStruct((1, d), F32)],
        compiler_params=_params("arbitrary"))(h, g.reshape(1, d), target)
    return loss[0, 0], dh, dg


def _adamw(w, g, m, v, name):
    rows, width = w.shape
    br = _row_block(rows, width, budget=1 << 20)
    c1 = 1.0 - ADAM_B1 ** ADAM_STEP
    c2 = 1.0 - ADAM_B2 ** ADAM_STEP

    def body(w_ref, g_ref, m_ref, v_ref, d_ref, nm_ref, nv_ref):
        grad = g_ref[...]
        new_m = ADAM_B1 * m_ref[...] + (1.0 - ADAM_B1) * grad
        new_v = ADAM_B2 * v_ref[...] + (1.0 - ADAM_B2) * (grad * grad)
        d_ref[...] = -ADAM_LR * ((new_m / c1) / (jnp.sqrt(new_v / c2) + ADAM_EPS) + ADAM_WD * w_ref[...])
        nm_ref[...] = new_m
        nv_ref[...] = new_v

    spec = pl.BlockSpec((br, width), lambda i: (i, 0))
    shp = jax.ShapeDtypeStruct(w.shape, F32)
    return pl.pallas_call(body, name=name, grid=(rows // br,), in_specs=[spec] * 4,
                          out_specs=[spec] * 3, out_shape=[shp] * 3,
                          compiler_params=_params("parallel"))(w, g, m, v)


def _gelu(x):
    return 0.5 * x * (1.0 + lax.erf(x * (2.0 ** -0.5)))


def _gelu_grad(x):
    return 0.5 * (1.0 + lax.erf(x * (2.0 ** -0.5))) + x * jnp.exp(-0.5 * x * x) * ((2.0 * math.pi) ** -0.5)


def _tril_mask():
    row = lax.broadcasted_iota(jnp.int32, (BLOCK, BLOCK), 0)
    col = lax.broadcasted_iota(jnp.int32, (BLOCK, BLOCK), 1)
    return col <= row


def _gmlp_specs(s_len, d):
    gw = d // GMLP_GROUPS
    zp = pl.BlockSpec((BLOCK, 2 * d), lambda i: (i, 0))
    row = pl.BlockSpec((BLOCK, d), lambda i: (i, 0))
    vec = pl.BlockSpec((1, d), lambda i: (0, 0))
    ws = pl.BlockSpec((GMLP_GROUPS, BLOCK, BLOCK), lambda i: (0, 0, 0))
    bst = pl.BlockSpec((BLOCK, GMLP_GROUPS), lambda i: (0, 0))
    return gw, zp, row, vec, ws, bst


def _gmlp_fwd(zp, vgain, ws, bs, name):
    s_len, d2 = zp.shape
    d = d2 // 2
    gw, zp_spec, row_spec, vec_spec, ws_spec, bst_spec = _gmlp_specs(s_len, d)

    def body(zp_ref, vg_ref, ws_ref, bst_ref, y_ref):
        u = _gelu(zp_ref[:, :d].astype(F32))
        vv = _gelu(zp_ref[:, d:].astype(F32))
        r = lax.rsqrt(jnp.mean(vv * vv, axis=-1, keepdims=True) + NORM_EPS)
        vn = (vv * r * vg_ref[...]).astype(BF16)
        mask = _tril_mask()
        for g in range(GMLP_GROUPS):
            cols = slice(g * gw, (g + 1) * gw)
            wg = jnp.where(mask, ws_ref[g], 0.0).astype(BF16)
            mixed = jnp.dot(wg, vn[:, cols], preferred_element_type=F32) + bst_ref[:, g:g + 1]
            y_ref[:, cols] = (u[:, cols] * mixed).astype(BF16)

    return pl.pallas_call(body, name=name, grid=(s_len // BLOCK,),
                          in_specs=[zp_spec, vec_spec, ws_spec, bst_spec], out_specs=row_spec,
                          out_shape=jax.ShapeDtypeStruct((s_len, d), BF16),
                          compiler_params=_params("parallel"))(zp, vgain.reshape(1, d), ws, bs.T)


def _gmlp_bwd(zp, dy, vgain, ws, bs, name):
    s_len, d2 = zp.shape
    d = d2 // 2
    gw, zp_spec, row_spec, vec_spec, ws_spec, bst_spec = _gmlp_specs(s_len, d)

    def body(zp_ref, dy_ref, vg_ref, ws_ref, bst_ref, dzp_ref, dws_ref, dbst_ref, dvg_ref, dvn_ref):
        @pl.when(pl.program_id(0) == 0)
        def _():
            dws_ref[...] = jnp.zeros_like(dws_ref)
            dbst_ref[...] = jnp.zeros_like(dbst_ref)
            dvg_ref[...] = jnp.zeros_like(dvg_ref)

        zu = zp_ref[:, :d].astype(F32)
        zv = zp_ref[:, d:].astype(F32)
        u = _gelu(zu)
        vv = _gelu(zv)
        r = lax.rsqrt(jnp.mean(vv * vv, axis=-1, keepdims=True) + NORM_EPS)
        vhat = vv * r
        gain = vg_ref[...]
        vn = (vhat * gain).astype(BF16)
        dyf = dy_ref[...].astype(F32)
        dmixed = dyf * u
        dmixed_b = dmixed.astype(BF16)
        mask = _tril_mask()
        lane = lax.broadcasted_iota(jnp.int32, (BLOCK, GMLP_GROUPS), 1)
        dbs_step = jnp.zeros((BLOCK, GMLP_GROUPS), F32)
        for g in range(GMLP_GROUPS):
            cols = slice(g * gw, (g + 1) * gw)
            wg = jnp.where(mask, ws_ref[g], 0.0).astype(BF16)
            mixed = jnp.dot(wg, vn[:, cols], preferred_element_type=F32) + bst_ref[:, g:g + 1]
            dzp_ref[:, cols] = (dyf[:, cols] * mixed * _gelu_grad(zu[:, cols])).astype(BF16)
            dm = dmixed_b[:, cols]
            dw = lax.dot_general(dm, vn[:, cols], (((1,), (1,)), ((), ())), preferred_element_type=F32)
            dws_ref[g] += jnp.where(mask, dw, 0.0)
            dbs_step = dbs_step + jnp.where(lane == g, jnp.sum(dmixed[:, cols], axis=-1, keepdims=True), 0.0)
            dvn_ref[:, cols] = lax.dot_general(wg, dm, (((0,), (0,)), ((), ())), preferred_element_type=F32)
        dbst_ref[...] += dbs_step
        dvn = dvn_ref[...]
        dvg_ref[...] += jnp.sum(dvn * vhat, axis=0, keepdims=True)
        dvhat = dvn * gain
        dvv = r * (dvhat - vhat * jnp.mean(dvhat * vhat, axis=-1, keepdims=True))
        dzp_ref[:, d:] = (dvv * _gelu_grad(zv)).astype(BF16)

    return pl.pallas_call(
        body, name=name, grid=(s_len // BLOCK,),
        in_specs=[zp_spec, row_spec, vec_spec, ws_spec, bst_spec],
        out_specs=[zp_spec, ws_spec, bst_spec, vec_spec],
        out_shape=[jax.ShapeDtypeStruct((s_len, d2), BF16), jax.ShapeDtypeStruct(ws.shape, F32),
                   jax.ShapeDtypeStruct((BLOCK, GMLP_GROUPS), F32), jax.ShapeDtypeStruct((1, d), F32)],
        scratch_shapes=[pltpu.VMEM((BLOCK, d), F32)],
        compiler_params=_params("arbitrary"))(zp, dy, vgain.reshape(1, d), ws, bs.T)


def _rope_tables(s_len, sign):
    half = ROPE_DIM // 2
    inv_freq = ROPE_THETA ** (-(jnp.arange(half, dtype=F32) * 2.0 / ROPE_DIM))
    ang = jnp.arange(s_len, dtype=F32)[:, None] * inv_freq[None, :]
    cos, sin = jnp.cos(ang), jnp.sin(ang) * sign
    pad = jnp.zeros((s_len, SWA_HEAD_DIM - ROPE_DIM), F32)
    zero = jnp.zeros_like(sin)
    cos_t = jnp.concatenate([cos, cos, pad + 1.0], axis=1)
    sin_up = jnp.concatenate([-sin, zero, pad], axis=1)
    sin_dn = jnp.concatenate([zero, sin, pad], axis=1)
    return [jnp.tile(t, (1, LANES // SWA_HEAD_DIM)) for t in (cos_t, sin_up, sin_dn)]


def _rotate(x, cos_t, sin_up, sin_dn):
    width = x.shape[-1]
    half = ROPE_DIM // 2
    reps = width // cos_t.shape[-1]
    if reps > 1:
        cos_t, sin_up, sin_dn = (jnp.tile(t, (1, reps)) for t in (cos_t, sin_up, sin_dn))
    elif reps == 0:
        cos_t, sin_up, sin_dn = (t[:, :width] for t in (cos_t, sin_up, sin_dn))
    return x * cos_t + pltpu.roll(x, width - half, 1) * sin_up + pltpu.roll(x, half, 1) * sin_dn


def _rope_fwd(qkv, name):
    s_len, total = qkv.shape
    wkv = total // (SWA_GROUP + 2)
    wq = SWA_GROUP * wkv
    br = _row_block(s_len, total, budget=2 << 20)
    tables = _rope_tables(s_len, 1.0)

    def body(q_ref, k_ref, v_ref, c_ref, su_ref, sd_ref, qo_ref, ko_ref, vo_ref):
        t = (c_ref[...], su_ref[...], sd_ref[...])
        qo_ref[...] = _rotate(q_ref[...], *t).astype(BF16)
        ko_ref[...] = _rotate(k_ref[...], *t).astype(BF16)
        vo_ref[...] = v_ref[...].astype(BF16)

    qs = pl.BlockSpec((br, wq), lambda i: (i, 0))
    ks = pl.BlockSpec((br, wkv), lambda i: (i, SWA_GROUP))
    vs = pl.BlockSpec((br, wkv), lambda i: (i, SWA_GROUP + 1))
    ts = pl.BlockSpec((br, LANES), lambda i: (i, 0))
    kv_out = pl.BlockSpec((br, wkv), lambda i: (i, 0))
    return pl.pallas_call(
        body, name=name, grid=(s_len // br,), in_specs=[qs, ks, vs, ts, ts, ts],
        out_specs=[qs, kv_out, kv_out],
        out_shape=[jax.ShapeDtypeStruct((s_len, wq), BF16), jax.ShapeDtypeStruct((s_len, wkv), BF16),
                   jax.ShapeDtypeStruct((s_len, wkv), BF16)],
        compiler_params=_params("parallel"))(qkv, qkv, qkv, *tables)


def _rope_bwd(dq, dk, dv, name):
    s_len, wq = dq.shape
    wkv = dk.shape[1]
    br = _row_block(s_len, wq + 2 * wkv, budget=2 << 20)
    tables = _rope_tables(s_len, -1.0)

    def body(q_ref, k_ref, v_ref, c_ref, su_ref, sd_ref, o_ref):
        t = (c_ref[...], su_ref[...], sd_ref[...])
        o_ref[:, :wq] = _rotate(q_ref[...], *t).astype(BF16)
        o_ref[:, wq:wq + wkv] = _rotate(k_ref[...], *t).astype(BF16)
        o_ref[:, wq + wkv:] = v_ref[...].astype(BF16)

    qs = pl.BlockSpec((br, wq), lambda i: (i, 0))
    kvs = pl.BlockSpec((br, wkv), lambda i: (i, 0))
    ts = pl.BlockSpec((br, LANES), lambda i: (i, 0))
    return pl.pallas_call(
        body, name=name, grid=(s_len // br,), in_specs=[qs, kvs, kvs, ts, ts, ts],
        out_specs=pl.BlockSpec((br, wq + 2 * wkv), lambda i: (i, 0)),
        out_shape=jax.ShapeDtypeStruct((s_len, wq + 2 * wkv), BF16),
        compiler_params=_params("parallel"))(dq, dk, dv, *tables)


def _swa_valid(i):
    row = lax.broadcasted_iota(jnp.int32, (BLOCK, 2 * BLOCK), 0)
    col = lax.broadcasted_iota(jnp.int32, (BLOCK, 2 * BLOCK), 1)
    return (col - BLOCK <= row) & (row < col) & ((col >= BLOCK) | (i > 0))


def _swa_specs(wq, wkv):
    q_spec = pl.BlockSpec((BLOCK, wq), lambda i: (i, 0))
    cur = pl.BlockSpec((BLOCK, wkv), lambda i: (i, 0))
    prev = pl.BlockSpec((BLOCK, wkv), lambda i: (jnp.maximum(i - 1, 0), 0))
    sink = pl.BlockSpec(memory_space=pltpu.SMEM)
    return q_spec, cur, prev, sink


def _swa_probs(q_h, k_cat, valid, sink):
    s = lax.dot_general(q_h, k_cat, (((1,), (1,)), ((), ())), preferred_element_type=F32)
    s = jnp.where(valid, s * (SWA_HEAD_DIM ** -0.5), NEG_INF)
    m = jnp.maximum(jnp.max(s, axis=-1, keepdims=True), sink)
    p = jnp.exp(s - m)
    e_sink = jnp.exp(sink - m)
    denom = jnp.sum(p, axis=-1, keepdims=True) + e_sink
    return p / denom, e_sink / denom


def _swa_fwd(q, k, v, sinks, name):
    s_len, wq = q.shape
    wkv = k.shape[1]
    hd = SWA_HEAD_DIM
    q_spec, cur, prev, sink_spec = _swa_specs(wq, wkv)

    def body(q_ref, kc_ref, kp_ref, vc_ref, vp_ref, sink_ref, o_ref):
        valid = _swa_valid(pl.program_id(0))
        for j in range(wkv // hd):
            lanes = slice(j * hd, (j + 1) * hd)
            k_cat = jnp.concatenate([kp_ref[:, lanes], kc_ref[:, lanes]], axis=0)
            v_cat = jnp.concatenate([vp_ref[:, lanes], vc_ref[:, lanes]], axis=0)
            for hh in range(SWA_GROUP):
                h = j * SWA_GROUP + hh
                pn, _ = _swa_probs(q_ref[:, h * hd:(h + 1) * hd], k_cat, valid, sink_ref[h])
                o_ref[:, h * hd:(h + 1) * hd] = jnp.dot(
                    pn.astype(BF16), v_cat, preferred_element_type=F32).astype(BF16)

    return pl.pallas_call(body, name=name, grid=(s_len // BLOCK,),
                          in_specs=[q_spec, cur, prev, cur, prev, sink_spec], out_specs=q_spec,
                          out_shape=jax.ShapeDtypeStruct((s_len, wq), BF16),
                          compiler_params=_params("parallel"))(q, k, k, v, v, sinks)


def _swa_bwd(q, k, v, sinks, do, name):
    s_len, wq = q.shape
    wkv = k.shape[1]
    hd = SWA_HEAD_DIM
    q_spec, cur, prev, sink_spec = _swa_specs(wq, wkv)
    full = pl.BlockSpec((s_len, wkv), lambda i: (0, 0))
    one = pl.BlockSpec((1, LANES), lambda i: (0, 0))
    scale = hd ** -0.5

    def body(q_ref, kc_ref, kp_ref, vc_ref, vp_ref, sink_ref, do_ref, dq_ref, dk_ref, dv_ref, ds_ref):
        i = pl.program_id(0)

        @pl.when(i == 0)
        def _():
            dk_ref[...] = jnp.zeros_like(dk_ref)
            dv_ref[...] = jnp.zeros_like(dv_ref)
            ds_ref[...] = jnp.zeros_like(ds_ref)

        valid = _swa_valid(i)
        lane = lax.broadcasted_iota(jnp.int32, (1, LANES), 1)
        dsink_step = jnp.zeros((1, LANES), F32)
        rows_prev = pl.ds(pl.multiple_of(jnp.maximum(i - 1, 0) * BLOCK, BLOCK), BLOCK)
        rows_cur = pl.ds(pl.multiple_of(i * BLOCK, BLOCK), BLOCK)
        for j in range(wkv // hd):
            lanes = slice(j * hd, (j + 1) * hd)
            k_cat = jnp.concatenate([kp_ref[:, lanes], kc_ref[:, lanes]], axis=0)
            v_cat = jnp.concatenate([vp_ref[:, lanes], vc_ref[:, lanes]], axis=0)
            dk_cat = jnp.zeros((2 * BLOCK, hd), F32)
            dv_cat = jnp.zeros((2 * BLOCK, hd), F32)
            for hh in range(SWA_GROUP):
                h = j * SWA_GROUP + hh
                q_h = q_ref[:, h * hd:(h + 1) * hd]
                do_h = do_ref[:, h * hd:(h + 1) * hd]
                pn, p_sink = _swa_probs(q_h, k_cat, valid, sink_ref[h])
                dpn = lax.dot_general(do_h, v_cat, (((1,), (1,)), ((), ())), preferred_element_type=F32)
                delta = jnp.sum(dpn * pn, axis=-1, keepdims=True)
                ds = (pn * (dpn - delta) * scale).astype(BF16)
                dsink_h = -jnp.sum(p_sink * delta, axis=0, keepdims=True)
                dsink_step = dsink_step + jnp.where(lane == h, dsink_h, 0.0)
                dq_ref[:, h * hd:(h + 1) * hd] = jnp.dot(ds, k_cat, preferred_element_type=F32)
                dk_cat = dk_cat + lax.dot_general(ds, q_h, (((0,), (0,)), ((), ())),
                                                  preferred_element_type=F32)
                dv_cat = dv_cat + lax.dot_general(pn.astype(BF16), do_h, (((0,), (0,)), ((), ())),
                                                  preferred_element_type=F32)
            dk_ref[rows_prev, lanes] += dk_cat[:BLOCK]
            dk_ref[rows_cur, lanes] += dk_cat[BLOCK:]
            dv_ref[rows_prev, lanes] += dv_cat[:BLOCK]
            dv_ref[rows_cur, lanes] += dv_cat[BLOCK:]
        ds_ref[...] += dsink_step

    return pl.pallas_call(
        body, name=name, grid=(s_len // BLOCK,),
        in_specs=[q_spec, cur, prev, cur, prev, sink_spec, q_spec],
        out_specs=[q_spec, full, full, one],
        out_shape=[jax.ShapeDtypeStruct((s_len, wq), F32), jax.ShapeDtypeStruct((s_len, wkv), F32),
                   jax.ShapeDtypeStruct((s_len, wkv), F32), jax.ShapeDtypeStruct((1, LANES), F32)],
        compiler_params=_params("arbitrary"))(q, k, k, v, v, sinks, do)


def _log_sigmoid(x):
    return jnp.minimum(x, 0.0) - jnp.log(1.0 + jnp.exp(-jnp.abs(x)))


def _tri_ones(lower):
    row = lax.broadcasted_iota(jnp.int32, (BLOCK, BLOCK), 0)
    col = lax.broadcasted_iota(jnp.int32, (BLOCK, BLOCK), 1)
    return jnp.where((col <= row) if lower else (col >= row), 1.0, 0.0).astype(F32)


def _fox_decay(proj, bf_row, fl_block, name):
    s_len = proj.shape[0]
    nchunk = s_len // BLOCK

    def body(fl_ref, bf_ref, dec_ref):
        tri = _tri_ones(True)
        carry = jnp.zeros((1, LANES), F32)
        for c in range(nchunk):
            rows = slice(c * BLOCK, (c + 1) * BLOCK)
            log_f = _log_sigmoid(fl_ref[rows, :] + bf_ref[...])
            loc = jnp.dot(tri, log_f, preferred_element_type=F32, precision=lax.Precision.HIGHEST) + carry
            dec_ref[rows, :] = loc
            carry = loc[BLOCK - 1:BLOCK, :]

    return pl.pallas_call(
        body, name=name, grid=(1,),
        in_specs=[pl.BlockSpec((s_len, LANES), lambda i: (0, fl_block)),
                  pl.BlockSpec((1, LANES), lambda i: (0, 0))],
        out_specs=pl.BlockSpec((s_len, LANES), lambda i: (0, 0)),
        out_shape=jax.ShapeDtypeStruct((s_len, LANES), F32),
        compiler_params=_params("arbitrary"))(proj, bf_row)


def _fox_decay_bwd(ddq, ddk, proj, bf_row, fl_block, heads, name):
    s_len = proj.shape[0]
    nchunk = s_len // BLOCK

    def body(ddq_ref, ddk_ref, fl_ref, bf_ref, dfl_ref, dbf_ref):
        tri = _tri_ones(False)
        lane_ok = lax.broadcasted_iota(jnp.int32, (BLOCK, LANES), 1) < heads
        carry = jnp.zeros((1, LANES), F32)
        dbf = jnp.zeros((1, LANES), F32)
        for c in reversed(range(nchunk)):
            rows = slice(c * BLOCK, (c + 1) * BLOCK)
            ddec = ddq_ref[rows, :] + ddk_ref[rows, :]
            dlog = jnp.dot(tri, ddec, preferred_element_type=F32, precision=lax.Precision.HIGHEST) + carry
            carry = dlog[0:1, :]
            dfl = jnp.where(lane_ok, dlog * _sigmoid(-(fl_ref[rows, :] + bf_ref[...])), 0.0)
            dfl_ref[rows, :] = dfl.astype(BF16)
            dbf = dbf + jnp.sum(dfl, axis=0, keepdims=True)
        dbf_ref[...] = dbf

    blk = pl.BlockSpec((s_len, LANES), lambda i: (0, 0))
    one = pl.BlockSpec((1, LANES), lambda i: (0, 0))
    return pl.pallas_call(
        body, name=name, grid=(1,),
        in_specs=[blk, blk, pl.BlockSpec((s_len, LANES), lambda i: (0, fl_block)), one],
        out_specs=[blk, one],
        out_shape=[jax.ShapeDtypeStruct((s_len, LANES), BF16), jax.ShapeDtypeStruct((1, LANES), F32)],
        compiler_params=_params("arbitrary"))(ddq, ddk, proj, bf_row)


def _fox_scores(q, k, decq, deck, i, bq):
    s_len = k.shape[0]
    s = lax.dot_general(q, k, (((1,), (1,)), ((), ())), preferred_element_type=F32)
    s = s * (FOX_HEAD_DIM ** -0.5) + decq - deck
    row = lax.broadcasted_iota(jnp.int32, (bq, s_len), 0) + i * bq
    col = lax.broadcasted_iota(jnp.int32, (bq, s_len), 1)
    s = jnp.where(col <= row, s, NEG_INF)
    p = jnp.exp(s - jnp.max(s, axis=-1, keepdims=True))
    return p / jnp.sum(p, axis=-1, keepdims=True)


def _fox_key_spans(s_len, bq):
    n_span = min(4, s_len // bq)
    return [(j + 1) * (s_len // n_span) for j in range(n_span)]


def _fox_span_of(i, s_len, bq):
    span = s_len // min(4, s_len // bq)
    return ((i * bq) // span + 1) * span


def _fox_specs(s_len, heads, bq):
    hd = FOX_HEAD_DIM
    q_spec = pl.BlockSpec((bq, hd), lambda h, i: (i, h))
    k_spec = pl.BlockSpec((s_len, hd), lambda h, i: (0, heads + h))
    v_spec = pl.BlockSpec((s_len, hd), lambda h, i: (0, 2 * heads + h))
    dq_spec = pl.BlockSpec((None, bq, 1), lambda h, i: (h, i, 0))
    dk_spec = pl.BlockSpec((None, 1, s_len), lambda h, i: (h, 0, 0))
    return q_spec, k_spec, v_spec, dq_spec, dk_spec


def _fox_fwd(proj, decq, deck, heads, name):
    s_len = proj.shape[0]
    bq = _pick(s_len, (256, 128))
    q_spec, k_spec, v_spec, dq_spec, dk_spec = _fox_specs(s_len, heads, bq)

    def body(q_ref, k_ref, v_ref, decq_ref, deck_ref, o_ref):
        i = pl.program_id(1)
        for klen in _fox_key_spans(s_len, bq):
            @pl.when(_fox_span_of(i, s_len, bq) == klen)
            def _(klen=klen):
                pn = _fox_scores(q_ref[...].astype(BF16), k_ref[:klen, :].astype(BF16), decq_ref[...],
                                 deck_ref[:, :klen], i, bq)
                o_ref[...] = jnp.dot(pn.astype(BF16), v_ref[:klen, :].astype(BF16),
                                     preferred_element_type=F32).astype(BF16)

    return pl.pallas_call(body, name=name, grid=(heads, s_len // bq),
                          in_specs=[q_spec, k_spec, v_spec, dq_spec, dk_spec], out_specs=q_spec,
                          out_shape=jax.ShapeDtypeStruct((s_len, heads * FOX_HEAD_DIM), BF16),
                          compiler_params=_params("parallel", "parallel"))(proj, proj, proj, decq, deck)


def _fox_bwd(proj, decq, deck, do, heads, name):
    s_len = proj.shape[0]
    d = heads * FOX_HEAD_DIM
    bq = _pick(s_len, (256, 128))
    q_spec, k_spec, v_spec, dq_spec, dk_spec = _fox_specs(s_len, heads, bq)
    acc_spec = pl.BlockSpec((s_len, FOX_HEAD_DIM), lambda h, i: (0, h))
    scale = FOX_HEAD_DIM ** -0.5

    def body(q_ref, k_ref, v_ref, decq_ref, deck_ref, do_ref, dq_ref, dk_ref, dv_ref, ddq_ref, ddk_ref):
        i = pl.program_id(1)

        @pl.when(i == 0)
        def _():
            dk_ref[...] = jnp.zeros_like(dk_ref)
            dv_ref[...] = jnp.zeros_like(dv_ref)
            ddk_ref[...] = jnp.zeros_like(ddk_ref)

        q = q_ref[...].astype(BF16)
        do_b = do_ref[...]
        for klen in _fox_key_spans(s_len, bq):
            @pl.when(_fox_span_of(i, s_len, bq) == klen)
            def _(klen=klen):
                k = k_ref[:klen, :].astype(BF16)
                pn = _fox_scores(q, k, decq_ref[...], deck_ref[:, :klen], i, bq)
                dpn = lax.dot_general(do_b, v_ref[:klen, :].astype(BF16), (((1,), (1,)), ((), ())),
                                      preferred_element_type=F32)
                ds = pn * (dpn - jnp.sum(dpn * pn, axis=-1, keepdims=True))
                ddq_ref[...] = jnp.sum(ds, axis=-1, keepdims=True)
                ddk_ref[:, :klen] -= jnp.sum(ds, axis=0, keepdims=True)
                ds_b = (ds * scale).astype(BF16)
                dq_ref[...] = jnp.dot(ds_b, k, preferred_element_type=F32).astype(BF16)
                dk_ref[:klen, :] += lax.dot_general(ds_b, q, (((0,), (0,)), ((), ())),
                                                    preferred_element_type=F32)
                dv_ref[:klen, :] += lax.dot_general(pn.astype(BF16), do_b, (((0,), (0,)), ((), ())),
                                                    preferred_element_type=F32)

    return pl.pallas_call(
        body, name=name, grid=(heads, s_len // bq),
        in_specs=[q_spec, k_spec, v_spec, dq_spec, dk_spec, q_spec],
        out_specs=[q_spec, acc_spec, acc_spec, dq_spec, dk_spec],
        out_shape=[jax.ShapeDtypeStruct((s_len, d), BF16), jax.ShapeDtypeStruct((s_len, d), F32),
                   jax.ShapeDtypeStruct((s_len, d), F32), jax.ShapeDtypeStruct((heads, s_len, 1), F32),
                   jax.ShapeDtypeStruct((heads, 1, s_len), F32)],
        compiler_params=_params("parallel", "arbitrary"))(proj, proj, proj, decq, deck, do)


def _place():
    x, y, c = lax.axis_index("x"), lax.axis_index("y"), lax.axis_index("c")
    chips = [(1 - x, y), (x, 1 - y), (1 - x, 1 - y)]
    return x, y, c, chips


def _remote(src, dst, send_sems, recv_sems, idx, to):
    return pltpu.make_async_remote_copy(src_ref=src, dst_ref=dst, send_sem=send_sems.at[idx],
                                        recv_sem=recv_sems.at[idx], device_id=to, device_id_type=MESH)


def _row_chunks(rows, want):
    for k in (want, want // 2, want // 4):
        if k >= 1 and rows % (16 * k) == 0:
            return [(j * (rows // k), rows // k) for j in range(k)]
    return [(0, rows)]


D2D_CHUNKS = 8


def _cast_into_slot(w, me, name, dep=None):
    rows, width = w.shape
    br = _row_block(rows, width, budget=4 << 20)

    def body(me_ref, w_ref, *rest):
        rest[-1][...] = w_ref[...].astype(BF16)

    in_specs = [pl.BlockSpec((br, width), lambda i, me_ref: (i, 0))]
    if dep is not None:
        in_specs.append(pl.BlockSpec(dep.shape, lambda i, me_ref: (0, 0)))
    return pl.pallas_call(
        body, name=name,
        grid_spec=pltpu.PrefetchScalarGridSpec(
            num_scalar_prefetch=1, grid=(rows // br,), in_specs=in_specs,
            out_specs=pl.BlockSpec((None, br, width), lambda i, me_ref: (me_ref[0], i, 0))),
        out_shape=jax.ShapeDtypeStruct((N_CHIPS, rows, width), BF16),
        compiler_params=_params("parallel"))(me, w, *([dep] if dep is not None else []))


def _hbm(arr):
    return pltpu.with_memory_space_constraint(arr, pltpu.HBM)


def _token_shape():
    return jax.ShapeDtypeStruct((8, LANES), F32)


def _add_pair(grad, got, c, name):
    _, half, width = got.shape
    br = _row_block(half, width, itemsize=2, budget=3 << 20)
    nb = half // br

    def body(c_ref, a_ref, b_ref, o_ref):
        o_ref[...] = (a_ref[...].astype(F32) + b_ref[...].astype(F32)).astype(BF16)

    spec = pl.BlockSpec((None, br, width), lambda j, i, c_ref: (j, i, 0))
    mine = pl.BlockSpec((None, br, width), lambda j, i, c_ref: (j, c_ref[0] * nb + i, 0))
    return pl.pallas_call(
        body, name=name,
        grid_spec=pltpu.PrefetchScalarGridSpec(num_scalar_prefetch=1, grid=(N_CHIPS, nb),
                                               in_specs=[mine, spec], out_specs=spec),
        out_shape=jax.ShapeDtypeStruct(got.shape, BF16),
        compiler_params=_params("parallel", "parallel"))(c, grad, got)


def _sum_chips(pair, others, me, name):
    _, rows, width = pair.shape
    br = _row_block(rows, width, itemsize=4, budget=3 << 20)

    def body(me_ref, p_ref, o3_ref, o_ref):
        acc = p_ref[...].astype(F32)
        for r in range(N_CHIPS - 1):
            acc = acc + o3_ref[r].astype(F32)
        o_ref[...] = acc

    return pl.pallas_call(
        body, name=name,
        grid_spec=pltpu.PrefetchScalarGridSpec(
            num_scalar_prefetch=1, grid=(rows // br,),
            in_specs=[pl.BlockSpec((None, br, width), lambda i, me_ref: (me_ref[0], i, 0)),
                      pl.BlockSpec((N_CHIPS - 1, br, width), lambda i, me_ref: (0, i, 0))],
            out_specs=pl.BlockSpec((br, width), lambda i, me_ref: (i, 0))),
        out_shape=jax.ShapeDtypeStruct((rows, width), F32),
        compiler_params=_params("parallel"))(me, pair, others)


def _adamw_halves(w, mine, theirs, m, v, c, name):
    rows, width = w.shape
    half = rows // 2
    br = _row_block(half, width, budget=3 << 19)
    nb = half // br
    c1 = 1.0 - ADAM_B1 ** ADAM_STEP
    c2 = 1.0 - ADAM_B2 ** ADAM_STEP

    def body(c_ref, w_ref, a_ref, b_ref, m_ref, v_ref, g_ref, d_ref, nm_ref, nv_ref):
        grad = jnp.where(pl.program_id(0) == c_ref[0], a_ref[...], b_ref[...])
        new_m = ADAM_B1 * m_ref[...] + (1.0 - ADAM_B1) * grad
        new_v = ADAM_B2 * v_ref[...] + (1.0 - ADAM_B2) * (grad * grad)
        g_ref[...] = grad
        d_ref[...] = -ADAM_LR * ((new_m / c1) / (jnp.sqrt(new_v / c2) + ADAM_EPS) + ADAM_WD * w_ref[...])
        nm_ref[...] = new_m
        nv_ref[...] = new_v

    full = pl.BlockSpec((br, width), lambda h, i, c_ref: (h * nb + i, 0))
    mine_spec = pl.BlockSpec((br, width), lambda h, i, c_ref: (jnp.where(h == c_ref[0], i, 0), 0))
    theirs_spec = pl.BlockSpec((br, width), lambda h, i, c_ref: (jnp.where(h == c_ref[0], 0, i), 0))
    shp = jax.ShapeDtypeStruct(w.shape, F32)
    return pl.pallas_call(
        body, name=name,
        grid_spec=pltpu.PrefetchScalarGridSpec(num_scalar_prefetch=1, grid=(2, nb),
                                               in_specs=[full, mine_spec, theirs_spec, full, full],
                                               out_specs=[full] * 4),
        out_shape=[shp] * 4,
        compiler_params=_params("parallel", "parallel"))(c, w, mine, theirs, m, v)


class _Transfer:
    def __init__(self, n_sems, build):
        self.n_sems, self.build = n_sems, build


def _copies(src_of, dst_of, land_of, rows, chunks, send, recv, idx, to):
    starts = [_remote(src_of(s, z), dst_of(s, z), send, recv, idx, to) for s, z in _row_chunks(rows, chunks)]
    return starts, _remote(src_of(0, rows), land_of(0, rows), send, recv, idx, to)


def _gather_direct(keys, shapes):
    def build(refs, send, recv):
        x, y, c, chips = _place()
        me = 2 * x + y
        out = []
        for t, key in enumerate(keys):
            half = shapes[t][1] // 2
            for r, chip in enumerate(chips[:2]):
                slot = 2 * chip[0] + chip[1]
                out.append(_copies(lambda s, z, key=key, half=half: refs[key].at[me, pl.ds(c * half + s, z)],
                                   lambda s, z, key=key, half=half: refs[key].at[me, pl.ds(c * half + s, z)],
                                   lambda s, z, key=key, half=half, slot=slot: refs[key].at[slot, pl.ds(c * half + s, z)],
                                   half, 1, send, recv, 2 * t + r, (*chip, c)))
        return out
    return _Transfer(2 * len(keys), build)


def _gather_relay(keys, shapes):
    def build(refs, send, recv):
        x, y, c, chips = _place()
        slot_x, slot_y, slot_d = (2 * ch[0] + ch[1] for ch in chips)
        src_slot = slot_y + c * (slot_x - slot_y)
        to = (x ^ (1 - c), y ^ c, c)
        out = []
        for t, key in enumerate(keys):
            half = shapes[t][1] // 2
            out.append(_copies(lambda s, z, key=key, half=half: refs[key].at[src_slot, pl.ds(c * half + s, z)],
                               lambda s, z, key=key, half=half: refs[key].at[src_slot, pl.ds(c * half + s, z)],
                               lambda s, z, key=key, half=half: refs[key].at[slot_d, pl.ds(c * half + s, z)],
                               half, 1, send, recv, t, to))
        return out
    return _Transfer(len(keys), build)


def _gather_pair(keys, shapes):
    def build(refs, send, recv):
        x, y, c, chips = _place()
        out = []
        for t, key in enumerate(keys):
            half = shapes[t][1] // 2
            for r, chip in enumerate(chips):
                slot = 2 * chip[0] + chip[1]
                mine = lambda s, z, key=key, half=half, slot=slot: refs[key].at[slot, pl.ds(c * half + s, z)]
                land = lambda s, z, key=key, half=half, slot=slot: refs[key].at[slot, pl.ds((1 - c) * half + s, z)]
                out.append(_copies(mine, mine, land, half, D2D_CHUNKS, send, recv, 3 * t + r, (x, y, 1 - c)))
        return out
    return _Transfer(3 * len(keys), build)


def _grad_pair(keys, lands, shapes):
    def build(refs, send, recv):
        x, y, c, _ = _place()
        out = []
        for t, (key, land) in enumerate(zip(keys, lands)):
            half = shapes[t][1] // 2
            for j in range(N_CHIPS):
                out.append(_copies(
                    lambda s, z, key=key, half=half, j=j: refs[key].at[j, pl.ds((1 - c) * half + s, z)],
                    lambda s, z, land=land, j=j: refs[land].at[j, pl.ds(s, z)],
                    lambda s, z, land=land, j=j: refs[land].at[j, pl.ds(s, z)],
                    half, 2, send, recv, N_CHIPS * t + j, (x, y, 1 - c)))
        return out
    return _Transfer(N_CHIPS * len(keys), build)


def _grad_chips(keys, lands, shapes):
    def build(refs, send, recv):
        x, y, c, chips = _place()
        out = []
        for t, (key, land) in enumerate(zip(keys, lands)):
            rows = shapes[t][1]
            for r, chip in enumerate(chips):
                slot = 2 * chip[0] + chip[1]
                out.append(_copies(lambda s, z, key=key, slot=slot: refs[key].at[slot, pl.ds(s, z)],
                                   lambda s, z, land=land, r=r: refs[land].at[r, pl.ds(s, z)],
                                   lambda s, z, land=land, r=r: refs[land].at[r, pl.ds(s, z)],
                                   rows, 1, send, recv, 3 * t + r, (*chip, c)))
        return out
    return _Transfer(3 * len(keys), build)


def _grad_join(keys, lands, shapes):
    def build(refs, send, recv):
        x, y, c, _ = _place()
        out = []
        for t, (key, land) in enumerate(zip(keys, lands)):
            out.append(_copies(lambda s, z, key=key: refs[key].at[pl.ds(s, z)],
                               lambda s, z, land=land: refs[land].at[pl.ds(s, z)],
                               lambda s, z, land=land: refs[land].at[pl.ds(s, z)],
                               shapes[t][0], D2D_CHUNKS, send, recv, t, (x, y, 1 - c)))
        return out
    return _Transfer(len(keys), build)


def _comm_call(name, arrays, waits, starts, after):
    keys = list(arrays)
    n, nw, ns = len(keys), len(waits), len(starts)

    def body(*refs):
        in_sems = refs[n:n + 2 * nw]
        base = n + 2 * nw + 1
        out_sems = refs[base:base + 2 * ns]
        bufs = dict(zip(keys, refs[base + 2 * ns:base + 2 * ns + n]))
        token = refs[base + 2 * ns + n]
        for k, (transfer, _, _) in enumerate(waits):
            for _, whole in transfer.build(bufs, in_sems[2 * k], in_sems[2 * k + 1]):
                whole.wait_send()
                whole.wait_recv()
        for k, transfer in enumerate(starts):
            for chunks, _ in transfer.build(bufs, out_sems[2 * k], out_sems[2 * k + 1]):
                for cp in chunks:
                    cp.start()
        token[...] = jnp.zeros_like(token)

    sem_shapes = []
    for transfer in starts:
        sem_shapes += [pltpu.SemaphoreType.DMA((transfer.n_sems,))] * 2
    operands = [_hbm(arrays[k]) for k in keys]
    for _, send, recv in waits:
        operands += [send, recv]
    res = pl.pallas_call(
        body, name=name, in_specs=[HBM] * n + [SEM] * (2 * nw) + [pl.BlockSpec(memory_space=pl.ANY)],
        out_specs=[SEM] * (2 * ns) + [HBM] * n + [pl.BlockSpec(memory_space=pltpu.VMEM)],
        out_shape=sem_shapes + [pltpu.HBM(arrays[k].shape, arrays[k].dtype) for k in keys] + [_token_shape()],
        input_output_aliases={t: 2 * ns + t for t in range(n)},
        compiler_params=pltpu.CompilerParams(has_side_effects=EFFECT),
    )(*operands, after)
    sems = [(res[2 * k], res[2 * k + 1]) for k in range(ns)]
    return dict(zip(keys, res[2 * ns:2 * ns + n])), sems, res[2 * ns + n]


def _device_gather(part_key, all_key, rows):
    def build(refs, send, recv):
        x, y, c, _ = _place()
        me = 4 * x + 2 * y + c
        out = []
        for r in range(1, N_DEV):
            peer = (x ^ (r >> 2), y ^ ((r >> 1) & 1), c ^ (r & 1))
            theirs = 4 * peer[0] + 2 * peer[1] + peer[2]
            out.append(_copies(lambda s, z: refs[part_key].at[pl.ds(s, z)],
                               lambda s, z: refs[all_key].at[me, pl.ds(s, z)],
                               lambda s, z, theirs=theirs: refs[all_key].at[theirs, pl.ds(s, z)],
                               rows, 1, send, recv, r - 1, peer))
        return out
    return _Transfer(N_DEV - 1, build)


def _sum_devices(parts, name):
    _, rows, width = parts.shape
    br = _row_block(rows, width, budget=1 << 19)

    def body(p_ref, o_ref):
        acc = p_ref[0]
        for dev in range(1, N_DEV):
            acc = acc + p_ref[dev]
        o_ref[...] = acc

    return pl.pallas_call(body, name=name, grid=(rows // br,),
                          in_specs=[pl.BlockSpec((N_DEV, br, width), lambda i: (0, i, 0))],
                          out_specs=pl.BlockSpec((br, width), lambda i: (i, 0)),
                          out_shape=jax.ShapeDtypeStruct((rows, width), F32),
                          compiler_params=_params("parallel"))(parts)


INPUT_NAMES = None


def _weight_names():
    names = []
    for i, kind in enumerate(("gmlp", "swa", "fox", "gmlp")):
        p = f"l{i}_"
        names += [p + "ffn1_norm", p + "ffn1_wi", p + "ffn1_wo", p + "mix_norm", p + "mix_win"]
        if kind == "gmlp":
            names += [p + "gmlp_vnorm", p + "gmlp_ws", p + "gmlp_bs"]
        elif kind == "swa":
            names += [p + "swa_sinks"]
        else:
            names += [p + "fox_bf"]
        names += [p + "mix_wout", p + "ffn2_norm", p + "ffn2_wi", p + "ffn2_wo"]
    return names + ["final_norm"]


WEIGHTS = _weight_names()
MIXERS = ("gmlp", "swa", "fox", "gmlp")
BIG = ("ffn1_wi", "ffn1_wo", "mix_win", "mix_wout", "ffn2_wi", "ffn2_wo")


def _ffn_fwd(h, gain, wi, wo, tag, dep=None):
    n = _rms_fwd(h, gain, tag + "_norm", dep=dep)
    z = _matmul(n, wi, name=tag + "_up", out_dtype=BF16)
    a = _swiglu_fwd(z, tag + "_act")
    f, d = wo.shape[0] * wo.shape[1], wo.shape[2]
    out = _matmul(a, wo.reshape(f, d), name=tag + "_down", out_dtype=F32, scale=0.5, resid=h)
    return out, (h, n, z, a)


def _ffn_bwd(dout, saved, gain, wi, wo, tag, dep=None):
    h, n, z, a = saved
    f, d = wo.shape[0] * wo.shape[1], wo.shape[2]
    da = _matmul(dout, wo.reshape(f, d), tb=True, name=tag + "_bdown", out_dtype=BF16, scale=0.5, dep=dep)
    dwo = _matmul(a, dout, ta=True, name=tag + "_gdown", out_dtype=BF16, scale=0.5, dep=dep)
    dz = _swiglu_bwd(z, da, tag + "_bact")
    dn = _matmul(dz, wi, tb=True, name=tag + "_bup", out_dtype=F32)
    dwi = _matmul(n, dz, ta=True, name=tag + "_gup", out_dtype=BF16, out_shards=N_CHIPS)
    dh, dgain = _norm_bwd(h, gain, dn, dout, tag + "_bnorm")
    return dh, dgain, dwi, dwo.reshape(wo.shape)


def _natural(w_sharded, pad_to):
    ns, rows, csh = w_sharded.shape
    nat = jnp.transpose(w_sharded, (1, 0, 2)).reshape(rows, ns * csh)
    extra = (-nat.shape[1]) % pad_to
    return jnp.pad(nat, ((0, 0), (0, extra))) if extra else nat


def _mixer_fwd(kind, h, p, tag, dep=None):
    s_len, d = h.shape
    n = _rms_fwd(h, p["mix_norm"], tag + "_norm", dep=dep)
    wout = p["mix_wout"].reshape(d, d)
    if kind == "gmlp":
        zp = _matmul(n, p["mix_win"], name=tag + "_in", out_dtype=BF16)
        y = _gmlp_fwd(zp, p["gmlp_vnorm"], p["gmlp_ws"], p["gmlp_bs"], tag + "_gate")
        saved = (h, n, zp, y)
    elif kind == "swa":
        qkv = _matmul(n, p["mix_win"], name=tag + "_in", out_dtype=F32)
        q, k, v = _rope_fwd(qkv, tag + "_rope")
        y = _swa_fwd(q, k, v, p["swa_sinks"], tag + "_attn")
        saved = (h, n, q, k, v, y)
    else:
        heads = d // FOX_HEAD_DIM
        win = _natural(p["mix_win"], LANES)
        proj = _matmul(n, win, name=tag + "_in", out_dtype=F32)
        bf_row = jnp.pad(p["fox_bf"], (0, LANES - heads)).reshape(1, LANES)
        dec = _fox_decay(proj, bf_row, 3 * heads, tag + "_decay")
        dec_t = dec[:, :heads].T
        decq, deck = dec_t.reshape(heads, s_len, 1), dec_t.reshape(heads, 1, s_len)
        y = _fox_fwd(proj, decq, deck, heads, tag + "_attn")
        saved = (h, n, win, proj, bf_row, decq, deck, y)
    out = _matmul(y, wout, name=tag + "_out", out_dtype=F32, resid=h)
    return out, saved


def _mixer_bwd(kind, dout, saved, p, tag, dep=None):
    h, n = saved[0], saved[1]
    y = saved[-1]
    s_len, d = h.shape
    wout = p["mix_wout"].reshape(d, d)
    grads = {}
    dy = _matmul(dout, wout, tb=True, name=tag + "_bout", out_dtype=BF16, dep=dep)
    grads["mix_wout"] = _matmul(y, dout, ta=True, name=tag + "_gout", out_dtype=BF16,
                                dep=dep).reshape(p["mix_wout"].shape)
    if kind == "gmlp":
        zp = saved[2]
        dzp, dws, dbst, dvg = _gmlp_bwd(zp, dy, p["gmlp_vnorm"], p["gmlp_ws"], p["gmlp_bs"], tag + "_bgate")
        grads.update(gmlp_ws=dws, gmlp_bs=dbst.T, gmlp_vnorm=dvg.reshape(d))
        dn = _matmul(dzp, p["mix_win"], tb=True, name=tag + "_bin", out_dtype=F32)
        grads["mix_win"] = _matmul(n, dzp, ta=True, name=tag + "_gin", out_dtype=BF16, out_shards=N_CHIPS)
    elif kind == "swa":
        q, k, v = saved[2:5]
        dq, dk, dv, dsinks = _swa_bwd(q, k, v, p["swa_sinks"], dy, tag + "_battn")
        grads["swa_sinks"] = dsinks[0, :p["swa_sinks"].shape[0]]
        dqkv = _rope_bwd(dq, dk, dv, tag + "_brope")
        dn = _matmul(dqkv, p["mix_win"], tb=True, name=tag + "_bin", out_dtype=F32)
        grads["mix_win"] = _matmul(n, dqkv, ta=True, name=tag + "_gin", out_dtype=BF16, out_shards=N_CHIPS)
    else:
        win, proj, bf_row, decq, deck = saved[2:7]
        heads = d // FOX_HEAD_DIM
        dq, dk, dv, ddq, ddk = _fox_bwd(proj, decq, deck, dy, heads, tag + "_battn")
        widen = lambda t: jnp.pad(t.reshape(heads, s_len).T, ((0, 0), (0, LANES - heads)))
        dfl, dbf = _fox_decay_bwd(widen(ddq), widen(ddk), proj, bf_row, 3 * heads, heads, tag + "_bdecay")
        grads["fox_bf"] = dbf[0, :heads]
        dproj = jnp.concatenate([dq, dk.astype(BF16), dv.astype(BF16), dfl], axis=1)
        dn = _matmul(dproj, win, tb=True, name=tag + "_bin", out_dtype=F32)
        dwin = _matmul(n, dproj, ta=True, name=tag + "_gin", out_dtype=BF16)
        ns, rows, csh = p["mix_win"].shape
        grads["mix_win"] = jnp.transpose(dwin[:, :ns * csh].reshape(rows, ns, csh), (1, 0, 2))
    dh, dgain = _norm_bwd(h, p["mix_norm"], dn, dout, tag + "_bnorm")
    grads["mix_norm"] = dgain.reshape(d)
    return dh, grads


def _pack_small(arrays):
    flat = jnp.concatenate([a.reshape(-1).astype(F32) for a in arrays])
    pad = (-flat.shape[0]) % (512 * LANES)
    return jnp.pad(flat, (0, pad)).reshape(-1, LANES)


def _unpack_small(packed, like):
    flat, out, pos = packed.reshape(-1), [], 0
    for a in like:
        out.append(flat[pos:pos + a.size].reshape(a.shape))
        pos += a.size
    return out


def _step(inp):
    x, target = inp["x"][0], inp["loss_target"][0]
    d = x.shape[1]

    core = lax.axis_index("c").astype(jnp.int32).reshape(1)
    chip = (2 * lax.axis_index("x") + lax.axis_index("y")).astype(jnp.int32).reshape(1)

    groups = []
    for i in range(len(MIXERS)):
        groups += [(i, "ffn1", [f"l{i}_ffn1_wi", f"l{i}_ffn1_wo"]), (i, "mix", [f"l{i}_mix_win", f"l{i}_mix_wout"]),
                   (i, "ffn2", [f"l{i}_ffn2_wi", f"l{i}_ffn2_wo"])]

    def layer_params(i, full):
        p = {nm[len(f"l{i}_"):]: inp[nm] for nm in WEIGHTS if nm.startswith(f"l{i}_")}
        p.update({nm[len(f"l{i}_"):]: w for nm, w in full.items()})
        return p

    n_groups = len(groups)
    valid = lambda k: 0 <= k < n_groups

    bufs = {}
    full_shapes = lambda names: [(N_CHIPS, *inp[nm].shape) for nm in names]
    direct = [_gather_direct(names, full_shapes(names)) for _, _, names in groups]
    relay = [_gather_relay(names, full_shapes(names)) for _, _, names in groups]
    to_pair = [_gather_pair(names, full_shapes(names)) for _, _, names in groups]
    sems = {}

    def cast_groups(which, dep):
        for g in which:
            for nm in groups[g][2]:
                bufs[nm] = _cast_into_slot(inp[nm], chip, nm + "_cast", dep=dep)

    def gather_step(step, after):
        waits, starts, tags, keys = [], [], [], []
        for kind, transfers, g, begin in (("pair", to_pair, step, False), ("relay", relay, step + 1, False),
                                          ("pair", to_pair, step + 1, True), ("direct", direct, step + 2, False),
                                          ("relay", relay, step + 2, True), ("direct", direct, step + 3, True)):
            if not valid(g):
                continue
            keys += [nm for nm in groups[g][2] if nm not in keys]
            if begin:
                starts.append(transfers[g])
                tags.append((kind, g))
            else:
                waits.append((transfers[g], *sems.pop((kind, g))))
        new, started, token = _comm_call(f"gather_step{step + 3}", {k: bufs[k] for k in keys}, waits, starts, after)
        bufs.update(new)
        sems.update(zip(tags, started))
        return token

    cast_groups([0], None)
    token = gather_step(-3, x)
    cast_groups(range(1, 6), token)
    token = gather_step(-2, bufs[groups[5][2][-1]])
    cast_groups(range(6, n_groups), token)
    token = gather_step(-1, bufs[groups[-1][2][-1]])
    h, saved, fulls = x, [], []
    for g, (i, part, names) in enumerate(groups):
        token = gather_step(g, h)
        full = {nm: bufs[nm] for nm in names}
        p = layer_params(i, full)
        if part == "mix":
            h, s = _mixer_fwd(MIXERS[i], h, p, f"l{i}_mix", dep=token)
        else:
            h, s = _ffn_fwd(h, p[part + "_norm"], p[part + "_wi"], p[part + "_wo"], f"l{i}_{part}", dep=token)
        saved.append(s)
        fulls.append(full)
    loss_part, dh, dfinal = _loss_head(h, inp["final_norm"], target, "loss_head")
    loss = lax.psum(loss_part, ("x", "y", "c"))

    small_grads = {"final_norm": dfinal.reshape(d)}
    outs = {}
    work = {}
    stage = {}
    small_names = [nm for nm in WEIGHTS if nm.split("_", 1)[1] not in BIG]
    last_small = "l0_ffn1_norm"
    small_sets = {}

    def small_start(tag, names, after):
        part = _pack_small([small_grads[nm] for nm in names])
        device = 4 * lax.axis_index("x") + 2 * lax.axis_index("y") + lax.axis_index("c")
        work[tag + "#part"] = part
        work[tag + "#all"] = lax.dynamic_update_slice(jnp.zeros((N_DEV, *part.shape), F32), part[None],
                                                      (device, 0, 0))
        stage[(tag, 0)] = _device_gather(tag + "#part", tag + "#all", part.shape[0])
        stage_keys[(tag, 0)] = [tag + "#part", tag + "#all"]
        small_sets[tag] = names
        return comm(f"small_{tag}_start", [], [(tag, 0)], after)

    def small_finish(tag, after):
        names = small_sets[tag]
        comm(f"small_{tag}_wait", [(tag, 0)], [], after)
        total = _sum_devices(work[tag + "#all"], f"small_{tag}_sum")
        like = [inp[nm] for nm in names]
        upd = _adamw(_pack_small(like), total, _pack_small([inp["m_" + nm] for nm in names]),
                     _pack_small([inp["v_" + nm] for nm in names]), f"small_{tag}_adamw")
        unpacked = [_unpack_small(t, like) for t in (total, *upd)]
        for k, nm in enumerate(names):
            outs[nm] = tuple(u[k] for u in unpacked)
        return upd[0]

    def comm(name, transfers_to_wait, transfers_to_start, after):
        waits = [(stage[k], *sems.pop(k)) for k in transfers_to_wait if valid(k[1])]
        starts = [k for k in transfers_to_start if valid(k[1])]
        if not waits and not starts:
            return after
        keys = []
        for k in [k for k in transfers_to_wait if valid(k[1])] + starts:
            keys += [key for key in stage_keys[k] if key not in keys]
        new, started, token = _comm_call(name, {k: work[k] for k in keys}, waits, [stage[k] for k in starts], after)
        work.update(new)
        sems.update(zip(starts, started))
        return token

    stage_keys = {}

    def reduce_step(g, after):
        token = comm(f"rs_pair_step{n_groups - 1 - g}", [("pair", g + 1)], [("pair", g)], after)
        if valid(g + 1):
            names = groups[g + 1][2]
            for nm in names:
                work[nm + "#sum"] = _add_pair(work[nm + "#grad"], work[nm + "#got"], core, nm + "_rs_add")
                work[nm + "#others"] = lax.empty((N_CHIPS - 1, *work[nm + "#sum"].shape[1:]), BF16)
            shapes = [work[nm + "#sum"].shape for nm in names]
            stage[("chips", g + 1)] = _grad_chips([nm + "#sum" for nm in names], [nm + "#others" for nm in names], shapes)
            stage_keys[("chips", g + 1)] = [nm + sfx for nm in names for sfx in ("#sum", "#others")]
        token = comm(f"rs_chips_step{n_groups - 1 - g}", [("chips", g + 2)], [("chips", g + 1)], token)
        if valid(g + 2):
            names = groups[g + 2][2]
            for nm in names:
                work[nm + "#half"] = _sum_chips(work[nm + "#sum"], work[nm + "#others"], chip, nm + "_rs_sum")
                work[nm + "#theirs"] = lax.empty(work[nm + "#half"].shape, F32)
            shapes = [work[nm + "#half"].shape for nm in names]
            stage[("join", g + 2)] = _grad_join([nm + "#half" for nm in names], [nm + "#theirs" for nm in names], shapes)
            stage_keys[("join", g + 2)] = [nm + sfx for nm in names for sfx in ("#half", "#theirs")]
        token = comm(f"rs_join_step{n_groups - 1 - g}", [("join", g + 3)], [("join", g + 2)], token)
        if valid(g + 3):
            for nm in groups[g + 3][2]:
                outs[nm] = tuple(_adamw_halves(inp[nm], work[nm + "#half"], work[nm + "#theirs"], inp["m_" + nm],
                                               inp["v_" + nm], core, nm + "_adamw"))
        return token

    dep = None
    for g in reversed(range(n_groups)):
        i, part, names = groups[g]
        p = layer_params(i, fulls[g])
        if part == "mix":
            dh, mg = _mixer_bwd(MIXERS[i], dh, saved[g], p, f"l{i}_mix", dep=dep)
            grads = [mg.pop("mix_win"), mg.pop("mix_wout")]
            small_grads.update({f"l{i}_{key}": val for key, val in mg.items()})
        else:
            dh, g_norm, dwi, dwo = _ffn_bwd(dh, saved[g], p[part + "_norm"], p[part + "_wi"], p[part + "_wo"],
                                            f"l{i}_{part}", dep=dep)
            small_grads[f"l{i}_{part}_norm"] = g_norm.reshape(d)
            grads = [dwi, dwo]
        for nm, gr in zip(names, grads):
            work[nm + "#grad"] = gr
            work[nm + "#got"] = lax.empty((gr.shape[0], gr.shape[1] // 2, gr.shape[2]), BF16)
        stage[("pair", g)] = _grad_pair([nm + "#grad" for nm in names], [nm + "#got" for nm in names],
                                        [gr.shape for gr in grads])
        stage_keys[("pair", g)] = [nm + sfx for nm in names for sfx in ("#grad", "#got")]
        dep = reduce_step(g, dh)
        if g == 1:
            dep = small_start("early", [nm for nm in small_names if nm != last_small], dep)
    dep = small_start("late", [last_small], dep)
    dep = reduce_step(-1, dep)
    dep = small_finish("early", dep)
    dep = reduce_step(-2, dep)
    dep = small_finish("late", dep)
    reduce_step(-3, dep)

    result = [loss, dh[None]]
    for part in range(4):
        result += [outs[nm][part] for nm in WEIGHTS]
    return tuple(result)


def kernel(x, l0_ffn1_norm, l0_ffn1_wi, l0_ffn1_wo, l0_mix_norm, l0_mix_win, l0_gmlp_vnorm, l0_gmlp_ws, l0_gmlp_bs, l0_mix_wout, l0_ffn2_norm, l0_ffn2_wi, l0_ffn2_wo, l1_ffn1_norm, l1_ffn1_wi, l1_ffn1_wo, l1_mix_norm, l1_mix_win, l1_swa_sinks, l1_mix_wout, l1_ffn2_norm, l1_ffn2_wi, l1_ffn2_wo, l2_ffn1_norm, l2_ffn1_wi, l2_ffn1_wo, l2_mix_norm, l2_mix_win, l2_fox_bf, l2_mix_wout, l2_ffn2_norm, l2_ffn2_wi, l2_ffn2_wo, l3_ffn1_norm, l3_ffn1_wi, l3_ffn1_wo, l3_mix_norm, l3_mix_win, l3_gmlp_vnorm, l3_gmlp_ws, l3_gmlp_bs, l3_mix_wout, l3_ffn2_norm, l3_ffn2_wi, l3_ffn2_wo, final_norm, loss_target, m_l0_ffn1_norm, m_l0_ffn1_wi, m_l0_ffn1_wo, m_l0_mix_norm, m_l0_mix_win, m_l0_gmlp_vnorm, m_l0_gmlp_ws, m_l0_gmlp_bs, m_l0_mix_wout, m_l0_ffn2_norm, m_l0_ffn2_wi, m_l0_ffn2_wo, m_l1_ffn1_norm, m_l1_ffn1_wi, m_l1_ffn1_wo, m_l1_mix_norm, m_l1_mix_win, m_l1_swa_sinks, m_l1_mix_wout, m_l1_ffn2_norm, m_l1_ffn2_wi, m_l1_ffn2_wo, m_l2_ffn1_norm, m_l2_ffn1_wi, m_l2_ffn1_wo, m_l2_mix_norm, m_l2_mix_win, m_l2_fox_bf, m_l2_mix_wout, m_l2_ffn2_norm, m_l2_ffn2_wi, m_l2_ffn2_wo, m_l3_ffn1_norm, m_l3_ffn1_wi, m_l3_ffn1_wo, m_l3_mix_norm, m_l3_mix_win, m_l3_gmlp_vnorm, m_l3_gmlp_ws, m_l3_gmlp_bs, m_l3_mix_wout, m_l3_ffn2_norm, m_l3_ffn2_wi, m_l3_ffn2_wo, m_final_norm, v_l0_ffn1_norm, v_l0_ffn1_wi, v_l0_ffn1_wo, v_l0_mix_norm, v_l0_mix_win, v_l0_gmlp_vnorm, v_l0_gmlp_ws, v_l0_gmlp_bs, v_l0_mix_wout, v_l0_ffn2_norm, v_l0_ffn2_wi, v_l0_ffn2_wo, v_l1_ffn1_norm, v_l1_ffn1_wi, v_l1_ffn1_wo, v_l1_mix_norm, v_l1_mix_win, v_l1_swa_sinks, v_l1_mix_wout, v_l1_ffn2_norm, v_l1_ffn2_wi, v_l1_ffn2_wo, v_l2_ffn1_norm, v_l2_ffn1_wi, v_l2_ffn1_wo, v_l2_mix_norm, v_l2_mix_win, v_l2_fox_bf, v_l2_mix_wout, v_l2_ffn2_norm, v_l2_ffn2_wi, v_l2_ffn2_wo, v_l3_ffn1_norm, v_l3_ffn1_wi, v_l3_ffn1_wo, v_l3_mix_norm, v_l3_mix_win, v_l3_gmlp_vnorm, v_l3_gmlp_ws, v_l3_gmlp_bs, v_l3_mix_wout, v_l3_ffn2_norm, v_l3_ffn2_wi, v_l3_ffn2_wo, v_final_norm):
    return _step(dict(locals()))
```

```python
import functools
import math

import jax
import jax.numpy as jnp
from jax import lax
from jax.experimental import pallas as pl
from jax.experimental.pallas import tpu as pltpu

F32 = jnp.float32
BF16 = jnp.bfloat16

NORM_EPS = 1e-5
NEG_INF = -1e30
BLOCK = 128
GMLP_GROUPS = 16
SWA_HEAD_DIM = 64
SWA_GROUP = 8
ROPE_DIM = SWA_HEAD_DIM // 4
ROPE_THETA = 500000.0
FOX_HEAD_DIM = 128
ADAM_LR = 0.001
ADAM_B1 = 0.9
ADAM_B2 = 0.999
ADAM_EPS = 1e-08
ADAM_WD = 0.01
ADAM_STEP = 10
N_CHIPS = 4
N_DEV = 8
LANES = 128
VMEM_LIMIT = 56 * 1024 * 1024
MESH = pl.DeviceIdType.MESH
HBM = pl.BlockSpec(memory_space=pltpu.HBM)
SEM = pl.BlockSpec(memory_space=pltpu.SEMAPHORE)
EFFECT = pltpu.SideEffectType.DATAFLOW_SIDE_EFFECTING

MM_TILES = (1024, 1408, 896, 640, 512, 384, 256, 128)
K_TILES = (2816,) + MM_TILES


def _pick(n, prefs):
    for p in prefs:
        if p <= n and n % p == 0:
            return p
    return n


def _params(*sem):
    return pltpu.CompilerParams(dimension_semantics=sem or None, vmem_limit_bytes=VMEM_LIMIT)


def _cols(arr):
    return arr.shape[-1] * (arr.shape[0] if arr.ndim == 3 else 1)


def _mat_spec(arr, rb, cb, ridx, cidx):
    if arr.ndim == 2:
        return pl.BlockSpec((rb, cb), lambda j, i, k: (ridx(j, i, k), cidx(j, i, k)))
    per = arr.shape[2] // cb
    return pl.BlockSpec((None, rb, cb),
                        lambda j, i, k: (cidx(j, i, k) // per, ridx(j, i, k), cidx(j, i, k) % per))


def _matmul(a, b, *, name, out_dtype, ta=False, tb=False, out_shards=1, scale=1.0, resid=None, dep=None):
    m_dim, k_dim = (a.shape[1], a.shape[0]) if ta else a.shape
    n_dim = b.shape[-2] if tb else _cols(b)
    assert k_dim == (_cols(b) if tb else b.shape[-2]), (a.shape, b.shape, ta, tb)
    n_unit = n_dim // out_shards
    if b.ndim == 3 and not tb:
        n_unit = math.gcd(n_unit, b.shape[2])
    k_unit = b.shape[2] if (b.ndim == 3 and tb) else k_dim
    bm = _pick(m_dim, MM_TILES)
    bn = _pick(n_unit, MM_TILES)
    bk = k_unit if k_unit <= 2048 else _pick(k_unit, K_TILES)
    nk = k_dim // bk
    i_of, j_of, k_of = (lambda j, i, k: i), (lambda j, i, k: j), (lambda j, i, k: k)
    a_spec = _mat_spec(a, bk, bm, k_of, i_of) if ta else _mat_spec(a, bm, bk, i_of, k_of)
    b_spec = _mat_spec(b, bn, bk, j_of, k_of) if tb else _mat_spec(b, bk, bn, k_of, j_of)
    out_shape = (m_dim, n_dim) if out_shards == 1 else (out_shards, m_dim, n_dim // out_shards)
    out = jax.ShapeDtypeStruct(out_shape, out_dtype)
    o_spec = _mat_spec(out, bm, bn, i_of, j_of)
    dims = (((0 if ta else 1,), (1 if tb else 0,)), ((), ()))
    operands, in_specs = [a, b], [a_spec, b_spec]
    if resid is not None:
        operands.append(resid)
        in_specs.append(_mat_spec(resid, bm, bn, i_of, j_of))
    if dep is not None:
        operands.append(dep)
        in_specs.append(pl.BlockSpec(dep.shape, lambda j, i, k: (0, 0)))
    n_in = len(operands)

    def body(*refs):
        a_ref, b_ref = refs[0], refs[1]
        r_ref = refs[2] if resid is not None else None
        o_ref = refs[n_in]
        part = lax.dot_general(a_ref[...].astype(BF16), b_ref[...].astype(BF16), dims,
                               preferred_element_type=F32)

        def finish(acc):
            val = acc * scale if scale != 1.0 else acc
            if r_ref is not None:
                val = r_ref[...] + val
            o_ref[...] = val.astype(o_ref.dtype)

        if nk == 1:
            finish(part)
        else:
            acc_ref = refs[-1]
            k = pl.program_id(2)

            @pl.when(k == 0)
            def _():
                acc_ref[...] = part

            @pl.when(k > 0)
            def _():
                acc_ref[...] += part

            @pl.when(k == nk - 1)
            def _():
                finish(acc_ref[...])

    return pl.pallas_call(
        body, name=name, grid=(n_dim // bn, m_dim // bm, nk),
        in_specs=in_specs, out_specs=o_spec, out_shape=out,
        scratch_shapes=[pltpu.VMEM((bm, bn), F32)] if nk > 1 else [],
        compiler_params=_params("parallel", "parallel", "arbitrary"),
    )(*operands)


def _row_block(rows, width, itemsize=4, budget=2 << 20):
    best = None
    for br in range(16, rows + 1, 16):
        if rows % br == 0 and br * width * itemsize <= budget:
            best = br
    return best or rows


def _rms_fwd(h, g, name, dep=None):
    s_len, d = h.shape
    br = _row_block(s_len, d)

    def body(h_ref, g_ref, *rest):
        o_ref = rest[-1]
        x = h_ref[...]
        r = lax.rsqrt(jnp.mean(x * x, axis=-1, keepdims=True) + NORM_EPS)
        o_ref[...] = (x * r * g_ref[...]).astype(BF16)

    spec = pl.BlockSpec((br, d), lambda i: (i, 0))
    operands = [h, g.reshape(1, d)] + ([dep] if dep is not None else [])
    in_specs = [spec, pl.BlockSpec((1, d), lambda i: (0, 0))]
    if dep is not None:
        in_specs.append(pl.BlockSpec(dep.shape, lambda i: (0, 0)))
    return pl.pallas_call(body, name=name, grid=(s_len // br,), in_specs=in_specs, out_specs=spec,
                          out_shape=jax.ShapeDtypeStruct((s_len, d), BF16),
                          compiler_params=_params("parallel"))(*operands)


def _rms_bwd_rows(x, g, dn):
    r = lax.rsqrt(jnp.mean(x * x, axis=-1, keepdims=True) + NORM_EPS)
    xhat = x * r
    gdn = dn * g
    dx = r * (gdn - xhat * jnp.mean(gdn * xhat, axis=-1, keepdims=True))
    return dx, dn * xhat


def _norm_bwd(h, g, dn, dres, name):
    s_len, d = h.shape
    br = _row_block(s_len, d, budget=1 << 20)

    def body(h_ref, g_ref, dn_ref, dres_ref, dh_ref, dg_ref):
        dx, dg_rows = _rms_bwd_rows(h_ref[...], g_ref[...], dn_ref[...].astype(F32))
        dh_ref[...] = dres_ref[...] + dx

        @pl.when(pl.program_id(0) == 0)
        def _():
            dg_ref[...] = jnp.zeros_like(dg_ref)

        dg_ref[...] += jnp.sum(dg_rows, axis=0, keepdims=True)

    spec = pl.BlockSpec((br, d), lambda i: (i, 0))
    vec = pl.BlockSpec((1, d), lambda i: (0, 0))
    return pl.pallas_call(body, name=name, grid=(s_len // br,),
                          in_specs=[spec, vec, spec, spec], out_specs=[spec, vec],
                          out_shape=[jax.ShapeDtypeStruct((s_len, d), F32),
                                     jax.ShapeDtypeStruct((1, d), F32)],
                          compiler_params=_params("arbitrary"))(h, g.reshape(1, d), dn, dres)


def _sigmoid(x):
    return 0.5 * (1.0 + jnp.tanh(0.5 * x))


def _swiglu_fwd(z, name):
    s_len, f2 = z.shape
    f = f2 // 2
    br = _row_block(s_len, f2, itemsize=2, budget=6 << 20)
    fc = _pick(f, MM_TILES)

    def body(z_ref, a_ref):
        for c0 in range(0, f, fc):
            gate = z_ref[:, c0:c0 + fc].astype(F32)
            up = z_ref[:, f + c0:f + c0 + fc].astype(F32)
            a_ref[:, c0:c0 + fc] = (gate * _sigmoid(gate) * up).astype(BF16)

    return pl.pallas_call(body, name=name, grid=(s_len // br,),
                          in_specs=[pl.BlockSpec((br, f2), lambda i: (i, 0))],
                          out_specs=pl.BlockSpec((br, f), lambda i: (i, 0)),
                          out_shape=jax.ShapeDtypeStruct((s_len, f), BF16),
                          compiler_params=_params("parallel"))(z)


def _swiglu_bwd(z, da, name):
    s_len, f2 = z.shape
    f = f2 // 2
    br = _row_block(s_len, f2, itemsize=2, budget=6 << 20)
    fc = _pick(f, MM_TILES)

    def body(z_ref, da_ref, dz_ref):
        for c0 in range(0, f, fc):
            gate = z_ref[:, c0:c0 + fc].astype(F32)
            up = z_ref[:, f + c0:f + c0 + fc].astype(F32)
            d = da_ref[:, c0:c0 + fc].astype(F32)
            sig = _sigmoid(gate)
            dz_ref[:, c0:c0 + fc] = (d * up * (sig * (1.0 + gate * (1.0 - sig)))).astype(BF16)
            dz_ref[:, f + c0:f + c0 + fc] = (d * gate * sig).astype(BF16)

    return pl.pallas_call(body, name=name, grid=(s_len // br,),
                          in_specs=[pl.BlockSpec((br, f2), lambda i: (i, 0)),
                                    pl.BlockSpec((br, f), lambda i: (i, 0))],
                          out_specs=pl.BlockSpec((br, f2), lambda i: (i, 0)),
                          out_shape=jax.ShapeDtypeStruct((s_len, f2), BF16),
                          compiler_params=_params("parallel"))(z, da)


def _loss_head(h, g, target, name):
    s_len, d = h.shape
    br = _row_block(s_len, d, budget=1 << 20)

    def body(h_ref, g_ref, t_ref, loss_ref, dh_ref, dg_ref):
        x = h_ref[...]
        gain = g_ref[...]
        r = lax.rsqrt(jnp.mean(x * x, axis=-1, keepdims=True) + NORM_EPS)
        err = x * r * gain - t_ref[...]
        part = 0.5 * jnp.sum(jnp.mean(err * err, axis=-1, keepdims=True), axis=0, keepdims=True)
        dx, dg_rows = _rms_bwd_rows(x, gain, err * (1.0 / d))
        dh_ref[...] = dx

        @pl.when(pl.program_id(0) == 0)
        def _():
            dg_ref[...] = jnp.zeros_like(dg_ref)
            loss_ref[...] = jnp.zeros_like(loss_ref)

        dg_ref[...] += jnp.sum(dg_rows, axis=0, keepdims=True)
        loss_ref[...] += jnp.broadcast_to(part, loss_ref.shape)

    spec = pl.BlockSpec((br, d), lambda i: (i, 0))
    vec = pl.BlockSpec((1, d), lambda i: (0, 0))
    one = pl.BlockSpec((1, LANES), lambda i: (0, 0))
    loss, dh, dg = pl.pallas_call(
        body, name=name, grid=(s_len // br,), in_specs=[spec, vec, spec],
        out_specs=[one, spec, vec],
        out_shape=[jax.ShapeDtypeStruct((1, LANES), F32), jax.ShapeDtypeStruct((s_len, d), F32),
                   jax.ShapeDtypeStruct((1, d), F32)],
        compiler_params=_params("arbitrary"))(h, g.reshape(1, d), target)
    return loss[0, 0], dh, dg


def _adamw(w, g, m, v, name):
    rows, width = w.shape
    br = _row_block(rows, width, budget=1 << 20)
    c1 = 1.0 - ADAM_B1 ** ADAM_STEP
    c2 = 1.0 - ADAM_B2 ** ADAM_STEP

    def body(w_ref, g_ref, m_ref, v_ref, d_ref, nm_ref, nv_ref):
        grad = g_ref[...]
        new_m = ADAM_B1 * m_ref[...] + (1.0 - ADAM_B1) * grad
        new_v = ADAM_B2 * v_ref[...] + (1.0 - ADAM_B2) * (grad * grad)
        d_ref[...] = -ADAM_LR * ((new_m / c1) / (jnp.sqrt(new_v / c2) + ADAM_EPS) + ADAM_WD * w_ref[...])
        nm_ref[...] = new_m
        nv_ref[...] = new_v

    spec = pl.BlockSpec((br, width), lambda i: (i, 0))
    shp = jax.ShapeDtypeStruct(w.shape, F32)
    return pl.pallas_call(body, name=name, grid=(rows // br,), in_specs=[spec] * 4,
                          out_specs=[spec] * 3, out_shape=[shp] * 3,
                          compiler_params=_params("parallel"))(w, g, m, v)


def _gelu(x):
    return 0.5 * x * (1.0 + lax.erf(x * (2.0 ** -0.5)))


def _gelu_grad(x):
    return 0.5 * (1.0 + lax.erf(x * (2.0 ** -0.5))) + x * jnp.exp(-0.5 * x * x) * ((2.0 * math.pi) ** -0.5)


def _tril_mask():
    row = lax.broadcasted_iota(jnp.int32, (BLOCK, BLOCK), 0)
    col = lax.broadcasted_iota(jnp.int32, (BLOCK, BLOCK), 1)
    return col <= row


def _gmlp_specs(s_len, d):
    gw = d // GMLP_GROUPS
    zp = pl.BlockSpec((BLOCK, 2 * d), lambda i: (i, 0))
    row = pl.BlockSpec((BLOCK, d), lambda i: (i, 0))
    vec = pl.BlockSpec((1, d), lambda i: (0, 0))
    ws = pl.BlockSpec((GMLP_GROUPS, BLOCK, BLOCK), lambda i: (0, 0, 0))
    bst = pl.BlockSpec((BLOCK, GMLP_GROUPS), lambda i: (0, 0))
    return gw, zp, row, vec, ws, bst


def _gmlp_fwd(zp, vgain, ws, bs, name):
    s_len, d2 = zp.shape
    d = d2 // 2
    gw, zp_spec, row_spec, vec_spec, ws_spec, bst_spec = _gmlp_specs(s_len, d)

    def body(zp_ref, vg_ref, ws_ref, bst_ref, y_ref):
        u = _gelu(zp_ref[:, :d].astype(F32))
        vv = _gelu(zp_ref[:, d:].astype(F32))
        r = lax.rsqrt(jnp.mean(vv * vv, axis=-1, keepdims=True) + NORM_EPS)
        vn = (vv * r * vg_ref[...]).astype(BF16)
        mask = _tril_mask()
        for g in range(GMLP_GROUPS):
            cols = slice(g * gw, (g + 1) * gw)
            wg = jnp.where(mask, ws_ref[g], 0.0).astype(BF16)
            mixed = jnp.dot(wg, vn[:, cols], preferred_element_type=F32) + bst_ref[:, g:g + 1]
            y_ref[:, cols] = (u[:, cols] * mixed).astype(BF16)

    return pl.pallas_call(body, name=name, grid=(s_len // BLOCK,),
                          in_specs=[zp_spec, vec_spec, ws_spec, bst_spec], out_specs=row_spec,
                          out_shape=jax.ShapeDtypeStruct((s_len, d), BF16),
                          compiler_params=_params("parallel"))(zp, vgain.reshape(1, d), ws, bs.T)


def _gmlp_bwd(zp, dy, vgain, ws, bs, name):
    s_len, d2 = zp.shape
    d = d2 // 2
    gw, zp_spec, row_spec, vec_spec, ws_spec, bst_spec = _gmlp_specs(s_len, d)

    def body(zp_ref, dy_ref, vg_ref, ws_ref, bst_ref, dzp_ref, dws_ref, dbst_ref, dvg_ref, dvn_ref):
        @pl.when(pl.program_id(0) == 0)
        def _():
            dws_ref[...] = jnp.zeros_like(dws_ref)
            dbst_ref[...] = jnp.zeros_like(dbst_ref)
            dvg_ref[...] = jnp.zeros_like(dvg_ref)

        zu = zp_ref[:, :d].astype(F32)
        zv = zp_ref[:, d:].astype(F32)
        u = _gelu(zu)
        vv = _gelu(zv)
        r = lax.rsqrt(jnp.mean(vv * vv, axis=-1, keepdims=True) + NORM_EPS)
        vhat = vv * r
        gain = vg_ref[...]
        vn = (vhat * gain).astype(BF16)
        dyf = dy_ref[...].astype(F32)
        dmixed = dyf * u
        dmixed_b = dmixed.astype(BF16)
        mask = _tril_mask()
        lane = lax.broadcasted_iota(jnp.int32, (BLOCK, GMLP_GROUPS), 1)
        dbs_step = jnp.zeros((BLOCK, GMLP_GROUPS), F32)
        for g in range(GMLP_GROUPS):
            cols = slice(g * gw, (g + 1) * gw)
            wg = jnp.where(mask, ws_ref[g], 0.0).astype(BF16)
            mixed = jnp.dot(wg, vn[:, cols], preferred_element_type=F32) + bst_ref[:, g:g + 1]
            dzp_ref[:, cols] = (dyf[:, cols] * mixed * _gelu_grad(zu[:, cols])).astype(BF16)
            dm = dmixed_b[:, cols]
            dw = lax.dot_general(dm, vn[:, cols], (((1,), (1,)), ((), ())), preferred_element_type=F32)
            dws_ref[g] += jnp.where(mask, dw, 0.0)
            dbs_step = dbs_step + jnp.where(lane == g, jnp.sum(dmixed[:, cols], axis=-1, keepdims=True), 0.0)
            dvn_ref[:, cols] = lax.dot_general(wg, dm, (((0,), (0,)), ((), ())), preferred_element_type=F32)
        dbst_ref[...] += dbs_step
        dvn = dvn_ref[...]
        dvg_ref[...] += jnp.sum(dvn * vhat, axis=0, keepdims=True)
        dvhat = dvn * gain
        dvv = r * (dvhat - vhat * jnp.mean(dvhat * vhat, axis=-1, keepdims=True))
        dzp_ref[:, d:] = (dvv * _gelu_grad(zv)).astype(BF16)

    return pl.pallas_call(
        body, name=name, grid=(s_len // BLOCK,),
        in_specs=[zp_spec, row_spec, vec_spec, ws_spec, bst_spec],
        out_specs=[zp_spec, ws_spec, bst_spec, vec_spec],
        out_shape=[jax.ShapeDtypeStruct((s_len, d2), BF16), jax.ShapeDtypeStruct(ws.shape, F32),
                   jax.ShapeDtypeStruct((BLOCK, GMLP_GROUPS), F32), jax.ShapeDtypeStruct((1, d), F32)],
        scratch_shapes=[pltpu.VMEM((BLOCK, d), F32)],
        compiler_params=_params("arbitrary"))(zp, dy, vgain.reshape(1, d), ws, bs.T)


def _rope_tables(s_len, sign):
    half = ROPE_DIM // 2
    inv_freq = ROPE_THETA ** (-(jnp.arange(half, dtype=F32) * 2.0 / ROPE_DIM))
    ang = jnp.arange(s_len, dtype=F32)[:, None] * inv_freq[None, :]
    cos, sin = jnp.cos(ang), jnp.sin(ang) * sign
    pad = jnp.zeros((s_len, SWA_HEAD_DIM - ROPE_DIM), F32)
    zero = jnp.zeros_like(sin)
    cos_t = jnp.concatenate([cos, cos, pad + 1.0], axis=1)
    sin_up = jnp.concatenate([-sin, zero, pad], axis=1)
    sin_dn = jnp.concatenate([zero, sin, pad], axis=1)
    return [jnp.tile(t, (1, LANES // SWA_HEAD_DIM)) for t in (cos_t, sin_up, sin_dn)]


def _rotate(x, cos_t, sin_up, sin_dn):
    width = x.shape[-1]
    half = ROPE_DIM // 2
    reps = width // cos_t.shape[-1]
    if reps > 1:
        cos_t, sin_up, sin_dn = (jnp.tile(t, (1, reps)) for t in (cos_t, sin_up, sin_dn))
    elif reps == 0:
        cos_t, sin_up, sin_dn = (t[:, :width] for t in (cos_t, sin_up, sin_dn))
    return x * cos_t + pltpu.roll(x, width - half, 1) * sin_up + pltpu.roll(x, half, 1) * sin_dn


def _rope_fwd(qkv, name):
    s_len, total = qkv.shape
    wkv = total // (SWA_GROUP + 2)
    wq = SWA_GROUP * wkv
    br = _row_block(s_len, total, budget=2 << 20)
    tables = _rope_tables(s_len, 1.0)

    def body(q_ref, k_ref, v_ref, c_ref, su_ref, sd_ref, qo_ref, ko_ref, vo_ref):
        t = (c_ref[...], su_ref[...], sd_ref[...])
        qo_ref[...] = _rotate(q_ref[...], *t).astype(BF16)
        ko_ref[...] = _rotate(k_ref[...], *t).astype(BF16)
        vo_ref[...] = v_ref[...].astype(BF16)

    qs = pl.BlockSpec((br, wq), lambda i: (i, 0))
    ks = pl.BlockSpec((br, wkv), lambda i: (i, SWA_GROUP))
    vs = pl.BlockSpec((br, wkv), lambda i: (i, SWA_GROUP + 1))
    ts = pl.BlockSpec((br, LANES), lambda i: (i, 0))
    kv_out = pl.BlockSpec((br, wkv), lambda i: (i, 0))
    return pl.pallas_call(
        body, name=name, grid=(s_len // br,), in_specs=[qs, ks, vs, ts, ts, ts],
        out_specs=[qs, kv_out, kv_out],
        out_shape=[jax.ShapeDtypeStruct((s_len, wq), BF16), jax.ShapeDtypeStruct((s_len, wkv), BF16),
                   jax.ShapeDtypeStruct((s_len, wkv), BF16)],
        compiler_params=_params("parallel"))(qkv, qkv, qkv, *tables)


def _rope_bwd(dq, dk, dv, name):
    s_len, wq = dq.shape
    wkv = dk.shape[1]
    br = _row_block(s_len, wq + 2 * wkv, budget=2 << 20)
    tables = _rope_tables(s_len, -1.0)

    def body(q_ref, k_ref, v_ref, c_ref, su_ref, sd_ref, o_ref):
        t = (c_ref[...], su_ref[...], sd_ref[...])
        o_ref[:, :wq] = _rotate(q_ref[...], *t).astype(BF16)
        o_ref[:, wq:wq + wkv] = _rotate(k_ref[...], *t).astype(BF16)
        o_ref[:, wq + wkv:] = v_ref[...].astype(BF16)

    qs = pl.BlockSpec((br, wq), lambda i: (i, 0))
    kvs = pl.BlockSpec((br, wkv), lambda i: (i, 0))
    ts = pl.BlockSpec((br, LANES), lambda i: (i, 0))
    return pl.pallas_call(
        body, name=name, grid=(s_len // br,), in_specs=[qs, kvs, kvs, ts, ts, ts],
        out_specs=pl.BlockSpec((br, wq + 2 * wkv), lambda i: (i, 0)),
        out_shape=jax.ShapeDtypeStruct((s_len, wq + 2 * wkv), BF16),
        compiler_params=_params("parallel"))(dq, dk, dv, *tables)


def _swa_valid(i):
    row = lax.broadcasted_iota(jnp.int32, (BLOCK, 2 * BLOCK), 0)
    col = lax.broadcasted_iota(jnp.int32, (BLOCK, 2 * BLOCK), 1)
    return (col - BLOCK <= row) & (row < col) & ((col >= BLOCK) | (i > 0))


def _swa_specs(wq, wkv):
    q_spec = pl.BlockSpec((BLOCK, wq), lambda i: (i, 0))
    cur = pl.BlockSpec((BLOCK, wkv), lambda i: (i, 0))
    prev = pl.BlockSpec((BLOCK, wkv), lambda i: (jnp.maximum(i - 1, 0), 0))
    sink = pl.BlockSpec(memory_space=pltpu.SMEM)
    return q_spec, cur, prev, sink


def _swa_probs(q_h, k_cat, valid, sink):
    s = lax.dot_general(q_h, k_cat, (((1,), (1,)), ((), ())), preferred_element_type=F32)
    s = jnp.where(valid, s * (SWA_HEAD_DIM ** -0.5), NEG_INF)
    m = jnp.maximum(jnp.max(s, axis=-1, keepdims=True), sink)
    p = jnp.exp(s - m)
    e_sink = jnp.exp(sink - m)
    denom = jnp.sum(p, axis=-1, keepdims=True) + e_sink
    return p / denom, e_sink / denom


def _swa_fwd(q, k, v, sinks, name):
    s_len, wq = q.shape
    wkv = k.shape[1]
    hd = SWA_HEAD_DIM
    q_spec, cur, prev, sink_spec = _swa_specs(wq, wkv)

    def body(q_ref, kc_ref, kp_ref, vc_ref, vp_ref, sink_ref, o_ref):
        valid = _swa_valid(pl.program_id(0))
        for j in range(wkv // hd):
            lanes = slice(j * hd, (j + 1) * hd)
            k_cat = jnp.concatenate([kp_ref[:, lanes], kc_ref[:, lanes]], axis=0)
            v_cat = jnp.concatenate([vp_ref[:, lanes], vc_ref[:, lanes]], axis=0)
            for hh in range(SWA_GROUP):
                h = j * SWA_GROUP + hh
                pn, _ = _swa_probs(q_ref[:, h * hd:(h + 1) * hd], k_cat, valid, sink_ref[h])
                o_ref[:, h * hd:(h + 1) * hd] = jnp.dot(
                    pn.astype(BF16), v_cat, preferred_element_type=F32).astype(BF16)

    return pl.pallas_call(body, name=name, grid=(s_len // BLOCK,),
                          in_specs=[q_spec, cur, prev, cur, prev, sink_spec], out_specs=q_spec,
                          out_shape=jax.ShapeDtypeStruct((s_len, wq), BF16),
                          compiler_params=_params("parallel"))(q, k, k, v, v, sinks)


def _swa_bwd(q, k, v, sinks, do, name):
    s_len, wq = q.shape
    wkv = k.shape[1]
    hd = SWA_HEAD_DIM
    q_spec, cur, prev, sink_spec = _swa_specs(wq, wkv)
    full = pl.BlockSpec((s_len, wkv), lambda i: (0, 0))
    one = pl.BlockSpec((1, LANES), lambda i: (0, 0))
    scale = hd ** -0.5

    def body(q_ref, kc_ref, kp_ref, vc_ref, vp_ref, sink_ref, do_ref, dq_ref, dk_ref, dv_ref, ds_ref):
        i = pl.program_id(0)

        @pl.when(i == 0)
        def _():
            dk_ref[...] = jnp.zeros_like(dk_ref)
            dv_ref[...] = jnp.zeros_like(dv_ref)
            ds_ref[...] = jnp.zeros_like(ds_ref)

        valid = _swa_valid(i)
        lane = lax.broadcasted_iota(jnp.int32, (1, LANES), 1)
        dsink_step = jnp.zeros((1, LANES), F32)
        rows_prev = pl.ds(pl.multiple_of(jnp.maximum(i - 1, 0) * BLOCK, BLOCK), BLOCK)
        rows_cur = pl.ds(pl.multiple_of(i * BLOCK, BLOCK), BLOCK)
        for j in range(wkv // hd):
            lanes = slice(j * hd, (j + 1) * hd)
            k_cat = jnp.concatenate([kp_ref[:, lanes], kc_ref[:, lanes]], axis=0)
            v_cat = jnp.concatenate([vp_ref[:, lanes], vc_ref[:, lanes]], axis=0)
            dk_cat = jnp.zeros((2 * BLOCK, hd), F32)
            dv_cat = jnp.zeros((2 * BLOCK, hd), F32)
            for hh in range(SWA_GROUP):
                h = j * SWA_GROUP + hh
                q_h = q_ref[:, h * hd:(h + 1) * hd]
                do_h = do_ref[:, h * hd:(h + 1) * hd]
                pn, p_sink = _swa_probs(q_h, k_cat, valid, sink_ref[h])
                dpn = lax.dot_general(do_h, v_cat, (((1,), (1,)), ((), ())), preferred_element_type=F32)
                delta = jnp.sum(dpn * pn, axis=-1, keepdims=True)
                ds = (pn * (dpn - delta) * scale).astype(BF16)
                dsink_h = -jnp.sum(p_sink * delta, axis=0, keepdims=True)
                dsink_step = dsink_step + jnp.where(lane == h, dsink_h, 0.0)
                dq_ref[:, h * hd:(h + 1) * hd] = jnp.dot(ds, k_cat, preferred_element_type=F32)
                dk_cat = dk_cat + lax.dot_general(ds, q_h, (((0,), (0,)), ((), ())),
                                                  preferred_element_type=F32)
                dv_cat = dv_cat + lax.dot_general(pn.astype(BF16), do_h, (((0,), (0,)), ((), ())),
                                                  preferred_element_type=F32)
            dk_ref[rows_prev, lanes] += dk_cat[:BLOCK]
            dk_ref[rows_cur, lanes] += dk_cat[BLOCK:]
            dv_ref[rows_prev, lanes] += dv_cat[:BLOCK]
            dv_ref[rows_cur, lanes] += dv_cat[BLOCK:]
        ds_ref[...] += dsink_step

    return pl.pallas_call(
        body, name=name, grid=(s_len // BLOCK,),
        in_specs=[q_spec, cur, prev, cur, prev, sink_spec, q_spec],
        out_specs=[q_spec, full, full, one],
        out_shape=[jax.ShapeDtypeStruct((s_len, wq), F32), jax.ShapeDtypeStruct((s_len, wkv), F32),
                   jax.ShapeDtypeStruct((s_len, wkv), F32), jax.ShapeDtypeStruct((1, LANES), F32)],
        compiler_params=_params("arbitrary"))(q, k, k, v, v, sinks, do)


def _log_sigmoid(x):
    return jnp.minimum(x, 0.0) - jnp.log(1.0 + jnp.exp(-jnp.abs(x)))


def _tri_ones(lower):
    row = lax.broadcasted_iota(jnp.int32, (BLOCK, BLOCK), 0)
    col = lax.broadcasted_iota(jnp.int32, (BLOCK, BLOCK), 1)
    return jnp.where((col <= row) if lower else (col >= row), 1.0, 0.0).astype(F32)


def _fox_decay(proj, bf_row, fl_block, name):
    s_len = proj.shape[0]
    nchunk = s_len // BLOCK

    def body(fl_ref, bf_ref, dec_ref):
        tri = _tri_ones(True)
        carry = jnp.zeros((1, LANES), F32)
        for c in range(nchunk):
            rows = slice(c * BLOCK, (c + 1) * BLOCK)
            log_f = _log_sigmoid(fl_ref[rows, :] + bf_ref[...])
            loc = jnp.dot(tri, log_f, preferred_element_type=F32, precision=lax.Precision.HIGHEST) + carry
            dec_ref[rows, :] = loc
            carry = loc[BLOCK - 1:BLOCK, :]

    return pl.pallas_call(
        body, name=name, grid=(1,),
        in_specs=[pl.BlockSpec((s_len, LANES), lambda i: (0, fl_block)),
                  pl.BlockSpec((1, LANES), lambda i: (0, 0))],
        out_specs=pl.BlockSpec((s_len, LANES), lambda i: (0, 0)),
        out_shape=jax.ShapeDtypeStruct((s_len, LANES), F32),
        compiler_params=_params("arbitrary"))(proj, bf_row)


def _fox_decay_bwd(ddq, ddk, proj, bf_row, fl_block, heads, name):
    s_len = proj.shape[0]
    nchunk = s_len // BLOCK

    def body(ddq_ref, ddk_ref, fl_ref, bf_ref, dfl_ref, dbf_ref):
        tri = _tri_ones(False)
        lane_ok = lax.broadcasted_iota(jnp.int32, (BLOCK, LANES), 1) < heads
        carry = jnp.zeros((1, LANES), F32)
        dbf = jnp.zeros((1, LANES), F32)
        for c in reversed(range(nchunk)):
            rows = slice(c * BLOCK, (c + 1) * BLOCK)
            ddec = ddq_ref[rows, :] + ddk_ref[rows, :]
            dlog = jnp.dot(tri, ddec, preferred_element_type=F32, precision=lax.Precision.HIGHEST) + carry
            carry = dlog[0:1, :]
            dfl = jnp.where(lane_ok, dlog * _sigmoid(-(fl_ref[rows, :] + bf_ref[...])), 0.0)
            dfl_ref[rows, :] = dfl.astype(BF16)
            dbf = dbf + jnp.sum(dfl, axis=0, keepdims=True)
        dbf_ref[...] = dbf

    blk = pl.BlockSpec((s_len, LANES), lambda i: (0, 0))
    one = pl.BlockSpec((1, LANES), lambda i: (0, 0))
    return pl.pallas_call(
        body, name=name, grid=(1,),
        in_specs=[blk, blk, pl.BlockSpec((s_len, LANES), lambda i: (0, fl_block)), one],
        out_specs=[blk, one],
        out_shape=[jax.ShapeDtypeStruct((s_len, LANES), BF16), jax.ShapeDtypeStruct((1, LANES), F32)],
        compiler_params=_params("arbitrary"))(ddq, ddk, proj, bf_row)


def _fox_scores(q, k, decq, deck, i, bq):
    s_len = k.shape[0]
    s = lax.dot_general(q, k, (((1,), (1,)), ((), ())), preferred_element_type=F32)
    s = s * (FOX_HEAD_DIM ** -0.5) + decq - deck
    row = lax.broadcasted_iota(jnp.int32, (bq, s_len), 0) + i * bq
    col = lax.broadcasted_iota(jnp.int32, (bq, s_len), 1)
    s = jnp.where(col <= row, s, NEG_INF)
    p = jnp.exp(s - jnp.max(s, axis=-1, keepdims=True))
    return p / jnp.sum(p, axis=-1, keepdims=True)


def _fox_key_spans(s_len, bq):
    n_span = min(4, s_len // bq)
    return [(j + 1) * (s_len // n_span) for j in range(n_span)]


def _fox_span_of(i, s_len, bq):
    span = s_len // min(4, s_len // bq)
    return ((i * bq) // span + 1) * span


def _fox_specs(s_len, heads, bq):
    hd = FOX_HEAD_DIM
    q_spec = pl.BlockSpec((bq, hd), lambda h, i: (i, h))
    k_spec = pl.BlockSpec((s_len, hd), lambda h, i: (0, heads + h))
    v_spec = pl.BlockSpec((s_len, hd), lambda h, i: (0, 2 * heads + h))
    dq_spec = pl.BlockSpec((None, bq, 1), lambda h, i: (h, i, 0))
    dk_spec = pl.BlockSpec((None, 1, s_len), lambda h, i: (h, 0, 0))
    return q_spec, k_spec, v_spec, dq_spec, dk_spec


def _fox_fwd(proj, decq, deck, heads, name):
    s_len = proj.shape[0]
    bq = _pick(s_len, (256, 128))
    q_spec, k_spec, v_spec, dq_spec, dk_spec = _fox_specs(s_len, heads, bq)

    def body(q_ref, k_ref, v_ref, decq_ref, deck_ref, o_ref):
        i = pl.program_id(1)
        for klen in _fox_key_spans(s_len, bq):
            @pl.when(_fox_span_of(i, s_len, bq) == klen)
            def _(klen=klen):
                pn = _fox_scores(q_ref[...].astype(BF16), k_ref[:klen, :].astype(BF16), decq_ref[...],
                                 deck_ref[:, :klen], i, bq)
                o_ref[...] = jnp.dot(pn.astype(BF16), v_ref[:klen, :].astype(BF16),
                                     preferred_element_type=F32).astype(BF16)

    return pl.pallas_call(body, name=name, grid=(heads, s_len // bq),
                          in_specs=[q_spec, k_spec, v_spec, dq_spec, dk_spec], out_specs=q_spec,
                          out_shape=jax.ShapeDtypeStruct((s_len, heads * FOX_HEAD_DIM), BF16),
                          compiler_params=_params("parallel", "parallel"))(proj, proj, proj, decq, deck)


def _fox_bwd(proj, decq, deck, do, heads, name):
    s_len = proj.shape[0]
    d = heads * FOX_HEAD_DIM
    bq = _pick(s_len, (256, 128))
    q_spec, k_spec, v_spec, dq_spec, dk_spec = _fox_specs(s_len, heads, bq)
    acc_spec = pl.BlockSpec((s_len, FOX_HEAD_DIM), lambda h, i: (0, h))
    scale = FOX_HEAD_DIM ** -0.5

    def body(q_ref, k_ref, v_ref, decq_ref, deck_ref, do_ref, dq_ref, dk_ref, dv_ref, ddq_ref, ddk_ref):
        i = pl.program_id(1)

        @pl.when(i == 0)
        def _():
            dk_ref[...] = jnp.zeros_like(dk_ref)
            dv_ref[...] = jnp.zeros_like(dv_ref)
            ddk_ref[...] = jnp.zeros_like(ddk_ref)

        q = q_ref[...].astype(BF16)
        do_b = do_ref[...]
        for klen in _fox_key_spans(s_len, bq):
            @pl.when(_fox_span_of(i, s_len, bq) == klen)
            def _(klen=klen):
                k = k_ref[:klen, :].astype(BF16)
                pn = _fox_scores(q, k, decq_ref[...], deck_ref[:, :klen], i, bq)
                dpn = lax.dot_general(do_b, v_ref[:klen, :].astype(BF16), (((1,), (1,)), ((), ())),
                                      preferred_element_type=F32)
                ds = pn * (dpn - jnp.sum(dpn * pn, axis=-1, keepdims=True))
                ddq_ref[...] = jnp.sum(ds, axis=-1, keepdims=True)
                ddk_ref[:, :klen] -= jnp.sum(ds, axis=0, keepdims=True)
                ds_b = (ds * scale).astype(BF16)
                dq_ref[...] = jnp.dot(ds_b, k, preferred_element_type=F32).astype(BF16)
                dk_ref[:klen, :] += lax.dot_general(ds_b, q, (((0,), (0,)), ((), ())),
                                                    preferred_element_type=F32)
                dv_ref[:klen, :] += lax.dot_general(pn.astype(BF16), do_b, (((0,), (0,)), ((), ())),
                                                    preferred_element_type=F32)

    return pl.pallas_call(
        body, name=name, grid=(heads, s_len // bq),
        in_specs=[q_spec, k_spec, v_spec, dq_spec, dk_spec, q_spec],
        out_specs=[q_spec, acc_spec, acc_spec, dq_spec, dk_spec],
        out_shape=[jax.ShapeDtypeStruct((s_len, d), BF16), jax.ShapeDtypeStruct((s_len, d), F32),
                   jax.ShapeDtypeStruct((s_len, d), F32), jax.ShapeDtypeStruct((heads, s_len, 1), F32),
                   jax.ShapeDtypeStruct((heads, 1, s_len), F32)],
        compiler_params=_params("parallel", "arbitrary"))(proj, proj, proj, decq, deck, do)


def _place():
    x, y, c = lax.axis_index("x"), lax.axis_index("y"), lax.axis_index("c")
    chips = [(1 - x, y), (x, 1 - y), (1 - x, 1 - y)]
    return x, y, c, chips


def _remote(src, dst, send_sems, recv_sems, idx, to):
    return pltpu.make_async_remote_copy(src_ref=src, dst_ref=dst, send_sem=send_sems.at[idx],
                                        recv_sem=recv_sems.at[idx], device_id=to, device_id_type=MESH)


def _row_chunks(rows, want):
    for k in (want, want // 2, want // 4):
        if k >= 1 and rows % (16 * k) == 0:
            return [(j * (rows // k), rows // k) for j in range(k)]
    return [(0, rows)]


D2D_CHUNKS = 8


def _cast_into_slot(w, me, name, dep=None):
    rows, width = w.shape
    br = _row_block(rows, width, budget=4 << 20)

    def body(me_ref, w_ref, *rest):
        rest[-1][...] = w_ref[...].astype(BF16)

    in_specs = [pl.BlockSpec((br, width), lambda i, me_ref: (i, 0))]
    if dep is not None:
        in_specs.append(pl.BlockSpec(dep.shape, lambda i, me_ref: (0, 0)))
    return pl.pallas_call(
        body, name=name,
        grid_spec=pltpu.PrefetchScalarGridSpec(
            num_scalar_prefetch=1, grid=(rows // br,), in_specs=in_specs,
            out_specs=pl.BlockSpec((None, br, width), lambda i, me_ref: (me_ref[0], i, 0))),
        out_shape=jax.ShapeDtypeStruct((N_CHIPS, rows, width), BF16),
        compiler_params=_params("parallel"))(me, w, *([dep] if dep is not None else []))


def _hbm(arr):
    return pltpu.with_memory_space_constraint(arr, pltpu.HBM)


def _token_shape():
    return jax.ShapeDtypeStruct((8, LANES), F32)


def _add_pair(grad, got, c, name):
    _, half, width = got.shape
    br = _row_block(half, width, itemsize=2, budget=3 << 20)
    nb = half // br

    def body(c_ref, a_ref, b_ref, o_ref):
        o_ref[...] = (a_ref[...].astype(F32) + b_ref[...].astype(F32)).astype(BF16)

    spec = pl.BlockSpec((None, br, width), lambda j, i, c_ref: (j, i, 0))
    mine = pl.BlockSpec((None, br, width), lambda j, i, c_ref: (j, c_ref[0] * nb + i, 0))
    return pl.pallas_call(
        body, name=name,
        grid_spec=pltpu.PrefetchScalarGridSpec(num_scalar_prefetch=1, grid=(N_CHIPS, nb),
                                               in_specs=[mine, spec], out_specs=spec),
        out_shape=jax.ShapeDtypeStruct(got.shape, BF16),
        compiler_params=_params("parallel", "parallel"))(c, grad, got)


def _sum_chips(pair, others, me, name):
    _, rows, width = pair.shape
    br = _row_block(rows, width, itemsize=4, budget=3 << 20)

    def body(me_ref, p_ref, o3_ref, o_ref):
        acc = p_ref[...].astype(F32)
        for r in range(N_CHIPS - 1):
            acc = acc + o3_ref[r].astype(F32)
        o_ref[...] = acc

    return pl.pallas_call(
        body, name=name,
        grid_spec=pltpu.PrefetchScalarGridSpec(
            num_scalar_prefetch=1, grid=(rows // br,),
            in_specs=[pl.BlockSpec((None, br, width), lambda i, me_ref: (me_ref[0], i, 0)),
                      pl.BlockSpec((N_CHIPS - 1, br, width), lambda i, me_ref: (0, i, 0))],
            out_specs=pl.BlockSpec((br, width), lambda i, me_ref: (i, 0))),
        out_shape=jax.ShapeDtypeStruct((rows, width), F32),
        compiler_params=_params("parallel"))(me, pair, others)


def _adamw_halves(w, mine, theirs, m, v, c, name):
    rows, width = w.shape
    half = rows // 2
    br = _row_block(half, width, budget=3 << 19)
    nb = half // br
    c1 = 1.0 - ADAM_B1 ** ADAM_STEP
    c2 = 1.0 - ADAM_B2 ** ADAM_STEP

    def body(c_ref, w_ref, a_ref, b_ref, m_ref, v_ref, g_ref, d_ref, nm_ref, nv_ref):
        grad = jnp.where(pl.program_id(0) == c_ref[0], a_ref[...], b_ref[...])
        new_m = ADAM_B1 * m_ref[...] + (1.0 - ADAM_B1) * grad
        new_v = ADAM_B2 * v_ref[...] + (1.0 - ADAM_B2) * (grad * grad)
        g_ref[...] = grad
        d_ref[...] = -ADAM_LR * ((new_m / c1) / (jnp.sqrt(new_v / c2) + ADAM_EPS) + ADAM_WD * w_ref[...])
        nm_ref[...] = new_m
        nv_ref[...] = new_v

    full = pl.BlockSpec((br, width), lambda h, i, c_ref: (h * nb + i, 0))
    mine_spec = pl.BlockSpec((br, width), lambda h, i, c_ref: (jnp.where(h == c_ref[0], i, 0), 0))
    theirs_spec = pl.BlockSpec((br, width), lambda h, i, c_ref: (jnp.where(h == c_ref[0], 0, i), 0))
    shp = jax.ShapeDtypeStruct(w.shape, F32)
    return pl.pallas_call(
        body, name=name,
        grid_spec=pltpu.PrefetchScalarGridSpec(num_scalar_prefetch=1, grid=(2, nb),
                                               in_specs=[full, mine_spec, theirs_spec, full, full],
                                               out_specs=[full] * 4),
        out_shape=[shp] * 4,
        compiler_params=_params("parallel", "parallel"))(c, w, mine, theirs, m, v)


class _Transfer:
    def __init__(self, n_sems, build):
        self.n_sems, self.build = n_sems, build


def _copies(src_of, dst_of, land_of, rows, chunks, send, recv, idx, to):
    starts = [_remote(src_of(s, z), dst_of(s, z), send, recv, idx, to) for s, z in _row_chunks(rows, chunks)]
    return starts, _remote(src_of(0, rows), land_of(0, rows), send, recv, idx, to)


def _gather_direct(keys, shapes):
    def build(refs, send, recv):
        x, y, c, chips = _place()
        me = 2 * x + y
        out = []
        for t, key in enumerate(keys):
            half = shapes[t][1] // 2
            for r, chip in enumerate(chips[:2]):
                slot = 2 * chip[0] + chip[1]
                out.append(_copies(lambda s, z, key=key, half=half: refs[key].at[me, pl.ds(c * half + s, z)],
                                   lambda s, z, key=key, half=half: refs[key].at[me, pl.ds(c * half + s, z)],
                                   lambda s, z, key=key, half=half, slot=slot: refs[key].at[slot, pl.ds(c * half + s, z)],
                                   half, 1, send, recv, 2 * t + r, (*chip, c)))
        return out
    return _Transfer(2 * len(keys), build)


def _gather_relay(keys, shapes):
    def build(refs, send, recv):
        x, y, c, chips = _place()
        slot_x, slot_y, slot_d = (2 * ch[0] + ch[1] for ch in chips)
        src_slot = slot_y + c * (slot_x - slot_y)
        to = (x ^ (1 - c), y ^ c, c)
        out = []
        for t, key in enumerate(keys):
            half = shapes[t][1] // 2
            out.append(_copies(lambda s, z, key=key, half=half: refs[key].at[src_slot, pl.ds(c * half + s, z)],
                               lambda s, z, key=key, half=half: refs[key].at[src_slot, pl.ds(c * half + s, z)],
                               lambda s, z, key=key, half=half: refs[key].at[slot_d, pl.ds(c * half + s, z)],
                               half, 1, send, recv, t, to))
        return out
    return _Transfer(len(keys), build)


def _gather_pair(keys, shapes):
    def build(refs, send, recv):
        x, y, c, chips = _place()
        out = []
        for t, key in enumerate(keys):
            half = shapes[t][1] // 2
            for r, chip in enumerate(chips):
                slot = 2 * chip[0] + chip[1]
                mine = lambda s, z, key=key, half=half, slot=slot: refs[key].at[slot, pl.ds(c * half + s, z)]
                land = lambda s, z, key=key, half=half, slot=slot: refs[key].at[slot, pl.ds((1 - c) * half + s, z)]
                out.append(_copies(mine, mine, land, half, D2D_CHUNKS, send, recv, 3 * t + r, (x, y, 1 - c)))
        return out
    return _Transfer(3 * len(keys), build)


def _grad_pair(keys, lands, shapes):
    def build(refs, send, recv):
        x, y, c, _ = _place()
        out = []
        for t, (key, land) in enumerate(zip(keys, lands)):
            half = shapes[t][1] // 2
            for j in range(N_CHIPS):
                out.append(_copies(
                    lambda s, z, key=key, half=half, j=j: refs[key].at[j, pl.ds((1 - c) * half + s, z)],
                    lambda s, z, land=land, j=j: refs[land].at[j, pl.ds(s, z)],
                    lambda s, z, land=land, j=j: refs[land].at[j, pl.ds(s, z)],
                    half, 2, send, recv, N_CHIPS * t + j, (x, y, 1 - c)))
        return out
    return _Transfer(N_CHIPS * len(keys), build)


def _grad_chips(keys, lands, shapes):
    def build(refs, send, recv):
        x, y, c, chips = _place()
        out = []
        for t, (key, land) in enumerate(zip(keys, lands)):
            rows = shapes[t][1]
            for r, chip in enumerate(chips):
                slot = 2 * chip[0] + chip[1]
                out.append(_copies(lambda s, z, key=key, slot=slot: refs[key].at[slot, pl.ds(s, z)],
                                   lambda s, z, land=land, r=r: refs[land].at[r, pl.ds(s, z)],
                                   lambda s, z, land=land, r=r: refs[land].at[r, pl.ds(s, z)],
                                   rows, 1, send, recv, 3 * t + r, (*chip, c)))
        return out
    return _Transfer(3 * len(keys), build)


def _grad_join(keys, lands, shapes):
    def build(refs, send, recv):
        x, y, c, _ = _place()
        out = []
        for t, (key, land) in enumerate(zip(keys, lands)):
            out.append(_copies(lambda s, z, key=key: refs[key].at[pl.ds(s, z)],
                               lambda s, z, land=land: refs[land].at[pl.ds(s, z)],
                               lambda s, z, land=land: refs[land].at[pl.ds(s, z)],
                               shapes[t][0], D2D_CHUNKS, send, recv, t, (x, y, 1 - c)))
        return out
    return _Transfer(len(keys), build)


def _comm_call(name, arrays, waits, starts, after):
    keys = list(arrays)
    n, nw, ns = len(keys), len(waits), len(starts)

    def body(*refs):
        in_sems = refs[n:n + 2 * nw]
        base = n + 2 * nw + 1
        out_sems = refs[base:base + 2 * ns]
        bufs = dict(zip(keys, refs[base + 2 * ns:base + 2 * ns + n]))
        token = refs[base + 2 * ns + n]
        for k, (transfer, _, _) in enumerate(waits):
            for _, whole in transfer.build(bufs, in_sems[2 * k], in_sems[2 * k + 1]):
                whole.wait_send()
                whole.wait_recv()
        for k, transfer in enumerate(starts):
            for chunks, _ in transfer.build(bufs, out_sems[2 * k], out_sems[2 * k + 1]):
                for cp in chunks:
                    cp.start()
        token[...] = jnp.zeros_like(token)

    sem_shapes = []
    for transfer in starts:
        sem_shapes += [pltpu.SemaphoreType.DMA((transfer.n_sems,))] * 2
    operands = [_hbm(arrays[k]) for k in keys]
    for _, send, recv in waits:
        operands += [send, recv]
    res = pl.pallas_call(
        body, name=name, in_specs=[HBM] * n + [SEM] * (2 * nw) + [pl.BlockSpec(memory_space=pl.ANY)],
        out_specs=[SEM] * (2 * ns) + [HBM] * n + [pl.BlockSpec(memory_space=pltpu.VMEM)],
        out_shape=sem_shapes + [pltpu.HBM(arrays[k].shape, arrays[k].dtype) for k in keys] + [_token_shape()],
        input_output_aliases={t: 2 * ns + t for t in range(n)},
        compiler_params=pltpu.CompilerParams(has_side_effects=EFFECT),
    )(*operands, after)
    sems = [(res[2 * k], res[2 * k + 1]) for k in range(ns)]
    return dict(zip(keys, res[2 * ns:2 * ns + n])), sems, res[2 * ns + n]


def _device_gather(part_key, all_key, rows):
    def build(refs, send, recv):
        x, y, c, _ = _place()
        me = 4 * x + 2 * y + c
        out = []
        for r in range(1, N_DEV):
            peer = (x ^ (r >> 2), y ^ ((r >> 1) & 1), c ^ (r & 1))
            theirs = 4 * peer[0] + 2 * peer[1] + peer[2]
            out.append(_copies(lambda s, z: refs[part_key].at[pl.ds(s, z)],
                               lambda s, z: refs[all_key].at[me, pl.ds(s, z)],
                               lambda s, z, theirs=theirs: refs[all_key].at[theirs, pl.ds(s, z)],
                               rows, 1, send, recv, r - 1, peer))
        return out
    return _Transfer(N_DEV - 1, build)


def _sum_devices(parts, name):
    _, rows, width = parts.shape
    br = _row_block(rows, width, budget=1 << 19)

    def body(p_ref, o_ref):
        acc = p_ref[0]
        for dev in range(1, N_DEV):
            acc = acc + p_ref[dev]
        o_ref[...] = acc

    return pl.pallas_call(body, name=name, grid=(rows // br,),
                          in_specs=[pl.BlockSpec((N_DEV, br, width), lambda i: (0, i, 0))],
                          out_specs=pl.BlockSpec((br, width), lambda i: (i, 0)),
                          out_shape=jax.ShapeDtypeStruct((rows, width), F32),
                          compiler_params=_params("parallel"))(parts)


INPUT_NAMES = None


def _weight_names():
    names = []
    for i, kind in enumerate(("gmlp", "swa", "fox", "gmlp")):
        p = f"l{i}_"
        names += [p + "ffn1_norm", p + "ffn1_wi", p + "ffn1_wo", p + "mix_norm", p + "mix_win"]
        if kind == "gmlp":
            names += [p + "gmlp_vnorm", p + "gmlp_ws", p + "gmlp_bs"]
        elif kind == "swa":
            names += [p + "swa_sinks"]
        else:
            names += [p + "fox_bf"]
        names += [p + "mix_wout", p + "ffn2_norm", p + "ffn2_wi", p + "ffn2_wo"]
    return names + ["final_norm"]


WEIGHTS = _weight_names()
MIXERS = ("gmlp", "swa", "fox", "gmlp")
BIG = ("ffn1_wi", "ffn1_wo", "mix_win", "mix_wout", "ffn2_wi", "ffn2_wo")


def _ffn_fwd(h, gain, wi, wo, tag, dep=None, midway=None):
    n = _rms_fwd(h, gain, tag + "_norm", dep=dep)
    z = _matmul(n, wi, name=tag + "_up", out_dtype=BF16)
    a = _swiglu_fwd(z, tag + "_act")
    f, d = wo.shape[0] * wo.shape[1], wo.shape[2]
    out = _matmul(a, wo.reshape(f, d), name=tag + "_down", out_dtype=F32, scale=0.5, resid=h,
                  dep=midway(a) if midway is not None else None)
    return out, (h, n, z, a)


def _ffn_bwd(dout, saved, gain, wi, wo, tag, dep=None):
    h, n, z, a = saved
    f, d = wo.shape[0] * wo.shape[1], wo.shape[2]
    da = _matmul(dout, wo.reshape(f, d), tb=True, name=tag + "_bdown", out_dtype=BF16, scale=0.5, dep=dep)
    dwo = _matmul(a, dout, ta=True, name=tag + "_gdown", out_dtype=BF16, scale=0.5, dep=dep)
    dz = _swiglu_bwd(z, da, tag + "_bact")
    dn = _matmul(dz, wi, tb=True, name=tag + "_bup", out_dtype=F32)
    dwi = _matmul(n, dz, ta=True, name=tag + "_gup", out_dtype=BF16, out_shards=N_CHIPS)
    dh, dgain = _norm_bwd(h, gain, dn, dout, tag + "_bnorm")
    return dh, dgain, dwi, dwo.reshape(wo.shape)


def _natural(w_sharded, pad_to):
    ns, rows, csh = w_sharded.shape
    nat = jnp.transpose(w_sharded, (1, 0, 2)).reshape(rows, ns * csh)
    extra = (-nat.shape[1]) % pad_to
    return jnp.pad(nat, ((0, 0), (0, extra))) if extra else nat


def _mixer_fwd(kind, h, p, tag, dep=None):
    s_len, d = h.shape
    n = _rms_fwd(h, p["mix_norm"], tag + "_norm", dep=dep)
    wout = p["mix_wout"].reshape(d, d)
    if kind == "gmlp":
        zp = _matmul(n, p["mix_win"], name=tag + "_in", out_dtype=BF16)
        y = _gmlp_fwd(zp, p["gmlp_vnorm"], p["gmlp_ws"], p["gmlp_bs"], tag + "_gate")
        saved = (h, n, zp, y)
    elif kind == "swa":
        qkv = _matmul(n, p["mix_win"], name=tag + "_in", out_dtype=F32)
        q, k, v = _rope_fwd(qkv, tag + "_rope")
        y = _swa_fwd(q, k, v, p["swa_sinks"], tag + "_attn")
        saved = (h, n, q, k, v, y)
    else:
        heads = d // FOX_HEAD_DIM
        win = _natural(p["mix_win"], LANES)
        proj = _matmul(n, win, name=tag + "_in", out_dtype=F32)
        bf_row = jnp.pad(p["fox_bf"], (0, LANES - heads)).reshape(1, LANES)
        dec = _fox_decay(proj, bf_row, 3 * heads, tag + "_decay")
        dec_t = dec[:, :heads].T
        decq, deck = dec_t.reshape(heads, s_len, 1), dec_t.reshape(heads, 1, s_len)
        y = _fox_fwd(proj, decq, deck, heads, tag + "_attn")
        saved = (h, n, win, proj, bf_row, decq, deck, y)
    out = _matmul(y, wout, name=tag + "_out", out_dtype=F32, resid=h)
    return out, saved


def _mixer_bwd(kind, dout, saved, p, tag, dep=None):
    h, n = saved[0], saved[1]
    y = saved[-1]
    s_len, d = h.shape
    wout = p["mix_wout"].reshape(d, d)
    grads = {}
    dy = _matmul(dout, wout, tb=True, name=tag + "_bout", out_dtype=BF16, dep=dep)
    grads["mix_wout"] = _matmul(y, dout, ta=True, name=tag + "_gout", out_dtype=BF16,
                                dep=dep).reshape(p["mix_wout"].shape)
    if kind == "gmlp":
        zp = saved[2]
        dzp, dws, dbst, dvg = _gmlp_bwd(zp, dy, p["gmlp_vnorm"], p["gmlp_ws"], p["gmlp_bs"], tag + "_bgate")
        grads.update(gmlp_ws=dws, gmlp_bs=dbst.T, gmlp_vnorm=dvg.reshape(d))
        dn = _matmul(dzp, p["mix_win"], tb=True, name=tag + "_bin", out_dtype=F32)
        grads["mix_win"] = _matmul(n, dzp, ta=True, name=tag + "_gin", out_dtype=BF16, out_shards=N_CHIPS)
    elif kind == "swa":
        q, k, v = saved[2:5]
        dq, dk, dv, dsinks = _swa_bwd(q, k, v, p["swa_sinks"], dy, tag + "_battn")
        grads["swa_sinks"] = dsinks[0, :p["swa_sinks"].shape[0]]
        dqkv = _rope_bwd(dq, dk, dv, tag + "_brope")
        dn = _matmul(dqkv, p["mix_win"], tb=True, name=tag + "_bin", out_dtype=F32)
        grads["mix_win"] = _matmul(n, dqkv, ta=True, name=tag + "_gin", out_dtype=BF16, out_shards=N_CHIPS)
    else:
        win, proj, bf_row, decq, deck = saved[2:7]
        heads = d // FOX_HEAD_DIM
        dq, dk, dv, ddq, ddk = _fox_bwd(proj, decq, deck, dy, heads, tag + "_battn")
        widen = lambda t: jnp.pad(t.reshape(heads, s_len).T, ((0, 0), (0, LANES - heads)))
        dfl, dbf = _fox_decay_bwd(widen(ddq), widen(ddk), proj, bf_row, 3 * heads, heads, tag + "_bdecay")
        grads["fox_bf"] = dbf[0, :heads]
        dproj = jnp.concatenate([dq, dk.astype(BF16), dv.astype(BF16), dfl], axis=1)
        dn = _matmul(dproj, win, tb=True, name=tag + "_bin", out_dtype=F32)
        dwin = _matmul(n, dproj, ta=True, name=tag + "_gin", out_dtype=BF16)
        ns, rows, csh = p["mix_win"].shape
        grads["mix_win"] = jnp.transpose(dwin[:, :ns * csh].reshape(rows, ns, csh), (1, 0, 2))
    dh, dgain = _norm_bwd(h, p["mix_norm"], dn, dout, tag + "_bnorm")
    grads["mix_norm"] = dgain.reshape(d)
    return dh, grads


def _pack_small(arrays):
    flat = jnp.concatenate([a.reshape(-1).astype(F32) for a in arrays])
    pad = (-flat.shape[0]) % (512 * LANES)
    return jnp.pad(flat, (0, pad)).reshape(-1, LANES)


def _unpack_small(packed, like):
    flat, out, pos = packed.reshape(-1), [], 0
    for a in like:
        out.append(flat[pos:pos + a.size].reshape(a.shape))
        pos += a.size
    return out


def _step(inp):
    x, target = inp["x"][0], inp["loss_target"][0]
    d = x.shape[1]

    core = lax.axis_index("c").astype(jnp.int32).reshape(1)
    chip = (2 * lax.axis_index("x") + lax.axis_index("y")).astype(jnp.int32).reshape(1)

    groups = []
    for i in range(len(MIXERS)):
        groups += [(i, "ffn1", [f"l{i}_ffn1_wi", f"l{i}_ffn1_wo"]), (i, "mix", [f"l{i}_mix_win", f"l{i}_mix_wout"]),
                   (i, "ffn2", [f"l{i}_ffn2_wi", f"l{i}_ffn2_wo"])]

    def layer_params(i, full):
        p = {nm[len(f"l{i}_"):]: inp[nm] for nm in WEIGHTS if nm.startswith(f"l{i}_")}
        p.update({nm[len(f"l{i}_"):]: w for nm, w in full.items()})
        return p

    n_groups = len(groups)
    valid = lambda k: 0 <= k < n_groups

    bufs = {}
    full_shapes = lambda names: [(N_CHIPS, *inp[nm].shape) for nm in names]
    direct = [_gather_direct(names, full_shapes(names)) for _, _, names in groups]
    relay = [_gather_relay(names, full_shapes(names)) for _, _, names in groups]
    to_pair = [_gather_pair(names, full_shapes(names)) for _, _, names in groups]
    sems = {}

    def cast_groups(which, dep):
        for g in which:
            for nm in groups[g][2]:
                bufs[nm] = _cast_into_slot(inp[nm], chip, nm + "_cast", dep=dep)

    def gather_step(step, after, which="all"):
        waits, starts, tags, keys = [], [], [], []
        for kind, transfers, g, begin in (("pair", to_pair, step, False), ("relay", relay, step + 1, False),
                                          ("pair", to_pair, step + 1, True), ("direct", direct, step + 2, False),
                                          ("relay", relay, step + 2, True), ("direct", direct, step + 3, True)):
            needed_now = (kind, g, begin) == ("pair", step, False)
            if not valid(g) or (which == "now" and not needed_now) or (which == "later" and needed_now):
                continue
            keys += [nm for nm in groups[g][2] if nm not in keys]
            if begin:
                starts.append(transfers[g])
                tags.append((kind, g))
            else:
                waits.append((transfers[g], *sems.pop((kind, g))))
        name = f"gather_step{step + 3}" + ("" if which == "all" else "_" + which)
        new, started, token = _comm_call(name, {k: bufs[k] for k in keys}, waits, starts, after)
        bufs.update(new)
        sems.update(zip(tags, started))
        return token

    cast_groups([0], None)
    token = gather_step(-3, x)
    cast_groups(range(1, 6), token)
    token = gather_step(-2, bufs[groups[5][2][-1]])
    cast_groups(range(6, n_groups), token)
    token = gather_step(-1, bufs[groups[-1][2][-1]])
    h, saved, fulls = x, [], []
    for g, (i, part, names) in enumerate(groups):
        token = gather_step(g, h, "now" if g == 0 else "all")
        midway = (lambda a: gather_step(0, a, "later")) if g == 0 else None
        full = {nm: bufs[nm] for nm in names}
        p = layer_params(i, full)
        if part == "mix":
            h, s = _mixer_fwd(MIXERS[i], h, p, f"l{i}_mix", dep=token)
        else:
            h, s = _ffn_fwd(h, p[part + "_norm"], p[part + "_wi"], p[part + "_wo"], f"l{i}_{part}", dep=token,
                            midway=midway)
        saved.append(s)
        fulls.append(full)
    loss_part, dh, dfinal = _loss_head(h, inp["final_norm"], target, "loss_head")
    loss = lax.psum(loss_part, ("x", "y", "c"))

    small_grads = {"final_norm": dfinal.reshape(d)}
    outs = {}
    work = {}
    stage = {}
    small_names = [nm for nm in WEIGHTS if nm.split("_", 1)[1] not in BIG]
    last_small = "l0_ffn1_norm"
    small_sets = {}

    def small_start(tag, names, after):
        part = _pack_small([small_grads[nm] for nm in names])
        device = 4 * lax.axis_index("x") + 2 * lax.axis_index("y") + lax.axis_index("c")
        work[tag + "#part"] = part
        work[tag + "#all"] = lax.dynamic_update_slice(jnp.zeros((N_DEV, *part.shape), F32), part[None],
                                                      (device, 0, 0))
        stage[(tag, 0)] = _device_gather(tag + "#part", tag + "#all", part.shape[0])
        stage_keys[(tag, 0)] = [tag + "#part", tag + "#all"]
        small_sets[tag] = names
        return comm(f"small_{tag}_start", [], [(tag, 0)], after)

    def small_finish(tag, after):
        names = small_sets[tag]
        comm(f"small_{tag}_wait", [(tag, 0)], [], after)
        total = _sum_devices(work[tag + "#all"], f"small_{tag}_sum")
        like = [inp[nm] for nm in names]
        upd = _adamw(_pack_small(like), total, _pack_small([inp["m_" + nm] for nm in names]),
                     _pack_small([inp["v_" + nm] for nm in names]), f"small_{tag}_adamw")
        unpacked = [_unpack_small(t, like) for t in (total, *upd)]
        for k, nm in enumerate(names):
            outs[nm] = tuple(u[k] for u in unpacked)
        return upd[0]

    def comm(name, transfers_to_wait, transfers_to_start, after):
        waits = [(stage[k], *sems.pop(k)) for k in transfers_to_wait if valid(k[1])]
        starts = [k for k in transfers_to_start if valid(k[1])]
        if not waits and not starts:
            return after
        keys = []
        for k in [k for k in transfers_to_wait if valid(k[1])] + starts:
            keys += [key for key in stage_keys[k] if key not in keys]
        new, started, token = _comm_call(name, {k: work[k] for k in keys}, waits, [stage[k] for k in starts], after)
        work.update(new)
        sems.update(zip(starts, started))
        return token

    stage_keys = {}

    def reduce_step(g, after):
        token = comm(f"rs_pair_step{n_groups - 1 - g}", [("pair", g + 1)], [("pair", g)], after)
        if valid(g + 1):
            names = groups[g + 1][2]
            for nm in names:
                work[nm + "#sum"] = _add_pair(work[nm + "#grad"], work[nm + "#got"], core, nm + "_rs_add")
                work[nm + "#others"] = lax.empty((N_CHIPS - 1, *work[nm + "#sum"].shape[1:]), BF16)
            shapes = [work[nm + "#sum"].shape for nm in names]
            stage[("chips", g + 1)] = _grad_chips([nm + "#sum" for nm in names], [nm + "#others" for nm in names], shapes)
            stage_keys[("chips", g + 1)] = [nm + sfx for nm in names for sfx in ("#sum", "#others")]
        token = comm(f"rs_chips_step{n_groups - 1 - g}", [("chips", g + 2)], [("chips", g + 1)], token)
        if valid(g + 2):
            names = groups[g + 2][2]
            for nm in names:
                work[nm + "#half"] = _sum_chips(work[nm + "#sum"], work[nm + "#others"], chip, nm + "_rs_sum")
                work[nm + "#theirs"] = lax.empty(work[nm + "#half"].shape, F32)
            shapes = [work[nm + "#half"].shape for nm in names]
            stage[("join", g + 2)] = _grad_join([nm + "#half" for nm in names], [nm + "#theirs" for nm in names], shapes)
            stage_keys[("join", g + 2)] = [nm + sfx for nm in names for sfx in ("#half", "#theirs")]
        token = comm(f"rs_join_step{n_groups - 1 - g}", [("join", g + 3)], [("join", g + 2)], token)
        if valid(g + 3):
            for nm in groups[g + 3][2]:
                outs[nm] = tuple(_adamw_halves(inp[nm], work[nm + "#half"], work[nm + "#theirs"], inp["m_" + nm],
                                               inp["v_" + nm], core, nm + "_adamw"))
        return token

    dep = None
    for g in reversed(range(n_groups)):
        i, part, names = groups[g]
        p = layer_params(i, fulls[g])
        if part == "mix":
            dh, mg = _mixer_bwd(MIXERS[i], dh, saved[g], p, f"l{i}_mix", dep=dep)
            grads = [mg.pop("mix_win"), mg.pop("mix_wout")]
            small_grads.update({f"l{i}_{key}": val for key, val in mg.items()})
        else:
            dh, g_norm, dwi, dwo = _ffn_bwd(dh, saved[g], p[part + "_norm"], p[part + "_wi"], p[part + "_wo"],
                                            f"l{i}_{part}", dep=dep)
            small_grads[f"l{i}_{part}_norm"] = g_norm.reshape(d)
            grads = [dwi, dwo]
        for nm, gr in zip(names, grads):
            work[nm + "#grad"] = gr
            work[nm + "#got"] = lax.empty((gr.shape[0], gr.shape[1] // 2, gr.shape[2]), BF16)
        stage[("pair", g)] = _grad_pair([nm + "#grad" for nm in names], [nm + "#got" for nm in names],
                                        [gr.shape for gr in grads])
        stage_keys[("pair", g)] = [nm + sfx for nm in names for sfx in ("#grad", "#got")]
        dep = reduce_step(g, dh)
        if g == 1:
            dep = small_start("early", [nm for nm in small_names if nm != last_small], dep)
    dep = small_start("late", [last_small], dep)
    dep = reduce_step(-1, dep)
    dep = small_finish("early", dep)
    dep = reduce_step(-2, dep)
    dep = small_finish("late", dep)
    reduce_step(-3, dep)

    result = [loss, dh[None]]
    for part in range(4):
        result += [outs[nm][part] for nm in WEIGHTS]
    return tuple(result)


def kernel(x, l0_ffn1_norm, l0_ffn1_wi, l0_ffn1_wo, l0_mix_norm, l0_mix_win, l0_gmlp_vnorm, l0_gmlp_ws, l0_gmlp_bs, l0_mix_wout, l0_ffn2_norm, l0_ffn2_wi, l0_ffn2_wo, l1_ffn1_norm, l1_ffn1_wi, l1_ffn1_wo, l1_mix_norm, l1_mix_win, l1_swa_sinks, l1_mix_wout, l1_ffn2_norm, l1_ffn2_wi, l1_ffn2_wo, l2_ffn1_norm, l2_ffn1_wi, l2_ffn1_wo, l2_mix_norm, l2_mix_win, l2_fox_bf, l2_mix_wout, l2_ffn2_norm, l2_ffn2_wi, l2_ffn2_wo, l3_ffn1_norm, l3_ffn1_wi, l3_ffn1_wo, l3_mix_norm, l3_mix_win, l3_gmlp_vnorm, l3_gmlp_ws, l3_gmlp_bs, l3_mix_wout, l3_ffn2_norm, l3_ffn2_wi, l3_ffn2_wo, final_norm, loss_target, m_l0_ffn1_norm, m_l0_ffn1_wi, m_l0_ffn1_wo, m_l0_mix_norm, m_l0_mix_win, m_l0_gmlp_vnorm, m_l0_gmlp_ws, m_l0_gmlp_bs, m_l0_mix_wout, m_l0_ffn2_norm, m_l0_ffn2_wi, m_l0_ffn2_wo, m_l1_ffn1_norm, m_l1_ffn1_wi, m_l1_ffn1_wo, m_l1_mix_norm, m_l1_mix_win, m_l1_swa_sinks, m_l1_mix_wout, m_l1_ffn2_norm, m_l1_ffn2_wi, m_l1_ffn2_wo, m_l2_ffn1_norm, m_l2_ffn1_wi, m_l2_ffn1_wo, m_l2_mix_norm, m_l2_mix_win, m_l2_fox_bf, m_l2_mix_wout, m_l2_ffn2_norm, m_l2_ffn2_wi, m_l2_ffn2_wo, m_l3_ffn1_norm, m_l3_ffn1_wi, m_l3_ffn1_wo, m_l3_mix_norm, m_l3_mix_win, m_l3_gmlp_vnorm, m_l3_gmlp_ws, m_l3_gmlp_bs, m_l3_mix_wout, m_l3_ffn2_norm, m_l3_ffn2_wi, m_l3_ffn2_wo, m_final_norm, v_l0_ffn1_norm, v_l0_ffn1_wi, v_l0_ffn1_wo, v_l0_mix_norm, v_l0_mix_win, v_l0_gmlp_vnorm, v_l0_gmlp_ws, v_l0_gmlp_bs, v_l0_mix_wout, v_l0_ffn2_norm, v_l0_ffn2_wi, v_l0_ffn2_wo, v_l1_ffn1_norm, v_l1_ffn1_wi, v_l1_ffn1_wo, v_l1_mix_norm, v_l1_mix_win, v_l1_swa_sinks, v_l1_mix_wout, v_l1_ffn2_norm, v_l1_ffn2_wi, v_l1_ffn2_wo, v_l2_ffn1_norm, v_l2_ffn1_wi, v_l2_ffn1_wo, v_l2_mix_norm, v_l2_mix_win, v_l2_fox_bf, v_l2_mix_wout, v_l2_ffn2_norm, v_l2_ffn2_wi, v_l2_ffn2_wo, v_l3_ffn1_norm, v_l3_ffn1_wi, v_l3_ffn1_wo, v_l3_mix_norm, v_l3_mix_win, v_l3_gmlp_vnorm, v_l3_gmlp_ws, v_l3_gmlp_bs, v_l3_mix_wout, v_l3_ffn2_norm, v_l3_ffn2_wi, v_l3_ffn2_wo, v_final_norm):
    return _step(dict(locals()))
```

```python
import functools
import math

import jax
import jax.numpy as jnp
from jax import lax
from jax.experimental import pallas as pl
from jax.experimental.pallas import tpu as pltpu

F32 = jnp.float32
BF16 = jnp.bfloat16

NORM_EPS = 1e-5
NEG_INF = -1e30
BLOCK = 128
GMLP_GROUPS = 16
SWA_HEAD_DIM = 64
SWA_GROUP = 8
ROPE_DIM = SWA_HEAD_DIM // 4
ROPE_THETA = 500000.0
FOX_HEAD_DIM = 128
ADAM_LR = 0.001
ADAM_B1 = 0.9
ADAM_B2 = 0.999
ADAM_EPS = 1e-08
ADAM_WD = 0.01
ADAM_STEP = 10
N_CHIPS = 4
N_DEV = 8
LANES = 128
VMEM_LIMIT = 56 * 1024 * 1024
MESH = pl.DeviceIdType.MESH
HBM = pl.BlockSpec(memory_space=pltpu.HBM)
SEM = pl.BlockSpec(memory_space=pltpu.SEMAPHORE)
EFFECT = pltpu.SideEffectType.DATAFLOW_SIDE_EFFECTING

MM_TILES = (1024, 1408, 896, 640, 512, 384, 256, 128)
K_TILES = (2816,) + MM_TILES


def _pick(n, prefs):
    for p in prefs:
        if p <= n and n % p == 0:
            return p
    return n


def _params(*sem):
    return pltpu.CompilerParams(dimension_semantics=sem or None, vmem_limit_bytes=VMEM_LIMIT)


def _cols(arr):
    return arr.shape[-1] * (arr.shape[0] if arr.ndim == 3 else 1)


def _mat_spec(arr, rb, cb, ridx, cidx):
    if arr.ndim == 2:
        return pl.BlockSpec((rb, cb), lambda j, i, k: (ridx(j, i, k), cidx(j, i, k)))
    per = arr.shape[2] // cb
    return pl.BlockSpec((None, rb, cb),
                        lambda j, i, k: (cidx(j, i, k) // per, ridx(j, i, k), cidx(j, i, k) % per))


def _matmul(a, b, *, name, out_dtype, ta=False, tb=False, out_shards=1, scale=1.0, resid=None, dep=None):
    m_dim, k_dim = (a.shape[1], a.shape[0]) if ta else a.shape
    n_dim = b.shape[-2] if tb else _cols(b)
    assert k_dim == (_cols(b) if tb else b.shape[-2]), (a.shape, b.shape, ta, tb)
    n_unit = n_dim // out_shards
    if b.ndim == 3 and not tb:
        n_unit = math.gcd(n_unit, b.shape[2])
    k_unit = b.shape[2] if (b.ndim == 3 and tb) else k_dim
    bm = _pick(m_dim, MM_TILES)
    bn = _pick(n_unit, MM_TILES)
    bk = k_unit if k_unit <= 2048 else _pick(k_unit, K_TILES)
    nk = k_dim // bk
    i_of, j_of, k_of = (lambda j, i, k: i), (lambda j, i, k: j), (lambda j, i, k: k)
    a_spec = _mat_spec(a, bk, bm, k_of, i_of) if ta else _mat_spec(a, bm, bk, i_of, k_of)
    b_spec = _mat_spec(b, bn, bk, j_of, k_of) if tb else _mat_spec(b, bk, bn, k_of, j_of)
    out_shape = (m_dim, n_dim) if out_shards == 1 else (out_shards, m_dim, n_dim // out_shards)
    out = jax.ShapeDtypeStruct(out_shape, out_dtype)
    o_spec = _mat_spec(out, bm, bn, i_of, j_of)
    dims = (((0 if ta else 1,), (1 if tb else 0,)), ((), ()))
    operands, in_specs = [a, b], [a_spec, b_spec]
    if resid is not None:
        operands.append(resid)
        in_specs.append(_mat_spec(resid, bm, bn, i_of, j_of))
    if dep is not None:
        operands.append(dep)
        in_specs.append(pl.BlockSpec(dep.shape, lambda j, i, k: (0, 0)))
    n_in = len(operands)

    def body(*refs):
        a_ref, b_ref = refs[0], refs[1]
        r_ref = refs[2] if resid is not None else None
        o_ref = refs[n_in]
        part = lax.dot_general(a_ref[...].astype(BF16), b_ref[...].astype(BF16), dims,
                               preferred_element_type=F32)

        def finish(acc):
            val = acc * scale if scale != 1.0 else acc
            if r_ref is not None:
                val = r_ref[...] + val
            o_ref[...] = val.astype(o_ref.dtype)

        if nk == 1:
            finish(part)
        else:
            acc_ref = refs[-1]
            k = pl.program_id(2)

            @pl.when(k == 0)
            def _():
                acc_ref[...] = part

            @pl.when(k > 0)
            def _():
                acc_ref[...] += part

            @pl.when(k == nk - 1)
            def _():
                finish(acc_ref[...])

    return pl.pallas_call(
        body, name=name, grid=(n_dim // bn, m_dim // bm, nk),
        in_specs=in_specs, out_specs=o_spec, out_shape=out,
        scratch_shapes=[pltpu.VMEM((bm, bn), F32)] if nk > 1 else [],
        compiler_params=_params("parallel", "parallel", "arbitrary"),
    )(*operands)


def _row_block(rows, width, itemsize=4, budget=2 << 20):
    best = None
    for br in range(16, rows + 1, 16):
        if rows % br == 0 and br * width * itemsize <= budget:
            best = br
    return best or rows


def _rms_fwd(h, g, name, dep=None):
    s_len, d = h.shape
    br = _row_block(s_len, d)

    def body(h_ref, g_ref, *rest):
        o_ref = rest[-1]
        x = h_ref[...]
        r = lax.rsqrt(jnp.mean(x * x, axis=-1, keepdims=True) + NORM_EPS)
        o_ref[...] = (x * r * g_ref[...]).astype(BF16)

    spec = pl.BlockSpec((br, d), lambda i: (i, 0))
    operands = [h, g.reshape(1, d)] + ([dep] if dep is not None else [])
    in_specs = [spec, pl.BlockSpec((1, d), lambda i: (0, 0))]
    if dep is not None:
        in_specs.append(pl.BlockSpec(dep.shape, lambda i: (0, 0)))
    return pl.pallas_call(body, name=name, grid=(s_len // br,), in_specs=in_specs, out_specs=spec,
                          out_shape=jax.ShapeDtypeStruct((s_len, d), BF16),
                          compiler_params=_params("parallel"))(*operands)


def _rms_bwd_rows(x, g, dn):
    r = lax.rsqrt(jnp.mean(x * x, axis=-1, keepdims=True) + NORM_EPS)
    xhat = x * r
    gdn = dn * g
    dx = r * (gdn - xhat * jnp.mean(gdn * xhat, axis=-1, keepdims=True))
    return dx, dn * xhat


def _norm_bwd(h, g, dn, dres, name):
    s_len, d = h.shape
    br = _row_block(s_len, d, budget=1 << 20)

    def body(h_ref, g_ref, dn_ref, dres_ref, dh_ref, dg_ref):
        dx, dg_rows = _rms_bwd_rows(h_ref[...], g_ref[...], dn_ref[...].astype(F32))
        dh_ref[...] = dres_ref[...] + dx

        @pl.when(pl.program_id(0) == 0)
        def _():
            dg_ref[...] = jnp.zeros_like(dg_ref)

        dg_ref[...] += jnp.sum(dg_rows, axis=0, keepdims=True)

    spec = pl.BlockSpec((br, d), lambda i: (i, 0))
    vec = pl.BlockSpec((1, d), lambda i: (0, 0))
    return pl.pallas_call(body, name=name, grid=(s_len // br,),
                          in_specs=[spec, vec, spec, spec], out_specs=[spec, vec],
                          out_shape=[jax.ShapeDtypeStruct((s_len, d), F32),
                                     jax.ShapeDtypeStruct((1, d), F32)],
                          compiler_params=_params("arbitrary"))(h, g.reshape(1, d), dn, dres)


def _sigmoid(x):
    return 0.5 * (1.0 + jnp.tanh(0.5 * x))


def _swiglu_fwd(z, name):
    s_len, f2 = z.shape
    f = f2 // 2
    br = _row_block(s_len, f2, itemsize=2, budget=6 << 20)
    fc = _pick(f, MM_TILES)

    def body(z_ref, a_ref):
        for c0 in range(0, f, fc):
            gate = z_ref[:, c0:c0 + fc].astype(F32)
            up = z_ref[:, f + c0:f + c0 + fc].astype(F32)
            a_ref[:, c0:c0 + fc] = (gate * _sigmoid(gate) * up).astype(BF16)

    return pl.pallas_call(body, name=name, grid=(s_len // br,),
                          in_specs=[pl.BlockSpec((br, f2), lambda i: (i, 0))],
                          out_specs=pl.BlockSpec((br, f), lambda i: (i, 0)),
                          out_shape=jax.ShapeDtypeStruct((s_len, f), BF16),
                          compiler_params=_params("parallel"))(z)


def _swiglu_bwd(z, da, name):
    s_len, f2 = z.shape
    f = f2 // 2
    br = _row_block(s_len, f2, itemsize=2, budget=6 << 20)
    fc = _pick(f, MM_TILES)

    def body(z_ref, da_ref, dz_ref):
        for c0 in range(0, f, fc):
            gate = z_ref[:, c0:c0 + fc].astype(F32)
            up = z_ref[:, f + c0:f + c0 + fc].astype(F32)
            d = da_ref[:, c0:c0 + fc].astype(F32)
            sig = _sigmoid(gate)
            dz_ref[:, c0:c0 + fc] = (d * up * (sig * (1.0 + gate * (1.0 - sig)))).astype(BF16)
            dz_ref[:, f + c0:f + c0 + fc] = (d * gate * sig).astype(BF16)

    return pl.pallas_call(body, name=name, grid=(s_len // br,),
                          in_specs=[pl.BlockSpec((br, f2), lambda i: (i, 0)),
                                    pl.BlockSpec((br, f), lambda i: (i, 0))],
                          out_specs=pl.BlockSpec((br, f2), lambda i: (i, 0)),
                          out_shape=jax.ShapeDtypeStruct((s_len, f2), BF16),
                          compiler_params=_params("parallel"))(z, da)


def _loss_head(h, g, target, name):
    s_len, d = h.shape
    br = _row_block(s_len, d, budget=1 << 20)

    def body(h_ref, g_ref, t_ref, loss_ref, dh_ref, dg_ref):
        x = h_ref[...]
        gain = g_ref[...]
        r = lax.rsqrt(jnp.mean(x * x, axis=-1, keepdims=True) + NORM_EPS)
        err = x * r * gain - t_ref[...]
        part = 0.5 * jnp.sum(jnp.mean(err * err, axis=-1, keepdims=True), axis=0, keepdims=True)
        dx, dg_rows = _rms_bwd_rows(x, gain, err * (1.0 / d))
        dh_ref[...] = dx

        @pl.when(pl.program_id(0) == 0)
        def _():
            dg_ref[...] = jnp.zeros_like(dg_ref)
            loss_ref[...] = jnp.zeros_like(loss_ref)

        dg_ref[...] += jnp.sum(dg_rows, axis=0, keepdims=True)
        loss_ref[...] += jnp.broadcast_to(part, loss_ref.shape)

    spec = pl.BlockSpec((br, d), lambda i: (i, 0))
    vec = pl.BlockSpec((1, d), lambda i: (0, 0))
    one = pl.BlockSpec((1, LANES), lambda i: (0, 0))
    loss, dh, dg = pl.pallas_call(
        body, name=name, grid=(s_len // br,), in_specs=[spec, vec, spec],
        out_specs=[one, spec, vec],
        out_shape=[jax.ShapeDtypeStruct((1, LANES), F32), jax.ShapeDtypeStruct((s_len, d), F32),
                   jax.ShapeDtypeStruct((1, d), F32)],
        compiler_params=_params("arbitrary"))(h, g.reshape(1, d), target)
    return loss[0, 0], dh, dg


def _adamw(w, g, m, v, name):
    rows, width = w.shape
    br = _row_block(rows, width, budget=1 << 20)
    c1 = 1.0 - ADAM_B1 ** ADAM_STEP
    c2 = 1.0 - ADAM_B2 ** ADAM_STEP

    def body(w_ref, g_ref, m_ref, v_ref, d_ref, nm_ref, nv_ref):
        grad = g_ref[...]
        new_m = ADAM_B1 * m_ref[...] + (1.0 - ADAM_B1) * grad
        new_v = ADAM_B2 * v_ref[...] + (1.0 - ADAM_B2) * (grad * grad)
        d_ref[...] = -ADAM_LR * ((new_m / c1) / (jnp.sqrt(new_v / c2) + ADAM_EPS) + ADAM_WD * w_ref[...])
        nm_ref[...] = new_m
        nv_ref[...] = new_v

    spec = pl.BlockSpec((br, width), lambda i: (i, 0))
    shp = jax.ShapeDtypeStruct(w.shape, F32)
    return pl.pallas_call(body, name=name, grid=(rows // br,), in_specs=[spec] * 4,
                          out_specs=[spec] * 3, out_shape=[shp] * 3,
                          compiler_params=_params("parallel"))(w, g, m, v)


def _gelu(x):
    return 0.5 * x * (1.0 + lax.erf(x * (2.0 ** -0.5)))


def _gelu_grad(x):
    return 0.5 * (1.0 + lax.erf(x * (2.0 ** -0.5))) + x * jnp.exp(-0.5 * x * x) * ((2.0 * math.pi) ** -0.5)


def _tril_mask():
    row = lax.broadcasted_iota(jnp.int32, (BLOCK, BLOCK), 0)
    col = lax.broadcasted_iota(jnp.int32, (BLOCK, BLOCK), 1)
    return col <= row


def _gmlp_specs(s_len, d):
    gw = d // GMLP_GROUPS
    zp = pl.BlockSpec((BLOCK, 2 * d), lambda i: (i, 0))
    row = pl.BlockSpec((BLOCK, d), lambda i: (i, 0))
    vec = pl.BlockSpec((1, d), lambda i: (0, 0))
    ws = pl.BlockSpec((GMLP_GROUPS, BLOCK, BLOCK), lambda i: (0, 0, 0))
    bst = pl.BlockSpec((BLOCK, GMLP_GROUPS), lambda i: (0, 0))
    return gw, zp, row, vec, ws, bst


def _gmlp_fwd(zp, vgain, ws, bs, name):
    s_len, d2 = zp.shape
    d = d2 // 2
    gw, zp_spec, row_spec, vec_spec, ws_spec, bst_spec = _gmlp_specs(s_len, d)

    def body(zp_ref, vg_ref, ws_ref, bst_ref, y_ref):
        u = _gelu(zp_ref[:, :d].astype(F32))
        vv = _gelu(zp_ref[:, d:].astype(F32))
        r = lax.rsqrt(jnp.mean(vv * vv, axis=-1, keepdims=True) + NORM_EPS)
        vn = (vv * r * vg_ref[...]).astype(BF16)
        mask = _tril_mask()
        for g in range(GMLP_GROUPS):
            cols = slice(g * gw, (g + 1) * gw)
            wg = jnp.where(mask, ws_ref[g], 0.0).astype(BF16)
            mixed = jnp.dot(wg, vn[:, cols], preferred_element_type=F32) + bst_ref[:, g:g + 1]
            y_ref[:, cols] = (u[:, cols] * mixed).astype(BF16)

    return pl.pallas_call(body, name=name, grid=(s_len // BLOCK,),
                          in_specs=[zp_spec, vec_spec, ws_spec, bst_spec], out_specs=row_spec,
                          out_shape=jax.ShapeDtypeStruct((s_len, d), BF16),
                          compiler_params=_params("parallel"))(zp, vgain.reshape(1, d), ws, bs.T)


def _gmlp_bwd(zp, dy, vgain, ws, bs, name):
    s_len, d2 = zp.shape
    d = d2 // 2
    gw, zp_spec, row_spec, vec_spec, ws_spec, bst_spec = _gmlp_specs(s_len, d)

    def body(zp_ref, dy_ref, vg_ref, ws_ref, bst_ref, dzp_ref, dws_ref, dbst_ref, dvg_ref, dvn_ref):
        @pl.when(pl.program_id(0) == 0)
        def _():
            dws_ref[...] = jnp.zeros_like(dws_ref)
            dbst_ref[...] = jnp.zeros_like(dbst_ref)
            dvg_ref[...] = jnp.zeros_like(dvg_ref)

        zu = zp_ref[:, :d].astype(F32)
        zv = zp_ref[:, d:].astype(F32)
        u = _gelu(zu)
        vv = _gelu(zv)
        r = lax.rsqrt(jnp.mean(vv * vv, axis=-1, keepdims=True) + NORM_EPS)
        vhat = vv * r
        gain = vg_ref[...]
        vn = (vhat * gain).astype(BF16)
        dyf = dy_ref[...].astype(F32)
        dmixed = dyf * u
        dmixed_b = dmixed.astype(BF16)
        mask = _tril_mask()
        lane = lax.broadcasted_iota(jnp.int32, (BLOCK, GMLP_GROUPS), 1)
        dbs_step = jnp.zeros((BLOCK, GMLP_GROUPS), F32)
        for g in range(GMLP_GROUPS):
            cols = slice(g * gw, (g + 1) * gw)
            wg = jnp.where(mask, ws_ref[g], 0.0).astype(BF16)
            mixed = jnp.dot(wg, vn[:, cols], preferred_element_type=F32) + bst_ref[:, g:g + 1]
            dzp_ref[:, cols] = (dyf[:, cols] * mixed * _gelu_grad(zu[:, cols])).astype(BF16)
            dm = dmixed_b[:, cols]
            dw = lax.dot_general(dm, vn[:, cols], (((1,), (1,)), ((), ())), preferred_element_type=F32)
            dws_ref[g] += jnp.where(mask, dw, 0.0)
            dbs_step = dbs_step + jnp.where(lane == g, jnp.sum(dmixed[:, cols], axis=-1, keepdims=True), 0.0)
            dvn_ref[:, cols] = lax.dot_general(wg, dm, (((0,), (0,)), ((), ())), preferred_element_type=F32)
        dbst_ref[...] += dbs_step
        dvn = dvn_ref[...]
        dvg_ref[...] += jnp.sum(dvn * vhat, axis=0, keepdims=True)
        dvhat = dvn * gain
        dvv = r * (dvhat - vhat * jnp.mean(dvhat * vhat, axis=-1, keepdims=True))
        dzp_ref[:, d:] = (dvv * _gelu_grad(zv)).astype(BF16)

    return pl.pallas_call(
        body, name=name, grid=(s_len // BLOCK,),
        in_specs=[zp_spec, row_spec, vec_spec, ws_spec, bst_spec],
        out_specs=[zp_spec, ws_spec, bst_spec, vec_spec],
        out_shape=[jax.ShapeDtypeStruct((s_len, d2), BF16), jax.ShapeDtypeStruct(ws.shape, F32),
                   jax.ShapeDtypeStruct((BLOCK, GMLP_GROUPS), F32), jax.ShapeDtypeStruct((1, d), F32)],
        scratch_shapes=[pltpu.VMEM((BLOCK, d), F32)],
        compiler_params=_params("arbitrary"))(zp, dy, vgain.reshape(1, d), ws, bs.T)


def _rope_tables(s_len, sign):
    half = ROPE_DIM // 2
    inv_freq = ROPE_THETA ** (-(jnp.arange(half, dtype=F32) * 2.0 / ROPE_DIM))
    ang = jnp.arange(s_len, dtype=F32)[:, None] * inv_freq[None, :]
    cos, sin = jnp.cos(ang), jnp.sin(ang) * sign
    pad = jnp.zeros((s_len, SWA_HEAD_DIM - ROPE_DIM), F32)
    zero = jnp.zeros_like(sin)
    cos_t = jnp.concatenate([cos, cos, pad + 1.0], axis=1)
    sin_up = jnp.concatenate([-sin, zero, pad], axis=1)
    sin_dn = jnp.concatenate([zero, sin, pad], axis=1)
    return [jnp.tile(t, (1, LANES // SWA_HEAD_DIM)) for t in (cos_t, sin_up, sin_dn)]


def _rotate(x, cos_t, sin_up, sin_dn):
    width = x.shape[-1]
    half = ROPE_DIM // 2
    reps = width // cos_t.shape[-1]
    if reps > 1:
        cos_t, sin_up, sin_dn = (jnp.tile(t, (1, reps)) for t in (cos_t, sin_up, sin_dn))
    elif reps == 0:
        cos_t, sin_up, sin_dn = (t[:, :width] for t in (cos_t, sin_up, sin_dn))
    return x * cos_t + pltpu.roll(x, width - half, 1) * sin_up + pltpu.roll(x, half, 1) * sin_dn


def _rope_fwd(qkv, name):
    s_len, total = qkv.shape
    wkv = total // (SWA_GROUP + 2)
    wq = SWA_GROUP * wkv
    br = _row_block(s_len, total, budget=2 << 20)
    tables = _rope_tables(s_len, 1.0)

    def body(q_ref, k_ref, v_ref, c_ref, su_ref, sd_ref, qo_ref, ko_ref, vo_ref):
        t = (c_ref[...], su_ref[...], sd_ref[...])
        qo_ref[...] = _rotate(q_ref[...], *t).astype(BF16)
        ko_ref[...] = _rotate(k_ref[...], *t).astype(BF16)
        vo_ref[...] = v_ref[...].astype(BF16)

    qs = pl.BlockSpec((br, wq), lambda i: (i, 0))
    ks = pl.BlockSpec((br, wkv), lambda i: (i, SWA_GROUP))
    vs = pl.BlockSpec((br, wkv), lambda i: (i, SWA_GROUP + 1))
    ts = pl.BlockSpec((br, LANES), lambda i: (i, 0))
    kv_out = pl.BlockSpec((br, wkv), lambda i: (i, 0))
    return pl.pallas_call(
        body, name=name, grid=(s_len // br,), in_specs=[qs, ks, vs, ts, ts, ts],
        out_specs=[qs, kv_out, kv_out],
        out_shape=[jax.ShapeDtypeStruct((s_len, wq), BF16), jax.ShapeDtypeStruct((s_len, wkv), BF16),
                   jax.ShapeDtypeStruct((s_len, wkv), BF16)],
        compiler_params=_params("parallel"))(qkv, qkv, qkv, *tables)


def _rope_bwd(dq, dk, dv, name):
    s_len, wq = dq.shape
    wkv = dk.shape[1]
    br = _row_block(s_len, wq + 2 * wkv, budget=2 << 20)
    tables = _rope_tables(s_len, -1.0)

    def body(q_ref, k_ref, v_ref, c_ref, su_ref, sd_ref, o_ref):
        t = (c_ref[...], su_ref[...], sd_ref[...])
        o_ref[:, :wq] = _rotate(q_ref[...], *t).astype(BF16)
        o_ref[:, wq:wq + wkv] = _rotate(k_ref[...], *t).astype(BF16)
        o_ref[:, wq + wkv:] = v_ref[...].astype(BF16)

    qs = pl.BlockSpec((br, wq), lambda i: (i, 0))
    kvs = pl.BlockSpec((br, wkv), lambda i: (i, 0))
    ts = pl.BlockSpec((br, LANES), lambda i: (i, 0))
    return pl.pallas_call(
        body, name=name, grid=(s_len // br,), in_specs=[qs, kvs, kvs, ts, ts, ts],
        out_specs=pl.BlockSpec((br, wq + 2 * wkv), lambda i: (i, 0)),
        out_shape=jax.ShapeDtypeStruct((s_len, wq + 2 * wkv), BF16),
        compiler_params=_params("parallel"))(dq, dk, dv, *tables)


def _swa_valid(i):
    row = lax.broadcasted_iota(jnp.int32, (BLOCK, 2 * BLOCK), 0)
    col = lax.broadcasted_iota(jnp.int32, (BLOCK, 2 * BLOCK), 1)
    return (col - BLOCK <= row) & (row < col) & ((col >= BLOCK) | (i > 0))


def _swa_specs(wq, wkv):
    q_spec = pl.BlockSpec((BLOCK, wq), lambda i: (i, 0))
    cur = pl.BlockSpec((BLOCK, wkv), lambda i: (i, 0))
    prev = pl.BlockSpec((BLOCK, wkv), lambda i: (jnp.maximum(i - 1, 0), 0))
    sink = pl.BlockSpec(memory_space=pltpu.SMEM)
    return q_spec, cur, prev, sink


def _swa_probs(q_h, k_cat, valid, sink):
    s = lax.dot_general(q_h, k_cat, (((1,), (1,)), ((), ())), preferred_element_type=F32)
    s = jnp.where(valid, s * (SWA_HEAD_DIM ** -0.5), NEG_INF)
    m = jnp.maximum(jnp.max(s, axis=-1, keepdims=True), sink)
    p = jnp.exp(s - m)
    e_sink = jnp.exp(sink - m)
    denom = jnp.sum(p, axis=-1, keepdims=True) + e_sink
    return p / denom, e_sink / denom


def _swa_fwd(q, k, v, sinks, name):
    s_len, wq = q.shape
    wkv = k.shape[1]
    hd = SWA_HEAD_DIM
    q_spec, cur, prev, sink_spec = _swa_specs(wq, wkv)

    def body(q_ref, kc_ref, kp_ref, vc_ref, vp_ref, sink_ref, o_ref):
        valid = _swa_valid(pl.program_id(0))
        for j in range(wkv // hd):
            lanes = slice(j * hd, (j + 1) * hd)
            k_cat = jnp.concatenate([kp_ref[:, lanes], kc_ref[:, lanes]], axis=0)
            v_cat = jnp.concatenate([vp_ref[:, lanes], vc_ref[:, lanes]], axis=0)
            for hh in range(SWA_GROUP):
                h = j * SWA_GROUP + hh
                pn, _ = _swa_probs(q_ref[:, h * hd:(h + 1) * hd], k_cat, valid, sink_ref[h])
                o_ref[:, h * hd:(h + 1) * hd] = jnp.dot(
                    pn.astype(BF16), v_cat, preferred_element_type=F32).astype(BF16)

    return pl.pallas_call(body, name=name, grid=(s_len // BLOCK,),
                          in_specs=[q_spec, cur, prev, cur, prev, sink_spec], out_specs=q_spec,
                          out_shape=jax.ShapeDtypeStruct((s_len, wq), BF16),
                          compiler_params=_params("parallel"))(q, k, k, v, v, sinks)


def _swa_bwd(q, k, v, sinks, do, name):
    s_len, wq = q.shape
    wkv = k.shape[1]
    hd = SWA_HEAD_DIM
    q_spec, cur, prev, sink_spec = _swa_specs(wq, wkv)
    full = pl.BlockSpec((s_len, wkv), lambda i: (0, 0))
    one = pl.BlockSpec((1, LANES), lambda i: (0, 0))
    scale = hd ** -0.5

    def body(q_ref, kc_ref, kp_ref, vc_ref, vp_ref, sink_ref, do_ref, dq_ref, dk_ref, dv_ref, ds_ref):
        i = pl.program_id(0)

        @pl.when(i == 0)
        def _():
            dk_ref[...] = jnp.zeros_like(dk_ref)
            dv_ref[...] = jnp.zeros_like(dv_ref)
            ds_ref[...] = jnp.zeros_like(ds_ref)

        valid = _swa_valid(i)
        lane = lax.broadcasted_iota(jnp.int32, (1, LANES), 1)
        dsink_step = jnp.zeros((1, LANES), F32)
        rows_prev = pl.ds(pl.multiple_of(jnp.maximum(i - 1, 0) * BLOCK, BLOCK), BLOCK)
        rows_cur = pl.ds(pl.multiple_of(i * BLOCK, BLOCK), BLOCK)
        for j in range(wkv // hd):
            lanes = slice(j * hd, (j + 1) * hd)
            k_cat = jnp.concatenate([kp_ref[:, lanes], kc_ref[:, lanes]], axis=0)
            v_cat = jnp.concatenate([vp_ref[:, lanes], vc_ref[:, lanes]], axis=0)
            dk_cat = jnp.zeros((2 * BLOCK, hd), F32)
            dv_cat = jnp.zeros((2 * BLOCK, hd), F32)
            for hh in range(SWA_GROUP):
                h = j * SWA_GROUP + hh
                q_h = q_ref[:, h * hd:(h + 1) * hd]
                do_h = do_ref[:, h * hd:(h + 1) * hd]
                pn, p_sink = _swa_probs(q_h, k_cat, valid, sink_ref[h])
                dpn = lax.dot_general(do_h, v_cat, (((1,), (1,)), ((), ())), preferred_element_type=F32)
                delta = jnp.sum(dpn * pn, axis=-1, keepdims=True)
                ds = (pn * (dpn - delta) * scale).astype(BF16)
                dsink_h = -jnp.sum(p_sink * delta, axis=0, keepdims=True)
                dsink_step = dsink_step + jnp.where(lane == h, dsink_h, 0.0)
                dq_ref[:, h * hd:(h + 1) * hd] = jnp.dot(ds, k_cat, preferred_element_type=F32)
                dk_cat = dk_cat + lax.dot_general(ds, q_h, (((0,), (0,)), ((), ())),
                                                  preferred_element_type=F32)
                dv_cat = dv_cat + lax.dot_general(pn.astype(BF16), do_h, (((0,), (0,)), ((), ())),
                                                  preferred_element_type=F32)
            dk_ref[rows_prev, lanes] += dk_cat[:BLOCK]
            dk_ref[rows_cur, lanes] += dk_cat[BLOCK:]
            dv_ref[rows_prev, lanes] += dv_cat[:BLOCK]
            dv_ref[rows_cur, lanes] += dv_cat[BLOCK:]
        ds_ref[...] += dsink_step

    return pl.pallas_call(
        body, name=name, grid=(s_len // BLOCK,),
        in_specs=[q_spec, cur, prev, cur, prev, sink_spec, q_spec],
        out_specs=[q_spec, full, full, one],
        out_shape=[jax.ShapeDtypeStruct((s_len, wq), F32), jax.ShapeDtypeStruct((s_len, wkv), F32),
                   jax.ShapeDtypeStruct((s_len, wkv), F32), jax.ShapeDtypeStruct((1, LANES), F32)],
        compiler_params=_params("arbitrary"))(q, k, k, v, v, sinks, do)


def _log_sigmoid(x):
    return jnp.minimum(x, 0.0) - jnp.log(1.0 + jnp.exp(-jnp.abs(x)))


def _tri_ones(lower):
    row = lax.broadcasted_iota(jnp.int32, (BLOCK, BLOCK), 0)
    col = lax.broadcasted_iota(jnp.int32, (BLOCK, BLOCK), 1)
    return jnp.where((col <= row) if lower else (col >= row), 1.0, 0.0).astype(F32)


def _fox_decay(proj, bf_row, fl_block, name):
    s_len = proj.shape[0]
    nchunk = s_len // BLOCK

    def body(fl_ref, bf_ref, dec_ref):
        tri = _tri_ones(True)
        carry = jnp.zeros((1, LANES), F32)
        for c in range(nchunk):
            rows = slice(c * BLOCK, (c + 1) * BLOCK)
            log_f = _log_sigmoid(fl_ref[rows, :] + bf_ref[...])
            loc = jnp.dot(tri, log_f, preferred_element_type=F32, precision=lax.Precision.HIGHEST) + carry
            dec_ref[rows, :] = loc
            carry = loc[BLOCK - 1:BLOCK, :]

    return pl.pallas_call(
        body, name=name, grid=(1,),
        in_specs=[pl.BlockSpec((s_len, LANES), lambda i: (0, fl_block)),
                  pl.BlockSpec((1, LANES), lambda i: (0, 0))],
        out_specs=pl.BlockSpec((s_len, LANES), lambda i: (0, 0)),
        out_shape=jax.ShapeDtypeStruct((s_len, LANES), F32),
        compiler_params=_params("arbitrary"))(proj, bf_row)


def _fox_decay_bwd(ddq, ddk, proj, bf_row, fl_block, heads, name):
    s_len = proj.shape[0]
    nchunk = s_len // BLOCK

    def body(ddq_ref, ddk_ref, fl_ref, bf_ref, dfl_ref, dbf_ref):
        tri = _tri_ones(False)
        lane_ok = lax.broadcasted_iota(jnp.int32, (BLOCK, LANES), 1) < heads
        carry = jnp.zeros((1, LANES), F32)
        dbf = jnp.zeros((1, LANES), F32)
        for c in reversed(range(nchunk)):
            rows = slice(c * BLOCK, (c + 1) * BLOCK)
            ddec = ddq_ref[rows, :] + ddk_ref[rows, :]
            dlog = jnp.dot(tri, ddec, preferred_element_type=F32, precision=lax.Precision.HIGHEST) + carry
            carry = dlog[0:1, :]
            dfl = jnp.where(lane_ok, dlog * _sigmoid(-(fl_ref[rows, :] + bf_ref[...])), 0.0)
            dfl_ref[rows, :] = dfl.astype(BF16)
            dbf = dbf + jnp.sum(dfl, axis=0, keepdims=True)
        dbf_ref[...] = dbf

    blk = pl.BlockSpec((s_len, LANES), lambda i: (0, 0))
    one = pl.BlockSpec((1, LANES), lambda i: (0, 0))
    return pl.pallas_call(
        body, name=name, grid=(1,),
        in_specs=[blk, blk, pl.BlockSpec((s_len, LANES), lambda i: (0, fl_block)), one],
        out_specs=[blk, one],
        out_shape=[jax.ShapeDtypeStruct((s_len, LANES), BF16), jax.ShapeDtypeStruct((1, LANES), F32)],
        compiler_params=_params("arbitrary"))(ddq, ddk, proj, bf_row)


def _fox_scores(q, k, decq, deck, i, bq):
    s_len = k.shape[0]
    s = lax.dot_general(q, k, (((1,), (1,)), ((), ())), preferred_element_type=F32)
    s = s * (FOX_HEAD_DIM ** -0.5) + decq - deck
    row = lax.broadcasted_iota(jnp.int32, (bq, s_len), 0) + i * bq
    col = lax.broadcasted_iota(jnp.int32, (bq, s_len), 1)
    s = jnp.where(col <= row, s, NEG_INF)
    p = jnp.exp(s - jnp.max(s, axis=-1, keepdims=True))
    return p / jnp.sum(p, axis=-1, keepdims=True)


def _fox_key_spans(s_len, bq):
    n_span = min(4, s_len // bq)
    return [(j + 1) * (s_len // n_span) for j in range(n_span)]


def _fox_span_of(i, s_len, bq):
    span = s_len // min(4, s_len // bq)
    return ((i * bq) // span + 1) * span


def _fox_specs(s_len, heads, bq):
    hd = FOX_HEAD_DIM
    q_spec = pl.BlockSpec((bq, hd), lambda h, i: (i, h))
    k_spec = pl.BlockSpec((s_len, hd), lambda h, i: (0, heads + h))
    v_spec = pl.BlockSpec((s_len, hd), lambda h, i: (0, 2 * heads + h))
    dq_spec = pl.BlockSpec((None, bq, 1), lambda h, i: (h, i, 0))
    dk_spec = pl.BlockSpec((None, 1, s_len), lambda h, i: (h, 0, 0))
    return q_spec, k_spec, v_spec, dq_spec, dk_spec


def _fox_fwd(proj, decq, deck, heads, name):
    s_len = proj.shape[0]
    bq = _pick(s_len, (256, 128))
    q_spec, k_spec, v_spec, dq_spec, dk_spec = _fox_specs(s_len, heads, bq)

    def body(q_ref, k_ref, v_ref, decq_ref, deck_ref, o_ref):
        i = pl.program_id(1)
        for klen in _fox_key_spans(s_len, bq):
            @pl.when(_fox_span_of(i, s_len, bq) == klen)
            def _(klen=klen):
                pn = _fox_scores(q_ref[...].astype(BF16), k_ref[:klen, :].astype(BF16), decq_ref[...],
                                 deck_ref[:, :klen], i, bq)
                o_ref[...] = jnp.dot(pn.astype(BF16), v_ref[:klen, :].astype(BF16),
                                     preferred_element_type=F32).astype(BF16)

    return pl.pallas_call(body, name=name, grid=(heads, s_len // bq),
                          in_specs=[q_spec, k_spec, v_spec, dq_spec, dk_spec], out_specs=q_spec,
                          out_shape=jax.ShapeDtypeStruct((s_len, heads * FOX_HEAD_DIM), BF16),
                          compiler_params=_params("parallel", "parallel"))(proj, proj, proj, decq, deck)


def _fox_bwd(proj, decq, deck, do, heads, name):
    s_len = proj.shape[0]
    d = heads * FOX_HEAD_DIM
    bq = _pick(s_len, (256, 128))
    q_spec, k_spec, v_spec, dq_spec, dk_spec = _fox_specs(s_len, heads, bq)
    acc_spec = pl.BlockSpec((s_len, FOX_HEAD_DIM), lambda h, i: (0, h))
    scale = FOX_HEAD_DIM ** -0.5

    def body(q_ref, k_ref, v_ref, decq_ref, deck_ref, do_ref, dq_ref, dk_ref, dv_ref, ddq_ref, ddk_ref):
        i = pl.program_id(1)

        @pl.when(i == 0)
        def _():
            dk_ref[...] = jnp.zeros_like(dk_ref)
            dv_ref[...] = jnp.zeros_like(dv_ref)
            ddk_ref[...] = jnp.zeros_like(ddk_ref)

        q = q_ref[...].astype(BF16)
        do_b = do_ref[...]
        for klen in _fox_key_spans(s_len, bq):
            @pl.when(_fox_span_of(i, s_len, bq) == klen)
            def _(klen=klen):
                k = k_ref[:klen, :].astype(BF16)
                pn = _fox_scores(q, k, decq_ref[...], deck_ref[:, :klen], i, bq)
                dpn = lax.dot_general(do_b, v_ref[:klen, :].astype(BF16), (((1,), (1,)), ((), ())),
                                      preferred_element_type=F32)
                ds = pn * (dpn - jnp.sum(dpn * pn, axis=-1, keepdims=True))
                ddq_ref[...] = jnp.sum(ds, axis=-1, keepdims=True)
                ddk_ref[:, :klen] -= jnp.sum(ds, axis=0, keepdims=True)
                ds_b = (ds * scale).astype(BF16)
                dq_ref[...] = jnp.dot(ds_b, k, preferred_element_type=F32).astype(BF16)
                dk_ref[:klen, :] += lax.dot_general(ds_b, q, (((0,), (0,)), ((), ())),
                                                    preferred_element_type=F32)
                dv_ref[:klen, :] += lax.dot_general(pn.astype(BF16), do_b, (((0,), (0,)), ((), ())),
                                                    preferred_element_type=F32)

    return pl.pallas_call(
        body, name=name, grid=(heads, s_len // bq),
        in_specs=[q_spec, k_spec, v_spec, dq_spec, dk_spec, q_spec],
        out_specs=[q_spec, acc_spec, acc_spec, dq_spec, dk_spec],
        out_shape=[jax.ShapeDtypeStruct((s_len, d), BF16), jax.ShapeDtypeStruct((s_len, d), F32),
                   jax.ShapeDtypeStruct((s_len, d), F32), jax.ShapeDtypeStruct((heads, s_len, 1), F32),
                   jax.ShapeDtypeStruct((heads, 1, s_len), F32)],
        compiler_params=_params("parallel", "arbitrary"))(proj, proj, proj, decq, deck, do)


def _place():
    x, y, c = lax.axis_index("x"), lax.axis_index("y"), lax.axis_index("c")
    chips = [(1 - x, y), (x, 1 - y), (1 - x, 1 - y)]
    return x, y, c, chips


def _remote(src, dst, send_sems, recv_sems, idx, to):
    return pltpu.make_async_remote_copy(src_ref=src, dst_ref=dst, send_sem=send_sems.at[idx],
                                        recv_sem=recv_sems.at[idx], device_id=to, device_id_type=MESH)


def _row_chunks(rows, want):
    for k in (want, want // 2, want // 4):
        if k >= 1 and rows % (16 * k) == 0:
            return [(j * (rows // k), rows // k) for j in range(k)]
    return [(0, rows)]


D2D_CHUNKS = 8


def _cast_into_slot(w, me, name, dep=None):
    rows, width = w.shape
    br = _row_block(rows, width, budget=4 << 20)

    def body(me_ref, w_ref, *rest):
        rest[-1][...] = w_ref[...].astype(BF16)

    in_specs = [pl.BlockSpec((br, width), lambda i, me_ref: (i, 0))]
    if dep is not None:
        in_specs.append(pl.BlockSpec(dep.shape, lambda i, me_ref: (0, 0)))
    return pl.pallas_call(
        body, name=name,
        grid_spec=pltpu.PrefetchScalarGridSpec(
            num_scalar_prefetch=1, grid=(rows // br,), in_specs=in_specs,
            out_specs=pl.BlockSpec((None, br, width), lambda i, me_ref: (me_ref[0], i, 0))),
        out_shape=jax.ShapeDtypeStruct((N_CHIPS, rows, width), BF16),
        compiler_params=_params("parallel"))(me, w, *([dep] if dep is not None else []))


def _hbm(arr):
    return pltpu.with_memory_space_constraint(arr, pltpu.HBM)


def _token_shape():
    return jax.ShapeDtypeStruct((8, LANES), F32)


def _add_pair(grad, got, c, name):
    _, half, width = got.shape
    br = _row_block(half, width, itemsize=2, budget=3 << 20)
    nb = half // br

    def body(c_ref, a_ref, b_ref, o_ref):
        o_ref[...] = (a_ref[...].astype(F32) + b_ref[...].astype(F32)).astype(BF16)

    spec = pl.BlockSpec((None, br, width), lambda j, i, c_ref: (j, i, 0))
    mine = pl.BlockSpec((None, br, width), lambda j, i, c_ref: (j, c_ref[0] * nb + i, 0))
    return pl.pallas_call(
        body, name=name,
        grid_spec=pltpu.PrefetchScalarGridSpec(num_scalar_prefetch=1, grid=(N_CHIPS, nb),
                                               in_specs=[mine, spec], out_specs=spec),
        out_shape=jax.ShapeDtypeStruct(got.shape, BF16),
        compiler_params=_params("parallel", "parallel"))(c, grad, got)


def _adamw_halves(w, pair, others, pair_t, others_t, m, v, place, name):
    rows, width = w.shape
    half = rows // 2
    br = _row_block(half, width, budget=3 << 19)
    nb = half // br
    c1 = 1.0 - ADAM_B1 ** ADAM_STEP
    c2 = 1.0 - ADAM_B2 ** ADAM_STEP

    def chip_sum(p, o3_ref):
        acc = p.astype(F32)
        for r in range(N_CHIPS - 1):
            acc = acc + o3_ref[r].astype(F32)
        return acc

    def body(place_ref, w_ref, p_ref, o_ref, pt_ref, ot_ref, m_ref, v_ref, g_ref, d_ref, nm_ref, nv_ref):
        grad = jnp.where(pl.program_id(0) == place_ref[0], chip_sum(p_ref[...], o_ref), chip_sum(pt_ref[...], ot_ref))
        new_m = ADAM_B1 * m_ref[...] + (1.0 - ADAM_B1) * grad
        new_v = ADAM_B2 * v_ref[...] + (1.0 - ADAM_B2) * (grad * grad)
        g_ref[...] = grad
        d_ref[...] = -ADAM_LR * ((new_m / c1) / (jnp.sqrt(new_v / c2) + ADAM_EPS) + ADAM_WD * w_ref[...])
        nm_ref[...] = new_m
        nv_ref[...] = new_v

    full = pl.BlockSpec((br, width), lambda h, i, s: (h * nb + i, 0))
    mine_i = lambda h, i, s: jnp.where(h == s[0], i, 0)
    theirs_i = lambda h, i, s: jnp.where(h == s[0], 0, i)
    specs = [full,
             pl.BlockSpec((None, br, width), lambda h, i, s: (s[1], mine_i(h, i, s), 0)),
             pl.BlockSpec((N_CHIPS - 1, br, width), lambda h, i, s: (0, mine_i(h, i, s), 0)),
             pl.BlockSpec((br, width), lambda h, i, s: (theirs_i(h, i, s), 0)),
             pl.BlockSpec((N_CHIPS - 1, br, width), lambda h, i, s: (0, theirs_i(h, i, s), 0)),
             full, full]
    shp = jax.ShapeDtypeStruct(w.shape, F32)
    return pl.pallas_call(
        body, name=name,
        grid_spec=pltpu.PrefetchScalarGridSpec(num_scalar_prefetch=1, grid=(2, nb), in_specs=specs,
                                               out_specs=[full] * 4),
        out_shape=[shp] * 4,
        compiler_params=_params("parallel", "parallel"))(place, w, pair, others, pair_t, others_t, m, v)


class _Transfer:
    def __init__(self, n_sems, build):
        self.n_sems, self.build = n_sems, build


def _copies(src_of, dst_of, land_of, rows, chunks, send, recv, idx, to):
    starts = [_remote(src_of(s, z), dst_of(s, z), send, recv, idx, to) for s, z in _row_chunks(rows, chunks)]
    return starts, _remote(src_of(0, rows), land_of(0, rows), send, recv, idx, to)


def _gather_direct(keys, shapes):
    def build(refs, send, recv):
        x, y, c, chips = _place()
        me = 2 * x + y
        out = []
        for t, key in enumerate(keys):
            half = shapes[t][1] // 2
            for r, chip in enumerate(chips[:2]):
                slot = 2 * chip[0] + chip[1]
                out.append(_copies(lambda s, z, key=key, half=half: refs[key].at[me, pl.ds(c * half + s, z)],
                                   lambda s, z, key=key, half=half: refs[key].at[me, pl.ds(c * half + s, z)],
                                   lambda s, z, key=key, half=half, slot=slot: refs[key].at[slot, pl.ds(c * half + s, z)],
                                   half, 1, send, recv, 2 * t + r, (*chip, c)))
        return out
    return _Transfer(2 * len(keys), build)


def _gather_relay(keys, shapes):
    def build(refs, send, recv):
        x, y, c, chips = _place()
        slot_x, slot_y, slot_d = (2 * ch[0] + ch[1] for ch in chips)
        src_slot = slot_y + c * (slot_x - slot_y)
        to = (x ^ (1 - c), y ^ c, c)
        out = []
        for t, key in enumerate(keys):
            half = shapes[t][1] // 2
            out.append(_copies(lambda s, z, key=key, half=half: refs[key].at[src_slot, pl.ds(c * half + s, z)],
                               lambda s, z, key=key, half=half: refs[key].at[src_slot, pl.ds(c * half + s, z)],
                               lambda s, z, key=key, half=half: refs[key].at[slot_d, pl.ds(c * half + s, z)],
                               half, 1, send, recv, t, to))
        return out
    return _Transfer(len(keys), build)


def _gather_pair(keys, shapes):
    def build(refs, send, recv):
        x, y, c, chips = _place()
        out = []
        for t, key in enumerate(keys):
            half = shapes[t][1] // 2
            for r, chip in enumerate(chips):
                slot = 2 * chip[0] + chip[1]
                mine = lambda s, z, key=key, half=half, slot=slot: refs[key].at[slot, pl.ds(c * half + s, z)]
                land = lambda s, z, key=key, half=half, slot=slot: refs[key].at[slot, pl.ds((1 - c) * half + s, z)]
                out.append(_copies(mine, mine, land, half, D2D_CHUNKS, send, recv, 3 * t + r, (x, y, 1 - c)))
        return out
    return _Transfer(3 * len(keys), build)


def _grad_pair(keys, lands, shapes):
    def build(refs, send, recv):
        x, y, c, _ = _place()
        out = []
        for t, (key, land) in enumerate(zip(keys, lands)):
            half = shapes[t][1] // 2
            for j in range(N_CHIPS):
                out.append(_copies(
                    lambda s, z, key=key, half=half, j=j: refs[key].at[j, pl.ds((1 - c) * half + s, z)],
                    lambda s, z, land=land, j=j: refs[land].at[j, pl.ds(s, z)],
                    lambda s, z, land=land, j=j: refs[land].at[j, pl.ds(s, z)],
                    half, 2, send, recv, N_CHIPS * t + j, (x, y, 1 - c)))
        return out
    return _Transfer(N_CHIPS * len(keys), build)


def _grad_chips(keys, lands, shapes):
    def build(refs, send, recv):
        x, y, c, chips = _place()
        out = []
        for t, (key, land) in enumerate(zip(keys, lands)):
            rows = shapes[t][1]
            for r, chip in enumerate(chips):
                slot = 2 * chip[0] + chip[1]
                out.append(_copies(lambda s, z, key=key, slot=slot: refs[key].at[slot, pl.ds(s, z)],
                                   lambda s, z, land=land, r=r: refs[land].at[r, pl.ds(s, z)],
                                   lambda s, z, land=land, r=r: refs[land].at[r, pl.ds(s, z)],
                                   rows, 1, send, recv, 3 * t + r, (*chip, c)))
        return out
    return _Transfer(3 * len(keys), build)


def _grad_join(pairs, others, pair_lands, other_lands, shapes):
    def build(refs, send, recv):
        x, y, c, _ = _place()
        me = 2 * x + y
        sibling = (x, y, 1 - c)
        out = []
        for t, (pair, other, pair_land, other_land) in enumerate(zip(pairs, others, pair_lands, other_lands)):
            rows = shapes[t][1]
            land = lambda s, z, k=pair_land: refs[k].at[pl.ds(s, z)]
            out.append(_copies(lambda s, z, k=pair: refs[k].at[me, pl.ds(s, z)], land, land,
                               rows, D2D_CHUNKS, send, recv, N_CHIPS * t, sibling))
            for r in range(N_CHIPS - 1):
                land = lambda s, z, k=other_land, r=r: refs[k].at[r, pl.ds(s, z)]
                out.append(_copies(lambda s, z, k=other, r=r: refs[k].at[r, pl.ds(s, z)], land, land,
                                   rows, D2D_CHUNKS // 2, send, recv, N_CHIPS * t + 1 + r, sibling))
        return out
    return _Transfer(N_CHIPS * len(pairs), build)


def _comm_call(name, arrays, waits, starts, after):
    keys = list(arrays)
    n, nw, ns = len(keys), len(waits), len(starts)

    def body(*refs):
        in_sems = refs[n:n + 2 * nw]
        base = n + 2 * nw + 1
        out_sems = refs[base:base + 2 * ns]
        bufs = dict(zip(keys, refs[base + 2 * ns:base + 2 * ns + n]))
        token = refs[base + 2 * ns + n]
        for k, (transfer, _, _) in enumerate(waits):
            for _, whole in transfer.build(bufs, in_sems[2 * k], in_sems[2 * k + 1]):
                whole.wait_send()
                whole.wait_recv()
        for k, transfer in enumerate(starts):
            for chunks, _ in transfer.build(bufs, out_sems[2 * k], out_sems[2 * k + 1]):
                for cp in chunks:
                    cp.start()
        token[...] = jnp.zeros_like(token)

    sem_shapes = []
    for transfer in starts:
        sem_shapes += [pltpu.SemaphoreType.DMA((transfer.n_sems,))] * 2
    operands = [_hbm(arrays[k]) for k in keys]
    for _, send, recv in waits:
        operands += [send, recv]
    res = pl.pallas_call(
        body, name=name, in_specs=[HBM] * n + [SEM] * (2 * nw) + [pl.BlockSpec(memory_space=pl.ANY)],
        out_specs=[SEM] * (2 * ns) + [HBM] * n + [pl.BlockSpec(memory_space=pltpu.VMEM)],
        out_shape=sem_shapes + [pltpu.HBM(arrays[k].shape, arrays[k].dtype) for k in keys] + [_token_shape()],
        input_output_aliases={t: 2 * ns + t for t in range(n)},
        compiler_params=pltpu.CompilerParams(has_side_effects=EFFECT),
    )(*operands, after)
    sems = [(res[2 * k], res[2 * k + 1]) for k in range(ns)]
    return dict(zip(keys, res[2 * ns:2 * ns + n])), sems, res[2 * ns + n]


def _device_gather(part_key, all_key, rows):
    def build(refs, send, recv):
        x, y, c, _ = _place()
        me = 4 * x + 2 * y + c
        out = []
        for r in range(1, N_DEV):
            peer = (x ^ (r >> 2), y ^ ((r >> 1) & 1), c ^ (r & 1))
            theirs = 4 * peer[0] + 2 * peer[1] + peer[2]
            out.append(_copies(lambda s, z: refs[part_key].at[pl.ds(s, z)],
                               lambda s, z: refs[all_key].at[me, pl.ds(s, z)],
                               lambda s, z, theirs=theirs: refs[all_key].at[theirs, pl.ds(s, z)],
                               rows, 1, send, recv, r - 1, peer))
        return out
    return _Transfer(N_DEV - 1, build)


def _sum_devices(parts, name):
    _, rows, width = parts.shape
    br = _row_block(rows, width, budget=1 << 19)

    def body(p_ref, o_ref):
        acc = p_ref[0]
        for dev in range(1, N_DEV):
            acc = acc + p_ref[dev]
        o_ref[...] = acc

    return pl.pallas_call(body, name=name, grid=(rows // br,),
                          in_specs=[pl.BlockSpec((N_DEV, br, width), lambda i: (0, i, 0))],
                          out_specs=pl.BlockSpec((br, width), lambda i: (i, 0)),
                          out_shape=jax.ShapeDtypeStruct((rows, width), F32),
                          compiler_params=_params("parallel"))(parts)


INPUT_NAMES = None


def _weight_names():
    names = []
    for i, kind in enumerate(("gmlp", "swa", "fox", "gmlp")):
        p = f"l{i}_"
        names += [p + "ffn1_norm", p + "ffn1_wi", p + "ffn1_wo", p + "mix_norm", p + "mix_win"]
        if kind == "gmlp":
            names += [p + "gmlp_vnorm", p + "gmlp_ws", p + "gmlp_bs"]
        elif kind == "swa":
            names += [p + "swa_sinks"]
        else:
            names += [p + "fox_bf"]
        names += [p + "mix_wout", p + "ffn2_norm", p + "ffn2_wi", p + "ffn2_wo"]
    return names + ["final_norm"]


WEIGHTS = _weight_names()
MIXERS = ("gmlp", "swa", "fox", "gmlp")
BIG = ("ffn1_wi", "ffn1_wo", "mix_win", "mix_wout", "ffn2_wi", "ffn2_wo")


def _ffn_fwd(h, gain, wi, wo, tag, dep=None):
    n = _rms_fwd(h, gain, tag + "_norm", dep=dep)
    z = _matmul(n, wi, name=tag + "_up", out_dtype=BF16)
    a = _swiglu_fwd(z, tag + "_act")
    f, d = wo.shape[0] * wo.shape[1], wo.shape[2]
    out = _matmul(a, wo.reshape(f, d), name=tag + "_down", out_dtype=F32, scale=0.5, resid=h)
    return out, (h, n, z, a)


def _ffn_bwd(dout, saved, gain, wi, wo, tag, dep=None):
    h, n, z, a = saved
    f, d = wo.shape[0] * wo.shape[1], wo.shape[2]
    da = _matmul(dout, wo.reshape(f, d), tb=True, name=tag + "_bdown", out_dtype=BF16, scale=0.5, dep=dep)
    dwo = _matmul(a, dout, ta=True, name=tag + "_gdown", out_dtype=BF16, scale=0.5, dep=dep)
    dz = _swiglu_bwd(z, da, tag + "_bact")
    dn = _matmul(dz, wi, tb=True, name=tag + "_bup", out_dtype=F32)
    dwi = _matmul(n, dz, ta=True, name=tag + "_gup", out_dtype=BF16, out_shards=N_CHIPS)
    dh, dgain = _norm_bwd(h, gain, dn, dout, tag + "_bnorm")
    return dh, dgain, dwi, dwo.reshape(wo.shape)


def _natural(w_sharded, pad_to):
    ns, rows, csh = w_sharded.shape
    nat = jnp.transpose(w_sharded, (1, 0, 2)).reshape(rows, ns * csh)
    extra = (-nat.shape[1]) % pad_to
    return jnp.pad(nat, ((0, 0), (0, extra))) if extra else nat


def _mixer_fwd(kind, h, p, tag, dep=None):
    s_len, d = h.shape
    n = _rms_fwd(h, p["mix_norm"], tag + "_norm", dep=dep)
    wout = p["mix_wout"].reshape(d, d)
    if kind == "gmlp":
        zp = _matmul(n, p["mix_win"], name=tag + "_in", out_dtype=BF16)
        y = _gmlp_fwd(zp, p["gmlp_vnorm"], p["gmlp_ws"], p["gmlp_bs"], tag + "_gate")
        saved = (h, n, zp, y)
    elif kind == "swa":
        qkv = _matmul(n, p["mix_win"], name=tag + "_in", out_dtype=F32)
        q, k, v = _rope_fwd(qkv, tag + "_rope")
        y = _swa_fwd(q, k, v, p["swa_sinks"], tag + "_attn")
        saved = (h, n, q, k, v, y)
    else:
        heads = d // FOX_HEAD_DIM
        win = _natural(p["mix_win"], LANES)
        proj = _matmul(n, win, name=tag + "_in", out_dtype=F32)
        bf_row = jnp.pad(p["fox_bf"], (0, LANES - heads)).reshape(1, LANES)
        dec = _fox_decay(proj, bf_row, 3 * heads, tag + "_decay")
        dec_t = dec[:, :heads].T
        decq, deck = dec_t.reshape(heads, s_len, 1), dec_t.reshape(heads, 1, s_len)
        y = _fox_fwd(proj, decq, deck, heads, tag + "_attn")
        saved = (h, n, win, proj, bf_row, decq, deck, y)
    out = _matmul(y, wout, name=tag + "_out", out_dtype=F32, resid=h)
    return out, saved


def _mixer_bwd(kind, dout, saved, p, tag, dep=None):
    h, n = saved[0], saved[1]
    y = saved[-1]
    s_len, d = h.shape
    wout = p["mix_wout"].reshape(d, d)
    grads = {}
    dy = _matmul(dout, wout, tb=True, name=tag + "_bout", out_dtype=BF16, dep=dep)
    grads["mix_wout"] = _matmul(y, dout, ta=True, name=tag + "_gout", out_dtype=BF16,
                                dep=dep).reshape(p["mix_wout"].shape)
    if kind == "gmlp":
        zp = saved[2]
        dzp, dws, dbst, dvg = _gmlp_bwd(zp, dy, p["gmlp_vnorm"], p["gmlp_ws"], p["gmlp_bs"], tag + "_bgate")
        grads.update(gmlp_ws=dws, gmlp_bs=dbst.T, gmlp_vnorm=dvg.reshape(d))
        dn = _matmul(dzp, p["mix_win"], tb=True, name=tag + "_bin", out_dtype=F32)
        grads["mix_win"] = _matmul(n, dzp, ta=True, name=tag + "_gin", out_dtype=BF16, out_shards=N_CHIPS)
    elif kind == "swa":
        q, k, v = saved[2:5]
        dq, dk, dv, dsinks = _swa_bwd(q, k, v, p["swa_sinks"], dy, tag + "_battn")
        grads["swa_sinks"] = dsinks[0, :p["swa_sinks"].shape[0]]
        dqkv = _rope_bwd(dq, dk, dv, tag + "_brope")
        dn = _matmul(dqkv, p["mix_win"], tb=True, name=tag + "_bin", out_dtype=F32)
        grads["mix_win"] = _matmul(n, dqkv, ta=True, name=tag + "_gin", out_dtype=BF16, out_shards=N_CHIPS)
    else:
        win, proj, bf_row, decq, deck = saved[2:7]
        heads = d // FOX_HEAD_DIM
        dq, dk, dv, ddq, ddk = _fox_bwd(proj, decq, deck, dy, heads, tag + "_battn")
        widen = lambda t: jnp.pad(t.reshape(heads, s_len).T, ((0, 0), (0, LANES - heads)))
        dfl, dbf = _fox_decay_bwd(widen(ddq), widen(ddk), proj, bf_row, 3 * heads, heads, tag + "_bdecay")
        grads["fox_bf"] = dbf[0, :heads]
        dproj = jnp.concatenate([dq, dk.astype(BF16), dv.astype(BF16), dfl], axis=1)
        dn = _matmul(dproj, win, tb=True, name=tag + "_bin", out_dtype=F32)
        dwin = _matmul(n, dproj, ta=True, name=tag + "_gin", out_dtype=BF16)
        ns, rows, csh = p["mix_win"].shape
        grads["mix_win"] = jnp.transpose(dwin[:, :ns * csh].reshape(rows, ns, csh), (1, 0, 2))
    dh, dgain = _norm_bwd(h, p["mix_norm"], dn, dout, tag + "_bnorm")
    grads["mix_norm"] = dgain.reshape(d)
    return dh, grads


def _pack_small(arrays):
    flat = jnp.concatenate([a.reshape(-1).astype(F32) for a in arrays])
    pad = (-flat.shape[0]) % (512 * LANES)
    return jnp.pad(flat, (0, pad)).reshape(-1, LANES)


def _unpack_small(packed, like):
    flat, out, pos = packed.reshape(-1), [], 0
    for a in like:
        out.append(flat[pos:pos + a.size].reshape(a.shape))
        pos += a.size
    return out


def _step(inp):
    x, target = inp["x"][0], inp["loss_target"][0]
    d = x.shape[1]

    core = lax.axis_index("c").astype(jnp.int32).reshape(1)
    chip = (2 * lax.axis_index("x") + lax.axis_index("y")).astype(jnp.int32).reshape(1)
    place = jnp.concatenate([core, chip])

    groups = []
    for i in range(len(MIXERS)):
        groups += [(i, "ffn1", [f"l{i}_ffn1_wi", f"l{i}_ffn1_wo"]), (i, "mix", [f"l{i}_mix_win", f"l{i}_mix_wout"]),
                   (i, "ffn2", [f"l{i}_ffn2_wi", f"l{i}_ffn2_wo"])]

    def layer_params(i, full):
        p = {nm[len(f"l{i}_"):]: inp[nm] for nm in WEIGHTS if nm.startswith(f"l{i}_")}
        p.update({nm[len(f"l{i}_"):]: w for nm, w in full.items()})
        return p

    n_groups = len(groups)
    valid = lambda k: 0 <= k < n_groups

    bufs = {}
    full_shapes = lambda names: [(N_CHIPS, *inp[nm].shape) for nm in names]
    direct = [_gather_direct(names, full_shapes(names)) for _, _, names in groups]
    relay = [_gather_relay(names, full_shapes(names)) for _, _, names in groups]
    to_pair = [_gather_pair(names, full_shapes(names)) for _, _, names in groups]
    sems = {}

    def cast_groups(which, dep):
        for g in which:
            for nm in groups[g][2]:
                bufs[nm] = _cast_into_slot(inp[nm], chip, nm + "_cast", dep=dep)

    def gather_step(step, after):
        waits, starts, tags, keys = [], [], [], []
        for kind, transfers, g, begin in (("pair", to_pair, step, False), ("relay", relay, step + 1, False),
                                          ("pair", to_pair, step + 1, True), ("direct", direct, step + 2, False),
                                          ("relay", relay, step + 2, True), ("direct", direct, step + 3, True)):
            if not valid(g):
                continue
            keys += [nm for nm in groups[g][2] if nm not in keys]
            if begin:
                starts.append(transfers[g])
                tags.append((kind, g))
            else:
                waits.append((transfers[g], *sems.pop((kind, g))))
        new, started, token = _comm_call(f"gather_step{step + 3}", {k: bufs[k] for k in keys}, waits, starts, after)
        bufs.update(new)
        sems.update(zip(tags, started))
        return token

    cast_groups([0], None)
    token = gather_step(-3, x)
    cast_groups(range(1, 6), token)
    token = gather_step(-2, bufs[groups[5][2][-1]])
    cast_groups(range(6, n_groups), token)
    token = gather_step(-1, bufs[groups[-1][2][-1]])
    h, saved, fulls = x, [], []
    for g, (i, part, names) in enumerate(groups):
        token = gather_step(g, h)
        full = {nm: bufs[nm] for nm in names}
        p = layer_params(i, full)
        if part == "mix":
            h, s = _mixer_fwd(MIXERS[i], h, p, f"l{i}_mix", dep=token)
        else:
            h, s = _ffn_fwd(h, p[part + "_norm"], p[part + "_wi"], p[part + "_wo"], f"l{i}_{part}", dep=token)
        saved.append(s)
        fulls.append(full)
    loss_part, dh, dfinal = _loss_head(h, inp["final_norm"], target, "loss_head")
    loss = lax.psum(loss_part, ("x", "y", "c"))

    small_grads = {"final_norm": dfinal.reshape(d)}
    outs = {}
    work = {}
    stage = {}
    small_names = [nm for nm in WEIGHTS if nm.split("_", 1)[1] not in BIG]
    last_small = "l0_ffn1_norm"
    small_sets = {}

    def small_start(tag, names, after):
        part = _pack_small([small_grads[nm] for nm in names])
        device = 4 * lax.axis_index("x") + 2 * lax.axis_index("y") + lax.axis_index("c")
        work[tag + "#part"] = part
        work[tag + "#all"] = lax.dynamic_update_slice(jnp.zeros((N_DEV, *part.shape), F32), part[None],
                                                      (device, 0, 0))
        stage[(tag, 0)] = _device_gather(tag + "#part", tag + "#all", part.shape[0])
        stage_keys[(tag, 0)] = [tag + "#part", tag + "#all"]
        small_sets[tag] = names
        return comm(f"small_{tag}_start", [], [(tag, 0)], after)

    def small_finish(tag, after):
        names = small_sets[tag]
        comm(f"small_{tag}_wait", [(tag, 0)], [], after)
        total = _sum_devices(work[tag + "#all"], f"small_{tag}_sum")
        like = [inp[nm] for nm in names]
        upd = _adamw(_pack_small(like), total, _pack_small([inp["m_" + nm] for nm in names]),
                     _pack_small([inp["v_" + nm] for nm in names]), f"small_{tag}_adamw")
        unpacked = [_unpack_small(t, like) for t in (total, *upd)]
        for k, nm in enumerate(names):
            outs[nm] = tuple(u[k] for u in unpacked)
        return upd[0]

    def comm(name, transfers_to_wait, transfers_to_start, after):
        waits = [(stage[k], *sems.pop(k)) for k in transfers_to_wait if valid(k[1])]
        starts = [k for k in transfers_to_start if valid(k[1])]
        if not waits and not starts:
            return after
        keys = []
        for k in [k for k in transfers_to_wait if valid(k[1])] + starts:
            keys += [key for key in stage_keys[k] if key not in keys]
        new, started, token = _comm_call(name, {k: work[k] for k in keys}, waits, [stage[k] for k in starts], after)
        work.update(new)
        sems.update(zip(starts, started))
        return token

    stage_keys = {}

    def reduce_step(g, after):
        token = comm(f"rs_pair_step{n_groups - 1 - g}", [("pair", g + 1)], [("pair", g)], after)
        if valid(g + 1):
            names = groups[g + 1][2]
            for nm in names:
                work[nm + "#sum"] = _add_pair(work[nm + "#grad"], work[nm + "#got"], core, nm + "_rs_add")
                work[nm + "#others"] = lax.empty((N_CHIPS - 1, *work[nm + "#sum"].shape[1:]), BF16)
            shapes = [work[nm + "#sum"].shape for nm in names]
            stage[("chips", g + 1)] = _grad_chips([nm + "#sum" for nm in names], [nm + "#others" for nm in names], shapes)
            stage_keys[("chips", g + 1)] = [nm + sfx for nm in names for sfx in ("#sum", "#others")]
        token = comm(f"rs_chips_step{n_groups - 1 - g}", [("chips", g + 2)], [("chips", g + 1)], token)
        if valid(g + 2):
            names = groups[g + 2][2]
            for nm in names:
                work[nm + "#sum_t"] = lax.empty(work[nm + "#sum"].shape[1:], BF16)
                work[nm + "#others_t"] = lax.empty(work[nm + "#others"].shape, BF16)
            sfxs = ("#sum", "#others", "#sum_t", "#others_t")
            stage[("join", g + 2)] = _grad_join(*[[nm + sfx for nm in names] for sfx in sfxs],
                                                [work[nm + "#sum"].shape for nm in names])
            stage_keys[("join", g + 2)] = [nm + sfx for nm in names for sfx in sfxs]
        token = comm(f"rs_join_step{n_groups - 1 - g}", [("join", g + 3)], [("join", g + 2)], token)
        if valid(g + 3):
            for nm in groups[g + 3][2]:
                outs[nm] = tuple(_adamw_halves(inp[nm], work[nm + "#sum"], work[nm + "#others"], work[nm + "#sum_t"],
                                               work[nm + "#others_t"], inp["m_" + nm], inp["v_" + nm], place,
                                               nm + "_adamw"))
        return token

    dep = None
    for g in reversed(range(n_groups)):
        i, part, names = groups[g]
        p = layer_params(i, fulls[g])
        if part == "mix":
            dh, mg = _mixer_bwd(MIXERS[i], dh, saved[g], p, f"l{i}_mix", dep=dep)
            grads = [mg.pop("mix_win"), mg.pop("mix_wout")]
            small_grads.update({f"l{i}_{key}": val for key, val in mg.items()})
        else:
            dh, g_norm, dwi, dwo = _ffn_bwd(dh, saved[g], p[part + "_norm"], p[part + "_wi"], p[part + "_wo"],
                                            f"l{i}_{part}", dep=dep)
            small_grads[f"l{i}_{part}_norm"] = g_norm.reshape(d)
            grads = [dwi, dwo]
        for nm, gr in zip(names, grads):
            work[nm + "#grad"] = gr
            work[nm + "#got"] = lax.empty((gr.shape[0], gr.shape[1] // 2, gr.shape[2]), BF16)
        stage[("pair", g)] = _grad_pair([nm + "#grad" for nm in names], [nm + "#got" for nm in names],
                                        [gr.shape for gr in grads])
        stage_keys[("pair", g)] = [nm + sfx for nm in names for sfx in ("#grad", "#got")]
        dep = reduce_step(g, dh)
        if g == 1:
            dep = small_start("early", [nm for nm in small_names if nm != last_small], dep)
    dep = small_start("late", [last_small], dep)
    dep = reduce_step(-1, dep)
    dep = small_finish("early", dep)
    dep = reduce_step(-2, dep)
    dep = small_finish("late", dep)
    reduce_step(-3, dep)

    result = [loss, dh[None]]
    for part in range(4):
        result += [outs[nm][part] for nm in WEIGHTS]
    return tuple(result)


def kernel(x, l0_ffn1_norm, l0_ffn1_wi, l0_ffn1_wo, l0_mix_norm, l0_mix_win, l0_gmlp_vnorm, l0_gmlp_ws, l0_gmlp_bs, l0_mix_wout, l0_ffn2_norm, l0_ffn2_wi, l0_ffn2_wo, l1_ffn1_norm, l1_ffn1_wi, l1_ffn1_wo, l1_mix_norm, l1_mix_win, l1_swa_sinks, l1_mix_wout, l1_ffn2_norm, l1_ffn2_wi, l1_ffn2_wo, l2_ffn1_norm, l2_ffn1_wi, l2_ffn1_wo, l2_mix_norm, l2_mix_win, l2_fox_bf, l2_mix_wout, l2_ffn2_norm, l2_ffn2_wi, l2_ffn2_wo, l3_ffn1_norm, l3_ffn1_wi, l3_ffn1_wo, l3_mix_norm, l3_mix_win, l3_gmlp_vnorm, l3_gmlp_ws, l3_gmlp_bs, l3_mix_wout, l3_ffn2_norm, l3_ffn2_wi, l3_ffn2_wo, final_norm, loss_target, m_l0_ffn1_norm, m_l0_ffn1_wi, m_l0_ffn1_wo, m_l0_mix_norm, m_l0_mix_win, m_l0_gmlp_vnorm, m_l0_gmlp_ws, m_l0_gmlp_bs, m_l0_mix_wout, m_l0_ffn2_norm, m_l0_ffn2_wi, m_l0_ffn2_wo, m_l1_ffn1_norm, m_l1_ffn1_wi, m_l1_ffn1_wo, m_l1_mix_norm, m_l1_mix_win, m_l1_swa_sinks, m_l1_mix_wout, m_l1_ffn2_norm, m_l1_ffn2_wi, m_l1_ffn2_wo, m_l2_ffn1_norm, m_l2_ffn1_wi, m_l2_ffn1_wo, m_l2_mix_norm, m_l2_mix_win, m_l2_fox_bf, m_l2_mix_wout, m_l2_ffn2_norm, m_l2_ffn2_wi, m_l2_ffn2_wo, m_l3_ffn1_norm, m_l3_ffn1_wi, m_l3_ffn1_wo, m_l3_mix_norm, m_l3_mix_win, m_l3_gmlp_vnorm, m_l3_gmlp_ws, m_l3_gmlp_bs, m_l3_mix_wout, m_l3_ffn2_norm, m_l3_ffn2_wi, m_l3_ffn2_wo, m_final_norm, v_l0_ffn1_norm, v_l0_ffn1_wi, v_l0_ffn1_wo, v_l0_mix_norm, v_l0_mix_win, v_l0_gmlp_vnorm, v_l0_gmlp_ws, v_l0_gmlp_bs, v_l0_mix_wout, v_l0_ffn2_norm, v_l0_ffn2_wi, v_l0_ffn2_wo, v_l1_ffn1_norm, v_l1_ffn1_wi, v_l1_ffn1_wo, v_l1_mix_norm, v_l1_mix_win, v_l1_swa_sinks, v_l1_mix_wout, v_l1_ffn2_norm, v_l1_ffn2_wi, v_l1_ffn2_wo, v_l2_ffn1_norm, v_l2_ffn1_wi, v_l2_ffn1_wo, v_l2_mix_norm, v_l2_mix_win, v_l2_fox_bf, v_l2_mix_wout, v_l2_ffn2_norm, v_l2_ffn2_wi, v_l2_ffn2_wo, v_l3_ffn1_norm, v_l3_ffn1_wi, v_l3_ffn1_wo, v_l3_mix_norm, v_l3_mix_win, v_l3_gmlp_vnorm, v_l3_gmlp_ws, v_l3_gmlp_bs, v_l3_mix_wout, v_l3_ffn2_norm, v_l3_ffn2_wi, v_l3_ffn2_wo, v_final_norm):
    return _step(dict(locals()))
```

```python
import functools
import math

import jax
import jax.numpy as jnp
from jax import lax
from jax.experimental import pallas as pl
from jax.experimental.pallas import tpu as pltpu

F32 = jnp.float32
BF16 = jnp.bfloat16

NORM_EPS = 1e-5
NEG_INF = -1e30
BLOCK = 128
GMLP_GROUPS = 16
SWA_HEAD_DIM = 64
SWA_GROUP = 8
ROPE_DIM = SWA_HEAD_DIM // 4
ROPE_THETA = 500000.0
FOX_HEAD_DIM = 128
ADAM_LR = 0.001
ADAM_B1 = 0.9
ADAM_B2 = 0.999
ADAM_EPS = 1e-08
ADAM_WD = 0.01
ADAM_STEP = 10
N_CHIPS = 4
N_DEV = 8
LANES = 128
VMEM_LIMIT = 56 * 1024 * 1024
MESH = pl.DeviceIdType.MESH
HBM = pl.BlockSpec(memory_space=pltpu.HBM)
SEM = pl.BlockSpec(memory_space=pltpu.SEMAPHORE)
EFFECT = pltpu.SideEffectType.DATAFLOW_SIDE_EFFECTING

MM_TILES = (1024, 1408, 896, 640, 512, 384, 256, 128)
K_TILES = (2816,) + MM_TILES


def _pick(n, prefs):
    for p in prefs:
        if p <= n and n % p == 0:
            return p
    return n


def _params(*sem):
    return pltpu.CompilerParams(dimension_semantics=sem or None, vmem_limit_bytes=VMEM_LIMIT)


def _cols(arr):
    return arr.shape[-1] * (arr.shape[0] if arr.ndim == 3 else 1)


def _mat_spec(arr, rb, cb, ridx, cidx):
    if arr.ndim == 2:
        return pl.BlockSpec((rb, cb), lambda j, i, k: (ridx(j, i, k), cidx(j, i, k)))
    per = arr.shape[2] // cb
    return pl.BlockSpec((None, rb, cb),
                        lambda j, i, k: (cidx(j, i, k) // per, ridx(j, i, k), cidx(j, i, k) % per))


def _matmul(a, b, *, name, out_dtype, ta=False, tb=False, out_shards=1, scale=1.0, resid=None, dep=None):
    m_dim, k_dim = (a.shape[1], a.shape[0]) if ta else a.shape
    n_dim = b.shape[-2] if tb else _cols(b)
    assert k_dim == (_cols(b) if tb else b.shape[-2]), (a.shape, b.shape, ta, tb)
    n_unit = n_dim // out_shards
    if b.ndim == 3 and not tb:
        n_unit = math.gcd(n_unit, b.shape[2])
    k_unit = b.shape[2] if (b.ndim == 3 and tb) else k_dim
    bm = _pick(m_dim, MM_TILES)
    bn = _pick(n_unit, MM_TILES)
    bk = k_unit if k_unit <= 2048 else _pick(k_unit, K_TILES)
    nk = k_dim // bk
    i_of, j_of, k_of = (lambda j, i, k: i), (lambda j, i, k: j), (lambda j, i, k: k)
    a_spec = _mat_spec(a, bk, bm, k_of, i_of) if ta else _mat_spec(a, bm, bk, i_of, k_of)
    b_spec = _mat_spec(b, bn, bk, j_of, k_of) if tb else _mat_spec(b, bk, bn, k_of, j_of)
    out_shape = (m_dim, n_dim) if out_shards == 1 else (out_shards, m_dim, n_dim // out_shards)
    out = jax.ShapeDtypeStruct(out_shape, out_dtype)
    o_spec = _mat_spec(out, bm, bn, i_of, j_of)
    dims = (((0 if ta else 1,), (1 if tb else 0,)), ((), ()))
    operands, in_specs = [a, b], [a_spec, b_spec]
    if resid is not None:
        operands.append(resid)
        in_specs.append(_mat_spec(resid, bm, bn, i_of, j_of))
    if dep is not None:
        operands.append(dep)
        in_specs.append(pl.BlockSpec(dep.shape, lambda j, i, k: (0, 0)))
    n_in = len(operands)

    def body(*refs):
        a_ref, b_ref = refs[0], refs[1]
        r_ref = refs[2] if resid is not None else None
        o_ref = refs[n_in]
        part = lax.dot_general(a_ref[...].astype(BF16), b_ref[...].astype(BF16), dims,
                               preferred_element_type=F32)

        def finish(acc):
            val = acc * scale if scale != 1.0 else acc
            if r_ref is not None:
                val = r_ref[...] + val
            o_ref[...] = val.astype(o_ref.dtype)

        if nk == 1:
            finish(part)
        else:
            acc_ref = refs[-1]
            k = pl.program_id(2)

            @pl.when(k == 0)
            def _():
                acc_ref[...] = part

            @pl.when(k > 0)
            def _():
                acc_ref[...] += part

            @pl.when(k == nk - 1)
            def _():
                finish(acc_ref[...])

    return pl.pallas_call(
        body, name=name, grid=(n_dim // bn, m_dim // bm, nk),
        in_specs=in_specs, out_specs=o_spec, out_shape=out,
        scratch_shapes=[pltpu.VMEM((bm, bn), F32)] if nk > 1 else [],
        compiler_params=_params("parallel", "parallel", "arbitrary"),
    )(*operands)


def _row_block(rows, width, itemsize=4, budget=2 << 20):
    best = None
    for br in range(16, rows + 1, 16):
        if rows % br == 0 and br * width * itemsize <= budget:
            best = br
    return best or rows


def _rms_fwd(h, g, name, dep=None):
    s_len, d = h.shape
    br = _row_block(s_len, d)

    def body(h_ref, g_ref, *rest):
        o_ref = rest[-1]
        x = h_ref[...]
        r = lax.rsqrt(jnp.mean(x * x, axis=-1, keepdims=True) + NORM_EPS)
        o_ref[...] = (x * r * g_ref[...]).astype(BF16)

    spec = pl.BlockSpec((br, d), lambda i: (i, 0))
    operands = [h, g.reshape(1, d)] + ([dep] if dep is not None else [])
    in_specs = [spec, pl.BlockSpec((1, d), lambda i: (0, 0))]
    if dep is not None:
        in_specs.append(pl.BlockSpec(dep.shape, lambda i: (0, 0)))
    return pl.pallas_call(body, name=name, grid=(s_len // br,), in_specs=in_specs, out_specs=spec,
                          out_shape=jax.ShapeDtypeStruct((s_len, d), BF16),
                          compiler_params=_params("parallel"))(*operands)


def _rms_bwd_rows(x, g, dn):
    r = lax.rsqrt(jnp.mean(x * x, axis=-1, keepdims=True) + NORM_EPS)
    xhat = x * r
    gdn = dn * g
    dx = r * (gdn - xhat * jnp.mean(gdn * xhat, axis=-1, keepdims=True))
    return dx, dn * xhat


def _norm_bwd(h, g, dn, dres, name):
    s_len, d = h.shape
    br = _row_block(s_len, d, budget=1 << 20)

    def body(h_ref, g_ref, dn_ref, dres_ref, dh_ref, dg_ref):
        dx, dg_rows = _rms_bwd_rows(h_ref[...], g_ref[...], dn_ref[...].astype(F32))
        dh_ref[...] = dres_ref[...] + dx

        @pl.when(pl.program_id(0) == 0)
        def _():
            dg_ref[...] = jnp.zeros_like(dg_ref)

        dg_ref[...] += jnp.sum(dg_rows, axis=0, keepdims=True)

    spec = pl.BlockSpec((br, d), lambda i: (i, 0))
    vec = pl.BlockSpec((1, d), lambda i: (0, 0))
    return pl.pallas_call(body, name=name, grid=(s_len // br,),
                          in_specs=[spec, vec, spec, spec], out_specs=[spec, vec],
                          out_shape=[jax.ShapeDtypeStruct((s_len, d), F32),
                                     jax.ShapeDtypeStruct((1, d), F32)],
                          compiler_params=_params("arbitrary"))(h, g.reshape(1, d), dn, dres)


def _sigmoid(x):
    return 0.5 * (1.0 + jnp.tanh(0.5 * x))


def _swiglu_fwd(z, name):
    s_len, f2 = z.shape
    f = f2 // 2
    br = _row_block(s_len, f2, itemsize=2, budget=6 << 20)
    fc = _pick(f, MM_TILES)

    def body(z_ref, a_ref):
        for c0 in range(0, f, fc):
            gate = z_ref[:, c0:c0 + fc].astype(F32)
            up = z_ref[:, f + c0:f + c0 + fc].astype(F32)
            a_ref[:, c0:c0 + fc] = (gate * _sigmoid(gate) * up).astype(BF16)

    return pl.pallas_call(body, name=name, grid=(s_len // br,),
                          in_specs=[pl.BlockSpec((br, f2), lambda i: (i, 0))],
                          out_specs=pl.BlockSpec((br, f), lambda i: (i, 0)),
                          out_shape=jax.ShapeDtypeStruct((s_len, f), BF16),
                          compiler_params=_params("parallel"))(z)


def _swiglu_bwd(z, da, name):
    s_len, f2 = z.shape
    f = f2 // 2
    br = _row_block(s_len, f2, itemsize=2, budget=6 << 20)
    fc = _pick(f, MM_TILES)

    def body(z_ref, da_ref, dz_ref):
        for c0 in range(0, f, fc):
            gate = z_ref[:, c0:c0 + fc].astype(F32)
            up = z_ref[:, f + c0:f + c0 + fc].astype(F32)
            d = da_ref[:, c0:c0 + fc].astype(F32)
            sig = _sigmoid(gate)
            dz_ref[:, c0:c0 + fc] = (d * up * (sig * (1.0 + gate * (1.0 - sig)))).astype(BF16)
            dz_ref[:, f + c0:f + c0 + fc] = (d * gate * sig).astype(BF16)

    return pl.pallas_call(body, name=name, grid=(s_len // br,),
                          in_specs=[pl.BlockSpec((br, f2), lambda i: (i, 0)),
                                    pl.BlockSpec((br, f), lambda i: (i, 0))],
                          out_specs=pl.BlockSpec((br, f2), lambda i: (i, 0)),
                          out_shape=jax.ShapeDtypeStruct((s_len, f2), BF16),
                          compiler_params=_params("parallel"))(z, da)


def _loss_head(h, g, target, name):
    s_len, d = h.shape
    br = _row_block(s_len, d, budget=1 << 20)

    def body(h_ref, g_ref, t_ref, loss_ref, dh_ref, dg_ref):
        x = h_ref[...]
        gain = g_ref[...]
        r = lax.rsqrt(jnp.mean(x * x, axis=-1, keepdims=True) + NORM_EPS)
        err = x * r * gain - t_ref[...]
        part = 0.5 * jnp.sum(jnp.mean(err * err, axis=-1, keepdims=True), axis=0, keepdims=True)
        dx, dg_rows = _rms_bwd_rows(x, gain, err * (1.0 / d))
        dh_ref[...] = dx

        @pl.when(pl.program_id(0) == 0)
        def _():
            dg_ref[...] = jnp.zeros_like(dg_ref)
            loss_ref[...] = jnp.zeros_like(loss_ref)

        dg_ref[...] += jnp.sum(dg_rows, axis=0, keepdims=True)
        loss_ref[...] += jnp.broadcast_to(part, loss_ref.shape)

    spec = pl.BlockSpec((br, d), lambda i: (i, 0))
    vec = pl.BlockSpec((1, d), lambda i: (0, 0))
    one = pl.BlockSpec((1, LANES), lambda i: (0, 0))
    loss, dh, dg = pl.pallas_call(
        body, name=name, grid=(s_len // br,), in_specs=[spec, vec, spec],
        out_specs=[one, spec, vec],
        out_shape=[jax.ShapeDtypeStruct((1, LANES), F32), jax.ShapeDtypeStruct((s_len, d), F32),
                   jax.ShapeDtypeStruct((1, d), F32)],
        compiler_params=_params("arbitrary"))(h, g.reshape(1, d), target)
    return loss[0, 0], dh, dg


def _adamw(w, g, m, v, name):
    rows, width = w.shape
    br = _row_block(rows, width, budget=1 << 20)
    c1 = 1.0 - ADAM_B1 ** ADAM_STEP
    c2 = 1.0 - ADAM_B2 ** ADAM_STEP

    def body(w_ref, g_ref, m_ref, v_ref, d_ref, nm_ref, nv_ref):
        grad = g_ref[...]
        new_m = ADAM_B1 * m_ref[...] + (1.0 - ADAM_B1) * grad
        new_v = ADAM_B2 * v_ref[...] + (1.0 - ADAM_B2) * (grad * grad)
        d_ref[...] = -ADAM_LR * ((new_m / c1) / (jnp.sqrt(new_v / c2) + ADAM_EPS) + ADAM_WD * w_ref[...])
        nm_ref[...] = new_m
        nv_ref[...] = new_v

    spec = pl.BlockSpec((br, width), lambda i: (i, 0))
    shp = jax.ShapeDtypeStruct(w.shape, F32)
    return pl.pallas_call(body, name=name, grid=(rows // br,), in_specs=[spec] * 4,
                          out_specs=[spec] * 3, out_shape=[shp] * 3,
                          compiler_params=_params("parallel"))(w, g, m, v)


def _gelu(x):
    return 0.5 * x * (1.0 + lax.erf(x * (2.0 ** -0.5)))


def _gelu_grad(x):
    return 0.5 * (1.0 + lax.erf(x * (2.0 ** -0.5))) + x * jnp.exp(-0.5 * x * x) * ((2.0 * math.pi) ** -0.5)


def _tril_mask():
    row = lax.broadcasted_iota(jnp.int32, (BLOCK, BLOCK), 0)
    col = lax.broadcasted_iota(jnp.int32, (BLOCK, BLOCK), 1)
    return col <= row


def _gmlp_specs(s_len, d):
    gw = d // GMLP_GROUPS
    zp = pl.BlockSpec((BLOCK, 2 * d), lambda i: (i, 0))
    row = pl.BlockSpec((BLOCK, d), lambda i: (i, 0))
    vec = pl.BlockSpec((1, d), lambda i: (0, 0))
    ws = pl.BlockSpec((GMLP_GROUPS, BLOCK, BLOCK), lambda i: (0, 0, 0))
    bst = pl.BlockSpec((BLOCK, GMLP_GROUPS), lambda i: (0, 0))
    return gw, zp, row, vec, ws, bst


def _gmlp_fwd(zp, vgain, ws, bs, name):
    s_len, d2 = zp.shape
    d = d2 // 2
    gw, zp_spec, row_spec, vec_spec, ws_spec, bst_spec = _gmlp_specs(s_len, d)

    def body(zp_ref, vg_ref, ws_ref, bst_ref, y_ref):
        u = _gelu(zp_ref[:, :d].astype(F32))
        vv = _gelu(zp_ref[:, d:].astype(F32))
        r = lax.rsqrt(jnp.mean(vv * vv, axis=-1, keepdims=True) + NORM_EPS)
        vn = (vv * r * vg_ref[...]).astype(BF16)
        mask = _tril_mask()
        for g in range(GMLP_GROUPS):
            cols = slice(g * gw, (g + 1) * gw)
            wg = jnp.where(mask, ws_ref[g], 0.0).astype(BF16)
            mixed = jnp.dot(wg, vn[:, cols], preferred_element_type=F32) + bst_ref[:, g:g + 1]
            y_ref[:, cols] = (u[:, cols] * mixed).astype(BF16)

    return pl.pallas_call(body, name=name, grid=(s_len // BLOCK,),
                          in_specs=[zp_spec, vec_spec, ws_spec, bst_spec], out_specs=row_spec,
                          out_shape=jax.ShapeDtypeStruct((s_len, d), BF16),
                          compiler_params=_params("parallel"))(zp, vgain.reshape(1, d), ws, bs.T)


def _gmlp_bwd(zp, dy, vgain, ws, bs, name):
    s_len, d2 = zp.shape
    d = d2 // 2
    gw, zp_spec, row_spec, vec_spec, ws_spec, bst_spec = _gmlp_specs(s_len, d)

    def body(zp_ref, dy_ref, vg_ref, ws_ref, bst_ref, dzp_ref, dws_ref, dbst_ref, dvg_ref, dvn_ref):
        @pl.when(pl.program_id(0) == 0)
        def _():
            dws_ref[...] = jnp.zeros_like(dws_ref)
            dbst_ref[...] = jnp.zeros_like(dbst_ref)
            dvg_ref[...] = jnp.zeros_like(dvg_ref)

        zu = zp_ref[:, :d].astype(F32)
        zv = zp_ref[:, d:].astype(F32)
        u = _gelu(zu)
        vv = _gelu(zv)
        r = lax.rsqrt(jnp.mean(vv * vv, axis=-1, keepdims=True) + NORM_EPS)
        vhat = vv * r
        gain = vg_ref[...]
        vn = (vhat * gain).astype(BF16)
        dyf = dy_ref[...].astype(F32)
        dmixed = dyf * u
        dmixed_b = dmixed.astype(BF16)
        mask = _tril_mask()
        lane = lax.broadcasted_iota(jnp.int32, (BLOCK, GMLP_GROUPS), 1)
        dbs_step = jnp.zeros((BLOCK, GMLP_GROUPS), F32)
        for g in range(GMLP_GROUPS):
            cols = slice(g * gw, (g + 1) * gw)
            wg = jnp.where(mask, ws_ref[g], 0.0).astype(BF16)
            mixed = jnp.dot(wg, vn[:, cols], preferred_element_type=F32) + bst_ref[:, g:g + 1]
            dzp_ref[:, cols] = (dyf[:, cols] * mixed * _gelu_grad(zu[:, cols])).astype(BF16)
            dm = dmixed_b[:, cols]
            dw = lax.dot_general(dm, vn[:, cols], (((1,), (1,)), ((), ())), preferred_element_type=F32)
            dws_ref[g] += jnp.where(mask, dw, 0.0)
            dbs_step = dbs_step + jnp.where(lane == g, jnp.sum(dmixed[:, cols], axis=-1, keepdims=True), 0.0)
            dvn_ref[:, cols] = lax.dot_general(wg, dm, (((0,), (0,)), ((), ())), preferred_element_type=F32)
        dbst_ref[...] += dbs_step
        dvn = dvn_ref[...]
        dvg_ref[...] += jnp.sum(dvn * vhat, axis=0, keepdims=True)
        dvhat = dvn * gain
        dvv = r * (dvhat - vhat * jnp.mean(dvhat * vhat, axis=-1, keepdims=True))
        dzp_ref[:, d:] = (dvv * _gelu_grad(zv)).astype(BF16)

    return pl.pallas_call(
        body, name=name, grid=(s_len // BLOCK,),
        in_specs=[zp_spec, row_spec, vec_spec, ws_spec, bst_spec],
        out_specs=[zp_spec, ws_spec, bst_spec, vec_spec],
        out_shape=[jax.ShapeDtypeStruct((s_len, d2), BF16), jax.ShapeDtypeStruct(ws.shape, F32),
                   jax.ShapeDtypeStruct((BLOCK, GMLP_GROUPS), F32), jax.ShapeDtypeStruct((1, d), F32)],
        scratch_shapes=[pltpu.VMEM((BLOCK, d), F32)],
        compiler_params=_params("arbitrary"))(zp, dy, vgain.reshape(1, d), ws, bs.T)


def _rope_tables(s_len, sign):
    half = ROPE_DIM // 2
    inv_freq = ROPE_THETA ** (-(jnp.arange(half, dtype=F32) * 2.0 / ROPE_DIM))
    ang = jnp.arange(s_len, dtype=F32)[:, None] * inv_freq[None, :]
    cos, sin = jnp.cos(ang), jnp.sin(ang) * sign
    pad = jnp.zeros((s_len, SWA_HEAD_DIM - ROPE_DIM), F32)
    zero = jnp.zeros_like(sin)
    cos_t = jnp.concatenate([cos, cos, pad + 1.0], axis=1)
    sin_up = jnp.concatenate([-sin, zero, pad], axis=1)
    sin_dn = jnp.concatenate([zero, sin, pad], axis=1)
    return [jnp.tile(t, (1, LANES // SWA_HEAD_DIM)) for t in (cos_t, sin_up, sin_dn)]


def _rotate(x, cos_t, sin_up, sin_dn):
    width = x.shape[-1]
    half = ROPE_DIM // 2
    reps = width // cos_t.shape[-1]
    if reps > 1:
        cos_t, sin_up, sin_dn = (jnp.tile(t, (1, reps)) for t in (cos_t, sin_up, sin_dn))
    elif reps == 0:
        cos_t, sin_up, sin_dn = (t[:, :width] for t in (cos_t, sin_up, sin_dn))
    return x * cos_t + pltpu.roll(x, width - half, 1) * sin_up + pltpu.roll(x, half, 1) * sin_dn


def _rope_fwd(qkv, name):
    s_len, total = qkv.shape
    wkv = total // (SWA_GROUP + 2)
    wq = SWA_GROUP * wkv
    br = _row_block(s_len, total, budget=2 << 20)
    tables = _rope_tables(s_len, 1.0)

    def body(q_ref, k_ref, v_ref, c_ref, su_ref, sd_ref, qo_ref, ko_ref, vo_ref):
        t = (c_ref[...], su_ref[...], sd_ref[...])
        qo_ref[...] = _rotate(q_ref[...], *t).astype(BF16)
        ko_ref[...] = _rotate(k_ref[...], *t).astype(BF16)
        vo_ref[...] = v_ref[...].astype(BF16)

    qs = pl.BlockSpec((br, wq), lambda i: (i, 0))
    ks = pl.BlockSpec((br, wkv), lambda i: (i, SWA_GROUP))
    vs = pl.BlockSpec((br, wkv), lambda i: (i, SWA_GROUP + 1))
    ts = pl.BlockSpec((br, LANES), lambda i: (i, 0))
    kv_out = pl.BlockSpec((br, wkv), lambda i: (i, 0))
    return pl.pallas_call(
        body, name=name, grid=(s_len // br,), in_specs=[qs, ks, vs, ts, ts, ts],
        out_specs=[qs, kv_out, kv_out],
        out_shape=[jax.ShapeDtypeStruct((s_len, wq), BF16), jax.ShapeDtypeStruct((s_len, wkv), BF16),
                   jax.ShapeDtypeStruct((s_len, wkv), BF16)],
        compiler_params=_params("parallel"))(qkv, qkv, qkv, *tables)


def _rope_bwd(dq, dk, dv, name):
    s_len, wq = dq.shape
    wkv = dk.shape[1]
    br = _row_block(s_len, wq + 2 * wkv, budget=2 << 20)
    tables = _rope_tables(s_len, -1.0)

    def body(q_ref, k_ref, v_ref, c_ref, su_ref, sd_ref, o_ref):
        t = (c_ref[...], su_ref[...], sd_ref[...])
        o_ref[:, :wq] = _rotate(q_ref[...], *t).astype(BF16)
        o_ref[:, wq:wq + wkv] = _rotate(k_ref[...], *t).astype(BF16)
        o_ref[:, wq + wkv:] = v_ref[...].astype(BF16)

    qs = pl.BlockSpec((br, wq), lambda i: (i, 0))
    kvs = pl.BlockSpec((br, wkv), lambda i: (i, 0))
    ts = pl.BlockSpec((br, LANES), lambda i: (i, 0))
    return pl.pallas_call(
        body, name=name, grid=(s_len // br,), in_specs=[qs, kvs, kvs, ts, ts, ts],
        out_specs=pl.BlockSpec((br, wq + 2 * wkv), lambda i: (i, 0)),
        out_shape=jax.ShapeDtypeStruct((s_len, wq + 2 * wkv), BF16),
        compiler_params=_params("parallel"))(dq, dk, dv, *tables)


def _swa_valid(i):
    row = lax.broadcasted_iota(jnp.int32, (BLOCK, 2 * BLOCK), 0)
    col = lax.broadcasted_iota(jnp.int32, (BLOCK, 2 * BLOCK), 1)
    return (col - BLOCK <= row) & (row < col) & ((col >= BLOCK) | (i > 0))


def _swa_specs(wq, wkv):
    q_spec = pl.BlockSpec((BLOCK, wq), lambda i: (i, 0))
    cur = pl.BlockSpec((BLOCK, wkv), lambda i: (i, 0))
    prev = pl.BlockSpec((BLOCK, wkv), lambda i: (jnp.maximum(i - 1, 0), 0))
    sink = pl.BlockSpec(memory_space=pltpu.SMEM)
    return q_spec, cur, prev, sink


def _swa_probs(q_h, k_cat, valid, sink):
    s = lax.dot_general(q_h, k_cat, (((1,), (1,)), ((), ())), preferred_element_type=F32)
    s = jnp.where(valid, s * (SWA_HEAD_DIM ** -0.5), NEG_INF)
    m = jnp.maximum(jnp.max(s, axis=-1, keepdims=True), sink)
    p = jnp.exp(s - m)
    e_sink = jnp.exp(sink - m)
    denom = jnp.sum(p, axis=-1, keepdims=True) + e_sink
    return p / denom, e_sink / denom


def _swa_fwd(q, k, v, sinks, name):
    s_len, wq = q.shape
    wkv = k.shape[1]
    hd = SWA_HEAD_DIM
    q_spec, cur, prev, sink_spec = _swa_specs(wq, wkv)

    def body(q_ref, kc_ref, kp_ref, vc_ref, vp_ref, sink_ref, o_ref):
        valid = _swa_valid(pl.program_id(0))
        for j in range(wkv // hd):
            lanes = slice(j * hd, (j + 1) * hd)
            k_cat = jnp.concatenate([kp_ref[:, lanes], kc_ref[:, lanes]], axis=0)
            v_cat = jnp.concatenate([vp_ref[:, lanes], vc_ref[:, lanes]], axis=0)
            for hh in range(SWA_GROUP):
                h = j * SWA_GROUP + hh
                pn, _ = _swa_probs(q_ref[:, h * hd:(h + 1) * hd], k_cat, valid, sink_ref[h])
                o_ref[:, h * hd:(h + 1) * hd] = jnp.dot(
                    pn.astype(BF16), v_cat, preferred_element_type=F32).astype(BF16)

    return pl.pallas_call(body, name=name, grid=(s_len // BLOCK,),
                          in_specs=[q_spec, cur, prev, cur, prev, sink_spec], out_specs=q_spec,
                          out_shape=jax.ShapeDtypeStruct((s_len, wq), BF16),
                          compiler_params=_params("parallel"))(q, k, k, v, v, sinks)


def _swa_bwd(q, k, v, sinks, do, name):
    s_len, wq = q.shape
    wkv = k.shape[1]
    hd = SWA_HEAD_DIM
    q_spec, cur, prev, sink_spec = _swa_specs(wq, wkv)
    full = pl.BlockSpec((s_len, wkv), lambda i: (0, 0))
    one = pl.BlockSpec((1, LANES), lambda i: (0, 0))
    scale = hd ** -0.5

    def body(q_ref, kc_ref, kp_ref, vc_ref, vp_ref, sink_ref, do_ref, dq_ref, dk_ref, dv_ref, ds_ref):
        i = pl.program_id(0)

        @pl.when(i == 0)
        def _():
            dk_ref[...] = jnp.zeros_like(dk_ref)
            dv_ref[...] = jnp.zeros_like(dv_ref)
            ds_ref[...] = jnp.zeros_like(ds_ref)

        valid = _swa_valid(i)
        lane = lax.broadcasted_iota(jnp.int32, (1, LANES), 1)
        dsink_step = jnp.zeros((1, LANES), F32)
        rows_prev = pl.ds(pl.multiple_of(jnp.maximum(i - 1, 0) * BLOCK, BLOCK), BLOCK)
        rows_cur = pl.ds(pl.multiple_of(i * BLOCK, BLOCK), BLOCK)
        for j in range(wkv // hd):
            lanes = slice(j * hd, (j + 1) * hd)
            k_cat = jnp.concatenate([kp_ref[:, lanes], kc_ref[:, lanes]], axis=0)
            v_cat = jnp.concatenate([vp_ref[:, lanes], vc_ref[:, lanes]], axis=0)
            dk_cat = jnp.zeros((2 * BLOCK, hd), F32)
            dv_cat = jnp.zeros((2 * BLOCK, hd), F32)
            for hh in range(SWA_GROUP):
                h = j * SWA_GROUP + hh
                q_h = q_ref[:, h * hd:(h + 1) * hd]
                do_h = do_ref[:, h * hd:(h + 1) * hd]
                pn, p_sink = _swa_probs(q_h, k_cat, valid, sink_ref[h])
                dpn = lax.dot_general(do_h, v_cat, (((1,), (1,)), ((), ())), preferred_element_type=F32)
                delta = jnp.sum(dpn * pn, axis=-1, keepdims=True)
                ds = (pn * (dpn - delta) * scale).astype(BF16)
                dsink_h = -jnp.sum(p_sink * delta, axis=0, keepdims=True)
                dsink_step = dsink_step + jnp.where(lane == h, dsink_h, 0.0)
                dq_ref[:, h * hd:(h + 1) * hd] = jnp.dot(ds, k_cat, preferred_element_type=F32)
                dk_cat = dk_cat + lax.dot_general(ds, q_h, (((0,), (0,)), ((), ())),
                                                  preferred_element_type=F32)
                dv_cat = dv_cat + lax.dot_general(pn.astype(BF16), do_h, (((0,), (0,)), ((), ())),
                                                  preferred_element_type=F32)
            dk_ref[rows_prev, lanes] += dk_cat[:BLOCK]
            dk_ref[rows_cur, lanes] += dk_cat[BLOCK:]
            dv_ref[rows_prev, lanes] += dv_cat[:BLOCK]
            dv_ref[rows_cur, lanes] += dv_cat[BLOCK:]
        ds_ref[...] += dsink_step

    return pl.pallas_call(
        body, name=name, grid=(s_len // BLOCK,),
        in_specs=[q_spec, cur, prev, cur, prev, sink_spec, q_spec],
        out_specs=[q_spec, full, full, one],
        out_shape=[jax.ShapeDtypeStruct((s_len, wq), F32), jax.ShapeDtypeStruct((s_len, wkv), F32),
                   jax.ShapeDtypeStruct((s_len, wkv), F32), jax.ShapeDtypeStruct((1, LANES), F32)],
        compiler_params=_params("arbitrary"))(q, k, k, v, v, sinks, do)


def _log_sigmoid(x):
    return jnp.minimum(x, 0.0) - jnp.log(1.0 + jnp.exp(-jnp.abs(x)))


def _tri_ones(lower):
    row = lax.broadcasted_iota(jnp.int32, (BLOCK, BLOCK), 0)
    col = lax.broadcasted_iota(jnp.int32, (BLOCK, BLOCK), 1)
    return jnp.where((col <= row) if lower else (col >= row), 1.0, 0.0).astype(F32)


def _fox_decay(proj, bf_row, fl_block, name):
    s_len = proj.shape[0]
    nchunk = s_len // BLOCK

    def body(fl_ref, bf_ref, dec_ref):
        tri = _tri_ones(True)
        carry = jnp.zeros((1, LANES), F32)
        for c in range(nchunk):
            rows = slice(c * BLOCK, (c + 1) * BLOCK)
            log_f = _log_sigmoid(fl_ref[rows, :] + bf_ref[...])
            loc = jnp.dot(tri, log_f, preferred_element_type=F32, precision=lax.Precision.HIGHEST) + carry
            dec_ref[rows, :] = loc
            carry = loc[BLOCK - 1:BLOCK, :]

    return pl.pallas_call(
        body, name=name, grid=(1,),
        in_specs=[pl.BlockSpec((s_len, LANES), lambda i: (0, fl_block)),
                  pl.BlockSpec((1, LANES), lambda i: (0, 0))],
        out_specs=pl.BlockSpec((s_len, LANES), lambda i: (0, 0)),
        out_shape=jax.ShapeDtypeStruct((s_len, LANES), F32),
        compiler_params=_params("arbitrary"))(proj, bf_row)


def _fox_decay_bwd(ddq, ddk, proj, bf_row, fl_block, heads, name):
    s_len = proj.shape[0]
    nchunk = s_len // BLOCK

    def body(ddq_ref, ddk_ref, fl_ref, bf_ref, dfl_ref, dbf_ref):
        tri = _tri_ones(False)
        lane_ok = lax.broadcasted_iota(jnp.int32, (BLOCK, LANES), 1) < heads
        carry = jnp.zeros((1, LANES), F32)
        dbf = jnp.zeros((1, LANES), F32)
        for c in reversed(range(nchunk)):
            rows = slice(c * BLOCK, (c + 1) * BLOCK)
            ddec = ddq_ref[rows, :] + ddk_ref[rows, :]
            dlog = jnp.dot(tri, ddec, preferred_element_type=F32, precision=lax.Precision.HIGHEST) + carry
            carry = dlog[0:1, :]
            dfl = jnp.where(lane_ok, dlog * _sigmoid(-(fl_ref[rows, :] + bf_ref[...])), 0.0)
            dfl_ref[rows, :] = dfl.astype(BF16)
            dbf = dbf + jnp.sum(dfl, axis=0, keepdims=True)
        dbf_ref[...] = dbf

    blk = pl.BlockSpec((s_len, LANES), lambda i: (0, 0))
    one = pl.BlockSpec((1, LANES), lambda i: (0, 0))
    return pl.pallas_call(
        body, name=name, grid=(1,),
        in_specs=[blk, blk, pl.BlockSpec((s_len, LANES), lambda i: (0, fl_block)), one],
        out_specs=[blk, one],
        out_shape=[jax.ShapeDtypeStruct((s_len, LANES), BF16), jax.ShapeDtypeStruct((1, LANES), F32)],
        compiler_params=_params("arbitrary"))(ddq, ddk, proj, bf_row)


def _fox_scores(q, k, decq, deck, i, bq):
    s_len = k.shape[0]
    s = lax.dot_general(q, k, (((1,), (1,)), ((), ())), preferred_element_type=F32)
    s = s * (FOX_HEAD_DIM ** -0.5) + decq - deck
    row = lax.broadcasted_iota(jnp.int32, (bq, s_len), 0) + i * bq
    col = lax.broadcasted_iota(jnp.int32, (bq, s_len), 1)
    s = jnp.where(col <= row, s, NEG_INF)
    p = jnp.exp(s - jnp.max(s, axis=-1, keepdims=True))
    return p / jnp.sum(p, axis=-1, keepdims=True)


def _fox_key_spans(s_len, bq):
    n_span = min(4, s_len // bq)
    return [(j + 1) * (s_len // n_span) for j in range(n_span)]


def _fox_span_of(i, s_len, bq):
    span = s_len // min(4, s_len // bq)
    return ((i * bq) // span + 1) * span


def _fox_specs(s_len, heads, bq):
    hd = FOX_HEAD_DIM
    q_spec = pl.BlockSpec((bq, hd), lambda h, i: (i, h))
    k_spec = pl.BlockSpec((s_len, hd), lambda h, i: (0, heads + h))
    v_spec = pl.BlockSpec((s_len, hd), lambda h, i: (0, 2 * heads + h))
    dq_spec = pl.BlockSpec((None, bq, 1), lambda h, i: (h, i, 0))
    dk_spec = pl.BlockSpec((None, 1, s_len), lambda h, i: (h, 0, 0))
    return q_spec, k_spec, v_spec, dq_spec, dk_spec


def _fox_fwd(proj, decq, deck, heads, name):
    s_len = proj.shape[0]
    bq = _pick(s_len, (256, 128))
    q_spec, k_spec, v_spec, dq_spec, dk_spec = _fox_specs(s_len, heads, bq)

    def body(q_ref, k_ref, v_ref, decq_ref, deck_ref, o_ref):
        i = pl.program_id(1)
        for klen in _fox_key_spans(s_len, bq):
            @pl.when(_fox_span_of(i, s_len, bq) == klen)
            def _(klen=klen):
                pn = _fox_scores(q_ref[...].astype(BF16), k_ref[:klen, :].astype(BF16), decq_ref[...],
                                 deck_ref[:, :klen], i, bq)
                o_ref[...] = jnp.dot(pn.astype(BF16), v_ref[:klen, :].astype(BF16),
                                     preferred_element_type=F32).astype(BF16)

    return pl.pallas_call(body, name=name, grid=(heads, s_len // bq),
                          in_specs=[q_spec, k_spec, v_spec, dq_spec, dk_spec], out_specs=q_spec,
                          out_shape=jax.ShapeDtypeStruct((s_len, heads * FOX_HEAD_DIM), BF16),
                          compiler_params=_params("parallel", "parallel"))(proj, proj, proj, decq, deck)


def _fox_bwd(proj, decq, deck, do, heads, name):
    s_len = proj.shape[0]
    d = heads * FOX_HEAD_DIM
    bq = _pick(s_len, (256, 128))
    q_spec, k_spec, v_spec, dq_spec, dk_spec = _fox_specs(s_len, heads, bq)
    acc_spec = pl.BlockSpec((s_len, FOX_HEAD_DIM), lambda h, i: (0, h))
    scale = FOX_HEAD_DIM ** -0.5

    def body(q_ref, k_ref, v_ref, decq_ref, deck_ref, do_ref, dq_ref, dk_ref, dv_ref, ddq_ref, ddk_ref):
        i = pl.program_id(1)

        @pl.when(i == 0)
        def _():
            dk_ref[...] = jnp.zeros_like(dk_ref)
            dv_ref[...] = jnp.zeros_like(dv_ref)
            ddk_ref[...] = jnp.zeros_like(ddk_ref)

        q = q_ref[...].astype(BF16)
        do_b = do_ref[...]
        for klen in _fox_key_spans(s_len, bq):
            @pl.when(_fox_span_of(i, s_len, bq) == klen)
            def _(klen=klen):
                k = k_ref[:klen, :].astype(BF16)
                pn = _fox_scores(q, k, decq_ref[...], deck_ref[:, :klen], i, bq)
                dpn = lax.dot_general(do_b, v_ref[:klen, :].astype(BF16), (((1,), (1,)), ((), ())),
                                      preferred_element_type=F32)
                ds = pn * (dpn - jnp.sum(dpn * pn, axis=-1, keepdims=True))
                ddq_ref[...] = jnp.sum(ds, axis=-1, keepdims=True)
                ddk_ref[:, :klen] -= jnp.sum(ds, axis=0, keepdims=True)
                ds_b = (ds * scale).astype(BF16)
                dq_ref[...] = jnp.dot(ds_b, k, preferred_element_type=F32).astype(BF16)
                dk_ref[:klen, :] += lax.dot_general(ds_b, q, (((0,), (0,)), ((), ())),
                                                    preferred_element_type=F32)
                dv_ref[:klen, :] += lax.dot_general(pn.astype(BF16), do_b, (((0,), (0,)), ((), ())),
                                                    preferred_element_type=F32)

    return pl.pallas_call(
        body, name=name, grid=(heads, s_len // bq),
        in_specs=[q_spec, k_spec, v_spec, dq_spec, dk_spec, q_spec],
        out_specs=[q_spec, acc_spec, acc_spec, dq_spec, dk_spec],
        out_shape=[jax.ShapeDtypeStruct((s_len, d), BF16), jax.ShapeDtypeStruct((s_len, d), F32),
                   jax.ShapeDtypeStruct((s_len, d), F32), jax.ShapeDtypeStruct((heads, s_len, 1), F32),
                   jax.ShapeDtypeStruct((heads, 1, s_len), F32)],
        compiler_params=_params("parallel", "arbitrary"))(proj, proj, proj, decq, deck, do)


def _place():
    x, y, c = lax.axis_index("x"), lax.axis_index("y"), lax.axis_index("c")
    chips = [(1 - x, y), (x, 1 - y), (1 - x, 1 - y)]
    return x, y, c, chips


def _remote(src, dst, send_sems, recv_sems, idx, to):
    return pltpu.make_async_remote_copy(src_ref=src, dst_ref=dst, send_sem=send_sems.at[idx],
                                        recv_sem=recv_sems.at[idx], device_id=to, device_id_type=MESH)


def _row_chunks(rows, want):
    for k in (want, want // 2, want // 4):
        if k >= 1 and rows % (16 * k) == 0:
            return [(j * (rows // k), rows // k) for j in range(k)]
    return [(0, rows)]


D2D_CHUNKS = 8


def _cast_into_slot(w, me, name, dep=None):
    rows, width = w.shape
    br = _row_block(rows, width, budget=4 << 20)

    def body(me_ref, w_ref, *rest):
        rest[-1][...] = w_ref[...].astype(BF16)

    in_specs = [pl.BlockSpec((br, width), lambda i, me_ref: (i, 0))]
    if dep is not None:
        in_specs.append(pl.BlockSpec(dep.shape, lambda i, me_ref: (0, 0)))
    return pl.pallas_call(
        body, name=name,
        grid_spec=pltpu.PrefetchScalarGridSpec(
            num_scalar_prefetch=1, grid=(rows // br,), in_specs=in_specs,
            out_specs=pl.BlockSpec((None, br, width), lambda i, me_ref: (me_ref[0], i, 0))),
        out_shape=jax.ShapeDtypeStruct((N_CHIPS, rows, width), BF16),
        compiler_params=_params("parallel"))(me, w, *([dep] if dep is not None else []))


def _hbm(arr):
    return pltpu.with_memory_space_constraint(arr, pltpu.HBM)


def _token_shape():
    return jax.ShapeDtypeStruct((8, LANES), F32)


def _add_pair(grad, got, c, name):
    _, half, width = got.shape
    br = _row_block(half, width, itemsize=2, budget=3 << 20)
    nb = half // br

    def body(c_ref, a_ref, b_ref, o_ref):
        o_ref[...] = (a_ref[...].astype(F32) + b_ref[...].astype(F32)).astype(BF16)

    spec = pl.BlockSpec((None, br, width), lambda j, i, c_ref: (j, i, 0))
    mine = pl.BlockSpec((None, br, width), lambda j, i, c_ref: (j, c_ref[0] * nb + i, 0))
    return pl.pallas_call(
        body, name=name,
        grid_spec=pltpu.PrefetchScalarGridSpec(num_scalar_prefetch=1, grid=(N_CHIPS, nb),
                                               in_specs=[mine, spec], out_specs=spec),
        out_shape=jax.ShapeDtypeStruct(got.shape, BF16),
        compiler_params=_params("parallel", "parallel"))(c, grad, got)


def _adamw_halves(w, pair, others, pair_t, others_t, m, v, place, name):
    rows, width = w.shape
    half = rows // 2
    br = _row_block(half, width, budget=3 << 19)
    nb = half // br
    c1 = 1.0 - ADAM_B1 ** ADAM_STEP
    c2 = 1.0 - ADAM_B2 ** ADAM_STEP

    def chip_sum(p, o3_ref):
        acc = p.astype(F32)
        for r in range(N_CHIPS - 1):
            acc = acc + o3_ref[r].astype(F32)
        return acc

    def body(place_ref, w_ref, p_ref, o_ref, pt_ref, ot_ref, m_ref, v_ref, g_ref, d_ref, nm_ref, nv_ref):
        grad = jnp.where(pl.program_id(0) == place_ref[0], chip_sum(p_ref[...], o_ref), chip_sum(pt_ref[...], ot_ref))
        new_m = ADAM_B1 * m_ref[...] + (1.0 - ADAM_B1) * grad
        new_v = ADAM_B2 * v_ref[...] + (1.0 - ADAM_B2) * (grad * grad)
        g_ref[...] = grad
        d_ref[...] = -ADAM_LR * ((new_m / c1) / (jnp.sqrt(new_v / c2) + ADAM_EPS) + ADAM_WD * w_ref[...])
        nm_ref[...] = new_m
        nv_ref[...] = new_v

    full = pl.BlockSpec((br, width), lambda h, i, s: (h * nb + i, 0))
    mine_i = lambda h, i, s: jnp.where(h == s[0], i, 0)
    theirs_i = lambda h, i, s: jnp.where(h == s[0], 0, i)
    specs = [full,
             pl.BlockSpec((None, br, width), lambda h, i, s: (s[1], mine_i(h, i, s), 0)),
             pl.BlockSpec((N_CHIPS - 1, br, width), lambda h, i, s: (0, mine_i(h, i, s), 0)),
             pl.BlockSpec((br, width), lambda h, i, s: (theirs_i(h, i, s), 0)),
             pl.BlockSpec((N_CHIPS - 1, br, width), lambda h, i, s: (0, theirs_i(h, i, s), 0)),
             full, full]
    shp = jax.ShapeDtypeStruct(w.shape, F32)
    return pl.pallas_call(
        body, name=name,
        grid_spec=pltpu.PrefetchScalarGridSpec(num_scalar_prefetch=1, grid=(2, nb), in_specs=specs,
                                               out_specs=[full] * 4),
        out_shape=[shp] * 4,
        compiler_params=_params("parallel", "parallel"))(place, w, pair, others, pair_t, others_t, m, v)


class _Transfer:
    def __init__(self, n_sems, build):
        self.n_sems, self.build = n_sems, build


def _copies(src_of, dst_of, land_of, rows, chunks, send, recv, idx, to):
    starts = [_remote(src_of(s, z), dst_of(s, z), send, recv, idx, to) for s, z in _row_chunks(rows, chunks)]
    return starts, _remote(src_of(0, rows), land_of(0, rows), send, recv, idx, to)


def _gather_direct(keys, shapes):
    def build(refs, send, recv):
        x, y, c, chips = _place()
        me = 2 * x + y
        out = []
        for t, key in enumerate(keys):
            half = shapes[t][1] // 2
            for r, chip in enumerate(chips[:2]):
                slot = 2 * chip[0] + chip[1]
                out.append(_copies(lambda s, z, key=key, half=half: refs[key].at[me, pl.ds(c * half + s, z)],
                                   lambda s, z, key=key, half=half: refs[key].at[me, pl.ds(c * half + s, z)],
                                   lambda s, z, key=key, half=half, slot=slot: refs[key].at[slot, pl.ds(c * half + s, z)],
                                   half, 1, send, recv, 2 * t + r, (*chip, c)))
        return out
    return _Transfer(2 * len(keys), build)


def _gather_relay(keys, shapes):
    def build(refs, send, recv):
        x, y, c, chips = _place()
        slot_x, slot_y, slot_d = (2 * ch[0] + ch[1] for ch in chips)
        src_slot = slot_y + c * (slot_x - slot_y)
        to = (x ^ (1 - c), y ^ c, c)
        out = []
        for t, key in enumerate(keys):
            half = shapes[t][1] // 2
            out.append(_copies(lambda s, z, key=key, half=half: refs[key].at[src_slot, pl.ds(c * half + s, z)],
                               lambda s, z, key=key, half=half: refs[key].at[src_slot, pl.ds(c * half + s, z)],
                               lambda s, z, key=key, half=half: refs[key].at[slot_d, pl.ds(c * half + s, z)],
                               half, 1, send, recv, t, to))
        return out
    return _Transfer(len(keys), build)


def _gather_pair(keys, shapes):
    def build(refs, send, recv):
        x, y, c, chips = _place()
        out = []
        for t, key in enumerate(keys):
            half = shapes[t][1] // 2
            for r, chip in enumerate(chips):
                slot = 2 * chip[0] + chip[1]
                mine = lambda s, z, key=key, half=half, slot=slot: refs[key].at[slot, pl.ds(c * half + s, z)]
                land = lambda s, z, key=key, half=half, slot=slot: refs[key].at[slot, pl.ds((1 - c) * half + s, z)]
                out.append(_copies(mine, mine, land, half, D2D_CHUNKS, send, recv, 3 * t + r, (x, y, 1 - c)))
        return out
    return _Transfer(3 * len(keys), build)


def _grad_pair(keys, lands, shapes):
    def build(refs, send, recv):
        x, y, c, _ = _place()
        out = []
        for t, (key, land) in enumerate(zip(keys, lands)):
            half = shapes[t][1] // 2
            for j in range(N_CHIPS):
                out.append(_copies(
                    lambda s, z, key=key, half=half, j=j: refs[key].at[j, pl.ds((1 - c) * half + s, z)],
                    lambda s, z, land=land, j=j: refs[land].at[j, pl.ds(s, z)],
                    lambda s, z, land=land, j=j: refs[land].at[j, pl.ds(s, z)],
                    half, 2, send, recv, N_CHIPS * t + j, (x, y, 1 - c)))
        return out
    return _Transfer(N_CHIPS * len(keys), build)


def _grad_chips(keys, lands, shapes):
    def build(refs, send, recv):
        x, y, c, chips = _place()
        out = []
        for t, (key, land) in enumerate(zip(keys, lands)):
            rows = shapes[t][1]
            for r, chip in enumerate(chips):
                slot = 2 * chip[0] + chip[1]
                out.append(_copies(lambda s, z, key=key, slot=slot: refs[key].at[slot, pl.ds(s, z)],
                                   lambda s, z, land=land, r=r: refs[land].at[r, pl.ds(s, z)],
                                   lambda s, z, land=land, r=r: refs[land].at[r, pl.ds(s, z)],
                                   rows, 1, send, recv, 3 * t + r, (*chip, c)))
        return out
    return _Transfer(3 * len(keys), build)


def _grad_join(pairs, others, pair_lands, other_lands, shapes):
    def build(refs, send, recv):
        x, y, c, _ = _place()
        me = 2 * x + y
        sibling = (x, y, 1 - c)
        out = []
        for t, (pair, other, pair_land, other_land) in enumerate(zip(pairs, others, pair_lands, other_lands)):
            rows = shapes[t][1]
            land = lambda s, z, k=pair_land: refs[k].at[pl.ds(s, z)]
            out.append(_copies(lambda s, z, k=pair: refs[k].at[me, pl.ds(s, z)], land, land,
                               rows, D2D_CHUNKS, send, recv, N_CHIPS * t, sibling))
            for r in range(N_CHIPS - 1):
                land = lambda s, z, k=other_land, r=r: refs[k].at[r, pl.ds(s, z)]
                out.append(_copies(lambda s, z, k=other, r=r: refs[k].at[r, pl.ds(s, z)], land, land,
                                   rows, D2D_CHUNKS // 2, send, recv, N_CHIPS * t + 1 + r, sibling))
        return out
    return _Transfer(N_CHIPS * len(pairs), build)


def _comm_call(name, arrays, waits, starts, after):
    keys = list(arrays)
    n, nw, ns = len(keys), len(waits), len(starts)

    def body(*refs):
        in_sems = refs[n:n + 2 * nw]
        base = n + 2 * nw + 1
        out_sems = refs[base:base + 2 * ns]
        bufs = dict(zip(keys, refs[base + 2 * ns:base + 2 * ns + n]))
        token = refs[base + 2 * ns + n]
        for k, (transfer, _, _) in enumerate(waits):
            for _, whole in transfer.build(bufs, in_sems[2 * k], in_sems[2 * k + 1]):
                whole.wait_send()
                whole.wait_recv()
        for k, transfer in enumerate(starts):
            for chunks, _ in transfer.build(bufs, out_sems[2 * k], out_sems[2 * k + 1]):
                for cp in chunks:
                    cp.start()
        token[...] = jnp.zeros_like(token)

    sem_shapes = []
    for transfer in starts:
        sem_shapes += [pltpu.SemaphoreType.DMA((transfer.n_sems,))] * 2
    operands = [_hbm(arrays[k]) for k in keys]
    for _, send, recv in waits:
        operands += [send, recv]
    res = pl.pallas_call(
        body, name=name, in_specs=[HBM] * n + [SEM] * (2 * nw) + [pl.BlockSpec(memory_space=pl.ANY)],
        out_specs=[SEM] * (2 * ns) + [HBM] * n + [pl.BlockSpec(memory_space=pltpu.VMEM)],
        out_shape=sem_shapes + [pltpu.HBM(arrays[k].shape, arrays[k].dtype) for k in keys] + [_token_shape()],
        input_output_aliases={t: 2 * ns + t for t in range(n)},
        compiler_params=pltpu.CompilerParams(has_side_effects=EFFECT),
    )(*operands, after)
    sems = [(res[2 * k], res[2 * k + 1]) for k in range(ns)]
    return dict(zip(keys, res[2 * ns:2 * ns + n])), sems, res[2 * ns + n]


def _device_gather(part_key, all_key, rows):
    def build(refs, send, recv):
        x, y, c, _ = _place()
        me = 4 * x + 2 * y + c
        out = []
        for r in range(1, N_DEV):
            peer = (x ^ (r >> 2), y ^ ((r >> 1) & 1), c ^ (r & 1))
            theirs = 4 * peer[0] + 2 * peer[1] + peer[2]
            out.append(_copies(lambda s, z: refs[part_key].at[pl.ds(s, z)],
                               lambda s, z: refs[all_key].at[me, pl.ds(s, z)],
                               lambda s, z, theirs=theirs: refs[all_key].at[theirs, pl.ds(s, z)],
                               rows, 1, send, recv, r - 1, peer))
        return out
    return _Transfer(N_DEV - 1, build)


def _sum_devices(parts, name):
    _, rows, width = parts.shape
    br = _row_block(rows, width, budget=1 << 19)

    def body(p_ref, o_ref):
        acc = p_ref[0]
        for dev in range(1, N_DEV):
            acc = acc + p_ref[dev]
        o_ref[...] = acc

    return pl.pallas_call(body, name=name, grid=(rows // br,),
                          in_specs=[pl.BlockSpec((N_DEV, br, width), lambda i: (0, i, 0))],
                          out_specs=pl.BlockSpec((br, width), lambda i: (i, 0)),
                          out_shape=jax.ShapeDtypeStruct((rows, width), F32),
                          compiler_params=_params("parallel"))(parts)


INPUT_NAMES = None


def _weight_names():
    names = []
    for i, kind in enumerate(("gmlp", "swa", "fox", "gmlp")):
        p = f"l{i}_"
        names += [p + "ffn1_norm", p + "ffn1_wi", p + "ffn1_wo", p + "mix_norm", p + "mix_win"]
        if kind == "gmlp":
            names += [p + "gmlp_vnorm", p + "gmlp_ws", p + "gmlp_bs"]
        elif kind == "swa":
            names += [p + "swa_sinks"]
        else:
            names += [p + "fox_bf"]
        names += [p + "mix_wout", p + "ffn2_norm", p + "ffn2_wi", p + "ffn2_wo"]
    return names + ["final_norm"]


WEIGHTS = _weight_names()
MIXERS = ("gmlp", "swa", "fox", "gmlp")
BIG = ("ffn1_wi", "ffn1_wo", "mix_win", "mix_wout", "ffn2_wi", "ffn2_wo")


def _ffn_fwd(h, gain, wi, wo, tag, dep=None):
    n = _rms_fwd(h, gain, tag + "_norm", dep=dep)
    z = _matmul(n, wi, name=tag + "_up", out_dtype=BF16)
    a = _swiglu_fwd(z, tag + "_act")
    f, d = wo.shape[0] * wo.shape[1], wo.shape[2]
    out = _matmul(a, wo.reshape(f, d), name=tag + "_down", out_dtype=F32, scale=0.5, resid=h)
    return out, (h, n, z, a)


def _ffn_bwd(dout, saved, gain, wi, wo, tag, dep=None, grads_ready=None):
    h, n, z, a = saved
    f, d = wo.shape[0] * wo.shape[1], wo.shape[2]
    da = _matmul(dout, wo.reshape(f, d), tb=True, name=tag + "_bdown", out_dtype=BF16, scale=0.5, dep=dep)
    dwo = _matmul(a, dout, ta=True, name=tag + "_gdown", out_dtype=BF16, scale=0.5, dep=dep).reshape(wo.shape)
    dz = _swiglu_bwd(z, da, tag + "_bact")
    dwi = _matmul(n, dz, ta=True, name=tag + "_gup", out_dtype=BF16, out_shards=N_CHIPS)
    dn = _matmul(dz, wi, tb=True, name=tag + "_bup", out_dtype=F32,
                 dep=grads_ready(dwi, dwo) if grads_ready is not None else None)
    dh, dgain = _norm_bwd(h, gain, dn, dout, tag + "_bnorm")
    return dh, dgain, dwi, dwo


def _natural(w_sharded, pad_to):
    ns, rows, csh = w_sharded.shape
    nat = jnp.transpose(w_sharded, (1, 0, 2)).reshape(rows, ns * csh)
    extra = (-nat.shape[1]) % pad_to
    return jnp.pad(nat, ((0, 0), (0, extra))) if extra else nat


def _mixer_fwd(kind, h, p, tag, dep=None):
    s_len, d = h.shape
    n = _rms_fwd(h, p["mix_norm"], tag + "_norm", dep=dep)
    wout = p["mix_wout"].reshape(d, d)
    if kind == "gmlp":
        zp = _matmul(n, p["mix_win"], name=tag + "_in", out_dtype=BF16)
        y = _gmlp_fwd(zp, p["gmlp_vnorm"], p["gmlp_ws"], p["gmlp_bs"], tag + "_gate")
        saved = (h, n, zp, y)
    elif kind == "swa":
        qkv = _matmul(n, p["mix_win"], name=tag + "_in", out_dtype=F32)
        q, k, v = _rope_fwd(qkv, tag + "_rope")
        y = _swa_fwd(q, k, v, p["swa_sinks"], tag + "_attn")
        saved = (h, n, q, k, v, y)
    else:
        heads = d // FOX_HEAD_DIM
        win = _natural(p["mix_win"], LANES)
        proj = _matmul(n, win, name=tag + "_in", out_dtype=F32)
        bf_row = jnp.pad(p["fox_bf"], (0, LANES - heads)).reshape(1, LANES)
        dec = _fox_decay(proj, bf_row, 3 * heads, tag + "_decay")
        dec_t = dec[:, :heads].T
        decq, deck = dec_t.reshape(heads, s_len, 1), dec_t.reshape(heads, 1, s_len)
        y = _fox_fwd(proj, decq, deck, heads, tag + "_attn")
        saved = (h, n, win, proj, bf_row, decq, deck, y)
    out = _matmul(y, wout, name=tag + "_out", out_dtype=F32, resid=h)
    return out, saved


def _mixer_bwd(kind, dout, saved, p, tag, dep=None):
    h, n = saved[0], saved[1]
    y = saved[-1]
    s_len, d = h.shape
    wout = p["mix_wout"].reshape(d, d)
    grads = {}
    dy = _matmul(dout, wout, tb=True, name=tag + "_bout", out_dtype=BF16, dep=dep)
    grads["mix_wout"] = _matmul(y, dout, ta=True, name=tag + "_gout", out_dtype=BF16,
                                dep=dep).reshape(p["mix_wout"].shape)
    if kind == "gmlp":
        zp = saved[2]
        dzp, dws, dbst, dvg = _gmlp_bwd(zp, dy, p["gmlp_vnorm"], p["gmlp_ws"], p["gmlp_bs"], tag + "_bgate")
        grads.update(gmlp_ws=dws, gmlp_bs=dbst.T, gmlp_vnorm=dvg.reshape(d))
        dn = _matmul(dzp, p["mix_win"], tb=True, name=tag + "_bin", out_dtype=F32)
        grads["mix_win"] = _matmul(n, dzp, ta=True, name=tag + "_gin", out_dtype=BF16, out_shards=N_CHIPS)
    elif kind == "swa":
        q, k, v = saved[2:5]
        dq, dk, dv, dsinks = _swa_bwd(q, k, v, p["swa_sinks"], dy, tag + "_battn")
        grads["swa_sinks"] = dsinks[0, :p["swa_sinks"].shape[0]]
        dqkv = _rope_bwd(dq, dk, dv, tag + "_brope")
        dn = _matmul(dqkv, p["mix_win"], tb=True, name=tag + "_bin", out_dtype=F32)
        grads["mix_win"] = _matmul(n, dqkv, ta=True, name=tag + "_gin", out_dtype=BF16, out_shards=N_CHIPS)
    else:
        win, proj, bf_row, decq, deck = saved[2:7]
        heads = d // FOX_HEAD_DIM
        dq, dk, dv, ddq, ddk = _fox_bwd(proj, decq, deck, dy, heads, tag + "_battn")
        widen = lambda t: jnp.pad(t.reshape(heads, s_len).T, ((0, 0), (0, LANES - heads)))
        dfl, dbf = _fox_decay_bwd(widen(ddq), widen(ddk), proj, bf_row, 3 * heads, heads, tag + "_bdecay")
        grads["fox_bf"] = dbf[0, :heads]
        dproj = jnp.concatenate([dq, dk.astype(BF16), dv.astype(BF16), dfl], axis=1)
        dn = _matmul(dproj, win, tb=True, name=tag + "_bin", out_dtype=F32)
        dwin = _matmul(n, dproj, ta=True, name=tag + "_gin", out_dtype=BF16)
        ns, rows, csh = p["mix_win"].shape
        grads["mix_win"] = jnp.transpose(dwin[:, :ns * csh].reshape(rows, ns, csh), (1, 0, 2))
    dh, dgain = _norm_bwd(h, p["mix_norm"], dn, dout, tag + "_bnorm")
    grads["mix_norm"] = dgain.reshape(d)
    return dh, grads


def _pack_small(arrays):
    flat = jnp.concatenate([a.reshape(-1).astype(F32) for a in arrays])
    pad = (-flat.shape[0]) % (512 * LANES)
    return jnp.pad(flat, (0, pad)).reshape(-1, LANES)


def _unpack_small(packed, like):
    flat, out, pos = packed.reshape(-1), [], 0
    for a in like:
        out.append(flat[pos:pos + a.size].reshape(a.shape))
        pos += a.size
    return out


def _step(inp):
    x, target = inp["x"][0], inp["loss_target"][0]
    d = x.shape[1]

    core = lax.axis_index("c").astype(jnp.int32).reshape(1)
    chip = (2 * lax.axis_index("x") + lax.axis_index("y")).astype(jnp.int32).reshape(1)
    place = jnp.concatenate([core, chip])

    groups = []
    for i in range(len(MIXERS)):
        groups += [(i, "ffn1", [f"l{i}_ffn1_wi", f"l{i}_ffn1_wo"]), (i, "mix", [f"l{i}_mix_win", f"l{i}_mix_wout"]),
                   (i, "ffn2", [f"l{i}_ffn2_wi", f"l{i}_ffn2_wo"])]

    def layer_params(i, full):
        p = {nm[len(f"l{i}_"):]: inp[nm] for nm in WEIGHTS if nm.startswith(f"l{i}_")}
        p.update({nm[len(f"l{i}_"):]: w for nm, w in full.items()})
        return p

    n_groups = len(groups)
    valid = lambda k: 0 <= k < n_groups

    bufs = {}
    full_shapes = lambda names: [(N_CHIPS, *inp[nm].shape) for nm in names]
    direct = [_gather_direct(names, full_shapes(names)) for _, _, names in groups]
    relay = [_gather_relay(names, full_shapes(names)) for _, _, names in groups]
    to_pair = [_gather_pair(names, full_shapes(names)) for _, _, names in groups]
    sems = {}

    def cast_groups(which, dep):
        for g in which:
            for nm in groups[g][2]:
                bufs[nm] = _cast_into_slot(inp[nm], chip, nm + "_cast", dep=dep)

    def gather_step(step, after):
        waits, starts, tags, keys = [], [], [], []
        for kind, transfers, g, begin in (("pair", to_pair, step, False), ("relay", relay, step + 1, False),
                                          ("pair", to_pair, step + 1, True), ("direct", direct, step + 2, False),
                                          ("relay", relay, step + 2, True), ("direct", direct, step + 3, True)):
            if not valid(g):
                continue
            keys += [nm for nm in groups[g][2] if nm not in keys]
            if begin:
                starts.append(transfers[g])
                tags.append((kind, g))
            else:
                waits.append((transfers[g], *sems.pop((kind, g))))
        new, started, token = _comm_call(f"gather_step{step + 3}", {k: bufs[k] for k in keys}, waits, starts, after)
        bufs.update(new)
        sems.update(zip(tags, started))
        return token

    cast_groups([0], None)
    token = gather_step(-3, x)
    cast_groups(range(1, 6), token)
    token = gather_step(-2, bufs[groups[5][2][-1]])
    cast_groups(range(6, n_groups), token)
    token = gather_step(-1, bufs[groups[-1][2][-1]])
    h, saved, fulls = x, [], []
    for g, (i, part, names) in enumerate(groups):
        token = gather_step(g, h)
        full = {nm: bufs[nm] for nm in names}
        p = layer_params(i, full)
        if part == "mix":
            h, s = _mixer_fwd(MIXERS[i], h, p, f"l{i}_mix", dep=token)
        else:
            h, s = _ffn_fwd(h, p[part + "_norm"], p[part + "_wi"], p[part + "_wo"], f"l{i}_{part}", dep=token)
        saved.append(s)
        fulls.append(full)
    loss_part, dh, dfinal = _loss_head(h, inp["final_norm"], target, "loss_head")
    loss = lax.psum(loss_part, ("x", "y", "c"))

    small_grads = {"final_norm": dfinal.reshape(d)}
    outs = {}
    work = {}
    stage = {}
    small_names = [nm for nm in WEIGHTS if nm.split("_", 1)[1] not in BIG]
    last_small = "l0_ffn1_norm"
    small_sets = {}

    def small_start(tag, names, after):
        part = _pack_small([small_grads[nm] for nm in names])
        device = 4 * lax.axis_index("x") + 2 * lax.axis_index("y") + lax.axis_index("c")
        work[tag + "#part"] = part
        work[tag + "#all"] = lax.dynamic_update_slice(jnp.zeros((N_DEV, *part.shape), F32), part[None],
                                                      (device, 0, 0))
        stage[(tag, 0)] = _device_gather(tag + "#part", tag + "#all", part.shape[0])
        stage_keys[(tag, 0)] = [tag + "#part", tag + "#all"]
        small_sets[tag] = names
        return comm(f"small_{tag}_start", [], [(tag, 0)], after)

    def small_finish(tag, after):
        names = small_sets[tag]
        comm(f"small_{tag}_wait", [(tag, 0)], [], after)
        total = _sum_devices(work[tag + "#all"], f"small_{tag}_sum")
        like = [inp[nm] for nm in names]
        upd = _adamw(_pack_small(like), total, _pack_small([inp["m_" + nm] for nm in names]),
                     _pack_small([inp["v_" + nm] for nm in names]), f"small_{tag}_adamw")
        unpacked = [_unpack_small(t, like) for t in (total, *upd)]
        for k, nm in enumerate(names):
            outs[nm] = tuple(u[k] for u in unpacked)
        return upd[0]

    def comm(name, transfers_to_wait, transfers_to_start, after):
        waits = [(stage[k], *sems.pop(k)) for k in transfers_to_wait if valid(k[1])]
        starts = [k for k in transfers_to_start if valid(k[1])]
        if not waits and not starts:
            return after
        keys = []
        for k in [k for k in transfers_to_wait if valid(k[1])] + starts:
            keys += [key for key in stage_keys[k] if key not in keys]
        new, started, token = _comm_call(name, {k: work[k] for k in keys}, waits, [stage[k] for k in starts], after)
        work.update(new)
        sems.update(zip(starts, started))
        return token

    stage_keys = {}

    def reduce_step(g, after):
        token = comm(f"rs_pair_step{n_groups - 1 - g}", [("pair", g + 1)], [("pair", g)], after)
        if valid(g + 1):
            names = groups[g + 1][2]
            for nm in names:
                work[nm + "#sum"] = _add_pair(work[nm + "#grad"], work[nm + "#got"], core, nm + "_rs_add")
                work[nm + "#others"] = lax.empty((N_CHIPS - 1, *work[nm + "#sum"].shape[1:]), BF16)
            shapes = [work[nm + "#sum"].shape for nm in names]
            stage[("chips", g + 1)] = _grad_chips([nm + "#sum" for nm in names], [nm + "#others" for nm in names], shapes)
            stage_keys[("chips", g + 1)] = [nm + sfx for nm in names for sfx in ("#sum", "#others")]
        token = comm(f"rs_chips_step{n_groups - 1 - g}", [("chips", g + 2)], [("chips", g + 1)], token)
        if valid(g + 2):
            names = groups[g + 2][2]
            for nm in names:
                work[nm + "#sum_t"] = lax.empty(work[nm + "#sum"].shape[1:], BF16)
                work[nm + "#others_t"] = lax.empty(work[nm + "#others"].shape, BF16)
            sfxs = ("#sum", "#others", "#sum_t", "#others_t")
            stage[("join", g + 2)] = _grad_join(*[[nm + sfx for nm in names] for sfx in sfxs],
                                                [work[nm + "#sum"].shape for nm in names])
            stage_keys[("join", g + 2)] = [nm + sfx for nm in names for sfx in sfxs]
        token = comm(f"rs_join_step{n_groups - 1 - g}", [("join", g + 3)], [("join", g + 2)], token)
        if valid(g + 3):
            for nm in groups[g + 3][2]:
                outs[nm] = tuple(_adamw_halves(inp[nm], work[nm + "#sum"], work[nm + "#others"], work[nm + "#sum_t"],
                                               work[nm + "#others_t"], inp["m_" + nm], inp["v_" + nm], place,
                                               nm + "_adamw"))
        return token

    def reduce_from(g, names, grads, after):
        for nm, gr in zip(names, grads):
            work[nm + "#grad"] = gr
            work[nm + "#got"] = lax.empty((gr.shape[0], gr.shape[1] // 2, gr.shape[2]), BF16)
        stage[("pair", g)] = _grad_pair([nm + "#grad" for nm in names], [nm + "#got" for nm in names],
                                        [gr.shape for gr in grads])
        stage_keys[("pair", g)] = [nm + sfx for nm in names for sfx in ("#grad", "#got")]
        return reduce_step(g, after)

    dep = None
    for g in reversed(range(n_groups)):
        i, part, names = groups[g]
        p = layer_params(i, fulls[g])
        if part == "mix":
            dh, mg = _mixer_bwd(MIXERS[i], dh, saved[g], p, f"l{i}_mix", dep=dep)
            grads = [mg.pop("mix_win"), mg.pop("mix_wout")]
            small_grads.update({f"l{i}_{key}": val for key, val in mg.items()})
            dep = reduce_from(g, names, grads, dh)
        elif g > 0:
            dh, g_norm, dwi, dwo = _ffn_bwd(dh, saved[g], p[part + "_norm"], p[part + "_wi"], p[part + "_wo"],
                                            f"l{i}_{part}", dep=dep)
            small_grads[f"l{i}_{part}_norm"] = g_norm.reshape(d)
            dep = reduce_from(g, names, [dwi, dwo], dh)
        else:
            early = lambda dwi, dwo, names=names, dep=dep: reduce_step(-1, reduce_from(0, names, [dwi, dwo], dep))
            dh, g_norm, _, _ = _ffn_bwd(dh, saved[g], p[part + "_norm"], p[part + "_wi"], p[part + "_wo"],
                                        f"l{i}_{part}", dep=dep, grads_ready=early)
            small_grads[f"l{i}_{part}_norm"] = g_norm.reshape(d)
        if g == 1:
            dep = small_start("early", [nm for nm in small_names if nm != last_small], dep)
    dep = small_start("late", [last_small], dh)
    dep = small_finish("early", dep)
    dep = reduce_step(-2, dep)
    dep = small_finish("late", dep)
    reduce_step(-3, dep)

    result = [loss, dh[None]]
    for part in range(4):
        result += [outs[nm][part] for nm in WEIGHTS]
    return tuple(result)


def kernel(x, l0_ffn1_norm, l0_ffn1_wi, l0_ffn1_wo, l0_mix_norm, l0_mix_win, l0_gmlp_vnorm, l0_gmlp_ws, l0_gmlp_bs, l0_mix_wout, l0_ffn2_norm, l0_ffn2_wi, l0_ffn2_wo, l1_ffn1_norm, l1_ffn1_wi, l1_ffn1_wo, l1_mix_norm, l1_mix_win, l1_swa_sinks, l1_mix_wout, l1_ffn2_norm, l1_ffn2_wi, l1_ffn2_wo, l2_ffn1_norm, l2_ffn1_wi, l2_ffn1_wo, l2_mix_norm, l2_mix_win, l2_fox_bf, l2_mix_wout, l2_ffn2_norm, l2_ffn2_wi, l2_ffn2_wo, l3_ffn1_norm, l3_ffn1_wi, l3_ffn1_wo, l3_mix_norm, l3_mix_win, l3_gmlp_vnorm, l3_gmlp_ws, l3_gmlp_bs, l3_mix_wout, l3_ffn2_norm, l3_ffn2_wi, l3_ffn2_wo, final_norm, loss_target, m_l0_ffn1_norm, m_l0_ffn1_wi, m_l0_ffn1_wo, m_l0_mix_norm, m_l0_mix_win, m_l0_gmlp_vnorm, m_l0_gmlp_ws, m_l0_gmlp_bs, m_l0_mix_wout, m_l0_ffn2_norm, m_l0_ffn2_wi, m_l0_ffn2_wo, m_l1_ffn1_norm, m_l1_ffn1_wi, m_l1_ffn1_wo, m_l1_mix_norm, m_l1_mix_win, m_l1_swa_sinks, m_l1_mix_wout, m_l1_ffn2_norm, m_l1_ffn2_wi, m_l1_ffn2_wo, m_l2_ffn1_norm, m_l2_ffn1_wi, m_l2_ffn1_wo, m_l2_mix_norm, m_l2_mix_win, m_l2_fox_bf, m_l2_mix_wout, m_l2_ffn2_norm, m_l2_ffn2_wi, m_l2_ffn2_wo, m_l3_ffn1_norm, m_l3_ffn1_wi, m_l3_ffn1_wo, m_l3_mix_norm, m_l3_mix_win, m_l3_gmlp_vnorm, m_l3_gmlp_ws, m_l3_gmlp_bs, m_l3_mix_wout, m_l3_ffn2_norm, m_l3_ffn2_wi, m_l3_ffn2_wo, m_final_norm, v_l0_ffn1_norm, v_l0_ffn1_wi, v_l0_ffn1_wo, v_l0_mix_norm, v_l0_mix_win, v_l0_gmlp_vnorm, v_l0_gmlp_ws, v_l0_gmlp_bs, v_l0_mix_wout, v_l0_ffn2_norm, v_l0_ffn2_wi, v_l0_ffn2_wo, v_l1_ffn1_norm, v_l1_ffn1_wi, v_l1_ffn1_wo, v_l1_mix_norm, v_l1_mix_win, v_l1_swa_sinks, v_l1_mix_wout, v_l1_ffn2_norm, v_l1_ffn2_wi, v_l1_ffn2_wo, v_l2_ffn1_norm, v_l2_ffn1_wi, v_l2_ffn1_wo, v_l2_mix_norm, v_l2_mix_win, v_l2_fox_bf, v_l2_mix_wout, v_l2_ffn2_norm, v_l2_ffn2_wi, v_l2_ffn2_wo, v_l3_ffn1_norm, v_l3_ffn1_wi, v_l3_ffn1_wo, v_l3_mix_norm, v_l3_mix_win, v_l3_gmlp_vnorm, v_l3_gmlp_ws, v_l3_gmlp_bs, v_l3_mix_wout, v_l3_ffn2_norm, v_l3_ffn2_wi, v_l3_ffn2_wo, v_final_norm):
    return _step(dict(locals()))
```

```python
import functools
import math

import jax
import jax.numpy as jnp
from jax import lax
from jax.experimental import pallas as pl
from jax.experimental.pallas import tpu as pltpu

F32 = jnp.float32
BF16 = jnp.bfloat16

NORM_EPS = 1e-5
NEG_INF = -1e30
BLOCK = 128
GMLP_GROUPS = 16
SWA_HEAD_DIM = 64
SWA_GROUP = 8
ROPE_DIM = SWA_HEAD_DIM // 4
ROPE_THETA = 500000.0
FOX_HEAD_DIM = 128
ADAM_LR = 0.001
ADAM_B1 = 0.9
ADAM_B2 = 0.999
ADAM_EPS = 1e-08
ADAM_WD = 0.01
ADAM_STEP = 10
N_CHIPS = 4
N_DEV = 8
LANES = 128
VMEM_LIMIT = 56 * 1024 * 1024
MESH = pl.DeviceIdType.MESH
HBM = pl.BlockSpec(memory_space=pltpu.HBM)
SEM = pl.BlockSpec(memory_space=pltpu.SEMAPHORE)
EFFECT = pltpu.SideEffectType.DATAFLOW_SIDE_EFFECTING

MM_TILES = (1024, 1408, 896, 640, 512, 384, 256, 128)
K_TILES = (2816,) + MM_TILES


def _pick(n, prefs):
    for p in prefs:
        if p <= n and n % p == 0:
            return p
    return n


def _params(*sem):
    return pltpu.CompilerParams(dimension_semantics=sem or None, vmem_limit_bytes=VMEM_LIMIT)


def _cols(arr):
    return arr.shape[-1] * (arr.shape[0] if arr.ndim == 3 else 1)


def _mat_spec(arr, rb, cb, ridx, cidx):
    if arr.ndim == 2:
        return pl.BlockSpec((rb, cb), lambda j, i, k: (ridx(j, i, k), cidx(j, i, k)))
    per = arr.shape[2] // cb
    return pl.BlockSpec((None, rb, cb),
                        lambda j, i, k: (cidx(j, i, k) // per, ridx(j, i, k), cidx(j, i, k) % per))


def _matmul(a, b, *, name, out_dtype, ta=False, tb=False, out_shards=1, scale=1.0, resid=None, dep=None):
    m_dim, k_dim = (a.shape[1], a.shape[0]) if ta else a.shape
    n_dim = b.shape[-2] if tb else _cols(b)
    assert k_dim == (_cols(b) if tb else b.shape[-2]), (a.shape, b.shape, ta, tb)
    n_unit = n_dim // out_shards
    if b.ndim == 3 and not tb:
        n_unit = math.gcd(n_unit, b.shape[2])
    k_unit = b.shape[2] if (b.ndim == 3 and tb) else k_dim
    bm = _pick(m_dim, MM_TILES)
    bn = _pick(n_unit, MM_TILES)
    bk = k_unit if k_unit <= 2048 else _pick(k_unit, K_TILES)
    nk = k_dim // bk
    i_of, j_of, k_of = (lambda j, i, k: i), (lambda j, i, k: j), (lambda j, i, k: k)
    a_spec = _mat_spec(a, bk, bm, k_of, i_of) if ta else _mat_spec(a, bm, bk, i_of, k_of)
    b_spec = _mat_spec(b, bn, bk, j_of, k_of) if tb else _mat_spec(b, bk, bn, k_of, j_of)
    out_shape = (m_dim, n_dim) if out_shards == 1 else (out_shards, m_dim, n_dim // out_shards)
    out = jax.ShapeDtypeStruct(out_shape, out_dtype)
    o_spec = _mat_spec(out, bm, bn, i_of, j_of)
    dims = (((0 if ta else 1,), (1 if tb else 0,)), ((), ()))
    operands, in_specs = [a, b], [a_spec, b_spec]
    if resid is not None:
        operands.append(resid)
        in_specs.append(_mat_spec(resid, bm, bn, i_of, j_of))
    if dep is not None:
        operands.append(dep)
        in_specs.append(pl.BlockSpec(dep.shape, lambda j, i, k: (0, 0)))
    n_in = len(operands)

    def body(*refs):
        a_ref, b_ref = refs[0], refs[1]
        r_ref = refs[2] if resid is not None else None
        o_ref = refs[n_in]
        part = lax.dot_general(a_ref[...].astype(BF16), b_ref[...].astype(BF16), dims,
                               preferred_element_type=F32)

        def finish(acc):
            val = acc * scale if scale != 1.0 else acc
            if r_ref is not None:
                val = r_ref[...] + val
            o_ref[...] = val.astype(o_ref.dtype)

        if nk == 1:
            finish(part)
        else:
            acc_ref = refs[-1]
            k = pl.program_id(2)

            @pl.when(k == 0)
            def _():
                acc_ref[...] = part

            @pl.when(k > 0)
            def _():
                acc_ref[...] += part

            @pl.when(k == nk - 1)
            def _():
                finish(acc_ref[...])

    return pl.pallas_call(
        body, name=name, grid=(n_dim // bn, m_dim // bm, nk),
        in_specs=in_specs, out_specs=o_spec, out_shape=out,
        scratch_shapes=[pltpu.VMEM((bm, bn), F32)] if nk > 1 else [],
        compiler_params=_params("parallel", "parallel", "arbitrary"),
    )(*operands)


def _row_block(rows, width, itemsize=4, budget=2 << 20):
    best = None
    for br in range(16, rows + 1, 16):
        if rows % br == 0 and br * width * itemsize <= budget:
            best = br
    return best or rows


def _rms_fwd(h, g, name, dep=None):
    s_len, d = h.shape
    br = _row_block(s_len, d)

    def body(h_ref, g_ref, *rest):
        o_ref = rest[-1]
        x = h_ref[...]
        r = lax.rsqrt(jnp.mean(x * x, axis=-1, keepdims=True) + NORM_EPS)
        o_ref[...] = (x * r * g_ref[...]).astype(BF16)

    spec = pl.BlockSpec((br, d), lambda i: (i, 0))
    operands = [h, g.reshape(1, d)] + ([dep] if dep is not None else [])
    in_specs = [spec, pl.BlockSpec((1, d), lambda i: (0, 0))]
    if dep is not None:
        in_specs.append(pl.BlockSpec(dep.shape, lambda i: (0, 0)))
    return pl.pallas_call(body, name=name, grid=(s_len // br,), in_specs=in_specs, out_specs=spec,
                          out_shape=jax.ShapeDtypeStruct((s_len, d), BF16),
                          compiler_params=_params("parallel"))(*operands)


def _rms_bwd_rows(x, g, dn):
    r = lax.rsqrt(jnp.mean(x * x, axis=-1, keepdims=True) + NORM_EPS)
    xhat = x * r
    gdn = dn * g
    dx = r * (gdn - xhat * jnp.mean(gdn * xhat, axis=-1, keepdims=True))
    return dx, dn * xhat


def _norm_bwd(h, g, dn, dres, name):
    s_len, d = h.shape
    br = _row_block(s_len, d, budget=1 << 20)

    def body(h_ref, g_ref, dn_ref, dres_ref, dh_ref, dg_ref):
        dx, dg_rows = _rms_bwd_rows(h_ref[...], g_ref[...], dn_ref[...].astype(F32))
        dh_ref[...] = dres_ref[...] + dx

        @pl.when(pl.program_id(0) == 0)
        def _():
            dg_ref[...] = jnp.zeros_like(dg_ref)

        dg_ref[...] += jnp.sum(dg_rows, axis=0, keepdims=True)

    spec = pl.BlockSpec((br, d), lambda i: (i, 0))
    vec = pl.BlockSpec((1, d), lambda i: (0, 0))
    return pl.pallas_call(body, name=name, grid=(s_len // br,),
                          in_specs=[spec, vec, spec, spec], out_specs=[spec, vec],
                          out_shape=[jax.ShapeDtypeStruct((s_len, d), F32),
                                     jax.ShapeDtypeStruct((1, d), F32)],
                          compiler_params=_params("arbitrary"))(h, g.reshape(1, d), dn, dres)


def _sigmoid(x):
    return 0.5 * (1.0 + jnp.tanh(0.5 * x))


def _swiglu_fwd(z, name):
    s_len, f2 = z.shape
    f = f2 // 2
    br = _row_block(s_len, f2, itemsize=2, budget=6 << 20)
    fc = _pick(f, MM_TILES)

    def body(z_ref, a_ref):
        for c0 in range(0, f, fc):
            gate = z_ref[:, c0:c0 + fc].astype(F32)
            up = z_ref[:, f + c0:f + c0 + fc].astype(F32)
            a_ref[:, c0:c0 + fc] = (gate * _sigmoid(gate) * up).astype(BF16)

    return pl.pallas_call(body, name=name, grid=(s_len // br,),
                          in_specs=[pl.BlockSpec((br, f2), lambda i: (i, 0))],
                          out_specs=pl.BlockSpec((br, f), lambda i: (i, 0)),
                          out_shape=jax.ShapeDtypeStruct((s_len, f), BF16),
                          compiler_params=_params("parallel"))(z)


def _swiglu_bwd(z, da, name):
    s_len, f2 = z.shape
    f = f2 // 2
    br = _row_block(s_len, f2, itemsize=2, budget=6 << 20)
    fc = _pick(f, MM_TILES)

    def body(z_ref, da_ref, dz_ref):
        for c0 in range(0, f, fc):
            gate = z_ref[:, c0:c0 + fc].astype(F32)
            up = z_ref[:, f + c0:f + c0 + fc].astype(F32)
            d = da_ref[:, c0:c0 + fc].astype(F32)
            sig = _sigmoid(gate)
            dz_ref[:, c0:c0 + fc] = (d * up * (sig * (1.0 + gate * (1.0 - sig)))).astype(BF16)
            dz_ref[:, f + c0:f + c0 + fc] = (d * gate * sig).astype(BF16)

    return pl.pallas_call(body, name=name, grid=(s_len // br,),
                          in_specs=[pl.BlockSpec((br, f2), lambda i: (i, 0)),
                                    pl.BlockSpec((br, f), lambda i: (i, 0))],
                          out_specs=pl.BlockSpec((br, f2), lambda i: (i, 0)),
                          out_shape=jax.ShapeDtypeStruct((s_len, f2), BF16),
                          compiler_params=_params("parallel"))(z, da)


def _loss_head(h, g, target, name):
    s_len, d = h.shape
    br = _row_block(s_len, d, budget=1 << 20)

    def body(h_ref, g_ref, t_ref, loss_ref, dh_ref, dg_ref):
        x = h_ref[...]
        gain = g_ref[...]
        r = lax.rsqrt(jnp.mean(x * x, axis=-1, keepdims=True) + NORM_EPS)
        err = x * r * gain - t_ref[...]
        part = 0.5 * jnp.sum(jnp.mean(err * err, axis=-1, keepdims=True), axis=0, keepdims=True)
        dx, dg_rows = _rms_bwd_rows(x, gain, err * (1.0 / d))
        dh_ref[...] = dx

        @pl.when(pl.program_id(0) == 0)
        def _():
            dg_ref[...] = jnp.zeros_like(dg_ref)
            loss_ref[...] = jnp.zeros_like(loss_ref)

        dg_ref[...] += jnp.sum(dg_rows, axis=0, keepdims=True)
        loss_ref[...] += jnp.broadcast_to(part, loss_ref.shape)

    spec = pl.BlockSpec((br, d), lambda i: (i, 0))
    vec = pl.BlockSpec((1, d), lambda i: (0, 0))
    one = pl.BlockSpec((1, LANES), lambda i: (0, 0))
    loss, dh, dg = pl.pallas_call(
        body, name=name, grid=(s_len // br,), in_specs=[spec, vec, spec],
        out_specs=[one, spec, vec],
        out_shape=[jax.ShapeDtypeStruct((1, LANES), F32), jax.ShapeDtypeStruct((s_len, d), F32),
                   jax.ShapeDtypeStruct((1, d), F32)],
        compiler_params=_params("arbitrary"))(h, g.reshape(1, d), target)
    return loss[0, 0], dh, dg


def _adamw(w, g, m, v, name):
    rows, width = w.shape
    br = _row_block(rows, width, budget=1 << 20)
    c1 = 1.0 - ADAM_B1 ** ADAM_STEP
    c2 = 1.0 - ADAM_B2 ** ADAM_STEP

    def body(w_ref, g_ref, m_ref, v_ref, d_ref, nm_ref, nv_ref):
        grad = g_ref[...]
        new_m = ADAM_B1 * m_ref[...] + (1.0 - ADAM_B1) * grad
        new_v = ADAM_B2 * v_ref[...] + (1.0 - ADAM_B2) * (grad * grad)
        d_ref[...] = -ADAM_LR * ((new_m / c1) / (jnp.sqrt(new_v / c2) + ADAM_EPS) + ADAM_WD * w_ref[...])
        nm_ref[...] = new_m
        nv_ref[...] = new_v

    spec = pl.BlockSpec((br, width), lambda i: (i, 0))
    shp = jax.ShapeDtypeStruct(w.shape, F32)
    return pl.pallas_call(body, name=name, grid=(rows // br,), in_specs=[spec] * 4,
                          out_specs=[spec] * 3, out_shape=[shp] * 3,
                          compiler_params=_params("parallel"))(w, g, m, v)


def _gelu(x):
    return 0.5 * x * (1.0 + lax.erf(x * (2.0 ** -0.5)))


def _gelu_grad(x):
    return 0.5 * (1.0 + lax.erf(x * (2.0 ** -0.5))) + x * jnp.exp(-0.5 * x * x) * ((2.0 * math.pi) ** -0.5)


def _tril_mask():
    row = lax.broadcasted_iota(jnp.int32, (BLOCK, BLOCK), 0)
    col = lax.broadcasted_iota(jnp.int32, (BLOCK, BLOCK), 1)
    return col <= row


def _gmlp_specs(s_len, d):
    gw = d // GMLP_GROUPS
    zp = pl.BlockSpec((BLOCK, 2 * d), lambda i: (i, 0))
    row = pl.BlockSpec((BLOCK, d), lambda i: (i, 0))
    vec = pl.BlockSpec((1, d), lambda i: (0, 0))
    ws = pl.BlockSpec((GMLP_GROUPS, BLOCK, BLOCK), lambda i: (0, 0, 0))
    bst = pl.BlockSpec((BLOCK, GMLP_GROUPS), lambda i: (0, 0))
    return gw, zp, row, vec, ws, bst


def _gmlp_fwd(zp, vgain, ws, bs, name):
    s_len, d2 = zp.shape
    d = d2 // 2
    gw, zp_spec, row_spec, vec_spec, ws_spec, bst_spec = _gmlp_specs(s_len, d)

    def body(zp_ref, vg_ref, ws_ref, bst_ref, y_ref):
        u = _gelu(zp_ref[:, :d].astype(F32))
        vv = _gelu(zp_ref[:, d:].astype(F32))
        r = lax.rsqrt(jnp.mean(vv * vv, axis=-1, keepdims=True) + NORM_EPS)
        vn = (vv * r * vg_ref[...]).astype(BF16)
        mask = _tril_mask()
        for g in range(GMLP_GROUPS):
            cols = slice(g * gw, (g + 1) * gw)
            wg = jnp.where(mask, ws_ref[g], 0.0).astype(BF16)
            mixed = jnp.dot(wg, vn[:, cols], preferred_element_type=F32) + bst_ref[:, g:g + 1]
            y_ref[:, cols] = (u[:, cols] * mixed).astype(BF16)

    return pl.pallas_call(body, name=name, grid=(s_len // BLOCK,),
                          in_specs=[zp_spec, vec_spec, ws_spec, bst_spec], out_specs=row_spec,
                          out_shape=jax.ShapeDtypeStruct((s_len, d), BF16),
                          compiler_params=_params("parallel"))(zp, vgain.reshape(1, d), ws, bs.T)


def _gmlp_bwd(zp, dy, vgain, ws, bs, name):
    s_len, d2 = zp.shape
    d = d2 // 2
    gw, zp_spec, row_spec, vec_spec, ws_spec, bst_spec = _gmlp_specs(s_len, d)

    def body(zp_ref, dy_ref, vg_ref, ws_ref, bst_ref, dzp_ref, dws_ref, dbst_ref, dvg_ref, dvn_ref):
        @pl.when(pl.program_id(0) == 0)
        def _():
            dws_ref[...] = jnp.zeros_like(dws_ref)
            dbst_ref[...] = jnp.zeros_like(dbst_ref)
            dvg_ref[...] = jnp.zeros_like(dvg_ref)

        zu = zp_ref[:, :d].astype(F32)
        zv = zp_ref[:, d:].astype(F32)
        u = _gelu(zu)
        vv = _gelu(zv)
        r = lax.rsqrt(jnp.mean(vv * vv, axis=-1, keepdims=True) + NORM_EPS)
        vhat = vv * r
        gain = vg_ref[...]
        vn = (vhat * gain).astype(BF16)
        dyf = dy_ref[...].astype(F32)
        dmixed = dyf * u
        dmixed_b = dmixed.astype(BF16)
        mask = _tril_mask()
        lane = lax.broadcasted_iota(jnp.int32, (BLOCK, GMLP_GROUPS), 1)
        dbs_step = jnp.zeros((BLOCK, GMLP_GROUPS), F32)
        for g in range(GMLP_GROUPS):
            cols = slice(g * gw, (g + 1) * gw)
            wg = jnp.where(mask, ws_ref[g], 0.0).astype(BF16)
            mixed = jnp.dot(wg, vn[:, cols], preferred_element_type=F32) + bst_ref[:, g:g + 1]
            dzp_ref[:, cols] = (dyf[:, cols] * mixed * _gelu_grad(zu[:, cols])).astype(BF16)
            dm = dmixed_b[:, cols]
            dw = lax.dot_general(dm, vn[:, cols], (((1,), (1,)), ((), ())), preferred_element_type=F32)
            dws_ref[g] += jnp.where(mask, dw, 0.0)
            dbs_step = dbs_step + jnp.where(lane == g, jnp.sum(dmixed[:, cols], axis=-1, keepdims=True), 0.0)
            dvn_ref[:, cols] = lax.dot_general(wg, dm, (((0,), (0,)), ((), ())), preferred_element_type=F32)
        dbst_ref[...] += dbs_step
        dvn = dvn_ref[...]
        dvg_ref[...] += jnp.sum(dvn * vhat, axis=0, keepdims=True)
        dvhat = dvn * gain
        dvv = r * (dvhat - vhat * jnp.mean(dvhat * vhat, axis=-1, keepdims=True))
        dzp_ref[:, d:] = (dvv * _gelu_grad(zv)).astype(BF16)

    return pl.pallas_call(
        body, name=name, grid=(s_len // BLOCK,),
        in_specs=[zp_spec, row_spec, vec_spec, ws_spec, bst_spec],
        out_specs=[zp_spec, ws_spec, bst_spec, vec_spec],
        out_shape=[jax.ShapeDtypeStruct((s_len, d2), BF16), jax.ShapeDtypeStruct(ws.shape, F32),
                   jax.ShapeDtypeStruct((BLOCK, GMLP_GROUPS), F32), jax.ShapeDtypeStruct((1, d), F32)],
        scratch_shapes=[pltpu.VMEM((BLOCK, d), F32)],
        compiler_params=_params("arbitrary"))(zp, dy, vgain.reshape(1, d), ws, bs.T)


def _rope_tables(s_len, sign):
    half = ROPE_DIM // 2
    inv_freq = ROPE_THETA ** (-(jnp.arange(half, dtype=F32) * 2.0 / ROPE_DIM))
    ang = jnp.arange(s_len, dtype=F32)[:, None] * inv_freq[None, :]
    cos, sin = jnp.cos(ang), jnp.sin(ang) * sign
    pad = jnp.zeros((s_len, SWA_HEAD_DIM - ROPE_DIM), F32)
    zero = jnp.zeros_like(sin)
    cos_t = jnp.concatenate([cos, cos, pad + 1.0], axis=1)
    sin_up = jnp.concatenate([-sin, zero, pad], axis=1)
    sin_dn = jnp.concatenate([zero, sin, pad], axis=1)
    return [jnp.tile(t, (1, LANES // SWA_HEAD_DIM)) for t in (cos_t, sin_up, sin_dn)]


def _rotate(x, cos_t, sin_up, sin_dn):
    width = x.shape[-1]
    half = ROPE_DIM // 2
    reps = width // cos_t.shape[-1]
    if reps > 1:
        cos_t, sin_up, sin_dn = (jnp.tile(t, (1, reps)) for t in (cos_t, sin_up, sin_dn))
    elif reps == 0:
        cos_t, sin_up, sin_dn = (t[:, :width] for t in (cos_t, sin_up, sin_dn))
    return x * cos_t + pltpu.roll(x, width - half, 1) * sin_up + pltpu.roll(x, half, 1) * sin_dn


def _rope_fwd(qkv, name):
    s_len, total = qkv.shape
    wkv = total // (SWA_GROUP + 2)
    wq = SWA_GROUP * wkv
    br = _row_block(s_len, total, budget=2 << 20)
    tables = _rope_tables(s_len, 1.0)

    def body(q_ref, k_ref, v_ref, c_ref, su_ref, sd_ref, qo_ref, ko_ref, vo_ref):
        t = (c_ref[...], su_ref[...], sd_ref[...])
        qo_ref[...] = _rotate(q_ref[...], *t).astype(BF16)
        ko_ref[...] = _rotate(k_ref[...], *t).astype(BF16)
        vo_ref[...] = v_ref[...].astype(BF16)

    qs = pl.BlockSpec((br, wq), lambda i: (i, 0))
    ks = pl.BlockSpec((br, wkv), lambda i: (i, SWA_GROUP))
    vs = pl.BlockSpec((br, wkv), lambda i: (i, SWA_GROUP + 1))
    ts = pl.BlockSpec((br, LANES), lambda i: (i, 0))
    kv_out = pl.BlockSpec((br, wkv), lambda i: (i, 0))
    return pl.pallas_call(
        body, name=name, grid=(s_len // br,), in_specs=[qs, ks, vs, ts, ts, ts],
        out_specs=[qs, kv_out, kv_out],
        out_shape=[jax.ShapeDtypeStruct((s_len, wq), BF16), jax.ShapeDtypeStruct((s_len, wkv), BF16),
                   jax.ShapeDtypeStruct((s_len, wkv), BF16)],
        compiler_params=_params("parallel"))(qkv, qkv, qkv, *tables)


def _rope_bwd(dq, dk, dv, name):
    s_len, wq = dq.shape
    wkv = dk.shape[1]
    br = _row_block(s_len, wq + 2 * wkv, budget=2 << 20)
    tables = _rope_tables(s_len, -1.0)

    def body(q_ref, k_ref, v_ref, c_ref, su_ref, sd_ref, o_ref):
        t = (c_ref[...], su_ref[...], sd_ref[...])
        o_ref[:, :wq] = _rotate(q_ref[...], *t).astype(BF16)
        o_ref[:, wq:wq + wkv] = _rotate(k_ref[...], *t).astype(BF16)
        o_ref[:, wq + wkv:] = v_ref[...].astype(BF16)

    qs = pl.BlockSpec((br, wq), lambda i: (i, 0))
    kvs = pl.BlockSpec((br, wkv), lambda i: (i, 0))
    ts = pl.BlockSpec((br, LANES), lambda i: (i, 0))
    return pl.pallas_call(
        body, name=name, grid=(s_len // br,), in_specs=[qs, kvs, kvs, ts, ts, ts],
        out_specs=pl.BlockSpec((br, wq + 2 * wkv), lambda i: (i, 0)),
        out_shape=jax.ShapeDtypeStruct((s_len, wq + 2 * wkv), BF16),
        compiler_params=_params("parallel"))(dq, dk, dv, *tables)


def _swa_valid(i):
    row = lax.broadcasted_iota(jnp.int32, (BLOCK, 2 * BLOCK), 0)
    col = lax.broadcasted_iota(jnp.int32, (BLOCK, 2 * BLOCK), 1)
    return (col - BLOCK <= row) & (row < col) & ((col >= BLOCK) | (i > 0))


def _swa_specs(wq, wkv):
    q_spec = pl.BlockSpec((BLOCK, wq), lambda i: (i, 0))
    cur = pl.BlockSpec((BLOCK, wkv), lambda i: (i, 0))
    prev = pl.BlockSpec((BLOCK, wkv), lambda i: (jnp.maximum(i - 1, 0), 0))
    sink = pl.BlockSpec(memory_space=pltpu.SMEM)
    return q_spec, cur, prev, sink


def _swa_probs(q_h, k_cat, valid, sink):
    s = lax.dot_general(q_h, k_cat, (((1,), (1,)), ((), ())), preferred_element_type=F32)
    s = jnp.where(valid, s * (SWA_HEAD_DIM ** -0.5), NEG_INF)
    m = jnp.maximum(jnp.max(s, axis=-1, keepdims=True), sink)
    p = jnp.exp(s - m)
    e_sink = jnp.exp(sink - m)
    denom = jnp.sum(p, axis=-1, keepdims=True) + e_sink
    return p / denom, e_sink / denom


def _swa_fwd(q, k, v, sinks, name):
    s_len, wq = q.shape
    wkv = k.shape[1]
    hd = SWA_HEAD_DIM
    q_spec, cur, prev, sink_spec = _swa_specs(wq, wkv)

    def body(q_ref, kc_ref, kp_ref, vc_ref, vp_ref, sink_ref, o_ref):
        valid = _swa_valid(pl.program_id(0))
        for j in range(wkv // hd):
            lanes = slice(j * hd, (j + 1) * hd)
            k_cat = jnp.concatenate([kp_ref[:, lanes], kc_ref[:, lanes]], axis=0)
            v_cat = jnp.concatenate([vp_ref[:, lanes], vc_ref[:, lanes]], axis=0)
            for hh in range(SWA_GROUP):
                h = j * SWA_GROUP + hh
                pn, _ = _swa_probs(q_ref[:, h * hd:(h + 1) * hd], k_cat, valid, sink_ref[h])
                o_ref[:, h * hd:(h + 1) * hd] = jnp.dot(
                    pn.astype(BF16), v_cat, preferred_element_type=F32).astype(BF16)

    return pl.pallas_call(body, name=name, grid=(s_len // BLOCK,),
                          in_specs=[q_spec, cur, prev, cur, prev, sink_spec], out_specs=q_spec,
                          out_shape=jax.ShapeDtypeStruct((s_len, wq), BF16),
                          compiler_params=_params("parallel"))(q, k, k, v, v, sinks)


def _swa_bwd(q, k, v, sinks, do, name):
    s_len, wq = q.shape
    wkv = k.shape[1]
    hd = SWA_HEAD_DIM
    q_spec, cur, prev, sink_spec = _swa_specs(wq, wkv)
    full = pl.BlockSpec((s_len, wkv), lambda i: (0, 0))
    one = pl.BlockSpec((1, LANES), lambda i: (0, 0))
    scale = hd ** -0.5

    def body(q_ref, kc_ref, kp_ref, vc_ref, vp_ref, sink_ref, do_ref, dq_ref, dk_ref, dv_ref, ds_ref):
        i = pl.program_id(0)

        @pl.when(i == 0)
        def _():
            dk_ref[...] = jnp.zeros_like(dk_ref)
            dv_ref[...] = jnp.zeros_like(dv_ref)
            ds_ref[...] = jnp.zeros_like(ds_ref)

        valid = _swa_valid(i)
        lane = lax.broadcasted_iota(jnp.int32, (1, LANES), 1)
        dsink_step = jnp.zeros((1, LANES), F32)
        rows_prev = pl.ds(pl.multiple_of(jnp.maximum(i - 1, 0) * BLOCK, BLOCK), BLOCK)
        rows_cur = pl.ds(pl.multiple_of(i * BLOCK, BLOCK), BLOCK)
        for j in range(wkv // hd):
            lanes = slice(j * hd, (j + 1) * hd)
            k_cat = jnp.concatenate([kp_ref[:, lanes], kc_ref[:, lanes]], axis=0)
            v_cat = jnp.concatenate([vp_ref[:, lanes], vc_ref[:, lanes]], axis=0)
            dk_cat = jnp.zeros((2 * BLOCK, hd), F32)
            dv_cat = jnp.zeros((2 * BLOCK, hd), F32)
            for hh in range(SWA_GROUP):
                h = j * SWA_GROUP + hh
                q_h = q_ref[:, h * hd:(h + 1) * hd]
                do_h = do_ref[:, h * hd:(h + 1) * hd]
                pn, p_sink = _swa_probs(q_h, k_cat, valid, sink_ref[h])
                dpn = lax.dot_general(do_h, v_cat, (((1,), (1,)), ((), ())), preferred_element_type=F32)
                delta = jnp.sum(dpn * pn, axis=-1, keepdims=True)
                ds = (pn * (dpn - delta) * scale).astype(BF16)
                dsink_h = -jnp.sum(p_sink * delta, axis=0, keepdims=True)
                dsink_step = dsink_step + jnp.where(lane == h, dsink_h, 0.0)
                dq_ref[:, h * hd:(h + 1) * hd] = jnp.dot(ds, k_cat, preferred_element_type=F32)
                dk_cat = dk_cat + lax.dot_general(ds, q_h, (((0,), (0,)), ((), ())),
                                                  preferred_element_type=F32)
                dv_cat = dv_cat + lax.dot_general(pn.astype(BF16), do_h, (((0,), (0,)), ((), ())),
                                                  preferred_element_type=F32)
            dk_ref[rows_prev, lanes] += dk_cat[:BLOCK]
            dk_ref[rows_cur, lanes] += dk_cat[BLOCK:]
            dv_ref[rows_prev, lanes] += dv_cat[:BLOCK]
            dv_ref[rows_cur, lanes] += dv_cat[BLOCK:]
        ds_ref[...] += dsink_step

    return pl.pallas_call(
        body, name=name, grid=(s_len // BLOCK,),
        in_specs=[q_spec, cur, prev, cur, prev, sink_spec, q_spec],
        out_specs=[q_spec, full, full, one],
        out_shape=[jax.ShapeDtypeStruct((s_len, wq), F32), jax.ShapeDtypeStruct((s_len, wkv), F32),
                   jax.ShapeDtypeStruct((s_len, wkv), F32), jax.ShapeDtypeStruct((1, LANES), F32)],
        compiler_params=_params("arbitrary"))(q, k, k, v, v, sinks, do)


def _log_sigmoid(x):
    return jnp.minimum(x, 0.0) - jnp.log(1.0 + jnp.exp(-jnp.abs(x)))


def _tri_ones(lower):
    row = lax.broadcasted_iota(jnp.int32, (BLOCK, BLOCK), 0)
    col = lax.broadcasted_iota(jnp.int32, (BLOCK, BLOCK), 1)
    return jnp.where((col <= row) if lower else (col >= row), 1.0, 0.0).astype(F32)


def _fox_decay(proj, bf_row, fl_block, name):
    s_len = proj.shape[0]
    nchunk = s_len // BLOCK

    def body(fl_ref, bf_ref, dec_ref):
        tri = _tri_ones(True)
        carry = jnp.zeros((1, LANES), F32)
        for c in range(nchunk):
            rows = slice(c * BLOCK, (c + 1) * BLOCK)
            log_f = _log_sigmoid(fl_ref[rows, :] + bf_ref[...])
            loc = jnp.dot(tri, log_f, preferred_element_type=F32, precision=lax.Precision.HIGHEST) + carry
            dec_ref[rows, :] = loc
            carry = loc[BLOCK - 1:BLOCK, :]

    return pl.pallas_call(
        body, name=name, grid=(1,),
        in_specs=[pl.BlockSpec((s_len, LANES), lambda i: (0, fl_block)),
                  pl.BlockSpec((1, LANES), lambda i: (0, 0))],
        out_specs=pl.BlockSpec((s_len, LANES), lambda i: (0, 0)),
        out_shape=jax.ShapeDtypeStruct((s_len, LANES), F32),
        compiler_params=_params("arbitrary"))(proj, bf_row)


def _fox_decay_bwd(ddq, ddk, proj, bf_row, fl_block, heads, name):
    s_len = proj.shape[0]
    nchunk = s_len // BLOCK

    def body(ddq_ref, ddk_ref, fl_ref, bf_ref, dfl_ref, dbf_ref):
        tri = _tri_ones(False)
        lane_ok = lax.broadcasted_iota(jnp.int32, (BLOCK, LANES), 1) < heads
        carry = jnp.zeros((1, LANES), F32)
        dbf = jnp.zeros((1, LANES), F32)
        for c in reversed(range(nchunk)):
            rows = slice(c * BLOCK, (c + 1) * BLOCK)
            ddec = ddq_ref[rows, :] + ddk_ref[rows, :]
            dlog = jnp.dot(tri, ddec, preferred_element_type=F32, precision=lax.Precision.HIGHEST) + carry
            carry = dlog[0:1, :]
            dfl = jnp.where(lane_ok, dlog * _sigmoid(-(fl_ref[rows, :] + bf_ref[...])), 0.0)
            dfl_ref[rows, :] = dfl.astype(BF16)
            dbf = dbf + jnp.sum(dfl, axis=0, keepdims=True)
        dbf_ref[...] = dbf

    blk = pl.BlockSpec((s_len, LANES), lambda i: (0, 0))
    one = pl.BlockSpec((1, LANES), lambda i: (0, 0))
    return pl.pallas_call(
        body, name=name, grid=(1,),
        in_specs=[blk, blk, pl.BlockSpec((s_len, LANES), lambda i: (0, fl_block)), one],
        out_specs=[blk, one],
        out_shape=[jax.ShapeDtypeStruct((s_len, LANES), BF16), jax.ShapeDtypeStruct((1, LANES), F32)],
        compiler_params=_params("arbitrary"))(ddq, ddk, proj, bf_row)


def _fox_scores(q, k, decq, deck, i, bq):
    s_len = k.shape[0]
    s = lax.dot_general(q, k, (((1,), (1,)), ((), ())), preferred_element_type=F32)
    s = s * (FOX_HEAD_DIM ** -0.5) + decq - deck
    row = lax.broadcasted_iota(jnp.int32, (bq, s_len), 0) + i * bq
    col = lax.broadcasted_iota(jnp.int32, (bq, s_len), 1)
    s = jnp.where(col <= row, s, NEG_INF)
    p = jnp.exp(s - jnp.max(s, axis=-1, keepdims=True))
    return p / jnp.sum(p, axis=-1, keepdims=True)


def _fox_key_spans(s_len, bq):
    n_span = min(4, s_len // bq)
    return [(j + 1) * (s_len // n_span) for j in range(n_span)]


def _fox_span_of(i, s_len, bq):
    span = s_len // min(4, s_len // bq)
    return ((i * bq) // span + 1) * span


def _fox_specs(s_len, heads, bq):
    hd = FOX_HEAD_DIM
    q_spec = pl.BlockSpec((bq, hd), lambda h, i: (i, h))
    k_spec = pl.BlockSpec((s_len, hd), lambda h, i: (0, heads + h))
    v_spec = pl.BlockSpec((s_len, hd), lambda h, i: (0, 2 * heads + h))
    dq_spec = pl.BlockSpec((None, bq, 1), lambda h, i: (h, i, 0))
    dk_spec = pl.BlockSpec((None, 1, s_len), lambda h, i: (h, 0, 0))
    return q_spec, k_spec, v_spec, dq_spec, dk_spec


def _fox_fwd(proj, decq, deck, heads, name):
    s_len = proj.shape[0]
    bq = _pick(s_len, (256, 128))
    q_spec, k_spec, v_spec, dq_spec, dk_spec = _fox_specs(s_len, heads, bq)

    def body(q_ref, k_ref, v_ref, decq_ref, deck_ref, o_ref):
        i = pl.program_id(1)
        for klen in _fox_key_spans(s_len, bq):
            @pl.when(_fox_span_of(i, s_len, bq) == klen)
            def _(klen=klen):
                pn = _fox_scores(q_ref[...].astype(BF16), k_ref[:klen, :].astype(BF16), decq_ref[...],
                                 deck_ref[:, :klen], i, bq)
                o_ref[...] = jnp.dot(pn.astype(BF16), v_ref[:klen, :].astype(BF16),
                                     preferred_element_type=F32).astype(BF16)

    return pl.pallas_call(body, name=name, grid=(heads, s_len // bq),
                          in_specs=[q_spec, k_spec, v_spec, dq_spec, dk_spec], out_specs=q_spec,
                          out_shape=jax.ShapeDtypeStruct((s_len, heads * FOX_HEAD_DIM), BF16),
                          compiler_params=_params("parallel", "parallel"))(proj, proj, proj, decq, deck)


def _fox_bwd(proj, decq, deck, do, heads, name):
    s_len = proj.shape[0]
    d = heads * FOX_HEAD_DIM
    bq = _pick(s_len, (256, 128))
    q_spec, k_spec, v_spec, dq_spec, dk_spec = _fox_specs(s_len, heads, bq)
    acc_spec = pl.BlockSpec((s_len, FOX_HEAD_DIM), lambda h, i: (0, h))
    scale = FOX_HEAD_DIM ** -0.5

    def body(q_ref, k_ref, v_ref, decq_ref, deck_ref, do_ref, dq_ref, dk_ref, dv_ref, ddq_ref, ddk_ref):
        i = pl.program_id(1)

        @pl.when(i == 0)
        def _():
            dk_ref[...] = jnp.zeros_like(dk_ref)
            dv_ref[...] = jnp.zeros_like(dv_ref)
            ddk_ref[...] = jnp.zeros_like(ddk_ref)

        q = q_ref[...].astype(BF16)
        do_b = do_ref[...]
        for klen in _fox_key_spans(s_len, bq):
            @pl.when(_fox_span_of(i, s_len, bq) == klen)
            def _(klen=klen):
                k = k_ref[:klen, :].astype(BF16)
                pn = _fox_scores(q, k, decq_ref[...], deck_ref[:, :klen], i, bq)
                dpn = lax.dot_general(do_b, v_ref[:klen, :].astype(BF16), (((1,), (1,)), ((), ())),
                                      preferred_element_type=F32)
                ds = pn * (dpn - jnp.sum(dpn * pn, axis=-1, keepdims=True))
                ddq_ref[...] = jnp.sum(ds, axis=-1, keepdims=True)
                ddk_ref[:, :klen] -= jnp.sum(ds, axis=0, keepdims=True)
                ds_b = (ds * scale).astype(BF16)
                dq_ref[...] = jnp.dot(ds_b, k, preferred_element_type=F32).astype(BF16)
                dk_ref[:klen, :] += lax.dot_general(ds_b, q, (((0,), (0,)), ((), ())),
                                                    preferred_element_type=F32)
                dv_ref[:klen, :] += lax.dot_general(pn.astype(BF16), do_b, (((0,), (0,)), ((), ())),
                                                    preferred_element_type=F32)

    return pl.pallas_call(
        body, name=name, grid=(heads, s_len // bq),
        in_specs=[q_spec, k_spec, v_spec, dq_spec, dk_spec, q_spec],
        out_specs=[q_spec, acc_spec, acc_spec, dq_spec, dk_spec],
        out_shape=[jax.ShapeDtypeStruct((s_len, d), BF16), jax.ShapeDtypeStruct((s_len, d), F32),
                   jax.ShapeDtypeStruct((s_len, d), F32), jax.ShapeDtypeStruct((heads, s_len, 1), F32),
                   jax.ShapeDtypeStruct((heads, 1, s_len), F32)],
        compiler_params=_params("parallel", "arbitrary"))(proj, proj, proj, decq, deck, do)


def _place():
    x, y, c = lax.axis_index("x"), lax.axis_index("y"), lax.axis_index("c")
    chips = [(1 - x, y), (x, 1 - y), (1 - x, 1 - y)]
    return x, y, c, chips


def _remote(src, dst, send_sems, recv_sems, idx, to):
    return pltpu.make_async_remote_copy(src_ref=src, dst_ref=dst, send_sem=send_sems.at[idx],
                                        recv_sem=recv_sems.at[idx], device_id=to, device_id_type=MESH)


def _row_chunks(rows, want):
    for k in (want, want // 2, want // 4):
        if k >= 1 and rows % (16 * k) == 0:
            return [(j * (rows // k), rows // k) for j in range(k)]
    return [(0, rows)]


D2D_CHUNKS = 8


def _cast_into_slot(w, me, name, dep=None):
    rows, width = w.shape
    br = _row_block(rows, width, budget=4 << 20)

    def body(me_ref, w_ref, *rest):
        rest[-1][...] = w_ref[...].astype(BF16)

    in_specs = [pl.BlockSpec((br, width), lambda i, me_ref: (i, 0))]
    if dep is not None:
        in_specs.append(pl.BlockSpec(dep.shape, lambda i, me_ref: (0, 0)))
    return pl.pallas_call(
        body, name=name,
        grid_spec=pltpu.PrefetchScalarGridSpec(
            num_scalar_prefetch=1, grid=(rows // br,), in_specs=in_specs,
            out_specs=pl.BlockSpec((None, br, width), lambda i, me_ref: (me_ref[0], i, 0))),
        out_shape=jax.ShapeDtypeStruct((N_CHIPS, rows, width), BF16),
        compiler_params=_params("parallel"))(me, w, *([dep] if dep is not None else []))


def _hbm(arr):
    return pltpu.with_memory_space_constraint(arr, pltpu.HBM)


def _token_shape():
    return jax.ShapeDtypeStruct((8, LANES), F32)


def _add_pair(grad, got, c, name):
    _, half, width = got.shape
    br = _row_block(half, width, itemsize=2, budget=3 << 20)
    nb = half // br

    def body(c_ref, a_ref, b_ref, o_ref):
        o_ref[...] = (a_ref[...].astype(F32) + b_ref[...].astype(F32)).astype(BF16)

    spec = pl.BlockSpec((None, br, width), lambda j, i, c_ref: (j, i, 0))
    mine = pl.BlockSpec((None, br, width), lambda j, i, c_ref: (j, c_ref[0] * nb + i, 0))
    return pl.pallas_call(
        body, name=name,
        grid_spec=pltpu.PrefetchScalarGridSpec(num_scalar_prefetch=1, grid=(N_CHIPS, nb),
                                               in_specs=[mine, spec], out_specs=spec),
        out_shape=jax.ShapeDtypeStruct(got.shape, BF16),
        compiler_params=_params("parallel", "parallel"))(c, grad, got)


def _adamw_halves(w, pair, others, pair_t, others_t, m, v, place, name):
    rows, width = w.shape
    half = rows // 2
    br = _row_block(half, width, budget=3 << 19)
    nb = half // br
    c1 = 1.0 - ADAM_B1 ** ADAM_STEP
    c2 = 1.0 - ADAM_B2 ** ADAM_STEP

    def chip_sum(p, o3_ref):
        acc = p.astype(F32)
        for r in range(N_CHIPS - 1):
            acc = acc + o3_ref[r].astype(F32)
        return acc

    def body(place_ref, w_ref, p_ref, o_ref, pt_ref, ot_ref, m_ref, v_ref, g_ref, d_ref, nm_ref, nv_ref):
        grad = jnp.where(pl.program_id(0) == place_ref[0], chip_sum(p_ref[...], o_ref), chip_sum(pt_ref[...], ot_ref))
        new_m = ADAM_B1 * m_ref[...] + (1.0 - ADAM_B1) * grad
        new_v = ADAM_B2 * v_ref[...] + (1.0 - ADAM_B2) * (grad * grad)
        g_ref[...] = grad
        d_ref[...] = -ADAM_LR * ((new_m / c1) / (jnp.sqrt(new_v / c2) + ADAM_EPS) + ADAM_WD * w_ref[...])
        nm_ref[...] = new_m
        nv_ref[...] = new_v

    full = pl.BlockSpec((br, width), lambda h, i, s: (h * nb + i, 0))
    mine_i = lambda h, i, s: jnp.where(h == s[0], i, 0)
    theirs_i = lambda h, i, s: jnp.where(h == s[0], 0, i)
    specs = [full,
             pl.BlockSpec((None, br, width), lambda h, i, s: (s[1], mine_i(h, i, s), 0)),
             pl.BlockSpec((N_CHIPS - 1, br, width), lambda h, i, s: (0, mine_i(h, i, s), 0)),
             pl.BlockSpec((br, width), lambda h, i, s: (theirs_i(h, i, s), 0)),
             pl.BlockSpec((N_CHIPS - 1, br, width), lambda h, i, s: (0, theirs_i(h, i, s), 0)),
             full, full]
    shp = jax.ShapeDtypeStruct(w.shape, F32)
    return pl.pallas_call(
        body, name=name,
        grid_spec=pltpu.PrefetchScalarGridSpec(num_scalar_prefetch=1, grid=(2, nb), in_specs=specs,
                                               out_specs=[full] * 4),
        out_shape=[shp] * 4,
        compiler_params=_params("parallel", "parallel"))(place, w, pair, others, pair_t, others_t, m, v)


class _Transfer:
    def __init__(self, n_sems, build):
        self.n_sems, self.build = n_sems, build


def _copies(src_of, dst_of, land_of, rows, chunks, send, recv, idx, to):
    starts = [_remote(src_of(s, z), dst_of(s, z), send, recv, idx, to) for s, z in _row_chunks(rows, chunks)]
    return starts, _remote(src_of(0, rows), land_of(0, rows), send, recv, idx, to)


def _gather_direct(keys, shapes):
    def build(refs, send, recv):
        x, y, c, chips = _place()
        me = 2 * x + y
        out = []
        for t, key in enumerate(keys):
            half = shapes[t][1] // 2
            for r, chip in enumerate(chips[:2]):
                slot = 2 * chip[0] + chip[1]
                out.append(_copies(lambda s, z, key=key, half=half: refs[key].at[me, pl.ds(c * half + s, z)],
                                   lambda s, z, key=key, half=half: refs[key].at[me, pl.ds(c * half + s, z)],
                                   lambda s, z, key=key, half=half, slot=slot: refs[key].at[slot, pl.ds(c * half + s, z)],
                                   half, 1, send, recv, 2 * t + r, (*chip, c)))
        return out
    return _Transfer(2 * len(keys), build)


def _gather_relay(keys, shapes):
    def build(refs, send, recv):
        x, y, c, chips = _place()
        slot_x, slot_y, slot_d = (2 * ch[0] + ch[1] for ch in chips)
        src_slot = slot_y + c * (slot_x - slot_y)
        to = (x ^ (1 - c), y ^ c, c)
        out = []
        for t, key in enumerate(keys):
            half = shapes[t][1] // 2
            out.append(_copies(lambda s, z, key=key, half=half: refs[key].at[src_slot, pl.ds(c * half + s, z)],
                               lambda s, z, key=key, half=half: refs[key].at[src_slot, pl.ds(c * half + s, z)],
                               lambda s, z, key=key, half=half: refs[key].at[slot_d, pl.ds(c * half + s, z)],
                               half, 1, send, recv, t, to))
        return out
    return _Transfer(len(keys), build)


def _gather_pair(keys, shapes):
    def build(refs, send, recv):
        x, y, c, chips = _place()
        out = []
        for t, key in enumerate(keys):
            half = shapes[t][1] // 2
            for r, chip in enumerate(chips):
                slot = 2 * chip[0] + chip[1]
                mine = lambda s, z, key=key, half=half, slot=slot: refs[key].at[slot, pl.ds(c * half + s, z)]
                land = lambda s, z, key=key, half=half, slot=slot: refs[key].at[slot, pl.ds((1 - c) * half + s, z)]
                out.append(_copies(mine, mine, land, half, D2D_CHUNKS, send, recv, 3 * t + r, (x, y, 1 - c)))
        return out
    return _Transfer(3 * len(keys), build)


def _grad_pair(keys, lands, shapes):
    def build(refs, send, recv):
        x, y, c, _ = _place()
        out = []
        for t, (key, land) in enumerate(zip(keys, lands)):
            half = shapes[t][1] // 2
            for j in range(N_CHIPS):
                out.append(_copies(
                    lambda s, z, key=key, half=half, j=j: refs[key].at[j, pl.ds((1 - c) * half + s, z)],
                    lambda s, z, land=land, j=j: refs[land].at[j, pl.ds(s, z)],
                    lambda s, z, land=land, j=j: refs[land].at[j, pl.ds(s, z)],
                    half, 2, send, recv, N_CHIPS * t + j, (x, y, 1 - c)))
        return out
    return _Transfer(N_CHIPS * len(keys), build)


def _grad_chips(keys, lands, shapes):
    def build(refs, send, recv):
        x, y, c, chips = _place()
        out = []
        for t, (key, land) in enumerate(zip(keys, lands)):
            rows = shapes[t][1]
            for r, chip in enumerate(chips):
                slot = 2 * chip[0] + chip[1]
                out.append(_copies(lambda s, z, key=key, slot=slot: refs[key].at[slot, pl.ds(s, z)],
                                   lambda s, z, land=land, r=r: refs[land].at[r, pl.ds(s, z)],
                                   lambda s, z, land=land, r=r: refs[land].at[r, pl.ds(s, z)],
                                   rows, 1, send, recv, 3 * t + r, (*chip, c)))
        return out
    return _Transfer(3 * len(keys), build)


def _grad_join(pairs, others, pair_lands, other_lands, shapes):
    def build(refs, send, recv):
        x, y, c, _ = _place()
        me = 2 * x + y
        sibling = (x, y, 1 - c)
        out = []
        for t, (pair, other, pair_land, other_land) in enumerate(zip(pairs, others, pair_lands, other_lands)):
            rows = shapes[t][1]
            land = lambda s, z, k=pair_land: refs[k].at[pl.ds(s, z)]
            out.append(_copies(lambda s, z, k=pair: refs[k].at[me, pl.ds(s, z)], land, land,
                               rows, D2D_CHUNKS, send, recv, N_CHIPS * t, sibling))
            for r in range(N_CHIPS - 1):
                land = lambda s, z, k=other_land, r=r: refs[k].at[r, pl.ds(s, z)]
                out.append(_copies(lambda s, z, k=other, r=r: refs[k].at[r, pl.ds(s, z)], land, land,
                                   rows, D2D_CHUNKS // 2, send, recv, N_CHIPS * t + 1 + r, sibling))
        return out
    return _Transfer(N_CHIPS * len(pairs), build)


def _comm_call(name, arrays, waits, starts, after):
    keys = list(arrays)
    n, nw, ns = len(keys), len(waits), len(starts)

    def body(*refs):
        in_sems = refs[n:n + 2 * nw]
        base = n + 2 * nw + 1
        out_sems = refs[base:base + 2 * ns]
        bufs = dict(zip(keys, refs[base + 2 * ns:base + 2 * ns + n]))
        token = refs[base + 2 * ns + n]
        for k, (transfer, _, _) in enumerate(waits):
            for _, whole in transfer.build(bufs, in_sems[2 * k], in_sems[2 * k + 1]):
                whole.wait_send()
                whole.wait_recv()
        for k, transfer in enumerate(starts):
            for chunks, _ in transfer.build(bufs, out_sems[2 * k], out_sems[2 * k + 1]):
                for cp in chunks:
                    cp.start()
        token[...] = jnp.zeros_like(token)

    sem_shapes = []
    for transfer in starts:
        sem_shapes += [pltpu.SemaphoreType.DMA((transfer.n_sems,))] * 2
    operands = [_hbm(arrays[k]) for k in keys]
    for _, send, recv in waits:
        operands += [send, recv]
    res = pl.pallas_call(
        body, name=name, in_specs=[HBM] * n + [SEM] * (2 * nw) + [pl.BlockSpec(memory_space=pl.ANY)],
        out_specs=[SEM] * (2 * ns) + [HBM] * n + [pl.BlockSpec(memory_space=pltpu.VMEM)],
        out_shape=sem_shapes + [pltpu.HBM(arrays[k].shape, arrays[k].dtype) for k in keys] + [_token_shape()],
        input_output_aliases={t: 2 * ns + t for t in range(n)},
        compiler_params=pltpu.CompilerParams(has_side_effects=EFFECT),
    )(*operands, after)
    sems = [(res[2 * k], res[2 * k + 1]) for k in range(ns)]
    return dict(zip(keys, res[2 * ns:2 * ns + n])), sems, res[2 * ns + n]


def _device_gather(part_key, all_key, rows):
    def build(refs, send, recv):
        x, y, c, _ = _place()
        me = 4 * x + 2 * y + c
        out = []
        for r in range(1, N_DEV):
            peer = (x ^ (r >> 2), y ^ ((r >> 1) & 1), c ^ (r & 1))
            theirs = 4 * peer[0] + 2 * peer[1] + peer[2]
            out.append(_copies(lambda s, z: refs[part_key].at[pl.ds(s, z)],
                               lambda s, z: refs[all_key].at[me, pl.ds(s, z)],
                               lambda s, z, theirs=theirs: refs[all_key].at[theirs, pl.ds(s, z)],
                               rows, 1, send, recv, r - 1, peer))
        return out
    return _Transfer(N_DEV - 1, build)


def _sum_devices(parts, name):
    _, rows, width = parts.shape
    br = _row_block(rows, width, budget=1 << 19)

    def body(p_ref, o_ref):
        acc = p_ref[0]
        for dev in range(1, N_DEV):
            acc = acc + p_ref[dev]
        o_ref[...] = acc

    return pl.pallas_call(body, name=name, grid=(rows // br,),
                          in_specs=[pl.BlockSpec((N_DEV, br, width), lambda i: (0, i, 0))],
                          out_specs=pl.BlockSpec((br, width), lambda i: (i, 0)),
                          out_shape=jax.ShapeDtypeStruct((rows, width), F32),
                          compiler_params=_params("parallel"))(parts)


INPUT_NAMES = None


def _weight_names():
    names = []
    for i, kind in enumerate(("gmlp", "swa", "fox", "gmlp")):
        p = f"l{i}_"
        names += [p + "ffn1_norm", p + "ffn1_wi", p + "ffn1_wo", p + "mix_norm", p + "mix_win"]
        if kind == "gmlp":
            names += [p + "gmlp_vnorm", p + "gmlp_ws", p + "gmlp_bs"]
        elif kind == "swa":
            names += [p + "swa_sinks"]
        else:
            names += [p + "fox_bf"]
        names += [p + "mix_wout", p + "ffn2_norm", p + "ffn2_wi", p + "ffn2_wo"]
    return names + ["final_norm"]


WEIGHTS = _weight_names()
MIXERS = ("gmlp", "swa", "fox", "gmlp")
BIG = ("ffn1_wi", "ffn1_wo", "mix_win", "mix_wout", "ffn2_wi", "ffn2_wo")


def _ffn_fwd(h, gain, wi, wo, tag, dep=None):
    n = _rms_fwd(h, gain, tag + "_norm", dep=dep)
    z = _matmul(n, wi, name=tag + "_up", out_dtype=BF16)
    a = _swiglu_fwd(z, tag + "_act")
    f, d = wo.shape[0] * wo.shape[1], wo.shape[2]
    out = _matmul(a, wo.reshape(f, d), name=tag + "_down", out_dtype=F32, scale=0.5, resid=h)
    return out, (h, n, z, a)


def _ffn_bwd(dout, saved, gain, wi, wo, tag, dep=None, grads_ready=None):
    h, n, z, a = saved
    f, d = wo.shape[0] * wo.shape[1], wo.shape[2]
    da = _matmul(dout, wo.reshape(f, d), tb=True, name=tag + "_bdown", out_dtype=BF16, scale=0.5, dep=dep)
    dwo = _matmul(a, dout, ta=True, name=tag + "_gdown", out_dtype=BF16, scale=0.5, dep=dep).reshape(wo.shape)
    dz = _swiglu_bwd(z, da, tag + "_bact")
    dwi = _matmul(n, dz, ta=True, name=tag + "_gup", out_dtype=BF16, out_shards=N_CHIPS)
    dn = _matmul(dz, wi, tb=True, name=tag + "_bup", out_dtype=BF16,
                 dep=grads_ready(dwi, dwo) if grads_ready is not None else None)
    dh, dgain = _norm_bwd(h, gain, dn, dout, tag + "_bnorm")
    return dh, dgain, dwi, dwo


def _natural(w_sharded, pad_to):
    ns, rows, csh = w_sharded.shape
    nat = jnp.transpose(w_sharded, (1, 0, 2)).reshape(rows, ns * csh)
    extra = (-nat.shape[1]) % pad_to
    return jnp.pad(nat, ((0, 0), (0, extra))) if extra else nat


def _mixer_fwd(kind, h, p, tag, dep=None):
    s_len, d = h.shape
    n = _rms_fwd(h, p["mix_norm"], tag + "_norm", dep=dep)
    wout = p["mix_wout"].reshape(d, d)
    if kind == "gmlp":
        zp = _matmul(n, p["mix_win"], name=tag + "_in", out_dtype=BF16)
        y = _gmlp_fwd(zp, p["gmlp_vnorm"], p["gmlp_ws"], p["gmlp_bs"], tag + "_gate")
        saved = (h, n, zp, y)
    elif kind == "swa":
        qkv = _matmul(n, p["mix_win"], name=tag + "_in", out_dtype=F32)
        q, k, v = _rope_fwd(qkv, tag + "_rope")
        y = _swa_fwd(q, k, v, p["swa_sinks"], tag + "_attn")
        saved = (h, n, q, k, v, y)
    else:
        heads = d // FOX_HEAD_DIM
        win = _natural(p["mix_win"], LANES)
        proj = _matmul(n, win, name=tag + "_in", out_dtype=F32)
        bf_row = jnp.pad(p["fox_bf"], (0, LANES - heads)).reshape(1, LANES)
        dec = _fox_decay(proj, bf_row, 3 * heads, tag + "_decay")
        dec_t = dec[:, :heads].T
        decq, deck = dec_t.reshape(heads, s_len, 1), dec_t.reshape(heads, 1, s_len)
        y = _fox_fwd(proj, decq, deck, heads, tag + "_attn")
        saved = (h, n, win, proj, bf_row, decq, deck, y)
    out = _matmul(y, wout, name=tag + "_out", out_dtype=F32, resid=h)
    return out, saved


def _mixer_bwd(kind, dout, saved, p, tag, dep=None):
    h, n = saved[0], saved[1]
    y = saved[-1]
    s_len, d = h.shape
    wout = p["mix_wout"].reshape(d, d)
    grads = {}
    dy = _matmul(dout, wout, tb=True, name=tag + "_bout", out_dtype=BF16, dep=dep)
    grads["mix_wout"] = _matmul(y, dout, ta=True, name=tag + "_gout", out_dtype=BF16,
                                dep=dep).reshape(p["mix_wout"].shape)
    if kind == "gmlp":
        zp = saved[2]
        dzp, dws, dbst, dvg = _gmlp_bwd(zp, dy, p["gmlp_vnorm"], p["gmlp_ws"], p["gmlp_bs"], tag + "_bgate")
        grads.update(gmlp_ws=dws, gmlp_bs=dbst.T, gmlp_vnorm=dvg.reshape(d))
        dn = _matmul(dzp, p["mix_win"], tb=True, name=tag + "_bin", out_dtype=BF16)
        grads["mix_win"] = _matmul(n, dzp, ta=True, name=tag + "_gin", out_dtype=BF16, out_shards=N_CHIPS)
    elif kind == "swa":
        q, k, v = saved[2:5]
        dq, dk, dv, dsinks = _swa_bwd(q, k, v, p["swa_sinks"], dy, tag + "_battn")
        grads["swa_sinks"] = dsinks[0, :p["swa_sinks"].shape[0]]
        dqkv = _rope_bwd(dq, dk, dv, tag + "_brope")
        dn = _matmul(dqkv, p["mix_win"], tb=True, name=tag + "_bin", out_dtype=BF16)
        grads["mix_win"] = _matmul(n, dqkv, ta=True, name=tag + "_gin", out_dtype=BF16, out_shards=N_CHIPS)
    else:
        win, proj, bf_row, decq, deck = saved[2:7]
        heads = d // FOX_HEAD_DIM
        dq, dk, dv, ddq, ddk = _fox_bwd(proj, decq, deck, dy, heads, tag + "_battn")
        widen = lambda t: jnp.pad(t.reshape(heads, s_len).T, ((0, 0), (0, LANES - heads)))
        dfl, dbf = _fox_decay_bwd(widen(ddq), widen(ddk), proj, bf_row, 3 * heads, heads, tag + "_bdecay")
        grads["fox_bf"] = dbf[0, :heads]
        dproj = jnp.concatenate([dq, dk.astype(BF16), dv.astype(BF16), dfl], axis=1)
        dn = _matmul(dproj, win, tb=True, name=tag + "_bin", out_dtype=BF16)
        dwin = _matmul(n, dproj, ta=True, name=tag + "_gin", out_dtype=BF16)
        ns, rows, csh = p["mix_win"].shape
        grads["mix_win"] = jnp.transpose(dwin[:, :ns * csh].reshape(rows, ns, csh), (1, 0, 2))
    dh, dgain = _norm_bwd(h, p["mix_norm"], dn, dout, tag + "_bnorm")
    grads["mix_norm"] = dgain.reshape(d)
    return dh, grads


def _pack_small(arrays):
    flat = jnp.concatenate([a.reshape(-1).astype(F32) for a in arrays])
    pad = (-flat.shape[0]) % (512 * LANES)
    return jnp.pad(flat, (0, pad)).reshape(-1, LANES)


def _unpack_small(packed, like):
    flat, out, pos = packed.reshape(-1), [], 0
    for a in like:
        out.append(flat[pos:pos + a.size].reshape(a.shape))
        pos += a.size
    return out


def _step(inp):
    x, target = inp["x"][0], inp["loss_target"][0]
    d = x.shape[1]

    core = lax.axis_index("c").astype(jnp.int32).reshape(1)
    chip = (2 * lax.axis_index("x") + lax.axis_index("y")).astype(jnp.int32).reshape(1)
    place = jnp.concatenate([core, chip])

    groups = []
    for i in range(len(MIXERS)):
        groups += [(i, "ffn1", [f"l{i}_ffn1_wi", f"l{i}_ffn1_wo"]), (i, "mix", [f"l{i}_mix_win", f"l{i}_mix_wout"]),
                   (i, "ffn2", [f"l{i}_ffn2_wi", f"l{i}_ffn2_wo"])]

    def layer_params(i, full):
        p = {nm[len(f"l{i}_"):]: inp[nm] for nm in WEIGHTS if nm.startswith(f"l{i}_")}
        p.update({nm[len(f"l{i}_"):]: w for nm, w in full.items()})
        return p

    n_groups = len(groups)
    valid = lambda k: 0 <= k < n_groups

    bufs = {}
    full_shapes = lambda names: [(N_CHIPS, *inp[nm].shape) for nm in names]
    direct = [_gather_direct(names, full_shapes(names)) for _, _, names in groups]
    relay = [_gather_relay(names, full_shapes(names)) for _, _, names in groups]
    to_pair = [_gather_pair(names, full_shapes(names)) for _, _, names in groups]
    sems = {}

    def cast_groups(which, dep):
        for g in which:
            for nm in groups[g][2]:
                bufs[nm] = _cast_into_slot(inp[nm], chip, nm + "_cast", dep=dep)

    def gather_step(step, after):
        waits, starts, tags, keys = [], [], [], []
        for kind, transfers, g, begin in (("pair", to_pair, step, False), ("relay", relay, step + 1, False),
                                          ("pair", to_pair, step + 1, True), ("direct", direct, step + 2, False),
                                          ("relay", relay, step + 2, True), ("direct", direct, step + 3, True)):
            if not valid(g):
                continue
            keys += [nm for nm in groups[g][2] if nm not in keys]
            if begin:
                starts.append(transfers[g])
                tags.append((kind, g))
            else:
                waits.append((transfers[g], *sems.pop((kind, g))))
        new, started, token = _comm_call(f"gather_step{step + 3}", {k: bufs[k] for k in keys}, waits, starts, after)
        bufs.update(new)
        sems.update(zip(tags, started))
        return token

    cast_groups([0], None)
    token = gather_step(-3, x)
    cast_groups(range(1, 6), token)
    token = gather_step(-2, bufs[groups[5][2][-1]])
    cast_groups(range(6, n_groups), token)
    token = gather_step(-1, bufs[groups[-1][2][-1]])
    h, saved, fulls = x, [], []
    for g, (i, part, names) in enumerate(groups):
        token = gather_step(g, h)
        full = {nm: bufs[nm] for nm in names}
        p = layer_params(i, full)
        if part == "mix":
            h, s = _mixer_fwd(MIXERS[i], h, p, f"l{i}_mix", dep=token)
        else:
            h, s = _ffn_fwd(h, p[part + "_norm"], p[part + "_wi"], p[part + "_wo"], f"l{i}_{part}", dep=token)
        saved.append(s)
        fulls.append(full)
    loss_part, dh, dfinal = _loss_head(h, inp["final_norm"], target, "loss_head")
    loss = lax.psum(loss_part, ("x", "y", "c"))

    small_grads = {"final_norm": dfinal.reshape(d)}
    outs = {}
    work = {}
    stage = {}
    small_names = [nm for nm in WEIGHTS if nm.split("_", 1)[1] not in BIG]
    last_small = "l0_ffn1_norm"
    small_sets = {}

    def small_start(tag, names, after):
        part = _pack_small([small_grads[nm] for nm in names])
        device = 4 * lax.axis_index("x") + 2 * lax.axis_index("y") + lax.axis_index("c")
        work[tag + "#part"] = part
        work[tag + "#all"] = lax.dynamic_update_slice(jnp.zeros((N_DEV, *part.shape), F32), part[None],
                                                      (device, 0, 0))
        stage[(tag, 0)] = _device_gather(tag + "#part", tag + "#all", part.shape[0])
        stage_keys[(tag, 0)] = [tag + "#part", tag + "#all"]
        small_sets[tag] = names
        return comm(f"small_{tag}_start", [], [(tag, 0)], after)

    def small_finish(tag, after):
        names = small_sets[tag]
        comm(f"small_{tag}_wait", [(tag, 0)], [], after)
        total = _sum_devices(work[tag + "#all"], f"small_{tag}_sum")
        like = [inp[nm] for nm in names]
        upd = _adamw(_pack_small(like), total, _pack_small([inp["m_" + nm] for nm in names]),
                     _pack_small([inp["v_" + nm] for nm in names]), f"small_{tag}_adamw")
        unpacked = [_unpack_small(t, like) for t in (total, *upd)]
        for k, nm in enumerate(names):
            outs[nm] = tuple(u[k] for u in unpacked)
        return upd[0]

    def comm(name, transfers_to_wait, transfers_to_start, after):
        waits = [(stage[k], *sems.pop(k)) for k in transfers_to_wait if valid(k[1])]
        starts = [k for k in transfers_to_start if valid(k[1])]
        if not waits and not starts:
            return after
        keys = []
        for k in [k for k in transfers_to_wait if valid(k[1])] + starts:
            keys += [key for key in stage_keys[k] if key not in keys]
        new, started, token = _comm_call(name, {k: work[k] for k in keys}, waits, [stage[k] for k in starts], after)
        work.update(new)
        sems.update(zip(starts, started))
        return token

    stage_keys = {}

    def reduce_step(g, after):
        token = comm(f"rs_pair_step{n_groups - 1 - g}", [("pair", g + 1)], [("pair", g)], after)
        if valid(g + 1):
            names = groups[g + 1][2]
            for nm in names:
                work[nm + "#sum"] = _add_pair(work[nm + "#grad"], work[nm + "#got"], core, nm + "_rs_add")
                work[nm + "#others"] = lax.empty((N_CHIPS - 1, *work[nm + "#sum"].shape[1:]), BF16)
            shapes = [work[nm + "#sum"].shape for nm in names]
            stage[("chips", g + 1)] = _grad_chips([nm + "#sum" for nm in names], [nm + "#others" for nm in names], shapes)
            stage_keys[("chips", g + 1)] = [nm + sfx for nm in names for sfx in ("#sum", "#others")]
        token = comm(f"rs_chips_step{n_groups - 1 - g}", [("chips", g + 2)], [("chips", g + 1)], token)
        if valid(g + 2):
            names = groups[g + 2][2]
            for nm in names:
                work[nm + "#sum_t"] = lax.empty(work[nm + "#sum"].shape[1:], BF16)
                work[nm + "#others_t"] = lax.empty(work[nm + "#others"].shape, BF16)
            sfxs = ("#sum", "#others", "#sum_t", "#others_t")
            stage[("join", g + 2)] = _grad_join(*[[nm + sfx for nm in names] for sfx in sfxs],
                                                [work[nm + "#sum"].shape for nm in names])
            stage_keys[("join", g + 2)] = [nm + sfx for nm in names for sfx in sfxs]
        token = comm(f"rs_join_step{n_groups - 1 - g}", [("join", g + 3)], [("join", g + 2)], token)
        if valid(g + 3):
            for nm in groups[g + 3][2]:
                outs[nm] = tuple(_adamw_halves(inp[nm], work[nm + "#sum"], work[nm + "#others"], work[nm + "#sum_t"],
                                               work[nm + "#others_t"], inp["m_" + nm], inp["v_" + nm], place,
                                               nm + "_adamw"))
        return token

    def reduce_from(g, names, grads, after):
        for nm, gr in zip(names, grads):
            work[nm + "#grad"] = gr
            work[nm + "#got"] = lax.empty((gr.shape[0], gr.shape[1] // 2, gr.shape[2]), BF16)
        stage[("pair", g)] = _grad_pair([nm + "#grad" for nm in names], [nm + "#got" for nm in names],
                                        [gr.shape for gr in grads])
        stage_keys[("pair", g)] = [nm + sfx for nm in names for sfx in ("#grad", "#got")]
        return reduce_step(g, after)

    dep = None
    for g in reversed(range(n_groups)):
        i, part, names = groups[g]
        p = layer_params(i, fulls[g])
        if part == "mix":
            dh, mg = _mixer_bwd(MIXERS[i], dh, saved[g], p, f"l{i}_mix", dep=dep)
            grads = [mg.pop("mix_win"), mg.pop("mix_wout")]
            small_grads.update({f"l{i}_{key}": val for key, val in mg.items()})
            dep = reduce_from(g, names, grads, dh)
        elif g > 0:
            dh, g_norm, dwi, dwo = _ffn_bwd(dh, saved[g], p[part + "_norm"], p[part + "_wi"], p[part + "_wo"],
                                            f"l{i}_{part}", dep=dep)
            small_grads[f"l{i}_{part}_norm"] = g_norm.reshape(d)
            dep = reduce_from(g, names, [dwi, dwo], dh)
        else:
            early = lambda dwi, dwo, names=names, dep=dep: reduce_step(-1, reduce_from(0, names, [dwi, dwo], dep))
            dh, g_norm, _, _ = _ffn_bwd(dh, saved[g], p[part + "_norm"], p[part + "_wi"], p[part + "_wo"],
                                        f"l{i}_{part}", dep=dep, grads_ready=early)
            small_grads[f"l{i}_{part}_norm"] = g_norm.reshape(d)
        if g == 1:
            dep = small_start("early", [nm for nm in small_names if nm != last_small], dep)
    dep = small_start("late", [last_small], dh)
    dep = small_finish("early", dep)
    dep = reduce_step(-2, dep)
    dep = small_finish("late", dep)
    reduce_step(-3, dep)

    result = [loss, dh[None]]
    for part in range(4):
        result += [outs[nm][part] for nm in WEIGHTS]
    return tuple(result)


def kernel(x, l0_ffn1_norm, l0_ffn1_wi, l0_ffn1_wo, l0_mix_norm, l0_mix_win, l0_gmlp_vnorm, l0_gmlp_ws, l0_gmlp_bs, l0_mix_wout, l0_ffn2_norm, l0_ffn2_wi, l0_ffn2_wo, l1_ffn1_norm, l1_ffn1_wi, l1_ffn1_wo, l1_mix_norm, l1_mix_win, l1_swa_sinks, l1_mix_wout, l1_ffn2_norm, l1_ffn2_wi, l1_ffn2_wo, l2_ffn1_norm, l2_ffn1_wi, l2_ffn1_wo, l2_mix_norm, l2_mix_win, l2_fox_bf, l2_mix_wout, l2_ffn2_norm, l2_ffn2_wi, l2_ffn2_wo, l3_ffn1_norm, l3_ffn1_wi, l3_ffn1_wo, l3_mix_norm, l3_mix_win, l3_gmlp_vnorm, l3_gmlp_ws, l3_gmlp_bs, l3_mix_wout, l3_ffn2_norm, l3_ffn2_wi, l3_ffn2_wo, final_norm, loss_target, m_l0_ffn1_norm, m_l0_ffn1_wi, m_l0_ffn1_wo, m_l0_mix_norm, m_l0_mix_win, m_l0_gmlp_vnorm, m_l0_gmlp_ws, m_l0_gmlp_bs, m_l0_mix_wout, m_l0_ffn2_norm, m_l0_ffn2_wi, m_l0_ffn2_wo, m_l1_ffn1_norm, m_l1_ffn1_wi, m_l1_ffn1_wo, m_l1_mix_norm, m_l1_mix_win, m_l1_swa_sinks, m_l1_mix_wout, m_l1_ffn2_norm, m_l1_ffn2_wi, m_l1_ffn2_wo, m_l2_ffn1_norm, m_l2_ffn1_wi, m_l2_ffn1_wo, m_l2_mix_norm, m_l2_mix_win, m_l2_fox_bf, m_l2_mix_wout, m_l2_ffn2_norm, m_l2_ffn2_wi, m_l2_ffn2_wo, m_l3_ffn1_norm, m_l3_ffn1_wi, m_l3_ffn1_wo, m_l3_mix_norm, m_l3_mix_win, m_l3_gmlp_vnorm, m_l3_gmlp_ws, m_l3_gmlp_bs, m_l3_mix_wout, m_l3_ffn2_norm, m_l3_ffn2_wi, m_l3_ffn2_wo, m_final_norm, v_l0_ffn1_norm, v_l0_ffn1_wi, v_l0_ffn1_wo, v_l0_mix_norm, v_l0_mix_win, v_l0_gmlp_vnorm, v_l0_gmlp_ws, v_l0_gmlp_bs, v_l0_mix_wout, v_l0_ffn2_norm, v_l0_ffn2_wi, v_l0_ffn2_wo, v_l1_ffn1_norm, v_l1_ffn1_wi, v_l1_ffn1_wo, v_l1_mix_norm, v_l1_mix_win, v_l1_swa_sinks, v_l1_mix_wout, v_l1_ffn2_norm, v_l1_ffn2_wi, v_l1_ffn2_wo, v_l2_ffn1_norm, v_l2_ffn1_wi, v_l2_ffn1_wo, v_l2_mix_norm, v_l2_mix_win, v_l2_fox_bf, v_l2_mix_wout, v_l2_ffn2_norm, v_l2_ffn2_wi, v_l2_ffn2_wo, v_l3_ffn1_norm, v_l3_ffn1_wi, v_l3_ffn1_wo, v_l3_mix_norm, v_l3_mix_win, v_l3_gmlp_vnorm, v_l3_gmlp_ws, v_l3_gmlp_bs, v_l3_mix_wout, v_l3_ffn2_norm, v_l3_ffn2_wi, v_l3_ffn2_wo, v_final_norm):
    return _step(dict(locals()))
```

```python
import functools
import math

import jax
import jax.numpy as jnp
from jax import lax
from jax.experimental import pallas as pl
from jax.experimental.pallas import tpu as pltpu

F32 = jnp.float32
BF16 = jnp.bfloat16

NORM_EPS = 1e-5
NEG_INF = -1e30
BLOCK = 128
GMLP_GROUPS = 16
SWA_HEAD_DIM = 64
SWA_GROUP = 8
ROPE_DIM = SWA_HEAD_DIM // 4
ROPE_THETA = 500000.0
FOX_HEAD_DIM = 128
ADAM_LR = 0.001
ADAM_B1 = 0.9
ADAM_B2 = 0.999
ADAM_EPS = 1e-08
ADAM_WD = 0.01
ADAM_STEP = 10
N_CHIPS = 4
N_DEV = 8
LANES = 128
VMEM_LIMIT = 56 * 1024 * 1024
MESH = pl.DeviceIdType.MESH
HBM = pl.BlockSpec(memory_space=pltpu.HBM)
SEM = pl.BlockSpec(memory_space=pltpu.SEMAPHORE)
EFFECT = pltpu.SideEffectType.DATAFLOW_SIDE_EFFECTING

MM_TILES = (1024, 1408, 896, 640, 512, 384, 256, 128)
K_TILES = (2816,) + MM_TILES


def _pick(n, prefs):
    for p in prefs:
        if p <= n and n % p == 0:
            return p
    return n


def _params(*sem):
    return pltpu.CompilerParams(dimension_semantics=sem or None, vmem_limit_bytes=VMEM_LIMIT)


def _cols(arr):
    return arr.shape[-1] * (arr.shape[0] if arr.ndim == 3 else 1)


def _mat_spec(arr, rb, cb, ridx, cidx):
    if arr.ndim == 2:
        return pl.BlockSpec((rb, cb), lambda j, i, k: (ridx(j, i, k), cidx(j, i, k)))
    per = arr.shape[2] // cb
    return pl.BlockSpec((None, rb, cb),
                        lambda j, i, k: (cidx(j, i, k) // per, ridx(j, i, k), cidx(j, i, k) % per))


def _matmul(a, b, *, name, out_dtype, ta=False, tb=False, out_shards=1, scale=1.0, resid=None, dep=None):
    m_dim, k_dim = (a.shape[1], a.shape[0]) if ta else a.shape
    n_dim = b.shape[-2] if tb else _cols(b)
    assert k_dim == (_cols(b) if tb else b.shape[-2]), (a.shape, b.shape, ta, tb)
    n_unit = n_dim // out_shards
    if b.ndim == 3 and not tb:
        n_unit = math.gcd(n_unit, b.shape[2])
    k_unit = b.shape[2] if (b.ndim == 3 and tb) else k_dim
    bm = _pick(m_dim, MM_TILES)
    bn = _pick(n_unit, MM_TILES)
    bk = k_unit if k_unit <= 2048 else _pick(k_unit, K_TILES)
    nk = k_dim // bk
    i_of, j_of, k_of = (lambda j, i, k: i), (lambda j, i, k: j), (lambda j, i, k: k)
    a_spec = _mat_spec(a, bk, bm, k_of, i_of) if ta else _mat_spec(a, bm, bk, i_of, k_of)
    b_spec = _mat_spec(b, bn, bk, j_of, k_of) if tb else _mat_spec(b, bk, bn, k_of, j_of)
    out_shape = (m_dim, n_dim) if out_shards == 1 else (out_shards, m_dim, n_dim // out_shards)
    out = jax.ShapeDtypeStruct(out_shape, out_dtype)
    o_spec = _mat_spec(out, bm, bn, i_of, j_of)
    dims = (((0 if ta else 1,), (1 if tb else 0,)), ((), ()))
    operands, in_specs = [a, b], [a_spec, b_spec]
    if resid is not None:
        operands.append(resid)
        in_specs.append(_mat_spec(resid, bm, bn, i_of, j_of))
    if dep is not None:
        operands.append(dep)
        in_specs.append(pl.BlockSpec(dep.shape, lambda j, i, k: (0, 0)))
    n_in = len(operands)

    def body(*refs):
        a_ref, b_ref = refs[0], refs[1]
        r_ref = refs[2] if resid is not None else None
        o_ref = refs[n_in]
        part = lax.dot_general(a_ref[...].astype(BF16), b_ref[...].astype(BF16), dims,
                               preferred_element_type=F32)

        def finish(acc):
            val = acc * scale if scale != 1.0 else acc
            if r_ref is not None:
                val = r_ref[...] + val
            o_ref[...] = val.astype(o_ref.dtype)

        if nk == 1:
            finish(part)
        else:
            acc_ref = refs[-1]
            k = pl.program_id(2)

            @pl.when(k == 0)
            def _():
                acc_ref[...] = part

            @pl.when(k > 0)
            def _():
                acc_ref[...] += part

            @pl.when(k == nk - 1)
            def _():
                finish(acc_ref[...])

    return pl.pallas_call(
        body, name=name, grid=(n_dim // bn, m_dim // bm, nk),
        in_specs=in_specs, out_specs=o_spec, out_shape=out,
        scratch_shapes=[pltpu.VMEM((bm, bn), F32)] if nk > 1 else [],
        compiler_params=_params("parallel", "parallel", "arbitrary"),
    )(*operands)


def _row_block(rows, width, itemsize=4, budget=2 << 20):
    best = None
    for br in range(16, rows + 1, 16):
        if rows % br == 0 and br * width * itemsize <= budget:
            best = br
    return best or rows


def _rms_fwd(h, g, name, dep=None):
    s_len, d = h.shape
    br = _row_block(s_len, d)

    def body(h_ref, g_ref, *rest):
        o_ref = rest[-1]
        x = h_ref[...]
        r = lax.rsqrt(jnp.mean(x * x, axis=-1, keepdims=True) + NORM_EPS)
        o_ref[...] = (x * r * g_ref[...]).astype(BF16)

    spec = pl.BlockSpec((br, d), lambda i: (i, 0))
    operands = [h, g.reshape(1, d)] + ([dep] if dep is not None else [])
    in_specs = [spec, pl.BlockSpec((1, d), lambda i: (0, 0))]
    if dep is not None:
        in_specs.append(pl.BlockSpec(dep.shape, lambda i: (0, 0)))
    return pl.pallas_call(body, name=name, grid=(s_len // br,), in_specs=in_specs, out_specs=spec,
                          out_shape=jax.ShapeDtypeStruct((s_len, d), BF16),
                          compiler_params=_params("parallel"))(*operands)


def _rms_bwd_rows(x, g, dn):
    r = lax.rsqrt(jnp.mean(x * x, axis=-1, keepdims=True) + NORM_EPS)
    xhat = x * r
    gdn = dn * g
    dx = r * (gdn - xhat * jnp.mean(gdn * xhat, axis=-1, keepdims=True))
    return dx, dn * xhat


def _norm_bwd(h, g, dn, dres, name):
    s_len, d = h.shape
    br = _row_block(s_len, d, budget=1 << 20)

    def body(h_ref, g_ref, dn_ref, dres_ref, dh_ref, dg_ref):
        dx, dg_rows = _rms_bwd_rows(h_ref[...], g_ref[...], dn_ref[...].astype(F32))
        dh_ref[...] = dres_ref[...] + dx

        @pl.when(pl.program_id(0) == 0)
        def _():
            dg_ref[...] = jnp.zeros_like(dg_ref)

        dg_ref[...] += jnp.sum(dg_rows, axis=0, keepdims=True)

    spec = pl.BlockSpec((br, d), lambda i: (i, 0))
    vec = pl.BlockSpec((1, d), lambda i: (0, 0))
    return pl.pallas_call(body, name=name, grid=(s_len // br,),
                          in_specs=[spec, vec, spec, spec], out_specs=[spec, vec],
                          out_shape=[jax.ShapeDtypeStruct((s_len, d), F32),
                                     jax.ShapeDtypeStruct((1, d), F32)],
                          compiler_params=_params("arbitrary"))(h, g.reshape(1, d), dn, dres)


def _sigmoid(x):
    return 0.5 * (1.0 + jnp.tanh(0.5 * x))


def _swiglu_fwd(z, name):
    s_len, f2 = z.shape
    f = f2 // 2
    br = _row_block(s_len, f2, itemsize=2, budget=6 << 20)
    fc = _pick(f, MM_TILES)

    nc = f // fc

    def body(*refs):
        a_ref = refs[-1]
        for k in range(nc):
            gate = refs[k][...].astype(F32)
            up = refs[nc + k][...].astype(F32)
            a_ref[:, k * fc:(k + 1) * fc] = (gate * _sigmoid(gate) * up).astype(BF16)

    chunk = lambda k: pl.BlockSpec((br, fc), lambda i, k=k: (i, k))
    return pl.pallas_call(body, name=name, grid=(s_len // br,),
                          in_specs=[chunk(k) for k in range(2 * nc)],
                          out_specs=pl.BlockSpec((br, f), lambda i: (i, 0)),
                          out_shape=jax.ShapeDtypeStruct((s_len, f), BF16),
                          compiler_params=_params("parallel"))(*([z] * (2 * nc)))


def _swiglu_bwd(z, da, name):
    s_len, f2 = z.shape
    f = f2 // 2
    br = _row_block(s_len, f2, itemsize=2, budget=6 << 20)
    fc = _pick(f, MM_TILES)

    nc = f // fc

    def body(*refs):
        dz_ref = refs[-1]
        for k in range(nc):
            c0 = k * fc
            gate = refs[k][...].astype(F32)
            up = refs[nc + k][...].astype(F32)
            d = refs[2 * nc + k][...].astype(F32)
            sig = _sigmoid(gate)
            dz_ref[:, c0:c0 + fc] = (d * up * (sig * (1.0 + gate * (1.0 - sig)))).astype(BF16)
            dz_ref[:, f + c0:f + c0 + fc] = (d * gate * sig).astype(BF16)

    chunk = lambda k: pl.BlockSpec((br, fc), lambda i, k=k: (i, k))
    return pl.pallas_call(body, name=name, grid=(s_len // br,),
                          in_specs=[chunk(k) for k in range(2 * nc)] + [chunk(k) for k in range(nc)],
                          out_specs=pl.BlockSpec((br, f2), lambda i: (i, 0)),
                          out_shape=jax.ShapeDtypeStruct((s_len, f2), BF16),
                          compiler_params=_params("parallel"))(*([z] * (2 * nc) + [da] * nc))


def _loss_head(h, g, target, name):
    s_len, d = h.shape
    br = _row_block(s_len, d, budget=1 << 20)

    def body(h_ref, g_ref, t_ref, loss_ref, dh_ref, dg_ref):
        x = h_ref[...]
        gain = g_ref[...]
        r = lax.rsqrt(jnp.mean(x * x, axis=-1, keepdims=True) + NORM_EPS)
        err = x * r * gain - t_ref[...]
        part = 0.5 * jnp.sum(jnp.mean(err * err, axis=-1, keepdims=True), axis=0, keepdims=True)
        dx, dg_rows = _rms_bwd_rows(x, gain, err * (1.0 / d))
        dh_ref[...] = dx

        @pl.when(pl.program_id(0) == 0)
        def _():
            dg_ref[...] = jnp.zeros_like(dg_ref)
            loss_ref[...] = jnp.zeros_like(loss_ref)

        dg_ref[...] += jnp.sum(dg_rows, axis=0, keepdims=True)
        loss_ref[...] += jnp.broadcast_to(part, loss_ref.shape)

    spec = pl.BlockSpec((br, d), lambda i: (i, 0))
    vec = pl.BlockSpec((1, d), lambda i: (0, 0))
    one = pl.BlockSpec((1, LANES), lambda i: (0, 0))
    loss, dh, dg = pl.pallas_call(
        body, name=name, grid=(s_len // br,), in_specs=[spec, vec, spec],
        out_specs=[one, spec, vec],
        out_shape=[jax.ShapeDtypeStruct((1, LANES), F32), jax.ShapeDtypeStruct((s_len, d), F32),
                   jax.ShapeDtypeStruct((1, d), F32)],
        compiler_params=_params("arbitrary"))(h, g.reshape(1, d), target)
    return loss[0, 0], dh, dg


def _adamw(w, g, m, v, name):
    rows, width = w.shape
    br = _row_block(rows, width, budget=1 << 20)
    c1 = 1.0 - ADAM_B1 ** ADAM_STEP
    c2 = 1.0 - ADAM_B2 ** ADAM_STEP

    def body(w_ref, g_ref, m_ref, v_ref, d_ref, nm_ref, nv_ref):
        grad = g_ref[...]
        new_m = ADAM_B1 * m_ref[...] + (1.0 - ADAM_B1) * grad
        new_v = ADAM_B2 * v_ref[...] + (1.0 - ADAM_B2) * (grad * grad)
        d_ref[...] = -ADAM_LR * ((new_m / c1) / (jnp.sqrt(new_v / c2) + ADAM_EPS) + ADAM_WD * w_ref[...])
        nm_ref[...] = new_m
        nv_ref[...] = new_v

    spec = pl.BlockSpec((br, width), lambda i: (i, 0))
    shp = jax.ShapeDtypeStruct(w.shape, F32)
    return pl.pallas_call(body, name=name, grid=(rows // br,), in_specs=[spec] * 4,
                          out_specs=[spec] * 3, out_shape=[shp] * 3,
                          compiler_params=_params("parallel"))(w, g, m, v)


def _gelu(x):
    return 0.5 * x * (1.0 + lax.erf(x * (2.0 ** -0.5)))


def _gelu_grad(x):
    return 0.5 * (1.0 + lax.erf(x * (2.0 ** -0.5))) + x * jnp.exp(-0.5 * x * x) * ((2.0 * math.pi) ** -0.5)


def _tril_mask():
    row = lax.broadcasted_iota(jnp.int32, (BLOCK, BLOCK), 0)
    col = lax.broadcasted_iota(jnp.int32, (BLOCK, BLOCK), 1)
    return col <= row


def _gmlp_specs(s_len, d):
    gw = d // GMLP_GROUPS
    zp = pl.BlockSpec((BLOCK, 2 * d), lambda i: (i, 0))
    row = pl.BlockSpec((BLOCK, d), lambda i: (i, 0))
    vec = pl.BlockSpec((1, d), lambda i: (0, 0))
    ws = pl.BlockSpec((GMLP_GROUPS, BLOCK, BLOCK), lambda i: (0, 0, 0))
    bst = pl.BlockSpec((BLOCK, GMLP_GROUPS), lambda i: (0, 0))
    return gw, zp, row, vec, ws, bst


def _gmlp_fwd(zp, vgain, ws, bs, name):
    s_len, d2 = zp.shape
    d = d2 // 2
    gw, zp_spec, row_spec, vec_spec, ws_spec, bst_spec = _gmlp_specs(s_len, d)

    def body(zp_ref, vg_ref, ws_ref, bst_ref, y_ref):
        u = _gelu(zp_ref[:, :d].astype(F32))
        vv = _gelu(zp_ref[:, d:].astype(F32))
        r = lax.rsqrt(jnp.mean(vv * vv, axis=-1, keepdims=True) + NORM_EPS)
        vn = (vv * r * vg_ref[...]).astype(BF16)
        mask = _tril_mask()
        for g in range(GMLP_GROUPS):
            cols = slice(g * gw, (g + 1) * gw)
            wg = jnp.where(mask, ws_ref[g], 0.0).astype(BF16)
            mixed = jnp.dot(wg, vn[:, cols], preferred_element_type=F32) + bst_ref[:, g:g + 1]
            y_ref[:, cols] = (u[:, cols] * mixed).astype(BF16)

    return pl.pallas_call(body, name=name, grid=(s_len // BLOCK,),
                          in_specs=[zp_spec, vec_spec, ws_spec, bst_spec], out_specs=row_spec,
                          out_shape=jax.ShapeDtypeStruct((s_len, d), BF16),
                          compiler_params=_params("parallel"))(zp, vgain.reshape(1, d), ws, bs.T)


def _gmlp_bwd(zp, dy, vgain, ws, bs, name):
    s_len, d2 = zp.shape
    d = d2 // 2
    gw, zp_spec, row_spec, vec_spec, ws_spec, bst_spec = _gmlp_specs(s_len, d)

    def body(zp_ref, dy_ref, vg_ref, ws_ref, bst_ref, dzp_ref, dws_ref, dbst_ref, dvg_ref, dvn_ref):
        @pl.when(pl.program_id(0) == 0)
        def _():
            dws_ref[...] = jnp.zeros_like(dws_ref)
            dbst_ref[...] = jnp.zeros_like(dbst_ref)
            dvg_ref[...] = jnp.zeros_like(dvg_ref)

        zu = zp_ref[:, :d].astype(F32)
        zv = zp_ref[:, d:].astype(F32)
        u = _gelu(zu)
        vv = _gelu(zv)
        r = lax.rsqrt(jnp.mean(vv * vv, axis=-1, keepdims=True) + NORM_EPS)
        vhat = vv * r
        gain = vg_ref[...]
        vn = (vhat * gain).astype(BF16)
        dyf = dy_ref[...].astype(F32)
        dmixed = dyf * u
        dmixed_b = dmixed.astype(BF16)
        mask = _tril_mask()
        lane = lax.broadcasted_iota(jnp.int32, (BLOCK, GMLP_GROUPS), 1)
        dbs_step = jnp.zeros((BLOCK, GMLP_GROUPS), F32)
        for g in range(GMLP_GROUPS):
            cols = slice(g * gw, (g + 1) * gw)
            wg = jnp.where(mask, ws_ref[g], 0.0).astype(BF16)
            mixed = jnp.dot(wg, vn[:, cols], preferred_element_type=F32) + bst_ref[:, g:g + 1]
            dzp_ref[:, cols] = (dyf[:, cols] * mixed * _gelu_grad(zu[:, cols])).astype(BF16)
            dm = dmixed_b[:, cols]
            dw = lax.dot_general(dm, vn[:, cols], (((1,), (1,)), ((), ())), preferred_element_type=F32)
            dws_ref[g] += jnp.where(mask, dw, 0.0)
            dbs_step = dbs_step + jnp.where(lane == g, jnp.sum(dmixed[:, cols], axis=-1, keepdims=True), 0.0)
            dvn_ref[:, cols] = lax.dot_general(wg, dm, (((0,), (0,)), ((), ())), preferred_element_type=F32)
        dbst_ref[...] += dbs_step
        dvn = dvn_ref[...]
        dvg_ref[...] += jnp.sum(dvn * vhat, axis=0, keepdims=True)
        dvhat = dvn * gain
        dvv = r * (dvhat - vhat * jnp.mean(dvhat * vhat, axis=-1, keepdims=True))
        dzp_ref[:, d:] = (dvv * _gelu_grad(zv)).astype(BF16)

    return pl.pallas_call(
        body, name=name, grid=(s_len // BLOCK,),
        in_specs=[zp_spec, row_spec, vec_spec, ws_spec, bst_spec],
        out_specs=[zp_spec, ws_spec, bst_spec, vec_spec],
        out_shape=[jax.ShapeDtypeStruct((s_len, d2), BF16), jax.ShapeDtypeStruct(ws.shape, F32),
                   jax.ShapeDtypeStruct((BLOCK, GMLP_GROUPS), F32), jax.ShapeDtypeStruct((1, d), F32)],
        scratch_shapes=[pltpu.VMEM((BLOCK, d), F32)],
        compiler_params=_params("arbitrary"))(zp, dy, vgain.reshape(1, d), ws, bs.T)


def _rope_tables(s_len, sign):
    half = ROPE_DIM // 2
    inv_freq = ROPE_THETA ** (-(jnp.arange(half, dtype=F32) * 2.0 / ROPE_DIM))
    ang = jnp.arange(s_len, dtype=F32)[:, None] * inv_freq[None, :]
    cos, sin = jnp.cos(ang), jnp.sin(ang) * sign
    pad = jnp.zeros((s_len, SWA_HEAD_DIM - ROPE_DIM), F32)
    zero = jnp.zeros_like(sin)
    cos_t = jnp.concatenate([cos, cos, pad + 1.0], axis=1)
    sin_up = jnp.concatenate([-sin, zero, pad], axis=1)
    sin_dn = jnp.concatenate([zero, sin, pad], axis=1)
    return [jnp.tile(t, (1, LANES // SWA_HEAD_DIM)) for t in (cos_t, sin_up, sin_dn)]


def _rotate(x, cos_t, sin_up, sin_dn):
    width = x.shape[-1]
    half = ROPE_DIM // 2
    reps = width // cos_t.shape[-1]
    if reps > 1:
        cos_t, sin_up, sin_dn = (jnp.tile(t, (1, reps)) for t in (cos_t, sin_up, sin_dn))
    elif reps == 0:
        cos_t, sin_up, sin_dn = (t[:, :width] for t in (cos_t, sin_up, sin_dn))
    return x * cos_t + pltpu.roll(x, width - half, 1) * sin_up + pltpu.roll(x, half, 1) * sin_dn


def _rope_fwd(qkv, name):
    s_len, total = qkv.shape
    wkv = total // (SWA_GROUP + 2)
    wq = SWA_GROUP * wkv
    br = _row_block(s_len, total, budget=2 << 20)
    tables = _rope_tables(s_len, 1.0)

    def body(q_ref, k_ref, v_ref, c_ref, su_ref, sd_ref, qo_ref, ko_ref, vo_ref):
        t = (c_ref[...], su_ref[...], sd_ref[...])
        qo_ref[...] = _rotate(q_ref[...], *t).astype(BF16)
        ko_ref[...] = _rotate(k_ref[...], *t).astype(BF16)
        vo_ref[...] = v_ref[...].astype(BF16)

    qs = pl.BlockSpec((br, wq), lambda i: (i, 0))
    ks = pl.BlockSpec((br, wkv), lambda i: (i, SWA_GROUP))
    vs = pl.BlockSpec((br, wkv), lambda i: (i, SWA_GROUP + 1))
    ts = pl.BlockSpec((br, LANES), lambda i: (i, 0))
    kv_out = pl.BlockSpec((br, wkv), lambda i: (i, 0))
    return pl.pallas_call(
        body, name=name, grid=(s_len // br,), in_specs=[qs, ks, vs, ts, ts, ts],
        out_specs=[qs, kv_out, kv_out],
        out_shape=[jax.ShapeDtypeStruct((s_len, wq), BF16), jax.ShapeDtypeStruct((s_len, wkv), BF16),
                   jax.ShapeDtypeStruct((s_len, wkv), BF16)],
        compiler_params=_params("parallel"))(qkv, qkv, qkv, *tables)


def _rope_bwd(dq, dk, dv, name):
    s_len, wq = dq.shape
    wkv = dk.shape[1]
    br = _row_block(s_len, wq + 2 * wkv, budget=2 << 20)
    tables = _rope_tables(s_len, -1.0)

    def body(q_ref, k_ref, v_ref, c_ref, su_ref, sd_ref, o_ref):
        t = (c_ref[...], su_ref[...], sd_ref[...])
        o_ref[:, :wq] = _rotate(q_ref[...], *t).astype(BF16)
        o_ref[:, wq:wq + wkv] = _rotate(k_ref[...], *t).astype(BF16)
        o_ref[:, wq + wkv:] = v_ref[...].astype(BF16)

    qs = pl.BlockSpec((br, wq), lambda i: (i, 0))
    kvs = pl.BlockSpec((br, wkv), lambda i: (i, 0))
    ts = pl.BlockSpec((br, LANES), lambda i: (i, 0))
    return pl.pallas_call(
        body, name=name, grid=(s_len // br,), in_specs=[qs, kvs, kvs, ts, ts, ts],
        out_specs=pl.BlockSpec((br, wq + 2 * wkv), lambda i: (i, 0)),
        out_shape=jax.ShapeDtypeStruct((s_len, wq + 2 * wkv), BF16),
        compiler_params=_params("parallel"))(dq, dk, dv, *tables)


def _swa_valid(i):
    row = lax.broadcasted_iota(jnp.int32, (BLOCK, 2 * BLOCK), 0)
    col = lax.broadcasted_iota(jnp.int32, (BLOCK, 2 * BLOCK), 1)
    return (col - BLOCK <= row) & (row < col) & ((col >= BLOCK) | (i > 0))


def _swa_specs(wq, wkv):
    q_spec = pl.BlockSpec((BLOCK, wq), lambda i: (i, 0))
    cur = pl.BlockSpec((BLOCK, wkv), lambda i: (i, 0))
    prev = pl.BlockSpec((BLOCK, wkv), lambda i: (jnp.maximum(i - 1, 0), 0))
    sink = pl.BlockSpec(memory_space=pltpu.SMEM)
    return q_spec, cur, prev, sink


def _swa_probs(q_h, k_cat, valid, sink):
    s = lax.dot_general(q_h, k_cat, (((1,), (1,)), ((), ())), preferred_element_type=F32)
    s = jnp.where(valid, s * (SWA_HEAD_DIM ** -0.5), NEG_INF)
    m = jnp.maximum(jnp.max(s, axis=-1, keepdims=True), sink)
    p = jnp.exp(s - m)
    e_sink = jnp.exp(sink - m)
    denom = jnp.sum(p, axis=-1, keepdims=True) + e_sink
    return p / denom, e_sink / denom


def _swa_fwd(q, k, v, sinks, name):
    s_len, wq = q.shape
    wkv = k.shape[1]
    hd = SWA_HEAD_DIM
    q_spec, cur, prev, sink_spec = _swa_specs(wq, wkv)

    def body(q_ref, kc_ref, kp_ref, vc_ref, vp_ref, sink_ref, o_ref):
        valid = _swa_valid(pl.program_id(0))
        for j in range(wkv // hd):
            lanes = slice(j * hd, (j + 1) * hd)
            k_cat = jnp.concatenate([kp_ref[:, lanes], kc_ref[:, lanes]], axis=0)
            v_cat = jnp.concatenate([vp_ref[:, lanes], vc_ref[:, lanes]], axis=0)
            for hh in range(SWA_GROUP):
                h = j * SWA_GROUP + hh
                pn, _ = _swa_probs(q_ref[:, h * hd:(h + 1) * hd], k_cat, valid, sink_ref[h])
                o_ref[:, h * hd:(h + 1) * hd] = jnp.dot(
                    pn.astype(BF16), v_cat, preferred_element_type=F32).astype(BF16)

    return pl.pallas_call(body, name=name, grid=(s_len // BLOCK,),
                          in_specs=[q_spec, cur, prev, cur, prev, sink_spec], out_specs=q_spec,
                          out_shape=jax.ShapeDtypeStruct((s_len, wq), BF16),
                          compiler_params=_params("parallel"))(q, k, k, v, v, sinks)


def _swa_bwd(q, k, v, sinks, do, name):
    s_len, wq = q.shape
    wkv = k.shape[1]
    hd = SWA_HEAD_DIM
    q_spec, cur, prev, sink_spec = _swa_specs(wq, wkv)
    full = pl.BlockSpec((s_len, wkv), lambda i: (0, 0))
    one = pl.BlockSpec((1, LANES), lambda i: (0, 0))
    scale = hd ** -0.5

    def body(q_ref, kc_ref, kp_ref, vc_ref, vp_ref, sink_ref, do_ref, dq_ref, dk_ref, dv_ref, ds_ref):
        i = pl.program_id(0)

        @pl.when(i == 0)
        def _():
            dk_ref[...] = jnp.zeros_like(dk_ref)
            dv_ref[...] = jnp.zeros_like(dv_ref)
            ds_ref[...] = jnp.zeros_like(ds_ref)

        valid = _swa_valid(i)
        lane = lax.broadcasted_iota(jnp.int32, (1, LANES), 1)
        dsink_step = jnp.zeros((1, LANES), F32)
        rows_prev = pl.ds(pl.multiple_of(jnp.maximum(i - 1, 0) * BLOCK, BLOCK), BLOCK)
        rows_cur = pl.ds(pl.multiple_of(i * BLOCK, BLOCK), BLOCK)
        for j in range(wkv // hd):
            lanes = slice(j * hd, (j + 1) * hd)
            k_cat = jnp.concatenate([kp_ref[:, lanes], kc_ref[:, lanes]], axis=0)
            v_cat = jnp.concatenate([vp_ref[:, lanes], vc_ref[:, lanes]], axis=0)
            dk_cat = jnp.zeros((2 * BLOCK, hd), F32)
            dv_cat = jnp.zeros((2 * BLOCK, hd), F32)
            for hh in range(SWA_GROUP):
                h = j * SWA_GROUP + hh
                q_h = q_ref[:, h * hd:(h + 1) * hd]
                do_h = do_ref[:, h * hd:(h + 1) * hd]
                pn, p_sink = _swa_probs(q_h, k_cat, valid, sink_ref[h])
                dpn = lax.dot_general(do_h, v_cat, (((1,), (1,)), ((), ())), preferred_element_type=F32)
                delta = jnp.sum(dpn * pn, axis=-1, keepdims=True)
                ds = (pn * (dpn - delta) * scale).astype(BF16)
                dsink_h = -jnp.sum(p_sink * delta, axis=0, keepdims=True)
                dsink_step = dsink_step + jnp.where(lane == h, dsink_h, 0.0)
                dq_ref[:, h * hd:(h + 1) * hd] = jnp.dot(ds, k_cat, preferred_element_type=F32)
                dk_cat = dk_cat + lax.dot_general(ds, q_h, (((0,), (0,)), ((), ())),
                                                  preferred_element_type=F32)
                dv_cat = dv_cat + lax.dot_general(pn.astype(BF16), do_h, (((0,), (0,)), ((), ())),
                                                  preferred_element_type=F32)
            dk_ref[rows_prev, lanes] += dk_cat[:BLOCK]
            dk_ref[rows_cur, lanes] += dk_cat[BLOCK:]
            dv_ref[rows_prev, lanes] += dv_cat[:BLOCK]
            dv_ref[rows_cur, lanes] += dv_cat[BLOCK:]
        ds_ref[...] += dsink_step

    return pl.pallas_call(
        body, name=name, grid=(s_len // BLOCK,),
        in_specs=[q_spec, cur, prev, cur, prev, sink_spec, q_spec],
        out_specs=[q_spec, full, full, one],
        out_shape=[jax.ShapeDtypeStruct((s_len, wq), F32), jax.ShapeDtypeStruct((s_len, wkv), F32),
                   jax.ShapeDtypeStruct((s_len, wkv), F32), jax.ShapeDtypeStruct((1, LANES), F32)],
        compiler_params=_params("arbitrary"))(q, k, k, v, v, sinks, do)


def _log_sigmoid(x):
    return jnp.minimum(x, 0.0) - jnp.log(1.0 + jnp.exp(-jnp.abs(x)))


def _tri_ones(lower):
    row = lax.broadcasted_iota(jnp.int32, (BLOCK, BLOCK), 0)
    col = lax.broadcasted_iota(jnp.int32, (BLOCK, BLOCK), 1)
    return jnp.where((col <= row) if lower else (col >= row), 1.0, 0.0).astype(F32)


def _fox_decay(proj, bf_row, fl_block, name):
    s_len = proj.shape[0]
    nchunk = s_len // BLOCK

    def body(fl_ref, bf_ref, dec_ref):
        tri = _tri_ones(True)
        carry = jnp.zeros((1, LANES), F32)
        for c in range(nchunk):
            rows = slice(c * BLOCK, (c + 1) * BLOCK)
            log_f = _log_sigmoid(fl_ref[rows, :] + bf_ref[...])
            loc = jnp.dot(tri, log_f, preferred_element_type=F32, precision=lax.Precision.HIGHEST) + carry
            dec_ref[rows, :] = loc
            carry = loc[BLOCK - 1:BLOCK, :]

    return pl.pallas_call(
        body, name=name, grid=(1,),
        in_specs=[pl.BlockSpec((s_len, LANES), lambda i: (0, fl_block)),
                  pl.BlockSpec((1, LANES), lambda i: (0, 0))],
        out_specs=pl.BlockSpec((s_len, LANES), lambda i: (0, 0)),
        out_shape=jax.ShapeDtypeStruct((s_len, LANES), F32),
        compiler_params=_params("arbitrary"))(proj, bf_row)


def _fox_decay_bwd(ddq, ddk, proj, bf_row, fl_block, heads, name):
    s_len = proj.shape[0]
    nchunk = s_len // BLOCK

    def body(ddq_ref, ddk_ref, fl_ref, bf_ref, dfl_ref, dbf_ref):
        tri = _tri_ones(False)
        lane_ok = lax.broadcasted_iota(jnp.int32, (BLOCK, LANES), 1) < heads
        carry = jnp.zeros((1, LANES), F32)
        dbf = jnp.zeros((1, LANES), F32)
        for c in reversed(range(nchunk)):
            rows = slice(c * BLOCK, (c + 1) * BLOCK)
            ddec = ddq_ref[rows, :] + ddk_ref[rows, :]
            dlog = jnp.dot(tri, ddec, preferred_element_type=F32, precision=lax.Precision.HIGHEST) + carry
            carry = dlog[0:1, :]
            dfl = jnp.where(lane_ok, dlog * _sigmoid(-(fl_ref[rows, :] + bf_ref[...])), 0.0)
            dfl_ref[rows, :] = dfl.astype(BF16)
            dbf = dbf + jnp.sum(dfl, axis=0, keepdims=True)
        dbf_ref[...] = dbf

    blk = pl.BlockSpec((s_len, LANES), lambda i: (0, 0))
    one = pl.BlockSpec((1, LANES), lambda i: (0, 0))
    return pl.pallas_call(
        body, name=name, grid=(1,),
        in_specs=[blk, blk, pl.BlockSpec((s_len, LANES), lambda i: (0, fl_block)), one],
        out_specs=[blk, one],
        out_shape=[jax.ShapeDtypeStruct((s_len, LANES), BF16), jax.ShapeDtypeStruct((1, LANES), F32)],
        compiler_params=_params("arbitrary"))(ddq, ddk, proj, bf_row)


def _fox_scores(q, k, decq, deck, i, bq):
    s_len = k.shape[0]
    s = lax.dot_general(q, k, (((1,), (1,)), ((), ())), preferred_element_type=F32)
    s = s * (FOX_HEAD_DIM ** -0.5) + decq - deck
    row = lax.broadcasted_iota(jnp.int32, (bq, s_len), 0) + i * bq
    col = lax.broadcasted_iota(jnp.int32, (bq, s_len), 1)
    s = jnp.where(col <= row, s, NEG_INF)
    p = jnp.exp(s - jnp.max(s, axis=-1, keepdims=True))
    return p / jnp.sum(p, axis=-1, keepdims=True)


def _fox_key_spans(s_len, bq):
    n_span = min(4, s_len // bq)
    return [(j + 1) * (s_len // n_span) for j in range(n_span)]


def _fox_span_of(i, s_len, bq):
    span = s_len // min(4, s_len // bq)
    return ((i * bq) // span + 1) * span


def _fox_specs(s_len, heads, bq):
    hd = FOX_HEAD_DIM
    q_spec = pl.BlockSpec((bq, hd), lambda h, i: (i, h))
    k_spec = pl.BlockSpec((s_len, hd), lambda h, i: (0, heads + h))
    v_spec = pl.BlockSpec((s_len, hd), lambda h, i: (0, 2 * heads + h))
    dq_spec = pl.BlockSpec((None, bq, 1), lambda h, i: (h, i, 0))
    dk_spec = pl.BlockSpec((None, 1, s_len), lambda h, i: (h, 0, 0))
    return q_spec, k_spec, v_spec, dq_spec, dk_spec


def _fox_fwd(proj, decq, deck, heads, name):
    s_len = proj.shape[0]
    bq = _pick(s_len, (256, 128))
    q_spec, k_spec, v_spec, dq_spec, dk_spec = _fox_specs(s_len, heads, bq)

    def body(q_ref, k_ref, v_ref, decq_ref, deck_ref, o_ref):
        i = pl.program_id(1)
        for klen in _fox_key_spans(s_len, bq):
            @pl.when(_fox_span_of(i, s_len, bq) == klen)
            def _(klen=klen):
                pn = _fox_scores(q_ref[...].astype(BF16), k_ref[:klen, :].astype(BF16), decq_ref[...],
                                 deck_ref[:, :klen], i, bq)
                o_ref[...] = jnp.dot(pn.astype(BF16), v_ref[:klen, :].astype(BF16),
                                     preferred_element_type=F32).astype(BF16)

    return pl.pallas_call(body, name=name, grid=(heads, s_len // bq),
                          in_specs=[q_spec, k_spec, v_spec, dq_spec, dk_spec], out_specs=q_spec,
                          out_shape=jax.ShapeDtypeStruct((s_len, heads * FOX_HEAD_DIM), BF16),
                          compiler_params=_params("parallel", "parallel"))(proj, proj, proj, decq, deck)


def _fox_bwd(proj, decq, deck, do, heads, name):
    s_len = proj.shape[0]
    d = heads * FOX_HEAD_DIM
    bq = _pick(s_len, (256, 128))
    q_spec, k_spec, v_spec, dq_spec, dk_spec = _fox_specs(s_len, heads, bq)
    acc_spec = pl.BlockSpec((s_len, FOX_HEAD_DIM), lambda h, i: (0, h))
    scale = FOX_HEAD_DIM ** -0.5

    def body(q_ref, k_ref, v_ref, decq_ref, deck_ref, do_ref, dq_ref, dk_ref, dv_ref, ddq_ref, ddk_ref):
        i = pl.program_id(1)

        @pl.when(i == 0)
        def _():
            dk_ref[...] = jnp.zeros_like(dk_ref)
            dv_ref[...] = jnp.zeros_like(dv_ref)
            ddk_ref[...] = jnp.zeros_like(ddk_ref)

        q = q_ref[...].astype(BF16)
        do_b = do_ref[...]
        for klen in _fox_key_spans(s_len, bq):
            @pl.when(_fox_span_of(i, s_len, bq) == klen)
            def _(klen=klen):
                k = k_ref[:klen, :].astype(BF16)
                pn = _fox_scores(q, k, decq_ref[...], deck_ref[:, :klen], i, bq)
                dpn = lax.dot_general(do_b, v_ref[:klen, :].astype(BF16), (((1,), (1,)), ((), ())),
                                      preferred_element_type=F32)
                ds = pn * (dpn - jnp.sum(dpn * pn, axis=-1, keepdims=True))
                ddq_ref[...] = jnp.sum(ds, axis=-1, keepdims=True)
                ddk_ref[:, :klen] -= jnp.sum(ds, axis=0, keepdims=True)
                ds_b = (ds * scale).astype(BF16)
                dq_ref[...] = jnp.dot(ds_b, k, preferred_element_type=F32).astype(BF16)
                dk_ref[:klen, :] += lax.dot_general(ds_b, q, (((0,), (0,)), ((), ())),
                                                    preferred_element_type=F32)
                dv_ref[:klen, :] += lax.dot_general(pn.astype(BF16), do_b, (((0,), (0,)), ((), ())),
                                                    preferred_element_type=F32)

    return pl.pallas_call(
        body, name=name, grid=(heads, s_len // bq),
        in_specs=[q_spec, k_spec, v_spec, dq_spec, dk_spec, q_spec],
        out_specs=[q_spec, acc_spec, acc_spec, dq_spec, dk_spec],
        out_shape=[jax.ShapeDtypeStruct((s_len, d), BF16), jax.ShapeDtypeStruct((s_len, d), F32),
                   jax.ShapeDtypeStruct((s_len, d), F32), jax.ShapeDtypeStruct((heads, s_len, 1), F32),
                   jax.ShapeDtypeStruct((heads, 1, s_len), F32)],
        compiler_params=_params("parallel", "arbitrary"))(proj, proj, proj, decq, deck, do)


def _place():
    x, y, c = lax.axis_index("x"), lax.axis_index("y"), lax.axis_index("c")
    chips = [(1 - x, y), (x, 1 - y), (1 - x, 1 - y)]
    return x, y, c, chips


def _remote(src, dst, send_sems, recv_sems, idx, to):
    return pltpu.make_async_remote_copy(src_ref=src, dst_ref=dst, send_sem=send_sems.at[idx],
                                        recv_sem=recv_sems.at[idx], device_id=to, device_id_type=MESH)


def _row_chunks(rows, want):
    for k in (want, want // 2, want // 4):
        if k >= 1 and rows % (16 * k) == 0:
            return [(j * (rows // k), rows // k) for j in range(k)]
    return [(0, rows)]


D2D_CHUNKS = 8


def _cast_into_slot(w, me, name, dep=None):
    rows, width = w.shape
    br = _row_block(rows, width, budget=4 << 20)

    def body(me_ref, w_ref, *rest):
        rest[-1][...] = w_ref[...].astype(BF16)

    in_specs = [pl.BlockSpec((br, width), lambda i, me_ref: (i, 0))]
    if dep is not None:
        in_specs.append(pl.BlockSpec(dep.shape, lambda i, me_ref: (0, 0)))
    return pl.pallas_call(
        body, name=name,
        grid_spec=pltpu.PrefetchScalarGridSpec(
            num_scalar_prefetch=1, grid=(rows // br,), in_specs=in_specs,
            out_specs=pl.BlockSpec((None, br, width), lambda i, me_ref: (me_ref[0], i, 0))),
        out_shape=jax.ShapeDtypeStruct((N_CHIPS, rows, width), BF16),
        compiler_params=_params("parallel"))(me, w, *([dep] if dep is not None else []))


def _hbm(arr):
    return pltpu.with_memory_space_constraint(arr, pltpu.HBM)


def _token_shape():
    return jax.ShapeDtypeStruct((8, LANES), F32)


def _add_pair(grad, got, c, name):
    _, half, width = got.shape
    br = _row_block(half, width, itemsize=2, budget=3 << 20)
    nb = half // br

    nc = next((k for k in (4, 2) if width % (k * LANES) == 0), 1)
    wc = width // nc

    def body(c_ref, *refs):
        o_ref = refs[-1]
        for k in range(nc):
            o_ref[:, k * wc:(k + 1) * wc] = (refs[k][...].astype(F32) + refs[nc + k][...].astype(F32)).astype(BF16)

    spec = pl.BlockSpec((None, br, width), lambda j, i, c_ref: (j, i, 0))
    theirs = lambda k: pl.BlockSpec((None, br, wc), lambda j, i, c_ref, k=k: (j, i, k))
    mine = lambda k: pl.BlockSpec((None, br, wc), lambda j, i, c_ref, k=k: (j, c_ref[0] * nb + i, k))
    return pl.pallas_call(
        body, name=name,
        grid_spec=pltpu.PrefetchScalarGridSpec(
            num_scalar_prefetch=1, grid=(N_CHIPS, nb),
            in_specs=[mine(k) for k in range(nc)] + [theirs(k) for k in range(nc)], out_specs=spec),
        out_shape=jax.ShapeDtypeStruct(got.shape, BF16),
        compiler_params=_params("parallel", "parallel"))(c, *([grad] * nc + [got] * nc))


def _adamw_halves(w, pair, others, pair_t, others_t, m, v, place, name):
    rows, width = w.shape
    half = rows // 2
    br = _row_block(half, width, budget=3 << 19)
    nb = half // br
    c1 = 1.0 - ADAM_B1 ** ADAM_STEP
    c2 = 1.0 - ADAM_B2 ** ADAM_STEP

    def chip_sum(p, o3_ref):
        acc = p.astype(F32)
        for r in range(N_CHIPS - 1):
            acc = acc + o3_ref[r].astype(F32)
        return acc

    def body(place_ref, w_ref, p_ref, o_ref, pt_ref, ot_ref, m_ref, v_ref, g_ref, d_ref, nm_ref, nv_ref):
        grad = jnp.where(pl.program_id(0) == place_ref[0], chip_sum(p_ref[...], o_ref), chip_sum(pt_ref[...], ot_ref))
        new_m = ADAM_B1 * m_ref[...] + (1.0 - ADAM_B1) * grad
        new_v = ADAM_B2 * v_ref[...] + (1.0 - ADAM_B2) * (grad * grad)
        g_ref[...] = grad
        d_ref[...] = -ADAM_LR * ((new_m / c1) / (jnp.sqrt(new_v / c2) + ADAM_EPS) + ADAM_WD * w_ref[...])
        nm_ref[...] = new_m
        nv_ref[...] = new_v

    full = pl.BlockSpec((br, width), lambda h, i, s: (h * nb + i, 0))
    mine_i = lambda h, i, s: jnp.where(h == s[0], i, 0)
    theirs_i = lambda h, i, s: jnp.where(h == s[0], 0, i)
    specs = [full,
             pl.BlockSpec((None, br, width), lambda h, i, s: (s[1], mine_i(h, i, s), 0)),
             pl.BlockSpec((N_CHIPS - 1, br, width), lambda h, i, s: (0, mine_i(h, i, s), 0)),
             pl.BlockSpec((br, width), lambda h, i, s: (theirs_i(h, i, s), 0)),
             pl.BlockSpec((N_CHIPS - 1, br, width), lambda h, i, s: (0, theirs_i(h, i, s), 0)),
             full, full]
    shp = jax.ShapeDtypeStruct(w.shape, F32)
    return pl.pallas_call(
        body, name=name,
        grid_spec=pltpu.PrefetchScalarGridSpec(num_scalar_prefetch=1, grid=(2, nb), in_specs=specs,
                                               out_specs=[full] * 4),
        out_shape=[shp] * 4,
        compiler_params=_params("parallel", "parallel"))(place, w, pair, others, pair_t, others_t, m, v)


class _Transfer:
    def __init__(self, n_sems, build):
        self.n_sems, self.build = n_sems, build


def _copies(src_of, dst_of, land_of, rows, chunks, send, recv, idx, to):
    starts = [_remote(src_of(s, z), dst_of(s, z), send, recv, idx, to) for s, z in _row_chunks(rows, chunks)]
    return starts, _remote(src_of(0, rows), land_of(0, rows), send, recv, idx, to)


def _gather_direct(keys, shapes):
    def build(refs, send, recv):
        x, y, c, chips = _place()
        me = 2 * x + y
        out = []
        for t, key in enumerate(keys):
            half = shapes[t][1] // 2
            for r, chip in enumerate(chips[:2]):
                slot = 2 * chip[0] + chip[1]
                out.append(_copies(lambda s, z, key=key, half=half: refs[key].at[me, pl.ds(c * half + s, z)],
                                   lambda s, z, key=key, half=half: refs[key].at[me, pl.ds(c * half + s, z)],
                                   lambda s, z, key=key, half=half, slot=slot: refs[key].at[slot, pl.ds(c * half + s, z)],
                                   half, 1, send, recv, 2 * t + r, (*chip, c)))
        return out
    return _Transfer(2 * len(keys), build)


def _gather_relay(keys, shapes):
    def build(refs, send, recv):
        x, y, c, chips = _place()
        slot_x, slot_y, slot_d = (2 * ch[0] + ch[1] for ch in chips)
        src_slot = slot_y + c * (slot_x - slot_y)
        to = (x ^ (1 - c), y ^ c, c)
        out = []
        for t, key in enumerate(keys):
            half = shapes[t][1] // 2
            out.append(_copies(lambda s, z, key=key, half=half: refs[key].at[src_slot, pl.ds(c * half + s, z)],
                               lambda s, z, key=key, half=half: refs[key].at[src_slot, pl.ds(c * half + s, z)],
                               lambda s, z, key=key, half=half: refs[key].at[slot_d, pl.ds(c * half + s, z)],
                               half, 1, send, recv, t, to))
        return out
    return _Transfer(len(keys), build)


def _gather_pair(keys, shapes):
    def build(refs, send, recv):
        x, y, c, chips = _place()
        out = []
        for t, key in enumerate(keys):
            half = shapes[t][1] // 2
            for r, chip in enumerate(chips):
                slot = 2 * chip[0] + chip[1]
                mine = lambda s, z, key=key, half=half, slot=slot: refs[key].at[slot, pl.ds(c * half + s, z)]
                land = lambda s, z, key=key, half=half, slot=slot: refs[key].at[slot, pl.ds((1 - c) * half + s, z)]
                out.append(_copies(mine, mine, land, half, D2D_CHUNKS, send, recv, 3 * t + r, (x, y, 1 - c)))
        return out
    return _Transfer(3 * len(keys), build)


def _grad_pair(keys, lands, shapes):
    def build(refs, send, recv):
        x, y, c, _ = _place()
        out = []
        for t, (key, land) in enumerate(zip(keys, lands)):
            half = shapes[t][1] // 2
            for j in range(N_CHIPS):
                out.append(_copies(
                    lambda s, z, key=key, half=half, j=j: refs[key].at[j, pl.ds((1 - c) * half + s, z)],
                    lambda s, z, land=land, j=j: refs[land].at[j, pl.ds(s, z)],
                    lambda s, z, land=land, j=j: refs[land].at[j, pl.ds(s, z)],
                    half, 2, send, recv, N_CHIPS * t + j, (x, y, 1 - c)))
        return out
    return _Transfer(N_CHIPS * len(keys), build)


def _grad_chips(keys, lands, shapes):
    def build(refs, send, recv):
        x, y, c, chips = _place()
        out = []
        for t, (key, land) in enumerate(zip(keys, lands)):
            rows = shapes[t][1]
            for r, chip in enumerate(chips):
                slot = 2 * chip[0] + chip[1]
                out.append(_copies(lambda s, z, key=key, slot=slot: refs[key].at[slot, pl.ds(s, z)],
                                   lambda s, z, land=land, r=r: refs[land].at[r, pl.ds(s, z)],
                                   lambda s, z, land=land, r=r: refs[land].at[r, pl.ds(s, z)],
                                   rows, 1, send, recv, 3 * t + r, (*chip, c)))
        return out
    return _Transfer(3 * len(keys), build)


def _grad_join(pairs, others, pair_lands, other_lands, shapes):
    def build(refs, send, recv):
        x, y, c, _ = _place()
        me = 2 * x + y
        sibling = (x, y, 1 - c)
        out = []
        for t, (pair, other, pair_land, other_land) in enumerate(zip(pairs, others, pair_lands, other_lands)):
            rows = shapes[t][1]
            land = lambda s, z, k=pair_land: refs[k].at[pl.ds(s, z)]
            out.append(_copies(lambda s, z, k=pair: refs[k].at[me, pl.ds(s, z)], land, land,
                               rows, D2D_CHUNKS, send, recv, N_CHIPS * t, sibling))
            for r in range(N_CHIPS - 1):
                land = lambda s, z, k=other_land, r=r: refs[k].at[r, pl.ds(s, z)]
                out.append(_copies(lambda s, z, k=other, r=r: refs[k].at[r, pl.ds(s, z)], land, land,
                                   rows, D2D_CHUNKS // 2, send, recv, N_CHIPS * t + 1 + r, sibling))
        return out
    return _Transfer(N_CHIPS * len(pairs), build)


def _comm_call(name, arrays, waits, starts, after):
    keys = list(arrays)
    n, nw, ns = len(keys), len(waits), len(starts)

    def body(*refs):
        in_sems = refs[n:n + 2 * nw]
        base = n + 2 * nw + 1
        out_sems = refs[base:base + 2 * ns]
        bufs = dict(zip(keys, refs[base + 2 * ns:base + 2 * ns + n]))
        token = refs[base + 2 * ns + n]
        for k, (transfer, _, _) in enumerate(waits):
            for _, whole in transfer.build(bufs, in_sems[2 * k], in_sems[2 * k + 1]):
                whole.wait_send()
                whole.wait_recv()
        for k, transfer in enumerate(starts):
            for chunks, _ in transfer.build(bufs, out_sems[2 * k], out_sems[2 * k + 1]):
                for cp in chunks:
                    cp.start()
        token[...] = jnp.zeros_like(token)

    sem_shapes = []
    for transfer in starts:
        sem_shapes += [pltpu.SemaphoreType.DMA((transfer.n_sems,))] * 2
    operands = [_hbm(arrays[k]) for k in keys]
    for _, send, recv in waits:
        operands += [send, recv]
    res = pl.pallas_call(
        body, name=name, in_specs=[HBM] * n + [SEM] * (2 * nw) + [pl.BlockSpec(memory_space=pl.ANY)],
        out_specs=[SEM] * (2 * ns) + [HBM] * n + [pl.BlockSpec(memory_space=pltpu.VMEM)],
        out_shape=sem_shapes + [pltpu.HBM(arrays[k].shape, arrays[k].dtype) for k in keys] + [_token_shape()],
        input_output_aliases={t: 2 * ns + t for t in range(n)},
        compiler_params=pltpu.CompilerParams(has_side_effects=EFFECT),
    )(*operands, after)
    sems = [(res[2 * k], res[2 * k + 1]) for k in range(ns)]
    return dict(zip(keys, res[2 * ns:2 * ns + n])), sems, res[2 * ns + n]


def _device_gather(part_key, all_key, rows):
    def build(refs, send, recv):
        x, y, c, _ = _place()
        me = 4 * x + 2 * y + c
        out = []
        for r in range(1, N_DEV):
            peer = (x ^ (r >> 2), y ^ ((r >> 1) & 1), c ^ (r & 1))
            theirs = 4 * peer[0] + 2 * peer[1] + peer[2]
            out.append(_copies(lambda s, z: refs[part_key].at[pl.ds(s, z)],
                               lambda s, z: refs[all_key].at[me, pl.ds(s, z)],
                               lambda s, z, theirs=theirs: refs[all_key].at[theirs, pl.ds(s, z)],
                               rows, 1, send, recv, r - 1, peer))
        return out
    return _Transfer(N_DEV - 1, build)


def _sum_devices(parts, name):
    _, rows, width = parts.shape
    br = _row_block(rows, width, budget=1 << 19)

    def body(p_ref, o_ref):
        acc = p_ref[0]
        for dev in range(1, N_DEV):
            acc = acc + p_ref[dev]
        o_ref[...] = acc

    return pl.pallas_call(body, name=name, grid=(rows // br,),
                          in_specs=[pl.BlockSpec((N_DEV, br, width), lambda i: (0, i, 0))],
                          out_specs=pl.BlockSpec((br, width), lambda i: (i, 0)),
                          out_shape=jax.ShapeDtypeStruct((rows, width), F32),
                          compiler_params=_params("parallel"))(parts)


INPUT_NAMES = None


def _weight_names():
    names = []
    for i, kind in enumerate(("gmlp", "swa", "fox", "gmlp")):
        p = f"l{i}_"
        names += [p + "ffn1_norm", p + "ffn1_wi", p + "ffn1_wo", p + "mix_norm", p + "mix_win"]
        if kind == "gmlp":
            names += [p + "gmlp_vnorm", p + "gmlp_ws", p + "gmlp_bs"]
        elif kind == "swa":
            names += [p + "swa_sinks"]
        else:
            names += [p + "fox_bf"]
        names += [p + "mix_wout", p + "ffn2_norm", p + "ffn2_wi", p + "ffn2_wo"]
    return names + ["final_norm"]


WEIGHTS = _weight_names()
MIXERS = ("gmlp", "swa", "fox", "gmlp")
BIG = ("ffn1_wi", "ffn1_wo", "mix_win", "mix_wout", "ffn2_wi", "ffn2_wo")


def _ffn_fwd(h, gain, wi, wo, tag, dep=None):
    n = _rms_fwd(h, gain, tag + "_norm", dep=dep)
    z = _matmul(n, wi, name=tag + "_up", out_dtype=BF16)
    a = _swiglu_fwd(z, tag + "_act")
    f, d = wo.shape[0] * wo.shape[1], wo.shape[2]
    out = _matmul(a, wo.reshape(f, d), name=tag + "_down", out_dtype=F32, scale=0.5, resid=h)
    return out, (h, n, z, a)


def _ffn_bwd(dout, saved, gain, wi, wo, tag, dep=None, grads_ready=None):
    h, n, z, a = saved
    f, d = wo.shape[0] * wo.shape[1], wo.shape[2]
    da = _matmul(dout, wo.reshape(f, d), tb=True, name=tag + "_bdown", out_dtype=BF16, scale=0.5, dep=dep)
    dwo = _matmul(a, dout, ta=True, name=tag + "_gdown", out_dtype=BF16, scale=0.5, dep=dep).reshape(wo.shape)
    dz = _swiglu_bwd(z, da, tag + "_bact")
    dwi = _matmul(n, dz, ta=True, name=tag + "_gup", out_dtype=BF16, out_shards=N_CHIPS)
    dn = _matmul(dz, wi, tb=True, name=tag + "_bup", out_dtype=BF16,
                 dep=grads_ready(dwi, dwo) if grads_ready is not None else None)
    dh, dgain = _norm_bwd(h, gain, dn, dout, tag + "_bnorm")
    return dh, dgain, dwi, dwo


def _natural(w_sharded, pad_to):
    ns, rows, csh = w_sharded.shape
    nat = jnp.transpose(w_sharded, (1, 0, 2)).reshape(rows, ns * csh)
    extra = (-nat.shape[1]) % pad_to
    return jnp.pad(nat, ((0, 0), (0, extra))) if extra else nat


def _mixer_fwd(kind, h, p, tag, dep=None):
    s_len, d = h.shape
    n = _rms_fwd(h, p["mix_norm"], tag + "_norm", dep=dep)
    wout = p["mix_wout"].reshape(d, d)
    if kind == "gmlp":
        zp = _matmul(n, p["mix_win"], name=tag + "_in", out_dtype=BF16)
        y = _gmlp_fwd(zp, p["gmlp_vnorm"], p["gmlp_ws"], p["gmlp_bs"], tag + "_gate")
        saved = (h, n, zp, y)
    elif kind == "swa":
        qkv = _matmul(n, p["mix_win"], name=tag + "_in", out_dtype=F32)
        q, k, v = _rope_fwd(qkv, tag + "_rope")
        y = _swa_fwd(q, k, v, p["swa_sinks"], tag + "_attn")
        saved = (h, n, q, k, v, y)
    else:
        heads = d // FOX_HEAD_DIM
        win = _natural(p["mix_win"], LANES)
        proj = _matmul(n, win, name=tag + "_in", out_dtype=F32)
        bf_row = jnp.pad(p["fox_bf"], (0, LANES - heads)).reshape(1, LANES)
        dec = _fox_decay(proj, bf_row, 3 * heads, tag + "_decay")
        dec_t = dec[:, :heads].T
        decq, deck = dec_t.reshape(heads, s_len, 1), dec_t.reshape(heads, 1, s_len)
        y = _fox_fwd(proj, decq, deck, heads, tag + "_attn")
        saved = (h, n, win, proj, bf_row, decq, deck, y)
    out = _matmul(y, wout, name=tag + "_out", out_dtype=F32, resid=h)
    return out, saved


def _mixer_bwd(kind, dout, saved, p, tag, dep=None):
    h, n = saved[0], saved[1]
    y = saved[-1]
    s_len, d = h.shape
    wout = p["mix_wout"].reshape(d, d)
    grads = {}
    dy = _matmul(dout, wout, tb=True, name=tag + "_bout", out_dtype=BF16, dep=dep)
    grads["mix_wout"] = _matmul(y, dout, ta=True, name=tag + "_gout", out_dtype=BF16,
                                dep=dep).reshape(p["mix_wout"].shape)
    if kind == "gmlp":
        zp = saved[2]
        dzp, dws, dbst, dvg = _gmlp_bwd(zp, dy, p["gmlp_vnorm"], p["gmlp_ws"], p["gmlp_bs"], tag + "_bgate")
        grads.update(gmlp_ws=dws, gmlp_bs=dbst.T, gmlp_vnorm=dvg.reshape(d))
        dn = _matmul(dzp, p["mix_win"], tb=True, name=tag + "_bin", out_dtype=BF16)
        grads["mix_win"] = _matmul(n, dzp, ta=True, name=tag + "_gin", out_dtype=BF16, out_shards=N_CHIPS)
    elif kind == "swa":
        q, k, v = saved[2:5]
        dq, dk, dv, dsinks = _swa_bwd(q, k, v, p["swa_sinks"], dy, tag + "_battn")
        grads["swa_sinks"] = dsinks[0, :p["swa_sinks"].shape[0]]
        dqkv = _rope_bwd(dq, dk, dv, tag + "_brope")
        dn = _matmul(dqkv, p["mix_win"], tb=True, name=tag + "_bin", out_dtype=BF16)
        grads["mix_win"] = _matmul(n, dqkv, ta=True, name=tag + "_gin", out_dtype=BF16, out_shards=N_CHIPS)
    else:
        win, proj, bf_row, decq, deck = saved[2:7]
        heads = d // FOX_HEAD_DIM
        dq, dk, dv, ddq, ddk = _fox_bwd(proj, decq, deck, dy, heads, tag + "_battn")
        widen = lambda t: jnp.pad(t.reshape(heads, s_len).T, ((0, 0), (0, LANES - heads)))
        dfl, dbf = _fox_decay_bwd(widen(ddq), widen(ddk), proj, bf_row, 3 * heads, heads, tag + "_bdecay")
        grads["fox_bf"] = dbf[0, :heads]
        dproj = jnp.concatenate([dq, dk.astype(BF16), dv.astype(BF16), dfl], axis=1)
        dn = _matmul(dproj, win, tb=True, name=tag + "_bin", out_dtype=BF16)
        dwin = _matmul(n, dproj, ta=True, name=tag + "_gin", out_dtype=BF16)
        ns, rows, csh = p["mix_win"].shape
        grads["mix_win"] = jnp.transpose(dwin[:, :ns * csh].reshape(rows, ns, csh), (1, 0, 2))
    dh, dgain = _norm_bwd(h, p["mix_norm"], dn, dout, tag + "_bnorm")
    grads["mix_norm"] = dgain.reshape(d)
    return dh, grads


def _pack_small(arrays):
    flat = jnp.concatenate([a.reshape(-1).astype(F32) for a in arrays])
    pad = (-flat.shape[0]) % (512 * LANES)
    return jnp.pad(flat, (0, pad)).reshape(-1, LANES)


def _unpack_small(packed, like):
    flat, out, pos = packed.reshape(-1), [], 0
    for a in like:
        out.append(flat[pos:pos + a.size].reshape(a.shape))
        pos += a.size
    return out


def _step(inp):
    x, target = inp["x"][0], inp["loss_target"][0]
    d = x.shape[1]

    core = lax.axis_index("c").astype(jnp.int32).reshape(1)
    chip = (2 * lax.axis_index("x") + lax.axis_index("y")).astype(jnp.int32).reshape(1)
    place = jnp.concatenate([core, chip])

    groups = []
    for i in range(len(MIXERS)):
        groups += [(i, "ffn1", [f"l{i}_ffn1_wi", f"l{i}_ffn1_wo"]), (i, "mix", [f"l{i}_mix_win", f"l{i}_mix_wout"]),
                   (i, "ffn2", [f"l{i}_ffn2_wi", f"l{i}_ffn2_wo"])]

    def layer_params(i, full):
        p = {nm[len(f"l{i}_"):]: inp[nm] for nm in WEIGHTS if nm.startswith(f"l{i}_")}
        p.update({nm[len(f"l{i}_"):]: w for nm, w in full.items()})
        return p

    n_groups = len(groups)
    valid = lambda k: 0 <= k < n_groups

    bufs = {}
    full_shapes = lambda names: [(N_CHIPS, *inp[nm].shape) for nm in names]
    direct = [_gather_direct(names, full_shapes(names)) for _, _, names in groups]
    relay = [_gather_relay(names, full_shapes(names)) for _, _, names in groups]
    to_pair = [_gather_pair(names, full_shapes(names)) for _, _, names in groups]
    sems = {}

    def cast_groups(which, dep):
        for g in which:
            for nm in groups[g][2]:
                bufs[nm] = _cast_into_slot(inp[nm], chip, nm + "_cast", dep=dep)

    def gather_step(step, after):
        waits, starts, tags, keys = [], [], [], []
        for kind, transfers, g, begin in (("pair", to_pair, step, False), ("relay", relay, step + 1, False),
                                          ("pair", to_pair, step + 1, True), ("direct", direct, step + 2, False),
                                          ("relay", relay, step + 2, True), ("direct", direct, step + 3, True)):
            if not valid(g):
                continue
            keys += [nm for nm in groups[g][2] if nm not in keys]
            if begin:
                starts.append(transfers[g])
                tags.append((kind, g))
            else:
                waits.append((transfers[g], *sems.pop((kind, g))))
        new, started, token = _comm_call(f"gather_step{step + 3}", {k: bufs[k] for k in keys}, waits, starts, after)
        bufs.update(new)
        sems.update(zip(tags, started))
        return token

    cast_groups([0], None)
    token = gather_step(-3, x)
    cast_groups(range(1, 6), token)
    token = gather_step(-2, bufs[groups[5][2][-1]])
    cast_groups(range(6, n_groups), token)
    token = gather_step(-1, bufs[groups[-1][2][-1]])
    h, saved, fulls = x, [], []
    for g, (i, part, names) in enumerate(groups):
        token = gather_step(g, h)
        full = {nm: bufs[nm] for nm in names}
        p = layer_params(i, full)
        if part == "mix":
            h, s = _mixer_fwd(MIXERS[i], h, p, f"l{i}_mix", dep=token)
        else:
            h, s = _ffn_fwd(h, p[part + "_norm"], p[part + "_wi"], p[part + "_wo"], f"l{i}_{part}", dep=token)
        saved.append(s)
        fulls.append(full)
    loss_part, dh, dfinal = _loss_head(h, inp["final_norm"], target, "loss_head")
    loss = lax.psum(loss_part, ("x", "y", "c"))

    small_grads = {"final_norm": dfinal.reshape(d)}
    outs = {}
    work = {}
    stage = {}
    small_names = [nm for nm in WEIGHTS if nm.split("_", 1)[1] not in BIG]
    last_small = "l0_ffn1_norm"
    small_sets = {}

    def small_start(tag, names, after):
        part = _pack_small([small_grads[nm] for nm in names])
        device = 4 * lax.axis_index("x") + 2 * lax.axis_index("y") + lax.axis_index("c")
        work[tag + "#part"] = part
        work[tag + "#all"] = lax.dynamic_update_slice(jnp.zeros((N_DEV, *part.shape), F32), part[None],
                                                      (device, 0, 0))
        stage[(tag, 0)] = _device_gather(tag + "#part", tag + "#all", part.shape[0])
        stage_keys[(tag, 0)] = [tag + "#part", tag + "#all"]
        small_sets[tag] = names
        return comm(f"small_{tag}_start", [], [(tag, 0)], after)

    def small_finish(tag, after):
        names = small_sets[tag]
        comm(f"small_{tag}_wait", [(tag, 0)], [], after)
        total = _sum_devices(work[tag + "#all"], f"small_{tag}_sum")
        like = [inp[nm] for nm in names]
        upd = _adamw(_pack_small(like), total, _pack_small([inp["m_" + nm] for nm in names]),
                     _pack_small([inp["v_" + nm] for nm in names]), f"small_{tag}_adamw")
        unpacked = [_unpack_small(t, like) for t in (total, *upd)]
        for k, nm in enumerate(names):
            outs[nm] = tuple(u[k] for u in unpacked)
        return upd[0]

    def comm(name, transfers_to_wait, transfers_to_start, after):
        waits = [(stage[k], *sems.pop(k)) for k in transfers_to_wait if valid(k[1])]
        starts = [k for k in transfers_to_start if valid(k[1])]
        if not waits and not starts:
            return after
        keys = []
        for k in [k for k in transfers_to_wait if valid(k[1])] + starts:
            keys += [key for key in stage_keys[k] if key not in keys]
        new, started, token = _comm_call(name, {k: work[k] for k in keys}, waits, [stage[k] for k in starts], after)
        work.update(new)
        sems.update(zip(starts, started))
        return token

    stage_keys = {}

    def reduce_step(g, after):
        token = comm(f"rs_pair_step{n_groups - 1 - g}", [("pair", g + 1)], [("pair", g)], after)
        if valid(g + 1):
            names = groups[g + 1][2]
            for nm in names:
                work[nm + "#sum"] = _add_pair(work[nm + "#grad"], work[nm + "#got"], core, nm + "_rs_add")
                work[nm + "#others"] = lax.empty((N_CHIPS - 1, *work[nm + "#sum"].shape[1:]), BF16)
            shapes = [work[nm + "#sum"].shape for nm in names]
            stage[("chips", g + 1)] = _grad_chips([nm + "#sum" for nm in names], [nm + "#others" for nm in names], shapes)
            stage_keys[("chips", g + 1)] = [nm + sfx for nm in names for sfx in ("#sum", "#others")]
        token = comm(f"rs_chips_step{n_groups - 1 - g}", [("chips", g + 2)], [("chips", g + 1)], token)
        if valid(g + 2):
            names = groups[g + 2][2]
            for nm in names:
                work[nm + "#sum_t"] = lax.empty(work[nm + "#sum"].shape[1:], BF16)
                work[nm + "#others_t"] = lax.empty(work[nm + "#others"].shape, BF16)
            sfxs = ("#sum", "#others", "#sum_t", "#others_t")
            stage[("join", g + 2)] = _grad_join(*[[nm + sfx for nm in names] for sfx in sfxs],
                                                [work[nm + "#sum"].shape for nm in names])
            stage_keys[("join", g + 2)] = [nm + sfx for nm in names for sfx in sfxs]
        token = comm(f"rs_join_step{n_groups - 1 - g}", [("join", g + 3)], [("join", g + 2)], token)
        if valid(g + 3):
            for nm in groups[g + 3][2]:
                outs[nm] = tuple(_adamw_halves(inp[nm], work[nm + "#sum"], work[nm + "#others"], work[nm + "#sum_t"],
                                               work[nm + "#others_t"], inp["m_" + nm], inp["v_" + nm], place,
                                               nm + "_adamw"))
        return token

    def reduce_from(g, names, grads, after):
        for nm, gr in zip(names, grads):
            work[nm + "#grad"] = gr
            work[nm + "#got"] = lax.empty((gr.shape[0], gr.shape[1] // 2, gr.shape[2]), BF16)
        stage[("pair", g)] = _grad_pair([nm + "#grad" for nm in names], [nm + "#got" for nm in names],
                                        [gr.shape for gr in grads])
        stage_keys[("pair", g)] = [nm + sfx for nm in names for sfx in ("#grad", "#got")]
        return reduce_step(g, after)

    dep = None
    for g in reversed(range(n_groups)):
        i, part, names = groups[g]
        p = layer_params(i, fulls[g])
        if part == "mix":
            dh, mg = _mixer_bwd(MIXERS[i], dh, saved[g], p, f"l{i}_mix", dep=dep)
            grads = [mg.pop("mix_win"), mg.pop("mix_wout")]
            small_grads.update({f"l{i}_{key}": val for key, val in mg.items()})
            dep = reduce_from(g, names, grads, dh)
        elif g > 0:
            dh, g_norm, dwi, dwo = _ffn_bwd(dh, saved[g], p[part + "_norm"], p[part + "_wi"], p[part + "_wo"],
                                            f"l{i}_{part}", dep=dep)
            small_grads[f"l{i}_{part}_norm"] = g_norm.reshape(d)
            dep = reduce_from(g, names, [dwi, dwo], dh)
        else:
            early = lambda dwi, dwo, names=names, dep=dep: reduce_step(-1, reduce_from(0, names, [dwi, dwo], dep))
            dh, g_norm, _, _ = _ffn_bwd(dh, saved[g], p[part + "_norm"], p[part + "_wi"], p[part + "_wo"],
                                        f"l{i}_{part}", dep=dep, grads_ready=early)
            small_grads[f"l{i}_{part}_norm"] = g_norm.reshape(d)
        if g == 1:
            dep = small_start("early", [nm for nm in small_names if nm != last_small], dep)
    dep = small_start("late", [last_small], dh)
    dep = small_finish("early", dep)
    dep = reduce_step(-2, dep)
    dep = small_finish("late", dep)
    reduce_step(-3, dep)

    result = [loss, dh[None]]
    for part in range(4):
        result += [outs[nm][part] for nm in WEIGHTS]
    return tuple(result)


def kernel(x, l0_ffn1_norm, l0_ffn1_wi, l0_ffn1_wo, l0_mix_norm, l0_mix_win, l0_gmlp_vnorm, l0_gmlp_ws, l0_gmlp_bs, l0_mix_wout, l0_ffn2_norm, l0_ffn2_wi, l0_ffn2_wo, l1_ffn1_norm, l1_ffn1_wi, l1_ffn1_wo, l1_mix_norm, l1_mix_win, l1_swa_sinks, l1_mix_wout, l1_ffn2_norm, l1_ffn2_wi, l1_ffn2_wo, l2_ffn1_norm, l2_ffn1_wi, l2_ffn1_wo, l2_mix_norm, l2_mix_win, l2_fox_bf, l2_mix_wout, l2_ffn2_norm, l2_ffn2_wi, l2_ffn2_wo, l3_ffn1_norm, l3_ffn1_wi, l3_ffn1_wo, l3_mix_norm, l3_mix_win, l3_gmlp_vnorm, l3_gmlp_ws, l3_gmlp_bs, l3_mix_wout, l3_ffn2_norm, l3_ffn2_wi, l3_ffn2_wo, final_norm, loss_target, m_l0_ffn1_norm, m_l0_ffn1_wi, m_l0_ffn1_wo, m_l0_mix_norm, m_l0_mix_win, m_l0_gmlp_vnorm, m_l0_gmlp_ws, m_l0_gmlp_bs, m_l0_mix_wout, m_l0_ffn2_norm, m_l0_ffn2_wi, m_l0_ffn2_wo, m_l1_ffn1_norm, m_l1_ffn1_wi, m_l1_ffn1_wo, m_l1_mix_norm, m_l1_mix_win, m_l1_swa_sinks, m_l1_mix_wout, m_l1_ffn2_norm, m_l1_ffn2_wi, m_l1_ffn2_wo, m_l2_ffn1_norm, m_l2_ffn1_wi, m_l2_ffn1_wo, m_l2_mix_norm, m_l2_mix_win, m_l2_fox_bf, m_l2_mix_wout, m_l2_ffn2_norm, m_l2_ffn2_wi, m_l2_ffn2_wo, m_l3_ffn1_norm, m_l3_ffn1_wi, m_l3_ffn1_wo, m_l3_mix_norm, m_l3_mix_win, m_l3_gmlp_vnorm, m_l3_gmlp_ws, m_l3_gmlp_bs, m_l3_mix_wout, m_l3_ffn2_norm, m_l3_ffn2_wi, m_l3_ffn2_wo, m_final_norm, v_l0_ffn1_norm, v_l0_ffn1_wi, v_l0_ffn1_wo, v_l0_mix_norm, v_l0_mix_win, v_l0_gmlp_vnorm, v_l0_gmlp_ws, v_l0_gmlp_bs, v_l0_mix_wout, v_l0_ffn2_norm, v_l0_ffn2_wi, v_l0_ffn2_wo, v_l1_ffn1_norm, v_l1_ffn1_wi, v_l1_ffn1_wo, v_l1_mix_norm, v_l1_mix_win, v_l1_swa_sinks, v_l1_mix_wout, v_l1_ffn2_norm, v_l1_ffn2_wi, v_l1_ffn2_wo, v_l2_ffn1_norm, v_l2_ffn1_wi, v_l2_ffn1_wo, v_l2_mix_norm, v_l2_mix_win, v_l2_fox_bf, v_l2_mix_wout, v_l2_ffn2_norm, v_l2_ffn2_wi, v_l2_ffn2_wo, v_l3_ffn1_norm, v_l3_ffn1_wi, v_l3_ffn1_wo, v_l3_mix_norm, v_l3_mix_win, v_l3_gmlp_vnorm, v_l3_gmlp_ws, v_l3_gmlp_bs, v_l3_mix_wout, v_l3_ffn2_norm, v_l3_ffn2_wi, v_l3_ffn2_wo, v_final_norm):
    return _step(dict(locals()))
```
